```python
import math
import jax
import jax.numpy as jnp
from jax import lax
import numpy as np

D_MODEL = 1024
BATCH = 8
SEQ = 8192
DEPTH = 2

N_A_LAYERS = DEPTH // 2
N_B_LAYERS = DEPTH - N_A_LAYERS
EPS = 1e-6
NEG_INF = -1e30

GDN_QK_HEADS = 8
GDN_V_HEADS = 16
GDN_HEAD_DIM = 128
GDN_QK_DIM = GDN_QK_HEADS * GDN_HEAD_DIM
GDN_V_DIM = GDN_V_HEADS * GDN_HEAD_DIM
GDN_CONV_DIM = 2 * GDN_QK_DIM + GDN_V_DIM
GDN_IN_DIM = GDN_CONV_DIM + GDN_V_DIM + 2 * GDN_V_HEADS
GDN_CONV_WIDTH = 4
GDN_CHUNK = 64

SWA_Q_HEADS = 16
SWA_KV_HEADS = 4
SWA_GROUP = SWA_Q_HEADS // SWA_KV_HEADS
SWA_HEAD_DIM = 64
SWA_WINDOW = 128
SWA_BLOCK = 128

REL_BUCKETS = 32
REL_MAX_DISTANCE = 128

D_FF = 2816
FFN_CONV_WIDTH = 3

kernel_name = 'hybrid_gdn_swa_yoco'


def rmsnorm(x, w):
    xf = x.astype(jnp.float32)
    y = xf * lax.rsqrt(jnp.mean(xf * xf, axis=-1, keepdims=True) + EPS)
    return (y * w.astype(jnp.float32)).astype(x.dtype)


def l2norm(x):
    return x * lax.rsqrt(jnp.sum(x * x, axis=-1, keepdims=True) + EPS)


def causal_dwconv(x, w, b=None):
    width = w.shape[0]
    t = x.shape[1]
    xp = jnp.pad(x, ((0, 0), (width - 1, 0), (0, 0)))
    y = sum(xp[:, j:j + t] * w[j] for j in range(width))
    return y if b is None else y + b


def chunked_gated_delta_rule(q, k, v, g, beta):
    bsz, t, h, dk = q.shape
    dv = v.shape[-1]
    c = GDN_CHUNK
    n = t // c
    f32 = jnp.float32
    q = q.astype(f32).reshape(bsz, n, c, h, dk).transpose(0, 1, 3, 2, 4)
    k = k.astype(f32).reshape(bsz, n, c, h, dk).transpose(0, 1, 3, 2, 4)
    v = v.astype(f32).reshape(bsz, n, c, h, dv).transpose(0, 1, 3, 2, 4)
    g = g.astype(f32).reshape(bsz, n, c, h).transpose(0, 1, 3, 2)
    beta = beta.astype(f32).reshape(bsz, n, c, h).transpose(0, 1, 3, 2)

    gc = jnp.cumsum(g, axis=-1)
    tril = jnp.tril(jnp.ones((c, c), dtype=bool))
    strict = jnp.tril(jnp.ones((c, c), dtype=bool), -1)
    diff = gc[..., :, None] - gc[..., None, :]
    decay = jnp.where(tril, jnp.exp(jnp.where(tril, diff, 0.0)), 0.0)

    k_beta = k * beta[..., None]
    v_beta = v * beta[..., None]
    m = jnp.where(strict, jnp.einsum('bnhcd,bnhsd->bnhcs', k_beta, k) * decay, 0.0)
    eye = jnp.broadcast_to(jnp.eye(c, dtype=f32), m.shape)
    t_mat = lax.linalg.triangular_solve(m + eye, eye, left_side=True, lower=True,
                                        unit_diagonal=True)
    u = jnp.einsum('bnhcs,bnhse->bnhce', t_mat, v_beta)
    w = jnp.einsum('bnhcs,bnhsd->bnhcd', t_mat, k_beta * jnp.exp(gc)[..., None])
    a_intra = jnp.einsum('bnhcd,bnhsd->bnhcs', q, k) * decay
    q_dec = q * jnp.exp(gc)[..., None]
    k_dec = k * jnp.exp(gc[..., -1:] - gc)[..., None]
    g_last = jnp.exp(gc[..., -1])

    def step(state, inp):
        w_i, u_i, qd_i, kd_i, a_i, gl_i = inp
        v_new = u_i - jnp.einsum('bhcd,bhde->bhce', w_i, state)
        o_i = (jnp.einsum('bhcd,bhde->bhce', qd_i, state)
               + jnp.einsum('bhcs,bhse->bhce', a_i, v_new))
        state = state * gl_i[..., None, None] + jnp.einsum('bhcd,bhce->bhde', kd_i, v_new)
        return state, o_i

    xs = tuple(jnp.moveaxis(z, 1, 0) for z in (w, u, q_dec, k_dec, a_intra, g_last))
    s0 = jnp.zeros((bsz, h, dk, dv), f32)
    _, o = lax.scan(step, s0, xs)
    return o.transpose(1, 0, 3, 2, 4).reshape(bsz, t, h, dv)


def gated_deltanet(hn, w_in, conv_w, a_log, dt_bias, out_norm_w, w_out):
    bsz, t, _ = hn.shape
    proj = hn @ w_in
    qkv, z, b, a = jnp.split(
        proj, [GDN_CONV_DIM, GDN_CONV_DIM + GDN_V_DIM, GDN_CONV_DIM + GDN_V_DIM + GDN_V_HEADS],
        axis=-1)
    qkv = jax.nn.silu(causal_dwconv(qkv, conv_w))
    q, k, v = jnp.split(qkv, [GDN_QK_DIM, 2 * GDN_QK_DIM], axis=-1)
    q = l2norm(q.reshape(bsz, t, GDN_QK_HEADS, GDN_HEAD_DIM)) * (GDN_HEAD_DIM ** -0.5)
    k = l2norm(k.reshape(bsz, t, GDN_QK_HEADS, GDN_HEAD_DIM))
    v = v.reshape(bsz, t, GDN_V_HEADS, GDN_HEAD_DIM)
    rep = GDN_V_HEADS // GDN_QK_HEADS
    q = jnp.repeat(q, rep, axis=2)
    k = jnp.repeat(k, rep, axis=2)
    beta = jax.nn.sigmoid(b.astype(jnp.float32))
    g = -jnp.exp(a_log.astype(jnp.float32)) * jax.nn.softplus(
        a.astype(jnp.float32) + dt_bias.astype(jnp.float32))
    o = chunked_gated_delta_rule(q, k, v, g, beta)
    z = z.reshape(bsz, t, GDN_V_HEADS, GDN_HEAD_DIM).astype(jnp.float32)
    o = rmsnorm(o, out_norm_w) * jax.nn.silu(z)
    return o.reshape(bsz, t, GDN_V_DIM).astype(hn.dtype) @ w_out


def shared_kv(h, kv_norm_w, w_kv):
    bsz, t, _ = h.shape
    kv = rmsnorm(h, kv_norm_w) @ w_kv
    k, v = jnp.split(kv, 2, axis=-1)
    shape = (bsz, t, SWA_KV_HEADS, SWA_HEAD_DIM)
    return k.reshape(shape), v.reshape(shape)


def t5_bucket(dist):
    n = jnp.maximum(dist, 0)
    max_exact = REL_BUCKETS // 2
    nf = jnp.maximum(n, 1).astype(jnp.float32)
    large = max_exact + (jnp.log(nf / max_exact) / math.log(REL_MAX_DISTANCE / max_exact)
                         * (REL_BUCKETS - max_exact)).astype(jnp.int32)
    large = jnp.minimum(large, REL_BUCKETS - 1)
    return jnp.where(n < max_exact, n, large)


def band_bias_and_mask(rel_table, nb):
    qi = jnp.arange(SWA_BLOCK)[:, None]
    ki = jnp.arange(2 * SWA_BLOCK)[None, :]
    dist = qi + SWA_BLOCK - ki
    bias = rel_table.astype(jnp.float32)[t5_bucket(dist)]
    bias = jnp.transpose(bias, (2, 0, 1)).reshape(
        SWA_KV_HEADS, SWA_GROUP, SWA_BLOCK, 2 * SWA_BLOCK)
    in_window = (dist >= 0) & (dist < SWA_WINDOW)
    key_pos = jnp.arange(nb)[:, None, None] * SWA_BLOCK + ki[None] - SWA_BLOCK
    mask = in_window[None] & (key_pos >= 0)
    return bias, mask


def sliding_window_sink_attention(hn, k, v, w_q, sinks, w_o, bias, mask):
    bsz, t, _ = hn.shape
    nb = t // SWA_BLOCK
    q = (hn @ w_q).reshape(bsz, nb, SWA_BLOCK, SWA_KV_HEADS, SWA_GROUP, SWA_HEAD_DIM)
    q = q * (SWA_HEAD_DIM ** -0.5)

    def band(z):
        prev = jnp.concatenate([jnp.zeros_like(z[:, :SWA_BLOCK]), z[:, :-SWA_BLOCK]], axis=1)
        shape = (bsz, nb, SWA_BLOCK, SWA_KV_HEADS, SWA_HEAD_DIM)
        return jnp.concatenate([prev.reshape(shape), z.reshape(shape)], axis=2)

    kb, vb = band(k), band(v)
    s = jnp.einsum('bnqhgd,bnkhd->bnhgqk', q, kb).astype(jnp.float32) + bias
    s = jnp.where(mask[None, :, None, None], s, NEG_INF)
    sink = sinks.astype(jnp.float32).reshape(SWA_KV_HEADS, SWA_GROUP)[:, :, None]
    m = jnp.maximum(jnp.max(s, axis=-1), sink)
    p = jnp.exp(s - m[..., None])
    denom = jnp.sum(p, axis=-1) + jnp.exp(sink - m)
    probs = (p / denom[..., None]).astype(vb.dtype)
    o = jnp.einsum('bnhgqk,bnkhd->bnqhgd', probs, vb)
    return o.reshape(bsz, t, SWA_Q_HEADS * SWA_HEAD_DIM) @ w_o


def conv_gated_mlp(hn, w_up, conv_w, conv_b, w_down):
    u = causal_dwconv(hn @ w_up, conv_w, conv_b)
    gate, val = jnp.split(u, 2, axis=-1)
    return (jax.nn.silu(gate) * val) @ w_down


def _fwd_setup_inputs(seed: int = 0) -> dict:
    key = jax.random.key(seed)
    ks = jax.random.split(key, 24)
    f32 = jnp.float32

    def nrm(k, shape, scale):
        return jax.random.normal(k, shape, f32) * scale

    def gain(k, shape):
        return 1.0 + 0.02 * jax.random.normal(k, shape, f32)

    dt = jnp.exp(jax.random.uniform(ks[5], (N_A_LAYERS, GDN_V_HEADS), f32,
                                    minval=math.log(1e-3), maxval=math.log(1e-1)))
    return {
        'x': nrm(ks[0], (BATCH, SEQ, D_MODEL), 1.0),
        'a_norm_w': gain(ks[1], (N_A_LAYERS, D_MODEL)),
        'a_w_in': nrm(ks[2], (N_A_LAYERS, D_MODEL, GDN_IN_DIM), D_MODEL ** -0.5),
        'a_conv_w': nrm(ks[3], (N_A_LAYERS, GDN_CONV_WIDTH, GDN_CONV_DIM), GDN_CONV_WIDTH ** -0.5),
        'a_a_log': jnp.log(jax.random.uniform(ks[4], (N_A_LAYERS, GDN_V_HEADS), f32,
                                              minval=1.0, maxval=16.0)),
        'a_dt_bias': dt + jnp.log(-jnp.expm1(-dt)),
        'a_out_norm_w': gain(ks[6], (N_A_LAYERS, GDN_HEAD_DIM)),
        'a_w_out': nrm(ks[7], (N_A_LAYERS, GDN_V_DIM, D_MODEL), GDN_V_DIM ** -0.5),
        'kv_norm_w': gain(ks[8], (D_MODEL,)),
        'w_kv': nrm(ks[9], (D_MODEL, 2 * SWA_KV_HEADS * SWA_HEAD_DIM), D_MODEL ** -0.5),
        'b_norm_w': gain(ks[10], (N_B_LAYERS, D_MODEL)),
        'b_w_q': nrm(ks[11], (N_B_LAYERS, D_MODEL, SWA_Q_HEADS * SWA_HEAD_DIM), D_MODEL ** -0.5),
        'b_sinks': nrm(ks[12], (N_B_LAYERS, SWA_Q_HEADS), 0.5),
        'b_w_o': nrm(ks[13], (N_B_LAYERS, SWA_Q_HEADS * SWA_HEAD_DIM, D_MODEL),
                     (SWA_Q_HEADS * SWA_HEAD_DIM) ** -0.5),
        'rel_bias_table': nrm(ks[14], (REL_BUCKETS, SWA_Q_HEADS), 0.5),
        'ffn_norm_w': gain(ks[15], (DEPTH, D_MODEL)),
        'ffn_w_up': nrm(ks[16], (DEPTH, D_MODEL, 2 * D_FF), D_MODEL ** -0.5),
        'ffn_conv_w': nrm(ks[17], (DEPTH, FFN_CONV_WIDTH, 2 * D_FF), FFN_CONV_WIDTH ** -0.5),
        'ffn_conv_b': nrm(ks[18], (DEPTH, 2 * D_FF), 0.02),
        'ffn_w_down': nrm(ks[19], (DEPTH, D_FF, D_MODEL), D_FF ** -0.5),
        'final_norm_w': gain(ks[20], (D_MODEL,)),
    }


def _fwd_reference(x, a_norm_w, a_w_in, a_conv_w, a_a_log, a_dt_bias, a_out_norm_w, a_w_out,
              kv_norm_w, w_kv, b_norm_w, b_w_q, b_sinks, b_w_o, rel_bias_table,
              ffn_norm_w, ffn_w_up, ffn_conv_w, ffn_conv_b, ffn_w_down, final_norm_w):
    nb = x.shape[1] // SWA_BLOCK
    bias, mask = band_bias_and_mask(rel_bias_table, nb)
    h = x
    k_sh = None
    v_sh = None
    for layer in range(DEPTH):
        if layer < N_A_LAYERS:
            i = layer
            h = h + gated_deltanet(rmsnorm(h, a_norm_w[i]), a_w_in[i], a_conv_w[i], a_a_log[i],
                                   a_dt_bias[i], a_out_norm_w[i], a_w_out[i])
        else:
            j = layer - N_A_LAYERS
            if j == 0:
                k_sh, v_sh = shared_kv(h, kv_norm_w, w_kv)
            h = h + sliding_window_sink_attention(rmsnorm(h, b_norm_w[j]), k_sh, v_sh, b_w_q[j],
                                                  b_sinks[j], b_w_o[j], bias, mask)
        h = h + conv_gated_mlp(rmsnorm(h, ffn_norm_w[layer]), ffn_w_up[layer], ffn_conv_w[layer],
                               ffn_conv_b[layer], ffn_w_down[layer])
    return rmsnorm(h, final_norm_w)


import jax as _jax
import jax.numpy as _jnp

TWIN_FORMAT = 'train_step'
FWD_PARAMS = ['x', 'a_norm_w', 'a_w_in', 'a_conv_w', 'a_a_log', 'a_dt_bias', 'a_out_norm_w', 'a_w_out', 'kv_norm_w', 'w_kv', 'b_norm_w', 'b_w_q', 'b_sinks', 'b_w_o', 'rel_bias_table', 'ffn_norm_w', 'ffn_w_up', 'ffn_conv_w', 'ffn_conv_b', 'ffn_w_down', 'final_norm_w']
TWIN_WEIGHTS = ['a_norm_w', 'a_w_in', 'a_conv_w', 'a_a_log', 'a_dt_bias', 'a_out_norm_w', 'a_w_out', 'kv_norm_w', 'w_kv', 'b_norm_w', 'b_w_q', 'b_sinks', 'b_w_o', 'rel_bias_table', 'ffn_norm_w', 'ffn_w_up', 'ffn_conv_w', 'ffn_conv_b', 'ffn_w_down', 'final_norm_w']
TWIN_DIFF_INPUT = 'x'
TWIN_INPUTS = ['x', 'a_norm_w', 'a_w_in', 'a_conv_w', 'a_a_log', 'a_dt_bias', 'a_out_norm_w', 'a_w_out', 'kv_norm_w', 'w_kv', 'b_norm_w', 'b_w_q', 'b_sinks', 'b_w_o', 'rel_bias_table', 'ffn_norm_w', 'ffn_w_up', 'ffn_conv_w', 'ffn_conv_b', 'ffn_w_down', 'final_norm_w', 'loss_target', 'm_a_norm_w', 'm_a_w_in', 'm_a_conv_w', 'm_a_a_log', 'm_a_dt_bias', 'm_a_out_norm_w', 'm_a_w_out', 'm_kv_norm_w', 'm_w_kv', 'm_b_norm_w', 'm_b_w_q', 'm_b_sinks', 'm_b_w_o', 'm_rel_bias_table', 'm_ffn_norm_w', 'm_ffn_w_up', 'm_ffn_conv_w', 'm_ffn_conv_b', 'm_ffn_w_down', 'm_final_norm_w', 'v_a_norm_w', 'v_a_w_in', 'v_a_conv_w', 'v_a_a_log', 'v_a_dt_bias', 'v_a_out_norm_w', 'v_a_w_out', 'v_kv_norm_w', 'v_w_kv', 'v_b_norm_w', 'v_b_w_q', 'v_b_sinks', 'v_b_w_o', 'v_rel_bias_table', 'v_ffn_norm_w', 'v_ffn_w_up', 'v_ffn_conv_w', 'v_ffn_conv_b', 'v_ffn_w_down', 'v_final_norm_w']
TWIN_OUTPUTS = ['loss', 'grad_x', 'grad_a_norm_w', 'grad_a_w_in', 'grad_a_conv_w', 'grad_a_a_log', 'grad_a_dt_bias', 'grad_a_out_norm_w', 'grad_a_w_out', 'grad_kv_norm_w', 'grad_w_kv', 'grad_b_norm_w', 'grad_b_w_q', 'grad_b_sinks', 'grad_b_w_o', 'grad_rel_bias_table', 'grad_ffn_norm_w', 'grad_ffn_w_up', 'grad_ffn_conv_w', 'grad_ffn_conv_b', 'grad_ffn_w_down', 'grad_final_norm_w', 'delta_a_norm_w', 'delta_a_w_in', 'delta_a_conv_w', 'delta_a_a_log', 'delta_a_dt_bias', 'delta_a_out_norm_w', 'delta_a_w_out', 'delta_kv_norm_w', 'delta_w_kv', 'delta_b_norm_w', 'delta_b_w_q', 'delta_b_sinks', 'delta_b_w_o', 'delta_rel_bias_table', 'delta_ffn_norm_w', 'delta_ffn_w_up', 'delta_ffn_conv_w', 'delta_ffn_conv_b', 'delta_ffn_w_down', 'delta_final_norm_w', 'new_m_a_norm_w', 'new_m_a_w_in', 'new_m_a_conv_w', 'new_m_a_a_log', 'new_m_a_dt_bias', 'new_m_a_out_norm_w', 'new_m_a_w_out', 'new_m_kv_norm_w', 'new_m_w_kv', 'new_m_b_norm_w', 'new_m_b_w_q', 'new_m_b_sinks', 'new_m_b_w_o', 'new_m_rel_bias_table', 'new_m_ffn_norm_w', 'new_m_ffn_w_up', 'new_m_ffn_conv_w', 'new_m_ffn_conv_b', 'new_m_ffn_w_down', 'new_m_final_norm_w', 'new_v_a_norm_w', 'new_v_a_w_in', 'new_v_a_conv_w', 'new_v_a_a_log', 'new_v_a_dt_bias', 'new_v_a_out_norm_w', 'new_v_a_w_out', 'new_v_kv_norm_w', 'new_v_w_kv', 'new_v_b_norm_w', 'new_v_b_w_q', 'new_v_b_sinks', 'new_v_b_w_o', 'new_v_rel_bias_table', 'new_v_ffn_norm_w', 'new_v_ffn_w_up', 'new_v_ffn_conv_w', 'new_v_ffn_conv_b', 'new_v_ffn_w_down', 'new_v_final_norm_w']
TWIN_LEAF_KINDS = {'loss': 'loss', 'grad_x': 'grad_x', 'grad_a_norm_w': 'grad_w', 'grad_a_w_in': 'grad_w', 'grad_a_conv_w': 'grad_w', 'grad_a_a_log': 'grad_w', 'grad_a_dt_bias': 'grad_w', 'grad_a_out_norm_w': 'grad_w', 'grad_a_w_out': 'grad_w', 'grad_kv_norm_w': 'grad_w', 'grad_w_kv': 'grad_w', 'grad_b_norm_w': 'grad_w', 'grad_b_w_q': 'grad_w', 'grad_b_sinks': 'grad_w', 'grad_b_w_o': 'grad_w', 'grad_rel_bias_table': 'grad_w', 'grad_ffn_norm_w': 'grad_w', 'grad_ffn_w_up': 'grad_w', 'grad_ffn_conv_w': 'grad_w', 'grad_ffn_conv_b': 'grad_w', 'grad_ffn_w_down': 'grad_w', 'grad_final_norm_w': 'grad_w', 'delta_a_norm_w': 'delta_w', 'delta_a_w_in': 'delta_w', 'delta_a_conv_w': 'delta_w', 'delta_a_a_log': 'delta_w', 'delta_a_dt_bias': 'delta_w', 'delta_a_out_norm_w': 'delta_w', 'delta_a_w_out': 'delta_w', 'delta_kv_norm_w': 'delta_w', 'delta_w_kv': 'delta_w', 'delta_b_norm_w': 'delta_w', 'delta_b_w_q': 'delta_w', 'delta_b_sinks': 'delta_w', 'delta_b_w_o': 'delta_w', 'delta_rel_bias_table': 'delta_w', 'delta_ffn_norm_w': 'delta_w', 'delta_ffn_w_up': 'delta_w', 'delta_ffn_conv_w': 'delta_w', 'delta_ffn_conv_b': 'delta_w', 'delta_ffn_w_down': 'delta_w', 'delta_final_norm_w': 'delta_w', 'new_m_a_norm_w': 'new_m', 'new_m_a_w_in': 'new_m', 'new_m_a_conv_w': 'new_m', 'new_m_a_a_log': 'new_m', 'new_m_a_dt_bias': 'new_m', 'new_m_a_out_norm_w': 'new_m', 'new_m_a_w_out': 'new_m', 'new_m_kv_norm_w': 'new_m', 'new_m_w_kv': 'new_m', 'new_m_b_norm_w': 'new_m', 'new_m_b_w_q': 'new_m', 'new_m_b_sinks': 'new_m', 'new_m_b_w_o': 'new_m', 'new_m_rel_bias_table': 'new_m', 'new_m_ffn_norm_w': 'new_m', 'new_m_ffn_w_up': 'new_m', 'new_m_ffn_conv_w': 'new_m', 'new_m_ffn_conv_b': 'new_m', 'new_m_ffn_w_down': 'new_m', 'new_m_final_norm_w': 'new_m', 'new_v_a_norm_w': 'new_v', 'new_v_a_w_in': 'new_v', 'new_v_a_conv_w': 'new_v', 'new_v_a_a_log': 'new_v', 'new_v_a_dt_bias': 'new_v', 'new_v_a_out_norm_w': 'new_v', 'new_v_a_w_out': 'new_v', 'new_v_kv_norm_w': 'new_v', 'new_v_w_kv': 'new_v', 'new_v_b_norm_w': 'new_v', 'new_v_b_w_q': 'new_v', 'new_v_b_sinks': 'new_v', 'new_v_b_w_o': 'new_v', 'new_v_rel_bias_table': 'new_v', 'new_v_ffn_norm_w': 'new_v', 'new_v_ffn_w_up': 'new_v', 'new_v_ffn_conv_w': 'new_v', 'new_v_ffn_conv_b': 'new_v', 'new_v_ffn_w_down': 'new_v', 'new_v_final_norm_w': 'new_v'}


def _forward(args):
    return _fwd_reference(*[args[k] for k in FWD_PARAMS])


def _output_shape():
    def fwd():
        inp = _fwd_setup_inputs(0)
        return _fwd_reference(*[inp[k] for k in FWD_PARAMS])
    out = _jax.eval_shape(fwd)
    return out.shape, out.dtype

N_MICROBATCH = 1
ADAM_LR = 0.001
ADAM_B1 = 0.9
ADAM_B2 = 0.999
ADAM_EPS = 1e-08
ADAM_WD = 0.01
ADAM_STEP = 10
PER_EXAMPLE_BATCH_AXIS = {'x': 0, 'loss_target': 0}
SHARED_INPUTS = []
_WEIGHT_DTYPES = {'a_norm_w': _jnp.float32, 'a_w_in': _jnp.float32, 'a_conv_w': _jnp.float32, 'a_a_log': _jnp.float32, 'a_dt_bias': _jnp.float32, 'a_out_norm_w': _jnp.float32, 'a_w_out': _jnp.float32, 'kv_norm_w': _jnp.float32, 'w_kv': _jnp.float32, 'b_norm_w': _jnp.float32, 'b_w_q': _jnp.float32, 'b_sinks': _jnp.float32, 'b_w_o': _jnp.float32, 'rel_bias_table': _jnp.float32, 'ffn_norm_w': _jnp.float32, 'ffn_w_up': _jnp.float32, 'ffn_conv_w': _jnp.float32, 'ffn_conv_b': _jnp.float32, 'ffn_w_down': _jnp.float32, 'final_norm_w': _jnp.float32}
MOMENT_SCALE = {'a_norm_w': 2.953167e-01, 'a_w_in': 1.071052e-01, 'a_conv_w': 1.064261e-01, 'a_a_log': 5.282360e-01, 'a_dt_bias': 5.173982e-01, 'a_out_norm_w': 4.156134e-01, 'a_w_out': 1.559202e-01, 'kv_norm_w': 6.685033e-02, 'w_kv': 8.870202e-02, 'b_norm_w': 3.727423e-02, 'b_w_q': 3.881162e-02, 'b_sinks': 3.606919e-02, 'b_w_o': 4.918082e-02, 'rel_bias_table': 5.690535e-02, 'ffn_norm_w': 1.667371e-01, 'ffn_w_up': 7.261355e-02, 'ffn_conv_w': 7.334872e-02, 'ffn_conv_b': 7.174274e-02, 'ffn_w_down': 1.183923e-01, 'final_norm_w': 6.403781e+01}


def _to_microbatches(a, axis):
    t = _jnp.moveaxis(a, axis, 0)
    t = t.reshape((N_MICROBATCH, t.shape[0] // N_MICROBATCH) + t.shape[1:])
    return _jnp.moveaxis(t, 1, axis + 1)


def setup_inputs(seed: int = 0) -> dict:
    inp = _fwd_setup_inputs(seed)
    key = _jax.random.fold_in(_jax.random.key(seed), 7919)
    shape, _ = _output_shape()
    out = dict(inp)
    out["loss_target"] = _jax.random.normal(_jax.random.fold_in(key, 0), shape, _jnp.float32)
    for i, name in enumerate(TWIN_WEIGHTS):
        w = inp[name].astype(_jnp.float32)
        if MOMENT_SCALE is None:
            s = _jnp.sqrt(_jnp.mean(_jnp.square(w)) + 1e-30)
        else:
            s = MOMENT_SCALE[name]
        km, kv = _jax.random.split(_jax.random.fold_in(key, i + 1))
        out[name] = w
        out["m_" + name] = s * _jax.random.normal(km, w.shape, _jnp.float32)
        out["v_" + name] = (s * s) * _jax.random.uniform(kv, w.shape, _jnp.float32, 0.5, 1.5)
    if N_MICROBATCH > 1:
        for name, axis in PER_EXAMPLE_BATCH_AXIS.items():
            out[name] = _to_microbatches(out[name], axis)
    return {'x': out['x'], 'a_norm_w': out['a_norm_w'], 'a_w_in': out['a_w_in'], 'a_conv_w': out['a_conv_w'], 'a_a_log': out['a_a_log'], 'a_dt_bias': out['a_dt_bias'], 'a_out_norm_w': out['a_out_norm_w'], 'a_w_out': out['a_w_out'], 'kv_norm_w': out['kv_norm_w'], 'w_kv': out['w_kv'], 'b_norm_w': out['b_norm_w'], 'b_w_q': out['b_w_q'], 'b_sinks': out['b_sinks'], 'b_w_o': out['b_w_o'], 'rel_bias_table': out['rel_bias_table'], 'ffn_norm_w': out['ffn_norm_w'], 'ffn_w_up': out['ffn_w_up'], 'ffn_conv_w': out['ffn_conv_w'], 'ffn_conv_b': out['ffn_conv_b'], 'ffn_w_down': out['ffn_w_down'], 'final_norm_w': out['final_norm_w'], 'loss_target': out['loss_target'], 'm_a_norm_w': out['m_a_norm_w'], 'm_a_w_in': out['m_a_w_in'], 'm_a_conv_w': out['m_a_conv_w'], 'm_a_a_log': out['m_a_a_log'], 'm_a_dt_bias': out['m_a_dt_bias'], 'm_a_out_norm_w': out['m_a_out_norm_w'], 'm_a_w_out': out['m_a_w_out'], 'm_kv_norm_w': out['m_kv_norm_w'], 'm_w_kv': out['m_w_kv'], 'm_b_norm_w': out['m_b_norm_w'], 'm_b_w_q': out['m_b_w_q'], 'm_b_sinks': out['m_b_sinks'], 'm_b_w_o': out['m_b_w_o'], 'm_rel_bias_table': out['m_rel_bias_table'], 'm_ffn_norm_w': out['m_ffn_norm_w'], 'm_ffn_w_up': out['m_ffn_w_up'], 'm_ffn_conv_w': out['m_ffn_conv_w'], 'm_ffn_conv_b': out['m_ffn_conv_b'], 'm_ffn_w_down': out['m_ffn_w_down'], 'm_final_norm_w': out['m_final_norm_w'], 'v_a_norm_w': out['v_a_norm_w'], 'v_a_w_in': out['v_a_w_in'], 'v_a_conv_w': out['v_a_conv_w'], 'v_a_a_log': out['v_a_a_log'], 'v_a_dt_bias': out['v_a_dt_bias'], 'v_a_out_norm_w': out['v_a_out_norm_w'], 'v_a_w_out': out['v_a_w_out'], 'v_kv_norm_w': out['v_kv_norm_w'], 'v_w_kv': out['v_w_kv'], 'v_b_norm_w': out['v_b_norm_w'], 'v_b_w_q': out['v_b_w_q'], 'v_b_sinks': out['v_b_sinks'], 'v_b_w_o': out['v_b_w_o'], 'v_rel_bias_table': out['v_rel_bias_table'], 'v_ffn_norm_w': out['v_ffn_norm_w'], 'v_ffn_w_up': out['v_ffn_w_up'], 'v_ffn_conv_w': out['v_ffn_conv_w'], 'v_ffn_conv_b': out['v_ffn_conv_b'], 'v_ffn_w_down': out['v_ffn_w_down'], 'v_final_norm_w': out['v_final_norm_w']}


def _loss(weights, diff, rest, loss_target):
    with _jax.named_scope("forward"):
        args = {**rest, TWIN_DIFF_INPUT: diff, **{k: w.astype(_WEIGHT_DTYPES[k]) for k, w in weights.items()}}
        y = _forward(args)
    with _jax.named_scope("loss_head"):
        err = _jnp.square(y.astype(_jnp.float32) - loss_target)
        return 0.5 * _jnp.sum(_jnp.mean(err, axis=-1)) if err.ndim else 0.5 * err


def _adamw(w, g, m, v):
    m = ADAM_B1 * m + (1.0 - ADAM_B1) * g
    v = ADAM_B2 * v + (1.0 - ADAM_B2) * _jnp.square(g)
    m_hat = m / (1.0 - ADAM_B1 ** ADAM_STEP)
    v_hat = v / (1.0 - ADAM_B2 ** ADAM_STEP)
    delta = -ADAM_LR * (m_hat / (_jnp.sqrt(v_hat) + ADAM_EPS) + ADAM_WD * w)
    return delta, m, v


def reference(x, a_norm_w, a_w_in, a_conv_w, a_a_log, a_dt_bias, a_out_norm_w, a_w_out, kv_norm_w, w_kv, b_norm_w, b_w_q, b_sinks, b_w_o, rel_bias_table, ffn_norm_w, ffn_w_up, ffn_conv_w, ffn_conv_b, ffn_w_down, final_norm_w, loss_target, m_a_norm_w, m_a_w_in, m_a_conv_w, m_a_a_log, m_a_dt_bias, m_a_out_norm_w, m_a_w_out, m_kv_norm_w, m_w_kv, m_b_norm_w, m_b_w_q, m_b_sinks, m_b_w_o, m_rel_bias_table, m_ffn_norm_w, m_ffn_w_up, m_ffn_conv_w, m_ffn_conv_b, m_ffn_w_down, m_final_norm_w, v_a_norm_w, v_a_w_in, v_a_conv_w, v_a_a_log, v_a_dt_bias, v_a_out_norm_w, v_a_w_out, v_kv_norm_w, v_w_kv, v_b_norm_w, v_b_w_q, v_b_sinks, v_b_w_o, v_rel_bias_table, v_ffn_norm_w, v_ffn_w_up, v_ffn_conv_w, v_ffn_conv_b, v_ffn_w_down, v_final_norm_w):
    given = dict(x=x, a_norm_w=a_norm_w, a_w_in=a_w_in, a_conv_w=a_conv_w, a_a_log=a_a_log, a_dt_bias=a_dt_bias, a_out_norm_w=a_out_norm_w, a_w_out=a_w_out, kv_norm_w=kv_norm_w, w_kv=w_kv, b_norm_w=b_norm_w, b_w_q=b_w_q, b_sinks=b_sinks, b_w_o=b_w_o, rel_bias_table=rel_bias_table, ffn_norm_w=ffn_norm_w, ffn_w_up=ffn_w_up, ffn_conv_w=ffn_conv_w, ffn_conv_b=ffn_conv_b, ffn_w_down=ffn_w_down, final_norm_w=final_norm_w, loss_target=loss_target, m_a_norm_w=m_a_norm_w, m_a_w_in=m_a_w_in, m_a_conv_w=m_a_conv_w, m_a_a_log=m_a_a_log, m_a_dt_bias=m_a_dt_bias, m_a_out_norm_w=m_a_out_norm_w, m_a_w_out=m_a_w_out, m_kv_norm_w=m_kv_norm_w, m_w_kv=m_w_kv, m_b_norm_w=m_b_norm_w, m_b_w_q=m_b_w_q, m_b_sinks=m_b_sinks, m_b_w_o=m_b_w_o, m_rel_bias_table=m_rel_bias_table, m_ffn_norm_w=m_ffn_norm_w, m_ffn_w_up=m_ffn_w_up, m_ffn_conv_w=m_ffn_conv_w, m_ffn_conv_b=m_ffn_conv_b, m_ffn_w_down=m_ffn_w_down, m_final_norm_w=m_final_norm_w, v_a_norm_w=v_a_norm_w, v_a_w_in=v_a_w_in, v_a_conv_w=v_a_conv_w, v_a_a_log=v_a_a_log, v_a_dt_bias=v_a_dt_bias, v_a_out_norm_w=v_a_out_norm_w, v_a_w_out=v_a_w_out, v_kv_norm_w=v_kv_norm_w, v_w_kv=v_w_kv, v_b_norm_w=v_b_norm_w, v_b_w_q=v_b_w_q, v_b_sinks=v_b_sinks, v_b_w_o=v_b_w_o, v_rel_bias_table=v_rel_bias_table, v_ffn_norm_w=v_ffn_norm_w, v_ffn_w_up=v_ffn_w_up, v_ffn_conv_w=v_ffn_conv_w, v_ffn_conv_b=v_ffn_conv_b, v_ffn_w_down=v_ffn_w_down, v_final_norm_w=v_final_norm_w)
    weights = {n: given[n] for n in TWIN_WEIGHTS}
    shared = {n: given[n] for n in SHARED_INPUTS}
    per_example = {n: given[n] for n in ['x']}
    grad_fn = _jax.value_and_grad(_loss, argnums=(0, 1))

    def one_microbatch(ex, loss_target):
        ex = dict(ex)
        diff = ex.pop(TWIN_DIFF_INPUT)
        return grad_fn(weights, diff, {**shared, **ex}, loss_target)

    if N_MICROBATCH == 1:
        loss, (grad_w, grad_x) = one_microbatch(per_example, given["loss_target"])
    else:
        def body(carry, xs):
            loss_sum, grad_sum = carry
            l_k, (gw_k, gx_k) = one_microbatch(xs[0], xs[1])
            with _jax.named_scope("update"):
                return (loss_sum + l_k, _jax.tree.map(_jnp.add, grad_sum, gw_k)), gx_k

        init = (_jnp.zeros((), _jnp.float32), _jax.tree.map(_jnp.zeros_like, weights))
        (loss, grad_w), grad_x = _jax.lax.scan(body, init, (per_example, given["loss_target"]))
    with _jax.named_scope("update"):
        delta_w, new_m, new_v = {}, {}, {}
        for n in TWIN_WEIGHTS:
            delta_w[n], new_m[n], new_v[n] = _adamw(weights[n], grad_w[n], given["m_" + n], given["v_" + n])
    return (loss, grad_x, *[grad_w[n] for n in TWIN_WEIGHTS], *[delta_w[n] for n in TWIN_WEIGHTS],
            *[new_m[n] for n in TWIN_WEIGHTS], *[new_v[n] for n in TWIN_WEIGHTS])
```

```python
import functools
import math

import jax
import jax.numpy as jnp
from jax import lax
from jax.experimental import pallas as pl
from jax.experimental.pallas import tpu as pltpu

F32 = jnp.float32
BF16 = jnp.bfloat16
MESH = pl.DeviceIdType.MESH
HIGHEST = lax.Precision.HIGHEST

D = 1024
EPS = 1e-6
NEG_INF = -1e30
N_CHIPS = 4

GDN_QK_HEADS = 8
GDN_V_HEADS = 16
GDN_HD = 128
GDN_QK = GDN_QK_HEADS * GDN_HD
GDN_V = GDN_V_HEADS * GDN_HD
GDN_CONV = 2 * GDN_QK + GDN_V
GDN_MAIN = GDN_CONV + GDN_V
GDN_IN = GDN_MAIN + 2 * GDN_V_HEADS
GDN_IN_SHARD = GDN_IN // N_CHIPS
GDN_CHUNK = 64

SWA_Q_HEADS = 16
SWA_KV_HEADS = 4
SWA_GROUP = 4
SWA_HD = 64
SWA_BLOCK = 128
REL_BUCKETS = 32
REL_MAX_DISTANCE = 128

DFF = 2816
DFF2 = 2 * DFF
DFF2_SHARD = DFF2 // N_CHIPS
DFF_SHARD = DFF // N_CHIPS

ADAM_LR = 0.001
ADAM_B1 = 0.9
ADAM_B2 = 0.999
ADAM_EPS = 1e-08
ADAM_WD = 0.01
ADAM_STEP = 10

LANE = 128
SUBLANE = 8
VMEM_LIMIT = 56 * 1024 * 1024


def _params(sem, vmem=VMEM_LIMIT):
    return pltpu.CompilerParams(dimension_semantics=sem, vmem_limit_bytes=vmem)


def _rowcall(name, fn, T, tm, ins, outs):
    n = T // tm
    r8 = tm // SUBLANE
    last8 = T // SUBLANE - 1
    arrays, in_specs = [], []
    for arr, kind, cols in ins:
        arrays.append(arr)
        if kind == "full":
            in_specs.append(pl.BlockSpec(arr.shape, functools.partial(lambda nd, i: (0,) * nd, arr.ndim)))
        elif arr.ndim == 2:
            w, ci = cols if cols is not None else (arr.shape[1], 0)
            if kind == "row":
                in_specs.append(pl.BlockSpec((tm, w), functools.partial(lambda ci, i: (i, ci), ci)))
            elif kind == "prev":
                in_specs.append(pl.BlockSpec(
                    (SUBLANE, w), functools.partial(lambda ci, i: (jnp.maximum(i * r8 - 1, 0), ci), ci)))
            else:
                in_specs.append(pl.BlockSpec(
                    (SUBLANE, w), functools.partial(lambda ci, i: (jnp.minimum((i + 1) * r8, last8), ci), ci)))
        else:
            lead = arr.shape[:-2]
            in_specs.append(pl.BlockSpec(lead + (tm, arr.shape[-1]),
                                         functools.partial(lambda nl, i: (0,) * nl + (i, 0), len(lead))))
    out_shape, out_specs = [], []
    for shape, dtype, kind in outs:
        out_shape.append(jax.ShapeDtypeStruct(shape, dtype))
        if kind == "acc":
            out_specs.append(pl.BlockSpec(shape, functools.partial(lambda nd, i: (0,) * nd, len(shape))))
        else:
            lead = shape[:-2]
            out_specs.append(pl.BlockSpec(lead + (tm, shape[-1]),
                                          functools.partial(lambda nl, i: (0,) * nl + (i, 0), len(lead))))
    nin = len(arrays)

    def body(*refs):
        i = pl.program_id(0)
        vals = [r[...] for r in refs[:nin]]
        res = fn(i, *vals)
        for (shape, dtype, kind), o, r in zip(outs, refs[nin:], res):
            if kind == "row":
                o[...] = r.astype(dtype)
            else:
                @pl.when(i == 0)
                def _():
                    o[...] = r.astype(dtype)

                @pl.when(i > 0)
                def _():
                    o[...] += r.astype(dtype)

    res = pl.pallas_call(
        body, name=name, grid=(n,), in_specs=in_specs, out_specs=out_specs, out_shape=out_shape,
        compiler_params=_params(("arbitrary",)),
    )(*arrays)
    return res


def _mm(name, a, b, out_shape, out_dtype, grid, a_spec, b_spec, o_spec, dims, acc_shape, res=None, precision=None):
    nk = grid[2]

    def body(*refs):
        if res is not None:
            a_ref, b_ref, r_ref, o_ref = refs[:4]
        else:
            a_ref, b_ref, o_ref = refs[:3]
        av, bv = a_ref[...], b_ref[...]
        if precision is None:
            av, bv = av.astype(BF16), bv.astype(BF16)
        p = lax.dot_general(av, bv, (dims, ((), ())), preferred_element_type=F32, precision=precision)

        def finish(x):
            if res is not None:
                x = x + r_ref[...].astype(F32)
            o_ref[...] = x.astype(out_dtype)

        if nk == 1:
            finish(p)
        else:
            acc = refs[-1]
            k = pl.program_id(2)

            @pl.when(k == 0)
            def _():
                acc[...] = p

            @pl.when(k > 0)
            def _():
                acc[...] += p

            @pl.when(k == nk - 1)
            def _():
                finish(acc[...])

    ops = [a, b] + ([res] if res is not None else [])
    specs = [a_spec, b_spec] + ([o_spec] if res is not None else [])
    return pl.pallas_call(
        body, name=name, grid=grid, in_specs=specs, out_specs=o_spec,
        out_shape=jax.ShapeDtypeStruct(out_shape, out_dtype),
        scratch_shapes=[pltpu.VMEM(acc_shape, F32)] if nk > 1 else [],
        compiler_params=_params(("parallel", "parallel", "arbitrary")),
    )(*ops)


NN = ((1,), (0,))
NT = ((1,), (1,))
TN = ((0,), (0,))


def _tile(n, pref):
    for t in pref:
        if n % t == 0:
            return t
    return n


def _mm_nn(name, a, w, out_dtype, res=None, precision=None):
    M, K = a.shape
    N = w.shape[1]
    tm, tn = _tile(M, (512, 256, 128)), _tile(N, (512, 256, 128))
    return _mm(name, a, w, (M, N), out_dtype, (M // tm, N // tn, 1),
               pl.BlockSpec((tm, K), lambda i, j, k: (i, 0)), pl.BlockSpec((K, tn), lambda i, j, k: (0, j)),
               pl.BlockSpec((tm, tn), lambda i, j, k: (i, j)), NN, (tm, tn), res=res, precision=precision)


def _mm_nt(name, g, w, out_dtype, res=None, precision=None):
    M, N = g.shape
    K = w.shape[0]
    tm, tk = _tile(M, (512, 256, 128)), _tile(K, (512, 1408, 256, 128))
    tn = _tile(N, (1536, 1024, 512, 256, 128))
    return _mm(name, g, w, (M, K), out_dtype, (M // tm, K // tk, N // tn),
               pl.BlockSpec((tm, tn), lambda i, j, k: (i, k)), pl.BlockSpec((tk, tn), lambda i, j, k: (j, k)),
               pl.BlockSpec((tm, tk), lambda i, j, k: (i, j)), NT, (tm, tk), res=res, precision=precision)


def _mm_tn(name, a, g, out_dtype=F32, precision=None):
    T, K = a.shape
    N = g.shape[1]
    tk, tn = _tile(K, (512, 1408, 256, 128)), _tile(N, (512, 256, 128))
    tt = _tile(T, (1024, 512, 256, 128))
    return _mm(name, a, g, (K, N), out_dtype, (K // tk, N // tn, T // tt),
               pl.BlockSpec((tt, tk), lambda i, j, k: (k, i)), pl.BlockSpec((tt, tn), lambda i, j, k: (k, j)),
               pl.BlockSpec((tk, tn), lambda i, j, k: (i, j)), TN, (tk, tn), precision=precision)


def _mm_up(name, n, wup, layer):
    T = n.shape[0]
    tm = _tile(T, (512, 256, 128))
    return _mm(name, n, wup, (T, DFF2), BF16, (T // tm, N_CHIPS, 1),
               pl.BlockSpec((tm, D), lambda i, j, k: (i, 0)),
               pl.BlockSpec((None, None, D, DFF2_SHARD), lambda i, j, k: (j, layer, 0, 0)),
               pl.BlockSpec((tm, DFF2_SHARD), lambda i, j, k: (i, j)), NN, (tm, DFF2_SHARD))


def _mm_up_nt(name, du, wup, layer):
    T = du.shape[0]
    tm, tk = _tile(T, (512, 256, 128)), 512
    return _mm(name, du, wup, (T, D), F32, (T // tm, D // tk, N_CHIPS),
               pl.BlockSpec((tm, DFF2_SHARD), lambda i, j, k: (i, k)),
               pl.BlockSpec((None, None, tk, DFF2_SHARD), lambda i, j, k: (k, layer, j, 0)),
               pl.BlockSpec((tm, tk), lambda i, j, k: (i, j)), NT, (tm, tk))


def _mm_up_tn(name, n, du):
    T = n.shape[0]
    tk, tt = 512, _tile(T, (1024, 512, 256, 128))
    return _mm(name, n, du, (N_CHIPS, D, DFF2_SHARD), F32, (D // tk, N_CHIPS, T // tt),
               pl.BlockSpec((tt, tk), lambda i, j, k: (k, i)), pl.BlockSpec((tt, DFF2_SHARD), lambda i, j, k: (k, j)),
               pl.BlockSpec((None, tk, DFF2_SHARD), lambda i, j, k: (j, i, 0)), TN, (tk, DFF2_SHARD))


def _sigmoid(x):
    return 1.0 / (1.0 + jnp.exp(-x))


def _silu(x):
    return x * _sigmoid(x)


def _softplus(x):
    return jnp.maximum(x, 0.0) + jnp.log(1.0 + jnp.exp(-jnp.abs(x)))


def _rms_core(h, w):
    return h * lax.rsqrt(jnp.mean(h * h, axis=-1, keepdims=True) + EPS) * w


def _shift_down(x, halo, s, i):
    if s == 0:
        return x
    tm = x.shape[0]
    rolled = pltpu.roll(x, s, 0)
    patch = pltpu.roll(jnp.where(i == 0, 0.0, halo), s, 0)
    row = lax.broadcasted_iota(jnp.int32, patch.shape, 0)
    top = jnp.where(row < s, patch, rolled[:SUBLANE])
    return jnp.concatenate([top, rolled[SUBLANE:]], axis=0) if tm > SUBLANE else top


def _shift_up(x, halo, s, i, n):
    if s == 0:
        return x
    tm = x.shape[0]
    rolled = pltpu.roll(x, tm - s, 0)
    patch = pltpu.roll(jnp.where(i == n - 1, 0.0, halo), SUBLANE - s, 0)
    row = lax.broadcasted_iota(jnp.int32, patch.shape, 0)
    bottom = jnp.where(row >= SUBLANE - s, patch, rolled[tm - SUBLANE:])
    return jnp.concatenate([rolled[:tm - SUBLANE], bottom], axis=0) if tm > SUBLANE else bottom


def _conv_fwd(x, halo, w, i):
    K = w.shape[0]
    y = w[K - 1:K, :] * x
    for j in range(K - 1):
        y = y + w[j:j + 1, :] * _shift_down(x, halo, K - 1 - j, i)
    return y


def _conv_dx(dy, halo_next, w, i, n):
    K = w.shape[0]
    dx = w[K - 1:K, :] * dy
    for j in range(K - 1):
        dx = dx + w[j:j + 1, :] * _shift_up(dy, halo_next, K - 1 - j, i, n)
    return dx


def _conv_dw(dy, x, halo, K, i):
    rows = [jnp.sum(dy * _shift_down(x, halo, K - 1 - j, i), axis=0, keepdims=True) for j in range(K)]
    return jnp.concatenate(rows + [jnp.zeros((SUBLANE - K, dy.shape[1]), F32)], axis=0)


def _rms_fwd(name, h, w, tm=512):
    T = h.shape[0]
    tm = min(tm, T)

    def fn(i, hv, wv):
        return (_rms_core(hv, wv),)

    return _rowcall(name, fn, T, tm, [(h, "row", None), (w, "full", None)], [((T, D), BF16, "row")])[0]


def _rms_bwd(name, h, pairs, adds, tm=256):
    T = h.shape[0]
    tm = min(tm, T)
    npair, nadd = len(pairs), len(adds)

    def fn(i, hv, *rest):
        ws, dns, ads = rest[:npair], rest[npair:2 * npair], rest[2 * npair:]
        dh = None
        dws = []
        for wv, dn in zip(ws, dns):
            _, vjp = jax.vjp(_rms_core, hv, wv)
            dhi, dwi = vjp(dn.astype(F32))
            dh = dhi if dh is None else dh + dhi
            dws.append(dwi)
        for a in ads:
            dh = dh + a.astype(F32)
        return (dh, *dws)

    ins = [(h, "row", None)] + [(w, "full", None) for w, _ in pairs] + [(dn, "row", None) for _, dn in pairs]
    ins += [(a, "row", None) for a in adds]
    outs = [((T, D), F32, "row")] + [((1, D), F32, "acc")] * npair
    return _rowcall(name, fn, T, tm, ins, outs)


def _l2(x):
    return x * lax.rsqrt(jnp.sum(x * x, axis=-1, keepdims=True) + EPS)


def _gdn_post_core(yq, yk, yv, pb, pa, a_log, dtb):
    qn = tuple(_l2(_silu(a)) * (GDN_HD ** -0.5) for a in yq)
    kn = tuple(_l2(_silu(a)) for a in yk)
    v = _silu(yv)
    beta = _sigmoid(pb)
    g = -jnp.exp(a_log) * _softplus(pa + dtb)
    return qn, kn, v, beta, g


def _heads(x, n):
    return tuple(x[:, GDN_HD * h:GDN_HD * (h + 1)] for h in range(n))


def _gdn_pre_fwd(pm, pba, conv_w, a_log, dtb, tm=128):
    T = pm.shape[0]
    tm = min(tm, T)

    def fn(i, x, halo, pbav, cw, al, db):
        y = _conv_fwd(x.astype(F32), halo.astype(F32), cw, i)
        qn, kn, v, beta, g = _gdn_post_core(_heads(y[:, :GDN_QK], 8), _heads(y[:, GDN_QK:2 * GDN_QK], 8),
                                            y[:, 2 * GDN_QK:], pbav[:, :LANE], pbav[:, LANE:], al, db)
        return jnp.stack(qn), jnp.stack(kn), jnp.stack(_heads(v, GDN_V_HEADS)), beta, g

    ins = [(pm, "row", (GDN_CONV, 0)), (pm, "prev", (GDN_CONV, 0)), (pba, "row", None),
           (conv_w, "full", None), (a_log, "full", None), (dtb, "full", None)]
    outs = [((GDN_QK_HEADS, T, GDN_HD), BF16, "row"), ((GDN_QK_HEADS, T, GDN_HD), BF16, "row"),
            ((GDN_V_HEADS, T, GDN_HD), BF16, "row"), ((T, LANE), F32, "row"), ((T, LANE), F32, "row")]
    return _rowcall("gdn_pre_fwd", fn, T, tm, ins, outs)


def _gdn_pre_bwd(pm, pba, conv_w, a_log, dtb, dqn, dkn, dv, dbeta, dg, tm=128):
    T = pm.shape[0]
    tm = min(tm, T)

    def fn(i, x, halo, pbav, cw, al, db, dqv, dkv, dvv, dbv, dgv):
        xf, hf = x.astype(F32), halo.astype(F32)
        y = _conv_fwd(xf, hf, cw, i)
        prim = (_heads(y[:, :GDN_QK], 8), _heads(y[:, GDN_QK:2 * GDN_QK], 8), y[:, 2 * GDN_QK:],
                pbav[:, :LANE], pbav[:, LANE:], al, db)
        _, vjp = jax.vjp(_gdn_post_core, *prim)
        cot = (tuple(dqv[h].astype(F32) for h in range(8)), tuple(dkv[h].astype(F32) for h in range(8)),
               jnp.concatenate([dvv[h].astype(F32) for h in range(GDN_V_HEADS)], axis=1), dbv, dgv)
        dyq, dyk, dyv, dpb, dpa, dal, ddb = vjp(cot)
        dy = jnp.concatenate(list(dyq) + list(dyk) + [dyv], axis=1)
        dcw = _conv_dw(dy, xf, hf, 4, i)
        return dy, jnp.concatenate([dpb, dpa], axis=1), dcw, dal, ddb

    ins = [(pm, "row", (GDN_CONV, 0)), (pm, "prev", (GDN_CONV, 0)), (pba, "row", None),
           (conv_w, "full", None), (a_log, "full", None), (dtb, "full", None),
           (dqn, "row", None), (dkn, "row", None), (dv, "row", None), (dbeta, "row", None), (dg, "row", None)]
    outs = [((T, GDN_CONV), BF16, "row"), ((T, 2 * LANE), F32, "row"), ((SUBLANE, GDN_CONV), F32, "acc"),
            ((1, LANE), F32, "acc"), ((1, LANE), F32, "acc")]
    return _rowcall("gdn_pre_bwd", fn, T, tm, ins, outs)


def _gdn_conv_bwd(dy, dz, conv_w, tm=256):
    T = dy.shape[0]
    tm = min(tm, T)
    n = T // tm

    def fn(i, dyv, halo, dzv, cw):
        dx = _conv_dx(dyv.astype(F32), halo.astype(F32), cw, i, n)
        return (jnp.concatenate([dx.astype(BF16), dzv.astype(BF16)], axis=1),)

    ins = [(dy, "row", None), (dy, "next", None), (dz, "row", None), (conv_w, "full", None)]
    return _rowcall("gdn_conv_bwd", fn, T, tm, ins, [((T, GDN_MAIN), BF16, "row")])[0]


def _bdot(a, b, dims=NN):
    return lax.dot_general(a.astype(BF16), b.astype(BF16), (dims, ((), ())), preferred_element_type=F32)


def _fdot(a, b):
    return lax.dot_general(a, b, (NN, ((), ())), preferred_element_type=F32, precision=HIGHEST)


def _gdn_chunk(q, k, v, gcol, bcol, S):
    C = q.shape[0]
    r = lax.broadcasted_iota(jnp.int32, (C, C), 0)
    c = lax.broadcasted_iota(jnp.int32, (C, C), 1)
    tril, strict = r >= c, r > c
    gc_col = jnp.sum(jnp.where(tril, jnp.sum(jnp.where(r == c, gcol, 0.0), axis=0, keepdims=True), 0.0),
                     axis=1, keepdims=True)
    gc_row = jnp.sum(jnp.where(r <= c, gcol, 0.0), axis=0, keepdims=True)
    gc_last = jnp.sum(gcol, axis=0, keepdims=True)
    decay = jnp.where(tril, jnp.exp(jnp.where(tril, gc_col - gc_row, 0.0)), 0.0)
    kb = k * bcol
    vb = v * bcol
    m = jnp.where(strict, _bdot(kb, k, NT) * decay, 0.0)
    eye = jnp.where(r == c, 1.0, 0.0)
    t_mat = eye - m
    pw = _fdot(m, m)
    for it in range(int(math.log2(C)) - 1):
        t_mat = t_mat + _fdot(t_mat, pw)
        if it < int(math.log2(C)) - 2:
            pw = _fdot(pw, pw)
    egc = jnp.exp(gc_col)
    u = _bdot(t_mat, vb)
    w = _bdot(t_mat, kb * egc)
    a_intra = _bdot(q, k, NT) * decay
    qd = q * egc
    kd = k * jnp.exp(gc_last - gc_col)
    v_new = u - _bdot(w, S)
    o = _bdot(qd, S) + _bdot(a_intra, v_new)
    s_new = S * jnp.exp(gc_last) + _bdot(kd, v_new, TN)
    return o, s_new


def _gdn_tb(T):
    return min(512, T)


def _gdn_fwd(qn, kn, v, g, beta):
    T = qn.shape[1]
    tb = _gdn_tb(T)
    nc = tb // GDN_CHUNK

    def body(q_ref, k_ref, v_ref, g_ref, b_ref, o_ref, sall_ref, s_scr):
        i, h = pl.program_id(0), pl.program_id(1)
        onehot = lax.broadcasted_iota(jnp.int32, (1, LANE), 1) == h

        @pl.when(i == 0)
        def _():
            s_scr[h] = jnp.zeros((GDN_HD, GDN_HD), F32)

        def chunk(ci, carry):
            rows = pl.ds(pl.multiple_of(ci * GDN_CHUNK, GDN_CHUNK), GDN_CHUNK)
            gcol = jnp.sum(jnp.where(onehot, g_ref[rows, :], 0.0), axis=1, keepdims=True)
            bcol = jnp.sum(jnp.where(onehot, b_ref[rows, :], 0.0), axis=1, keepdims=True)
            s = s_scr[h]
            sall_ref[ci] = s
            o, s_new = _gdn_chunk(q_ref[rows, :].astype(F32), k_ref[rows, :].astype(F32),
                                  v_ref[rows, :].astype(F32), gcol, bcol, s)
            o_ref[rows, :] = o.astype(o_ref.dtype)
            s_scr[h] = s_new
            return carry

        lax.fori_loop(0, nc, chunk, 0)

    qk_spec = pl.BlockSpec((None, tb, GDN_HD), lambda i, h: (h // 2, i, 0))
    v_spec = pl.BlockSpec((None, tb, GDN_HD), lambda i, h: (h, i, 0))
    g_spec = pl.BlockSpec((tb, LANE), lambda i, h: (i, 0))
    return pl.pallas_call(
        body, name="gdn_fwd", grid=(T // tb, GDN_V_HEADS),
        in_specs=[qk_spec, qk_spec, v_spec, g_spec, g_spec],
        out_specs=[v_spec, pl.BlockSpec((nc, None, GDN_HD, GDN_HD), lambda i, h: (i, h, 0, 0))],
        out_shape=[jax.ShapeDtypeStruct((GDN_V_HEADS, T, GDN_HD), BF16),
                   jax.ShapeDtypeStruct((T // GDN_CHUNK, GDN_V_HEADS, GDN_HD, GDN_HD), F32)],
        scratch_shapes=[pltpu.VMEM((GDN_V_HEADS, GDN_HD, GDN_HD), F32)],
        compiler_params=_params(("arbitrary", "arbitrary")),
    )(qn, kn, v, g, beta)


def _gdn_bwd(qn, kn, v, g, beta, sall, do):
    T = qn.shape[1]
    tb = _gdn_tb(T)
    nc = tb // GDN_CHUNK
    nb = T // tb

    def body(q_ref, k_ref, v_ref, g_ref, b_ref, sall_ref, do_ref, dq_ref, dk_ref, dv_ref, dg_ref, db_ref, ds_scr):
        i, h = pl.program_id(0), pl.program_id(1)
        onehot = lax.broadcasted_iota(jnp.int32, (1, LANE), 1) == h

        @pl.when(i == 0)
        def _():
            ds_scr[h] = jnp.zeros((GDN_HD, GDN_HD), F32)

        def chunk(cr, carry):
            ci = nc - 1 - cr
            rows = pl.ds(pl.multiple_of(ci * GDN_CHUNK, GDN_CHUNK), GDN_CHUNK)
            gcol = jnp.sum(jnp.where(onehot, g_ref[rows, :], 0.0), axis=1, keepdims=True)
            bcol = jnp.sum(jnp.where(onehot, b_ref[rows, :], 0.0), axis=1, keepdims=True)
            _, vjp = jax.vjp(_gdn_chunk, q_ref[rows, :].astype(F32), k_ref[rows, :].astype(F32),
                             v_ref[rows, :].astype(F32), gcol, bcol, sall_ref[ci])
            dq, dk, dv, dgc, dbc, ds = vjp((do_ref[rows, :].astype(F32), ds_scr[h]))
            ds_scr[h] = ds
            dv_ref[rows, :] = dv
            dg_add = jnp.where(onehot, dgc, 0.0)
            db_add = jnp.where(onehot, dbc, 0.0)

            @pl.when(h % 2 == 0)
            def _():
                dq_ref[rows, :] = dq
                dk_ref[rows, :] = dk

            @pl.when(h % 2 == 1)
            def _():
                dq_ref[rows, :] += dq
                dk_ref[rows, :] += dk

            @pl.when(h == 0)
            def _():
                dg_ref[rows, :] = dg_add
                db_ref[rows, :] = db_add

            @pl.when(h > 0)
            def _():
                dg_ref[rows, :] += dg_add
                db_ref[rows, :] += db_add

            return carry

        lax.fori_loop(0, nc, chunk, 0)

    qk_spec = pl.BlockSpec((None, tb, GDN_HD), lambda i, h: (h // 2, nb - 1 - i, 0))
    v_spec = pl.BlockSpec((None, tb, GDN_HD), lambda i, h: (h, nb - 1 - i, 0))
    g_spec = pl.BlockSpec((tb, LANE), lambda i, h: (nb - 1 - i, 0))
    s_spec = pl.BlockSpec((nc, None, GDN_HD, GDN_HD), lambda i, h: (nb - 1 - i, h, 0, 0))
    return pl.pallas_call(
        body, name="gdn_bwd", grid=(nb, GDN_V_HEADS),
        in_specs=[qk_spec, qk_spec, v_spec, g_spec, g_spec, s_spec, v_spec],
        out_specs=[qk_spec, qk_spec, v_spec, g_spec, g_spec],
        out_shape=[jax.ShapeDtypeStruct((GDN_QK_HEADS, T, GDN_HD), F32),
                   jax.ShapeDtypeStruct((GDN_QK_HEADS, T, GDN_HD), F32),
                   jax.ShapeDtypeStruct((GDN_V_HEADS, T, GDN_HD), F32),
                   jax.ShapeDtypeStruct((T, LANE), F32), jax.ShapeDtypeStruct((T, LANE), F32)],
        scratch_shapes=[pltpu.VMEM((GDN_V_HEADS, GDN_HD, GDN_HD), F32)],
        compiler_params=_params(("arbitrary", "arbitrary")),
    )(qn, kn, v, g, beta, sall, do)


def _gnorm_core(o, z, w):
    return tuple(_rms_core(oh, w) * _silu(zh) for oh, zh in zip(o, z))


def _gnorm_fwd(o, pm, w, tm=256):
    T = pm.shape[0]
    tm = min(tm, T)

    def fn(i, ov, zv, wv):
        zf = zv.astype(F32)
        out = _gnorm_core(tuple(ov[h].astype(F32) for h in range(GDN_V_HEADS)), _heads(zf, GDN_V_HEADS), wv)
        return (jnp.concatenate(out, axis=1),)

    ins = [(o, "row", None), (pm, "row", (GDN_V, 2)), (w, "full", None)]
    return _rowcall("gnorm_fwd", fn, T, tm, ins, [((T, GDN_V), BF16, "row")])[0]


def _gnorm_bwd(o, pm, w, don, tm=128):
    T = pm.shape[0]
    tm = min(tm, T)

    def fn(i, ov, zv, wv, dv):
        zf, df = zv.astype(F32), dv.astype(F32)
        _, vjp = jax.vjp(_gnorm_core, tuple(ov[h].astype(F32) for h in range(GDN_V_HEADS)),
                         _heads(zf, GDN_V_HEADS), wv)
        do, dz, dw = vjp(_heads(df, GDN_V_HEADS))
        return jnp.stack(do), jnp.concatenate(dz, axis=1), dw

    ins = [(o, "row", None), (pm, "row", (GDN_V, 2)), (w, "full", None), (don, "row", None)]
    outs = [((GDN_V_HEADS, T, GDN_HD), BF16, "row"), ((T, GDN_V), BF16, "row"), ((1, GDN_HD), F32, "acc")]
    return _rowcall("gnorm_bwd", fn, T, tm, ins, outs)


def _ffn_act_fwd(name, up, conv_w, conv_b, tm=128):
    T = up.shape[0]
    tm = min(tm, T)

    def fn(i, x, halo, cw, cb):
        u = _conv_fwd(x.astype(F32), halo.astype(F32), cw, i) + cb
        return (_silu(u[:, :DFF]) * u[:, DFF:],)

    ins = [(up, "row", None), (up, "prev", None), (conv_w, "full", None), (conv_b, "full", None)]
    return _rowcall(name, fn, T, tm, ins, [((T, DFF), BF16, "row")])[0]


def _ffn_act_bwd(name, up, conv_w, conv_b, dact, tm=128):
    T = up.shape[0]
    tm = min(tm, T)

    def fn(i, x, halo, cw, cb, da):
        xf, hf, da = x.astype(F32), halo.astype(F32), da.astype(F32)
        u = _conv_fwd(xf, hf, cw, i) + cb
        gate, val = u[:, :DFF], u[:, DFF:]
        sg = _sigmoid(gate)
        dgate = da * val * sg * (1.0 + gate * (1.0 - sg))
        dval = da * gate * sg
        du = jnp.concatenate([dgate, dval], axis=1)
        return du, _conv_dw(du, xf, hf, 3, i), jnp.sum(du, axis=0, keepdims=True)

    ins = [(up, "row", None), (up, "prev", None), (conv_w, "full", None), (conv_b, "full", None),
           (dact, "row", None)]
    outs = [((T, DFF2), BF16, "row"), ((SUBLANE, DFF2), F32, "acc"), ((1, DFF2), F32, "acc")]
    return _rowcall(name, fn, T, tm, ins, outs)


def _ffn_conv_bwd(name, du, conv_w, tm=256):
    T = du.shape[0]
    tm = min(tm, T)
    n = T // tm

    def fn(i, dv, halo, cw):
        return (_conv_dx(dv.astype(F32), halo.astype(F32), cw, i, n),)

    ins = [(du, "row", None), (du, "next", None), (conv_w, "full", None)]
    return _rowcall(name, fn, T, tm, ins, [((T, DFF2), BF16, "row")])[0]


def _attn_core(qs, kp, kc, vp, vc, biases, sinks, mask):
    kcat = jnp.concatenate([kp, kc], axis=0)
    vcat = jnp.concatenate([vp, vc], axis=0)
    outs = []
    for q, bias, sink in zip(qs, biases, sinks):
        s = _bdot(q * (SWA_HD ** -0.5), kcat, NT) + bias
        s = jnp.where(mask, s, NEG_INF)
        m = lax.stop_gradient(jnp.maximum(jnp.max(s, axis=-1, keepdims=True), sink))
        p = jnp.exp(s - m)
        denom = jnp.sum(p, axis=-1, keepdims=True) + jnp.exp(sink - m)
        outs.append(_bdot(p / denom, vcat))
    return tuple(outs)


def _attn_mask(i):
    qi = lax.broadcasted_iota(jnp.int32, (SWA_BLOCK, 2 * SWA_BLOCK), 0)
    ki = lax.broadcasted_iota(jnp.int32, (SWA_BLOCK, 2 * SWA_BLOCK), 1)
    dist = qi + SWA_BLOCK - ki
    return (dist >= 0) & (dist < SWA_BLOCK) & ((ki >= SWA_BLOCK) | (i > 0))


def _attn_fwd(q, k, v, bias, sinks):
    T = q.shape[1]
    nb = T // SWA_BLOCK

    def body(q_ref, kc_ref, kp_ref, vc_ref, vp_ref, b_ref, s_ref, o_ref):
        i = pl.program_id(1)
        outs = _attn_core(tuple(q_ref[g].astype(F32) for g in range(SWA_GROUP)),
                          kp_ref[...].astype(F32), kc_ref[...].astype(F32),
                          vp_ref[...].astype(F32), vc_ref[...].astype(F32),
                          tuple(b_ref[g] for g in range(SWA_GROUP)),
                          tuple(s_ref[g:g + 1, 0:1] for g in range(SWA_GROUP)), _attn_mask(i))
        for g in range(SWA_GROUP):
            o_ref[g] = outs[g].astype(o_ref.dtype)

    q_spec = pl.BlockSpec((SWA_GROUP, SWA_BLOCK, SWA_HD), lambda j, i: (j, i, 0))
    cur = pl.BlockSpec((None, SWA_BLOCK, SWA_HD), lambda j, i: (j, i, 0))
    prev = pl.BlockSpec((None, SWA_BLOCK, SWA_HD), lambda j, i: (j, jnp.maximum(i - 1, 0), 0))
    return pl.pallas_call(
        body, name="attn_fwd", grid=(SWA_KV_HEADS, nb),
        in_specs=[q_spec, cur, prev, cur, prev,
                  pl.BlockSpec((SWA_GROUP, SWA_BLOCK, 2 * SWA_BLOCK), lambda j, i: (j, 0, 0)),
                  pl.BlockSpec((None, SWA_GROUP, LANE), lambda j, i: (j, 0, 0))],
        out_specs=q_spec, out_shape=jax.ShapeDtypeStruct(q.shape, BF16),
        compiler_params=_params(("parallel", "arbitrary")),
    )(q, k, k, v, v, bias, sinks)


def _attn_bwd(q, k, v, bias, sinks, do):
    T = q.shape[1]
    nb = T // SWA_BLOCK

    def body(q_ref, kc_ref, kp_ref, vc_ref, vp_ref, b_ref, s_ref, do_ref,
             dq_ref, dk_ref, dv_ref, db_ref, dsk_ref, kcar, vcar):
        i = pl.program_id(1)

        @pl.when(i < nb)
        def _():
            prim = (tuple(q_ref[g].astype(F32) for g in range(SWA_GROUP)),
                    kp_ref[...].astype(F32), kc_ref[...].astype(F32),
                    vp_ref[...].astype(F32), vc_ref[...].astype(F32),
                    tuple(b_ref[g] for g in range(SWA_GROUP)),
                    tuple(s_ref[g:g + 1, 0:1] for g in range(SWA_GROUP)))
            _, vjp = jax.vjp(functools.partial(_attn_core, mask=_attn_mask(i)), *prim)
            dqs, dkp, dkc, dvp, dvc, dbs, dss = vjp(tuple(do_ref[g].astype(F32) for g in range(SWA_GROUP)))
            for g in range(SWA_GROUP):
                dq_ref[g] = dqs[g].astype(dq_ref.dtype)
            dsk = jnp.concatenate([jnp.broadcast_to(d, (1, LANE)) for d in dss], axis=0)

            @pl.when(i == 0)
            def _():
                for g in range(SWA_GROUP):
                    db_ref[g] = dbs[g]
                dsk_ref[...] = dsk

            @pl.when(i > 0)
            def _():
                for g in range(SWA_GROUP):
                    db_ref[g] += dbs[g]
                dsk_ref[...] += dsk
                dk_ref[...] = (kcar[...] + dkp).astype(dk_ref.dtype)
                dv_ref[...] = (vcar[...] + dvp).astype(dv_ref.dtype)

            kcar[...] = dkc
            vcar[...] = dvc

        @pl.when(i == nb)
        def _():
            dk_ref[...] = kcar[...].astype(dk_ref.dtype)
            dv_ref[...] = vcar[...].astype(dv_ref.dtype)

    last = nb - 1
    q_spec = pl.BlockSpec((SWA_GROUP, SWA_BLOCK, SWA_HD), lambda j, i: (j, jnp.minimum(i, last), 0))
    cur = pl.BlockSpec((None, SWA_BLOCK, SWA_HD), lambda j, i: (j, jnp.minimum(i, last), 0))
    prev = pl.BlockSpec((None, SWA_BLOCK, SWA_HD), lambda j, i: (j, jnp.clip(i - 1, 0, last), 0))
    b_spec = pl.BlockSpec((SWA_GROUP, SWA_BLOCK, 2 * SWA_BLOCK), lambda j, i: (j, 0, 0))
    s_spec = pl.BlockSpec((None, SWA_GROUP, LANE), lambda j, i: (j, 0, 0))
    return pl.pallas_call(
        body, name="attn_bwd", grid=(SWA_KV_HEADS, nb + 1),
        in_specs=[q_spec, cur, prev, cur, prev, b_spec, s_spec, q_spec],
        out_specs=[q_spec, prev, prev, b_spec, s_spec],
        out_shape=[jax.ShapeDtypeStruct(q.shape, BF16), jax.ShapeDtypeStruct(k.shape, BF16),
                   jax.ShapeDtypeStruct(k.shape, BF16), jax.ShapeDtypeStruct(bias.shape, F32),
                   jax.ShapeDtypeStruct(sinks.shape, F32)],
        scratch_shapes=[pltpu.VMEM((SWA_BLOCK, SWA_HD), F32), pltpu.VMEM((SWA_BLOCK, SWA_HD), F32)],
        compiler_params=_params(("parallel", "arbitrary")),
    )(q, k, k, v, v, bias, sinks, do)


def _rel_onehot():
    qi = jnp.arange(SWA_BLOCK)[:, None]
    ki = jnp.arange(2 * SWA_BLOCK)[None, :]
    n = jnp.maximum(qi + SWA_BLOCK - ki, 0)
    max_exact = REL_BUCKETS // 2
    nf = jnp.maximum(n, 1).astype(F32)
    large = max_exact + (jnp.log(nf / max_exact) / math.log(REL_MAX_DISTANCE / max_exact)
                         * (REL_BUCKETS - max_exact)).astype(jnp.int32)
    bucket = jnp.where(n < max_exact, n, jnp.minimum(large, REL_BUCKETS - 1)).reshape(-1)
    return (bucket[None, :] == jnp.arange(REL_BUCKETS)[:, None]).astype(F32)


def _final(h, w, target, tm=256):
    T = h.shape[0]
    tm = min(tm, T)

    def fn(i, hv, wv, tv):
        y, vjp = jax.vjp(_rms_core, hv, wv)
        err = y - tv
        dh, dw = vjp(err * (1.0 / D))
        part = 0.5 * jnp.sum(jnp.sum(err * err, axis=1, keepdims=True) * (1.0 / D), axis=0, keepdims=True)
        return jnp.broadcast_to(part, (SUBLANE, LANE)), dh, dw

    ins = [(h, "row", None), (w, "full", None), (target, "row", None)]
    outs = [((SUBLANE, LANE), F32, "acc"), ((T, D), F32, "row"), ((1, D), F32, "acc")]
    return _rowcall("final", fn, T, tm, ins, outs)


def _heads_major(a, heads, hd):
    return a.reshape(a.shape[0], heads, hd).transpose(1, 0, 2)


def _heads_minor(a):
    return a.transpose(1, 0, 2).reshape(a.shape[1], a.shape[0] * a.shape[2])


def _ffn_fwd(tag, h, P, layer):
    n = _rms_fwd(f"{tag}_rms", h, P["ffn_norm_w"][layer:layer + 1])
    up = _mm_up(f"{tag}_up", n, P["w_up"], layer)
    act = _ffn_act_fwd(f"{tag}_act", up, P["ffn_conv_w"][layer], P["ffn_conv_b"][layer:layer + 1])
    out = _mm_nn(f"{tag}_down", act, P["w_down"][layer], F32, res=h)
    return out, (n, up, act)


def _ffn_bwd(tag, h, saved, dout, P, layer):
    n, up, act = saved
    cw, cb = P["ffn_conv_w"][layer], P["ffn_conv_b"][layer:layer + 1]
    dact = _mm_nt(f"{tag}_down_dx", dout, P["w_down"][layer], BF16)
    g_down = _mm_tn(f"{tag}_down_dw", act, dout)
    du, dcw, dcb = _ffn_act_bwd(f"{tag}_act_bwd", up, cw, cb, dact)
    dup = _ffn_conv_bwd(f"{tag}_conv_bwd", du, cw)
    g_up = _mm_up_tn(f"{tag}_up_dw", n, dup)
    dn = _mm_up_nt(f"{tag}_up_dx", dup, P["w_up"], layer)
    dh, dnw = _rms_bwd(f"{tag}_rms_bwd", h, [(P["ffn_norm_w"][layer:layer + 1], dn)], [dout])
    return dh, dict(w_down=g_down, w_up=g_up, conv_w=dcw[:3], conv_b=dcb, norm_w=dnw)


def _local_step(x, target, P):
    T = x.shape[0]
    n0 = _rms_fwd("a_rms", x, P["a_norm_w"])
    pm = _mm_nn("gdn_in", n0, P["w_in_main"], BF16)
    pba = _mm_nn("gdn_in_ba", n0, P["w_in_ba"], F32)
    qn, kn, v, beta, g = _gdn_pre_fwd(pm, pba, P["a_conv_w"], P["a_log"], P["dt_bias"])
    o, sall = _gdn_fwd(qn, kn, v, g, beta)
    on = _gnorm_fwd(o, pm, P["a_out_norm_w"])
    h1 = _mm_nn("gdn_out", on, P["w_out"], F32, res=x)
    h2, ffn0 = _ffn_fwd("ffn0", h1, P, 0)
    nkv = _rms_fwd("kv_rms", h2, P["kv_norm_w"])
    kv = _mm_nn("kv_proj", nkv, P["w_kv"], BF16)
    nb = _rms_fwd("b_rms", h2, P["b_norm_w"])
    qp = _mm_nn("q_proj", nb, P["w_q"], BF16)
    q3 = _heads_major(qp, SWA_Q_HEADS, SWA_HD)
    k3 = _heads_major(kv[:, :SWA_KV_HEADS * SWA_HD], SWA_KV_HEADS, SWA_HD)
    v3 = _heads_major(kv[:, SWA_KV_HEADS * SWA_HD:], SWA_KV_HEADS, SWA_HD)
    onehot = _rel_onehot()
    bias = _mm_nn("rel_bias", P["rel_table_t"], onehot, F32, precision=HIGHEST)
    bias = bias.reshape(SWA_Q_HEADS, SWA_BLOCK, 2 * SWA_BLOCK)
    oa = _heads_minor(_attn_fwd(q3, k3, v3, bias, P["sinks"]))
    h3 = _mm_nn("o_proj", oa, P["w_o"], F32, res=h2)
    h4, ffn1 = _ffn_fwd("ffn1", h3, P, 1)
    loss, dh4, d_final = _final(h4, P["final_norm_w"], target)

    dh3, gf1 = _ffn_bwd("ffn1", h3, ffn1, dh4, P, 1)
    doa = _mm_nt("o_proj_dx", dh3, P["w_o"], BF16)
    g_wo = _mm_tn("o_proj_dw", oa, dh3)
    dq3, dk3, dv3, dbias, dsinks = _attn_bwd(q3, k3, v3, bias, P["sinks"], _heads_major(doa, SWA_Q_HEADS, SWA_HD))
    dqp = _heads_minor(dq3)
    dkv = jnp.concatenate([_heads_minor(dk3), _heads_minor(dv3)], axis=1)
    g_wq = _mm_tn("q_proj_dw", nb, dqp)
    dnb = _mm_nt("q_proj_dx", dqp, P["w_q"], F32)
    g_wkv = _mm_tn("kv_proj_dw", nkv, dkv)
    dnkv = _mm_nt("kv_proj_dx", dkv, P["w_kv"], F32)
    dh2, d_bnorm, d_kvnorm = _rms_bwd("b_kv_rms_bwd", h2, [(P["b_norm_w"], dnb), (P["kv_norm_w"], dnkv)], [dh3])
    g_table = _mm_nt("rel_bias_dw", onehot, dbias.reshape(SWA_Q_HEADS, -1), F32, precision=HIGHEST)
    dh1, gf0 = _ffn_bwd("ffn0", h1, ffn0, dh2, P, 0)
    don = _mm_nt("gdn_out_dx", dh1, P["w_out"], BF16)
    g_wout = _mm_tn("gdn_out_dw", on, dh1)
    do, dz, d_gnorm = _gnorm_bwd(o, pm, P["a_out_norm_w"], don)
    dq, dk, dv, dg, dbeta = _gdn_bwd(qn, kn, v, g, beta, sall, do)
    dy, dpba, d_aconv, d_alog, d_dtb = _gdn_pre_bwd(pm, pba, P["a_conv_w"], P["a_log"], P["dt_bias"],
                                                    dq, dk, dv, dbeta, dg)
    dpm = _gdn_conv_bwd(dy, dz, P["a_conv_w"])
    g_win_main = _mm_tn("gdn_in_dw", n0, dpm)
    g_win_ba = _mm_tn("gdn_in_ba_dw", n0, dpba)
    dn0 = _mm_nt("gdn_in_dx", dpm, P["w_in_main"], F32)
    dn0 = _mm_nt("gdn_in_ba_dx", dpba, P["w_in_ba"], F32, res=dn0)
    dx, d_anorm = _rms_bwd("a_rms_bwd", x, [(P["a_norm_w"], dn0)], [dh1])

    nh = GDN_V_HEADS
    grads = dict(
        a_norm_w=d_anorm,
        a_w_in=jnp.concatenate([g_win_main, g_win_ba[:, :nh], g_win_ba[:, LANE:LANE + nh]], axis=1),
        a_conv_w=d_aconv[:4], a_a_log=d_alog[:, :nh], a_dt_bias=d_dtb[:, :nh], a_out_norm_w=d_gnorm,
        a_w_out=g_wout, kv_norm_w=d_kvnorm, w_kv=g_wkv, b_norm_w=d_bnorm, b_w_q=g_wq,
        b_sinks=dsinks[:, :, 0].reshape(1, SWA_Q_HEADS), b_w_o=g_wo, rel_bias_table=g_table,
        ffn_norm_w=jnp.concatenate([gf0["norm_w"], gf1["norm_w"]], axis=0),
        ffn_w_up=jnp.stack([gf0["w_up"], gf1["w_up"]], axis=1),
        ffn_conv_w=jnp.stack([gf0["conv_w"], gf1["conv_w"]], axis=0),
        ffn_conv_b=jnp.concatenate([gf0["conv_b"], gf1["conv_b"]], axis=0),
        ffn_w_down=jnp.stack([gf0["w_down"], gf1["w_down"]], axis=0), final_norm_w=d_final,
    )
    return loss, dx, grads


HBM_SPEC = pl.BlockSpec(memory_space=pltpu.HBM)
VMEM_SPEC = pl.BlockSpec(memory_space=pltpu.VMEM)


def _coords():
    return lax.axis_index("x"), lax.axis_index("y"), lax.axis_index("c")


def _remote(src, dst, send_sem, recv_sem, device):
    return pltpu.make_async_remote_copy(src_ref=src, dst_ref=dst, send_sem=send_sem, recv_sem=recv_sem,
                                        device_id=device, device_id_type=MESH)


def _other_chips(x, y):
    return [(1 - x, y), (x, 1 - y), (1 - x, 1 - y)]


def _all_gather(arrs, split):
    n = len(arrs)

    def body(*refs):
        ins, outs = refs[:n], refs[n:2 * n]
        send_sems, recv_sems, local_sems = refs[2 * n:]
        x, y, c = _coords()
        p = 2 * x + y
        chips = _other_chips(x, y)

        def rows(a, half):
            h = arrs[a].shape[0] // 2
            return pl.ds(half * h, h)

        pending = []
        for a in range(n):
            cp = pltpu.make_async_copy(ins[a], outs[a].at[p], local_sems.at[a])
            cp.start()
            pending.append(cp)
        sends = []
        for a in range(n):
            for j, chip in enumerate(chips):
                if split[a]:
                    src, dst = ins[a].at[rows(a, c)], outs[a].at[p, rows(a, c)]
                else:
                    src, dst = ins[a], outs[a].at[p]
                cp = _remote(src, dst, send_sems.at[6 * a + j], recv_sems.at[6 * a + j], (*chip, c))
                cp.start()
                sends.append(cp)
        for a in range(n):
            for j, chip in enumerate(chips):
                q = 2 * chip[0] + chip[1]
                land = outs[a].at[q, rows(a, c)] if split[a] else outs[a].at[q]
                _remote(land, land, send_sems.at[6 * a + j], recv_sems.at[6 * a + j], (*chip, c)).wait_recv()
                if split[a]:
                    fw = _remote(land, land, send_sems.at[6 * a + 3 + j], recv_sems.at[6 * a + 3 + j], (x, y, 1 - c))
                    fw.start()
                    sends.append(fw)
        for a in range(n):
            if split[a]:
                for j, chip in enumerate(chips):
                    q = 2 * chip[0] + chip[1]
                    land = outs[a].at[q, rows(a, 1 - c)]
                    _remote(land, land, send_sems.at[6 * a + 3 + j], recv_sems.at[6 * a + 3 + j],
                            (x, y, 1 - c)).wait_recv()
        for cp in sends:
            cp.wait_send()
        for cp in pending:
            cp.wait()

    return pl.pallas_call(
        body, name="weights_all_gather", in_specs=[HBM_SPEC] * n, out_specs=[HBM_SPEC] * n,
        out_shape=[jax.ShapeDtypeStruct((N_CHIPS,) + a.shape, a.dtype) for a in arrs],
        scratch_shapes=[pltpu.SemaphoreType.DMA((6 * n,)), pltpu.SemaphoreType.DMA((6 * n,)),
                        pltpu.SemaphoreType.DMA((n,))],
    )(*arrs)


def _pair_swap(gs):
    n = len(gs)

    def body(*refs):
        ins, own, other = refs[:n], refs[n:2 * n], refs[2 * n:3 * n]
        send_sems, recv_sems, local_sems = refs[3 * n:]
        x, y, c = _coords()
        cps = []
        for a in range(n):
            h = gs[a].shape[1] // 2
            cp = pltpu.make_async_copy(ins[a].at[:, pl.ds(c * h, h)], own[a], local_sems.at[a])
            cp.start()
            cps.append(cp)
            cp = _remote(ins[a].at[:, pl.ds((1 - c) * h, h)], other[a], send_sems.at[a], recv_sems.at[a],
                         (x, y, 1 - c))
            cp.start()
            cps.append(cp)
        for cp in cps:
            cp.wait()

    half = [jax.ShapeDtypeStruct((N_CHIPS, g.shape[1] // 2, g.shape[2]), g.dtype) for g in gs]
    res = pl.pallas_call(
        body, name="grads_pair_swap", in_specs=[HBM_SPEC] * n, out_specs=[HBM_SPEC] * (2 * n), out_shape=half + half,
        scratch_shapes=[pltpu.SemaphoreType.DMA((n,)), pltpu.SemaphoreType.DMA((n,)), pltpu.SemaphoreType.DMA((n,))],
    )(*gs)
    return res[:n], res[n:]


def _chip_scatter(ps):
    n = len(ps)

    def body(*refs):
        ins, outs = refs[:n], refs[n:2 * n]
        send_sems, recv_sems, local_sems = refs[2 * n:]
        x, y, c = _coords()
        p = 2 * x + y
        chips = _other_chips(x, y)
        cps, sends = [], []
        for a in range(n):
            cp = pltpu.make_async_copy(ins[a].at[p], outs[a].at[p], local_sems.at[a])
            cp.start()
            cps.append(cp)
            for j, chip in enumerate(chips):
                q = 2 * chip[0] + chip[1]
                cp = _remote(ins[a].at[q], outs[a].at[p], send_sems.at[3 * a + j], recv_sems.at[3 * a + j], (*chip, c))
                cp.start()
                sends.append(cp)
        for a in range(n):
            for j, chip in enumerate(chips):
                q = 2 * chip[0] + chip[1]
                land = outs[a].at[q]
                _remote(land, land, send_sems.at[3 * a + j], recv_sems.at[3 * a + j], (*chip, c)).wait_recv()
        for cp in sends:
            cp.wait_send()
        for cp in cps:
            cp.wait()

    return pl.pallas_call(
        body, name="grads_chip_scatter", in_specs=[HBM_SPEC] * n, out_specs=[HBM_SPEC] * n,
        out_shape=[jax.ShapeDtypeStruct(a.shape, a.dtype) for a in ps],
        scratch_shapes=[pltpu.SemaphoreType.DMA((3 * n,)), pltpu.SemaphoreType.DMA((3 * n,)),
                        pltpu.SemaphoreType.DMA((n,))],
    )(*ps)


def _pair_share(rs):
    n = len(rs)

    def body(*refs):
        ins, outs = refs[:n], refs[n:2 * n]
        send_sems, recv_sems, local_sems = refs[2 * n:]
        x, y, c = _coords()
        cps, sends = [], []
        for a in range(n):
            h = rs[a].shape[0]
            cp = pltpu.make_async_copy(ins[a], outs[a].at[pl.ds(c * h, h)], local_sems.at[a])
            cp.start()
            cps.append(cp)
            cp = _remote(ins[a], outs[a].at[pl.ds(c * h, h)], send_sems.at[a], recv_sems.at[a], (x, y, 1 - c))
            cp.start()
            sends.append(cp)
        for a in range(n):
            h = rs[a].shape[0]
            land = outs[a].at[pl.ds((1 - c) * h, h)]
            _remote(land, land, send_sems.at[a], recv_sems.at[a], (x, y, 1 - c)).wait_recv()
        for cp in sends:
            cp.wait_send()
        for cp in cps:
            cp.wait()

    return pl.pallas_call(
        body, name="grads_pair_share", in_specs=[HBM_SPEC] * n, out_specs=[HBM_SPEC] * n,
        out_shape=[jax.ShapeDtypeStruct((2 * a.shape[0], a.shape[1]), a.dtype) for a in rs],
        scratch_shapes=[pltpu.SemaphoreType.DMA((n,)), pltpu.SemaphoreType.DMA((n,)), pltpu.SemaphoreType.DMA((n,))],
    )(*rs)


def _small_all_reduce(buf):
    R = buf.shape[0]
    ndev = 2 * N_CHIPS

    def body(in_ref, out_ref, gath, send_sems, recv_sems):
        x, y, c = _coords()
        me = 4 * x + 2 * y + c
        gath[me] = in_ref[...]
        peers = []
        for d in range(1, ndev):
            px = 1 - x if d & 4 else x
            py = 1 - y if d & 2 else y
            pc = 1 - c if d & 1 else c
            peers.append((px, py, pc))
        sends = []
        for d, peer in enumerate(peers):
            cp = _remote(in_ref, gath.at[me], send_sems.at[d], recv_sems.at[d], peer)
            cp.start()
            sends.append(cp)
        for d, peer in enumerate(peers):
            land = gath.at[4 * peer[0] + 2 * peer[1] + peer[2]]
            _remote(land, land, send_sems.at[d], recv_sems.at[d], peer).wait_recv()
        for cp in sends:
            cp.wait_send()
        acc = gath[0]
        for s in range(1, ndev):
            acc = acc + gath[s]
        out_ref[...] = acc

    return pl.pallas_call(
        body, name="small_all_reduce", in_specs=[VMEM_SPEC], out_specs=VMEM_SPEC,
        out_shape=jax.ShapeDtypeStruct(buf.shape, F32),
        scratch_shapes=[pltpu.VMEM((ndev, R, LANE), F32), pltpu.SemaphoreType.DMA((ndev - 1,)),
                        pltpu.SemaphoreType.DMA((ndev - 1,))],
    )(buf)


def _pair_add(name, own, other):
    h = own.shape[1]
    tm = _tile(h, (128, 64, 32, 16))

    def fn(i, a, b):
        return (a + b,)

    return _rowcall(name, fn, h, tm, [(own, "row", None), (other, "row", None)], [(own.shape, BF16, "row")])[0]


def _chip_add(name, parts):
    h = parts.shape[1]
    tm = _tile(h, (128, 64, 32, 16))

    def fn(i, a):
        a = a.astype(F32)
        return (((a[0] + a[1]) + a[2]) + a[3],)

    return _rowcall(name, fn, h, tm, [(parts, "row", None)], [(parts.shape[1:], F32, "row")])[0]


def _adamw(name, w, g, m, v):
    R = w.shape[0]
    tm = _tile(R, (256, 128, 64, 32, 16, 8))

    def fn(i, wv, gv, mv, vv):
        m2 = ADAM_B1 * mv + (1.0 - ADAM_B1) * gv
        v2 = ADAM_B2 * vv + (1.0 - ADAM_B2) * (gv * gv)
        m_hat = m2 / (1.0 - ADAM_B1 ** ADAM_STEP)
        v_hat = v2 / (1.0 - ADAM_B2 ** ADAM_STEP)
        delta = -ADAM_LR * (m_hat / (jnp.sqrt(v_hat) + ADAM_EPS) + ADAM_WD * wv)
        return delta, m2, v2

    ins = [(a, "row", None) for a in (w, g, m, v)]
    return _rowcall(name, fn, R, tm, ins, [(w.shape, F32, "row")] * 3)


def _pack(arrs):
    flat = jnp.concatenate([a.reshape(-1).astype(F32) for a in arrs])
    size = flat.shape[0]
    padded = -(-size // (SUBLANE * LANE)) * SUBLANE * LANE
    return jnp.pad(flat, (0, padded - size)).reshape(-1, LANE)


def _unpack(buf, shapes):
    flat = buf.reshape(-1)
    out, off = [], 0
    for s in shapes:
        size = math.prod(s)
        out.append(flat[off:off + size].reshape(s))
        off += size
    return out


BIG = ("a_w_in", "a_w_out", "w_kv", "b_w_q", "b_w_o", "ffn_w_up", "ffn_w_down")
WEIGHTS = ("a_norm_w", "a_w_in", "a_conv_w", "a_a_log", "a_dt_bias", "a_out_norm_w", "a_w_out", "kv_norm_w", "w_kv",
           "b_norm_w", "b_w_q", "b_sinks", "b_w_o", "rel_bias_table", "ffn_norm_w", "ffn_w_up", "ffn_conv_w",
           "ffn_conv_b", "ffn_w_down", "final_norm_w")
SMALL = tuple(n for n in WEIGHTS if n not in BIG)
SMALL_SHARDED = {"a_norm_w": 1, "a_conv_w": 2, "ffn_conv_w": 2}


def _quarter_2d(name, a):
    if name in ("ffn_w_up", "ffn_w_down"):
        return a.reshape(a.shape[0] * a.shape[1], a.shape[2])
    return a.reshape(a.shape[-2], a.shape[-1])


def _whole_weights(w):
    bigs = [_quarter_2d(n, w[n]).astype(BF16) for n in BIG]
    smalls = [w["a_norm_w"], w["a_conv_w"][0], w["ffn_conv_w"].reshape(6, DFF2_SHARD)]
    g = _all_gather(bigs + smalls, [True] * len(bigs) + [False] * len(smalls))
    w_in = g[0].transpose(1, 0, 2).reshape(D, GDN_IN)
    nh = GDN_V_HEADS
    zpad = jnp.zeros((D, LANE - nh), BF16)
    w_in_ba = jnp.concatenate([w_in[:, GDN_MAIN:GDN_MAIN + nh], zpad, w_in[:, GDN_MAIN + nh:], zpad], axis=1)
    lane_pad = lambda a: jnp.pad(a, ((0, 0), (0, LANE - nh)))
    return dict(
        a_norm_w=g[7].reshape(1, D), w_in_main=w_in[:, :GDN_MAIN], w_in_ba=w_in_ba,
        a_conv_w=g[8].transpose(1, 0, 2).reshape(4, GDN_CONV), a_log=lane_pad(w["a_a_log"]),
        dt_bias=lane_pad(w["a_dt_bias"]), a_out_norm_w=w["a_out_norm_w"], w_out=g[1].reshape(GDN_V, D),
        kv_norm_w=w["kv_norm_w"].reshape(1, D), w_kv=g[2].reshape(D, 2 * SWA_KV_HEADS * SWA_HD),
        b_norm_w=w["b_norm_w"], w_q=g[3].reshape(D, D), w_o=g[4].reshape(D, D),
        sinks=jnp.broadcast_to(w["b_sinks"].reshape(SWA_KV_HEADS, SWA_GROUP, 1), (SWA_KV_HEADS, SWA_GROUP, LANE)),
        rel_table_t=w["rel_bias_table"].T, ffn_norm_w=w["ffn_norm_w"],
        w_up=g[5].reshape(N_CHIPS, 2, D, DFF2_SHARD),
        ffn_conv_w=g[9].reshape(N_CHIPS, 2, 3, DFF2_SHARD).transpose(1, 2, 0, 3).reshape(2, 3, DFF2),
        ffn_conv_b=w["ffn_conv_b"],
        w_down=g[6].reshape(N_CHIPS, 2, DFF_SHARD, D).transpose(1, 0, 2, 3).reshape(2, DFF, D),
        final_norm_w=w["final_norm_w"].reshape(1, D),
    )


def _chip_major(name, g):
    if name == "a_w_in":
        return g.reshape(D, N_CHIPS, GDN_IN_SHARD).transpose(1, 0, 2)
    if name == "ffn_w_up":
        return g.reshape(N_CHIPS, 2 * D, DFF2_SHARD)
    if name == "ffn_w_down":
        return g.reshape(2, N_CHIPS, DFF_SHARD, D).transpose(1, 0, 2, 3).reshape(N_CHIPS, 2 * DFF_SHARD, D)
    return g.reshape(N_CHIPS, g.shape[0] // N_CHIPS, g.shape[1])


def kernel(x, a_norm_w, a_w_in, a_conv_w, a_a_log, a_dt_bias, a_out_norm_w, a_w_out, kv_norm_w, w_kv, b_norm_w, b_w_q, b_sinks, b_w_o, rel_bias_table, ffn_norm_w, ffn_w_up, ffn_conv_w, ffn_conv_b, ffn_w_down, final_norm_w, loss_target, m_a_norm_w, m_a_w_in, m_a_conv_w, m_a_a_log, m_a_dt_bias, m_a_out_norm_w, m_a_w_out, m_kv_norm_w, m_w_kv, m_b_norm_w, m_b_w_q, m_b_sinks, m_b_w_o, m_rel_bias_table, m_ffn_norm_w, m_ffn_w_up, m_ffn_conv_w, m_ffn_conv_b, m_ffn_w_down, m_final_norm_w, v_a_norm_w, v_a_w_in, v_a_conv_w, v_a_a_log, v_a_dt_bias, v_a_out_norm_w, v_a_w_out, v_kv_norm_w, v_w_kv, v_b_norm_w, v_b_w_q, v_b_sinks, v_b_w_o, v_rel_bias_table, v_ffn_norm_w, v_ffn_w_up, v_ffn_conv_w, v_ffn_conv_b, v_ffn_w_down, v_final_norm_w):
    w = dict(zip(WEIGHTS, (a_norm_w, a_w_in, a_conv_w, a_a_log, a_dt_bias, a_out_norm_w, a_w_out, kv_norm_w, w_kv,
                           b_norm_w, b_w_q, b_sinks, b_w_o, rel_bias_table, ffn_norm_w, ffn_w_up, ffn_conv_w,
                           ffn_conv_b, ffn_w_down, final_norm_w)))
    m = dict(zip(WEIGHTS, (m_a_norm_w, m_a_w_in, m_a_conv_w, m_a_a_log, m_a_dt_bias, m_a_out_norm_w, m_a_w_out,
                           m_kv_norm_w, m_w_kv, m_b_norm_w, m_b_w_q, m_b_sinks, m_b_w_o, m_rel_bias_table,
                           m_ffn_norm_w, m_ffn_w_up, m_ffn_conv_w, m_ffn_conv_b, m_ffn_w_down, m_final_norm_w)))
    v = dict(zip(WEIGHTS, (v_a_norm_w, v_a_w_in, v_a_conv_w, v_a_a_log, v_a_dt_bias, v_a_out_norm_w, v_a_w_out,
                           v_kv_norm_w, v_w_kv, v_b_norm_w, v_b_w_q, v_b_sinks, v_b_w_o, v_rel_bias_table,
                           v_ffn_norm_w, v_ffn_w_up, v_ffn_conv_w, v_ffn_conv_b, v_ffn_w_down, v_final_norm_w)))
    T = x.shape[1]
    chip = 2 * lax.axis_index("x") + lax.axis_index("y")

    loss_part, dx, grads = _local_step(x.reshape(T, D), loss_target.reshape(T, D), _whole_weights(w))

    own, other = _pair_swap([_chip_major(n, grads[n]) for n in BIG])
    pair = [_pair_add(f"pair_add_{n}", a, b) for n, a, b in zip(BIG, own, other)]
    parts = _chip_scatter(pair)
    halves = [_chip_add(f"chip_add_{n}", a) for n, a in zip(BIG, parts)]
    quarter = _pair_share(halves)
    out_g, out_d, out_m, out_v = {}, {}, {}, {}
    for n, g2 in zip(BIG, quarter):
        res = _adamw(f"adamw_{n}", _quarter_2d(n, w[n]), g2, _quarter_2d(n, m[n]), _quarter_2d(n, v[n]))
        out_g[n] = g2.reshape(w[n].shape)
        out_d[n], out_m[n], out_v[n] = (r.reshape(w[n].shape) for r in res)

    whole = [grads[n] for n in SMALL]
    summed = _unpack(_small_all_reduce(_pack([loss_part[0:1, 0:1]] + whole)), [(1, 1)] + [a.shape for a in whole])
    loss = summed[0].reshape(())
    small_g = []
    for n, g in zip(SMALL, summed[1:]):
        if n in SMALL_SHARDED:
            axis = SMALL_SHARDED[n]
            g = g.reshape(w[n].shape[:axis] + (-1,) + w[n].shape[axis + 1:])
            size = w[n].shape[axis]
            g = lax.dynamic_slice_in_dim(g, chip * size, size, axis)
        small_g.append(g.reshape(w[n].shape))
    shapes = [w[n].shape for n in SMALL]
    res = _adamw("adamw_small", _pack([w[n] for n in SMALL]), _pack(small_g), _pack([m[n] for n in SMALL]),
                 _pack([v[n] for n in SMALL]))
    small_d, small_m, small_v = (_unpack(r, shapes) for r in res)
    for i, n in enumerate(SMALL):
        out_g[n], out_d[n], out_m[n], out_v[n] = small_g[i], small_d[i], small_m[i], small_v[i]

    return (loss, dx.reshape(x.shape), *[out_g[n] for n in WEIGHTS], *[out_d[n] for n in WEIGHTS],
            *[out_m[n] for n in WEIGHTS], *[out_v[n] for n in WEIGHTS])
```

```python
import functools
import math

import jax
import jax.numpy as jnp
from jax import lax
from jax.experimental import pallas as pl
from jax.experimental.pallas import tpu as pltpu

F32 = jnp.float32
BF16 = jnp.bfloat16
MESH = pl.DeviceIdType.MESH
HIGHEST = lax.Precision.HIGHEST

D = 1024
EPS = 1e-6
NEG_INF = -1e30
N_CHIPS = 4

GDN_QK_HEADS = 8
GDN_V_HEADS = 16
GDN_HD = 128
GDN_QK = GDN_QK_HEADS * GDN_HD
GDN_V = GDN_V_HEADS * GDN_HD
GDN_CONV = 2 * GDN_QK + GDN_V
GDN_MAIN = GDN_CONV + GDN_V
GDN_IN = GDN_MAIN + 2 * GDN_V_HEADS
GDN_IN_SHARD = GDN_IN // N_CHIPS
GDN_CHUNK = 64

SWA_Q_HEADS = 16
SWA_KV_HEADS = 4
SWA_GROUP = 4
SWA_HD = 64
SWA_BLOCK = 128
REL_BUCKETS = 32
REL_MAX_DISTANCE = 128

DFF = 2816
DFF2 = 2 * DFF
DFF2_SHARD = DFF2 // N_CHIPS
DFF_SHARD = DFF // N_CHIPS

ADAM_LR = 0.001
ADAM_B1 = 0.9
ADAM_B2 = 0.999
ADAM_EPS = 1e-08
ADAM_WD = 0.01
ADAM_STEP = 10

LANE = 128
SUBLANE = 8
VMEM_LIMIT = 56 * 1024 * 1024


def _params(sem, vmem=VMEM_LIMIT):
    return pltpu.CompilerParams(dimension_semantics=sem, vmem_limit_bytes=vmem)


def _rowcall(name, fn, T, tm, ins, outs):
    n = T // tm
    r8 = tm // SUBLANE
    last8 = T // SUBLANE - 1
    arrays, in_specs = [], []
    for arr, kind, cols in ins:
        arrays.append(arr)
        if kind == "full":
            in_specs.append(pl.BlockSpec(arr.shape, functools.partial(lambda nd, i: (0,) * nd, arr.ndim)))
        elif arr.ndim == 2:
            w, ci = cols if cols is not None else (arr.shape[1], 0)
            if kind == "row":
                in_specs.append(pl.BlockSpec((tm, w), functools.partial(lambda ci, i: (i, ci), ci)))
            elif kind == "prev":
                in_specs.append(pl.BlockSpec(
                    (SUBLANE, w), functools.partial(lambda ci, i: (jnp.maximum(i * r8 - 1, 0), ci), ci)))
            else:
                in_specs.append(pl.BlockSpec(
                    (SUBLANE, w), functools.partial(lambda ci, i: (jnp.minimum((i + 1) * r8, last8), ci), ci)))
        else:
            lead = arr.shape[:-2]
            in_specs.append(pl.BlockSpec(lead + (tm, arr.shape[-1]),
                                         functools.partial(lambda nl, i: (0,) * nl + (i, 0), len(lead))))
    out_shape, out_specs = [], []
    for shape, dtype, kind in outs:
        out_shape.append(jax.ShapeDtypeStruct(shape, dtype))
        if kind == "acc":
            out_specs.append(pl.BlockSpec(shape, functools.partial(lambda nd, i: (0,) * nd, len(shape))))
        else:
            lead = shape[:-2]
            out_specs.append(pl.BlockSpec(lead + (tm, shape[-1]),
                                          functools.partial(lambda nl, i: (0,) * nl + (i, 0), len(lead))))
    nin = len(arrays)

    def body(*refs):
        i = pl.program_id(0)
        vals = [r[...] for r in refs[:nin]]
        res = fn(i, *vals)
        for (shape, dtype, kind), o, r in zip(outs, refs[nin:], res):
            if kind == "row":
                o[...] = r.astype(dtype)
            else:
                @pl.when(i == 0)
                def _():
                    o[...] = r.astype(dtype)

                @pl.when(i > 0)
                def _():
                    o[...] += r.astype(dtype)

    res = pl.pallas_call(
        body, name=name, grid=(n,), in_specs=in_specs, out_specs=out_specs, out_shape=out_shape,
        compiler_params=_params(("arbitrary",)),
    )(*arrays)
    return res


def _mm(name, a, b, out_shape, out_dtype, grid, a_spec, b_spec, o_spec, dims, acc_shape, res=None, precision=None):
    nk = grid[2]

    def body(*refs):
        if res is not None:
            a_ref, b_ref, r_ref, o_ref = refs[:4]
        else:
            a_ref, b_ref, o_ref = refs[:3]
        av, bv = a_ref[...], b_ref[...]
        if precision is None:
            av, bv = av.astype(BF16), bv.astype(BF16)
        p = lax.dot_general(av, bv, (dims, ((), ())), preferred_element_type=F32, precision=precision)

        def finish(x):
            if res is not None:
                x = x + r_ref[...].astype(F32)
            o_ref[...] = x.astype(out_dtype)

        if nk == 1:
            finish(p)
        else:
            acc = refs[-1]
            k = pl.program_id(2)

            @pl.when(k == 0)
            def _():
                acc[...] = p

            @pl.when(k > 0)
            def _():
                acc[...] += p

            @pl.when(k == nk - 1)
            def _():
                finish(acc[...])

    ops = [a, b] + ([res] if res is not None else [])
    specs = [a_spec, b_spec] + ([o_spec] if res is not None else [])
    return pl.pallas_call(
        body, name=name, grid=grid, in_specs=specs, out_specs=o_spec,
        out_shape=jax.ShapeDtypeStruct(out_shape, out_dtype),
        scratch_shapes=[pltpu.VMEM(acc_shape, F32)] if nk > 1 else [],
        compiler_params=_params(("parallel", "parallel", "arbitrary")),
    )(*ops)


NN = ((1,), (0,))
NT = ((1,), (1,))
TN = ((0,), (0,))


def _tile(n, pref):
    for t in pref:
        if n % t == 0:
            return t
    return n


def _mm_nn(name, a, w, out_dtype, res=None, precision=None):
    M, K = a.shape
    N = w.shape[1]
    tm, tn = _tile(M, (512, 256, 128)), _tile(N, (512, 256, 128))
    return _mm(name, a, w, (M, N), out_dtype, (M // tm, N // tn, 1),
               pl.BlockSpec((tm, K), lambda i, j, k: (i, 0)), pl.BlockSpec((K, tn), lambda i, j, k: (0, j)),
               pl.BlockSpec((tm, tn), lambda i, j, k: (i, j)), NN, (tm, tn), res=res, precision=precision)


def _mm_nt(name, g, w, out_dtype, res=None, precision=None):
    M, N = g.shape
    K = w.shape[0]
    tm, tk = _tile(M, (512, 256, 128)), _tile(K, (512, 1408, 256, 128))
    tn = _tile(N, (1536, 1024, 512, 256, 128))
    return _mm(name, g, w, (M, K), out_dtype, (M // tm, K // tk, N // tn),
               pl.BlockSpec((tm, tn), lambda i, j, k: (i, k)), pl.BlockSpec((tk, tn), lambda i, j, k: (j, k)),
               pl.BlockSpec((tm, tk), lambda i, j, k: (i, j)), NT, (tm, tk), res=res, precision=precision)


def _mm_tn(name, a, g, out_dtype=F32, precision=None):
    T, K = a.shape
    N = g.shape[1]
    tk, tn = _tile(K, (512, 1408, 256, 128)), _tile(N, (512, 256, 128))
    tt = _tile(T, (1024, 512, 256, 128))
    return _mm(name, a, g, (K, N), out_dtype, (K // tk, N // tn, T // tt),
               pl.BlockSpec((tt, tk), lambda i, j, k: (k, i)), pl.BlockSpec((tt, tn), lambda i, j, k: (k, j)),
               pl.BlockSpec((tk, tn), lambda i, j, k: (i, j)), TN, (tk, tn), precision=precision)


def _mm_up(name, n, wup, layer):
    T = n.shape[0]
    tm = _tile(T, (512, 256, 128))
    return _mm(name, n, wup, (T, DFF2), BF16, (T // tm, N_CHIPS, 1),
               pl.BlockSpec((tm, D), lambda i, j, k: (i, 0)),
               pl.BlockSpec((None, None, D, DFF2_SHARD), lambda i, j, k: (j, layer, 0, 0)),
               pl.BlockSpec((tm, DFF2_SHARD), lambda i, j, k: (i, j)), NN, (tm, DFF2_SHARD))


def _mm_up_nt(name, du, wup, layer):
    T = du.shape[0]
    tm, tk = _tile(T, (512, 256, 128)), 512
    return _mm(name, du, wup, (T, D), F32, (T // tm, D // tk, N_CHIPS),
               pl.BlockSpec((tm, DFF2_SHARD), lambda i, j, k: (i, k)),
               pl.BlockSpec((None, None, tk, DFF2_SHARD), lambda i, j, k: (k, layer, j, 0)),
               pl.BlockSpec((tm, tk), lambda i, j, k: (i, j)), NT, (tm, tk))


def _mm_up_tn(name, n, du):
    T = n.shape[0]
    tk, tt = 512, _tile(T, (1024, 512, 256, 128))
    return _mm(name, n, du, (N_CHIPS, D, DFF2_SHARD), F32, (D // tk, N_CHIPS, T // tt),
               pl.BlockSpec((tt, tk), lambda i, j, k: (k, i)), pl.BlockSpec((tt, DFF2_SHARD), lambda i, j, k: (k, j)),
               pl.BlockSpec((None, tk, DFF2_SHARD), lambda i, j, k: (j, i, 0)), TN, (tk, DFF2_SHARD))


def _sigmoid(x):
    return 1.0 / (1.0 + jnp.exp(-x))


def _silu(x):
    return x * _sigmoid(x)


def _softplus(x):
    return jnp.maximum(x, 0.0) + jnp.log(1.0 + jnp.exp(-jnp.abs(x)))


def _rms_core(h, w):
    return h * lax.rsqrt(jnp.mean(h * h, axis=-1, keepdims=True) + EPS) * w


def _shift_down(x, halo, s, i):
    if s == 0:
        return x
    tm = x.shape[0]
    rolled = pltpu.roll(x, s, 0)
    patch = pltpu.roll(jnp.where(i == 0, 0.0, halo), s, 0)
    row = lax.broadcasted_iota(jnp.int32, patch.shape, 0)
    top = jnp.where(row < s, patch, rolled[:SUBLANE])
    return jnp.concatenate([top, rolled[SUBLANE:]], axis=0) if tm > SUBLANE else top


def _shift_up(x, halo, s, i, n):
    if s == 0:
        return x
    tm = x.shape[0]
    rolled = pltpu.roll(x, tm - s, 0)
    patch = pltpu.roll(jnp.where(i == n - 1, 0.0, halo), SUBLANE - s, 0)
    row = lax.broadcasted_iota(jnp.int32, patch.shape, 0)
    bottom = jnp.where(row >= SUBLANE - s, patch, rolled[tm - SUBLANE:])
    return jnp.concatenate([rolled[:tm - SUBLANE], bottom], axis=0) if tm > SUBLANE else bottom


def _conv_fwd(x, halo, w, i):
    K = w.shape[0]
    y = w[K - 1:K, :] * x
    for j in range(K - 1):
        y = y + w[j:j + 1, :] * _shift_down(x, halo, K - 1 - j, i)
    return y


def _conv_dx(dy, halo_next, w, i, n):
    K = w.shape[0]
    dx = w[K - 1:K, :] * dy
    for j in range(K - 1):
        dx = dx + w[j:j + 1, :] * _shift_up(dy, halo_next, K - 1 - j, i, n)
    return dx


def _conv_dw(dy, x, halo, K, i):
    rows = [jnp.sum(dy * _shift_down(x, halo, K - 1 - j, i), axis=0, keepdims=True) for j in range(K)]
    return jnp.concatenate(rows + [jnp.zeros((SUBLANE - K, dy.shape[1]), F32)], axis=0)


def _rms_fwd(name, h, w, tm=512):
    T = h.shape[0]
    tm = min(tm, T)

    def fn(i, hv, wv):
        return (_rms_core(hv, wv),)

    return _rowcall(name, fn, T, tm, [(h, "row", None), (w, "full", None)], [((T, D), BF16, "row")])[0]


def _rms_bwd(name, h, pairs, adds, tm=256):
    T = h.shape[0]
    tm = min(tm, T)
    npair, nadd = len(pairs), len(adds)

    def fn(i, hv, *rest):
        ws, dns, ads = rest[:npair], rest[npair:2 * npair], rest[2 * npair:]
        dh = None
        dws = []
        for wv, dn in zip(ws, dns):
            _, vjp = jax.vjp(_rms_core, hv, wv)
            dhi, dwi = vjp(dn.astype(F32))
            dh = dhi if dh is None else dh + dhi
            dws.append(dwi)
        for a in ads:
            dh = dh + a.astype(F32)
        return (dh, *dws)

    ins = [(h, "row", None)] + [(w, "full", None) for w, _ in pairs] + [(dn, "row", None) for _, dn in pairs]
    ins += [(a, "row", None) for a in adds]
    outs = [((T, D), F32, "row")] + [((1, D), F32, "acc")] * npair
    return _rowcall(name, fn, T, tm, ins, outs)


def _l2(x):
    return x * lax.rsqrt(jnp.sum(x * x, axis=-1, keepdims=True) + EPS)


def _gdn_post_core(yq, yk, yv, pb, pa, a_log, dtb):
    qn = tuple(_l2(_silu(a)) * (GDN_HD ** -0.5) for a in yq)
    kn = tuple(_l2(_silu(a)) for a in yk)
    v = _silu(yv)
    beta = _sigmoid(pb)
    g = -jnp.exp(a_log) * _softplus(pa + dtb)
    return qn, kn, v, beta, g


def _heads(x, n):
    return tuple(x[:, GDN_HD * h:GDN_HD * (h + 1)] for h in range(n))


def _gdn_pre_fwd(pm, pba, conv_w, a_log, dtb, tm=128):
    T = pm.shape[0]
    tm = min(tm, T)

    def fn(i, x, halo, pbav, cw, al, db):
        y = _conv_fwd(x.astype(F32), halo.astype(F32), cw, i)
        qn, kn, v, beta, g = _gdn_post_core(_heads(y[:, :GDN_QK], 8), _heads(y[:, GDN_QK:2 * GDN_QK], 8),
                                            y[:, 2 * GDN_QK:], pbav[:, :LANE], pbav[:, LANE:], al, db)
        return jnp.stack(qn), jnp.stack(kn), jnp.stack(_heads(v, GDN_V_HEADS)), beta, g

    ins = [(pm, "row", (GDN_CONV, 0)), (pm, "prev", (GDN_CONV, 0)), (pba, "row", None),
           (conv_w, "full", None), (a_log, "full", None), (dtb, "full", None)]
    outs = [((GDN_QK_HEADS, T, GDN_HD), BF16, "row"), ((GDN_QK_HEADS, T, GDN_HD), BF16, "row"),
            ((GDN_V_HEADS, T, GDN_HD), BF16, "row"), ((T, LANE), F32, "row"), ((T, LANE), F32, "row")]
    return _rowcall("gdn_pre_fwd", fn, T, tm, ins, outs)


def _gdn_pre_bwd(pm, pba, conv_w, a_log, dtb, dqn, dkn, dv, dbeta, dg, tm=128):
    T = pm.shape[0]
    tm = min(tm, T)

    def fn(i, x, halo, pbav, cw, al, db, dqv, dkv, dvv, dbv, dgv):
        xf, hf = x.astype(F32), halo.astype(F32)
        y = _conv_fwd(xf, hf, cw, i)
        prim = (_heads(y[:, :GDN_QK], 8), _heads(y[:, GDN_QK:2 * GDN_QK], 8), y[:, 2 * GDN_QK:],
                pbav[:, :LANE], pbav[:, LANE:], al, db)
        _, vjp = jax.vjp(_gdn_post_core, *prim)
        cot = (tuple(dqv[h].astype(F32) for h in range(8)), tuple(dkv[h].astype(F32) for h in range(8)),
               jnp.concatenate([dvv[h].astype(F32) for h in range(GDN_V_HEADS)], axis=1), dbv, dgv)
        dyq, dyk, dyv, dpb, dpa, dal, ddb = vjp(cot)
        dy = jnp.concatenate(list(dyq) + list(dyk) + [dyv], axis=1)
        dcw = _conv_dw(dy, xf, hf, 4, i)
        return dy, jnp.concatenate([dpb, dpa], axis=1), dcw, dal, ddb

    ins = [(pm, "row", (GDN_CONV, 0)), (pm, "prev", (GDN_CONV, 0)), (pba, "row", None),
           (conv_w, "full", None), (a_log, "full", None), (dtb, "full", None),
           (dqn, "row", None), (dkn, "row", None), (dv, "row", None), (dbeta, "row", None), (dg, "row", None)]
    outs = [((T, GDN_CONV), BF16, "row"), ((T, 2 * LANE), F32, "row"), ((SUBLANE, GDN_CONV), F32, "acc"),
            ((1, LANE), F32, "acc"), ((1, LANE), F32, "acc")]
    return _rowcall("gdn_pre_bwd", fn, T, tm, ins, outs)


def _gdn_conv_bwd(dy, dz, conv_w, tm=256):
    T = dy.shape[0]
    tm = min(tm, T)
    n = T // tm

    def fn(i, dyv, halo, dzv, cw):
        dx = _conv_dx(dyv.astype(F32), halo.astype(F32), cw, i, n)
        return (jnp.concatenate([dx.astype(BF16), dzv.astype(BF16)], axis=1),)

    ins = [(dy, "row", None), (dy, "next", None), (dz, "row", None), (conv_w, "full", None)]
    return _rowcall("gdn_conv_bwd", fn, T, tm, ins, [((T, GDN_MAIN), BF16, "row")])[0]


def _bdot(a, b, dims=NN):
    return lax.dot_general(a.astype(BF16), b.astype(BF16), (dims, ((), ())), preferred_element_type=F32)


BNN = ((2,), (1,))
BNT = ((2,), (2,))
BTN = ((1,), (1,))


def _bmm(a, b, dims=BNN):
    return lax.dot_general(a.astype(BF16), b.astype(BF16), (dims, ((0,), (0,))), preferred_element_type=F32)


def _bmm3(a, b):
    ah, bh = a.astype(BF16), b.astype(BF16)
    al, bl = (a - ah.astype(F32)).astype(BF16), (b - bh.astype(F32)).astype(BF16)
    dn = (BNN, ((0,), (0,)))
    return (lax.dot_general(ah, bh, dn, preferred_element_type=F32)
            + lax.dot_general(al, bh, dn, preferred_element_type=F32)
            + lax.dot_general(ah, bl, dn, preferred_element_type=F32))


@jax.custom_vjp
def _tri_inv(m):
    C = m.shape[-1]
    r = lax.broadcasted_iota(jnp.int32, (C, C), 0)
    c = lax.broadcasted_iota(jnp.int32, (C, C), 1)
    t = jnp.where(r == c, 1.0, 0.0) - m
    pw = _bmm3(m, m)
    t = t + _bmm3(t, pw)
    for _ in range(int(math.log2(C)) - 2):
        pw = _bmm(pw, pw)
        t = t + _bmm(t, pw)
    return t


def _tri_inv_fwd(m):
    t = _tri_inv(m)
    return t, t


def _tri_inv_bwd(t, dt):
    tt = jnp.swapaxes(t, 1, 2)
    return (-_bmm(_bmm(tt, dt), tt),)


_tri_inv.defvjp(_tri_inv_fwd, _tri_inv_bwd)


def _twice(a):
    return jnp.broadcast_to(a[:, None], (a.shape[0], 2) + a.shape[1:]).reshape((2 * a.shape[0],) + a.shape[1:])


def _gdn_chunk(q, k, v, grow, brow, S):
    C = q.shape[1]
    r = lax.broadcasted_iota(jnp.int32, (C, C), 0)
    c = lax.broadcasted_iota(jnp.int32, (C, C), 1)
    tril, strict, eye = r >= c, r > c, r == c
    gcol = jnp.sum(jnp.where(eye, grow, 0.0), axis=2, keepdims=True)
    bcol = jnp.sum(jnp.where(eye, brow, 0.0), axis=2, keepdims=True)
    gc_col = jnp.sum(jnp.where(tril, grow, 0.0), axis=2, keepdims=True)
    gc_row = jnp.sum(jnp.where(r <= c, gcol, 0.0), axis=1, keepdims=True)
    gc_last = jnp.sum(grow, axis=2, keepdims=True)
    decay = jnp.where(tril, jnp.exp(jnp.where(tril, gc_col - gc_row, 0.0)), 0.0)
    kk = _twice(_bmm(k, k, BNT))
    qk = _twice(_bmm(q, k, BNT))
    t_mat = _tri_inv(jnp.where(strict, bcol * kk * decay, 0.0))
    k2, q2 = _twice(k), _twice(q)
    egc = jnp.exp(gc_col)
    u = _bmm(t_mat, v * bcol)
    w = _bmm(t_mat, k2 * (bcol * egc))
    v_new = u - _bmm(w, S)
    o = _bmm(q2 * egc, S) + _bmm(qk * decay, v_new)
    s_new = S * jnp.exp(gc_last) + _bmm(k2 * jnp.exp(gc_last - gc_col), v_new, BTN)
    return o, s_new


def _gdn_tb(T):
    return min(256, T)


def _gate_rows(g):
    T = g.shape[0]
    g = g[:, :GDN_V_HEADS].reshape(T // GDN_CHUNK, GDN_CHUNK, GDN_V_HEADS)
    return g.transpose(0, 2, 1)[:, :, None, :]


def _gate_cols(g):
    nc = g.shape[0]
    g = g[:, :, 0, :].transpose(0, 2, 1).reshape(nc * GDN_CHUNK, GDN_V_HEADS)
    return jnp.pad(g, ((0, 0), (0, LANE - GDN_V_HEADS)))


def _gdn_fwd(qn, kn, v, g, beta):
    T = qn.shape[1]
    tb = _gdn_tb(T)
    nc = tb // GDN_CHUNK

    def body(q_ref, k_ref, v_ref, g_ref, b_ref, o_ref, sall_ref, s_scr):
        @pl.when(pl.program_id(0) == 0)
        def _():
            s_scr[...] = jnp.zeros(s_scr.shape, F32)

        def chunk(ci, carry):
            rows = pl.ds(pl.multiple_of(ci * GDN_CHUNK, GDN_CHUNK), GDN_CHUNK)
            s = s_scr[...]
            sall_ref[ci] = s
            o, s_new = _gdn_chunk(q_ref[:, rows, :].astype(F32), k_ref[:, rows, :].astype(F32),
                                  v_ref[:, rows, :].astype(F32), g_ref[ci], b_ref[ci], s)
            o_ref[:, rows, :] = o.astype(o_ref.dtype)
            s_scr[...] = s_new
            return carry

        lax.fori_loop(0, nc, chunk, 0)

    qk_spec = pl.BlockSpec((GDN_QK_HEADS, tb, GDN_HD), lambda i: (0, i, 0))
    v_spec = pl.BlockSpec((GDN_V_HEADS, tb, GDN_HD), lambda i: (0, i, 0))
    g_spec = pl.BlockSpec((nc, GDN_V_HEADS, 1, GDN_CHUNK), lambda i: (i, 0, 0, 0))
    return pl.pallas_call(
        body, name="gdn_fwd", grid=(T // tb,),
        in_specs=[qk_spec, qk_spec, v_spec, g_spec, g_spec],
        out_specs=[v_spec, pl.BlockSpec((nc, GDN_V_HEADS, GDN_HD, GDN_HD), lambda i: (i, 0, 0, 0))],
        out_shape=[jax.ShapeDtypeStruct((GDN_V_HEADS, T, GDN_HD), BF16),
                   jax.ShapeDtypeStruct((T // GDN_CHUNK, GDN_V_HEADS, GDN_HD, GDN_HD), F32)],
        scratch_shapes=[pltpu.VMEM((GDN_V_HEADS, GDN_HD, GDN_HD), F32)],
        compiler_params=_params(("arbitrary",)),
    )(qn, kn, v, g, beta)


def _gdn_bwd(qn, kn, v, g, beta, sall, do):
    T = qn.shape[1]
    tb = _gdn_tb(T)
    nc = tb // GDN_CHUNK
    nb = T // tb

    def body(q_ref, k_ref, v_ref, g_ref, b_ref, sall_ref, do_ref, dq_ref, dk_ref, dv_ref, dg_ref, db_ref, ds_scr):
        @pl.when(pl.program_id(0) == 0)
        def _():
            ds_scr[...] = jnp.zeros(ds_scr.shape, F32)

        def chunk(cr, carry):
            ci = nc - 1 - cr
            rows = pl.ds(pl.multiple_of(ci * GDN_CHUNK, GDN_CHUNK), GDN_CHUNK)
            _, vjp = jax.vjp(_gdn_chunk, q_ref[:, rows, :].astype(F32), k_ref[:, rows, :].astype(F32),
                             v_ref[:, rows, :].astype(F32), g_ref[ci], b_ref[ci], sall_ref[ci])
            dq, dk, dv, dg, db, ds = vjp((do_ref[:, rows, :].astype(F32), ds_scr[...]))
            ds_scr[...] = ds
            dq_ref[:, rows, :] = dq
            dk_ref[:, rows, :] = dk
            dv_ref[:, rows, :] = dv
            dg_ref[ci] = dg
            db_ref[ci] = db
            return carry

        lax.fori_loop(0, nc, chunk, 0)

    qk_spec = pl.BlockSpec((GDN_QK_HEADS, tb, GDN_HD), lambda i: (0, nb - 1 - i, 0))
    v_spec = pl.BlockSpec((GDN_V_HEADS, tb, GDN_HD), lambda i: (0, nb - 1 - i, 0))
    g_spec = pl.BlockSpec((nc, GDN_V_HEADS, 1, GDN_CHUNK), lambda i: (nb - 1 - i, 0, 0, 0))
    s_spec = pl.BlockSpec((nc, GDN_V_HEADS, GDN_HD, GDN_HD), lambda i: (nb - 1 - i, 0, 0, 0))
    return pl.pallas_call(
        body, name="gdn_bwd", grid=(nb,),
        in_specs=[qk_spec, qk_spec, v_spec, g_spec, g_spec, s_spec, v_spec],
        out_specs=[qk_spec, qk_spec, v_spec, g_spec, g_spec],
        out_shape=[jax.ShapeDtypeStruct((GDN_QK_HEADS, T, GDN_HD), F32),
                   jax.ShapeDtypeStruct((GDN_QK_HEADS, T, GDN_HD), F32),
                   jax.ShapeDtypeStruct((GDN_V_HEADS, T, GDN_HD), F32),
                   jax.ShapeDtypeStruct(g.shape, F32), jax.ShapeDtypeStruct(g.shape, F32)],
        scratch_shapes=[pltpu.VMEM((GDN_V_HEADS, GDN_HD, GDN_HD), F32)],
        compiler_params=_params(("arbitrary",)),
    )(qn, kn, v, g, beta, sall, do)


def _gnorm_core(o, z, w):
    return tuple(_rms_core(oh, w) * _silu(zh) for oh, zh in zip(o, z))


def _gnorm_fwd(o, pm, w, tm=256):
    T = pm.shape[0]
    tm = min(tm, T)

    def fn(i, ov, zv, wv):
        zf = zv.astype(F32)
        out = _gnorm_core(tuple(ov[h].astype(F32) for h in range(GDN_V_HEADS)), _heads(zf, GDN_V_HEADS), wv)
        return (jnp.concatenate(out, axis=1),)

    ins = [(o, "row", None), (pm, "row", (GDN_V, 2)), (w, "full", None)]
    return _rowcall("gnorm_fwd", fn, T, tm, ins, [((T, GDN_V), BF16, "row")])[0]


def _gnorm_bwd(o, pm, w, don, tm=128):
    T = pm.shape[0]
    tm = min(tm, T)

    def fn(i, ov, zv, wv, dv):
        zf, df = zv.astype(F32), dv.astype(F32)
        _, vjp = jax.vjp(_gnorm_core, tuple(ov[h].astype(F32) for h in range(GDN_V_HEADS)),
                         _heads(zf, GDN_V_HEADS), wv)
        do, dz, dw = vjp(_heads(df, GDN_V_HEADS))
        return jnp.stack(do), jnp.concatenate(dz, axis=1), dw

    ins = [(o, "row", None), (pm, "row", (GDN_V, 2)), (w, "full", None), (don, "row", None)]
    outs = [((GDN_V_HEADS, T, GDN_HD), BF16, "row"), ((T, GDN_V), BF16, "row"), ((1, GDN_HD), F32, "acc")]
    return _rowcall("gnorm_bwd", fn, T, tm, ins, outs)


def _ffn_act_fwd(name, up, conv_w, conv_b, tm=128):
    T = up.shape[0]
    tm = min(tm, T)

    def fn(i, x, halo, cw, cb):
        u = _conv_fwd(x.astype(F32), halo.astype(F32), cw, i) + cb
        return (_silu(u[:, :DFF]) * u[:, DFF:],)

    ins = [(up, "row", None), (up, "prev", None), (conv_w, "full", None), (conv_b, "full", None)]
    return _rowcall(name, fn, T, tm, ins, [((T, DFF), BF16, "row")])[0]


def _ffn_act_bwd(name, up, conv_w, conv_b, dact, tm=128):
    T = up.shape[0]
    tm = min(tm, T)

    def fn(i, x, halo, cw, cb, da):
        xf, hf, da = x.astype(F32), halo.astype(F32), da.astype(F32)
        u = _conv_fwd(xf, hf, cw, i) + cb
        gate, val = u[:, :DFF], u[:, DFF:]
        sg = _sigmoid(gate)
        dgate = da * val * sg * (1.0 + gate * (1.0 - sg))
        dval = da * gate * sg
        du = jnp.concatenate([dgate, dval], axis=1)
        return du, _conv_dw(du, xf, hf, 3, i), jnp.sum(du, axis=0, keepdims=True)

    ins = [(up, "row", None), (up, "prev", None), (conv_w, "full", None), (conv_b, "full", None),
           (dact, "row", None)]
    outs = [((T, DFF2), BF16, "row"), ((SUBLANE, DFF2), F32, "acc"), ((1, DFF2), F32, "acc")]
    return _rowcall(name, fn, T, tm, ins, outs)


def _ffn_conv_bwd(name, du, conv_w, tm=256):
    T = du.shape[0]
    tm = min(tm, T)
    n = T // tm

    def fn(i, dv, halo, cw):
        return (_conv_dx(dv.astype(F32), halo.astype(F32), cw, i, n),)

    ins = [(du, "row", None), (du, "next", None), (conv_w, "full", None)]
    return _rowcall(name, fn, T, tm, ins, [((T, DFF2), BF16, "row")])[0]


def _attn_core(qs, kp, kc, vp, vc, biases, sinks, mask):
    kcat = jnp.concatenate([kp, kc], axis=0)
    vcat = jnp.concatenate([vp, vc], axis=0)
    outs = []
    for q, bias, sink in zip(qs, biases, sinks):
        s = _bdot(q * (SWA_HD ** -0.5), kcat, NT) + bias
        s = jnp.where(mask, s, NEG_INF)
        m = lax.stop_gradient(jnp.maximum(jnp.max(s, axis=-1, keepdims=True), sink))
        p = jnp.exp(s - m)
        denom = jnp.sum(p, axis=-1, keepdims=True) + jnp.exp(sink - m)
        outs.append(_bdot(p / denom, vcat))
    return tuple(outs)


def _attn_mask(i):
    qi = lax.broadcasted_iota(jnp.int32, (SWA_BLOCK, 2 * SWA_BLOCK), 0)
    ki = lax.broadcasted_iota(jnp.int32, (SWA_BLOCK, 2 * SWA_BLOCK), 1)
    dist = qi + SWA_BLOCK - ki
    return (dist >= 0) & (dist < SWA_BLOCK) & ((ki >= SWA_BLOCK) | (i > 0))


def _attn_fwd(q, k, v, bias, sinks):
    T = q.shape[1]
    nb = T // SWA_BLOCK

    def body(q_ref, kc_ref, kp_ref, vc_ref, vp_ref, b_ref, s_ref, o_ref):
        i = pl.program_id(1)
        outs = _attn_core(tuple(q_ref[g].astype(F32) for g in range(SWA_GROUP)),
                          kp_ref[...].astype(F32), kc_ref[...].astype(F32),
                          vp_ref[...].astype(F32), vc_ref[...].astype(F32),
                          tuple(b_ref[g] for g in range(SWA_GROUP)),
                          tuple(s_ref[g:g + 1, 0:1] for g in range(SWA_GROUP)), _attn_mask(i))
        for g in range(SWA_GROUP):
            o_ref[g] = outs[g].astype(o_ref.dtype)

    q_spec = pl.BlockSpec((SWA_GROUP, SWA_BLOCK, SWA_HD), lambda j, i: (j, i, 0))
    cur = pl.BlockSpec((None, SWA_BLOCK, SWA_HD), lambda j, i: (j, i, 0))
    prev = pl.BlockSpec((None, SWA_BLOCK, SWA_HD), lambda j, i: (j, jnp.maximum(i - 1, 0), 0))
    return pl.pallas_call(
        body, name="attn_fwd", grid=(SWA_KV_HEADS, nb),
        in_specs=[q_spec, cur, prev, cur, prev,
                  pl.BlockSpec((SWA_GROUP, SWA_BLOCK, 2 * SWA_BLOCK), lambda j, i: (j, 0, 0)),
                  pl.BlockSpec((None, SWA_GROUP, LANE), lambda j, i: (j, 0, 0))],
        out_specs=q_spec, out_shape=jax.ShapeDtypeStruct(q.shape, BF16),
        compiler_params=_params(("parallel", "arbitrary")),
    )(q, k, k, v, v, bias, sinks)


def _attn_bwd(q, k, v, bias, sinks, do):
    T = q.shape[1]
    nb = T // SWA_BLOCK

    def body(q_ref, kc_ref, kp_ref, vc_ref, vp_ref, b_ref, s_ref, do_ref,
             dq_ref, dk_ref, dv_ref, db_ref, dsk_ref, kcar, vcar):
        i = pl.program_id(1)

        @pl.when(i < nb)
        def _():
            prim = (tuple(q_ref[g].astype(F32) for g in range(SWA_GROUP)),
                    kp_ref[...].astype(F32), kc_ref[...].astype(F32),
                    vp_ref[...].astype(F32), vc_ref[...].astype(F32),
                    tuple(b_ref[g] for g in range(SWA_GROUP)),
                    tuple(s_ref[g:g + 1, 0:1] for g in range(SWA_GROUP)))
            _, vjp = jax.vjp(functools.partial(_attn_core, mask=_attn_mask(i)), *prim)
            dqs, dkp, dkc, dvp, dvc, dbs, dss = vjp(tuple(do_ref[g].astype(F32) for g in range(SWA_GROUP)))
            for g in range(SWA_GROUP):
                dq_ref[g] = dqs[g].astype(dq_ref.dtype)
            dsk = jnp.concatenate([jnp.broadcast_to(d, (1, LANE)) for d in dss], axis=0)

            @pl.when(i == 0)
            def _():
                for g in range(SWA_GROUP):
                    db_ref[g] = dbs[g]
                dsk_ref[...] = dsk

            @pl.when(i > 0)
            def _():
                for g in range(SWA_GROUP):
                    db_ref[g] += dbs[g]
                dsk_ref[...] += dsk
                dk_ref[...] = (kcar[...] + dkp).astype(dk_ref.dtype)
                dv_ref[...] = (vcar[...] + dvp).astype(dv_ref.dtype)

            kcar[...] = dkc
            vcar[...] = dvc

        @pl.when(i == nb)
        def _():
            dk_ref[...] = kcar[...].astype(dk_ref.dtype)
            dv_ref[...] = vcar[...].astype(dv_ref.dtype)

    last = nb - 1
    q_spec = pl.BlockSpec((SWA_GROUP, SWA_BLOCK, SWA_HD), lambda j, i: (j, jnp.minimum(i, last), 0))
    cur = pl.BlockSpec((None, SWA_BLOCK, SWA_HD), lambda j, i: (j, jnp.minimum(i, last), 0))
    prev = pl.BlockSpec((None, SWA_BLOCK, SWA_HD), lambda j, i: (j, jnp.clip(i - 1, 0, last), 0))
    b_spec = pl.BlockSpec((SWA_GROUP, SWA_BLOCK, 2 * SWA_BLOCK), lambda j, i: (j, 0, 0))
    s_spec = pl.BlockSpec((None, SWA_GROUP, LANE), lambda j, i: (j, 0, 0))
    return pl.pallas_call(
        body, name="attn_bwd", grid=(SWA_KV_HEADS, nb + 1),
        in_specs=[q_spec, cur, prev, cur, prev, b_spec, s_spec, q_spec],
        out_specs=[q_spec, prev, prev, b_spec, s_spec],
        out_shape=[jax.ShapeDtypeStruct(q.shape, BF16), jax.ShapeDtypeStruct(k.shape, BF16),
                   jax.ShapeDtypeStruct(k.shape, BF16), jax.ShapeDtypeStruct(bias.shape, F32),
                   jax.ShapeDtypeStruct(sinks.shape, F32)],
        scratch_shapes=[pltpu.VMEM((SWA_BLOCK, SWA_HD), F32), pltpu.VMEM((SWA_BLOCK, SWA_HD), F32)],
        compiler_params=_params(("parallel", "arbitrary")),
    )(q, k, k, v, v, bias, sinks, do)


def _rel_onehot():
    qi = jnp.arange(SWA_BLOCK)[:, None]
    ki = jnp.arange(2 * SWA_BLOCK)[None, :]
    n = jnp.maximum(qi + SWA_BLOCK - ki, 0)
    max_exact = REL_BUCKETS // 2
    nf = jnp.maximum(n, 1).astype(F32)
    large = max_exact + (jnp.log(nf / max_exact) / math.log(REL_MAX_DISTANCE / max_exact)
                         * (REL_BUCKETS - max_exact)).astype(jnp.int32)
    bucket = jnp.where(n < max_exact, n, jnp.minimum(large, REL_BUCKETS - 1)).reshape(-1)
    return (bucket[None, :] == jnp.arange(REL_BUCKETS)[:, None]).astype(F32)


def _final(h, w, target, tm=256):
    T = h.shape[0]
    tm = min(tm, T)

    def fn(i, hv, wv, tv):
        y, vjp = jax.vjp(_rms_core, hv, wv)
        err = y - tv
        dh, dw = vjp(err * (1.0 / D))
        part = 0.5 * jnp.sum(jnp.sum(err * err, axis=1, keepdims=True) * (1.0 / D), axis=0, keepdims=True)
        return jnp.broadcast_to(part, (SUBLANE, LANE)), dh, dw

    ins = [(h, "row", None), (w, "full", None), (target, "row", None)]
    outs = [((SUBLANE, LANE), F32, "acc"), ((T, D), F32, "row"), ((1, D), F32, "acc")]
    return _rowcall("final", fn, T, tm, ins, outs)


def _heads_major(a, heads, hd):
    return a.reshape(a.shape[0], heads, hd).transpose(1, 0, 2)


def _heads_minor(a):
    return a.transpose(1, 0, 2).reshape(a.shape[1], a.shape[0] * a.shape[2])


def _ffn_fwd(tag, h, P, layer):
    n = _rms_fwd(f"{tag}_rms", h, P["ffn_norm_w"][layer:layer + 1])
    up = _mm_up(f"{tag}_up", n, P["w_up"], layer)
    act = _ffn_act_fwd(f"{tag}_act", up, P["ffn_conv_w"][layer], P["ffn_conv_b"][layer:layer + 1])
    out = _mm_nn(f"{tag}_down", act, P["w_down"][layer], F32, res=h)
    return out, (n, up, act)


def _ffn_bwd(tag, h, saved, dout, P, layer):
    n, up, act = saved
    cw, cb = P["ffn_conv_w"][layer], P["ffn_conv_b"][layer:layer + 1]
    dact = _mm_nt(f"{tag}_down_dx", dout, P["w_down"][layer], BF16)
    g_down = _mm_tn(f"{tag}_down_dw", act, dout)
    du, dcw, dcb = _ffn_act_bwd(f"{tag}_act_bwd", up, cw, cb, dact)
    dup = _ffn_conv_bwd(f"{tag}_conv_bwd", du, cw)
    g_up = _mm_up_tn(f"{tag}_up_dw", n, dup)
    dn = _mm_up_nt(f"{tag}_up_dx", dup, P["w_up"], layer)
    dh, dnw = _rms_bwd(f"{tag}_rms_bwd", h, [(P["ffn_norm_w"][layer:layer + 1], dn)], [dout])
    return dh, dict(w_down=g_down, w_up=g_up, conv_w=dcw[:3], conv_b=dcb, norm_w=dnw)


def _local_step(x, target, P):
    T = x.shape[0]
    n0 = _rms_fwd("a_rms", x, P["a_norm_w"])
    pm = _mm_nn("gdn_in", n0, P["w_in_main"], BF16)
    pba = _mm_nn("gdn_in_ba", n0, P["w_in_ba"], F32)
    qn, kn, v, beta, g = _gdn_pre_fwd(pm, pba, P["a_conv_w"], P["a_log"], P["dt_bias"])
    g_rows, beta_rows = _gate_rows(g), _gate_rows(beta)
    o, sall = _gdn_fwd(qn, kn, v, g_rows, beta_rows)
    on = _gnorm_fwd(o, pm, P["a_out_norm_w"])
    h1 = _mm_nn("gdn_out", on, P["w_out"], F32, res=x)
    h2, ffn0 = _ffn_fwd("ffn0", h1, P, 0)
    nkv = _rms_fwd("kv_rms", h2, P["kv_norm_w"])
    kv = _mm_nn("kv_proj", nkv, P["w_kv"], BF16)
    nb = _rms_fwd("b_rms", h2, P["b_norm_w"])
    qp = _mm_nn("q_proj", nb, P["w_q"], BF16)
    q3 = _heads_major(qp, SWA_Q_HEADS, SWA_HD)
    k3 = _heads_major(kv[:, :SWA_KV_HEADS * SWA_HD], SWA_KV_HEADS, SWA_HD)
    v3 = _heads_major(kv[:, SWA_KV_HEADS * SWA_HD:], SWA_KV_HEADS, SWA_HD)
    onehot = _rel_onehot()
    bias = _mm_nn("rel_bias", P["rel_table_t"], onehot, F32, precision=HIGHEST)
    bias = bias.reshape(SWA_Q_HEADS, SWA_BLOCK, 2 * SWA_BLOCK)
    oa = _heads_minor(_attn_fwd(q3, k3, v3, bias, P["sinks"]))
    h3 = _mm_nn("o_proj", oa, P["w_o"], F32, res=h2)
    h4, ffn1 = _ffn_fwd("ffn1", h3, P, 1)
    loss, dh4, d_final = _final(h4, P["final_norm_w"], target)

    dh3, gf1 = _ffn_bwd("ffn1", h3, ffn1, dh4, P, 1)
    doa = _mm_nt("o_proj_dx", dh3, P["w_o"], BF16)
    g_wo = _mm_tn("o_proj_dw", oa, dh3)
    dq3, dk3, dv3, dbias, dsinks = _attn_bwd(q3, k3, v3, bias, P["sinks"], _heads_major(doa, SWA_Q_HEADS, SWA_HD))
    dqp = _heads_minor(dq3)
    dkv = jnp.concatenate([_heads_minor(dk3), _heads_minor(dv3)], axis=1)
    g_wq = _mm_tn("q_proj_dw", nb, dqp)
    dnb = _mm_nt("q_proj_dx", dqp, P["w_q"], F32)
    g_wkv = _mm_tn("kv_proj_dw", nkv, dkv)
    dnkv = _mm_nt("kv_proj_dx", dkv, P["w_kv"], F32)
    dh2, d_bnorm, d_kvnorm = _rms_bwd("b_kv_rms_bwd", h2, [(P["b_norm_w"], dnb), (P["kv_norm_w"], dnkv)], [dh3])
    g_table = _mm_nt("rel_bias_dw", onehot, dbias.reshape(SWA_Q_HEADS, -1), F32, precision=HIGHEST)
    dh1, gf0 = _ffn_bwd("ffn0", h1, ffn0, dh2, P, 0)
    don = _mm_nt("gdn_out_dx", dh1, P["w_out"], BF16)
    g_wout = _mm_tn("gdn_out_dw", on, dh1)
    do, dz, d_gnorm = _gnorm_bwd(o, pm, P["a_out_norm_w"], don)
    dq, dk, dv, dg, dbeta = _gdn_bwd(qn, kn, v, g_rows, beta_rows, sall, do)
    dy, dpba, d_aconv, d_alog, d_dtb = _gdn_pre_bwd(pm, pba, P["a_conv_w"], P["a_log"], P["dt_bias"],
                                                    dq, dk, dv, _gate_cols(dbeta), _gate_cols(dg))
    dpm = _gdn_conv_bwd(dy, dz, P["a_conv_w"])
    g_win_main = _mm_tn("gdn_in_dw", n0, dpm)
    g_win_ba = _mm_tn("gdn_in_ba_dw", n0, dpba)
    dn0 = _mm_nt("gdn_in_dx", dpm, P["w_in_main"], F32)
    dn0 = _mm_nt("gdn_in_ba_dx", dpba, P["w_in_ba"], F32, res=dn0)
    dx, d_anorm = _rms_bwd("a_rms_bwd", x, [(P["a_norm_w"], dn0)], [dh1])

    nh = GDN_V_HEADS
    grads = dict(
        a_norm_w=d_anorm,
        a_w_in=jnp.concatenate([g_win_main, g_win_ba[:, :nh], g_win_ba[:, LANE:LANE + nh]], axis=1),
        a_conv_w=d_aconv[:4], a_a_log=d_alog[:, :nh], a_dt_bias=d_dtb[:, :nh], a_out_norm_w=d_gnorm,
        a_w_out=g_wout, kv_norm_w=d_kvnorm, w_kv=g_wkv, b_norm_w=d_bnorm, b_w_q=g_wq,
        b_sinks=dsinks[:, :, 0].reshape(1, SWA_Q_HEADS), b_w_o=g_wo, rel_bias_table=g_table,
        ffn_norm_w=jnp.concatenate([gf0["norm_w"], gf1["norm_w"]], axis=0),
        ffn_w_up=jnp.stack([gf0["w_up"], gf1["w_up"]], axis=1),
        ffn_conv_w=jnp.stack([gf0["conv_w"], gf1["conv_w"]], axis=0),
        ffn_conv_b=jnp.concatenate([gf0["conv_b"], gf1["conv_b"]], axis=0),
        ffn_w_down=jnp.stack([gf0["w_down"], gf1["w_down"]], axis=0), final_norm_w=d_final,
    )
    return loss, dx, grads


HBM_SPEC = pl.BlockSpec(memory_space=pltpu.HBM)
VMEM_SPEC = pl.BlockSpec(memory_space=pltpu.VMEM)


def _coords():
    return lax.axis_index("x"), lax.axis_index("y"), lax.axis_index("c")


def _remote(src, dst, send_sem, recv_sem, device):
    return pltpu.make_async_remote_copy(src_ref=src, dst_ref=dst, send_sem=send_sem, recv_sem=recv_sem,
                                        device_id=device, device_id_type=MESH)


def _other_chips(x, y):
    return [(1 - x, y), (x, 1 - y), (1 - x, 1 - y)]


def _all_gather(arrs, split):
    n = len(arrs)

    def body(*refs):
        ins, outs = refs[:n], refs[n:2 * n]
        send_sems, recv_sems, local_sems = refs[2 * n:]
        x, y, c = _coords()
        p = 2 * x + y
        chips = _other_chips(x, y)

        def rows(a, half):
            h = arrs[a].shape[0] // 2
            return pl.ds(half * h, h)

        pending = []
        for a in range(n):
            cp = pltpu.make_async_copy(ins[a], outs[a].at[p], local_sems.at[a])
            cp.start()
            pending.append(cp)
        sends = []
        for a in range(n):
            for j, chip in enumerate(chips):
                if split[a]:
                    src, dst = ins[a].at[rows(a, c)], outs[a].at[p, rows(a, c)]
                else:
                    src, dst = ins[a], outs[a].at[p]
                cp = _remote(src, dst, send_sems.at[6 * a + j], recv_sems.at[6 * a + j], (*chip, c))
                cp.start()
                sends.append(cp)
        for a in range(n):
            for j, chip in enumerate(chips):
                q = 2 * chip[0] + chip[1]
                land = outs[a].at[q, rows(a, c)] if split[a] else outs[a].at[q]
                _remote(land, land, send_sems.at[6 * a + j], recv_sems.at[6 * a + j], (*chip, c)).wait_recv()
                if split[a]:
                    fw = _remote(land, land, send_sems.at[6 * a + 3 + j], recv_sems.at[6 * a + 3 + j], (x, y, 1 - c))
                    fw.start()
                    sends.append(fw)
        for a in range(n):
            if split[a]:
                for j, chip in enumerate(chips):
                    q = 2 * chip[0] + chip[1]
                    land = outs[a].at[q, rows(a, 1 - c)]
                    _remote(land, land, send_sems.at[6 * a + 3 + j], recv_sems.at[6 * a + 3 + j],
                            (x, y, 1 - c)).wait_recv()
        for cp in sends:
            cp.wait_send()
        for cp in pending:
            cp.wait()

    return pl.pallas_call(
        body, name="weights_all_gather", in_specs=[HBM_SPEC] * n, out_specs=[HBM_SPEC] * n,
        out_shape=[jax.ShapeDtypeStruct((N_CHIPS,) + a.shape, a.dtype) for a in arrs],
        scratch_shapes=[pltpu.SemaphoreType.DMA((6 * n,)), pltpu.SemaphoreType.DMA((6 * n,)),
                        pltpu.SemaphoreType.DMA((n,))],
    )(*arrs)


PAIR_SWAP_PIECES = 2


def _pair_swap(gs):
    n = len(gs)

    def body(*refs):
        ins, own, other = refs[:n], refs[n:2 * n], refs[2 * n:3 * n]
        send_sems, recv_sems, local_sems = refs[3 * n:]
        x, y, c = _coords()
        cps = []
        for a in range(n):
            h = gs[a].shape[1] // 2
            cp = pltpu.make_async_copy(ins[a].at[:, pl.ds(c * h, h)], own[a], local_sems.at[a])
            cp.start()
            cps.append(cp)
            piece = h // PAIR_SWAP_PIECES
            for q in range(N_CHIPS):
                for r in range(PAIR_SWAP_PIECES):
                    k = (a * N_CHIPS + q) * PAIR_SWAP_PIECES + r
                    cp = _remote(ins[a].at[q, pl.ds((1 - c) * h + r * piece, piece)],
                                 other[a].at[q, pl.ds(r * piece, piece)], send_sems.at[k], recv_sems.at[k],
                                 (x, y, 1 - c))
                    cp.start()
                    cps.append(cp)
        for cp in cps:
            cp.wait()

    half = [jax.ShapeDtypeStruct((N_CHIPS, g.shape[1] // 2, g.shape[2]), g.dtype) for g in gs]
    nsem = n * N_CHIPS * PAIR_SWAP_PIECES
    res = pl.pallas_call(
        body, name="grads_pair_swap", in_specs=[HBM_SPEC] * n, out_specs=[HBM_SPEC] * (2 * n), out_shape=half + half,
        scratch_shapes=[pltpu.SemaphoreType.DMA((nsem,)), pltpu.SemaphoreType.DMA((nsem,)),
                        pltpu.SemaphoreType.DMA((n,))],
    )(*gs)
    return res[:n], res[n:]


def _chip_scatter(ps):
    n = len(ps)

    def body(*refs):
        ins, outs = refs[:n], refs[n:2 * n]
        send_sems, recv_sems, local_sems = refs[2 * n:]
        x, y, c = _coords()
        p = 2 * x + y
        chips = _other_chips(x, y)
        cps, sends = [], []
        for a in range(n):
            cp = pltpu.make_async_copy(ins[a].at[p], outs[a].at[p], local_sems.at[a])
            cp.start()
            cps.append(cp)
            for j, chip in enumerate(chips):
                q = 2 * chip[0] + chip[1]
                cp = _remote(ins[a].at[q], outs[a].at[p], send_sems.at[3 * a + j], recv_sems.at[3 * a + j], (*chip, c))
                cp.start()
                sends.append(cp)
        for a in range(n):
            for j, chip in enumerate(chips):
                q = 2 * chip[0] + chip[1]
                land = outs[a].at[q]
                _remote(land, land, send_sems.at[3 * a + j], recv_sems.at[3 * a + j], (*chip, c)).wait_recv()
        for cp in sends:
            cp.wait_send()
        for cp in cps:
            cp.wait()

    return pl.pallas_call(
        body, name="grads_chip_scatter", in_specs=[HBM_SPEC] * n, out_specs=[HBM_SPEC] * n,
        out_shape=[jax.ShapeDtypeStruct(a.shape, a.dtype) for a in ps],
        scratch_shapes=[pltpu.SemaphoreType.DMA((3 * n,)), pltpu.SemaphoreType.DMA((3 * n,)),
                        pltpu.SemaphoreType.DMA((n,))],
    )(*ps)


def _pair_share(rs):
    n = len(rs)

    def body(*refs):
        ins, outs, stage, land = refs[:n], refs[n:2 * n], refs[2 * n:3 * n], refs[3 * n:4 * n]
        send_sems, recv_sems, local_sems, stage_sems, out_sems = refs[4 * n:]
        x, y, c = _coords()
        own, staged, sends, stores = [], [], [], []
        for a in range(n):
            h = rs[a].shape[0]
            cp = pltpu.make_async_copy(ins[a], stage[a], stage_sems.at[a])
            cp.start()
            staged.append(cp)
            cp = pltpu.make_async_copy(ins[a], outs[a].at[pl.ds(c * h, h)], local_sems.at[a])
            cp.start()
            own.append(cp)
        for a in range(n):
            staged[a].wait()
            cp = _remote(stage[a], land[a], send_sems.at[a], recv_sems.at[a], (x, y, 1 - c))
            cp.start()
            sends.append(cp)
        for a in range(n):
            h = rs[a].shape[0]
            sends[a].wait_recv()
            cp = pltpu.make_async_copy(land[a], outs[a].at[pl.ds((1 - c) * h, h)], out_sems.at[a])
            cp.start()
            stores.append(cp)
        for cp in sends:
            cp.wait_send()
        for cp in own + stores:
            cp.wait()

    vmem = [pltpu.VMEM(a.shape, a.dtype) for a in rs]
    return pl.pallas_call(
        body, name="grads_pair_share", in_specs=[HBM_SPEC] * n, out_specs=[HBM_SPEC] * n,
        out_shape=[jax.ShapeDtypeStruct((2 * a.shape[0], a.shape[1]), a.dtype) for a in rs],
        scratch_shapes=vmem + vmem + [pltpu.SemaphoreType.DMA((n,))] * 5,
        compiler_params=pltpu.CompilerParams(vmem_limit_bytes=VMEM_LIMIT),
    )(*rs)


def _small_all_reduce(buf):
    R = buf.shape[0]
    ndev = 2 * N_CHIPS

    def body(in_ref, out_ref, gath, send_sems, recv_sems):
        x, y, c = _coords()
        me = 4 * x + 2 * y + c
        gath[me] = in_ref[...]
        peers = []
        for d in range(1, ndev):
            px = 1 - x if d & 4 else x
            py = 1 - y if d & 2 else y
            pc = 1 - c if d & 1 else c
            peers.append((px, py, pc))
        sends = []
        for d, peer in enumerate(peers):
            cp = _remote(in_ref, gath.at[me], send_sems.at[d], recv_sems.at[d], peer)
            cp.start()
            sends.append(cp)
        for d, peer in enumerate(peers):
            land = gath.at[4 * peer[0] + 2 * peer[1] + peer[2]]
            _remote(land, land, send_sems.at[d], recv_sems.at[d], peer).wait_recv()
        for cp in sends:
            cp.wait_send()
        acc = gath[0]
        for s in range(1, ndev):
            acc = acc + gath[s]
        out_ref[...] = acc

    return pl.pallas_call(
        body, name="small_all_reduce", in_specs=[VMEM_SPEC], out_specs=VMEM_SPEC,
        out_shape=jax.ShapeDtypeStruct(buf.shape, F32),
        scratch_shapes=[pltpu.VMEM((ndev, R, LANE), F32), pltpu.SemaphoreType.DMA((ndev - 1,)),
                        pltpu.SemaphoreType.DMA((ndev - 1,))],
    )(buf)


def _pair_add(name, own, other):
    h = own.shape[1]
    tm = _tile(h, (128, 64, 32, 16))

    def fn(i, a, b):
        return (a + b,)

    return _rowcall(name, fn, h, tm, [(own, "row", None), (other, "row", None)], [(own.shape, BF16, "row")])[0]


def _chip_add(name, parts):
    h = parts.shape[1]
    tm = _tile(h, (128, 64, 32, 16))

    def fn(i, a):
        a = a.astype(F32)
        return (((a[0] + a[1]) + a[2]) + a[3],)

    return _rowcall(name, fn, h, tm, [(parts, "row", None)], [(parts.shape[1:], F32, "row")])[0]


def _adamw(name, w, g, m, v):
    R = w.shape[0]
    tm = _tile(R, (256, 128, 64, 32, 16, 8))

    def fn(i, wv, gv, mv, vv):
        m2 = ADAM_B1 * mv + (1.0 - ADAM_B1) * gv
        v2 = ADAM_B2 * vv + (1.0 - ADAM_B2) * (gv * gv)
        m_hat = m2 / (1.0 - ADAM_B1 ** ADAM_STEP)
        v_hat = v2 / (1.0 - ADAM_B2 ** ADAM_STEP)
        delta = -ADAM_LR * (m_hat / (jnp.sqrt(v_hat) + ADAM_EPS) + ADAM_WD * wv)
        return delta, m2, v2

    ins = [(a, "row", None) for a in (w, g, m, v)]
    return _rowcall(name, fn, R, tm, ins, [(w.shape, F32, "row")] * 3)


def _pack(arrs):
    flat = jnp.concatenate([a.reshape(-1).astype(F32) for a in arrs])
    size = flat.shape[0]
    padded = -(-size // (SUBLANE * LANE)) * SUBLANE * LANE
    return jnp.pad(flat, (0, padded - size)).reshape(-1, LANE)


def _unpack(buf, shapes):
    flat = buf.reshape(-1)
    out, off = [], 0
    for s in shapes:
        size = math.prod(s)
        out.append(flat[off:off + size].reshape(s))
        off += size
    return out


BIG = ("a_w_in", "a_w_out", "w_kv", "b_w_q", "b_w_o", "ffn_w_up", "ffn_w_down")
WEIGHTS = ("a_norm_w", "a_w_in", "a_conv_w", "a_a_log", "a_dt_bias", "a_out_norm_w", "a_w_out", "kv_norm_w", "w_kv",
           "b_norm_w", "b_w_q", "b_sinks", "b_w_o", "rel_bias_table", "ffn_norm_w", "ffn_w_up", "ffn_conv_w",
           "ffn_conv_b", "ffn_w_down", "final_norm_w")
SMALL = tuple(n for n in WEIGHTS if n not in BIG)
SMALL_SHARDED = {"a_norm_w": 1, "a_conv_w": 2, "ffn_conv_w": 2}


def _quarter_2d(name, a):
    if name in ("ffn_w_up", "ffn_w_down"):
        return a.reshape(a.shape[0] * a.shape[1], a.shape[2])
    return a.reshape(a.shape[-2], a.shape[-1])


def _whole_weights(w):
    bigs = [_quarter_2d(n, w[n]).astype(BF16) for n in BIG]
    smalls = [w["a_norm_w"], w["a_conv_w"][0], w["ffn_conv_w"].reshape(6, DFF2_SHARD)]
    g = _all_gather(bigs + smalls, [True] * len(bigs) + [False] * len(smalls))
    w_in = g[0].transpose(1, 0, 2).reshape(D, GDN_IN)
    nh = GDN_V_HEADS
    zpad = jnp.zeros((D, LANE - nh), BF16)
    w_in_ba = jnp.concatenate([w_in[:, GDN_MAIN:GDN_MAIN + nh], zpad, w_in[:, GDN_MAIN + nh:], zpad], axis=1)
    lane_pad = lambda a: jnp.pad(a, ((0, 0), (0, LANE - nh)))
    return dict(
        a_norm_w=g[7].reshape(1, D), w_in_main=w_in[:, :GDN_MAIN], w_in_ba=w_in_ba,
        a_conv_w=g[8].transpose(1, 0, 2).reshape(4, GDN_CONV), a_log=lane_pad(w["a_a_log"]),
        dt_bias=lane_pad(w["a_dt_bias"]), a_out_norm_w=w["a_out_norm_w"], w_out=g[1].reshape(GDN_V, D),
        kv_norm_w=w["kv_norm_w"].reshape(1, D), w_kv=g[2].reshape(D, 2 * SWA_KV_HEADS * SWA_HD),
        b_norm_w=w["b_norm_w"], w_q=g[3].reshape(D, D), w_o=g[4].reshape(D, D),
        sinks=jnp.broadcast_to(w["b_sinks"].reshape(SWA_KV_HEADS, SWA_GROUP, 1), (SWA_KV_HEADS, SWA_GROUP, LANE)),
        rel_table_t=w["rel_bias_table"].T, ffn_norm_w=w["ffn_norm_w"],
        w_up=g[5].reshape(N_CHIPS, 2, D, DFF2_SHARD),
        ffn_conv_w=g[9].reshape(N_CHIPS, 2, 3, DFF2_SHARD).transpose(1, 2, 0, 3).reshape(2, 3, DFF2),
        ffn_conv_b=w["ffn_conv_b"],
        w_down=g[6].reshape(N_CHIPS, 2, DFF_SHARD, D).transpose(1, 0, 2, 3).reshape(2, DFF, D),
        final_norm_w=w["final_norm_w"].reshape(1, D),
    )


def _chip_major(name, g):
    if name == "a_w_in":
        return g.reshape(D, N_CHIPS, GDN_IN_SHARD).transpose(1, 0, 2)
    if name == "ffn_w_up":
        return g.reshape(N_CHIPS, 2 * D, DFF2_SHARD)
    if name == "ffn_w_down":
        return g.reshape(2, N_CHIPS, DFF_SHARD, D).transpose(1, 0, 2, 3).reshape(N_CHIPS, 2 * DFF_SHARD, D)
    return g.reshape(N_CHIPS, g.shape[0] // N_CHIPS, g.shape[1])


def kernel(x, a_norm_w, a_w_in, a_conv_w, a_a_log, a_dt_bias, a_out_norm_w, a_w_out, kv_norm_w, w_kv, b_norm_w, b_w_q, b_sinks, b_w_o, rel_bias_table, ffn_norm_w, ffn_w_up, ffn_conv_w, ffn_conv_b, ffn_w_down, final_norm_w, loss_target, m_a_norm_w, m_a_w_in, m_a_conv_w, m_a_a_log, m_a_dt_bias, m_a_out_norm_w, m_a_w_out, m_kv_norm_w, m_w_kv, m_b_norm_w, m_b_w_q, m_b_sinks, m_b_w_o, m_rel_bias_table, m_ffn_norm_w, m_ffn_w_up, m_ffn_conv_w, m_ffn_conv_b, m_ffn_w_down, m_final_norm_w, v_a_norm_w, v_a_w_in, v_a_conv_w, v_a_a_log, v_a_dt_bias, v_a_out_norm_w, v_a_w_out, v_kv_norm_w, v_w_kv, v_b_norm_w, v_b_w_q, v_b_sinks, v_b_w_o, v_rel_bias_table, v_ffn_norm_w, v_ffn_w_up, v_ffn_conv_w, v_ffn_conv_b, v_ffn_w_down, v_final_norm_w):
    w = dict(zip(WEIGHTS, (a_norm_w, a_w_in, a_conv_w, a_a_log, a_dt_bias, a_out_norm_w, a_w_out, kv_norm_w, w_kv,
                           b_norm_w, b_w_q, b_sinks, b_w_o, rel_bias_table, ffn_norm_w, ffn_w_up, ffn_conv_w,
                           ffn_conv_b, ffn_w_down, final_norm_w)))
    m = dict(zip(WEIGHTS, (m_a_norm_w, m_a_w_in, m_a_conv_w, m_a_a_log, m_a_dt_bias, m_a_out_norm_w, m_a_w_out,
                           m_kv_norm_w, m_w_kv, m_b_norm_w, m_b_w_q, m_b_sinks, m_b_w_o, m_rel_bias_table,
                           m_ffn_norm_w, m_ffn_w_up, m_ffn_conv_w, m_ffn_conv_b, m_ffn_w_down, m_final_norm_w)))
    v = dict(zip(WEIGHTS, (v_a_norm_w, v_a_w_in, v_a_conv_w, v_a_a_log, v_a_dt_bias, v_a_out_norm_w, v_a_w_out,
                           v_kv_norm_w, v_w_kv, v_b_norm_w, v_b_w_q, v_b_sinks, v_b_w_o, v_rel_bias_table,
                           v_ffn_norm_w, v_ffn_w_up, v_ffn_conv_w, v_ffn_conv_b, v_ffn_w_down, v_final_norm_w)))
    T = x.shape[1]
    chip = 2 * lax.axis_index("x") + lax.axis_index("y")

    loss_part, dx, grads = _local_step(x.reshape(T, D), loss_target.reshape(T, D), _whole_weights(w))

    own, other = _pair_swap([_chip_major(n, grads[n]) for n in BIG])
    pair = [_pair_add(f"pair_add_{n}", a, b) for n, a, b in zip(BIG, own, other)]
    parts = _chip_scatter(pair)
    halves = [_chip_add(f"chip_add_{n}", a) for n, a in zip(BIG, parts)]
    quarter = _pair_share(halves)
    out_g, out_d, out_m, out_v = {}, {}, {}, {}
    for n, g2 in zip(BIG, quarter):
        res = _adamw(f"adamw_{n}", _quarter_2d(n, w[n]), g2, _quarter_2d(n, m[n]), _quarter_2d(n, v[n]))
        out_g[n] = g2.reshape(w[n].shape)
        out_d[n], out_m[n], out_v[n] = (r.reshape(w[n].shape) for r in res)

    whole = [grads[n] for n in SMALL]
    summed = _unpack(_small_all_reduce(_pack([loss_part[0:1, 0:1]] + whole)), [(1, 1)] + [a.shape for a in whole])
    loss = summed[0].reshape(())
    small_g = []
    for n, g in zip(SMALL, summed[1:]):
        if n in SMALL_SHARDED:
            axis = SMALL_SHARDED[n]
            g = g.reshape(w[n].shape[:axis] + (-1,) + w[n].shape[axis + 1:])
            size = w[n].shape[axis]
            g = lax.dynamic_slice_in_dim(g, chip * size, size, axis)
        small_g.append(g.reshape(w[n].shape))
    shapes = [w[n].shape for n in SMALL]
    res = _adamw("adamw_small", _pack([w[n] for n in SMALL]), _pack(small_g), _pack([m[n] for n in SMALL]),
                 _pack([v[n] for n in SMALL]))
    small_d, small_m, small_v = (_unpack(r, shapes) for r in res)
    for i, n in enumerate(SMALL):
        out_g[n], out_d[n], out_m[n], out_v[n] = small_g[i], small_d[i], small_m[i], small_v[i]

    return (loss, dx.reshape(x.shape), *[out_g[n] for n in WEIGHTS], *[out_d[n] for n in WEIGHTS],
            *[out_m[n] for n in WEIGHTS], *[out_v[n] for n in WEIGHTS])
```

```python
import functools
import math

import jax
import jax.numpy as jnp
from jax import lax
from jax.experimental import pallas as pl
from jax.experimental.pallas import tpu as pltpu

F32 = jnp.float32
BF16 = jnp.bfloat16
MESH = pl.DeviceIdType.MESH
HIGHEST = lax.Precision.HIGHEST

D = 1024
EPS = 1e-6
NEG_INF = -1e30
N_CHIPS = 4

GDN_QK_HEADS = 8
GDN_V_HEADS = 16
GDN_HD = 128
GDN_QK = GDN_QK_HEADS * GDN_HD
GDN_V = GDN_V_HEADS * GDN_HD
GDN_CONV = 2 * GDN_QK + GDN_V
GDN_MAIN = GDN_CONV + GDN_V
GDN_IN = GDN_MAIN + 2 * GDN_V_HEADS
GDN_IN_SHARD = GDN_IN // N_CHIPS
GDN_CHUNK = 64

SWA_Q_HEADS = 16
SWA_KV_HEADS = 4
SWA_GROUP = 4
SWA_HD = 64
SWA_BLOCK = 128
REL_BUCKETS = 32
REL_MAX_DISTANCE = 128

DFF = 2816
DFF2 = 2 * DFF
DFF2_SHARD = DFF2 // N_CHIPS
DFF_SHARD = DFF // N_CHIPS

ADAM_LR = 0.001
ADAM_B1 = 0.9
ADAM_B2 = 0.999
ADAM_EPS = 1e-08
ADAM_WD = 0.01
ADAM_STEP = 10

LANE = 128
SUBLANE = 8
VMEM_LIMIT = 56 * 1024 * 1024


def _params(sem, vmem=VMEM_LIMIT):
    return pltpu.CompilerParams(dimension_semantics=sem, vmem_limit_bytes=vmem)


def _rowcall(name, fn, T, tm, ins, outs):
    n = T // tm
    r8 = tm // SUBLANE
    last8 = T // SUBLANE - 1
    arrays, in_specs = [], []
    for arr, kind, cols in ins:
        arrays.append(arr)
        if kind == "full":
            in_specs.append(pl.BlockSpec(arr.shape, functools.partial(lambda nd, i: (0,) * nd, arr.ndim)))
        elif arr.ndim == 2:
            w, ci = cols if cols is not None else (arr.shape[1], 0)
            if kind == "row":
                in_specs.append(pl.BlockSpec((tm, w), functools.partial(lambda ci, i: (i, ci), ci)))
            elif kind == "prev":
                in_specs.append(pl.BlockSpec(
                    (SUBLANE, w), functools.partial(lambda ci, i: (jnp.maximum(i * r8 - 1, 0), ci), ci)))
            else:
                in_specs.append(pl.BlockSpec(
                    (SUBLANE, w), functools.partial(lambda ci, i: (jnp.minimum((i + 1) * r8, last8), ci), ci)))
        else:
            lead = arr.shape[:-2]
            in_specs.append(pl.BlockSpec(lead + (tm, arr.shape[-1]),
                                         functools.partial(lambda nl, i: (0,) * nl + (i, 0), len(lead))))
    out_shape, out_specs = [], []
    for shape, dtype, kind in outs:
        out_shape.append(jax.ShapeDtypeStruct(shape, dtype))
        if kind == "acc":
            out_specs.append(pl.BlockSpec(shape, functools.partial(lambda nd, i: (0,) * nd, len(shape))))
        else:
            lead = shape[:-2]
            out_specs.append(pl.BlockSpec(lead + (tm, shape[-1]),
                                          functools.partial(lambda nl, i: (0,) * nl + (i, 0), len(lead))))
    nin = len(arrays)

    def body(*refs):
        i = pl.program_id(0)
        vals = [r[...] for r in refs[:nin]]
        res = fn(i, *vals)
        for (shape, dtype, kind), o, r in zip(outs, refs[nin:], res):
            if kind == "row":
                o[...] = r.astype(dtype)
            else:
                @pl.when(i == 0)
                def _():
                    o[...] = r.astype(dtype)

                @pl.when(i > 0)
                def _():
                    o[...] += r.astype(dtype)

    res = pl.pallas_call(
        body, name=name, grid=(n,), in_specs=in_specs, out_specs=out_specs, out_shape=out_shape,
        compiler_params=_params(("arbitrary",)),
    )(*arrays)
    return res


def _mm(name, a, b, out_shape, out_dtype, grid, a_spec, b_spec, o_spec, dims, acc_shape, res=None, precision=None):
    nk = grid[2]

    def body(*refs):
        if res is not None:
            a_ref, b_ref, r_ref, o_ref = refs[:4]
        else:
            a_ref, b_ref, o_ref = refs[:3]
        av, bv = a_ref[...], b_ref[...]
        if precision is None:
            av, bv = av.astype(BF16), bv.astype(BF16)
        p = lax.dot_general(av, bv, (dims, ((), ())), preferred_element_type=F32, precision=precision)

        def finish(x):
            if res is not None:
                x = x + r_ref[...].astype(F32)
            o_ref[...] = x.astype(out_dtype)

        if nk == 1:
            finish(p)
        else:
            acc = refs[-1]
            k = pl.program_id(2)

            @pl.when(k == 0)
            def _():
                acc[...] = p

            @pl.when(k > 0)
            def _():
                acc[...] += p

            @pl.when(k == nk - 1)
            def _():
                finish(acc[...])

    ops = [a, b] + ([res] if res is not None else [])
    specs = [a_spec, b_spec] + ([o_spec] if res is not None else [])
    return pl.pallas_call(
        body, name=name, grid=grid, in_specs=specs, out_specs=o_spec,
        out_shape=jax.ShapeDtypeStruct(out_shape, out_dtype),
        scratch_shapes=[pltpu.VMEM(acc_shape, F32)] if nk > 1 else [],
        compiler_params=_params(("parallel", "parallel", "arbitrary")),
    )(*ops)


NN = ((1,), (0,))
NT = ((1,), (1,))
TN = ((0,), (0,))


BIG_TILES = (1024, 512, 256, 128)


def _tile(n, pref):
    for t in pref:
        if n % t == 0:
            return t
    return n


def _mm_nn(name, a, w, out_dtype, res=None, precision=None):
    M, K = a.shape
    N = w.shape[1]
    tm = _tile(M, BIG_TILES if K <= 2048 else BIG_TILES[1:])
    tn = _tile(N, BIG_TILES)
    return _mm(name, a, w, (M, N), out_dtype, (M // tm, N // tn, 1),
               pl.BlockSpec((tm, K), lambda i, j, k: (i, 0)), pl.BlockSpec((K, tn), lambda i, j, k: (0, j)),
               pl.BlockSpec((tm, tn), lambda i, j, k: (i, j)), NN, (tm, tn), res=res, precision=precision)


def _mm_nt(name, g, w, out_dtype, res=None, precision=None):
    M, N = g.shape
    K = w.shape[0]
    tm, tk = _tile(M, BIG_TILES), _tile(K, (1024, 1408, 512, 256, 128))
    tn = _tile(N, (1536,) + BIG_TILES)
    return _mm(name, g, w, (M, K), out_dtype, (M // tm, K // tk, N // tn),
               pl.BlockSpec((tm, tn), lambda i, j, k: (i, k)), pl.BlockSpec((tk, tn), lambda i, j, k: (j, k)),
               pl.BlockSpec((tm, tk), lambda i, j, k: (i, j)), NT, (tm, tk), res=res, precision=precision)


def _mm_tn(name, a, g, out_dtype=F32, precision=None):
    T, K = a.shape
    N = g.shape[1]
    tk, tn = _tile(K, (1024, 1408, 512, 256, 128)), _tile(N, BIG_TILES)
    tt = _tile(T, BIG_TILES)
    return _mm(name, a, g, (K, N), out_dtype, (K // tk, N // tn, T // tt),
               pl.BlockSpec((tt, tk), lambda i, j, k: (k, i)), pl.BlockSpec((tt, tn), lambda i, j, k: (k, j)),
               pl.BlockSpec((tk, tn), lambda i, j, k: (i, j)), TN, (tk, tn), precision=precision)


def _mm_up(name, n, wup, layer):
    T = n.shape[0]
    tm = _tile(T, BIG_TILES)
    return _mm(name, n, wup, (T, DFF2), BF16, (T // tm, N_CHIPS, 1),
               pl.BlockSpec((tm, D), lambda i, j, k: (i, 0)),
               pl.BlockSpec((None, None, D, DFF2_SHARD), lambda i, j, k: (j, layer, 0, 0)),
               pl.BlockSpec((tm, DFF2_SHARD), lambda i, j, k: (i, j)), NN, (tm, DFF2_SHARD))


def _mm_up_nt(name, du, wup, layer):
    T = du.shape[0]
    tm, tk = _tile(T, BIG_TILES), D
    return _mm(name, du, wup, (T, D), F32, (T // tm, D // tk, N_CHIPS),
               pl.BlockSpec((tm, DFF2_SHARD), lambda i, j, k: (i, k)),
               pl.BlockSpec((None, None, tk, DFF2_SHARD), lambda i, j, k: (k, layer, j, 0)),
               pl.BlockSpec((tm, tk), lambda i, j, k: (i, j)), NT, (tm, tk))


def _mm_up_tn(name, n, du):
    T = n.shape[0]
    tk, tt = D, _tile(T, BIG_TILES)
    return _mm(name, n, du, (N_CHIPS, D, DFF2_SHARD), F32, (D // tk, N_CHIPS, T // tt),
               pl.BlockSpec((tt, tk), lambda i, j, k: (k, i)), pl.BlockSpec((tt, DFF2_SHARD), lambda i, j, k: (k, j)),
               pl.BlockSpec((None, tk, DFF2_SHARD), lambda i, j, k: (j, i, 0)), TN, (tk, DFF2_SHARD))


def _sigmoid(x):
    return 1.0 / (1.0 + jnp.exp(-x))


def _silu(x):
    return x * _sigmoid(x)


def _softplus(x):
    return jnp.maximum(x, 0.0) + jnp.log(1.0 + jnp.exp(-jnp.abs(x)))


def _rms_core(h, w):
    return h * lax.rsqrt(jnp.mean(h * h, axis=-1, keepdims=True) + EPS) * w


def _shift_down(x, halo, s, i):
    if s == 0:
        return x
    tm = x.shape[0]
    rolled = pltpu.roll(x, s, 0)
    patch = pltpu.roll(jnp.where(i == 0, 0.0, halo), s, 0)
    row = lax.broadcasted_iota(jnp.int32, patch.shape, 0)
    top = jnp.where(row < s, patch, rolled[:SUBLANE])
    return jnp.concatenate([top, rolled[SUBLANE:]], axis=0) if tm > SUBLANE else top


def _shift_up(x, halo, s, i, n):
    if s == 0:
        return x
    tm = x.shape[0]
    rolled = pltpu.roll(x, tm - s, 0)
    patch = pltpu.roll(jnp.where(i == n - 1, 0.0, halo), SUBLANE - s, 0)
    row = lax.broadcasted_iota(jnp.int32, patch.shape, 0)
    bottom = jnp.where(row >= SUBLANE - s, patch, rolled[tm - SUBLANE:])
    return jnp.concatenate([rolled[:tm - SUBLANE], bottom], axis=0) if tm > SUBLANE else bottom


def _conv_fwd(x, halo, w, i):
    K = w.shape[0]
    y = w[K - 1:K, :] * x
    for j in range(K - 1):
        y = y + w[j:j + 1, :] * _shift_down(x, halo, K - 1 - j, i)
    return y


def _conv_dx(dy, halo_next, w, i, n):
    K = w.shape[0]
    dx = w[K - 1:K, :] * dy
    for j in range(K - 1):
        dx = dx + w[j:j + 1, :] * _shift_up(dy, halo_next, K - 1 - j, i, n)
    return dx


def _conv_dw(dy, x, halo, K, i):
    rows = [jnp.sum(dy * _shift_down(x, halo, K - 1 - j, i), axis=0, keepdims=True) for j in range(K)]
    return jnp.concatenate(rows + [jnp.zeros((SUBLANE - K, dy.shape[1]), F32)], axis=0)


def _rms_fwd(name, h, w, tm=512):
    T = h.shape[0]
    tm = min(tm, T)

    def fn(i, hv, wv):
        return (_rms_core(hv, wv),)

    return _rowcall(name, fn, T, tm, [(h, "row", None), (w, "full", None)], [((T, D), BF16, "row")])[0]


def _rms_bwd(name, h, pairs, adds, tm=256):
    T = h.shape[0]
    tm = min(tm, T)
    npair, nadd = len(pairs), len(adds)

    def fn(i, hv, *rest):
        ws, dns, ads = rest[:npair], rest[npair:2 * npair], rest[2 * npair:]
        dh = None
        dws = []
        for wv, dn in zip(ws, dns):
            _, vjp = jax.vjp(_rms_core, hv, wv)
            dhi, dwi = vjp(dn.astype(F32))
            dh = dhi if dh is None else dh + dhi
            dws.append(dwi)
        for a in ads:
            dh = dh + a.astype(F32)
        return (dh, *dws)

    ins = [(h, "row", None)] + [(w, "full", None) for w, _ in pairs] + [(dn, "row", None) for _, dn in pairs]
    ins += [(a, "row", None) for a in adds]
    outs = [((T, D), F32, "row")] + [((1, D), F32, "acc")] * npair
    return _rowcall(name, fn, T, tm, ins, outs)


def _l2(x):
    return x * lax.rsqrt(jnp.sum(x * x, axis=-1, keepdims=True) + EPS)


def _gdn_post_core(yq, yk, yv, pb, pa, a_log, dtb):
    qn = tuple(_l2(_silu(a)) * (GDN_HD ** -0.5) for a in yq)
    kn = tuple(_l2(_silu(a)) for a in yk)
    v = _silu(yv)
    beta = _sigmoid(pb)
    g = -jnp.exp(a_log) * _softplus(pa + dtb)
    return qn, kn, v, beta, g


def _heads(x, n):
    return tuple(x[:, GDN_HD * h:GDN_HD * (h + 1)] for h in range(n))


def _gdn_pre_fwd(pm, pba, conv_w, a_log, dtb, tm=128):
    T = pm.shape[0]
    tm = min(tm, T)

    def fn(i, x, halo, pbav, cw, al, db):
        y = _conv_fwd(x.astype(F32), halo.astype(F32), cw, i)
        qn, kn, v, beta, g = _gdn_post_core(_heads(y[:, :GDN_QK], 8), _heads(y[:, GDN_QK:2 * GDN_QK], 8),
                                            y[:, 2 * GDN_QK:], pbav[:, :LANE], pbav[:, LANE:], al, db)
        return jnp.stack(qn), jnp.stack(kn), jnp.stack(_heads(v, GDN_V_HEADS)), beta, g

    ins = [(pm, "row", (GDN_CONV, 0)), (pm, "prev", (GDN_CONV, 0)), (pba, "row", None),
           (conv_w, "full", None), (a_log, "full", None), (dtb, "full", None)]
    outs = [((GDN_QK_HEADS, T, GDN_HD), BF16, "row"), ((GDN_QK_HEADS, T, GDN_HD), BF16, "row"),
            ((GDN_V_HEADS, T, GDN_HD), BF16, "row"), ((T, LANE), F32, "row"), ((T, LANE), F32, "row")]
    return _rowcall("gdn_pre_fwd", fn, T, tm, ins, outs)


def _gdn_pre_bwd(pm, pba, conv_w, a_log, dtb, dqn, dkn, dv, dbeta, dg, tm=128):
    T = pm.shape[0]
    tm = min(tm, T)

    def fn(i, x, halo, pbav, cw, al, db, dqv, dkv, dvv, dbv, dgv):
        xf, hf = x.astype(F32), halo.astype(F32)
        y = _conv_fwd(xf, hf, cw, i)
        prim = (_heads(y[:, :GDN_QK], 8), _heads(y[:, GDN_QK:2 * GDN_QK], 8), y[:, 2 * GDN_QK:],
                pbav[:, :LANE], pbav[:, LANE:], al, db)
        _, vjp = jax.vjp(_gdn_post_core, *prim)
        cot = (tuple(dqv[h].astype(F32) for h in range(8)), tuple(dkv[h].astype(F32) for h in range(8)),
               jnp.concatenate([dvv[h].astype(F32) for h in range(GDN_V_HEADS)], axis=1), dbv, dgv)
        dyq, dyk, dyv, dpb, dpa, dal, ddb = vjp(cot)
        dy = jnp.concatenate(list(dyq) + list(dyk) + [dyv], axis=1)
        dcw = _conv_dw(dy, xf, hf, 4, i)
        return dy, jnp.concatenate([dpb, dpa], axis=1), dcw, dal, ddb

    ins = [(pm, "row", (GDN_CONV, 0)), (pm, "prev", (GDN_CONV, 0)), (pba, "row", None),
           (conv_w, "full", None), (a_log, "full", None), (dtb, "full", None),
           (dqn, "row", None), (dkn, "row", None), (dv, "row", None), (dbeta, "row", None), (dg, "row", None)]
    outs = [((T, GDN_CONV), BF16, "row"), ((T, 2 * LANE), F32, "row"), ((SUBLANE, GDN_CONV), F32, "acc"),
            ((1, LANE), F32, "acc"), ((1, LANE), F32, "acc")]
    return _rowcall("gdn_pre_bwd", fn, T, tm, ins, outs)


def _gdn_conv_bwd(dy, dz, conv_w, tm=256):
    T = dy.shape[0]
    tm = min(tm, T)
    n = T // tm

    def fn(i, dyv, halo, dzv, cw):
        dx = _conv_dx(dyv.astype(F32), halo.astype(F32), cw, i, n)
        return (jnp.concatenate([dx.astype(BF16), dzv.astype(BF16)], axis=1),)

    ins = [(dy, "row", None), (dy, "next", None), (dz, "row", None), (conv_w, "full", None)]
    return _rowcall("gdn_conv_bwd", fn, T, tm, ins, [((T, GDN_MAIN), BF16, "row")])[0]


def _bdot(a, b, dims=NN):
    return lax.dot_general(a.astype(BF16), b.astype(BF16), (dims, ((), ())), preferred_element_type=F32)


BNN = ((2,), (1,))
BNT = ((2,), (2,))
BTN = ((1,), (1,))


def _bmm(a, b, dims=BNN):
    return lax.dot_general(a.astype(BF16), b.astype(BF16), (dims, ((0,), (0,))), preferred_element_type=F32)


def _bmm3(a, b):
    ah, bh = a.astype(BF16), b.astype(BF16)
    al, bl = (a - ah.astype(F32)).astype(BF16), (b - bh.astype(F32)).astype(BF16)
    dn = (BNN, ((0,), (0,)))
    return (lax.dot_general(ah, bh, dn, preferred_element_type=F32)
            + lax.dot_general(al, bh, dn, preferred_element_type=F32)
            + lax.dot_general(ah, bl, dn, preferred_element_type=F32))


@jax.custom_vjp
def _tri_inv(m):
    C = m.shape[-1]
    r = lax.broadcasted_iota(jnp.int32, (C, C), 0)
    c = lax.broadcasted_iota(jnp.int32, (C, C), 1)
    t = jnp.where(r == c, 1.0, 0.0) - m
    pw = _bmm3(m, m)
    t = t + _bmm3(t, pw)
    for _ in range(int(math.log2(C)) - 2):
        pw = _bmm(pw, pw)
        t = t + _bmm(t, pw)
    return t


def _tri_inv_fwd(m):
    t = _tri_inv(m)
    return t, t


def _tri_inv_bwd(t, dt):
    tt = jnp.swapaxes(t, 1, 2)
    return (-_bmm(_bmm(tt, dt), tt),)


_tri_inv.defvjp(_tri_inv_fwd, _tri_inv_bwd)


def _twice(a):
    return jnp.broadcast_to(a[:, None], (a.shape[0], 2) + a.shape[1:]).reshape((2 * a.shape[0],) + a.shape[1:])


def _gdn_chunk(q, k, v, grow, brow, S):
    C = q.shape[1]
    r = lax.broadcasted_iota(jnp.int32, (C, C), 0)
    c = lax.broadcasted_iota(jnp.int32, (C, C), 1)
    tril, strict, eye = r >= c, r > c, r == c
    gcol = jnp.sum(jnp.where(eye, grow, 0.0), axis=2, keepdims=True)
    bcol = jnp.sum(jnp.where(eye, brow, 0.0), axis=2, keepdims=True)
    gc_col = jnp.sum(jnp.where(tril, grow, 0.0), axis=2, keepdims=True)
    gc_row = jnp.sum(jnp.where(r <= c, gcol, 0.0), axis=1, keepdims=True)
    gc_last = jnp.sum(grow, axis=2, keepdims=True)
    decay = jnp.where(tril, jnp.exp(jnp.where(tril, gc_col - gc_row, 0.0)), 0.0)
    kk = _twice(_bmm(k, k, BNT))
    qk = _twice(_bmm(q, k, BNT))
    t_mat = _tri_inv(jnp.where(strict, bcol * kk * decay, 0.0))
    k2, q2 = _twice(k), _twice(q)
    egc = jnp.exp(gc_col)
    u = _bmm(t_mat, v * bcol)
    w = _bmm(t_mat, k2 * (bcol * egc))
    v_new = u - _bmm(w, S)
    o = _bmm(q2 * egc, S) + _bmm(qk * decay, v_new)
    s_new = S * jnp.exp(gc_last) + _bmm(k2 * jnp.exp(gc_last - gc_col), v_new, BTN)
    return o, s_new


def _gdn_tb(T):
    return min(256, T)


def _gate_rows(g):
    T = g.shape[0]
    g = g[:, :GDN_V_HEADS].reshape(T // GDN_CHUNK, GDN_CHUNK, GDN_V_HEADS)
    return g.transpose(0, 2, 1)[:, :, None, :]


def _gate_cols(g):
    nc = g.shape[0]
    g = g[:, :, 0, :].transpose(0, 2, 1).reshape(nc * GDN_CHUNK, GDN_V_HEADS)
    return jnp.pad(g, ((0, 0), (0, LANE - GDN_V_HEADS)))


def _gdn_fwd(qn, kn, v, g, beta):
    T = qn.shape[1]
    tb = _gdn_tb(T)
    nc = tb // GDN_CHUNK

    def body(q_ref, k_ref, v_ref, g_ref, b_ref, o_ref, sall_ref, s_scr):
        @pl.when(pl.program_id(0) == 0)
        def _():
            s_scr[...] = jnp.zeros(s_scr.shape, F32)

        def chunk(ci, carry):
            rows = pl.ds(pl.multiple_of(ci * GDN_CHUNK, GDN_CHUNK), GDN_CHUNK)
            s = s_scr[...]
            sall_ref[ci] = s
            o, s_new = _gdn_chunk(q_ref[:, rows, :].astype(F32), k_ref[:, rows, :].astype(F32),
                                  v_ref[:, rows, :].astype(F32), g_ref[ci], b_ref[ci], s)
            o_ref[:, rows, :] = o.astype(o_ref.dtype)
            s_scr[...] = s_new
            return carry

        lax.fori_loop(0, nc, chunk, 0)

    qk_spec = pl.BlockSpec((GDN_QK_HEADS, tb, GDN_HD), lambda i: (0, i, 0))
    v_spec = pl.BlockSpec((GDN_V_HEADS, tb, GDN_HD), lambda i: (0, i, 0))
    g_spec = pl.BlockSpec((nc, GDN_V_HEADS, 1, GDN_CHUNK), lambda i: (i, 0, 0, 0))
    return pl.pallas_call(
        body, name="gdn_fwd", grid=(T // tb,),
        in_specs=[qk_spec, qk_spec, v_spec, g_spec, g_spec],
        out_specs=[v_spec, pl.BlockSpec((nc, GDN_V_HEADS, GDN_HD, GDN_HD), lambda i: (i, 0, 0, 0))],
        out_shape=[jax.ShapeDtypeStruct((GDN_V_HEADS, T, GDN_HD), BF16),
                   jax.ShapeDtypeStruct((T // GDN_CHUNK, GDN_V_HEADS, GDN_HD, GDN_HD), F32)],
        scratch_shapes=[pltpu.VMEM((GDN_V_HEADS, GDN_HD, GDN_HD), F32)],
        compiler_params=_params(("arbitrary",)),
    )(qn, kn, v, g, beta)


def _gdn_bwd(qn, kn, v, g, beta, sall, do):
    T = qn.shape[1]
    tb = _gdn_tb(T)
    nc = tb // GDN_CHUNK
    nb = T // tb

    def body(q_ref, k_ref, v_ref, g_ref, b_ref, sall_ref, do_ref, dq_ref, dk_ref, dv_ref, dg_ref, db_ref, ds_scr):
        @pl.when(pl.program_id(0) == 0)
        def _():
            ds_scr[...] = jnp.zeros(ds_scr.shape, F32)

        def chunk(cr, carry):
            ci = nc - 1 - cr
            rows = pl.ds(pl.multiple_of(ci * GDN_CHUNK, GDN_CHUNK), GDN_CHUNK)
            _, vjp = jax.vjp(_gdn_chunk, q_ref[:, rows, :].astype(F32), k_ref[:, rows, :].astype(F32),
                             v_ref[:, rows, :].astype(F32), g_ref[ci], b_ref[ci], sall_ref[ci])
            dq, dk, dv, dg, db, ds = vjp((do_ref[:, rows, :].astype(F32), ds_scr[...]))
            ds_scr[...] = ds
            dq_ref[:, rows, :] = dq
            dk_ref[:, rows, :] = dk
            dv_ref[:, rows, :] = dv
            dg_ref[ci] = dg
            db_ref[ci] = db
            return carry

        lax.fori_loop(0, nc, chunk, 0)

    qk_spec = pl.BlockSpec((GDN_QK_HEADS, tb, GDN_HD), lambda i: (0, nb - 1 - i, 0))
    v_spec = pl.BlockSpec((GDN_V_HEADS, tb, GDN_HD), lambda i: (0, nb - 1 - i, 0))
    g_spec = pl.BlockSpec((nc, GDN_V_HEADS, 1, GDN_CHUNK), lambda i: (nb - 1 - i, 0, 0, 0))
    s_spec = pl.BlockSpec((nc, GDN_V_HEADS, GDN_HD, GDN_HD), lambda i: (nb - 1 - i, 0, 0, 0))
    return pl.pallas_call(
        body, name="gdn_bwd", grid=(nb,),
        in_specs=[qk_spec, qk_spec, v_spec, g_spec, g_spec, s_spec, v_spec],
        out_specs=[qk_spec, qk_spec, v_spec, g_spec, g_spec],
        out_shape=[jax.ShapeDtypeStruct((GDN_QK_HEADS, T, GDN_HD), F32),
                   jax.ShapeDtypeStruct((GDN_QK_HEADS, T, GDN_HD), F32),
                   jax.ShapeDtypeStruct((GDN_V_HEADS, T, GDN_HD), F32),
                   jax.ShapeDtypeStruct(g.shape, F32), jax.ShapeDtypeStruct(g.shape, F32)],
        scratch_shapes=[pltpu.VMEM((GDN_V_HEADS, GDN_HD, GDN_HD), F32)],
        compiler_params=_params(("arbitrary",)),
    )(qn, kn, v, g, beta, sall, do)


def _gnorm_core(o, z, w):
    return tuple(_rms_core(oh, w) * _silu(zh) for oh, zh in zip(o, z))


def _gnorm_fwd(o, pm, w, tm=256):
    T = pm.shape[0]
    tm = min(tm, T)

    def fn(i, ov, zv, wv):
        zf = zv.astype(F32)
        out = _gnorm_core(tuple(ov[h].astype(F32) for h in range(GDN_V_HEADS)), _heads(zf, GDN_V_HEADS), wv)
        return (jnp.concatenate(out, axis=1),)

    ins = [(o, "row", None), (pm, "row", (GDN_V, 2)), (w, "full", None)]
    return _rowcall("gnorm_fwd", fn, T, tm, ins, [((T, GDN_V), BF16, "row")])[0]


def _gnorm_bwd(o, pm, w, don, tm=128):
    T = pm.shape[0]
    tm = min(tm, T)

    def fn(i, ov, zv, wv, dv):
        zf, df = zv.astype(F32), dv.astype(F32)
        _, vjp = jax.vjp(_gnorm_core, tuple(ov[h].astype(F32) for h in range(GDN_V_HEADS)),
                         _heads(zf, GDN_V_HEADS), wv)
        do, dz, dw = vjp(_heads(df, GDN_V_HEADS))
        return jnp.stack(do), jnp.concatenate(dz, axis=1), dw

    ins = [(o, "row", None), (pm, "row", (GDN_V, 2)), (w, "full", None), (don, "row", None)]
    outs = [((GDN_V_HEADS, T, GDN_HD), BF16, "row"), ((T, GDN_V), BF16, "row"), ((1, GDN_HD), F32, "acc")]
    return _rowcall("gnorm_bwd", fn, T, tm, ins, outs)


def _ffn_act_fwd(name, up, conv_w, conv_b, tm=128):
    T = up.shape[0]
    tm = min(tm, T)

    def fn(i, x, halo, cw, cb):
        u = _conv_fwd(x.astype(F32), halo.astype(F32), cw, i) + cb
        return (_silu(u[:, :DFF]) * u[:, DFF:],)

    ins = [(up, "row", None), (up, "prev", None), (conv_w, "full", None), (conv_b, "full", None)]
    return _rowcall(name, fn, T, tm, ins, [((T, DFF), BF16, "row")])[0]


def _ffn_act_bwd(name, up, conv_w, conv_b, dact, tm=128):
    T = up.shape[0]
    tm = min(tm, T)

    def fn(i, x, halo, cw, cb, da):
        xf, hf, da = x.astype(F32), halo.astype(F32), da.astype(F32)
        u = _conv_fwd(xf, hf, cw, i) + cb
        gate, val = u[:, :DFF], u[:, DFF:]
        sg = _sigmoid(gate)
        dgate = da * val * sg * (1.0 + gate * (1.0 - sg))
        dval = da * gate * sg
        du = jnp.concatenate([dgate, dval], axis=1)
        return du, _conv_dw(du, xf, hf, 3, i), jnp.sum(du, axis=0, keepdims=True)

    ins = [(up, "row", None), (up, "prev", None), (conv_w, "full", None), (conv_b, "full", None),
           (dact, "row", None)]
    outs = [((T, DFF2), BF16, "row"), ((SUBLANE, DFF2), F32, "acc"), ((1, DFF2), F32, "acc")]
    return _rowcall(name, fn, T, tm, ins, outs)


def _ffn_conv_bwd(name, du, conv_w, tm=256):
    T = du.shape[0]
    tm = min(tm, T)
    n = T // tm

    def fn(i, dv, halo, cw):
        return (_conv_dx(dv.astype(F32), halo.astype(F32), cw, i, n),)

    ins = [(du, "row", None), (du, "next", None), (conv_w, "full", None)]
    return _rowcall(name, fn, T, tm, ins, [((T, DFF2), BF16, "row")])[0]


GROUP_ROWS = SWA_GROUP * SWA_BLOCK


def _attn_core(q, kp, kc, vp, vc, bias, sink, mask):
    kcat = jnp.concatenate([kp, kc], axis=0)
    vcat = jnp.concatenate([vp, vc], axis=0)
    s = _bdot(q * (SWA_HD ** -0.5), kcat, NT) + bias
    s = jnp.where(mask, s, NEG_INF)
    m = lax.stop_gradient(jnp.maximum(jnp.max(s, axis=-1, keepdims=True), sink))
    p = jnp.exp(s - m)
    denom = jnp.sum(p, axis=-1, keepdims=True) + jnp.exp(sink - m)
    return _bdot(p / denom, vcat)


def _attn_mask(i):
    qi = lax.broadcasted_iota(jnp.int32, (GROUP_ROWS, 2 * SWA_BLOCK), 0) & (SWA_BLOCK - 1)
    ki = lax.broadcasted_iota(jnp.int32, (GROUP_ROWS, 2 * SWA_BLOCK), 1)
    dist = qi + SWA_BLOCK - ki
    return (dist >= 0) & (dist < SWA_BLOCK) & ((ki >= SWA_BLOCK) | (i > 0))


def _attn_operands(q_ref, kc_ref, kp_ref, vc_ref, vp_ref, b_ref, s_ref):
    sink = jnp.concatenate([jnp.broadcast_to(s_ref[g:g + 1, 0:1], (SWA_BLOCK, 1)) for g in range(SWA_GROUP)], axis=0)
    return (q_ref[...].astype(F32).reshape(GROUP_ROWS, SWA_HD), kp_ref[...].astype(F32), kc_ref[...].astype(F32),
            vp_ref[...].astype(F32), vc_ref[...].astype(F32), b_ref[...].reshape(GROUP_ROWS, 2 * SWA_BLOCK), sink)


def _attn_fwd(q, k, v, bias, sinks):
    T = q.shape[1]
    nb = T // SWA_BLOCK

    def body(q_ref, kc_ref, kp_ref, vc_ref, vp_ref, b_ref, s_ref, o_ref):
        i = pl.program_id(1)
        out = _attn_core(*_attn_operands(q_ref, kc_ref, kp_ref, vc_ref, vp_ref, b_ref, s_ref), _attn_mask(i))
        o_ref[...] = out.reshape(SWA_GROUP, SWA_BLOCK, SWA_HD).astype(o_ref.dtype)

    q_spec = pl.BlockSpec((SWA_GROUP, SWA_BLOCK, SWA_HD), lambda j, i: (j, i, 0))
    cur = pl.BlockSpec((None, SWA_BLOCK, SWA_HD), lambda j, i: (j, i, 0))
    prev = pl.BlockSpec((None, SWA_BLOCK, SWA_HD), lambda j, i: (j, jnp.maximum(i - 1, 0), 0))
    return pl.pallas_call(
        body, name="attn_fwd", grid=(SWA_KV_HEADS, nb),
        in_specs=[q_spec, cur, prev, cur, prev,
                  pl.BlockSpec((SWA_GROUP, SWA_BLOCK, 2 * SWA_BLOCK), lambda j, i: (j, 0, 0)),
                  pl.BlockSpec((None, SWA_GROUP, LANE), lambda j, i: (j, 0, 0))],
        out_specs=q_spec, out_shape=jax.ShapeDtypeStruct(q.shape, BF16),
        compiler_params=_params(("parallel", "arbitrary")),
    )(q, k, k, v, v, bias, sinks)


def _attn_bwd(q, k, v, bias, sinks, do):
    T = q.shape[1]
    nb = T // SWA_BLOCK

    def body(q_ref, kc_ref, kp_ref, vc_ref, vp_ref, b_ref, s_ref, do_ref,
             dq_ref, dk_ref, dv_ref, db_ref, dsk_ref, kcar, vcar):
        i = pl.program_id(1)

        @pl.when(i < nb)
        def _():
            prim = _attn_operands(q_ref, kc_ref, kp_ref, vc_ref, vp_ref, b_ref, s_ref)
            _, vjp = jax.vjp(functools.partial(_attn_core, mask=_attn_mask(i)), *prim)
            dq, dkp, dkc, dvp, dvc, db, dsc = vjp(do_ref[...].astype(F32).reshape(GROUP_ROWS, SWA_HD))
            dq_ref[...] = dq.reshape(SWA_GROUP, SWA_BLOCK, SWA_HD).astype(dq_ref.dtype)
            db = db.reshape(SWA_GROUP, SWA_BLOCK, 2 * SWA_BLOCK)
            dsk = jnp.concatenate(
                [jnp.broadcast_to(jnp.sum(dsc[g * SWA_BLOCK:(g + 1) * SWA_BLOCK], axis=0, keepdims=True), (1, LANE))
                 for g in range(SWA_GROUP)], axis=0)

            @pl.when(i == 0)
            def _():
                db_ref[...] = db
                dsk_ref[...] = dsk

            @pl.when(i > 0)
            def _():
                db_ref[...] += db
                dsk_ref[...] += dsk
                dk_ref[...] = (kcar[...] + dkp).astype(dk_ref.dtype)
                dv_ref[...] = (vcar[...] + dvp).astype(dv_ref.dtype)

            kcar[...] = dkc
            vcar[...] = dvc

        @pl.when(i == nb)
        def _():
            dk_ref[...] = kcar[...].astype(dk_ref.dtype)
            dv_ref[...] = vcar[...].astype(dv_ref.dtype)

    last = nb - 1
    q_spec = pl.BlockSpec((SWA_GROUP, SWA_BLOCK, SWA_HD), lambda j, i: (j, jnp.minimum(i, last), 0))
    cur = pl.BlockSpec((None, SWA_BLOCK, SWA_HD), lambda j, i: (j, jnp.minimum(i, last), 0))
    prev = pl.BlockSpec((None, SWA_BLOCK, SWA_HD), lambda j, i: (j, jnp.clip(i - 1, 0, last), 0))
    b_spec = pl.BlockSpec((SWA_GROUP, SWA_BLOCK, 2 * SWA_BLOCK), lambda j, i: (j, 0, 0))
    s_spec = pl.BlockSpec((None, SWA_GROUP, LANE), lambda j, i: (j, 0, 0))
    return pl.pallas_call(
        body, name="attn_bwd", grid=(SWA_KV_HEADS, nb + 1),
        in_specs=[q_spec, cur, prev, cur, prev, b_spec, s_spec, q_spec],
        out_specs=[q_spec, prev, prev, b_spec, s_spec],
        out_shape=[jax.ShapeDtypeStruct(q.shape, BF16), jax.ShapeDtypeStruct(k.shape, BF16),
                   jax.ShapeDtypeStruct(k.shape, BF16), jax.ShapeDtypeStruct(bias.shape, F32),
                   jax.ShapeDtypeStruct(sinks.shape, F32)],
        scratch_shapes=[pltpu.VMEM((SWA_BLOCK, SWA_HD), F32), pltpu.VMEM((SWA_BLOCK, SWA_HD), F32)],
        compiler_params=_params(("parallel", "arbitrary")),
    )(q, k, k, v, v, bias, sinks, do)


def _rel_onehot():
    qi = jnp.arange(SWA_BLOCK)[:, None]
    ki = jnp.arange(2 * SWA_BLOCK)[None, :]
    n = jnp.maximum(qi + SWA_BLOCK - ki, 0)
    max_exact = REL_BUCKETS // 2
    nf = jnp.maximum(n, 1).astype(F32)
    large = max_exact + (jnp.log(nf / max_exact) / math.log(REL_MAX_DISTANCE / max_exact)
                         * (REL_BUCKETS - max_exact)).astype(jnp.int32)
    bucket = jnp.where(n < max_exact, n, jnp.minimum(large, REL_BUCKETS - 1)).reshape(-1)
    return (bucket[None, :] == jnp.arange(REL_BUCKETS)[:, None]).astype(F32)


def _final(h, w, target, tm=256):
    T = h.shape[0]
    tm = min(tm, T)

    def fn(i, hv, wv, tv):
        y, vjp = jax.vjp(_rms_core, hv, wv)
        err = y - tv
        dh, dw = vjp(err * (1.0 / D))
        part = 0.5 * jnp.sum(jnp.sum(err * err, axis=1, keepdims=True) * (1.0 / D), axis=0, keepdims=True)
        return jnp.broadcast_to(part, (SUBLANE, LANE)), dh, dw

    ins = [(h, "row", None), (w, "full", None), (target, "row", None)]
    outs = [((SUBLANE, LANE), F32, "acc"), ((T, D), F32, "row"), ((1, D), F32, "acc")]
    return _rowcall("final", fn, T, tm, ins, outs)


def _heads_major(a, heads, hd):
    return a.reshape(a.shape[0], heads, hd).transpose(1, 0, 2)


def _heads_minor(a):
    return a.transpose(1, 0, 2).reshape(a.shape[1], a.shape[0] * a.shape[2])


def _ffn_fwd(tag, h, P, layer):
    n = _rms_fwd(f"{tag}_rms", h, P["ffn_norm_w"][layer:layer + 1])
    up = _mm_up(f"{tag}_up", n, P["w_up"], layer)
    act = _ffn_act_fwd(f"{tag}_act", up, P["ffn_conv_w"][layer], P["ffn_conv_b"][layer:layer + 1])
    out = _mm_nn(f"{tag}_down", act, P["w_down"][layer], F32, res=h)
    return out, (n, up, act)


def _ffn_bwd(tag, h, saved, dout, P, layer):
    n, up, act = saved
    cw, cb = P["ffn_conv_w"][layer], P["ffn_conv_b"][layer:layer + 1]
    dact = _mm_nt(f"{tag}_down_dx", dout, P["w_down"][layer], BF16)
    g_down = _mm_tn(f"{tag}_down_dw", act, dout)
    du, dcw, dcb = _ffn_act_bwd(f"{tag}_act_bwd", up, cw, cb, dact)
    dup = _ffn_conv_bwd(f"{tag}_conv_bwd", du, cw)
    g_up = _mm_up_tn(f"{tag}_up_dw", n, dup)
    dn = _mm_up_nt(f"{tag}_up_dx", dup, P["w_up"], layer)
    dh, dnw = _rms_bwd(f"{tag}_rms_bwd", h, [(P["ffn_norm_w"][layer:layer + 1], dn)], [dout])
    return dh, dict(w_down=g_down, w_up=g_up, conv_w=dcw[:3], conv_b=dcb, norm_w=dnw)


def _local_step(x, target, P):
    T = x.shape[0]
    n0 = _rms_fwd("a_rms", x, P["a_norm_w"])
    pm = _mm_nn("gdn_in", n0, P["w_in_main"], BF16)
    pba = _mm_nn("gdn_in_ba", n0, P["w_in_ba"], F32)
    qn, kn, v, beta, g = _gdn_pre_fwd(pm, pba, P["a_conv_w"], P["a_log"], P["dt_bias"])
    g_rows, beta_rows = _gate_rows(g), _gate_rows(beta)
    o, sall = _gdn_fwd(qn, kn, v, g_rows, beta_rows)
    on = _gnorm_fwd(o, pm, P["a_out_norm_w"])
    h1 = _mm_nn("gdn_out", on, P["w_out"], F32, res=x)
    h2, ffn0 = _ffn_fwd("ffn0", h1, P, 0)
    nkv = _rms_fwd("kv_rms", h2, P["kv_norm_w"])
    kv = _mm_nn("kv_proj", nkv, P["w_kv"], BF16)
    nb = _rms_fwd("b_rms", h2, P["b_norm_w"])
    qp = _mm_nn("q_proj", nb, P["w_q"], BF16)
    q3 = _heads_major(qp, SWA_Q_HEADS, SWA_HD)
    k3 = _heads_major(kv[:, :SWA_KV_HEADS * SWA_HD], SWA_KV_HEADS, SWA_HD)
    v3 = _heads_major(kv[:, SWA_KV_HEADS * SWA_HD:], SWA_KV_HEADS, SWA_HD)
    onehot = _rel_onehot()
    bias = _mm_nn("rel_bias", P["rel_table_t"], onehot, F32, precision=HIGHEST)
    bias = bias.reshape(SWA_Q_HEADS, SWA_BLOCK, 2 * SWA_BLOCK)
    oa = _heads_minor(_attn_fwd(q3, k3, v3, bias, P["sinks"]))
    h3 = _mm_nn("o_proj", oa, P["w_o"], F32, res=h2)
    h4, ffn1 = _ffn_fwd("ffn1", h3, P, 1)
    loss, dh4, d_final = _final(h4, P["final_norm_w"], target)

    dh3, gf1 = _ffn_bwd("ffn1", h3, ffn1, dh4, P, 1)
    doa = _mm_nt("o_proj_dx", dh3, P["w_o"], BF16)
    g_wo = _mm_tn("o_proj_dw", oa, dh3)
    dq3, dk3, dv3, dbias, dsinks = _attn_bwd(q3, k3, v3, bias, P["sinks"], _heads_major(doa, SWA_Q_HEADS, SWA_HD))
    dqp = _heads_minor(dq3)
    dkv = jnp.concatenate([_heads_minor(dk3), _heads_minor(dv3)], axis=1)
    g_wq = _mm_tn("q_proj_dw", nb, dqp)
    dnb = _mm_nt("q_proj_dx", dqp, P["w_q"], F32)
    g_wkv = _mm_tn("kv_proj_dw", nkv, dkv)
    dnkv = _mm_nt("kv_proj_dx", dkv, P["w_kv"], F32)
    dh2, d_bnorm, d_kvnorm = _rms_bwd("b_kv_rms_bwd", h2, [(P["b_norm_w"], dnb), (P["kv_norm_w"], dnkv)], [dh3])
    g_table = _mm_nt("rel_bias_dw", onehot, dbias.reshape(SWA_Q_HEADS, -1), F32, precision=HIGHEST)
    dh1, gf0 = _ffn_bwd("ffn0", h1, ffn0, dh2, P, 0)
    don = _mm_nt("gdn_out_dx", dh1, P["w_out"], BF16)
    g_wout = _mm_tn("gdn_out_dw", on, dh1)
    do, dz, d_gnorm = _gnorm_bwd(o, pm, P["a_out_norm_w"], don)
    dq, dk, dv, dg, dbeta = _gdn_bwd(qn, kn, v, g_rows, beta_rows, sall, do)
    dy, dpba, d_aconv, d_alog, d_dtb = _gdn_pre_bwd(pm, pba, P["a_conv_w"], P["a_log"], P["dt_bias"],
                                                    dq, dk, dv, _gate_cols(dbeta), _gate_cols(dg))
    dpm = _gdn_conv_bwd(dy, dz, P["a_conv_w"])
    g_win_main = _mm_tn("gdn_in_dw", n0, dpm)
    g_win_ba = _mm_tn("gdn_in_ba_dw", n0, dpba)
    dn0 = _mm_nt("gdn_in_dx", dpm, P["w_in_main"], F32)
    dn0 = _mm_nt("gdn_in_ba_dx", dpba, P["w_in_ba"], F32, res=dn0)
    dx, d_anorm = _rms_bwd("a_rms_bwd", x, [(P["a_norm_w"], dn0)], [dh1])

    nh = GDN_V_HEADS
    grads = dict(
        a_norm_w=d_anorm,
        a_w_in=jnp.concatenate([g_win_main, g_win_ba[:, :nh], g_win_ba[:, LANE:LANE + nh]], axis=1),
        a_conv_w=d_aconv[:4], a_a_log=d_alog[:, :nh], a_dt_bias=d_dtb[:, :nh], a_out_norm_w=d_gnorm,
        a_w_out=g_wout, kv_norm_w=d_kvnorm, w_kv=g_wkv, b_norm_w=d_bnorm, b_w_q=g_wq,
        b_sinks=dsinks[:, :, 0].reshape(1, SWA_Q_HEADS), b_w_o=g_wo, rel_bias_table=g_table,
        ffn_norm_w=jnp.concatenate([gf0["norm_w"], gf1["norm_w"]], axis=0),
        ffn_w_up=jnp.stack([gf0["w_up"], gf1["w_up"]], axis=1),
        ffn_conv_w=jnp.stack([gf0["conv_w"], gf1["conv_w"]], axis=0),
        ffn_conv_b=jnp.concatenate([gf0["conv_b"], gf1["conv_b"]], axis=0),
        ffn_w_down=jnp.stack([gf0["w_down"], gf1["w_down"]], axis=0), final_norm_w=d_final,
    )
    return loss, dx, grads


HBM_SPEC = pl.BlockSpec(memory_space=pltpu.HBM)
VMEM_SPEC = pl.BlockSpec(memory_space=pltpu.VMEM)


def _coords():
    return lax.axis_index("x"), lax.axis_index("y"), lax.axis_index("c")


def _remote(src, dst, send_sem, recv_sem, device):
    return pltpu.make_async_remote_copy(src_ref=src, dst_ref=dst, send_sem=send_sem, recv_sem=recv_sem,
                                        device_id=device, device_id_type=MESH)


def _other_chips(x, y):
    return [(1 - x, y), (x, 1 - y), (1 - x, 1 - y)]


def _all_gather(arrs, split):
    n = len(arrs)

    def body(*refs):
        ins, outs = refs[:n], refs[n:2 * n]
        send_sems, recv_sems = refs[2 * n:]
        x, y, c = _coords()
        p = 2 * x + y
        chips = _other_chips(x, y)

        def rows(a, half):
            h = arrs[a].shape[0] // 2
            return pl.ds(half * h, h)

        sends = []
        for a in range(n):
            for j, chip in enumerate(chips):
                if split[a]:
                    src, dst = ins[a].at[rows(a, c)], outs[a].at[p, rows(a, c)]
                else:
                    src, dst = ins[a], outs[a].at[p]
                cp = _remote(src, dst, send_sems.at[6 * a + j], recv_sems.at[6 * a + j], (*chip, c))
                cp.start()
                sends.append(cp)
        for a in range(n):
            for j, chip in enumerate(chips):
                q = 2 * chip[0] + chip[1]
                land = outs[a].at[q, rows(a, c)] if split[a] else outs[a].at[q]
                _remote(land, land, send_sems.at[6 * a + j], recv_sems.at[6 * a + j], (*chip, c)).wait_recv()
                if split[a]:
                    fw = _remote(land, land, send_sems.at[6 * a + 3 + j], recv_sems.at[6 * a + 3 + j], (x, y, 1 - c))
                    fw.start()
                    sends.append(fw)
        for a in range(n):
            if split[a]:
                for j, chip in enumerate(chips):
                    q = 2 * chip[0] + chip[1]
                    land = outs[a].at[q, rows(a, 1 - c)]
                    _remote(land, land, send_sems.at[6 * a + 3 + j], recv_sems.at[6 * a + 3 + j],
                            (x, y, 1 - c)).wait_recv()
        for cp in sends:
            cp.wait_send()

    res = pl.pallas_call(
        body, name="weights_all_gather", in_specs=[HBM_SPEC] * n, out_specs=[HBM_SPEC] * n,
        out_shape=[jax.ShapeDtypeStruct((N_CHIPS,) + a.shape, a.dtype) for a in arrs],
        scratch_shapes=[pltpu.SemaphoreType.DMA((6 * n,)), pltpu.SemaphoreType.DMA((6 * n,))],
    )(*arrs)
    chip = 2 * lax.axis_index("x") + lax.axis_index("y")
    return [lax.dynamic_update_index_in_dim(r, a, chip, 0) for r, a in zip(res, arrs)]


PAIR_SWAP_PIECES = 2


def _pair_swap(gs):
    n = len(gs)

    def body(*refs):
        ins, other = refs[:n], refs[n:2 * n]
        send_sems, recv_sems = refs[2 * n:]
        x, y, c = _coords()
        cps = []
        for a in range(n):
            h = gs[a].shape[1] // 2
            piece = h // PAIR_SWAP_PIECES
            for q in range(N_CHIPS):
                for r in range(PAIR_SWAP_PIECES):
                    k = (a * N_CHIPS + q) * PAIR_SWAP_PIECES + r
                    cp = _remote(ins[a].at[q, pl.ds((1 - c) * h + r * piece, piece)],
                                 other[a].at[q, pl.ds(r * piece, piece)], send_sems.at[k], recv_sems.at[k],
                                 (x, y, 1 - c))
                    cp.start()
                    cps.append(cp)
        for cp in cps:
            cp.wait()

    half = [jax.ShapeDtypeStruct((N_CHIPS, g.shape[1] // 2, g.shape[2]), g.dtype) for g in gs]
    nsem = n * N_CHIPS * PAIR_SWAP_PIECES
    return pl.pallas_call(
        body, name="grads_pair_swap", in_specs=[HBM_SPEC] * n, out_specs=[HBM_SPEC] * n, out_shape=half,
        scratch_shapes=[pltpu.SemaphoreType.DMA((nsem,)), pltpu.SemaphoreType.DMA((nsem,))],
    )(*gs)


def _chip_scatter(ps):
    n = len(ps)

    def body(*refs):
        ins, outs = refs[:n], refs[n:2 * n]
        send_sems, recv_sems = refs[2 * n:]
        x, y, c = _coords()
        chips = _other_chips(x, y)
        sends = []
        for a in range(n):
            for j, chip in enumerate(chips):
                q = 2 * chip[0] + chip[1]
                cp = _remote(ins[a].at[q], outs[a].at[j], send_sems.at[3 * a + j], recv_sems.at[3 * a + j], (*chip, c))
                cp.start()
                sends.append(cp)
        for cp in sends:
            cp.wait_recv()
        for cp in sends:
            cp.wait_send()

    return pl.pallas_call(
        body, name="grads_chip_scatter", in_specs=[HBM_SPEC] * n, out_specs=[HBM_SPEC] * n,
        out_shape=[jax.ShapeDtypeStruct((N_CHIPS - 1,) + a.shape[1:], a.dtype) for a in ps],
        scratch_shapes=[pltpu.SemaphoreType.DMA((3 * n,)), pltpu.SemaphoreType.DMA((3 * n,))],
    )(*ps)


def _pair_share(rs):
    n = len(rs)

    def body(*refs):
        ins, outs = refs[:n], refs[n:2 * n]
        send_sems, recv_sems = refs[2 * n:]
        x, y, c = _coords()
        sends = []
        for a in range(n):
            cp = _remote(ins[a], outs[a], send_sems.at[a], recv_sems.at[a], (x, y, 1 - c))
            cp.start()
            sends.append(cp)
        for cp in sends:
            cp.wait()

    theirs = pl.pallas_call(
        body, name="grads_pair_share", in_specs=[HBM_SPEC] * n, out_specs=[HBM_SPEC] * n,
        out_shape=[jax.ShapeDtypeStruct(a.shape, a.dtype) for a in rs],
        scratch_shapes=[pltpu.SemaphoreType.DMA((n,)), pltpu.SemaphoreType.DMA((n,))],
    )(*rs)
    c = lax.axis_index("c")
    out = []
    for mine, other in zip(rs, theirs):
        h = mine.shape[0]
        both = jnp.zeros((2 * h, mine.shape[1]), mine.dtype)
        both = lax.dynamic_update_slice_in_dim(both, mine, c * h, 0)
        out.append(lax.dynamic_update_slice_in_dim(both, other, (1 - c) * h, 0))
    return out


def _small_all_reduce(buf):
    R = buf.shape[0]
    ndev = 2 * N_CHIPS

    def body(in_ref, out_ref, gath, send_sems, recv_sems):
        x, y, c = _coords()
        me = 4 * x + 2 * y + c
        gath[me] = in_ref[...]
        peers = []
        for d in range(1, ndev):
            px = 1 - x if d & 4 else x
            py = 1 - y if d & 2 else y
            pc = 1 - c if d & 1 else c
            peers.append((px, py, pc))
        sends = []
        for d, peer in enumerate(peers):
            cp = _remote(in_ref, gath.at[me], send_sems.at[d], recv_sems.at[d], peer)
            cp.start()
            sends.append(cp)
        for d, peer in enumerate(peers):
            land = gath.at[4 * peer[0] + 2 * peer[1] + peer[2]]
            _remote(land, land, send_sems.at[d], recv_sems.at[d], peer).wait_recv()
        for cp in sends:
            cp.wait_send()
        acc = gath[0]
        for s in range(1, ndev):
            acc = acc + gath[s]
        out_ref[...] = acc

    return pl.pallas_call(
        body, name="small_all_reduce", in_specs=[VMEM_SPEC], out_specs=VMEM_SPEC,
        out_shape=jax.ShapeDtypeStruct(buf.shape, F32),
        scratch_shapes=[pltpu.VMEM((ndev, R, LANE), F32), pltpu.SemaphoreType.DMA((ndev - 1,)),
                        pltpu.SemaphoreType.DMA((ndev - 1,))],
    )(buf)


def _pair_add(name, own, other):
    h = own.shape[1]
    tm = _tile(h, (128, 64, 32, 16))

    def fn(i, a, b):
        return (a + b,)

    return _rowcall(name, fn, h, tm, [(own, "row", None), (other, "row", None)], [(own.shape, BF16, "row")])[0]


def _chip_add(name, own, parts):
    h = parts.shape[1]
    tm = _tile(h, (128, 64, 32, 16))

    def fn(i, o, a):
        a = a.astype(F32)
        return (((o.astype(F32) + a[0]) + a[1]) + a[2],)

    return _rowcall(name, fn, h, tm, [(own, "row", None), (parts, "row", None)], [(parts.shape[1:], F32, "row")])[0]


def _adamw(name, w, g, m, v):
    R = w.shape[0]
    tm = _tile(R, (256, 128, 64, 32, 16, 8))

    def fn(i, wv, gv, mv, vv):
        m2 = ADAM_B1 * mv + (1.0 - ADAM_B1) * gv
        v2 = ADAM_B2 * vv + (1.0 - ADAM_B2) * (gv * gv)
        m_hat = m2 / (1.0 - ADAM_B1 ** ADAM_STEP)
        v_hat = v2 / (1.0 - ADAM_B2 ** ADAM_STEP)
        delta = -ADAM_LR * (m_hat / (jnp.sqrt(v_hat) + ADAM_EPS) + ADAM_WD * wv)
        return delta, m2, v2

    ins = [(a, "row", None) for a in (w, g, m, v)]
    return _rowcall(name, fn, R, tm, ins, [(w.shape, F32, "row")] * 3)


def _pack(arrs):
    flat = jnp.concatenate([a.reshape(-1).astype(F32) for a in arrs])
    size = flat.shape[0]
    padded = -(-size // (SUBLANE * LANE)) * SUBLANE * LANE
    return jnp.pad(flat, (0, padded - size)).reshape(-1, LANE)


def _unpack(buf, shapes):
    flat = buf.reshape(-1)
    out, off = [], 0
    for s in shapes:
        size = math.prod(s)
        out.append(flat[off:off + size].reshape(s))
        off += size
    return out


BIG = ("a_w_in", "a_w_out", "w_kv", "b_w_q", "b_w_o", "ffn_w_up", "ffn_w_down")
WEIGHTS = ("a_norm_w", "a_w_in", "a_conv_w", "a_a_log", "a_dt_bias", "a_out_norm_w", "a_w_out", "kv_norm_w", "w_kv",
           "b_norm_w", "b_w_q", "b_sinks", "b_w_o", "rel_bias_table", "ffn_norm_w", "ffn_w_up", "ffn_conv_w",
           "ffn_conv_b", "ffn_w_down", "final_norm_w")
SMALL = tuple(n for n in WEIGHTS if n not in BIG)
SMALL_SHARDED = {"a_norm_w": 1, "a_conv_w": 2, "ffn_conv_w": 2}


def _quarter_2d(name, a):
    if name in ("ffn_w_up", "ffn_w_down"):
        return a.reshape(a.shape[0] * a.shape[1], a.shape[2])
    return a.reshape(a.shape[-2], a.shape[-1])


def _whole_weights(w):
    bigs = [_quarter_2d(n, w[n]).astype(BF16) for n in BIG]
    smalls = [w["a_norm_w"], w["a_conv_w"][0], w["ffn_conv_w"].reshape(6, DFF2_SHARD)]
    g = _all_gather(bigs + smalls, [True] * len(bigs) + [False] * len(smalls))
    w_in = g[0].transpose(1, 0, 2).reshape(D, GDN_IN)
    nh = GDN_V_HEADS
    zpad = jnp.zeros((D, LANE - nh), BF16)
    w_in_ba = jnp.concatenate([w_in[:, GDN_MAIN:GDN_MAIN + nh], zpad, w_in[:, GDN_MAIN + nh:], zpad], axis=1)
    lane_pad = lambda a: jnp.pad(a, ((0, 0), (0, LANE - nh)))
    return dict(
        a_norm_w=g[7].reshape(1, D), w_in_main=w_in[:, :GDN_MAIN], w_in_ba=w_in_ba,
        a_conv_w=g[8].transpose(1, 0, 2).reshape(4, GDN_CONV), a_log=lane_pad(w["a_a_log"]),
        dt_bias=lane_pad(w["a_dt_bias"]), a_out_norm_w=w["a_out_norm_w"], w_out=g[1].reshape(GDN_V, D),
        kv_norm_w=w["kv_norm_w"].reshape(1, D), w_kv=g[2].reshape(D, 2 * SWA_KV_HEADS * SWA_HD),
        b_norm_w=w["b_norm_w"], w_q=g[3].reshape(D, D), w_o=g[4].reshape(D, D),
        sinks=jnp.broadcast_to(w["b_sinks"].reshape(SWA_KV_HEADS, SWA_GROUP, 1), (SWA_KV_HEADS, SWA_GROUP, LANE)),
        rel_table_t=w["rel_bias_table"].T, ffn_norm_w=w["ffn_norm_w"],
        w_up=g[5].reshape(N_CHIPS, 2, D, DFF2_SHARD),
        ffn_conv_w=g[9].reshape(N_CHIPS, 2, 3, DFF2_SHARD).transpose(1, 2, 0, 3).reshape(2, 3, DFF2),
        ffn_conv_b=w["ffn_conv_b"],
        w_down=g[6].reshape(N_CHIPS, 2, DFF_SHARD, D).transpose(1, 0, 2, 3).reshape(2, DFF, D),
        final_norm_w=w["final_norm_w"].reshape(1, D),
    )


def _chip_major(name, g):
    if name == "a_w_in":
        return g.reshape(D, N_CHIPS, GDN_IN_SHARD).transpose(1, 0, 2)
    if name == "ffn_w_up":
        return g.reshape(N_CHIPS, 2 * D, DFF2_SHARD)
    if name == "ffn_w_down":
        return g.reshape(2, N_CHIPS, DFF_SHARD, D).transpose(1, 0, 2, 3).reshape(N_CHIPS, 2 * DFF_SHARD, D)
    return g.reshape(N_CHIPS, g.shape[0] // N_CHIPS, g.shape[1])


def kernel(x, a_norm_w, a_w_in, a_conv_w, a_a_log, a_dt_bias, a_out_norm_w, a_w_out, kv_norm_w, w_kv, b_norm_w, b_w_q, b_sinks, b_w_o, rel_bias_table, ffn_norm_w, ffn_w_up, ffn_conv_w, ffn_conv_b, ffn_w_down, final_norm_w, loss_target, m_a_norm_w, m_a_w_in, m_a_conv_w, m_a_a_log, m_a_dt_bias, m_a_out_norm_w, m_a_w_out, m_kv_norm_w, m_w_kv, m_b_norm_w, m_b_w_q, m_b_sinks, m_b_w_o, m_rel_bias_table, m_ffn_norm_w, m_ffn_w_up, m_ffn_conv_w, m_ffn_conv_b, m_ffn_w_down, m_final_norm_w, v_a_norm_w, v_a_w_in, v_a_conv_w, v_a_a_log, v_a_dt_bias, v_a_out_norm_w, v_a_w_out, v_kv_norm_w, v_w_kv, v_b_norm_w, v_b_w_q, v_b_sinks, v_b_w_o, v_rel_bias_table, v_ffn_norm_w, v_ffn_w_up, v_ffn_conv_w, v_ffn_conv_b, v_ffn_w_down, v_final_norm_w):
    w = dict(zip(WEIGHTS, (a_norm_w, a_w_in, a_conv_w, a_a_log, a_dt_bias, a_out_norm_w, a_w_out, kv_norm_w, w_kv,
                           b_norm_w, b_w_q, b_sinks, b_w_o, rel_bias_table, ffn_norm_w, ffn_w_up, ffn_conv_w,
                           ffn_conv_b, ffn_w_down, final_norm_w)))
    m = dict(zip(WEIGHTS, (m_a_norm_w, m_a_w_in, m_a_conv_w, m_a_a_log, m_a_dt_bias, m_a_out_norm_w, m_a_w_out,
                           m_kv_norm_w, m_w_kv, m_b_norm_w, m_b_w_q, m_b_sinks, m_b_w_o, m_rel_bias_table,
                           m_ffn_norm_w, m_ffn_w_up, m_ffn_conv_w, m_ffn_conv_b, m_ffn_w_down, m_final_norm_w)))
    v = dict(zip(WEIGHTS, (v_a_norm_w, v_a_w_in, v_a_conv_w, v_a_a_log, v_a_dt_bias, v_a_out_norm_w, v_a_w_out,
                           v_kv_norm_w, v_w_kv, v_b_norm_w, v_b_w_q, v_b_sinks, v_b_w_o, v_rel_bias_table,
                           v_ffn_norm_w, v_ffn_w_up, v_ffn_conv_w, v_ffn_conv_b, v_ffn_w_down, v_final_norm_w)))
    T = x.shape[1]
    chip = 2 * lax.axis_index("x") + lax.axis_index("y")

    loss_part, dx, grads = _local_step(x.reshape(T, D), loss_target.reshape(T, D), _whole_weights(w))

    core = lax.axis_index("c")
    whole = [_chip_major(n, grads[n]) for n in BIG]
    other = _pair_swap(whole)
    own = [lax.dynamic_slice_in_dim(g, core * (g.shape[1] // 2), g.shape[1] // 2, 1) for g in whole]
    pair = [_pair_add(f"pair_add_{n}", a, b) for n, a, b in zip(BIG, own, other)]
    parts = _chip_scatter(pair)
    halves = [_chip_add(f"chip_add_{n}", lax.dynamic_index_in_dim(a, chip, 0, keepdims=False), b)
              for n, a, b in zip(BIG, pair, parts)]
    quarter = _pair_share(halves)
    out_g, out_d, out_m, out_v = {}, {}, {}, {}
    for n, g2 in zip(BIG, quarter):
        res = _adamw(f"adamw_{n}", _quarter_2d(n, w[n]), g2, _quarter_2d(n, m[n]), _quarter_2d(n, v[n]))
        out_g[n] = g2.reshape(w[n].shape)
        out_d[n], out_m[n], out_v[n] = (r.reshape(w[n].shape) for r in res)

    whole = [grads[n] for n in SMALL]
    summed = _unpack(_small_all_reduce(_pack([loss_part[0:1, 0:1]] + whole)), [(1, 1)] + [a.shape for a in whole])
    loss = summed[0].reshape(())
    small_g = []
    for n, g in zip(SMALL, summed[1:]):
        if n in SMALL_SHARDED:
            axis = SMALL_SHARDED[n]
            g = g.reshape(w[n].shape[:axis] + (-1,) + w[n].shape[axis + 1:])
            size = w[n].shape[axis]
            g = lax.dynamic_slice_in_dim(g, chip * size, size, axis)
        small_g.append(g.reshape(w[n].shape))
    shapes = [w[n].shape for n in SMALL]
    res = _adamw("adamw_small", _pack([w[n] for n in SMALL]), _pack(small_g), _pack([m[n] for n in SMALL]),
                 _pack([v[n] for n in SMALL]))
    small_d, small_m, small_v = (_unpack(r, shapes) for r in res)
    for i, n in enumerate(SMALL):
        out_g[n], out_d[n], out_m[n], out_v[n] = small_g[i], small_d[i], small_m[i], small_v[i]

    return (loss, dx.reshape(x.shape), *[out_g[n] for n in WEIGHTS], *[out_d[n] for n in WEIGHTS],
            *[out_m[n] for n in WEIGHTS], *[out_v[n] for n in WEIGHTS])
```

```python
import functools
import math

import jax
import jax.numpy as jnp
from jax import lax
from jax.experimental import pallas as pl
from jax.experimental.pallas import tpu as pltpu

F32 = jnp.float32
BF16 = jnp.bfloat16
MESH = pl.DeviceIdType.MESH
HIGHEST = lax.Precision.HIGHEST

D = 1024
EPS = 1e-6
NEG_INF = -1e30
N_CHIPS = 4

GDN_QK_HEADS = 8
GDN_V_HEADS = 16
GDN_HD = 128
GDN_QK = GDN_QK_HEADS * GDN_HD
GDN_V = GDN_V_HEADS * GDN_HD
GDN_CONV = 2 * GDN_QK + GDN_V
GDN_MAIN = GDN_CONV + GDN_V
GDN_IN = GDN_MAIN + 2 * GDN_V_HEADS
GDN_IN_SHARD = GDN_IN // N_CHIPS
GDN_CHUNK = 64

SWA_Q_HEADS = 16
SWA_KV_HEADS = 4
SWA_GROUP = 4
SWA_HD = 64
SWA_BLOCK = 128
REL_BUCKETS = 32
REL_MAX_DISTANCE = 128

DFF = 2816
DFF2 = 2 * DFF
DFF2_SHARD = DFF2 // N_CHIPS
DFF_SHARD = DFF // N_CHIPS

ADAM_LR = 0.001
ADAM_B1 = 0.9
ADAM_B2 = 0.999
ADAM_EPS = 1e-08
ADAM_WD = 0.01
ADAM_STEP = 10

LANE = 128
SUBLANE = 8
VMEM_LIMIT = 56 * 1024 * 1024


def _params(sem, vmem=VMEM_LIMIT):
    return pltpu.CompilerParams(dimension_semantics=sem, vmem_limit_bytes=vmem)


def _rowcall(name, fn, T, tm, ins, outs):
    n = T // tm
    r8 = tm // SUBLANE
    last8 = T // SUBLANE - 1
    arrays, in_specs = [], []
    for arr, kind, cols in ins:
        arrays.append(arr)
        if kind == "full":
            in_specs.append(pl.BlockSpec(arr.shape, functools.partial(lambda nd, i: (0,) * nd, arr.ndim)))
        elif arr.ndim == 2:
            w, ci = cols if cols is not None else (arr.shape[1], 0)
            if kind == "row":
                in_specs.append(pl.BlockSpec((tm, w), functools.partial(lambda ci, i: (i, ci), ci)))
            elif kind == "prev":
                in_specs.append(pl.BlockSpec(
                    (SUBLANE, w), functools.partial(lambda ci, i: (jnp.maximum(i * r8 - 1, 0), ci), ci)))
            else:
                in_specs.append(pl.BlockSpec(
                    (SUBLANE, w), functools.partial(lambda ci, i: (jnp.minimum((i + 1) * r8, last8), ci), ci)))
        else:
            lead = arr.shape[:-2]
            in_specs.append(pl.BlockSpec(lead + (tm, arr.shape[-1]),
                                         functools.partial(lambda nl, i: (0,) * nl + (i, 0), len(lead))))
    out_shape, out_specs = [], []
    for shape, dtype, kind in outs:
        out_shape.append(jax.ShapeDtypeStruct(shape, dtype))
        if kind == "acc":
            out_specs.append(pl.BlockSpec(shape, functools.partial(lambda nd, i: (0,) * nd, len(shape))))
        else:
            lead = shape[:-2]
            out_specs.append(pl.BlockSpec(lead + (tm, shape[-1]),
                                          functools.partial(lambda nl, i: (0,) * nl + (i, 0), len(lead))))
    nin = len(arrays)

    def body(*refs):
        i = pl.program_id(0)
        vals = [r[...] for r in refs[:nin]]
        res = fn(i, *vals)
        for (shape, dtype, kind), o, r in zip(outs, refs[nin:], res):
            if kind == "row":
                o[...] = r.astype(dtype)
            else:
                @pl.when(i == 0)
                def _():
                    o[...] = r.astype(dtype)

                @pl.when(i > 0)
                def _():
                    o[...] += r.astype(dtype)

    res = pl.pallas_call(
        body, name=name, grid=(n,), in_specs=in_specs, out_specs=out_specs, out_shape=out_shape,
        compiler_params=_params(("arbitrary",)),
    )(*arrays)
    return res


def _mm(name, a, b, out_shape, out_dtype, grid, a_spec, b_spec, o_spec, dims, acc_shape, res=None, precision=None):
    nk = grid[2]

    def body(*refs):
        if res is not None:
            a_ref, b_ref, r_ref, o_ref = refs[:4]
        else:
            a_ref, b_ref, o_ref = refs[:3]
        av, bv = a_ref[...], b_ref[...]
        if precision is None:
            av, bv = av.astype(BF16), bv.astype(BF16)
        p = lax.dot_general(av, bv, (dims, ((), ())), preferred_element_type=F32, precision=precision)

        def finish(x):
            if res is not None:
                x = x + r_ref[...].astype(F32)
            o_ref[...] = x.astype(out_dtype)

        if nk == 1:
            finish(p)
        else:
            acc = refs[-1]
            k = pl.program_id(2)

            @pl.when(k == 0)
            def _():
                acc[...] = p

            @pl.when(k > 0)
            def _():
                acc[...] += p

            @pl.when(k == nk - 1)
            def _():
                finish(acc[...])

    ops = [a, b] + ([res] if res is not None else [])
    specs = [a_spec, b_spec] + ([o_spec] if res is not None else [])
    return pl.pallas_call(
        body, name=name, grid=grid, in_specs=specs, out_specs=o_spec,
        out_shape=jax.ShapeDtypeStruct(out_shape, out_dtype),
        scratch_shapes=[pltpu.VMEM(acc_shape, F32)] if nk > 1 else [],
        compiler_params=_params(("parallel", "parallel", "arbitrary")),
    )(*ops)


NN = ((1,), (0,))
NT = ((1,), (1,))
TN = ((0,), (0,))


BIG_TILES = (1024, 512, 256, 128)


def _tile(n, pref):
    for t in pref:
        if n % t == 0:
            return t
    return n


def _mm_nn(name, a, w, out_dtype, res=None, precision=None):
    M, K = a.shape
    N = w.shape[1]
    tm = _tile(M, BIG_TILES if K <= 2048 else BIG_TILES[1:])
    tn = _tile(N, BIG_TILES)
    return _mm(name, a, w, (M, N), out_dtype, (M // tm, N // tn, 1),
               pl.BlockSpec((tm, K), lambda i, j, k: (i, 0)), pl.BlockSpec((K, tn), lambda i, j, k: (0, j)),
               pl.BlockSpec((tm, tn), lambda i, j, k: (i, j)), NN, (tm, tn), res=res, precision=precision)


def _mm_nt(name, g, w, out_dtype, res=None, precision=None):
    M, N = g.shape
    K = w.shape[0]
    tm, tk = _tile(M, BIG_TILES), _tile(K, (1024, 1408, 512, 256, 128))
    tn = _tile(N, (1536,) + BIG_TILES)
    return _mm(name, g, w, (M, K), out_dtype, (M // tm, K // tk, N // tn),
               pl.BlockSpec((tm, tn), lambda i, j, k: (i, k)), pl.BlockSpec((tk, tn), lambda i, j, k: (j, k)),
               pl.BlockSpec((tm, tk), lambda i, j, k: (i, j)), NT, (tm, tk), res=res, precision=precision)


def _mm_tn(name, a, g, out_dtype=F32, precision=None):
    T, K = a.shape
    N = g.shape[1]
    tk, tn = _tile(K, (1024, 1408, 512, 256, 128)), _tile(N, BIG_TILES)
    tt = _tile(T, BIG_TILES)
    return _mm(name, a, g, (K, N), out_dtype, (K // tk, N // tn, T // tt),
               pl.BlockSpec((tt, tk), lambda i, j, k: (k, i)), pl.BlockSpec((tt, tn), lambda i, j, k: (k, j)),
               pl.BlockSpec((tk, tn), lambda i, j, k: (i, j)), TN, (tk, tn), precision=precision)


def _mm_up(name, n, wup, layer):
    T = n.shape[0]
    tm = _tile(T, BIG_TILES)
    return _mm(name, n, wup, (T, DFF2), BF16, (T // tm, N_CHIPS, 1),
               pl.BlockSpec((tm, D), lambda i, j, k: (i, 0)),
               pl.BlockSpec((None, None, D, DFF2_SHARD), lambda i, j, k: (j, layer, 0, 0)),
               pl.BlockSpec((tm, DFF2_SHARD), lambda i, j, k: (i, j)), NN, (tm, DFF2_SHARD))


def _mm_up_nt(name, du, wup, layer):
    T = du.shape[0]
    tm, tk = _tile(T, BIG_TILES), D
    return _mm(name, du, wup, (T, D), F32, (T // tm, D // tk, N_CHIPS),
               pl.BlockSpec((tm, DFF2_SHARD), lambda i, j, k: (i, k)),
               pl.BlockSpec((None, None, tk, DFF2_SHARD), lambda i, j, k: (k, layer, j, 0)),
               pl.BlockSpec((tm, tk), lambda i, j, k: (i, j)), NT, (tm, tk))


def _mm_up_tn(name, n, du):
    T = n.shape[0]
    tk, tt = D, _tile(T, BIG_TILES)
    return _mm(name, n, du, (N_CHIPS, D, DFF2_SHARD), F32, (D // tk, N_CHIPS, T // tt),
               pl.BlockSpec((tt, tk), lambda i, j, k: (k, i)), pl.BlockSpec((tt, DFF2_SHARD), lambda i, j, k: (k, j)),
               pl.BlockSpec((None, tk, DFF2_SHARD), lambda i, j, k: (j, i, 0)), TN, (tk, DFF2_SHARD))


def _sigmoid(x):
    return 0.5 * jnp.tanh(0.5 * x) + 0.5


def _silu(x):
    return x * _sigmoid(x)


def _softplus(x):
    return jnp.maximum(x, 0.0) + jnp.log(1.0 + jnp.exp(-jnp.abs(x)))


def _rms_core(h, w):
    return h * lax.rsqrt(jnp.mean(h * h, axis=-1, keepdims=True) + EPS) * w


def _shift_down(x, halo, s, i):
    if s == 0:
        return x
    tm = x.shape[0]
    rolled = pltpu.roll(x, s, 0)
    patch = pltpu.roll(jnp.where(i == 0, 0.0, halo), s, 0)
    row = lax.broadcasted_iota(jnp.int32, patch.shape, 0)
    top = jnp.where(row < s, patch, rolled[:SUBLANE])
    return jnp.concatenate([top, rolled[SUBLANE:]], axis=0) if tm > SUBLANE else top


def _shift_up(x, halo, s, i, n):
    if s == 0:
        return x
    tm = x.shape[0]
    rolled = pltpu.roll(x, tm - s, 0)
    patch = pltpu.roll(jnp.where(i == n - 1, 0.0, halo), SUBLANE - s, 0)
    row = lax.broadcasted_iota(jnp.int32, patch.shape, 0)
    bottom = jnp.where(row >= SUBLANE - s, patch, rolled[tm - SUBLANE:])
    return jnp.concatenate([rolled[:tm - SUBLANE], bottom], axis=0) if tm > SUBLANE else bottom


def _taps(x, halo, K, i):
    return [_shift_down(x, halo, K - 1 - j, i) for j in range(K)]


def _conv_fwd(taps, w):
    y = w[0:1, :] * taps[0]
    for j in range(1, len(taps)):
        y = y + w[j:j + 1, :] * taps[j]
    return y


def _conv_dx(dy, halo_next, w, i, n):
    K = w.shape[0]
    dx = w[K - 1:K, :] * dy
    for j in range(K - 1):
        dx = dx + w[j:j + 1, :] * _shift_up(dy, halo_next, K - 1 - j, i, n)
    return dx


def _conv_dw(dy, taps):
    rows = [jnp.sum(dy * tap, axis=0, keepdims=True) for tap in taps]
    return jnp.concatenate(rows + [jnp.zeros((SUBLANE - len(taps), dy.shape[1]), F32)], axis=0)


def _rms_fwd(name, h, w, tm=512):
    T = h.shape[0]
    tm = min(tm, T)

    def fn(i, hv, wv):
        return (_rms_core(hv, wv),)

    return _rowcall(name, fn, T, tm, [(h, "row", None), (w, "full", None)], [((T, D), BF16, "row")])[0]


def _rms_bwd(name, h, pairs, adds, tm=256):
    T = h.shape[0]
    tm = min(tm, T)
    npair, nadd = len(pairs), len(adds)

    def fn(i, hv, *rest):
        ws, dns, ads = rest[:npair], rest[npair:2 * npair], rest[2 * npair:]
        dh = None
        dws = []
        for wv, dn in zip(ws, dns):
            _, vjp = jax.vjp(_rms_core, hv, wv)
            dhi, dwi = vjp(dn.astype(F32))
            dh = dhi if dh is None else dh + dhi
            dws.append(dwi)
        for a in ads:
            dh = dh + a.astype(F32)
        return (dh, *dws)

    ins = [(h, "row", None)] + [(w, "full", None) for w, _ in pairs] + [(dn, "row", None) for _, dn in pairs]
    ins += [(a, "row", None) for a in adds]
    outs = [((T, D), F32, "row")] + [((1, D), F32, "acc")] * npair
    return _rowcall(name, fn, T, tm, ins, outs)


def _l2(x):
    return x * lax.rsqrt(jnp.sum(x * x, axis=-1, keepdims=True) + EPS)


def _gdn_post_core(yq, yk, yv, pb, pa, a_log, dtb):
    qn = tuple(_l2(_silu(a)) * (GDN_HD ** -0.5) for a in yq)
    kn = tuple(_l2(_silu(a)) for a in yk)
    v = _silu(yv)
    beta = _sigmoid(pb)
    g = -jnp.exp(a_log) * _softplus(pa + dtb)
    return qn, kn, v, beta, g


def _heads(x, n):
    return tuple(x[:, GDN_HD * h:GDN_HD * (h + 1)] for h in range(n))


def _gdn_pre_fwd(pm, pba, conv_w, a_log, dtb, tm=128):
    T = pm.shape[0]
    tm = min(tm, T)

    def fn(i, x, halo, pbav, cw, al, db):
        y = _conv_fwd(_taps(x.astype(F32), halo.astype(F32), 4, i), cw)
        qn, kn, v, beta, g = _gdn_post_core(_heads(y[:, :GDN_QK], 8), _heads(y[:, GDN_QK:2 * GDN_QK], 8),
                                            y[:, 2 * GDN_QK:], pbav[:, :LANE], pbav[:, LANE:], al, db)
        return jnp.stack(qn), jnp.stack(kn), jnp.stack(_heads(v, GDN_V_HEADS)), beta, g

    ins = [(pm, "row", (GDN_CONV, 0)), (pm, "prev", (GDN_CONV, 0)), (pba, "row", None),
           (conv_w, "full", None), (a_log, "full", None), (dtb, "full", None)]
    outs = [((GDN_QK_HEADS, T, GDN_HD), BF16, "row"), ((GDN_QK_HEADS, T, GDN_HD), BF16, "row"),
            ((GDN_V_HEADS, T, GDN_HD), BF16, "row"), ((T, LANE), F32, "row"), ((T, LANE), F32, "row")]
    return _rowcall("gdn_pre_fwd", fn, T, tm, ins, outs)


def _gdn_pre_bwd(pm, pba, conv_w, a_log, dtb, dqn, dkn, dv, dbeta, dg, tm=128):
    T = pm.shape[0]
    tm = min(tm, T)

    def fn(i, x, halo, pbav, cw, al, db, dqv, dkv, dvv, dbv, dgv):
        taps = _taps(x.astype(F32), halo.astype(F32), 4, i)
        y = _conv_fwd(taps, cw)
        prim = (_heads(y[:, :GDN_QK], 8), _heads(y[:, GDN_QK:2 * GDN_QK], 8), y[:, 2 * GDN_QK:],
                pbav[:, :LANE], pbav[:, LANE:], al, db)
        _, vjp = jax.vjp(_gdn_post_core, *prim)
        cot = (tuple(dqv[h].astype(F32) for h in range(8)), tuple(dkv[h].astype(F32) for h in range(8)),
               jnp.concatenate([dvv[h].astype(F32) for h in range(GDN_V_HEADS)], axis=1), dbv, dgv)
        dyq, dyk, dyv, dpb, dpa, dal, ddb = vjp(cot)
        dy = jnp.concatenate(list(dyq) + list(dyk) + [dyv], axis=1)
        dcw = _conv_dw(dy, taps)
        return dy, jnp.concatenate([dpb, dpa], axis=1), dcw, dal, ddb

    ins = [(pm, "row", (GDN_CONV, 0)), (pm, "prev", (GDN_CONV, 0)), (pba, "row", None),
           (conv_w, "full", None), (a_log, "full", None), (dtb, "full", None),
           (dqn, "row", None), (dkn, "row", None), (dv, "row", None), (dbeta, "row", None), (dg, "row", None)]
    outs = [((T, GDN_CONV), BF16, "row"), ((T, 2 * LANE), F32, "row"), ((SUBLANE, GDN_CONV), F32, "acc"),
            ((1, LANE), F32, "acc"), ((1, LANE), F32, "acc")]
    return _rowcall("gdn_pre_bwd", fn, T, tm, ins, outs)


def _gdn_conv_bwd(dy, dz, conv_w, tm=256):
    T = dy.shape[0]
    tm = min(tm, T)
    n = T // tm

    def fn(i, dyv, halo, dzv, cw):
        dx = _conv_dx(dyv.astype(F32), halo.astype(F32), cw, i, n)
        return (jnp.concatenate([dx.astype(BF16), dzv.astype(BF16)], axis=1),)

    ins = [(dy, "row", None), (dy, "next", None), (dz, "row", None), (conv_w, "full", None)]
    return _rowcall("gdn_conv_bwd", fn, T, tm, ins, [((T, GDN_MAIN), BF16, "row")])[0]


def _bdot(a, b, dims=NN):
    return lax.dot_general(a.astype(BF16), b.astype(BF16), (dims, ((), ())), preferred_element_type=F32)


BNN = ((2,), (1,))
BNT = ((2,), (2,))
BTN = ((1,), (1,))


def _bmm(a, b, dims=BNN):
    return lax.dot_general(a.astype(BF16), b.astype(BF16), (dims, ((0,), (0,))), preferred_element_type=F32)


def _bmm3(a, b):
    ah, bh = a.astype(BF16), b.astype(BF16)
    al, bl = (a - ah.astype(F32)).astype(BF16), (b - bh.astype(F32)).astype(BF16)
    dn = (BNN, ((0,), (0,)))
    return (lax.dot_general(ah, bh, dn, preferred_element_type=F32)
            + lax.dot_general(al, bh, dn, preferred_element_type=F32)
            + lax.dot_general(ah, bl, dn, preferred_element_type=F32))


@jax.custom_vjp
def _tri_inv(m):
    C = m.shape[-1]
    r = lax.broadcasted_iota(jnp.int32, (C, C), 0)
    c = lax.broadcasted_iota(jnp.int32, (C, C), 1)
    t = jnp.where(r == c, 1.0, 0.0) - m
    pw = _bmm3(m, m)
    t = t + _bmm3(t, pw)
    for _ in range(int(math.log2(C)) - 2):
        pw = _bmm(pw, pw)
        t = t + _bmm(t, pw)
    return t


def _tri_inv_fwd(m):
    t = _tri_inv(m)
    return t, t


def _tri_inv_bwd(t, dt):
    tt = jnp.swapaxes(t, 1, 2)
    return (-_bmm(_bmm(tt, dt), tt),)


_tri_inv.defvjp(_tri_inv_fwd, _tri_inv_bwd)


def _twice(a):
    return jnp.broadcast_to(a[:, None], (a.shape[0], 2) + a.shape[1:]).reshape((2 * a.shape[0],) + a.shape[1:])


def _gdn_chunk(q, k, v, grow, brow, S):
    C = q.shape[1]
    r = lax.broadcasted_iota(jnp.int32, (C, C), 0)
    c = lax.broadcasted_iota(jnp.int32, (C, C), 1)
    tril, strict, eye = r >= c, r > c, r == c
    gcol = jnp.sum(jnp.where(eye, grow, 0.0), axis=2, keepdims=True)
    bcol = jnp.sum(jnp.where(eye, brow, 0.0), axis=2, keepdims=True)
    gc_col = jnp.sum(jnp.where(tril, grow, 0.0), axis=2, keepdims=True)
    gc_row = jnp.sum(jnp.where(r <= c, gcol, 0.0), axis=1, keepdims=True)
    gc_last = jnp.sum(grow, axis=2, keepdims=True)
    decay = jnp.where(tril, jnp.exp(jnp.where(tril, gc_col - gc_row, 0.0)), 0.0)
    kk = _twice(_bmm(k, k, BNT))
    qk = _twice(_bmm(q, k, BNT))
    t_mat = _tri_inv(jnp.where(strict, bcol * kk * decay, 0.0))
    k2, q2 = _twice(k), _twice(q)
    egc = jnp.exp(gc_col)
    u = _bmm(t_mat, v * bcol)
    w = _bmm(t_mat, k2 * (bcol * egc))
    v_new = u - _bmm(w, S)
    o = _bmm(q2 * egc, S) + _bmm(qk * decay, v_new)
    s_new = S * jnp.exp(gc_last) + _bmm(k2 * jnp.exp(gc_last - gc_col), v_new, BTN)
    return o, s_new


def _gdn_tb(T):
    return min(256, T)


def _gate_rows(g):
    T = g.shape[0]
    g = g[:, :GDN_V_HEADS].reshape(T // GDN_CHUNK, GDN_CHUNK, GDN_V_HEADS)
    return g.transpose(0, 2, 1)[:, :, None, :]


def _gate_cols(g):
    nc = g.shape[0]
    g = g[:, :, 0, :].transpose(0, 2, 1).reshape(nc * GDN_CHUNK, GDN_V_HEADS)
    return jnp.pad(g, ((0, 0), (0, LANE - GDN_V_HEADS)))


def _gdn_fwd(qn, kn, v, g, beta):
    T = qn.shape[1]
    tb = _gdn_tb(T)
    nc = tb // GDN_CHUNK

    def body(q_ref, k_ref, v_ref, g_ref, b_ref, o_ref, sall_ref, s_scr):
        @pl.when(pl.program_id(0) == 0)
        def _():
            s_scr[...] = jnp.zeros(s_scr.shape, F32)

        def chunk(ci, carry):
            rows = pl.ds(pl.multiple_of(ci * GDN_CHUNK, GDN_CHUNK), GDN_CHUNK)
            s = s_scr[...]
            sall_ref[ci] = s
            o, s_new = _gdn_chunk(q_ref[:, rows, :].astype(F32), k_ref[:, rows, :].astype(F32),
                                  v_ref[:, rows, :].astype(F32), g_ref[ci], b_ref[ci], s)
            o_ref[:, rows, :] = o.astype(o_ref.dtype)
            s_scr[...] = s_new
            return carry

        lax.fori_loop(0, nc, chunk, 0)

    qk_spec = pl.BlockSpec((GDN_QK_HEADS, tb, GDN_HD), lambda i: (0, i, 0))
    v_spec = pl.BlockSpec((GDN_V_HEADS, tb, GDN_HD), lambda i: (0, i, 0))
    g_spec = pl.BlockSpec((nc, GDN_V_HEADS, 1, GDN_CHUNK), lambda i: (i, 0, 0, 0))
    return pl.pallas_call(
        body, name="gdn_fwd", grid=(T // tb,),
        in_specs=[qk_spec, qk_spec, v_spec, g_spec, g_spec],
        out_specs=[v_spec, pl.BlockSpec((nc, GDN_V_HEADS, GDN_HD, GDN_HD), lambda i: (i, 0, 0, 0))],
        out_shape=[jax.ShapeDtypeStruct((GDN_V_HEADS, T, GDN_HD), BF16),
                   jax.ShapeDtypeStruct((T // GDN_CHUNK, GDN_V_HEADS, GDN_HD, GDN_HD), F32)],
        scratch_shapes=[pltpu.VMEM((GDN_V_HEADS, GDN_HD, GDN_HD), F32)],
        compiler_params=_params(("arbitrary",)),
    )(qn, kn, v, g, beta)


def _gdn_bwd(qn, kn, v, g, beta, sall, do):
    T = qn.shape[1]
    tb = _gdn_tb(T)
    nc = tb // GDN_CHUNK
    nb = T // tb

    def body(q_ref, k_ref, v_ref, g_ref, b_ref, sall_ref, do_ref, dq_ref, dk_ref, dv_ref, dg_ref, db_ref, ds_scr):
        @pl.when(pl.program_id(0) == 0)
        def _():
            ds_scr[...] = jnp.zeros(ds_scr.shape, F32)

        def chunk(cr, carry):
            ci = nc - 1 - cr
            rows = pl.ds(pl.multiple_of(ci * GDN_CHUNK, GDN_CHUNK), GDN_CHUNK)
            _, vjp = jax.vjp(_gdn_chunk, q_ref[:, rows, :].astype(F32), k_ref[:, rows, :].astype(F32),
                             v_ref[:, rows, :].astype(F32), g_ref[ci], b_ref[ci], sall_ref[ci])
            dq, dk, dv, dg, db, ds = vjp((do_ref[:, rows, :].astype(F32), ds_scr[...]))
            ds_scr[...] = ds
            dq_ref[:, rows, :] = dq
            dk_ref[:, rows, :] = dk
            dv_ref[:, rows, :] = dv
            dg_ref[ci] = dg
            db_ref[ci] = db
            return carry

        lax.fori_loop(0, nc, chunk, 0)

    qk_spec = pl.BlockSpec((GDN_QK_HEADS, tb, GDN_HD), lambda i: (0, nb - 1 - i, 0))
    v_spec = pl.BlockSpec((GDN_V_HEADS, tb, GDN_HD), lambda i: (0, nb - 1 - i, 0))
    g_spec = pl.BlockSpec((nc, GDN_V_HEADS, 1, GDN_CHUNK), lambda i: (nb - 1 - i, 0, 0, 0))
    s_spec = pl.BlockSpec((nc, GDN_V_HEADS, GDN_HD, GDN_HD), lambda i: (nb - 1 - i, 0, 0, 0))
    return pl.pallas_call(
        body, name="gdn_bwd", grid=(nb,),
        in_specs=[qk_spec, qk_spec, v_spec, g_spec, g_spec, s_spec, v_spec],
        out_specs=[qk_spec, qk_spec, v_spec, g_spec, g_spec],
        out_shape=[jax.ShapeDtypeStruct((GDN_QK_HEADS, T, GDN_HD), F32),
                   jax.ShapeDtypeStruct((GDN_QK_HEADS, T, GDN_HD), F32),
                   jax.ShapeDtypeStruct((GDN_V_HEADS, T, GDN_HD), F32),
                   jax.ShapeDtypeStruct(g.shape, F32), jax.ShapeDtypeStruct(g.shape, F32)],
        scratch_shapes=[pltpu.VMEM((GDN_V_HEADS, GDN_HD, GDN_HD), F32)],
        compiler_params=_params(("arbitrary",)),
    )(qn, kn, v, g, beta, sall, do)


def _gnorm_core(o, z, w):
    return tuple(_rms_core(oh, w) * _silu(zh) for oh, zh in zip(o, z))


def _gnorm_fwd(o, pm, w, tm=256):
    T = pm.shape[0]
    tm = min(tm, T)

    def fn(i, ov, zv, wv):
        zf = zv.astype(F32)
        out = _gnorm_core(tuple(ov[h].astype(F32) for h in range(GDN_V_HEADS)), _heads(zf, GDN_V_HEADS), wv)
        return (jnp.concatenate(out, axis=1),)

    ins = [(o, "row", None), (pm, "row", (GDN_V, 2)), (w, "full", None)]
    return _rowcall("gnorm_fwd", fn, T, tm, ins, [((T, GDN_V), BF16, "row")])[0]


def _gnorm_bwd(o, pm, w, don, tm=128):
    T = pm.shape[0]
    tm = min(tm, T)

    def fn(i, ov, zv, wv, dv):
        zf, df = zv.astype(F32), dv.astype(F32)
        _, vjp = jax.vjp(_gnorm_core, tuple(ov[h].astype(F32) for h in range(GDN_V_HEADS)),
                         _heads(zf, GDN_V_HEADS), wv)
        do, dz, dw = vjp(_heads(df, GDN_V_HEADS))
        return jnp.stack(do), jnp.concatenate(dz, axis=1), dw

    ins = [(o, "row", None), (pm, "row", (GDN_V, 2)), (w, "full", None), (don, "row", None)]
    outs = [((GDN_V_HEADS, T, GDN_HD), BF16, "row"), ((T, GDN_V), BF16, "row"), ((1, GDN_HD), F32, "acc")]
    return _rowcall("gnorm_bwd", fn, T, tm, ins, outs)


def _ffn_act_fwd(name, up, conv_w, conv_b, tm=128):
    T = up.shape[0]
    tm = min(tm, T)

    def fn(i, x, halo, cw, cb):
        u = _conv_fwd(_taps(x.astype(F32), halo.astype(F32), 3, i), cw) + cb
        return (_silu(u[:, :DFF]) * u[:, DFF:],)

    ins = [(up, "row", None), (up, "prev", None), (conv_w, "full", None), (conv_b, "full", None)]
    return _rowcall(name, fn, T, tm, ins, [((T, DFF), BF16, "row")])[0]


def _ffn_act_bwd(name, up, conv_w, conv_b, dact, tm=128):
    T = up.shape[0]
    tm = min(tm, T)

    def fn(i, x, halo, cw, cb, da):
        taps = _taps(x.astype(F32), halo.astype(F32), 3, i)
        da = da.astype(F32)
        u = _conv_fwd(taps, cw) + cb
        gate, val = u[:, :DFF], u[:, DFF:]
        sg = _sigmoid(gate)
        dgate = da * val * sg * (1.0 + gate * (1.0 - sg))
        dval = da * gate * sg
        du = jnp.concatenate([dgate, dval], axis=1)
        return du, _conv_dw(du, taps), jnp.sum(du, axis=0, keepdims=True)

    ins = [(up, "row", None), (up, "prev", None), (conv_w, "full", None), (conv_b, "full", None),
           (dact, "row", None)]
    outs = [((T, DFF2), BF16, "row"), ((SUBLANE, DFF2), F32, "acc"), ((1, DFF2), F32, "acc")]
    return _rowcall(name, fn, T, tm, ins, outs)


def _ffn_conv_bwd(name, du, conv_w, tm=256):
    T = du.shape[0]
    tm = min(tm, T)
    n = T // tm

    def fn(i, dv, halo, cw):
        return (_conv_dx(dv.astype(F32), halo.astype(F32), cw, i, n),)

    ins = [(du, "row", None), (du, "next", None), (conv_w, "full", None)]
    return _rowcall(name, fn, T, tm, ins, [((T, DFF2), BF16, "row")])[0]


GROUP_ROWS = SWA_GROUP * SWA_BLOCK


def _attn_core(q, kp, kc, vp, vc, bias, sink, mask):
    kcat = jnp.concatenate([kp, kc], axis=0)
    vcat = jnp.concatenate([vp, vc], axis=0)
    s = _bdot(q * (SWA_HD ** -0.5), kcat, NT) + bias
    s = jnp.where(mask, s, NEG_INF)
    m = lax.stop_gradient(jnp.maximum(jnp.max(s, axis=-1, keepdims=True), sink))
    p = jnp.exp(s - m)
    denom = jnp.sum(p, axis=-1, keepdims=True) + jnp.exp(sink - m)
    return _bdot(p / denom, vcat)


def _attn_mask(i):
    qi = lax.broadcasted_iota(jnp.int32, (GROUP_ROWS, 2 * SWA_BLOCK), 0) & (SWA_BLOCK - 1)
    ki = lax.broadcasted_iota(jnp.int32, (GROUP_ROWS, 2 * SWA_BLOCK), 1)
    dist = qi + SWA_BLOCK - ki
    return (dist >= 0) & (dist < SWA_BLOCK) & ((ki >= SWA_BLOCK) | (i > 0))


def _attn_operands(j, q_ref, kc_ref, kp_ref, vc_ref, vp_ref, b_ref, s_ref):
    heads = slice(SWA_GROUP * j, SWA_GROUP * (j + 1))
    sink = jnp.concatenate([jnp.broadcast_to(s_ref[j, g:g + 1, 0:1], (SWA_BLOCK, 1)) for g in range(SWA_GROUP)],
                           axis=0)
    return (q_ref[heads].astype(F32).reshape(GROUP_ROWS, SWA_HD), kp_ref[j].astype(F32), kc_ref[j].astype(F32),
            vp_ref[j].astype(F32), vc_ref[j].astype(F32), b_ref[heads].reshape(GROUP_ROWS, 2 * SWA_BLOCK), sink)


def _attn_fwd(q, k, v, bias, sinks):
    T = q.shape[1]
    nb = T // SWA_BLOCK

    def body(q_ref, kc_ref, kp_ref, vc_ref, vp_ref, b_ref, s_ref, o_ref):
        mask = _attn_mask(pl.program_id(0))
        for j in range(SWA_KV_HEADS):
            out = _attn_core(*_attn_operands(j, q_ref, kc_ref, kp_ref, vc_ref, vp_ref, b_ref, s_ref), mask)
            o_ref[SWA_GROUP * j:SWA_GROUP * (j + 1)] = out.reshape(SWA_GROUP, SWA_BLOCK, SWA_HD).astype(o_ref.dtype)

    q_spec = pl.BlockSpec((SWA_Q_HEADS, SWA_BLOCK, SWA_HD), lambda i: (0, i, 0))
    cur = pl.BlockSpec((SWA_KV_HEADS, SWA_BLOCK, SWA_HD), lambda i: (0, i, 0))
    prev = pl.BlockSpec((SWA_KV_HEADS, SWA_BLOCK, SWA_HD), lambda i: (0, jnp.maximum(i - 1, 0), 0))
    return pl.pallas_call(
        body, name="attn_fwd", grid=(nb,),
        in_specs=[q_spec, cur, prev, cur, prev, pl.BlockSpec(bias.shape, lambda i: (0, 0, 0)),
                  pl.BlockSpec(sinks.shape, lambda i: (0, 0, 0))],
        out_specs=q_spec, out_shape=jax.ShapeDtypeStruct(q.shape, BF16),
        compiler_params=_params(("arbitrary",)),
    )(q, k, k, v, v, bias, sinks)


def _attn_bwd(q, k, v, bias, sinks, do):
    T = q.shape[1]
    nb = T // SWA_BLOCK

    def body(q_ref, kc_ref, kp_ref, vc_ref, vp_ref, b_ref, s_ref, do_ref,
             dq_ref, dk_ref, dv_ref, db_ref, dsk_ref, kcar, vcar):
        i = pl.program_id(0)

        @pl.when(i < nb)
        def _():
            mask = _attn_mask(i)
            for j in range(SWA_KV_HEADS):
                heads = slice(SWA_GROUP * j, SWA_GROUP * (j + 1))
                prim = _attn_operands(j, q_ref, kc_ref, kp_ref, vc_ref, vp_ref, b_ref, s_ref)
                _, vjp = jax.vjp(functools.partial(_attn_core, mask=mask), *prim)
                dq, dkp, dkc, dvp, dvc, db, dsc = vjp(do_ref[heads].astype(F32).reshape(GROUP_ROWS, SWA_HD))
                dq_ref[heads] = dq.reshape(SWA_GROUP, SWA_BLOCK, SWA_HD).astype(dq_ref.dtype)
                db = db.reshape(SWA_GROUP, SWA_BLOCK, 2 * SWA_BLOCK)
                dsk = jnp.concatenate(
                    [jnp.broadcast_to(jnp.sum(dsc[g * SWA_BLOCK:(g + 1) * SWA_BLOCK], axis=0, keepdims=True),
                                      (1, LANE)) for g in range(SWA_GROUP)], axis=0)

                @pl.when(i == 0)
                def _():
                    db_ref[heads] = db
                    dsk_ref[j] = dsk

                @pl.when(i > 0)
                def _():
                    db_ref[heads] += db
                    dsk_ref[j] += dsk
                    dk_ref[j] = (kcar[j] + dkp).astype(dk_ref.dtype)
                    dv_ref[j] = (vcar[j] + dvp).astype(dv_ref.dtype)

                kcar[j] = dkc
                vcar[j] = dvc

        @pl.when(i == nb)
        def _():
            dk_ref[...] = kcar[...].astype(dk_ref.dtype)
            dv_ref[...] = vcar[...].astype(dv_ref.dtype)

    last = nb - 1
    q_spec = pl.BlockSpec((SWA_Q_HEADS, SWA_BLOCK, SWA_HD), lambda i: (0, jnp.minimum(i, last), 0))
    cur = pl.BlockSpec((SWA_KV_HEADS, SWA_BLOCK, SWA_HD), lambda i: (0, jnp.minimum(i, last), 0))
    prev = pl.BlockSpec((SWA_KV_HEADS, SWA_BLOCK, SWA_HD), lambda i: (0, jnp.clip(i - 1, 0, last), 0))
    b_spec = pl.BlockSpec(bias.shape, lambda i: (0, 0, 0))
    s_spec = pl.BlockSpec(sinks.shape, lambda i: (0, 0, 0))
    carry = pltpu.VMEM((SWA_KV_HEADS, SWA_BLOCK, SWA_HD), F32)
    return pl.pallas_call(
        body, name="attn_bwd", grid=(nb + 1,),
        in_specs=[q_spec, cur, prev, cur, prev, b_spec, s_spec, q_spec],
        out_specs=[q_spec, prev, prev, b_spec, s_spec],
        out_shape=[jax.ShapeDtypeStruct(q.shape, BF16), jax.ShapeDtypeStruct(k.shape, BF16),
                   jax.ShapeDtypeStruct(k.shape, BF16), jax.ShapeDtypeStruct(bias.shape, F32),
                   jax.ShapeDtypeStruct(sinks.shape, F32)],
        scratch_shapes=[carry, carry],
        compiler_params=_params(("arbitrary",)),
    )(q, k, k, v, v, bias, sinks, do)


def _rel_onehot():
    qi = jnp.arange(SWA_BLOCK)[:, None]
    ki = jnp.arange(2 * SWA_BLOCK)[None, :]
    n = jnp.maximum(qi + SWA_BLOCK - ki, 0)
    max_exact = REL_BUCKETS // 2
    nf = jnp.maximum(n, 1).astype(F32)
    large = max_exact + (jnp.log(nf / max_exact) / math.log(REL_MAX_DISTANCE / max_exact)
                         * (REL_BUCKETS - max_exact)).astype(jnp.int32)
    bucket = jnp.where(n < max_exact, n, jnp.minimum(large, REL_BUCKETS - 1)).reshape(-1)
    return (bucket[None, :] == jnp.arange(REL_BUCKETS)[:, None]).astype(F32)


def _final(h, w, target, tm=256):
    T = h.shape[0]
    tm = min(tm, T)

    def fn(i, hv, wv, tv):
        y, vjp = jax.vjp(_rms_core, hv, wv)
        err = y - tv
        dh, dw = vjp(err * (1.0 / D))
        part = 0.5 * jnp.sum(jnp.sum(err * err, axis=1, keepdims=True) * (1.0 / D), axis=0, keepdims=True)
        return jnp.broadcast_to(part, (SUBLANE, LANE)), dh, dw

    ins = [(h, "row", None), (w, "full", None), (target, "row", None)]
    outs = [((SUBLANE, LANE), F32, "acc"), ((T, D), F32, "row"), ((1, D), F32, "acc")]
    return _rowcall("final", fn, T, tm, ins, outs)


def _heads_major(a, heads, hd):
    return a.reshape(a.shape[0], heads, hd).transpose(1, 0, 2)


def _heads_minor(a):
    return a.transpose(1, 0, 2).reshape(a.shape[1], a.shape[0] * a.shape[2])


def _ffn_fwd(tag, h, P, layer):
    n = _rms_fwd(f"{tag}_rms", h, P["ffn_norm_w"][layer:layer + 1])
    up = _mm_up(f"{tag}_up", n, P["w_up"], layer)
    act = _ffn_act_fwd(f"{tag}_act", up, P["ffn_conv_w"][layer], P["ffn_conv_b"][layer:layer + 1])
    out = _mm_nn(f"{tag}_down", act, P["w_down"][layer], F32, res=h)
    return out, (n, up, act)


def _ffn_bwd(tag, h, saved, dout, P, layer):
    n, up, act = saved
    cw, cb = P["ffn_conv_w"][layer], P["ffn_conv_b"][layer:layer + 1]
    dact = _mm_nt(f"{tag}_down_dx", dout, P["w_down"][layer], BF16)
    g_down = _mm_tn(f"{tag}_down_dw", act, dout)
    du, dcw, dcb = _ffn_act_bwd(f"{tag}_act_bwd", up, cw, cb, dact)
    dup = _ffn_conv_bwd(f"{tag}_conv_bwd", du, cw)
    g_up = _mm_up_tn(f"{tag}_up_dw", n, dup)
    dn = _mm_up_nt(f"{tag}_up_dx", dup, P["w_up"], layer)
    dh, dnw = _rms_bwd(f"{tag}_rms_bwd", h, [(P["ffn_norm_w"][layer:layer + 1], dn)], [dout])
    return dh, dict(w_down=g_down, w_up=g_up, conv_w=dcw[:3], conv_b=dcb, norm_w=dnw)


def _local_step(x, target, P):
    T = x.shape[0]
    n0 = _rms_fwd("a_rms", x, P["a_norm_w"])
    pm = _mm_nn("gdn_in", n0, P["w_in_main"], BF16)
    pba = _mm_nn("gdn_in_ba", n0, P["w_in_ba"], F32)
    qn, kn, v, beta, g = _gdn_pre_fwd(pm, pba, P["a_conv_w"], P["a_log"], P["dt_bias"])
    g_rows, beta_rows = _gate_rows(g), _gate_rows(beta)
    o, sall = _gdn_fwd(qn, kn, v, g_rows, beta_rows)
    on = _gnorm_fwd(o, pm, P["a_out_norm_w"])
    h1 = _mm_nn("gdn_out", on, P["w_out"], F32, res=x)
    h2, ffn0 = _ffn_fwd("ffn0", h1, P, 0)
    nkv = _rms_fwd("kv_rms", h2, P["kv_norm_w"])
    kv = _mm_nn("kv_proj", nkv, P["w_kv"], BF16)
    nb = _rms_fwd("b_rms", h2, P["b_norm_w"])
    qp = _mm_nn("q_proj", nb, P["w_q"], BF16)
    q3 = _heads_major(qp, SWA_Q_HEADS, SWA_HD)
    k3 = _heads_major(kv[:, :SWA_KV_HEADS * SWA_HD], SWA_KV_HEADS, SWA_HD)
    v3 = _heads_major(kv[:, SWA_KV_HEADS * SWA_HD:], SWA_KV_HEADS, SWA_HD)
    onehot = _rel_onehot()
    bias = _mm_nn("rel_bias", P["rel_table_t"], onehot, F32, precision=HIGHEST)
    bias = bias.reshape(SWA_Q_HEADS, SWA_BLOCK, 2 * SWA_BLOCK)
    oa = _heads_minor(_attn_fwd(q3, k3, v3, bias, P["sinks"]))
    h3 = _mm_nn("o_proj", oa, P["w_o"], F32, res=h2)
    h4, ffn1 = _ffn_fwd("ffn1", h3, P, 1)
    loss, dh4, d_final = _final(h4, P["final_norm_w"], target)

    dh3, gf1 = _ffn_bwd("ffn1", h3, ffn1, dh4, P, 1)
    doa = _mm_nt("o_proj_dx", dh3, P["w_o"], BF16)
    g_wo = _mm_tn("o_proj_dw", oa, dh3)
    dq3, dk3, dv3, dbias, dsinks = _attn_bwd(q3, k3, v3, bias, P["sinks"], _heads_major(doa, SWA_Q_HEADS, SWA_HD))
    dqp = _heads_minor(dq3)
    dkv = jnp.concatenate([_heads_minor(dk3), _heads_minor(dv3)], axis=1)
    g_wq = _mm_tn("q_proj_dw", nb, dqp)
    dnb = _mm_nt("q_proj_dx", dqp, P["w_q"], F32)
    g_wkv = _mm_tn("kv_proj_dw", nkv, dkv)
    dnkv = _mm_nt("kv_proj_dx", dkv, P["w_kv"], F32)
    dh2, d_bnorm, d_kvnorm = _rms_bwd("b_kv_rms_bwd", h2, [(P["b_norm_w"], dnb), (P["kv_norm_w"], dnkv)], [dh3])
    g_table = _mm_nt("rel_bias_dw", onehot, dbias.reshape(SWA_Q_HEADS, -1), F32, precision=HIGHEST)
    dh1, gf0 = _ffn_bwd("ffn0", h1, ffn0, dh2, P, 0)
    don = _mm_nt("gdn_out_dx", dh1, P["w_out"], BF16)
    g_wout = _mm_tn("gdn_out_dw", on, dh1)
    do, dz, d_gnorm = _gnorm_bwd(o, pm, P["a_out_norm_w"], don)
    dq, dk, dv, dg, dbeta = _gdn_bwd(qn, kn, v, g_rows, beta_rows, sall, do)
    dy, dpba, d_aconv, d_alog, d_dtb = _gdn_pre_bwd(pm, pba, P["a_conv_w"], P["a_log"], P["dt_bias"],
                                                    dq, dk, dv, _gate_cols(dbeta), _gate_cols(dg))
    dpm = _gdn_conv_bwd(dy, dz, P["a_conv_w"])
    g_win_main = _mm_tn("gdn_in_dw", n0, dpm)
    g_win_ba = _mm_tn("gdn_in_ba_dw", n0, dpba)
    dn0 = _mm_nt("gdn_in_dx", dpm, P["w_in_main"], F32)
    dn0 = _mm_nt("gdn_in_ba_dx", dpba, P["w_in_ba"], F32, res=dn0)
    dx, d_anorm = _rms_bwd("a_rms_bwd", x, [(P["a_norm_w"], dn0)], [dh1])

    nh = GDN_V_HEADS
    grads = dict(
        a_norm_w=d_anorm,
        a_w_in=jnp.concatenate([g_win_main, g_win_ba[:, :nh], g_win_ba[:, LANE:LANE + nh]], axis=1),
        a_conv_w=d_aconv[:4], a_a_log=d_alog[:, :nh], a_dt_bias=d_dtb[:, :nh], a_out_norm_w=d_gnorm,
        a_w_out=g_wout, kv_norm_w=d_kvnorm, w_kv=g_wkv, b_norm_w=d_bnorm, b_w_q=g_wq,
        b_sinks=dsinks[:, :, 0].reshape(1, SWA_Q_HEADS), b_w_o=g_wo, rel_bias_table=g_table,
        ffn_norm_w=jnp.concatenate([gf0["norm_w"], gf1["norm_w"]], axis=0),
        ffn_w_up=jnp.stack([gf0["w_up"], gf1["w_up"]], axis=1),
        ffn_conv_w=jnp.stack([gf0["conv_w"], gf1["conv_w"]], axis=0),
        ffn_conv_b=jnp.concatenate([gf0["conv_b"], gf1["conv_b"]], axis=0),
        ffn_w_down=jnp.stack([gf0["w_down"], gf1["w_down"]], axis=0), final_norm_w=d_final,
    )
    return loss, dx, grads


HBM_SPEC = pl.BlockSpec(memory_space=pltpu.HBM)
VMEM_SPEC = pl.BlockSpec(memory_space=pltpu.VMEM)


def _coords():
    return lax.axis_index("x"), lax.axis_index("y"), lax.axis_index("c")


def _remote(src, dst, send_sem, recv_sem, device):
    return pltpu.make_async_remote_copy(src_ref=src, dst_ref=dst, send_sem=send_sem, recv_sem=recv_sem,
                                        device_id=device, device_id_type=MESH)


def _other_chips(x, y):
    return [(1 - x, y), (x, 1 - y), (1 - x, 1 - y)]


def _all_gather(arrs, split):
    n = len(arrs)

    def body(*refs):
        ins, outs = refs[:n], refs[n:2 * n]
        send_sems, recv_sems = refs[2 * n:]
        x, y, c = _coords()
        p = 2 * x + y
        chips = _other_chips(x, y)

        def rows(a, half):
            h = arrs[a].shape[0] // 2
            return pl.ds(half * h, h)

        sends = []
        for a in range(n):
            for j, chip in enumerate(chips):
                if split[a]:
                    src, dst = ins[a].at[rows(a, c)], outs[a].at[p, rows(a, c)]
                else:
                    src, dst = ins[a], outs[a].at[p]
                cp = _remote(src, dst, send_sems.at[6 * a + j], recv_sems.at[6 * a + j], (*chip, c))
                cp.start()
                sends.append(cp)
        for a in range(n):
            for j, chip in enumerate(chips):
                q = 2 * chip[0] + chip[1]
                land = outs[a].at[q, rows(a, c)] if split[a] else outs[a].at[q]
                _remote(land, land, send_sems.at[6 * a + j], recv_sems.at[6 * a + j], (*chip, c)).wait_recv()
                if split[a]:
                    fw = _remote(land, land, send_sems.at[6 * a + 3 + j], recv_sems.at[6 * a + 3 + j], (x, y, 1 - c))
                    fw.start()
                    sends.append(fw)
        for a in range(n):
            if split[a]:
                for j, chip in enumerate(chips):
                    q = 2 * chip[0] + chip[1]
                    land = outs[a].at[q, rows(a, 1 - c)]
                    _remote(land, land, send_sems.at[6 * a + 3 + j], recv_sems.at[6 * a + 3 + j],
                            (x, y, 1 - c)).wait_recv()
        for cp in sends:
            cp.wait_send()

    res = pl.pallas_call(
        body, name="weights_all_gather", in_specs=[HBM_SPEC] * n, out_specs=[HBM_SPEC] * n,
        out_shape=[jax.ShapeDtypeStruct((N_CHIPS,) + a.shape, a.dtype) for a in arrs],
        scratch_shapes=[pltpu.SemaphoreType.DMA((6 * n,)), pltpu.SemaphoreType.DMA((6 * n,))],
    )(*arrs)
    chip = 2 * lax.axis_index("x") + lax.axis_index("y")
    return [lax.dynamic_update_index_in_dim(r, a, chip, 0) for r, a in zip(res, arrs)]


PAIR_SWAP_PIECES = 2


def _pair_swap(gs):
    n = len(gs)

    def body(*refs):
        ins, other = refs[:n], refs[n:2 * n]
        send_sems, recv_sems = refs[2 * n:]
        x, y, c = _coords()
        cps = []
        for a in range(n):
            h = gs[a].shape[1] // 2
            piece = h // PAIR_SWAP_PIECES
            for q in range(N_CHIPS):
                for r in range(PAIR_SWAP_PIECES):
                    k = (a * N_CHIPS + q) * PAIR_SWAP_PIECES + r
                    cp = _remote(ins[a].at[q, pl.ds((1 - c) * h + r * piece, piece)],
                                 other[a].at[q, pl.ds(r * piece, piece)], send_sems.at[k], recv_sems.at[k],
                                 (x, y, 1 - c))
                    cp.start()
                    cps.append(cp)
        for cp in cps:
            cp.wait()

    half = [jax.ShapeDtypeStruct((N_CHIPS, g.shape[1] // 2, g.shape[2]), g.dtype) for g in gs]
    nsem = n * N_CHIPS * PAIR_SWAP_PIECES
    return pl.pallas_call(
        body, name="grads_pair_swap", in_specs=[HBM_SPEC] * n, out_specs=[HBM_SPEC] * n, out_shape=half,
        scratch_shapes=[pltpu.SemaphoreType.DMA((nsem,)), pltpu.SemaphoreType.DMA((nsem,))],
    )(*gs)


def _chip_scatter(ps):
    n = len(ps)

    def body(*refs):
        ins, outs = refs[:n], refs[n:2 * n]
        send_sems, recv_sems = refs[2 * n:]
        x, y, c = _coords()
        chips = _other_chips(x, y)
        sends = []
        for a in range(n):
            for j, chip in enumerate(chips):
                q = 2 * chip[0] + chip[1]
                cp = _remote(ins[a].at[q], outs[a].at[j], send_sems.at[3 * a + j], recv_sems.at[3 * a + j], (*chip, c))
                cp.start()
                sends.append(cp)
        for cp in sends:
            cp.wait_recv()
        for cp in sends:
            cp.wait_send()

    return pl.pallas_call(
        body, name="grads_chip_scatter", in_specs=[HBM_SPEC] * n, out_specs=[HBM_SPEC] * n,
        out_shape=[jax.ShapeDtypeStruct((N_CHIPS - 1,) + a.shape[1:], a.dtype) for a in ps],
        scratch_shapes=[pltpu.SemaphoreType.DMA((3 * n,)), pltpu.SemaphoreType.DMA((3 * n,))],
    )(*ps)


def _pair_share(rs):
    n = len(rs)

    def body(*refs):
        ins, outs = refs[:n], refs[n:2 * n]
        send_sems, recv_sems = refs[2 * n:]
        x, y, c = _coords()
        sends = []
        for a in range(n):
            cp = _remote(ins[a], outs[a], send_sems.at[a], recv_sems.at[a], (x, y, 1 - c))
            cp.start()
            sends.append(cp)
        for cp in sends:
            cp.wait()

    theirs = pl.pallas_call(
        body, name="grads_pair_share", in_specs=[HBM_SPEC] * n, out_specs=[HBM_SPEC] * n,
        out_shape=[jax.ShapeDtypeStruct(a.shape, a.dtype) for a in rs],
        scratch_shapes=[pltpu.SemaphoreType.DMA((n,)), pltpu.SemaphoreType.DMA((n,))],
    )(*rs)
    c = lax.axis_index("c")
    out = []
    for mine, other in zip(rs, theirs):
        h = mine.shape[0]
        both = jnp.zeros((2 * h, mine.shape[1]), mine.dtype)
        both = lax.dynamic_update_slice_in_dim(both, mine, c * h, 0)
        out.append(lax.dynamic_update_slice_in_dim(both, other, (1 - c) * h, 0))
    return out


def _small_all_reduce(buf):
    R = buf.shape[0]
    ndev = 2 * N_CHIPS

    def body(in_ref, out_ref, gath, send_sems, recv_sems):
        x, y, c = _coords()
        me = 4 * x + 2 * y + c
        gath[me] = in_ref[...]
        peers = []
        for d in range(1, ndev):
            px = 1 - x if d & 4 else x
            py = 1 - y if d & 2 else y
            pc = 1 - c if d & 1 else c
            peers.append((px, py, pc))
        sends = []
        for d, peer in enumerate(peers):
            cp = _remote(in_ref, gath.at[me], send_sems.at[d], recv_sems.at[d], peer)
            cp.start()
            sends.append(cp)
        for d, peer in enumerate(peers):
            land = gath.at[4 * peer[0] + 2 * peer[1] + peer[2]]
            _remote(land, land, send_sems.at[d], recv_sems.at[d], peer).wait_recv()
        for cp in sends:
            cp.wait_send()
        acc = gath[0]
        for s in range(1, ndev):
            acc = acc + gath[s]
        out_ref[...] = acc

    return pl.pallas_call(
        body, name="small_all_reduce", in_specs=[VMEM_SPEC], out_specs=VMEM_SPEC,
        out_shape=jax.ShapeDtypeStruct(buf.shape, F32),
        scratch_shapes=[pltpu.VMEM((ndev, R, LANE), F32), pltpu.SemaphoreType.DMA((ndev - 1,)),
                        pltpu.SemaphoreType.DMA((ndev - 1,))],
    )(buf)


def _pair_add(name, own, other):
    h = own.shape[1]
    tm = _tile(h, (128, 64, 32, 16))

    def fn(i, a, b):
        return (a + b,)

    return _rowcall(name, fn, h, tm, [(own, "row", None), (other, "row", None)], [(own.shape, BF16, "row")])[0]


def _chip_add(name, own, parts):
    h = parts.shape[1]
    tm = _tile(h, (128, 64, 32, 16))

    def fn(i, o, a):
        a = a.astype(F32)
        return (((o.astype(F32) + a[0]) + a[1]) + a[2],)

    return _rowcall(name, fn, h, tm, [(own, "row", None), (parts, "row", None)], [(parts.shape[1:], F32, "row")])[0]


def _adamw(name, w, g, m, v):
    R = w.shape[0]
    tm = _tile(R, (256, 128, 64, 32, 16, 8))

    def fn(i, wv, gv, mv, vv):
        m2 = ADAM_B1 * mv + (1.0 - ADAM_B1) * gv
        v2 = ADAM_B2 * vv + (1.0 - ADAM_B2) * (gv * gv)
        m_hat = m2 / (1.0 - ADAM_B1 ** ADAM_STEP)
        v_hat = v2 / (1.0 - ADAM_B2 ** ADAM_STEP)
        delta = -ADAM_LR * (m_hat / (jnp.sqrt(v_hat) + ADAM_EPS) + ADAM_WD * wv)
        return delta, m2, v2

    ins = [(a, "row", None) for a in (w, g, m, v)]
    return _rowcall(name, fn, R, tm, ins, [(w.shape, F32, "row")] * 3)


def _pack(arrs):
    flat = jnp.concatenate([a.reshape(-1).astype(F32) for a in arrs])
    size = flat.shape[0]
    padded = -(-size // (SUBLANE * LANE)) * SUBLANE * LANE
    return jnp.pad(flat, (0, padded - size)).reshape(-1, LANE)


def _unpack(buf, shapes):
    flat = buf.reshape(-1)
    out, off = [], 0
    for s in shapes:
        size = math.prod(s)
        out.append(flat[off:off + size].reshape(s))
        off += size
    return out


BIG = ("a_w_in", "a_w_out", "w_kv", "b_w_q", "b_w_o", "ffn_w_up", "ffn_w_down")
WEIGHTS = ("a_norm_w", "a_w_in", "a_conv_w", "a_a_log", "a_dt_bias", "a_out_norm_w", "a_w_out", "kv_norm_w", "w_kv",
           "b_norm_w", "b_w_q", "b_sinks", "b_w_o", "rel_bias_table", "ffn_norm_w", "ffn_w_up", "ffn_conv_w",
           "ffn_conv_b", "ffn_w_down", "final_norm_w")
SMALL = tuple(n for n in WEIGHTS if n not in BIG)
SMALL_SHARDED = {"a_norm_w": 1, "a_conv_w": 2, "ffn_conv_w": 2}


def _quarter_2d(name, a):
    if name in ("ffn_w_up", "ffn_w_down"):
        return a.reshape(a.shape[0] * a.shape[1], a.shape[2])
    return a.reshape(a.shape[-2], a.shape[-1])


def _whole_weights(w):
    bigs = [_quarter_2d(n, w[n]).astype(BF16) for n in BIG]
    smalls = [w["a_norm_w"], w["a_conv_w"][0], w["ffn_conv_w"].reshape(6, DFF2_SHARD)]
    g = _all_gather(bigs + smalls, [True] * len(bigs) + [False] * len(smalls))
    w_in = g[0].transpose(1, 0, 2).reshape(D, GDN_IN)
    nh = GDN_V_HEADS
    zpad = jnp.zeros((D, LANE - nh), BF16)
    w_in_ba = jnp.concatenate([w_in[:, GDN_MAIN:GDN_MAIN + nh], zpad, w_in[:, GDN_MAIN + nh:], zpad], axis=1)
    lane_pad = lambda a: jnp.pad(a, ((0, 0), (0, LANE - nh)))
    return dict(
        a_norm_w=g[7].reshape(1, D), w_in_main=w_in[:, :GDN_MAIN], w_in_ba=w_in_ba,
        a_conv_w=g[8].transpose(1, 0, 2).reshape(4, GDN_CONV), a_log=lane_pad(w["a_a_log"]),
        dt_bias=lane_pad(w["a_dt_bias"]), a_out_norm_w=w["a_out_norm_w"], w_out=g[1].reshape(GDN_V, D),
        kv_norm_w=w["kv_norm_w"].reshape(1, D), w_kv=g[2].reshape(D, 2 * SWA_KV_HEADS * SWA_HD),
        b_norm_w=w["b_norm_w"], w_q=g[3].reshape(D, D), w_o=g[4].reshape(D, D),
        sinks=jnp.broadcast_to(w["b_sinks"].reshape(SWA_KV_HEADS, SWA_GROUP, 1), (SWA_KV_HEADS, SWA_GROUP, LANE)),
        rel_table_t=w["rel_bias_table"].T, ffn_norm_w=w["ffn_norm_w"],
        w_up=g[5].reshape(N_CHIPS, 2, D, DFF2_SHARD),
        ffn_conv_w=g[9].reshape(N_CHIPS, 2, 3, DFF2_SHARD).transpose(1, 2, 0, 3).reshape(2, 3, DFF2),
        ffn_conv_b=w["ffn_conv_b"],
        w_down=g[6].reshape(N_CHIPS, 2, DFF_SHARD, D).transpose(1, 0, 2, 3).reshape(2, DFF, D),
        final_norm_w=w["final_norm_w"].reshape(1, D),
    )


def _chip_major(name, g):
    if name == "a_w_in":
        return g.reshape(D, N_CHIPS, GDN_IN_SHARD).transpose(1, 0, 2)
    if name == "ffn_w_up":
        return g.reshape(N_CHIPS, 2 * D, DFF2_SHARD)
    if name == "ffn_w_down":
        return g.reshape(2, N_CHIPS, DFF_SHARD, D).transpose(1, 0, 2, 3).reshape(N_CHIPS, 2 * DFF_SHARD, D)
    return g.reshape(N_CHIPS, g.shape[0] // N_CHIPS, g.shape[1])


def kernel(x, a_norm_w, a_w_in, a_conv_w, a_a_log, a_dt_bias, a_out_norm_w, a_w_out, kv_norm_w, w_kv, b_norm_w, b_w_q, b_sinks, b_w_o, rel_bias_table, ffn_norm_w, ffn_w_up, ffn_conv_w, ffn_conv_b, ffn_w_down, final_norm_w, loss_target, m_a_norm_w, m_a_w_in, m_a_conv_w, m_a_a_log, m_a_dt_bias, m_a_out_norm_w, m_a_w_out, m_kv_norm_w, m_w_kv, m_b_norm_w, m_b_w_q, m_b_sinks, m_b_w_o, m_rel_bias_table, m_ffn_norm_w, m_ffn_w_up, m_ffn_conv_w, m_ffn_conv_b, m_ffn_w_down, m_final_norm_w, v_a_norm_w, v_a_w_in, v_a_conv_w, v_a_a_log, v_a_dt_bias, v_a_out_norm_w, v_a_w_out, v_kv_norm_w, v_w_kv, v_b_norm_w, v_b_w_q, v_b_sinks, v_b_w_o, v_rel_bias_table, v_ffn_norm_w, v_ffn_w_up, v_ffn_conv_w, v_ffn_conv_b, v_ffn_w_down, v_final_norm_w):
    w = dict(zip(WEIGHTS, (a_norm_w, a_w_in, a_conv_w, a_a_log, a_dt_bias, a_out_norm_w, a_w_out, kv_norm_w, w_kv,
                           b_norm_w, b_w_q, b_sinks, b_w_o, rel_bias_table, ffn_norm_w, ffn_w_up, ffn_conv_w,
                           ffn_conv_b, ffn_w_down, final_norm_w)))
    m = dict(zip(WEIGHTS, (m_a_norm_w, m_a_w_in, m_a_conv_w, m_a_a_log, m_a_dt_bias, m_a_out_norm_w, m_a_w_out,
                           m_kv_norm_w, m_w_kv, m_b_norm_w, m_b_w_q, m_b_sinks, m_b_w_o, m_rel_bias_table,
                           m_ffn_norm_w, m_ffn_w_up, m_ffn_conv_w, m_ffn_conv_b, m_ffn_w_down, m_final_norm_w)))
    v = dict(zip(WEIGHTS, (v_a_norm_w, v_a_w_in, v_a_conv_w, v_a_a_log, v_a_dt_bias, v_a_out_norm_w, v_a_w_out,
                           v_kv_norm_w, v_w_kv, v_b_norm_w, v_b_w_q, v_b_sinks, v_b_w_o, v_rel_bias_table,
                           v_ffn_norm_w, v_ffn_w_up, v_ffn_conv_w, v_ffn_conv_b, v_ffn_w_down, v_final_norm_w)))
    T = x.shape[1]
    chip = 2 * lax.axis_index("x") + lax.axis_index("y")

    loss_part, dx, grads = _local_step(x.reshape(T, D), loss_target.reshape(T, D), _whole_weights(w))

    core = lax.axis_index("c")
    whole = [_chip_major(n, grads[n]) for n in BIG]
    other = _pair_swap(whole)
    own = [lax.dynamic_slice_in_dim(g, core * (g.shape[1] // 2), g.shape[1] // 2, 1) for g in whole]
    pair = [_pair_add(f"pair_add_{n}", a, b) for n, a, b in zip(BIG, own, other)]
    parts = _chip_scatter(pair)
    halves = [_chip_add(f"chip_add_{n}", lax.dynamic_index_in_dim(a, chip, 0, keepdims=False), b)
              for n, a, b in zip(BIG, pair, parts)]
    quarter = _pair_share(halves)
    out_g, out_d, out_m, out_v = {}, {}, {}, {}
    for n, g2 in zip(BIG, quarter):
        res = _adamw(f"adamw_{n}", _quarter_2d(n, w[n]), g2, _quarter_2d(n, m[n]), _quarter_2d(n, v[n]))
        out_g[n] = g2.reshape(w[n].shape)
        out_d[n], out_m[n], out_v[n] = (r.reshape(w[n].shape) for r in res)

    whole = [grads[n] for n in SMALL]
    summed = _unpack(_small_all_reduce(_pack([loss_part[0:1, 0:1]] + whole)), [(1, 1)] + [a.shape for a in whole])
    loss = summed[0].reshape(())
    small_g = []
    for n, g in zip(SMALL, summed[1:]):
        if n in SMALL_SHARDED:
            axis = SMALL_SHARDED[n]
            g = g.reshape(w[n].shape[:axis] + (-1,) + w[n].shape[axis + 1:])
            size = w[n].shape[axis]
            g = lax.dynamic_slice_in_dim(g, chip * size, size, axis)
        small_g.append(g.reshape(w[n].shape))
    shapes = [w[n].shape for n in SMALL]
    res = _adamw("adamw_small", _pack([w[n] for n in SMALL]), _pack(small_g), _pack([m[n] for n in SMALL]),
                 _pack([v[n] for n in SMALL]))
    small_d, small_m, small_v = (_unpack(r, shapes) for r in res)
    for i, n in enumerate(SMALL):
        out_g[n], out_d[n], out_m[n], out_v[n] = small_g[i], small_d[i], small_m[i], small_v[i]

    return (loss, dx.reshape(x.shape), *[out_g[n] for n in WEIGHTS], *[out_d[n] for n in WEIGHTS],
            *[out_m[n] for n in WEIGHTS], *[out_v[n] for n in WEIGHTS])
```

```python
import functools
import math

import jax
import jax.numpy as jnp
from jax import lax
from jax.experimental import pallas as pl
from jax.experimental.pallas import tpu as pltpu

F32 = jnp.float32
BF16 = jnp.bfloat16
MESH = pl.DeviceIdType.MESH
HIGHEST = lax.Precision.HIGHEST

D = 1024
EPS = 1e-6
NEG_INF = -1e30
N_CHIPS = 4

GDN_QK_HEADS = 8
GDN_V_HEADS = 16
GDN_HD = 128
GDN_QK = GDN_QK_HEADS * GDN_HD
GDN_V = GDN_V_HEADS * GDN_HD
GDN_CONV = 2 * GDN_QK + GDN_V
GDN_MAIN = GDN_CONV + GDN_V
GDN_IN = GDN_MAIN + 2 * GDN_V_HEADS
GDN_IN_SHARD = GDN_IN // N_CHIPS
GDN_CHUNK = 64

SWA_Q_HEADS = 16
SWA_KV_HEADS = 4
SWA_GROUP = 4
SWA_HD = 64
SWA_BLOCK = 128
REL_BUCKETS = 32
REL_MAX_DISTANCE = 128

DFF = 2816
DFF2 = 2 * DFF
DFF2_SHARD = DFF2 // N_CHIPS
DFF_SHARD = DFF // N_CHIPS

ADAM_LR = 0.001
ADAM_B1 = 0.9
ADAM_B2 = 0.999
ADAM_EPS = 1e-08
ADAM_WD = 0.01
ADAM_STEP = 10

LANE = 128
SUBLANE = 8
VMEM_LIMIT = 56 * 1024 * 1024


def _params(sem, vmem=VMEM_LIMIT):
    return pltpu.CompilerParams(dimension_semantics=sem, vmem_limit_bytes=vmem)


def _rowcall(name, fn, T, tm, ins, outs):
    n = T // tm
    r8 = tm // SUBLANE
    last8 = T // SUBLANE - 1
    arrays, in_specs = [], []
    for arr, kind, cols in ins:
        arrays.append(arr)
        if kind == "full":
            in_specs.append(pl.BlockSpec(arr.shape, functools.partial(lambda nd, i: (0,) * nd, arr.ndim)))
        elif arr.ndim == 2:
            w, ci = cols if cols is not None else (arr.shape[1], 0)
            if kind == "row":
                in_specs.append(pl.BlockSpec((tm, w), functools.partial(lambda ci, i: (i, ci), ci)))
            elif kind == "prev":
                in_specs.append(pl.BlockSpec(
                    (SUBLANE, w), functools.partial(lambda ci, i: (jnp.maximum(i * r8 - 1, 0), ci), ci)))
            else:
                in_specs.append(pl.BlockSpec(
                    (SUBLANE, w), functools.partial(lambda ci, i: (jnp.minimum((i + 1) * r8, last8), ci), ci)))
        else:
            lead = arr.shape[:-2]
            in_specs.append(pl.BlockSpec(lead + (tm, arr.shape[-1]),
                                         functools.partial(lambda nl, i: (0,) * nl + (i, 0), len(lead))))
    out_shape, out_specs = [], []
    for shape, dtype, kind in outs:
        out_shape.append(jax.ShapeDtypeStruct(shape, dtype))
        if kind == "acc":
            out_specs.append(pl.BlockSpec(shape, functools.partial(lambda nd, i: (0,) * nd, len(shape))))
        else:
            lead = shape[:-2]
            out_specs.append(pl.BlockSpec(lead + (tm, shape[-1]),
                                          functools.partial(lambda nl, i: (0,) * nl + (i, 0), len(lead))))
    nin = len(arrays)

    def body(*refs):
        i = pl.program_id(0)
        vals = [r[...] for r in refs[:nin]]
        res = fn(i, *vals)
        for (shape, dtype, kind), o, r in zip(outs, refs[nin:], res):
            if kind == "row":
                o[...] = r.astype(dtype)
            else:
                @pl.when(i == 0)
                def _():
                    o[...] = r.astype(dtype)

                @pl.when(i > 0)
                def _():
                    o[...] += r.astype(dtype)

    res = pl.pallas_call(
        body, name=name, grid=(n,), in_specs=in_specs, out_specs=out_specs, out_shape=out_shape,
        compiler_params=_params(("arbitrary",)),
    )(*arrays)
    return res


def _mm(name, a, b, out_shape, out_dtype, grid, a_spec, b_spec, o_spec, dims, acc_shape, res=None, precision=None,
        into=None):
    nk = grid[2]
    n_in = 2 + (res is not None) + (into is not None)

    def body(*refs):
        a_ref, b_ref, o_ref = refs[0], refs[1], refs[n_in]
        r_ref = refs[2] if res is not None else None
        av, bv = a_ref[...], b_ref[...]
        if precision is None:
            av, bv = av.astype(BF16), bv.astype(BF16)
        p = lax.dot_general(av, bv, (dims, ((), ())), preferred_element_type=F32, precision=precision)

        def finish(x):
            if res is not None:
                x = x + r_ref[...].astype(F32)
            o_ref[...] = x.astype(out_dtype).reshape(o_ref.shape)

        if nk == 1:
            finish(p)
        else:
            acc = refs[-1]
            k = pl.program_id(2)

            @pl.when(k == 0)
            def _():
                acc[...] = p

            @pl.when(k > 0)
            def _():
                acc[...] += p

            @pl.when(k == nk - 1)
            def _():
                finish(acc[...])

    ops = [a, b] + ([res] if res is not None else []) + ([into] if into is not None else [])
    specs = [a_spec, b_spec] + ([o_spec] if res is not None else [])
    specs += [pl.BlockSpec(memory_space=pl.ANY)] if into is not None else []
    return pl.pallas_call(
        body, name=name, grid=grid, in_specs=specs, out_specs=o_spec,
        out_shape=jax.ShapeDtypeStruct(out_shape, out_dtype),
        input_output_aliases={n_in - 1: 0} if into is not None else {},
        scratch_shapes=[pltpu.VMEM(acc_shape, F32)] if nk > 1 else [],
        compiler_params=_params(("parallel", "parallel", "arbitrary")),
    )(*ops)


NN = ((1,), (0,))
NT = ((1,), (1,))
TN = ((0,), (0,))


BIG_TILES = (1024, 512, 256, 128)


def _tile(n, pref):
    for t in pref:
        if n % t == 0:
            return t
    return n


def _mm_nn(name, a, w, out_dtype, res=None, precision=None):
    M, K = a.shape
    N = w.shape[1]
    tm = _tile(M, BIG_TILES if K <= 2048 else BIG_TILES[1:])
    tn = _tile(N, BIG_TILES)
    return _mm(name, a, w, (M, N), out_dtype, (M // tm, N // tn, 1),
               pl.BlockSpec((tm, K), lambda i, j, k: (i, 0)), pl.BlockSpec((K, tn), lambda i, j, k: (0, j)),
               pl.BlockSpec((tm, tn), lambda i, j, k: (i, j)), NN, (tm, tn), res=res, precision=precision)


def _mm_nt(name, g, w, out_dtype, res=None, precision=None):
    M, N = g.shape
    K = w.shape[0]
    tm, tk = _tile(M, BIG_TILES), _tile(K, (1024, 1408, 512, 256, 128))
    tn = _tile(N, (1536,) + BIG_TILES)
    return _mm(name, g, w, (M, K), out_dtype, (M // tm, K // tk, N // tn),
               pl.BlockSpec((tm, tn), lambda i, j, k: (i, k)), pl.BlockSpec((tk, tn), lambda i, j, k: (j, k)),
               pl.BlockSpec((tm, tk), lambda i, j, k: (i, j)), NT, (tm, tk), res=res, precision=precision)


def _mm_tn(name, a, g, out_dtype=F32, precision=None):
    T, K = a.shape
    N = g.shape[1]
    tk, tn = _tile(K, (1024, 1408, 512, 256, 128)), _tile(N, BIG_TILES)
    tt = _tile(T, BIG_TILES)
    return _mm(name, a, g, (K, N), out_dtype, (K // tk, N // tn, T // tt),
               pl.BlockSpec((tt, tk), lambda i, j, k: (k, i)), pl.BlockSpec((tt, tn), lambda i, j, k: (k, j)),
               pl.BlockSpec((tk, tn), lambda i, j, k: (i, j)), TN, (tk, tn), precision=precision)


def _mm_up(name, n, wup, layer):
    T = n.shape[0]
    tm = _tile(T, BIG_TILES)
    return _mm(name, n, wup, (T, DFF2), BF16, (T // tm, N_CHIPS, 1),
               pl.BlockSpec((tm, D), lambda i, j, k: (i, 0)),
               pl.BlockSpec((None, None, D, DFF2_SHARD), lambda i, j, k: (j, layer, 0, 0)),
               pl.BlockSpec((tm, DFF2_SHARD), lambda i, j, k: (i, j)), NN, (tm, DFF2_SHARD))


def _mm_up_nt(name, du, wup, layer):
    T = du.shape[0]
    tm, tk = _tile(T, BIG_TILES), D
    return _mm(name, du, wup, (T, D), F32, (T // tm, D // tk, N_CHIPS),
               pl.BlockSpec((tm, DFF2_SHARD), lambda i, j, k: (i, k)),
               pl.BlockSpec((None, None, tk, DFF2_SHARD), lambda i, j, k: (k, layer, j, 0)),
               pl.BlockSpec((tm, tk), lambda i, j, k: (i, j)), NT, (tm, tk))


def _mm_up_tn(name, n, du, layer, into):
    T = n.shape[0]
    tk, tt = D, _tile(T, BIG_TILES)
    return _mm(name, n, du, (N_CHIPS, 2, D, DFF2_SHARD), F32, (D // tk, N_CHIPS, T // tt),
               pl.BlockSpec((tt, tk), lambda i, j, k: (k, i)), pl.BlockSpec((tt, DFF2_SHARD), lambda i, j, k: (k, j)),
               pl.BlockSpec((None, None, tk, DFF2_SHARD), lambda i, j, k: (j, layer, i, 0)), TN, (tk, DFF2_SHARD),
               into=into)


def _mm_down_tn(name, act, dout, layer, into):
    T = act.shape[0]
    tk, tn, tt = 2 * DFF_SHARD, _tile(D, BIG_TILES), _tile(T, BIG_TILES)
    return _mm(name, act, dout, (2, 2, 2, DFF_SHARD, D), F32, (DFF // tk, D // tn, T // tt),
               pl.BlockSpec((tt, tk), lambda i, j, k: (k, i)), pl.BlockSpec((tt, tn), lambda i, j, k: (k, j)),
               pl.BlockSpec((None, 2, None, DFF_SHARD, tn), lambda i, j, k: (i, 0, layer, 0, j)), TN, (tk, tn),
               into=into)


def _sigmoid(x):
    return 0.5 * jnp.tanh(0.5 * x) + 0.5


def _silu(x):
    return x * _sigmoid(x)


def _softplus(x):
    return jnp.maximum(x, 0.0) + jnp.log(1.0 + jnp.exp(-jnp.abs(x)))


def _rms_core(h, w):
    return h * lax.rsqrt(jnp.mean(h * h, axis=-1, keepdims=True) + EPS) * w


def _shift_down(x, halo, s, i):
    if s == 0:
        return x
    tm = x.shape[0]
    rolled = pltpu.roll(x, s, 0)
    patch = pltpu.roll(jnp.where(i == 0, 0.0, halo), s, 0)
    row = lax.broadcasted_iota(jnp.int32, patch.shape, 0)
    top = jnp.where(row < s, patch, rolled[:SUBLANE])
    return jnp.concatenate([top, rolled[SUBLANE:]], axis=0) if tm > SUBLANE else top


def _shift_up(x, halo, s, i, n):
    if s == 0:
        return x
    tm = x.shape[0]
    rolled = pltpu.roll(x, tm - s, 0)
    patch = pltpu.roll(jnp.where(i == n - 1, 0.0, halo), SUBLANE - s, 0)
    row = lax.broadcasted_iota(jnp.int32, patch.shape, 0)
    bottom = jnp.where(row >= SUBLANE - s, patch, rolled[tm - SUBLANE:])
    return jnp.concatenate([rolled[:tm - SUBLANE], bottom], axis=0) if tm > SUBLANE else bottom


def _taps(x, halo, K, i):
    return [_shift_down(x, halo, K - 1 - j, i) for j in range(K)]


def _conv_fwd(taps, w):
    y = w[0:1, :] * taps[0]
    for j in range(1, len(taps)):
        y = y + w[j:j + 1, :] * taps[j]
    return y


def _conv_dx(dy, halo_next, w, i, n):
    K = w.shape[0]
    dx = w[K - 1:K, :] * dy
    for j in range(K - 1):
        dx = dx + w[j:j + 1, :] * _shift_up(dy, halo_next, K - 1 - j, i, n)
    return dx


def _conv_dw(dy, taps):
    rows = [jnp.sum(dy * tap, axis=0, keepdims=True) for tap in taps]
    return jnp.concatenate(rows + [jnp.zeros((SUBLANE - len(taps), dy.shape[1]), F32)], axis=0)


def _rms_fwd(name, h, w, tm=512):
    T = h.shape[0]
    tm = min(tm, T)

    def fn(i, hv, wv):
        return (_rms_core(hv, wv),)

    return _rowcall(name, fn, T, tm, [(h, "row", None), (w, "full", None)], [((T, D), BF16, "row")])[0]


def _rms_bwd(name, h, pairs, adds, tm=256):
    T = h.shape[0]
    tm = min(tm, T)
    npair, nadd = len(pairs), len(adds)

    def fn(i, hv, *rest):
        ws, dns, ads = rest[:npair], rest[npair:2 * npair], rest[2 * npair:]
        dh = None
        dws = []
        for wv, dn in zip(ws, dns):
            _, vjp = jax.vjp(_rms_core, hv, wv)
            dhi, dwi = vjp(dn.astype(F32))
            dh = dhi if dh is None else dh + dhi
            dws.append(dwi)
        for a in ads:
            dh = dh + a.astype(F32)
        return (dh, *dws)

    ins = [(h, "row", None)] + [(w, "full", None) for w, _ in pairs] + [(dn, "row", None) for _, dn in pairs]
    ins += [(a, "row", None) for a in adds]
    outs = [((T, D), F32, "row")] + [((1, D), F32, "acc")] * npair
    return _rowcall(name, fn, T, tm, ins, outs)


def _l2(x):
    return x * lax.rsqrt(jnp.sum(x * x, axis=-1, keepdims=True) + EPS)


def _gdn_post_core(yq, yk, yv, pb, pa, a_log, dtb):
    qn = tuple(_l2(_silu(a)) * (GDN_HD ** -0.5) for a in yq)
    kn = tuple(_l2(_silu(a)) for a in yk)
    v = _silu(yv)
    beta = _sigmoid(pb)
    g = -jnp.exp(a_log) * _softplus(pa + dtb)
    return qn, kn, v, beta, g


def _heads(x, n):
    return tuple(x[:, GDN_HD * h:GDN_HD * (h + 1)] for h in range(n))


def _gdn_pre_fwd(pm, pba, conv_w, a_log, dtb, tm=128):
    T = pm.shape[0]
    tm = min(tm, T)

    def fn(i, x, halo, pbav, cw, al, db):
        y = _conv_fwd(_taps(x.astype(F32), halo.astype(F32), 4, i), cw)
        qn, kn, v, beta, g = _gdn_post_core(_heads(y[:, :GDN_QK], 8), _heads(y[:, GDN_QK:2 * GDN_QK], 8),
                                            y[:, 2 * GDN_QK:], pbav[:, :LANE], pbav[:, LANE:], al, db)
        return jnp.stack(qn), jnp.stack(kn), jnp.stack(_heads(v, GDN_V_HEADS)), beta, g

    ins = [(pm, "row", (GDN_CONV, 0)), (pm, "prev", (GDN_CONV, 0)), (pba, "row", None),
           (conv_w, "full", None), (a_log, "full", None), (dtb, "full", None)]
    outs = [((GDN_QK_HEADS, T, GDN_HD), BF16, "row"), ((GDN_QK_HEADS, T, GDN_HD), BF16, "row"),
            ((GDN_V_HEADS, T, GDN_HD), BF16, "row"), ((T, LANE), F32, "row"), ((T, LANE), F32, "row")]
    return _rowcall("gdn_pre_fwd", fn, T, tm, ins, outs)


def _gdn_pre_bwd(pm, pba, conv_w, a_log, dtb, dqn, dkn, dv, dbeta, dg, tm=128):
    T = pm.shape[0]
    tm = min(tm, T)

    def fn(i, x, halo, pbav, cw, al, db, dqv, dkv, dvv, dbv, dgv):
        taps = _taps(x.astype(F32), halo.astype(F32), 4, i)
        y = _conv_fwd(taps, cw)
        prim = (_heads(y[:, :GDN_QK], 8), _heads(y[:, GDN_QK:2 * GDN_QK], 8), y[:, 2 * GDN_QK:],
                pbav[:, :LANE], pbav[:, LANE:], al, db)
        _, vjp = jax.vjp(_gdn_post_core, *prim)
        cot = (tuple(dqv[h].astype(F32) for h in range(8)), tuple(dkv[h].astype(F32) for h in range(8)),
               jnp.concatenate([dvv[h].astype(F32) for h in range(GDN_V_HEADS)], axis=1), dbv, dgv)
        dyq, dyk, dyv, dpb, dpa, dal, ddb = vjp(cot)
        dy = jnp.concatenate(list(dyq) + list(dyk) + [dyv], axis=1)
        dcw = _conv_dw(dy, taps)
        return dy, jnp.concatenate([dpb, dpa], axis=1), dcw, dal, ddb

    ins = [(pm, "row", (GDN_CONV, 0)), (pm, "prev", (GDN_CONV, 0)), (pba, "row", None),
           (conv_w, "full", None), (a_log, "full", None), (dtb, "full", None),
           (dqn, "row", None), (dkn, "row", None), (dv, "row", None), (dbeta, "row", None), (dg, "row", None)]
    outs = [((T, GDN_CONV), BF16, "row"), ((T, 2 * LANE), F32, "row"), ((SUBLANE, GDN_CONV), F32, "acc"),
            ((1, LANE), F32, "acc"), ((1, LANE), F32, "acc")]
    return _rowcall("gdn_pre_bwd", fn, T, tm, ins, outs)


def _gdn_conv_bwd(dy, dz, conv_w, tm=256):
    T = dy.shape[0]
    tm = min(tm, T)
    n = T // tm

    def fn(i, dyv, halo, dzv, cw):
        dx = _conv_dx(dyv.astype(F32), halo.astype(F32), cw, i, n)
        return (jnp.concatenate([dx.astype(BF16), dzv.astype(BF16)], axis=1),)

    ins = [(dy, "row", None), (dy, "next", None), (dz, "row", None), (conv_w, "full", None)]
    return _rowcall("gdn_conv_bwd", fn, T, tm, ins, [((T, GDN_MAIN), BF16, "row")])[0]


def _bdot(a, b, dims=NN):
    return lax.dot_general(a.astype(BF16), b.astype(BF16), (dims, ((), ())), preferred_element_type=F32)


BNN = ((2,), (1,))
BNT = ((2,), (2,))
BTN = ((1,), (1,))


def _bmm(a, b, dims=BNN):
    return lax.dot_general(a.astype(BF16), b.astype(BF16), (dims, ((0,), (0,))), preferred_element_type=F32)


def _bmm3(a, b):
    ah, bh = a.astype(BF16), b.astype(BF16)
    al, bl = (a - ah.astype(F32)).astype(BF16), (b - bh.astype(F32)).astype(BF16)
    dn = (BNN, ((0,), (0,)))
    return (lax.dot_general(ah, bh, dn, preferred_element_type=F32)
            + lax.dot_general(al, bh, dn, preferred_element_type=F32)
            + lax.dot_general(ah, bl, dn, preferred_element_type=F32))


def _tri_inv(m):
    C = m.shape[-1]
    r = lax.broadcasted_iota(jnp.int32, (C, C), 0)
    c = lax.broadcasted_iota(jnp.int32, (C, C), 1)
    t = jnp.where(r == c, 1.0, 0.0) - m
    pw = _bmm3(m, m)
    t = t + _bmm3(t, pw)
    for _ in range(int(math.log2(C)) - 2):
        pw = _bmm(pw, pw)
        t = t + _bmm(t, pw)
    return t


def _tri_inv_vjp(t, dt):
    tt = jnp.swapaxes(t, 1, 2)
    return -_bmm(_bmm(tt, dt), tt)


def _twice(a):
    return jnp.broadcast_to(a[:, None], (a.shape[0], 2) + a.shape[1:]).reshape((2 * a.shape[0],) + a.shape[1:])


def _gdn_gates(grow, brow):
    C = grow.shape[2]
    r = lax.broadcasted_iota(jnp.int32, (C, C), 0)
    c = lax.broadcasted_iota(jnp.int32, (C, C), 1)
    tril, eye = r >= c, r == c
    gcol = jnp.sum(jnp.where(eye, grow, 0.0), axis=2, keepdims=True)
    bcol = jnp.sum(jnp.where(eye, brow, 0.0), axis=2, keepdims=True)
    gc_col = jnp.sum(jnp.where(tril, grow, 0.0), axis=2, keepdims=True)
    gc_row = jnp.sum(jnp.where(r <= c, gcol, 0.0), axis=1, keepdims=True)
    gc_last = jnp.sum(grow, axis=2, keepdims=True)
    decay = jnp.where(tril, jnp.exp(jnp.where(tril, gc_col - gc_row, 0.0)), 0.0)
    return bcol, gc_col, gc_last, decay


def _gdn_m(k, grow, brow):
    C = k.shape[1]
    strict = lax.broadcasted_iota(jnp.int32, (C, C), 0) > lax.broadcasted_iota(jnp.int32, (C, C), 1)
    bcol, _, _, decay = _gdn_gates(grow, brow)
    return jnp.where(strict, bcol * _twice(_bmm(k, k, BNT)) * decay, 0.0)


def _gdn_rest(q, k, v, grow, brow, t_mat, S):
    bcol, gc_col, gc_last, decay = _gdn_gates(grow, brow)
    qk = _twice(_bmm(q, k, BNT))
    k2, q2 = _twice(k), _twice(q)
    egc = jnp.exp(gc_col)
    u = _bmm(t_mat, v * bcol)
    w = _bmm(t_mat, k2 * (bcol * egc))
    v_new = u - _bmm(w, S)
    o = _bmm(q2 * egc, S) + _bmm(qk * decay, v_new)
    s_new = S * jnp.exp(gc_last) + _bmm(k2 * jnp.exp(gc_last - gc_col), v_new, BTN)
    return o, s_new


def _gdn_tb(T):
    return min(256, T)


def _gate_rows(g):
    T = g.shape[0]
    g = g[:, :GDN_V_HEADS].reshape(T // GDN_CHUNK, GDN_CHUNK, GDN_V_HEADS)
    return g.transpose(0, 2, 1)[:, :, None, :]


def _gate_cols(g):
    nc = g.shape[0]
    g = g[:, :, 0, :].transpose(0, 2, 1).reshape(nc * GDN_CHUNK, GDN_V_HEADS)
    return jnp.pad(g, ((0, 0), (0, LANE - GDN_V_HEADS)))


def _gdn_fwd(qn, kn, v, g, beta):
    T = qn.shape[1]
    tb = _gdn_tb(T)
    nc = tb // GDN_CHUNK

    def body(q_ref, k_ref, v_ref, g_ref, b_ref, o_ref, sall_ref, tall_ref, s_scr):
        @pl.when(pl.program_id(0) == 0)
        def _():
            s_scr[...] = jnp.zeros(s_scr.shape, F32)

        def chunk(ci, carry):
            rows = pl.ds(pl.multiple_of(ci * GDN_CHUNK, GDN_CHUNK), GDN_CHUNK)
            s = s_scr[...]
            sall_ref[ci] = s
            q, k = q_ref[:, rows, :].astype(F32), k_ref[:, rows, :].astype(F32)
            t_mat = _tri_inv(_gdn_m(k, g_ref[ci], b_ref[ci])).astype(BF16)
            tall_ref[ci] = t_mat
            o, s_new = _gdn_rest(q, k, v_ref[:, rows, :].astype(F32), g_ref[ci], b_ref[ci], t_mat.astype(F32), s)
            o_ref[:, rows, :] = o.astype(o_ref.dtype)
            s_scr[...] = s_new
            return carry

        lax.fori_loop(0, nc, chunk, 0)

    qk_spec = pl.BlockSpec((GDN_QK_HEADS, tb, GDN_HD), lambda i: (0, i, 0))
    v_spec = pl.BlockSpec((GDN_V_HEADS, tb, GDN_HD), lambda i: (0, i, 0))
    g_spec = pl.BlockSpec((nc, GDN_V_HEADS, 1, GDN_CHUNK), lambda i: (i, 0, 0, 0))
    return pl.pallas_call(
        body, name="gdn_fwd", grid=(T // tb,),
        in_specs=[qk_spec, qk_spec, v_spec, g_spec, g_spec],
        out_specs=[v_spec, pl.BlockSpec((nc, GDN_V_HEADS, GDN_HD, GDN_HD), lambda i: (i, 0, 0, 0)),
                   pl.BlockSpec((nc, GDN_V_HEADS, GDN_CHUNK, GDN_CHUNK), lambda i: (i, 0, 0, 0))],
        out_shape=[jax.ShapeDtypeStruct((GDN_V_HEADS, T, GDN_HD), BF16),
                   jax.ShapeDtypeStruct((T // GDN_CHUNK, GDN_V_HEADS, GDN_HD, GDN_HD), F32),
                   jax.ShapeDtypeStruct((T // GDN_CHUNK, GDN_V_HEADS, GDN_CHUNK, GDN_CHUNK), BF16)],
        scratch_shapes=[pltpu.VMEM((GDN_V_HEADS, GDN_HD, GDN_HD), F32)],
        compiler_params=_params(("arbitrary",)),
    )(qn, kn, v, g, beta)


def _gdn_bwd(qn, kn, v, g, beta, sall, tall, do):
    T = qn.shape[1]
    tb = _gdn_tb(T)
    nc = tb // GDN_CHUNK
    nb = T // tb

    def body(q_ref, k_ref, v_ref, g_ref, b_ref, sall_ref, tall_ref, do_ref,
             dq_ref, dk_ref, dv_ref, dg_ref, db_ref, ds_scr):
        @pl.when(pl.program_id(0) == 0)
        def _():
            ds_scr[...] = jnp.zeros(ds_scr.shape, F32)

        def chunk(cr, carry):
            ci = nc - 1 - cr
            rows = pl.ds(pl.multiple_of(ci * GDN_CHUNK, GDN_CHUNK), GDN_CHUNK)
            k, t_mat = k_ref[:, rows, :].astype(F32), tall_ref[ci].astype(F32)
            _, vjp = jax.vjp(_gdn_rest, q_ref[:, rows, :].astype(F32), k, v_ref[:, rows, :].astype(F32),
                             g_ref[ci], b_ref[ci], t_mat, sall_ref[ci])
            dq, dk, dv, dg, db, dt, ds = vjp((do_ref[:, rows, :].astype(F32), ds_scr[...]))
            _, vjp_m = jax.vjp(_gdn_m, k, g_ref[ci], b_ref[ci])
            dk_m, dg_m, db_m = vjp_m(_tri_inv_vjp(t_mat, dt))
            ds_scr[...] = ds
            dq_ref[:, rows, :] = dq
            dk_ref[:, rows, :] = dk + dk_m
            dv_ref[:, rows, :] = dv
            dg_ref[ci] = dg + dg_m
            db_ref[ci] = db + db_m
            return carry

        lax.fori_loop(0, nc, chunk, 0)

    qk_spec = pl.BlockSpec((GDN_QK_HEADS, tb, GDN_HD), lambda i: (0, nb - 1 - i, 0))
    v_spec = pl.BlockSpec((GDN_V_HEADS, tb, GDN_HD), lambda i: (0, nb - 1 - i, 0))
    g_spec = pl.BlockSpec((nc, GDN_V_HEADS, 1, GDN_CHUNK), lambda i: (nb - 1 - i, 0, 0, 0))
    s_spec = pl.BlockSpec((nc, GDN_V_HEADS, GDN_HD, GDN_HD), lambda i: (nb - 1 - i, 0, 0, 0))
    t_spec = pl.BlockSpec((nc, GDN_V_HEADS, GDN_CHUNK, GDN_CHUNK), lambda i: (nb - 1 - i, 0, 0, 0))
    return pl.pallas_call(
        body, name="gdn_bwd", grid=(nb,),
        in_specs=[qk_spec, qk_spec, v_spec, g_spec, g_spec, s_spec, t_spec, v_spec],
        out_specs=[qk_spec, qk_spec, v_spec, g_spec, g_spec],
        out_shape=[jax.ShapeDtypeStruct((GDN_QK_HEADS, T, GDN_HD), F32),
                   jax.ShapeDtypeStruct((GDN_QK_HEADS, T, GDN_HD), F32),
                   jax.ShapeDtypeStruct((GDN_V_HEADS, T, GDN_HD), F32),
                   jax.ShapeDtypeStruct(g.shape, F32), jax.ShapeDtypeStruct(g.shape, F32)],
        scratch_shapes=[pltpu.VMEM((GDN_V_HEADS, GDN_HD, GDN_HD), F32)],
        compiler_params=_params(("arbitrary",)),
    )(qn, kn, v, g, beta, sall, tall, do)


def _gnorm_core(o, z, w):
    return tuple(_rms_core(oh, w) * _silu(zh) for oh, zh in zip(o, z))


def _gnorm_fwd(o, pm, w, tm=256):
    T = pm.shape[0]
    tm = min(tm, T)

    def fn(i, ov, zv, wv):
        zf = zv.astype(F32)
        out = _gnorm_core(tuple(ov[h].astype(F32) for h in range(GDN_V_HEADS)), _heads(zf, GDN_V_HEADS), wv)
        return (jnp.concatenate(out, axis=1),)

    ins = [(o, "row", None), (pm, "row", (GDN_V, 2)), (w, "full", None)]
    return _rowcall("gnorm_fwd", fn, T, tm, ins, [((T, GDN_V), BF16, "row")])[0]


def _gnorm_bwd(o, pm, w, don, tm=128):
    T = pm.shape[0]
    tm = min(tm, T)

    def fn(i, ov, zv, wv, dv):
        zf, df = zv.astype(F32), dv.astype(F32)
        _, vjp = jax.vjp(_gnorm_core, tuple(ov[h].astype(F32) for h in range(GDN_V_HEADS)),
                         _heads(zf, GDN_V_HEADS), wv)
        do, dz, dw = vjp(_heads(df, GDN_V_HEADS))
        return jnp.stack(do), jnp.concatenate(dz, axis=1), dw

    ins = [(o, "row", None), (pm, "row", (GDN_V, 2)), (w, "full", None), (don, "row", None)]
    outs = [((GDN_V_HEADS, T, GDN_HD), BF16, "row"), ((T, GDN_V), BF16, "row"), ((1, GDN_HD), F32, "acc")]
    return _rowcall("gnorm_bwd", fn, T, tm, ins, outs)


def _ffn_act_fwd(name, up, conv_w, conv_b, tm=128):
    T = up.shape[0]
    tm = min(tm, T)

    def fn(i, x, halo, cw, cb):
        u = _conv_fwd(_taps(x.astype(F32), halo.astype(F32), 3, i), cw) + cb
        return (_silu(u[:, :DFF]) * u[:, DFF:],)

    ins = [(up, "row", None), (up, "prev", None), (conv_w, "full", None), (conv_b, "full", None)]
    return _rowcall(name, fn, T, tm, ins, [((T, DFF), BF16, "row")])[0]


def _ffn_act_bwd(name, up, conv_w, conv_b, dact, tm=128):
    T = up.shape[0]
    tm = min(tm, T)

    def fn(i, x, halo, cw, cb, da):
        taps = _taps(x.astype(F32), halo.astype(F32), 3, i)
        da = da.astype(F32)
        u = _conv_fwd(taps, cw) + cb
        gate, val = u[:, :DFF], u[:, DFF:]
        sg = _sigmoid(gate)
        dgate = da * val * sg * (1.0 + gate * (1.0 - sg))
        dval = da * gate * sg
        du = jnp.concatenate([dgate, dval], axis=1)
        return du, _conv_dw(du, taps), jnp.sum(du, axis=0, keepdims=True)

    ins = [(up, "row", None), (up, "prev", None), (conv_w, "full", None), (conv_b, "full", None),
           (dact, "row", None)]
    outs = [((T, DFF2), BF16, "row"), ((SUBLANE, DFF2), F32, "acc"), ((1, DFF2), F32, "acc")]
    return _rowcall(name, fn, T, tm, ins, outs)


def _ffn_conv_bwd(name, du, conv_w, tm=256):
    T = du.shape[0]
    tm = min(tm, T)
    n = T // tm

    def fn(i, dv, halo, cw):
        return (_conv_dx(dv.astype(F32), halo.astype(F32), cw, i, n),)

    ins = [(du, "row", None), (du, "next", None), (conv_w, "full", None)]
    return _rowcall(name, fn, T, tm, ins, [((T, DFF2), BF16, "row")])[0]


GROUP_ROWS = SWA_GROUP * SWA_BLOCK


def _attn_core(q, kp, kc, vp, vc, bias, sink, mask):
    kcat = jnp.concatenate([kp, kc], axis=0)
    vcat = jnp.concatenate([vp, vc], axis=0)
    s = _bdot(q * (SWA_HD ** -0.5), kcat, NT) + bias
    s = jnp.where(mask, s, NEG_INF)
    m = lax.stop_gradient(jnp.maximum(jnp.max(s, axis=-1, keepdims=True), sink))
    p = jnp.exp(s - m)
    denom = jnp.sum(p, axis=-1, keepdims=True) + jnp.exp(sink - m)
    return _bdot(p / denom, vcat)


def _attn_mask(i):
    qi = lax.broadcasted_iota(jnp.int32, (GROUP_ROWS, 2 * SWA_BLOCK), 0) & (SWA_BLOCK - 1)
    ki = lax.broadcasted_iota(jnp.int32, (GROUP_ROWS, 2 * SWA_BLOCK), 1)
    dist = qi + SWA_BLOCK - ki
    return (dist >= 0) & (dist < SWA_BLOCK) & ((ki >= SWA_BLOCK) | (i > 0))


def _attn_operands(j, q_ref, kc_ref, kp_ref, vc_ref, vp_ref, b_ref, s_ref):
    heads = slice(SWA_GROUP * j, SWA_GROUP * (j + 1))
    sink = jnp.concatenate([jnp.broadcast_to(s_ref[j, g:g + 1, 0:1], (SWA_BLOCK, 1)) for g in range(SWA_GROUP)],
                           axis=0)
    return (q_ref[heads].astype(F32).reshape(GROUP_ROWS, SWA_HD), kp_ref[j].astype(F32), kc_ref[j].astype(F32),
            vp_ref[j].astype(F32), vc_ref[j].astype(F32), b_ref[heads].reshape(GROUP_ROWS, 2 * SWA_BLOCK), sink)


def _attn_fwd(q, k, v, bias, sinks):
    T = q.shape[1]
    nb = T // SWA_BLOCK

    def body(q_ref, kc_ref, kp_ref, vc_ref, vp_ref, b_ref, s_ref, o_ref):
        mask = _attn_mask(pl.program_id(0))
        for j in range(SWA_KV_HEADS):
            out = _attn_core(*_attn_operands(j, q_ref, kc_ref, kp_ref, vc_ref, vp_ref, b_ref, s_ref), mask)
            o_ref[SWA_GROUP * j:SWA_GROUP * (j + 1)] = out.reshape(SWA_GROUP, SWA_BLOCK, SWA_HD).astype(o_ref.dtype)

    q_spec = pl.BlockSpec((SWA_Q_HEADS, SWA_BLOCK, SWA_HD), lambda i: (0, i, 0))
    cur = pl.BlockSpec((SWA_KV_HEADS, SWA_BLOCK, SWA_HD), lambda i: (0, i, 0))
    prev = pl.BlockSpec((SWA_KV_HEADS, SWA_BLOCK, SWA_HD), lambda i: (0, jnp.maximum(i - 1, 0), 0))
    return pl.pallas_call(
        body, name="attn_fwd", grid=(nb,),
        in_specs=[q_spec, cur, prev, cur, prev, pl.BlockSpec(bias.shape, lambda i: (0, 0, 0)),
                  pl.BlockSpec(sinks.shape, lambda i: (0, 0, 0))],
        out_specs=q_spec, out_shape=jax.ShapeDtypeStruct(q.shape, BF16),
        compiler_params=_params(("arbitrary",)),
    )(q, k, k, v, v, bias, sinks)


def _attn_bwd(q, k, v, bias, sinks, do):
    T = q.shape[1]
    nb = T // SWA_BLOCK

    def body(q_ref, kc_ref, kp_ref, vc_ref, vp_ref, b_ref, s_ref, do_ref,
             dq_ref, dk_ref, dv_ref, db_ref, dsk_ref, kcar, vcar):
        i = pl.program_id(0)

        @pl.when(i < nb)
        def _():
            mask = _attn_mask(i)
            for j in range(SWA_KV_HEADS):
                heads = slice(SWA_GROUP * j, SWA_GROUP * (j + 1))
                prim = _attn_operands(j, q_ref, kc_ref, kp_ref, vc_ref, vp_ref, b_ref, s_ref)
                _, vjp = jax.vjp(functools.partial(_attn_core, mask=mask), *prim)
                dq, dkp, dkc, dvp, dvc, db, dsc = vjp(do_ref[heads].astype(F32).reshape(GROUP_ROWS, SWA_HD))
                dq_ref[heads] = dq.reshape(SWA_GROUP, SWA_BLOCK, SWA_HD).astype(dq_ref.dtype)
                db = db.reshape(SWA_GROUP, SWA_BLOCK, 2 * SWA_BLOCK)
                dsk = jnp.concatenate(
                    [jnp.broadcast_to(jnp.sum(dsc[g * SWA_BLOCK:(g + 1) * SWA_BLOCK], axis=0, keepdims=True),
                                      (1, LANE)) for g in range(SWA_GROUP)], axis=0)

                @pl.when(i == 0)
                def _():
                    db_ref[heads] = db
                    dsk_ref[j] = dsk

                @pl.when(i > 0)
                def _():
                    db_ref[heads] += db
                    dsk_ref[j] += dsk
                    dk_ref[j] = (kcar[j] + dkp).astype(dk_ref.dtype)
                    dv_ref[j] = (vcar[j] + dvp).astype(dv_ref.dtype)

                kcar[j] = dkc
                vcar[j] = dvc

        @pl.when(i == nb)
        def _():
            dk_ref[...] = kcar[...].astype(dk_ref.dtype)
            dv_ref[...] = vcar[...].astype(dv_ref.dtype)

    last = nb - 1
    q_spec = pl.BlockSpec((SWA_Q_HEADS, SWA_BLOCK, SWA_HD), lambda i: (0, jnp.minimum(i, last), 0))
    cur = pl.BlockSpec((SWA_KV_HEADS, SWA_BLOCK, SWA_HD), lambda i: (0, jnp.minimum(i, last), 0))
    prev = pl.BlockSpec((SWA_KV_HEADS, SWA_BLOCK, SWA_HD), lambda i: (0, jnp.clip(i - 1, 0, last), 0))
    b_spec = pl.BlockSpec(bias.shape, lambda i: (0, 0, 0))
    s_spec = pl.BlockSpec(sinks.shape, lambda i: (0, 0, 0))
    carry = pltpu.VMEM((SWA_KV_HEADS, SWA_BLOCK, SWA_HD), F32)
    return pl.pallas_call(
        body, name="attn_bwd", grid=(nb + 1,),
        in_specs=[q_spec, cur, prev, cur, prev, b_spec, s_spec, q_spec],
        out_specs=[q_spec, prev, prev, b_spec, s_spec],
        out_shape=[jax.ShapeDtypeStruct(q.shape, BF16), jax.ShapeDtypeStruct(k.shape, BF16),
                   jax.ShapeDtypeStruct(k.shape, BF16), jax.ShapeDtypeStruct(bias.shape, F32),
                   jax.ShapeDtypeStruct(sinks.shape, F32)],
        scratch_shapes=[carry, carry],
        compiler_params=_params(("arbitrary",)),
    )(q, k, k, v, v, bias, sinks, do)


def _rel_onehot():
    qi = jnp.arange(SWA_BLOCK)[:, None]
    ki = jnp.arange(2 * SWA_BLOCK)[None, :]
    n = jnp.maximum(qi + SWA_BLOCK - ki, 0)
    max_exact = REL_BUCKETS // 2
    nf = jnp.maximum(n, 1).astype(F32)
    large = max_exact + (jnp.log(nf / max_exact) / math.log(REL_MAX_DISTANCE / max_exact)
                         * (REL_BUCKETS - max_exact)).astype(jnp.int32)
    bucket = jnp.where(n < max_exact, n, jnp.minimum(large, REL_BUCKETS - 1)).reshape(-1)
    return (bucket[None, :] == jnp.arange(REL_BUCKETS)[:, None]).astype(F32)


def _final(h, w, target, tm=256):
    T = h.shape[0]
    tm = min(tm, T)

    def fn(i, hv, wv, tv):
        y, vjp = jax.vjp(_rms_core, hv, wv)
        err = y - tv
        dh, dw = vjp(err * (1.0 / D))
        part = 0.5 * jnp.sum(jnp.sum(err * err, axis=1, keepdims=True) * (1.0 / D), axis=0, keepdims=True)
        return jnp.broadcast_to(part, (SUBLANE, LANE)), dh, dw

    ins = [(h, "row", None), (w, "full", None), (target, "row", None)]
    outs = [((SUBLANE, LANE), F32, "acc"), ((T, D), F32, "row"), ((1, D), F32, "acc")]
    return _rowcall("final", fn, T, tm, ins, outs)


def _heads_major(a, heads, hd):
    return a.reshape(a.shape[0], heads, hd).transpose(1, 0, 2)


def _heads_minor(a):
    return a.transpose(1, 0, 2).reshape(a.shape[1], a.shape[0] * a.shape[2])


def _ffn_fwd(tag, h, P, layer):
    n = _rms_fwd(f"{tag}_rms", h, P["ffn_norm_w"][layer:layer + 1])
    up = _mm_up(f"{tag}_up", n, P["w_up"], layer)
    act = _ffn_act_fwd(f"{tag}_act", up, P["ffn_conv_w"][layer], P["ffn_conv_b"][layer:layer + 1])
    out = _mm_nn(f"{tag}_down", act, P["w_down"][layer], F32, res=h)
    return out, (n, up, act)


def _ffn_bwd(tag, h, saved, dout, P, layer, into=(None, None)):
    n, up, act = saved
    cw, cb = P["ffn_conv_w"][layer], P["ffn_conv_b"][layer:layer + 1]
    dact = _mm_nt(f"{tag}_down_dx", dout, P["w_down"][layer], BF16)
    g_down = _mm_down_tn(f"{tag}_down_dw", act, dout, layer, into[1])
    du, dcw, dcb = _ffn_act_bwd(f"{tag}_act_bwd", up, cw, cb, dact)
    dup = _ffn_conv_bwd(f"{tag}_conv_bwd", du, cw)
    g_up = _mm_up_tn(f"{tag}_up_dw", n, dup, layer, into[0])
    dn = _mm_up_nt(f"{tag}_up_dx", dup, P["w_up"], layer)
    dh, dnw = _rms_bwd(f"{tag}_rms_bwd", h, [(P["ffn_norm_w"][layer:layer + 1], dn)], [dout])
    return dh, dict(w_down=g_down, w_up=g_up, conv_w=dcw[:3], conv_b=dcb, norm_w=dnw)


def _local_step(x, target, P):
    T = x.shape[0]
    n0 = _rms_fwd("a_rms", x, P["a_norm_w"])
    pm = _mm_nn("gdn_in", n0, P["w_in_main"], BF16)
    pba = _mm_nn("gdn_in_ba", n0, P["w_in_ba"], F32)
    qn, kn, v, beta, g = _gdn_pre_fwd(pm, pba, P["a_conv_w"], P["a_log"], P["dt_bias"])
    g_rows, beta_rows = _gate_rows(g), _gate_rows(beta)
    o, sall, tall = _gdn_fwd(qn, kn, v, g_rows, beta_rows)
    on = _gnorm_fwd(o, pm, P["a_out_norm_w"])
    h1 = _mm_nn("gdn_out", on, P["w_out"], F32, res=x)
    h2, ffn0 = _ffn_fwd("ffn0", h1, P, 0)
    nkv = _rms_fwd("kv_rms", h2, P["kv_norm_w"])
    kv = _mm_nn("kv_proj", nkv, P["w_kv"], BF16)
    nb = _rms_fwd("b_rms", h2, P["b_norm_w"])
    qp = _mm_nn("q_proj", nb, P["w_q"], BF16)
    q3 = _heads_major(qp, SWA_Q_HEADS, SWA_HD)
    k3 = _heads_major(kv[:, :SWA_KV_HEADS * SWA_HD], SWA_KV_HEADS, SWA_HD)
    v3 = _heads_major(kv[:, SWA_KV_HEADS * SWA_HD:], SWA_KV_HEADS, SWA_HD)
    onehot = _rel_onehot()
    bias = _mm_nn("rel_bias", P["rel_table_t"], onehot, F32, precision=HIGHEST)
    bias = bias.reshape(SWA_Q_HEADS, SWA_BLOCK, 2 * SWA_BLOCK)
    oa = _heads_minor(_attn_fwd(q3, k3, v3, bias, P["sinks"]))
    h3 = _mm_nn("o_proj", oa, P["w_o"], F32, res=h2)
    h4, ffn1 = _ffn_fwd("ffn1", h3, P, 1)
    loss, dh4, d_final = _final(h4, P["final_norm_w"], target)

    dh3, gf1 = _ffn_bwd("ffn1", h3, ffn1, dh4, P, 1)
    doa = _mm_nt("o_proj_dx", dh3, P["w_o"], BF16)
    g_wo = _mm_tn("o_proj_dw", oa, dh3)
    dq3, dk3, dv3, dbias, dsinks = _attn_bwd(q3, k3, v3, bias, P["sinks"], _heads_major(doa, SWA_Q_HEADS, SWA_HD))
    dqp = _heads_minor(dq3)
    dkv = jnp.concatenate([_heads_minor(dk3), _heads_minor(dv3)], axis=1)
    g_wq = _mm_tn("q_proj_dw", nb, dqp)
    dnb = _mm_nt("q_proj_dx", dqp, P["w_q"], F32)
    g_wkv = _mm_tn("kv_proj_dw", nkv, dkv)
    dnkv = _mm_nt("kv_proj_dx", dkv, P["w_kv"], F32)
    dh2, d_bnorm, d_kvnorm = _rms_bwd("b_kv_rms_bwd", h2, [(P["b_norm_w"], dnb), (P["kv_norm_w"], dnkv)], [dh3])
    g_table = _mm_nt("rel_bias_dw", onehot, dbias.reshape(SWA_Q_HEADS, -1), F32, precision=HIGHEST)
    dh1, gf0 = _ffn_bwd("ffn0", h1, ffn0, dh2, P, 0, into=(gf1["w_up"], gf1["w_down"]))
    don = _mm_nt("gdn_out_dx", dh1, P["w_out"], BF16)
    g_wout = _mm_tn("gdn_out_dw", on, dh1)
    do, dz, d_gnorm = _gnorm_bwd(o, pm, P["a_out_norm_w"], don)
    dq, dk, dv, dg, dbeta = _gdn_bwd(qn, kn, v, g_rows, beta_rows, sall, tall, do)
    dy, dpba, d_aconv, d_alog, d_dtb = _gdn_pre_bwd(pm, pba, P["a_conv_w"], P["a_log"], P["dt_bias"],
                                                    dq, dk, dv, _gate_cols(dbeta), _gate_cols(dg))
    dpm = _gdn_conv_bwd(dy, dz, P["a_conv_w"])
    g_win_main = _mm_tn("gdn_in_dw", n0, dpm)
    g_win_ba = _mm_tn("gdn_in_ba_dw", n0, dpba)
    dn0 = _mm_nt("gdn_in_dx", dpm, P["w_in_main"], F32)
    dn0 = _mm_nt("gdn_in_ba_dx", dpba, P["w_in_ba"], F32, res=dn0)
    dx, d_anorm = _rms_bwd("a_rms_bwd", x, [(P["a_norm_w"], dn0)], [dh1])

    nh = GDN_V_HEADS
    grads = dict(
        a_norm_w=d_anorm,
        a_w_in=jnp.concatenate([g_win_main, g_win_ba[:, :nh], g_win_ba[:, LANE:LANE + nh]], axis=1),
        a_conv_w=d_aconv[:4], a_a_log=d_alog[:, :nh], a_dt_bias=d_dtb[:, :nh], a_out_norm_w=d_gnorm,
        a_w_out=g_wout, kv_norm_w=d_kvnorm, w_kv=g_wkv, b_norm_w=d_bnorm, b_w_q=g_wq,
        b_sinks=dsinks[:, :, 0].reshape(1, SWA_Q_HEADS), b_w_o=g_wo, rel_bias_table=g_table,
        ffn_norm_w=jnp.concatenate([gf0["norm_w"], gf1["norm_w"]], axis=0),
        ffn_w_up=gf0["w_up"],
        ffn_conv_w=jnp.stack([gf0["conv_w"], gf1["conv_w"]], axis=0),
        ffn_conv_b=jnp.concatenate([gf0["conv_b"], gf1["conv_b"]], axis=0),
        ffn_w_down=gf0["w_down"],
        final_norm_w=d_final,
    )
    return loss, dx, grads


HBM_SPEC = pl.BlockSpec(memory_space=pltpu.HBM)
VMEM_SPEC = pl.BlockSpec(memory_space=pltpu.VMEM)


def _coords():
    return lax.axis_index("x"), lax.axis_index("y"), lax.axis_index("c")


def _remote(src, dst, send_sem, recv_sem, device):
    return pltpu.make_async_remote_copy(src_ref=src, dst_ref=dst, send_sem=send_sem, recv_sem=recv_sem,
                                        device_id=device, device_id_type=MESH)


def _other_chips(x, y):
    return [(1 - x, y), (x, 1 - y), (1 - x, 1 - y)]


def _all_gather(arrs, split):
    n = len(arrs)

    def body(*refs):
        ins, outs, stage = refs[:n], refs[n:2 * n], refs[2 * n:3 * n]
        send_sems, recv_sems, in_sems, out_sems = refs[3 * n:]
        x, y, c = _coords()
        p = 2 * x + y
        chips = _other_chips(x, y)

        def rows(a, half):
            h = arrs[a].shape[0] // 2
            return pl.ds(half * h, h)

        loads = [pltpu.make_async_copy(ins[a], stage[a], in_sems.at[a]) for a in range(n)]
        for cp in loads:
            cp.start()
        sends = []
        for a in range(n):
            for j, chip in enumerate(chips):
                if split[a]:
                    src, dst = ins[a].at[rows(a, c)], outs[a].at[p, rows(a, c)]
                else:
                    src, dst = ins[a], outs[a].at[p]
                cp = _remote(src, dst, send_sems.at[6 * a + j], recv_sems.at[6 * a + j], (*chip, c))
                cp.start()
                sends.append(cp)
        stores = [pltpu.make_async_copy(stage[a], outs[a].at[p], out_sems.at[a]) for a in range(n)]
        for a in range(n):
            loads[a].wait()
            stores[a].start()
        for a in range(n):
            for j, chip in enumerate(chips):
                q = 2 * chip[0] + chip[1]
                land = outs[a].at[q, rows(a, c)] if split[a] else outs[a].at[q]
                _remote(land, land, send_sems.at[6 * a + j], recv_sems.at[6 * a + j], (*chip, c)).wait_recv()
                if split[a]:
                    fw = _remote(land, land, send_sems.at[6 * a + 3 + j], recv_sems.at[6 * a + 3 + j], (x, y, 1 - c))
                    fw.start()
                    sends.append(fw)
        for a in range(n):
            if split[a]:
                for j, chip in enumerate(chips):
                    q = 2 * chip[0] + chip[1]
                    land = outs[a].at[q, rows(a, 1 - c)]
                    _remote(land, land, send_sems.at[6 * a + 3 + j], recv_sems.at[6 * a + 3 + j],
                            (x, y, 1 - c)).wait_recv()
        for cp in sends:
            cp.wait_send()
        for cp in stores:
            cp.wait()

    return pl.pallas_call(
        body, name="weights_all_gather", in_specs=[HBM_SPEC] * n, out_specs=[HBM_SPEC] * n,
        out_shape=[jax.ShapeDtypeStruct((N_CHIPS,) + a.shape, a.dtype) for a in arrs],
        scratch_shapes=[pltpu.VMEM(a.shape, a.dtype) for a in arrs]
        + [pltpu.SemaphoreType.DMA((6 * n,)), pltpu.SemaphoreType.DMA((6 * n,)),
           pltpu.SemaphoreType.DMA((n,)), pltpu.SemaphoreType.DMA((n,))],
        compiler_params=pltpu.CompilerParams(vmem_limit_bytes=VMEM_LIMIT),
    )(*arrs)


PAIR_SWAP_PIECES = 2


def _pair_swap(gs):
    n = len(gs)

    def body(*refs):
        ins, other = refs[:n], refs[n:2 * n]
        send_sems, recv_sems = refs[2 * n:]
        x, y, c = _coords()
        cps = []
        for a in range(n):
            h = gs[a].shape[1] // 2
            piece = h // PAIR_SWAP_PIECES
            for q in range(N_CHIPS):
                for r in range(PAIR_SWAP_PIECES):
                    k = (a * N_CHIPS + q) * PAIR_SWAP_PIECES + r
                    cp = _remote(ins[a].at[q, pl.ds((1 - c) * h + r * piece, piece)],
                                 other[a].at[q, pl.ds(r * piece, piece)], send_sems.at[k], recv_sems.at[k],
                                 (x, y, 1 - c))
                    cp.start()
                    cps.append(cp)
        for cp in cps:
            cp.wait()

    half = [jax.ShapeDtypeStruct((N_CHIPS, g.shape[1] // 2, g.shape[2]), g.dtype) for g in gs]
    nsem = n * N_CHIPS * PAIR_SWAP_PIECES
    return pl.pallas_call(
        body, name="grads_pair_swap", in_specs=[HBM_SPEC] * n, out_specs=[HBM_SPEC] * n, out_shape=half,
        scratch_shapes=[pltpu.SemaphoreType.DMA((nsem,)), pltpu.SemaphoreType.DMA((nsem,))],
    )(*gs)


def _chip_scatter(ps):
    n = len(ps)

    def body(*refs):
        ins, outs = refs[:n], refs[n:2 * n]
        send_sems, recv_sems = refs[2 * n:]
        x, y, c = _coords()
        chips = _other_chips(x, y)
        sends = []
        for a in range(n):
            for j, chip in enumerate(chips):
                q = 2 * chip[0] + chip[1]
                cp = _remote(ins[a].at[q], outs[a].at[j], send_sems.at[3 * a + j], recv_sems.at[3 * a + j], (*chip, c))
                cp.start()
                sends.append(cp)
        for cp in sends:
            cp.wait_recv()
        for cp in sends:
            cp.wait_send()

    return pl.pallas_call(
        body, name="grads_chip_scatter", in_specs=[HBM_SPEC] * n, out_specs=[HBM_SPEC] * n,
        out_shape=[jax.ShapeDtypeStruct((N_CHIPS - 1,) + a.shape[1:], a.dtype) for a in ps],
        scratch_shapes=[pltpu.SemaphoreType.DMA((3 * n,)), pltpu.SemaphoreType.DMA((3 * n,))],
    )(*ps)


def _pair_share(rs):
    n = len(rs)

    def body(*refs):
        ins, outs, stage = refs[:n], refs[n:2 * n], refs[2 * n:3 * n]
        send_sems, recv_sems, in_sems, out_sems = refs[3 * n:]
        x, y, c = _coords()

        def mine(a):
            h = rs[a].shape[0]
            return outs[a].at[pl.ds(c * h, h)]

        loads = [pltpu.make_async_copy(ins[a], stage[a], in_sems.at[a]) for a in range(n)]
        for cp in loads:
            cp.start()
        sends = [_remote(ins[a], mine(a), send_sems.at[a], recv_sems.at[a], (x, y, 1 - c)) for a in range(n)]
        for cp in sends:
            cp.start()
        stores = [pltpu.make_async_copy(stage[a], mine(a), out_sems.at[a]) for a in range(n)]
        for a in range(n):
            loads[a].wait()
            stores[a].start()
        for a in range(n):
            h = rs[a].shape[0]
            land = outs[a].at[pl.ds((1 - c) * h, h)]
            _remote(land, land, send_sems.at[a], recv_sems.at[a], (x, y, 1 - c)).wait_recv()
        for cp in sends:
            cp.wait_send()
        for cp in stores:
            cp.wait()

    return pl.pallas_call(
        body, name="grads_pair_share", in_specs=[HBM_SPEC] * n, out_specs=[HBM_SPEC] * n,
        out_shape=[jax.ShapeDtypeStruct((2 * a.shape[0], a.shape[1]), a.dtype) for a in rs],
        scratch_shapes=[pltpu.VMEM(a.shape, a.dtype) for a in rs] + [pltpu.SemaphoreType.DMA((n,))] * 4,
        compiler_params=pltpu.CompilerParams(vmem_limit_bytes=VMEM_LIMIT),
    )(*rs)


def _small_all_reduce(buf):
    R = buf.shape[0]
    ndev = 2 * N_CHIPS

    def body(in_ref, out_ref, gath, send_sems, recv_sems):
        x, y, c = _coords()
        me = 4 * x + 2 * y + c
        gath[me] = in_ref[...]
        peers = []
        for d in range(1, ndev):
            px = 1 - x if d & 4 else x
            py = 1 - y if d & 2 else y
            pc = 1 - c if d & 1 else c
            peers.append((px, py, pc))
        sends = []
        for d, peer in enumerate(peers):
            cp = _remote(in_ref, gath.at[me], send_sems.at[d], recv_sems.at[d], peer)
            cp.start()
            sends.append(cp)
        for d, peer in enumerate(peers):
            land = gath.at[4 * peer[0] + 2 * peer[1] + peer[2]]
            _remote(land, land, send_sems.at[d], recv_sems.at[d], peer).wait_recv()
        for cp in sends:
            cp.wait_send()
        acc = gath[0]
        for s in range(1, ndev):
            acc = acc + gath[s]
        out_ref[...] = acc

    return pl.pallas_call(
        body, name="small_all_reduce", in_specs=[VMEM_SPEC], out_specs=VMEM_SPEC,
        out_shape=jax.ShapeDtypeStruct(buf.shape, F32),
        scratch_shapes=[pltpu.VMEM((ndev, R, LANE), F32), pltpu.SemaphoreType.DMA((ndev - 1,)),
                        pltpu.SemaphoreType.DMA((ndev - 1,))],
    )(buf)


def _pair_add(name, own, other):
    h = own.shape[1]
    tm = _tile(h, (128, 64, 32, 16))

    def fn(i, a, b):
        return (a + b,)

    return _rowcall(name, fn, h, tm, [(own, "row", None), (other, "row", None)], [(own.shape, BF16, "row")])[0]


def _chip_add(name, own, parts):
    h = parts.shape[1]
    tm = _tile(h, (128, 64, 32, 16))

    def fn(i, o, a):
        a = a.astype(F32)
        return (((o.astype(F32) + a[0]) + a[1]) + a[2],)

    return _rowcall(name, fn, h, tm, [(own, "row", None), (parts, "row", None)], [(parts.shape[1:], F32, "row")])[0]


def _adamw(name, w, g, m, v):
    R = w.shape[0]
    tm = _tile(R, (256, 128, 64, 32, 16, 8))

    def fn(i, wv, gv, mv, vv):
        m2 = ADAM_B1 * mv + (1.0 - ADAM_B1) * gv
        v2 = ADAM_B2 * vv + (1.0 - ADAM_B2) * (gv * gv)
        m_hat = m2 / (1.0 - ADAM_B1 ** ADAM_STEP)
        v_hat = v2 / (1.0 - ADAM_B2 ** ADAM_STEP)
        delta = -ADAM_LR * (m_hat / (jnp.sqrt(v_hat) + ADAM_EPS) + ADAM_WD * wv)
        return delta, m2, v2

    ins = [(a, "row", None) for a in (w, g, m, v)]
    return _rowcall(name, fn, R, tm, ins, [(w.shape, F32, "row")] * 3)


def _pack(arrs):
    flat = jnp.concatenate([a.reshape(-1).astype(F32) for a in arrs])
    size = flat.shape[0]
    padded = -(-size // (SUBLANE * LANE)) * SUBLANE * LANE
    return jnp.pad(flat, (0, padded - size)).reshape(-1, LANE)


def _unpack(buf, shapes):
    flat = buf.reshape(-1)
    out, off = [], 0
    for s in shapes:
        size = math.prod(s)
        out.append(flat[off:off + size].reshape(s))
        off += size
    return out


BIG = ("a_w_in", "a_w_out", "w_kv", "b_w_q", "b_w_o", "ffn_w_up", "ffn_w_down")
WEIGHTS = ("a_norm_w", "a_w_in", "a_conv_w", "a_a_log", "a_dt_bias", "a_out_norm_w", "a_w_out", "kv_norm_w", "w_kv",
           "b_norm_w", "b_w_q", "b_sinks", "b_w_o", "rel_bias_table", "ffn_norm_w", "ffn_w_up", "ffn_conv_w",
           "ffn_conv_b", "ffn_w_down", "final_norm_w")
SMALL = tuple(n for n in WEIGHTS if n not in BIG)
SMALL_SHARDED = {"a_norm_w": 1, "a_conv_w": 2, "ffn_conv_w": 2}


def _quarter_2d(name, a):
    if name in ("ffn_w_up", "ffn_w_down"):
        return a.reshape(a.shape[0] * a.shape[1], a.shape[2])
    return a.reshape(a.shape[-2], a.shape[-1])


def _whole_weights(w):
    bigs = [_quarter_2d(n, w[n]).astype(BF16) for n in BIG]
    smalls = [w["a_norm_w"], w["a_conv_w"][0], w["ffn_conv_w"].reshape(6, DFF2_SHARD)]
    g = _all_gather(bigs + smalls, [True] * len(bigs) + [False] * len(smalls))
    w_in = g[0].transpose(1, 0, 2).reshape(D, GDN_IN)
    nh = GDN_V_HEADS
    zpad = jnp.zeros((D, LANE - nh), BF16)
    w_in_ba = jnp.concatenate([w_in[:, GDN_MAIN:GDN_MAIN + nh], zpad, w_in[:, GDN_MAIN + nh:], zpad], axis=1)
    lane_pad = lambda a: jnp.pad(a, ((0, 0), (0, LANE - nh)))
    return dict(
        a_norm_w=g[7].reshape(1, D), w_in_main=w_in[:, :GDN_MAIN], w_in_ba=w_in_ba,
        a_conv_w=g[8].transpose(1, 0, 2).reshape(4, GDN_CONV), a_log=lane_pad(w["a_a_log"]),
        dt_bias=lane_pad(w["a_dt_bias"]), a_out_norm_w=w["a_out_norm_w"], w_out=g[1].reshape(GDN_V, D),
        kv_norm_w=w["kv_norm_w"].reshape(1, D), w_kv=g[2].reshape(D, 2 * SWA_KV_HEADS * SWA_HD),
        b_norm_w=w["b_norm_w"], w_q=g[3].reshape(D, D), w_o=g[4].reshape(D, D),
        sinks=jnp.broadcast_to(w["b_sinks"].reshape(SWA_KV_HEADS, SWA_GROUP, 1), (SWA_KV_HEADS, SWA_GROUP, LANE)),
        rel_table_t=w["rel_bias_table"].T, ffn_norm_w=w["ffn_norm_w"],
        w_up=g[5].reshape(N_CHIPS, 2, D, DFF2_SHARD),
        ffn_conv_w=g[9].reshape(N_CHIPS, 2, 3, DFF2_SHARD).transpose(1, 2, 0, 3).reshape(2, 3, DFF2),
        ffn_conv_b=w["ffn_conv_b"],
        w_down=g[6].reshape(N_CHIPS, 2, DFF_SHARD, D).transpose(1, 0, 2, 3).reshape(2, DFF, D),
        final_norm_w=w["final_norm_w"].reshape(1, D),
    )


def _chip_major(name, g):
    if name == "a_w_in":
        return g.reshape(D, N_CHIPS, GDN_IN_SHARD).transpose(1, 0, 2)
    if name == "ffn_w_up":
        return g.reshape(N_CHIPS, 2 * D, DFF2_SHARD)
    if name == "ffn_w_down":
        return g.reshape(N_CHIPS, 2 * DFF_SHARD, D)
    return g.reshape(N_CHIPS, g.shape[0] // N_CHIPS, g.shape[1])


def kernel(x, a_norm_w, a_w_in, a_conv_w, a_a_log, a_dt_bias, a_out_norm_w, a_w_out, kv_norm_w, w_kv, b_norm_w, b_w_q, b_sinks, b_w_o, rel_bias_table, ffn_norm_w, ffn_w_up, ffn_conv_w, ffn_conv_b, ffn_w_down, final_norm_w, loss_target, m_a_norm_w, m_a_w_in, m_a_conv_w, m_a_a_log, m_a_dt_bias, m_a_out_norm_w, m_a_w_out, m_kv_norm_w, m_w_kv, m_b_norm_w, m_b_w_q, m_b_sinks, m_b_w_o, m_rel_bias_table, m_ffn_norm_w, m_ffn_w_up, m_ffn_conv_w, m_ffn_conv_b, m_ffn_w_down, m_final_norm_w, v_a_norm_w, v_a_w_in, v_a_conv_w, v_a_a_log, v_a_dt_bias, v_a_out_norm_w, v_a_w_out, v_kv_norm_w, v_w_kv, v_b_norm_w, v_b_w_q, v_b_sinks, v_b_w_o, v_rel_bias_table, v_ffn_norm_w, v_ffn_w_up, v_ffn_conv_w, v_ffn_conv_b, v_ffn_w_down, v_final_norm_w):
    w = dict(zip(WEIGHTS, (a_norm_w, a_w_in, a_conv_w, a_a_log, a_dt_bias, a_out_norm_w, a_w_out, kv_norm_w, w_kv,
                           b_norm_w, b_w_q, b_sinks, b_w_o, rel_bias_table, ffn_norm_w, ffn_w_up, ffn_conv_w,
                           ffn_conv_b, ffn_w_down, final_norm_w)))
    m = dict(zip(WEIGHTS, (m_a_norm_w, m_a_w_in, m_a_conv_w, m_a_a_log, m_a_dt_bias, m_a_out_norm_w, m_a_w_out,
                           m_kv_norm_w, m_w_kv, m_b_norm_w, m_b_w_q, m_b_sinks, m_b_w_o, m_rel_bias_table,
                           m_ffn_norm_w, m_ffn_w_up, m_ffn_conv_w, m_ffn_conv_b, m_ffn_w_down, m_final_norm_w)))
    v = dict(zip(WEIGHTS, (v_a_norm_w, v_a_w_in, v_a_conv_w, v_a_a_log, v_a_dt_bias, v_a_out_norm_w, v_a_w_out,
                           v_kv_norm_w, v_w_kv, v_b_norm_w, v_b_w_q, v_b_sinks, v_b_w_o, v_rel_bias_table,
                           v_ffn_norm_w, v_ffn_w_up, v_ffn_conv_w, v_ffn_conv_b, v_ffn_w_down, v_final_norm_w)))
    T = x.shape[1]
    chip = 2 * lax.axis_index("x") + lax.axis_index("y")

    loss_part, dx, grads = _local_step(x.reshape(T, D), loss_target.reshape(T, D), _whole_weights(w))

    core = lax.axis_index("c")
    whole = [_chip_major(n, grads[n]) for n in BIG]
    other = _pair_swap(whole)
    own = [lax.dynamic_slice_in_dim(g, core * (g.shape[1] // 2), g.shape[1] // 2, 1) for g in whole]
    pair = [_pair_add(f"pair_add_{n}", a, b) for n, a, b in zip(BIG, own, other)]
    parts = _chip_scatter(pair)
    halves = [_chip_add(f"chip_add_{n}", lax.dynamic_index_in_dim(a, chip, 0, keepdims=False), b)
              for n, a, b in zip(BIG, pair, parts)]
    quarter = _pair_share(halves)
    out_g, out_d, out_m, out_v = {}, {}, {}, {}
    for n, g2 in zip(BIG, quarter):
        res = _adamw(f"adamw_{n}", _quarter_2d(n, w[n]), g2, _quarter_2d(n, m[n]), _quarter_2d(n, v[n]))
        out_g[n] = g2.reshape(w[n].shape)
        out_d[n], out_m[n], out_v[n] = (r.reshape(w[n].shape) for r in res)

    whole = [grads[n] for n in SMALL]
    summed = _unpack(_small_all_reduce(_pack([loss_part[0:1, 0:1]] + whole)), [(1, 1)] + [a.shape for a in whole])
    loss = summed[0].reshape(())
    small_g = []
    for n, g in zip(SMALL, summed[1:]):
        if n in SMALL_SHARDED:
            axis = SMALL_SHARDED[n]
            g = g.reshape(w[n].shape[:axis] + (-1,) + w[n].shape[axis + 1:])
            size = w[n].shape[axis]
            g = lax.dynamic_slice_in_dim(g, chip * size, size, axis)
        small_g.append(g.reshape(w[n].shape))
    shapes = [w[n].shape for n in SMALL]
    res = _adamw("adamw_small", _pack([w[n] for n in SMALL]), _pack(small_g), _pack([m[n] for n in SMALL]),
                 _pack([v[n] for n in SMALL]))
    small_d, small_m, small_v = (_unpack(r, shapes) for r in res)
    for i, n in enumerate(SMALL):
        out_g[n], out_d[n], out_m[n], out_v[n] = small_g[i], small_d[i], small_m[i], small_v[i]

    return (loss, dx.reshape(x.shape), *[out_g[n] for n in WEIGHTS], *[out_d[n] for n in WEIGHTS],
            *[out_m[n] for n in WEIGHTS], *[out_v[n] for n in WEIGHTS])
```

```python
import functools
import math

import jax
import jax.numpy as jnp
from jax import lax
from jax.experimental import pallas as pl
from jax.experimental.pallas import tpu as pltpu

F32 = jnp.float32
BF16 = jnp.bfloat16
MESH = pl.DeviceIdType.MESH
HIGHEST = lax.Precision.HIGHEST

D = 1024
EPS = 1e-6
NEG_INF = -1e30
N_CHIPS = 4

GDN_QK_HEADS = 8
GDN_V_HEADS = 16
GDN_HD = 128
GDN_QK = GDN_QK_HEADS * GDN_HD
GDN_V = GDN_V_HEADS * GDN_HD
GDN_CONV = 2 * GDN_QK + GDN_V
GDN_MAIN = GDN_CONV + GDN_V
GDN_IN = GDN_MAIN + 2 * GDN_V_HEADS
GDN_IN_SHARD = GDN_IN // N_CHIPS
GDN_CHUNK = 64

SWA_Q_HEADS = 16
SWA_KV_HEADS = 4
SWA_GROUP = 4
SWA_HD = 64
SWA_BLOCK = 128
REL_BUCKETS = 32
REL_MAX_DISTANCE = 128

DFF = 2816
DFF2 = 2 * DFF
DFF2_SHARD = DFF2 // N_CHIPS
DFF_SHARD = DFF // N_CHIPS

ADAM_LR = 0.001
ADAM_B1 = 0.9
ADAM_B2 = 0.999
ADAM_EPS = 1e-08
ADAM_WD = 0.01
ADAM_STEP = 10

LANE = 128
SUBLANE = 8
VMEM_LIMIT = 56 * 1024 * 1024


def _params(sem, vmem=VMEM_LIMIT):
    return pltpu.CompilerParams(dimension_semantics=sem, vmem_limit_bytes=vmem)


def _rowcall(name, fn, T, tm, ins, outs):
    n = T // tm
    r8 = tm // SUBLANE
    last8 = T // SUBLANE - 1
    arrays, in_specs = [], []
    for arr, kind, cols in ins:
        arrays.append(arr)
        if kind == "full":
            in_specs.append(pl.BlockSpec(arr.shape, functools.partial(lambda nd, i: (0,) * nd, arr.ndim)))
        elif arr.ndim == 2:
            w, ci = cols if cols is not None else (arr.shape[1], 0)
            if kind == "row":
                in_specs.append(pl.BlockSpec((tm, w), functools.partial(lambda ci, i: (i, ci), ci)))
            elif kind == "prev":
                in_specs.append(pl.BlockSpec(
                    (SUBLANE, w), functools.partial(lambda ci, i: (jnp.maximum(i * r8 - 1, 0), ci), ci)))
            else:
                in_specs.append(pl.BlockSpec(
                    (SUBLANE, w), functools.partial(lambda ci, i: (jnp.minimum((i + 1) * r8, last8), ci), ci)))
        else:
            lead = arr.shape[:-2]
            in_specs.append(pl.BlockSpec(lead + (tm, arr.shape[-1]),
                                         functools.partial(lambda nl, i: (0,) * nl + (i, 0), len(lead))))
    out_shape, out_specs = [], []
    for shape, dtype, kind in outs:
        out_shape.append(jax.ShapeDtypeStruct(shape, dtype))
        if kind == "acc":
            out_specs.append(pl.BlockSpec(shape, functools.partial(lambda nd, i: (0,) * nd, len(shape))))
        else:
            lead = shape[:-2]
            out_specs.append(pl.BlockSpec(lead + (tm, shape[-1]),
                                          functools.partial(lambda nl, i: (0,) * nl + (i, 0), len(lead))))
    nin = len(arrays)

    def body(*refs):
        i = pl.program_id(0)
        vals = [r[...] for r in refs[:nin]]
        res = fn(i, *vals)
        for (shape, dtype, kind), o, r in zip(outs, refs[nin:], res):
            if kind == "row":
                o[...] = r.astype(dtype)
            else:
                @pl.when(i == 0)
                def _():
                    o[...] = r.astype(dtype)

                @pl.when(i > 0)
                def _():
                    o[...] += r.astype(dtype)

    res = pl.pallas_call(
        body, name=name, grid=(n,), in_specs=in_specs, out_specs=out_specs, out_shape=out_shape,
        compiler_params=_params(("arbitrary",)),
    )(*arrays)
    return res


def _mm(name, a, b, out_shape, out_dtype, grid, a_spec, b_spec, o_spec, dims, acc_shape, res=None, precision=None,
        into=None):
    nk = grid[2]
    n_in = 2 + (res is not None) + (into is not None)

    def body(*refs):
        a_ref, b_ref, o_ref = refs[0], refs[1], refs[n_in]
        r_ref = refs[2] if res is not None else None
        av, bv = a_ref[...], b_ref[...]
        if precision is None:
            av, bv = av.astype(BF16), bv.astype(BF16)
        p = lax.dot_general(av, bv, (dims, ((), ())), preferred_element_type=F32, precision=precision)

        def finish(x):
            if res is not None:
                x = x + r_ref[...].astype(F32)
            o_ref[...] = x.astype(out_dtype).reshape(o_ref.shape)

        if nk == 1:
            finish(p)
        else:
            acc = refs[-1]
            k = pl.program_id(2)

            @pl.when(k == 0)
            def _():
                acc[...] = p

            @pl.when(k > 0)
            def _():
                acc[...] += p

            @pl.when(k == nk - 1)
            def _():
                finish(acc[...])

    ops = [a, b] + ([res] if res is not None else []) + ([into] if into is not None else [])
    specs = [a_spec, b_spec] + ([o_spec] if res is not None else [])
    specs += [pl.BlockSpec(memory_space=pl.ANY)] if into is not None else []
    return pl.pallas_call(
        body, name=name, grid=grid, in_specs=specs, out_specs=o_spec,
        out_shape=jax.ShapeDtypeStruct(out_shape, out_dtype),
        input_output_aliases={n_in - 1: 0} if into is not None else {},
        scratch_shapes=[pltpu.VMEM(acc_shape, F32)] if nk > 1 else [],
        compiler_params=_params(("parallel", "parallel", "arbitrary")),
    )(*ops)


NN = ((1,), (0,))
NT = ((1,), (1,))
TN = ((0,), (0,))


BIG_TILES = (1024, 512, 256, 128)


def _tile(n, pref):
    for t in pref:
        if n % t == 0:
            return t
    return n


def _mm_nn(name, a, w, out_dtype, res=None, precision=None):
    M, K = a.shape
    N = w.shape[1]
    tm = _tile(M, BIG_TILES if K <= 2048 else BIG_TILES[1:])
    tn = _tile(N, BIG_TILES)
    return _mm(name, a, w, (M, N), out_dtype, (M // tm, N // tn, 1),
               pl.BlockSpec((tm, K), lambda i, j, k: (i, 0)), pl.BlockSpec((K, tn), lambda i, j, k: (0, j)),
               pl.BlockSpec((tm, tn), lambda i, j, k: (i, j)), NN, (tm, tn), res=res, precision=precision)


def _mm_nt(name, g, w, out_dtype, res=None, precision=None):
    M, N = g.shape
    K = w.shape[0]
    tm, tk = _tile(M, BIG_TILES), _tile(K, (1024, 1408, 512, 256, 128))
    tn = _tile(N, (1536,) + BIG_TILES)
    return _mm(name, g, w, (M, K), out_dtype, (M // tm, K // tk, N // tn),
               pl.BlockSpec((tm, tn), lambda i, j, k: (i, k)), pl.BlockSpec((tk, tn), lambda i, j, k: (j, k)),
               pl.BlockSpec((tm, tk), lambda i, j, k: (i, j)), NT, (tm, tk), res=res, precision=precision)


def _mm_tn(name, a, g, out_dtype=F32, precision=None):
    T, K = a.shape
    N = g.shape[1]
    tk, tn = _tile(K, (1024, 1408, 512, 256, 128)), _tile(N, BIG_TILES)
    tt = _tile(T, BIG_TILES)
    return _mm(name, a, g, (K, N), out_dtype, (K // tk, N // tn, T // tt),
               pl.BlockSpec((tt, tk), lambda i, j, k: (k, i)), pl.BlockSpec((tt, tn), lambda i, j, k: (k, j)),
               pl.BlockSpec((tk, tn), lambda i, j, k: (i, j)), TN, (tk, tn), precision=precision)


def _mm_up(name, n, wup, layer):
    T = n.shape[0]
    tm = _tile(T, BIG_TILES)
    return _mm(name, n, wup, (T, DFF2), BF16, (T // tm, N_CHIPS, 1),
               pl.BlockSpec((tm, D), lambda i, j, k: (i, 0)),
               pl.BlockSpec((None, None, D, DFF2_SHARD), lambda i, j, k: (j, layer, 0, 0)),
               pl.BlockSpec((tm, DFF2_SHARD), lambda i, j, k: (i, j)), NN, (tm, DFF2_SHARD))


def _mm_up_nt(name, du, wup, layer):
    T = du.shape[0]
    tm, tk = _tile(T, BIG_TILES), D
    return _mm(name, du, wup, (T, D), F32, (T // tm, D // tk, N_CHIPS),
               pl.BlockSpec((tm, DFF2_SHARD), lambda i, j, k: (i, k)),
               pl.BlockSpec((None, None, tk, DFF2_SHARD), lambda i, j, k: (k, layer, j, 0)),
               pl.BlockSpec((tm, tk), lambda i, j, k: (i, j)), NT, (tm, tk))


def _mm_up_tn(name, n, du, layer, into):
    T = n.shape[0]
    tk, tt = D, _tile(T, BIG_TILES)
    return _mm(name, n, du, (N_CHIPS, 2, D, DFF2_SHARD), F32, (D // tk, N_CHIPS, T // tt),
               pl.BlockSpec((tt, tk), lambda i, j, k: (k, i)), pl.BlockSpec((tt, DFF2_SHARD), lambda i, j, k: (k, j)),
               pl.BlockSpec((None, None, tk, DFF2_SHARD), lambda i, j, k: (j, layer, i, 0)), TN, (tk, DFF2_SHARD),
               into=into)


def _mm_down_tn(name, act, dout, layer, into):
    T = act.shape[0]
    tk, tn, tt = 2 * DFF_SHARD, _tile(D, BIG_TILES), _tile(T, BIG_TILES)
    return _mm(name, act, dout, (2, 2, 2, DFF_SHARD, D), F32, (DFF // tk, D // tn, T // tt),
               pl.BlockSpec((tt, tk), lambda i, j, k: (k, i)), pl.BlockSpec((tt, tn), lambda i, j, k: (k, j)),
               pl.BlockSpec((None, 2, None, DFF_SHARD, tn), lambda i, j, k: (i, 0, layer, 0, j)), TN, (tk, tn),
               into=into)


def _sigmoid(x):
    return 0.5 * jnp.tanh(0.5 * x) + 0.5


def _silu(x):
    return x * _sigmoid(x)


def _softplus(x):
    return jnp.maximum(x, 0.0) + jnp.log(1.0 + jnp.exp(-jnp.abs(x)))


def _rms_core(h, w):
    return h * lax.rsqrt(jnp.mean(h * h, axis=-1, keepdims=True) + EPS) * w


def _shift_down(x, halo, s, i):
    if s == 0:
        return x
    tm = x.shape[0]
    rolled = pltpu.roll(x, s, 0)
    patch = pltpu.roll(jnp.where(i == 0, 0.0, halo), s, 0)
    row = lax.broadcasted_iota(jnp.int32, patch.shape, 0)
    top = jnp.where(row < s, patch, rolled[:SUBLANE])
    return jnp.concatenate([top, rolled[SUBLANE:]], axis=0) if tm > SUBLANE else top


def _shift_up(x, halo, s, i, n):
    if s == 0:
        return x
    tm = x.shape[0]
    rolled = pltpu.roll(x, tm - s, 0)
    patch = pltpu.roll(jnp.where(i == n - 1, 0.0, halo), SUBLANE - s, 0)
    row = lax.broadcasted_iota(jnp.int32, patch.shape, 0)
    bottom = jnp.where(row >= SUBLANE - s, patch, rolled[tm - SUBLANE:])
    return jnp.concatenate([rolled[:tm - SUBLANE], bottom], axis=0) if tm > SUBLANE else bottom


def _taps(x, halo, K, i):
    return [_shift_down(x, halo, K - 1 - j, i) for j in range(K)]


def _conv_fwd(taps, w):
    y = w[0:1, :] * taps[0]
    for j in range(1, len(taps)):
        y = y + w[j:j + 1, :] * taps[j]
    return y


def _conv_dx(dy, halo_next, w, i, n):
    K = w.shape[0]
    dx = w[K - 1:K, :] * dy
    for j in range(K - 1):
        dx = dx + w[j:j + 1, :] * _shift_up(dy, halo_next, K - 1 - j, i, n)
    return dx


def _conv_dw(dy, taps):
    rows = [jnp.sum(dy * tap, axis=0, keepdims=True) for tap in taps]
    return jnp.concatenate(rows + [jnp.zeros((SUBLANE - len(taps), dy.shape[1]), F32)], axis=0)


def _rms_fwd(name, h, w, tm=512):
    T = h.shape[0]
    tm = min(tm, T)

    def fn(i, hv, wv):
        return (_rms_core(hv, wv),)

    return _rowcall(name, fn, T, tm, [(h, "row", None), (w, "full", None)], [((T, D), BF16, "row")])[0]


def _rms_bwd(name, h, pairs, adds, tm=256):
    T = h.shape[0]
    tm = min(tm, T)
    npair, nadd = len(pairs), len(adds)

    def fn(i, hv, *rest):
        ws, dns, ads = rest[:npair], rest[npair:2 * npair], rest[2 * npair:]
        dh = None
        dws = []
        for wv, dn in zip(ws, dns):
            _, vjp = jax.vjp(_rms_core, hv, wv)
            dhi, dwi = vjp(dn.astype(F32))
            dh = dhi if dh is None else dh + dhi
            dws.append(dwi)
        for a in ads:
            dh = dh + a.astype(F32)
        return (dh, *dws)

    ins = [(h, "row", None)] + [(w, "full", None) for w, _ in pairs] + [(dn, "row", None) for _, dn in pairs]
    ins += [(a, "row", None) for a in adds]
    outs = [((T, D), F32, "row")] + [((1, D), F32, "acc")] * npair
    return _rowcall(name, fn, T, tm, ins, outs)


def _l2(x):
    return x * lax.rsqrt(jnp.sum(x * x, axis=-1, keepdims=True) + EPS)


def _gdn_post_core(yq, yk, yv, pb, pa, a_log, dtb):
    qn = tuple(_l2(_silu(a)) * (GDN_HD ** -0.5) for a in yq)
    kn = tuple(_l2(_silu(a)) for a in yk)
    v = _silu(yv)
    beta = _sigmoid(pb)
    g = -jnp.exp(a_log) * _softplus(pa + dtb)
    return qn, kn, v, beta, g


def _heads(x, n):
    return tuple(x[:, GDN_HD * h:GDN_HD * (h + 1)] for h in range(n))


def _gdn_pre_fwd(pm, pba, conv_w, a_log, dtb, tm=128):
    T = pm.shape[0]
    tm = min(tm, T)

    def fn(i, x, halo, pbav, cw, al, db):
        y = _conv_fwd(_taps(x.astype(F32), halo.astype(F32), 4, i), cw)
        qn, kn, v, beta, g = _gdn_post_core(_heads(y[:, :GDN_QK], 8), _heads(y[:, GDN_QK:2 * GDN_QK], 8),
                                            y[:, 2 * GDN_QK:], pbav[:, :LANE], pbav[:, LANE:], al, db)
        return jnp.stack(qn), jnp.stack(kn), jnp.stack(_heads(v, GDN_V_HEADS)), beta, g

    ins = [(pm, "row", (GDN_CONV, 0)), (pm, "prev", (GDN_CONV, 0)), (pba, "row", None),
           (conv_w, "full", None), (a_log, "full", None), (dtb, "full", None)]
    outs = [((GDN_QK_HEADS, T, GDN_HD), BF16, "row"), ((GDN_QK_HEADS, T, GDN_HD), BF16, "row"),
            ((GDN_V_HEADS, T, GDN_HD), BF16, "row"), ((T, LANE), F32, "row"), ((T, LANE), F32, "row")]
    return _rowcall("gdn_pre_fwd", fn, T, tm, ins, outs)


def _gdn_pre_bwd(pm, pba, conv_w, a_log, dtb, dqn, dkn, dv, dbeta, dg, tm=128):
    T = pm.shape[0]
    tm = min(tm, T)

    def fn(i, x, halo, pbav, cw, al, db, dqv, dkv, dvv, dbv, dgv):
        taps = _taps(x.astype(F32), halo.astype(F32), 4, i)
        y = _conv_fwd(taps, cw)
        prim = (_heads(y[:, :GDN_QK], 8), _heads(y[:, GDN_QK:2 * GDN_QK], 8), y[:, 2 * GDN_QK:],
                pbav[:, :LANE], pbav[:, LANE:], al, db)
        _, vjp = jax.vjp(_gdn_post_core, *prim)
        cot = (tuple(dqv[h].astype(F32) for h in range(8)), tuple(dkv[h].astype(F32) for h in range(8)),
               jnp.concatenate([dvv[h].astype(F32) for h in range(GDN_V_HEADS)], axis=1), dbv, dgv)
        dyq, dyk, dyv, dpb, dpa, dal, ddb = vjp(cot)
        dy = jnp.concatenate(list(dyq) + list(dyk) + [dyv], axis=1)
        dcw = _conv_dw(dy, taps)
        return dy, jnp.concatenate([dpb, dpa], axis=1), dcw, dal, ddb

    ins = [(pm, "row", (GDN_CONV, 0)), (pm, "prev", (GDN_CONV, 0)), (pba, "row", None),
           (conv_w, "full", None), (a_log, "full", None), (dtb, "full", None),
           (dqn, "row", None), (dkn, "row", None), (dv, "row", None), (dbeta, "row", None), (dg, "row", None)]
    outs = [((T, GDN_CONV), BF16, "row"), ((T, 2 * LANE), F32, "row"), ((SUBLANE, GDN_CONV), F32, "acc"),
            ((1, LANE), F32, "acc"), ((1, LANE), F32, "acc")]
    return _rowcall("gdn_pre_bwd", fn, T, tm, ins, outs)


def _gdn_conv_bwd(dy, dz, conv_w, tm=256):
    T = dy.shape[0]
    tm = min(tm, T)
    n = T // tm

    def fn(i, dyv, halo, dzv, cw):
        dx = _conv_dx(dyv.astype(F32), halo.astype(F32), cw, i, n)
        return (jnp.concatenate([dx.astype(BF16), dzv.astype(BF16)], axis=1),)

    ins = [(dy, "row", None), (dy, "next", None), (dz, "row", None), (conv_w, "full", None)]
    return _rowcall("gdn_conv_bwd", fn, T, tm, ins, [((T, GDN_MAIN), BF16, "row")])[0]


def _bdot(a, b, dims=NN):
    return lax.dot_general(a.astype(BF16), b.astype(BF16), (dims, ((), ())), preferred_element_type=F32)


BNN = ((2,), (1,))
BNT = ((2,), (2,))
BTN = ((1,), (1,))


def _bmm(a, b, dims=BNN):
    return lax.dot_general(a.astype(BF16), b.astype(BF16), (dims, ((0,), (0,))), preferred_element_type=F32)


def _bmm3(a, b):
    ah, bh = a.astype(BF16), b.astype(BF16)
    al, bl = (a - ah.astype(F32)).astype(BF16), (b - bh.astype(F32)).astype(BF16)
    dn = (BNN, ((0,), (0,)))
    return (lax.dot_general(ah, bh, dn, preferred_element_type=F32)
            + lax.dot_general(al, bh, dn, preferred_element_type=F32)
            + lax.dot_general(ah, bl, dn, preferred_element_type=F32))


def _tri_inv(m):
    C = m.shape[-1]
    r = lax.broadcasted_iota(jnp.int32, (C, C), 0)
    c = lax.broadcasted_iota(jnp.int32, (C, C), 1)
    t = jnp.where(r == c, 1.0, 0.0) - m
    pw = _bmm3(m, m)
    t = t + _bmm3(t, pw)
    for _ in range(int(math.log2(C)) - 2):
        pw = _bmm(pw, pw)
        t = t + _bmm(t, pw)
    return t


def _tri_inv_vjp(t, dt):
    tt = jnp.swapaxes(t, 1, 2)
    return -_bmm(_bmm(tt, dt), tt)


def _twice(a):
    return jnp.broadcast_to(a[:, None], (a.shape[0], 2) + a.shape[1:]).reshape((2 * a.shape[0],) + a.shape[1:])


def _gdn_gates(grow, brow):
    C = grow.shape[2]
    r = lax.broadcasted_iota(jnp.int32, (C, C), 0)
    c = lax.broadcasted_iota(jnp.int32, (C, C), 1)
    tril, eye = r >= c, r == c
    gcol = jnp.sum(jnp.where(eye, grow, 0.0), axis=2, keepdims=True)
    bcol = jnp.sum(jnp.where(eye, brow, 0.0), axis=2, keepdims=True)
    gc_col = jnp.sum(jnp.where(tril, grow, 0.0), axis=2, keepdims=True)
    gc_row = jnp.sum(jnp.where(r <= c, gcol, 0.0), axis=1, keepdims=True)
    gc_last = jnp.sum(grow, axis=2, keepdims=True)
    decay = jnp.where(tril, jnp.exp(jnp.where(tril, gc_col - gc_row, 0.0)), 0.0)
    return bcol, gc_col, gc_last, decay


def _gdn_m(k, grow, brow):
    C = k.shape[1]
    strict = lax.broadcasted_iota(jnp.int32, (C, C), 0) > lax.broadcasted_iota(jnp.int32, (C, C), 1)
    bcol, _, _, decay = _gdn_gates(grow, brow)
    return jnp.where(strict, bcol * _twice(_bmm(k, k, BNT)) * decay, 0.0)


def _gdn_rest(q, k, v, grow, brow, t_mat, S):
    bcol, gc_col, gc_last, decay = _gdn_gates(grow, brow)
    qk = _twice(_bmm(q, k, BNT))
    k2, q2 = _twice(k), _twice(q)
    egc = jnp.exp(gc_col)
    u = _bmm(t_mat, v * bcol)
    w = _bmm(t_mat, k2 * (bcol * egc))
    v_new = u - _bmm(w, S)
    o = _bmm(q2 * egc, S) + _bmm(qk * decay, v_new)
    s_new = S * jnp.exp(gc_last) + _bmm(k2 * jnp.exp(gc_last - gc_col), v_new, BTN)
    return o, s_new


def _gdn_tb(T):
    return min(256, T)


def _gate_rows(g):
    T = g.shape[0]
    g = g[:, :GDN_V_HEADS].reshape(T // GDN_CHUNK, GDN_CHUNK, GDN_V_HEADS)
    return g.transpose(0, 2, 1)[:, :, None, :]


def _gate_cols(g):
    nc = g.shape[0]
    g = g[:, :, 0, :].transpose(0, 2, 1).reshape(nc * GDN_CHUNK, GDN_V_HEADS)
    return jnp.pad(g, ((0, 0), (0, LANE - GDN_V_HEADS)))


def _gdn_fwd(qn, kn, v, g, beta, gather=None):
    T = qn.shape[1]
    tb = _gdn_tb(T)
    nc = tb // GDN_CHUNK
    nsteps = T // tb
    quarters, buffers = gather if gather is not None else ((), ())
    ng = len(quarters)
    shapes = [a.shape for a in quarters]
    splits = [True] * ng

    def body(*refs):
        q_ref, k_ref, v_ref, g_ref, b_ref = refs[:5]
        src = refs[5:5 + ng]
        o_ref, sall_ref, tall_ref = refs[5 + 2 * ng:8 + 2 * ng]
        dst = refs[8 + 2 * ng:8 + 3 * ng]
        s_scr = refs[8 + 3 * ng]
        step = pl.program_id(0)

        @pl.when(step == 0)
        def _():
            s_scr[...] = jnp.zeros(s_scr.shape, F32)
            if ng:
                for cp in _gather_copies(shapes, splits, src, dst, *refs[9 + 3 * ng:])[0]:
                    cp.start()

        def chunk(ci, carry):
            rows = pl.ds(pl.multiple_of(ci * GDN_CHUNK, GDN_CHUNK), GDN_CHUNK)
            s = s_scr[...]
            sall_ref[ci] = s
            q, k = q_ref[:, rows, :].astype(F32), k_ref[:, rows, :].astype(F32)
            t_mat = _tri_inv(_gdn_m(k, g_ref[ci], b_ref[ci])).astype(BF16)
            tall_ref[ci] = t_mat
            o, s_new = _gdn_rest(q, k, v_ref[:, rows, :].astype(F32), g_ref[ci], b_ref[ci], t_mat.astype(F32), s)
            o_ref[:, rows, :] = o.astype(o_ref.dtype)
            s_scr[...] = s_new
            return carry

        lax.fori_loop(0, nc, chunk, 0)

        if ng:
            @pl.when(step == nsteps - 1)
            def _():
                _gather_arrival(shapes, splits, src, dst, *refs[9 + 3 * ng:])

    qk_spec = pl.BlockSpec((GDN_QK_HEADS, tb, GDN_HD), lambda i: (0, i, 0))
    v_spec = pl.BlockSpec((GDN_V_HEADS, tb, GDN_HD), lambda i: (0, i, 0))
    g_spec = pl.BlockSpec((nc, GDN_V_HEADS, 1, GDN_CHUNK), lambda i: (i, 0, 0, 0))
    anywhere = pl.BlockSpec(memory_space=pl.ANY)
    return pl.pallas_call(
        body, name="gdn_fwd", grid=(nsteps,),
        in_specs=[qk_spec, qk_spec, v_spec, g_spec, g_spec] + [anywhere] * (2 * ng),
        out_specs=[v_spec, pl.BlockSpec((nc, GDN_V_HEADS, GDN_HD, GDN_HD), lambda i: (i, 0, 0, 0)),
                   pl.BlockSpec((nc, GDN_V_HEADS, GDN_CHUNK, GDN_CHUNK), lambda i: (i, 0, 0, 0))] + [anywhere] * ng,
        out_shape=[jax.ShapeDtypeStruct((GDN_V_HEADS, T, GDN_HD), BF16),
                   jax.ShapeDtypeStruct((T // GDN_CHUNK, GDN_V_HEADS, GDN_HD, GDN_HD), F32),
                   jax.ShapeDtypeStruct((T // GDN_CHUNK, GDN_V_HEADS, GDN_CHUNK, GDN_CHUNK), BF16)]
        + [jax.ShapeDtypeStruct(b.shape, b.dtype) for b in buffers],
        input_output_aliases={5 + ng + a: 3 + a for a in range(ng)},
        scratch_shapes=[pltpu.VMEM((GDN_V_HEADS, GDN_HD, GDN_HD), F32)]
        + ([pltpu.SemaphoreType.DMA((6 * ng,)), pltpu.SemaphoreType.DMA((6 * ng,))] if ng else []),
        compiler_params=_params(("arbitrary",)),
    )(qn, kn, v, g, beta, *quarters, *buffers)


def _gdn_bwd(qn, kn, v, g, beta, sall, tall, do):
    T = qn.shape[1]
    tb = _gdn_tb(T)
    nc = tb // GDN_CHUNK
    nb = T // tb

    def body(q_ref, k_ref, v_ref, g_ref, b_ref, sall_ref, tall_ref, do_ref,
             dq_ref, dk_ref, dv_ref, dg_ref, db_ref, ds_scr):
        @pl.when(pl.program_id(0) == 0)
        def _():
            ds_scr[...] = jnp.zeros(ds_scr.shape, F32)

        def chunk(cr, carry):
            ci = nc - 1 - cr
            rows = pl.ds(pl.multiple_of(ci * GDN_CHUNK, GDN_CHUNK), GDN_CHUNK)
            k, t_mat = k_ref[:, rows, :].astype(F32), tall_ref[ci].astype(F32)
            _, vjp = jax.vjp(_gdn_rest, q_ref[:, rows, :].astype(F32), k, v_ref[:, rows, :].astype(F32),
                             g_ref[ci], b_ref[ci], t_mat, sall_ref[ci])
            dq, dk, dv, dg, db, dt, ds = vjp((do_ref[:, rows, :].astype(F32), ds_scr[...]))
            _, vjp_m = jax.vjp(_gdn_m, k, g_ref[ci], b_ref[ci])
            dk_m, dg_m, db_m = vjp_m(_tri_inv_vjp(t_mat, dt))
            ds_scr[...] = ds
            dq_ref[:, rows, :] = dq
            dk_ref[:, rows, :] = dk + dk_m
            dv_ref[:, rows, :] = dv
            dg_ref[ci] = dg + dg_m
            db_ref[ci] = db + db_m
            return carry

        lax.fori_loop(0, nc, chunk, 0)

    qk_spec = pl.BlockSpec((GDN_QK_HEADS, tb, GDN_HD), lambda i: (0, nb - 1 - i, 0))
    v_spec = pl.BlockSpec((GDN_V_HEADS, tb, GDN_HD), lambda i: (0, nb - 1 - i, 0))
    g_spec = pl.BlockSpec((nc, GDN_V_HEADS, 1, GDN_CHUNK), lambda i: (nb - 1 - i, 0, 0, 0))
    s_spec = pl.BlockSpec((nc, GDN_V_HEADS, GDN_HD, GDN_HD), lambda i: (nb - 1 - i, 0, 0, 0))
    t_spec = pl.BlockSpec((nc, GDN_V_HEADS, GDN_CHUNK, GDN_CHUNK), lambda i: (nb - 1 - i, 0, 0, 0))
    return pl.pallas_call(
        body, name="gdn_bwd", grid=(nb,),
        in_specs=[qk_spec, qk_spec, v_spec, g_spec, g_spec, s_spec, t_spec, v_spec],
        out_specs=[qk_spec, qk_spec, v_spec, g_spec, g_spec],
        out_shape=[jax.ShapeDtypeStruct((GDN_QK_HEADS, T, GDN_HD), F32),
                   jax.ShapeDtypeStruct((GDN_QK_HEADS, T, GDN_HD), F32),
                   jax.ShapeDtypeStruct((GDN_V_HEADS, T, GDN_HD), F32),
                   jax.ShapeDtypeStruct(g.shape, F32), jax.ShapeDtypeStruct(g.shape, F32)],
        scratch_shapes=[pltpu.VMEM((GDN_V_HEADS, GDN_HD, GDN_HD), F32)],
        compiler_params=_params(("arbitrary",)),
    )(qn, kn, v, g, beta, sall, tall, do)


def _gnorm_core(o, z, w):
    return tuple(_rms_core(oh, w) * _silu(zh) for oh, zh in zip(o, z))


def _gnorm_fwd(o, pm, w, tm=256):
    T = pm.shape[0]
    tm = min(tm, T)

    def fn(i, ov, zv, wv):
        zf = zv.astype(F32)
        out = _gnorm_core(tuple(ov[h].astype(F32) for h in range(GDN_V_HEADS)), _heads(zf, GDN_V_HEADS), wv)
        return (jnp.concatenate(out, axis=1),)

    ins = [(o, "row", None), (pm, "row", (GDN_V, 2)), (w, "full", None)]
    return _rowcall("gnorm_fwd", fn, T, tm, ins, [((T, GDN_V), BF16, "row")])[0]


def _gnorm_bwd(o, pm, w, don, tm=128):
    T = pm.shape[0]
    tm = min(tm, T)

    def fn(i, ov, zv, wv, dv):
        zf, df = zv.astype(F32), dv.astype(F32)
        _, vjp = jax.vjp(_gnorm_core, tuple(ov[h].astype(F32) for h in range(GDN_V_HEADS)),
                         _heads(zf, GDN_V_HEADS), wv)
        do, dz, dw = vjp(_heads(df, GDN_V_HEADS))
        return jnp.stack(do), jnp.concatenate(dz, axis=1), dw

    ins = [(o, "row", None), (pm, "row", (GDN_V, 2)), (w, "full", None), (don, "row", None)]
    outs = [((GDN_V_HEADS, T, GDN_HD), BF16, "row"), ((T, GDN_V), BF16, "row"), ((1, GDN_HD), F32, "acc")]
    return _rowcall("gnorm_bwd", fn, T, tm, ins, outs)


def _ffn_act_fwd(name, up, conv_w, conv_b, tm=128):
    T = up.shape[0]
    tm = min(tm, T)

    def fn(i, x, halo, cw, cb):
        u = _conv_fwd(_taps(x.astype(F32), halo.astype(F32), 3, i), cw) + cb
        return (_silu(u[:, :DFF]) * u[:, DFF:],)

    ins = [(up, "row", None), (up, "prev", None), (conv_w, "full", None), (conv_b, "full", None)]
    return _rowcall(name, fn, T, tm, ins, [((T, DFF), BF16, "row")])[0]


def _ffn_act_bwd(name, up, conv_w, conv_b, dact, tm=128):
    T = up.shape[0]
    tm = min(tm, T)

    def fn(i, x, halo, cw, cb, da):
        taps = _taps(x.astype(F32), halo.astype(F32), 3, i)
        da = da.astype(F32)
        u = _conv_fwd(taps, cw) + cb
        gate, val = u[:, :DFF], u[:, DFF:]
        sg = _sigmoid(gate)
        dgate = da * val * sg * (1.0 + gate * (1.0 - sg))
        dval = da * gate * sg
        du = jnp.concatenate([dgate, dval], axis=1)
        return du, _conv_dw(du, taps), jnp.sum(du, axis=0, keepdims=True)

    ins = [(up, "row", None), (up, "prev", None), (conv_w, "full", None), (conv_b, "full", None),
           (dact, "row", None)]
    outs = [((T, DFF2), BF16, "row"), ((SUBLANE, DFF2), F32, "acc"), ((1, DFF2), F32, "acc")]
    return _rowcall(name, fn, T, tm, ins, outs)


def _ffn_conv_bwd(name, du, conv_w, tm=256):
    T = du.shape[0]
    tm = min(tm, T)
    n = T // tm

    def fn(i, dv, halo, cw):
        return (_conv_dx(dv.astype(F32), halo.astype(F32), cw, i, n),)

    ins = [(du, "row", None), (du, "next", None), (conv_w, "full", None)]
    return _rowcall(name, fn, T, tm, ins, [((T, DFF2), BF16, "row")])[0]


GROUP_ROWS = SWA_GROUP * SWA_BLOCK


def _attn_core(q, kp, kc, vp, vc, bias, sink, mask):
    kcat = jnp.concatenate([kp, kc], axis=0)
    vcat = jnp.concatenate([vp, vc], axis=0)
    s = _bdot(q * (SWA_HD ** -0.5), kcat, NT) + bias
    s = jnp.where(mask, s, NEG_INF)
    m = lax.stop_gradient(jnp.maximum(jnp.max(s, axis=-1, keepdims=True), sink))
    p = jnp.exp(s - m)
    denom = jnp.sum(p, axis=-1, keepdims=True) + jnp.exp(sink - m)
    return _bdot(p / denom, vcat)


def _attn_mask(i):
    qi = lax.broadcasted_iota(jnp.int32, (GROUP_ROWS, 2 * SWA_BLOCK), 0) & (SWA_BLOCK - 1)
    ki = lax.broadcasted_iota(jnp.int32, (GROUP_ROWS, 2 * SWA_BLOCK), 1)
    dist = qi + SWA_BLOCK - ki
    return (dist >= 0) & (dist < SWA_BLOCK) & ((ki >= SWA_BLOCK) | (i > 0))


def _attn_operands(j, q_ref, kc_ref, kp_ref, vc_ref, vp_ref, b_ref, s_ref):
    heads = slice(SWA_GROUP * j, SWA_GROUP * (j + 1))
    sink = jnp.concatenate([jnp.broadcast_to(s_ref[j, g:g + 1, 0:1], (SWA_BLOCK, 1)) for g in range(SWA_GROUP)],
                           axis=0)
    return (q_ref[heads].astype(F32).reshape(GROUP_ROWS, SWA_HD), kp_ref[j].astype(F32), kc_ref[j].astype(F32),
            vp_ref[j].astype(F32), vc_ref[j].astype(F32), b_ref[heads].reshape(GROUP_ROWS, 2 * SWA_BLOCK), sink)


def _attn_fwd(q, k, v, bias, sinks):
    T = q.shape[1]
    nb = T // SWA_BLOCK

    def body(q_ref, kc_ref, kp_ref, vc_ref, vp_ref, b_ref, s_ref, o_ref):
        mask = _attn_mask(pl.program_id(0))
        for j in range(SWA_KV_HEADS):
            out = _attn_core(*_attn_operands(j, q_ref, kc_ref, kp_ref, vc_ref, vp_ref, b_ref, s_ref), mask)
            o_ref[SWA_GROUP * j:SWA_GROUP * (j + 1)] = out.reshape(SWA_GROUP, SWA_BLOCK, SWA_HD).astype(o_ref.dtype)

    q_spec = pl.BlockSpec((SWA_Q_HEADS, SWA_BLOCK, SWA_HD), lambda i: (0, i, 0))
    cur = pl.BlockSpec((SWA_KV_HEADS, SWA_BLOCK, SWA_HD), lambda i: (0, i, 0))
    prev = pl.BlockSpec((SWA_KV_HEADS, SWA_BLOCK, SWA_HD), lambda i: (0, jnp.maximum(i - 1, 0), 0))
    return pl.pallas_call(
        body, name="attn_fwd", grid=(nb,),
        in_specs=[q_spec, cur, prev, cur, prev, pl.BlockSpec(bias.shape, lambda i: (0, 0, 0)),
                  pl.BlockSpec(sinks.shape, lambda i: (0, 0, 0))],
        out_specs=q_spec, out_shape=jax.ShapeDtypeStruct(q.shape, BF16),
        compiler_params=_params(("arbitrary",)),
    )(q, k, k, v, v, bias, sinks)


def _attn_bwd(q, k, v, bias, sinks, do):
    T = q.shape[1]
    nb = T // SWA_BLOCK

    def body(q_ref, kc_ref, kp_ref, vc_ref, vp_ref, b_ref, s_ref, do_ref,
             dq_ref, dk_ref, dv_ref, db_ref, dsk_ref, kcar, vcar):
        i = pl.program_id(0)

        @pl.when(i < nb)
        def _():
            mask = _attn_mask(i)
            for j in range(SWA_KV_HEADS):
                heads = slice(SWA_GROUP * j, SWA_GROUP * (j + 1))
                prim = _attn_operands(j, q_ref, kc_ref, kp_ref, vc_ref, vp_ref, b_ref, s_ref)
                _, vjp = jax.vjp(functools.partial(_attn_core, mask=mask), *prim)
                dq, dkp, dkc, dvp, dvc, db, dsc = vjp(do_ref[heads].astype(F32).reshape(GROUP_ROWS, SWA_HD))
                dq_ref[heads] = dq.reshape(SWA_GROUP, SWA_BLOCK, SWA_HD).astype(dq_ref.dtype)
                db = db.reshape(SWA_GROUP, SWA_BLOCK, 2 * SWA_BLOCK)
                dsk = jnp.concatenate(
                    [jnp.broadcast_to(jnp.sum(dsc[g * SWA_BLOCK:(g + 1) * SWA_BLOCK], axis=0, keepdims=True),
                                      (1, LANE)) for g in range(SWA_GROUP)], axis=0)

                @pl.when(i == 0)
                def _():
                    db_ref[heads] = db
                    dsk_ref[j] = dsk

                @pl.when(i > 0)
                def _():
                    db_ref[heads] += db
                    dsk_ref[j] += dsk
                    dk_ref[j] = (kcar[j] + dkp).astype(dk_ref.dtype)
                    dv_ref[j] = (vcar[j] + dvp).astype(dv_ref.dtype)

                kcar[j] = dkc
                vcar[j] = dvc

        @pl.when(i == nb)
        def _():
            dk_ref[...] = kcar[...].astype(dk_ref.dtype)
            dv_ref[...] = vcar[...].astype(dv_ref.dtype)

    last = nb - 1
    q_spec = pl.BlockSpec((SWA_Q_HEADS, SWA_BLOCK, SWA_HD), lambda i: (0, jnp.minimum(i, last), 0))
    cur = pl.BlockSpec((SWA_KV_HEADS, SWA_BLOCK, SWA_HD), lambda i: (0, jnp.minimum(i, last), 0))
    prev = pl.BlockSpec((SWA_KV_HEADS, SWA_BLOCK, SWA_HD), lambda i: (0, jnp.clip(i - 1, 0, last), 0))
    b_spec = pl.BlockSpec(bias.shape, lambda i: (0, 0, 0))
    s_spec = pl.BlockSpec(sinks.shape, lambda i: (0, 0, 0))
    carry = pltpu.VMEM((SWA_KV_HEADS, SWA_BLOCK, SWA_HD), F32)
    return pl.pallas_call(
        body, name="attn_bwd", grid=(nb + 1,),
        in_specs=[q_spec, cur, prev, cur, prev, b_spec, s_spec, q_spec],
        out_specs=[q_spec, prev, prev, b_spec, s_spec],
        out_shape=[jax.ShapeDtypeStruct(q.shape, BF16), jax.ShapeDtypeStruct(k.shape, BF16),
                   jax.ShapeDtypeStruct(k.shape, BF16), jax.ShapeDtypeStruct(bias.shape, F32),
                   jax.ShapeDtypeStruct(sinks.shape, F32)],
        scratch_shapes=[carry, carry],
        compiler_params=_params(("arbitrary",)),
    )(q, k, k, v, v, bias, sinks, do)


def _rel_onehot():
    qi = jnp.arange(SWA_BLOCK)[:, None]
    ki = jnp.arange(2 * SWA_BLOCK)[None, :]
    n = jnp.maximum(qi + SWA_BLOCK - ki, 0)
    max_exact = REL_BUCKETS // 2
    nf = jnp.maximum(n, 1).astype(F32)
    large = max_exact + (jnp.log(nf / max_exact) / math.log(REL_MAX_DISTANCE / max_exact)
                         * (REL_BUCKETS - max_exact)).astype(jnp.int32)
    bucket = jnp.where(n < max_exact, n, jnp.minimum(large, REL_BUCKETS - 1)).reshape(-1)
    return (bucket[None, :] == jnp.arange(REL_BUCKETS)[:, None]).astype(F32)


def _final(h, w, target, tm=256):
    T = h.shape[0]
    tm = min(tm, T)

    def fn(i, hv, wv, tv):
        y, vjp = jax.vjp(_rms_core, hv, wv)
        err = y - tv
        dh, dw = vjp(err * (1.0 / D))
        part = 0.5 * jnp.sum(jnp.sum(err * err, axis=1, keepdims=True) * (1.0 / D), axis=0, keepdims=True)
        return jnp.broadcast_to(part, (SUBLANE, LANE)), dh, dw

    ins = [(h, "row", None), (w, "full", None), (target, "row", None)]
    outs = [((SUBLANE, LANE), F32, "acc"), ((T, D), F32, "row"), ((1, D), F32, "acc")]
    return _rowcall("final", fn, T, tm, ins, outs)


def _heads_major(a, heads, hd):
    return a.reshape(a.shape[0], heads, hd).transpose(1, 0, 2)


def _heads_minor(a):
    return a.transpose(1, 0, 2).reshape(a.shape[1], a.shape[0] * a.shape[2])


def _ffn_fwd(tag, h, P, layer):
    n = _rms_fwd(f"{tag}_rms", h, P["ffn_norm_w"][layer:layer + 1])
    up = _mm_up(f"{tag}_up", n, P["w_up"], layer)
    act = _ffn_act_fwd(f"{tag}_act", up, P["ffn_conv_w"][layer], P["ffn_conv_b"][layer:layer + 1])
    out = _mm_nn(f"{tag}_down", act, P["w_down"][layer], F32, res=h)
    return out, (n, up, act)


def _ffn_bwd(tag, h, saved, dout, P, layer, into=(None, None)):
    n, up, act = saved
    cw, cb = P["ffn_conv_w"][layer], P["ffn_conv_b"][layer:layer + 1]
    dact = _mm_nt(f"{tag}_down_dx", dout, P["w_down"][layer], BF16)
    g_down = _mm_down_tn(f"{tag}_down_dw", act, dout, layer, into[1])
    du, dcw, dcb = _ffn_act_bwd(f"{tag}_act_bwd", up, cw, cb, dact)
    dup = _ffn_conv_bwd(f"{tag}_conv_bwd", du, cw)
    g_up = _mm_up_tn(f"{tag}_up_dw", n, dup, layer, into[0])
    dn = _mm_up_nt(f"{tag}_up_dx", dup, P["w_up"], layer)
    dh, dnw = _rms_bwd(f"{tag}_rms_bwd", h, [(P["ffn_norm_w"][layer:layer + 1], dn)], [dout])
    return dh, dict(w_down=g_down, w_up=g_up, conv_w=dcw[:3], conv_b=dcb, norm_w=dnw)


def _local_step(x, target, P, late=None):
    T = x.shape[0]
    n0 = _rms_fwd("a_rms", x, P["a_norm_w"])
    pm = _mm_nn("gdn_in", n0, P["w_in_main"], BF16)
    pba = _mm_nn("gdn_in_ba", n0, P["w_in_ba"], F32)
    qn, kn, v, beta, g = _gdn_pre_fwd(pm, pba, P["a_conv_w"], P["a_log"], P["dt_bias"])
    g_rows, beta_rows = _gate_rows(g), _gate_rows(beta)
    o, sall, tall, *gathered = _gdn_fwd(qn, kn, v, g_rows, beta_rows, gather=late)
    if late is not None:
        P = {**P, **_late_weights(gathered)}
    on = _gnorm_fwd(o, pm, P["a_out_norm_w"])
    h1 = _mm_nn("gdn_out", on, P["w_out"], F32, res=x)
    h2, ffn0 = _ffn_fwd("ffn0", h1, P, 0)
    nkv = _rms_fwd("kv_rms", h2, P["kv_norm_w"])
    kv = _mm_nn("kv_proj", nkv, P["w_kv"], BF16)
    nb = _rms_fwd("b_rms", h2, P["b_norm_w"])
    qp = _mm_nn("q_proj", nb, P["w_q"], BF16)
    q3 = _heads_major(qp, SWA_Q_HEADS, SWA_HD)
    k3 = _heads_major(kv[:, :SWA_KV_HEADS * SWA_HD], SWA_KV_HEADS, SWA_HD)
    v3 = _heads_major(kv[:, SWA_KV_HEADS * SWA_HD:], SWA_KV_HEADS, SWA_HD)
    onehot = _rel_onehot()
    bias = _mm_nn("rel_bias", P["rel_table_t"], onehot, F32, precision=HIGHEST)
    bias = bias.reshape(SWA_Q_HEADS, SWA_BLOCK, 2 * SWA_BLOCK)
    oa = _heads_minor(_attn_fwd(q3, k3, v3, bias, P["sinks"]))
    h3 = _mm_nn("o_proj", oa, P["w_o"], F32, res=h2)
    h4, ffn1 = _ffn_fwd("ffn1", h3, P, 1)
    loss, dh4, d_final = _final(h4, P["final_norm_w"], target)

    dh3, gf1 = _ffn_bwd("ffn1", h3, ffn1, dh4, P, 1)
    doa = _mm_nt("o_proj_dx", dh3, P["w_o"], BF16)
    g_wo = _mm_tn("o_proj_dw", oa, dh3)
    dq3, dk3, dv3, dbias, dsinks = _attn_bwd(q3, k3, v3, bias, P["sinks"], _heads_major(doa, SWA_Q_HEADS, SWA_HD))
    dqp = _heads_minor(dq3)
    dkv = jnp.concatenate([_heads_minor(dk3), _heads_minor(dv3)], axis=1)
    g_wq = _mm_tn("q_proj_dw", nb, dqp)
    dnb = _mm_nt("q_proj_dx", dqp, P["w_q"], F32)
    g_wkv = _mm_tn("kv_proj_dw", nkv, dkv)
    dnkv = _mm_nt("kv_proj_dx", dkv, P["w_kv"], F32)
    dh2, d_bnorm, d_kvnorm = _rms_bwd("b_kv_rms_bwd", h2, [(P["b_norm_w"], dnb), (P["kv_norm_w"], dnkv)], [dh3])
    g_table = _mm_nt("rel_bias_dw", onehot, dbias.reshape(SWA_Q_HEADS, -1), F32, precision=HIGHEST)
    dh1, gf0 = _ffn_bwd("ffn0", h1, ffn0, dh2, P, 0, into=(gf1["w_up"], gf1["w_down"]))
    don = _mm_nt("gdn_out_dx", dh1, P["w_out"], BF16)
    g_wout = _mm_tn("gdn_out_dw", on, dh1)
    do, dz, d_gnorm = _gnorm_bwd(o, pm, P["a_out_norm_w"], don)
    dq, dk, dv, dg, dbeta = _gdn_bwd(qn, kn, v, g_rows, beta_rows, sall, tall, do)
    dy, dpba, d_aconv, d_alog, d_dtb = _gdn_pre_bwd(pm, pba, P["a_conv_w"], P["a_log"], P["dt_bias"],
                                                    dq, dk, dv, _gate_cols(dbeta), _gate_cols(dg))
    dpm = _gdn_conv_bwd(dy, dz, P["a_conv_w"])
    g_win_main = _mm_tn("gdn_in_dw", n0, dpm)
    g_win_ba = _mm_tn("gdn_in_ba_dw", n0, dpba)
    dn0 = _mm_nt("gdn_in_dx", dpm, P["w_in_main"], F32)
    dn0 = _mm_nt("gdn_in_ba_dx", dpba, P["w_in_ba"], F32, res=dn0)
    dx, d_anorm = _rms_bwd("a_rms_bwd", x, [(P["a_norm_w"], dn0)], [dh1])

    nh = GDN_V_HEADS
    grads = dict(
        a_norm_w=d_anorm,
        a_w_in=jnp.concatenate([g_win_main, g_win_ba[:, :nh], g_win_ba[:, LANE:LANE + nh]], axis=1),
        a_conv_w=d_aconv[:4], a_a_log=d_alog[:, :nh], a_dt_bias=d_dtb[:, :nh], a_out_norm_w=d_gnorm,
        a_w_out=g_wout, kv_norm_w=d_kvnorm, w_kv=g_wkv, b_norm_w=d_bnorm, b_w_q=g_wq,
        b_sinks=dsinks[:, :, 0].reshape(1, SWA_Q_HEADS), b_w_o=g_wo, rel_bias_table=g_table,
        ffn_norm_w=jnp.concatenate([gf0["norm_w"], gf1["norm_w"]], axis=0),
        ffn_w_up=gf0["w_up"],
        ffn_conv_w=jnp.stack([gf0["conv_w"], gf1["conv_w"]], axis=0),
        ffn_conv_b=jnp.concatenate([gf0["conv_b"], gf1["conv_b"]], axis=0),
        ffn_w_down=gf0["w_down"],
        final_norm_w=d_final,
    )
    return loss, dx, grads


HBM_SPEC = pl.BlockSpec(memory_space=pltpu.HBM)
VMEM_SPEC = pl.BlockSpec(memory_space=pltpu.VMEM)


def _coords():
    return lax.axis_index("x"), lax.axis_index("y"), lax.axis_index("c")


def _remote(src, dst, send_sem, recv_sem, device):
    return pltpu.make_async_remote_copy(src_ref=src, dst_ref=dst, send_sem=send_sem, recv_sem=recv_sem,
                                        device_id=device, device_id_type=MESH)


def _other_chips(x, y):
    return [(1 - x, y), (x, 1 - y), (1 - x, 1 - y)]


def _gather_copies(shapes, split, ins, outs, send_sems, recv_sems):
    x, y, c = _coords()
    p = 2 * x + y
    ici, forwards, from_sibling = [], [], []
    for a, shape in enumerate(shapes):
        h = shape[0] // 2
        for j, chip in enumerate(_other_chips(x, y)):
            q = 2 * chip[0] + chip[1]
            if split[a]:
                mine, theirs = pl.ds(c * h, h), pl.ds((1 - c) * h, h)
                ici.append(_remote(ins[a].at[mine], outs[a].at[p, mine], send_sems.at[6 * a + j],
                                   recv_sems.at[6 * a + j], (*chip, c)))
                land = outs[a].at[q, mine]
                forwards.append(_remote(land, land, send_sems.at[6 * a + 3 + j], recv_sems.at[6 * a + 3 + j],
                                        (x, y, 1 - c)))
                land = outs[a].at[q, theirs]
                from_sibling.append(_remote(land, land, send_sems.at[6 * a + 3 + j], recv_sems.at[6 * a + 3 + j],
                                            (x, y, 1 - c)))
            else:
                ici.append(_remote(ins[a], outs[a].at[p], send_sems.at[6 * a + j], recv_sems.at[6 * a + j],
                                   (*chip, c)))
                forwards.append(None)
    return ici, forwards, from_sibling


def _gather_arrival(shapes, split, ins, outs, send_sems, recv_sems):
    x, y, c = _coords()
    ici, forwards, from_sibling = _gather_copies(shapes, split, ins, outs, send_sems, recv_sems)
    k = 0
    for a, shape in enumerate(shapes):
        h = shape[0] // 2
        for j, chip in enumerate(_other_chips(x, y)):
            q = 2 * chip[0] + chip[1]
            land = outs[a].at[q, pl.ds(c * h, h)] if split[a] else outs[a].at[q]
            _remote(land, land, send_sems.at[6 * a + j], recv_sems.at[6 * a + j], (*chip, c)).wait_recv()
            if forwards[k] is not None:
                forwards[k].start()
            k += 1
    for cp in from_sibling:
        cp.wait_recv()
    for cp in ici + [f for f in forwards if f is not None]:
        cp.wait_send()


def _all_gather(arrs, split, remote):
    n = len(arrs)
    now = [a for a in range(n) if remote[a]]
    shapes = [arrs[a].shape for a in now]
    splits = [split[a] for a in now]

    def body(*refs):
        ins, outs, stage = refs[:n], refs[n:2 * n], refs[2 * n:3 * n]
        send_sems, recv_sems, in_sems, out_sems = refs[3 * n:]
        p = 2 * lax.axis_index("x") + lax.axis_index("y")
        gathered = ([ins[a] for a in now], [outs[a] for a in now], send_sems, recv_sems)
        loads = [pltpu.make_async_copy(ins[a], stage[a], in_sems.at[a]) for a in range(n)]
        for cp in loads:
            cp.start()
        for cp in _gather_copies(shapes, splits, *gathered)[0]:
            cp.start()
        stores = [pltpu.make_async_copy(stage[a], outs[a].at[p], out_sems.at[a]) for a in range(n)]
        for a in range(n):
            loads[a].wait()
            stores[a].start()
        _gather_arrival(shapes, splits, *gathered)
        for cp in stores:
            cp.wait()

    return pl.pallas_call(
        body, name="weights_all_gather", in_specs=[HBM_SPEC] * n, out_specs=[HBM_SPEC] * n,
        out_shape=[jax.ShapeDtypeStruct((N_CHIPS,) + a.shape, a.dtype) for a in arrs],
        scratch_shapes=[pltpu.VMEM(a.shape, a.dtype) for a in arrs]
        + [pltpu.SemaphoreType.DMA((6 * len(now),)), pltpu.SemaphoreType.DMA((6 * len(now),)),
           pltpu.SemaphoreType.DMA((n,)), pltpu.SemaphoreType.DMA((n,))],
        compiler_params=pltpu.CompilerParams(vmem_limit_bytes=VMEM_LIMIT),
    )(*arrs)


PAIR_SWAP_PIECES = 2


def _pair_swap(gs):
    n = len(gs)

    def body(*refs):
        ins, other = refs[:n], refs[n:2 * n]
        send_sems, recv_sems = refs[2 * n:]
        x, y, c = _coords()
        cps = []
        for a in range(n):
            h = gs[a].shape[1] // 2
            piece = h // PAIR_SWAP_PIECES
            for q in range(N_CHIPS):
                for r in range(PAIR_SWAP_PIECES):
                    k = (a * N_CHIPS + q) * PAIR_SWAP_PIECES + r
                    cp = _remote(ins[a].at[q, pl.ds((1 - c) * h + r * piece, piece)],
                                 other[a].at[q, pl.ds(r * piece, piece)], send_sems.at[k], recv_sems.at[k],
                                 (x, y, 1 - c))
                    cp.start()
                    cps.append(cp)
        for cp in cps:
            cp.wait()

    half = [jax.ShapeDtypeStruct((N_CHIPS, g.shape[1] // 2, g.shape[2]), g.dtype) for g in gs]
    nsem = n * N_CHIPS * PAIR_SWAP_PIECES
    return pl.pallas_call(
        body, name="grads_pair_swap", in_specs=[HBM_SPEC] * n, out_specs=[HBM_SPEC] * n, out_shape=half,
        scratch_shapes=[pltpu.SemaphoreType.DMA((nsem,)), pltpu.SemaphoreType.DMA((nsem,))],
    )(*gs)


def _chip_scatter(ps):
    n = len(ps)

    def body(*refs):
        ins, outs = refs[:n], refs[n:2 * n]
        send_sems, recv_sems = refs[2 * n:]
        x, y, c = _coords()
        chips = _other_chips(x, y)
        sends = []
        for a in range(n):
            for j, chip in enumerate(chips):
                q = 2 * chip[0] + chip[1]
                cp = _remote(ins[a].at[q], outs[a].at[j], send_sems.at[3 * a + j], recv_sems.at[3 * a + j], (*chip, c))
                cp.start()
                sends.append(cp)
        for cp in sends:
            cp.wait_recv()
        for cp in sends:
            cp.wait_send()

    return pl.pallas_call(
        body, name="grads_chip_scatter", in_specs=[HBM_SPEC] * n, out_specs=[HBM_SPEC] * n,
        out_shape=[jax.ShapeDtypeStruct((N_CHIPS - 1,) + a.shape[1:], a.dtype) for a in ps],
        scratch_shapes=[pltpu.SemaphoreType.DMA((3 * n,)), pltpu.SemaphoreType.DMA((3 * n,))],
    )(*ps)


def _pair_share(rs):
    n = len(rs)

    def body(*refs):
        ins, outs, stage = refs[:n], refs[n:2 * n], refs[2 * n:3 * n]
        send_sems, recv_sems, in_sems, out_sems = refs[3 * n:]
        x, y, c = _coords()

        def mine(a):
            h = rs[a].shape[0]
            return outs[a].at[pl.ds(c * h, h)]

        loads = [pltpu.make_async_copy(ins[a], stage[a], in_sems.at[a]) for a in range(n)]
        for cp in loads:
            cp.start()
        sends = [_remote(ins[a], mine(a), send_sems.at[a], recv_sems.at[a], (x, y, 1 - c)) for a in range(n)]
        for cp in sends:
            cp.start()
        stores = [pltpu.make_async_copy(stage[a], mine(a), out_sems.at[a]) for a in range(n)]
        for a in range(n):
            loads[a].wait()
            stores[a].start()
        for a in range(n):
            h = rs[a].shape[0]
            land = outs[a].at[pl.ds((1 - c) * h, h)]
            _remote(land, land, send_sems.at[a], recv_sems.at[a], (x, y, 1 - c)).wait_recv()
        for cp in sends:
            cp.wait_send()
        for cp in stores:
            cp.wait()

    return pl.pallas_call(
        body, name="grads_pair_share", in_specs=[HBM_SPEC] * n, out_specs=[HBM_SPEC] * n,
        out_shape=[jax.ShapeDtypeStruct((2 * a.shape[0], a.shape[1]), a.dtype) for a in rs],
        scratch_shapes=[pltpu.VMEM(a.shape, a.dtype) for a in rs] + [pltpu.SemaphoreType.DMA((n,))] * 4,
        compiler_params=pltpu.CompilerParams(vmem_limit_bytes=VMEM_LIMIT),
    )(*rs)


def _small_all_reduce(buf):
    R = buf.shape[0]
    ndev = 2 * N_CHIPS

    def body(in_ref, out_ref, gath, send_sems, recv_sems):
        x, y, c = _coords()
        me = 4 * x + 2 * y + c
        gath[me] = in_ref[...]
        peers = []
        for d in range(1, ndev):
            px = 1 - x if d & 4 else x
            py = 1 - y if d & 2 else y
            pc = 1 - c if d & 1 else c
            peers.append((px, py, pc))
        sends = []
        for d, peer in enumerate(peers):
            cp = _remote(in_ref, gath.at[me], send_sems.at[d], recv_sems.at[d], peer)
            cp.start()
            sends.append(cp)
        for d, peer in enumerate(peers):
            land = gath.at[4 * peer[0] + 2 * peer[1] + peer[2]]
            _remote(land, land, send_sems.at[d], recv_sems.at[d], peer).wait_recv()
        for cp in sends:
            cp.wait_send()
        acc = gath[0]
        for s in range(1, ndev):
            acc = acc + gath[s]
        out_ref[...] = acc

    return pl.pallas_call(
        body, name="small_all_reduce", in_specs=[VMEM_SPEC], out_specs=VMEM_SPEC,
        out_shape=jax.ShapeDtypeStruct(buf.shape, F32),
        scratch_shapes=[pltpu.VMEM((ndev, R, LANE), F32), pltpu.SemaphoreType.DMA((ndev - 1,)),
                        pltpu.SemaphoreType.DMA((ndev - 1,))],
    )(buf)


def _pair_add(name, own, other):
    h = own.shape[1]
    tm = _tile(h, (128, 64, 32, 16))

    def fn(i, a, b):
        return (a + b,)

    return _rowcall(name, fn, h, tm, [(own, "row", None), (other, "row", None)], [(own.shape, BF16, "row")])[0]


def _chip_add(name, own, parts):
    h = parts.shape[1]
    tm = _tile(h, (128, 64, 32, 16))

    def fn(i, o, a):
        a = a.astype(F32)
        return (((o.astype(F32) + a[0]) + a[1]) + a[2],)

    return _rowcall(name, fn, h, tm, [(own, "row", None), (parts, "row", None)], [(parts.shape[1:], F32, "row")])[0]


def _adamw(name, w, g, m, v):
    R = w.shape[0]
    tm = _tile(R, (256, 128, 64, 32, 16, 8))

    def fn(i, wv, gv, mv, vv):
        m2 = ADAM_B1 * mv + (1.0 - ADAM_B1) * gv
        v2 = ADAM_B2 * vv + (1.0 - ADAM_B2) * (gv * gv)
        m_hat = m2 / (1.0 - ADAM_B1 ** ADAM_STEP)
        v_hat = v2 / (1.0 - ADAM_B2 ** ADAM_STEP)
        delta = -ADAM_LR * (m_hat / (jnp.sqrt(v_hat) + ADAM_EPS) + ADAM_WD * wv)
        return delta, m2, v2

    ins = [(a, "row", None) for a in (w, g, m, v)]
    return _rowcall(name, fn, R, tm, ins, [(w.shape, F32, "row")] * 3)


def _pack(arrs):
    flat = jnp.concatenate([a.reshape(-1).astype(F32) for a in arrs])
    size = flat.shape[0]
    padded = -(-size // (SUBLANE * LANE)) * SUBLANE * LANE
    return jnp.pad(flat, (0, padded - size)).reshape(-1, LANE)


def _unpack(buf, shapes):
    flat = buf.reshape(-1)
    out, off = [], 0
    for s in shapes:
        size = math.prod(s)
        out.append(flat[off:off + size].reshape(s))
        off += size
    return out


BIG = ("a_w_in", "a_w_out", "w_kv", "b_w_q", "b_w_o", "ffn_w_up", "ffn_w_down")
WEIGHTS = ("a_norm_w", "a_w_in", "a_conv_w", "a_a_log", "a_dt_bias", "a_out_norm_w", "a_w_out", "kv_norm_w", "w_kv",
           "b_norm_w", "b_w_q", "b_sinks", "b_w_o", "rel_bias_table", "ffn_norm_w", "ffn_w_up", "ffn_conv_w",
           "ffn_conv_b", "ffn_w_down", "final_norm_w")
SMALL = tuple(n for n in WEIGHTS if n not in BIG)
SMALL_SHARDED = {"a_norm_w": 1, "a_conv_w": 2, "ffn_conv_w": 2}


def _quarter_2d(name, a):
    if name in ("ffn_w_up", "ffn_w_down"):
        return a.reshape(a.shape[0] * a.shape[1], a.shape[2])
    return a.reshape(a.shape[-2], a.shape[-1])


def _whole_weights(w):
    bigs = [_quarter_2d(n, w[n]).astype(BF16) for n in BIG]
    smalls = [w["a_norm_w"], w["a_conv_w"][0], w["ffn_conv_w"].reshape(6, DFF2_SHARD)]
    remote = [True] + [False] * (len(bigs) - 1) + [True] * len(smalls)
    g = _all_gather(bigs + smalls, [True] * len(bigs) + [False] * len(smalls), remote)
    w_in = g[0].transpose(1, 0, 2).reshape(D, GDN_IN)
    nh = GDN_V_HEADS
    zpad = jnp.zeros((D, LANE - nh), BF16)
    w_in_ba = jnp.concatenate([w_in[:, GDN_MAIN:GDN_MAIN + nh], zpad, w_in[:, GDN_MAIN + nh:], zpad], axis=1)
    lane_pad = lambda a: jnp.pad(a, ((0, 0), (0, LANE - nh)))
    early = dict(
        a_norm_w=g[7].reshape(1, D), w_in_main=w_in[:, :GDN_MAIN], w_in_ba=w_in_ba,
        a_conv_w=g[8].transpose(1, 0, 2).reshape(4, GDN_CONV), a_log=lane_pad(w["a_a_log"]),
        dt_bias=lane_pad(w["a_dt_bias"]), a_out_norm_w=w["a_out_norm_w"],
        kv_norm_w=w["kv_norm_w"].reshape(1, D), b_norm_w=w["b_norm_w"],
        sinks=jnp.broadcast_to(w["b_sinks"].reshape(SWA_KV_HEADS, SWA_GROUP, 1), (SWA_KV_HEADS, SWA_GROUP, LANE)),
        rel_table_t=w["rel_bias_table"].T, ffn_norm_w=w["ffn_norm_w"],
        ffn_conv_w=g[9].reshape(N_CHIPS, 2, 3, DFF2_SHARD).transpose(1, 2, 0, 3).reshape(2, 3, DFF2),
        ffn_conv_b=w["ffn_conv_b"], final_norm_w=w["final_norm_w"].reshape(1, D),
    )
    return early, (bigs[1:], g[1:len(bigs)])


def _late_weights(g):
    return dict(
        w_out=g[0].reshape(GDN_V, D), w_kv=g[1].reshape(D, 2 * SWA_KV_HEADS * SWA_HD), w_q=g[2].reshape(D, D),
        w_o=g[3].reshape(D, D), w_up=g[4].reshape(N_CHIPS, 2, D, DFF2_SHARD),
        w_down=g[5].reshape(N_CHIPS, 2, DFF_SHARD, D).transpose(1, 0, 2, 3).reshape(2, DFF, D),
    )


def _chip_major(name, g):
    if name == "a_w_in":
        return g.reshape(D, N_CHIPS, GDN_IN_SHARD).transpose(1, 0, 2)
    if name == "ffn_w_up":
        return g.reshape(N_CHIPS, 2 * D, DFF2_SHARD)
    if name == "ffn_w_down":
        return g.reshape(N_CHIPS, 2 * DFF_SHARD, D)
    return g.reshape(N_CHIPS, g.shape[0] // N_CHIPS, g.shape[1])


def kernel(x, a_norm_w, a_w_in, a_conv_w, a_a_log, a_dt_bias, a_out_norm_w, a_w_out, kv_norm_w, w_kv, b_norm_w, b_w_q, b_sinks, b_w_o, rel_bias_table, ffn_norm_w, ffn_w_up, ffn_conv_w, ffn_conv_b, ffn_w_down, final_norm_w, loss_target, m_a_norm_w, m_a_w_in, m_a_conv_w, m_a_a_log, m_a_dt_bias, m_a_out_norm_w, m_a_w_out, m_kv_norm_w, m_w_kv, m_b_norm_w, m_b_w_q, m_b_sinks, m_b_w_o, m_rel_bias_table, m_ffn_norm_w, m_ffn_w_up, m_ffn_conv_w, m_ffn_conv_b, m_ffn_w_down, m_final_norm_w, v_a_norm_w, v_a_w_in, v_a_conv_w, v_a_a_log, v_a_dt_bias, v_a_out_norm_w, v_a_w_out, v_kv_norm_w, v_w_kv, v_b_norm_w, v_b_w_q, v_b_sinks, v_b_w_o, v_rel_bias_table, v_ffn_norm_w, v_ffn_w_up, v_ffn_conv_w, v_ffn_conv_b, v_ffn_w_down, v_final_norm_w):
    w = dict(zip(WEIGHTS, (a_norm_w, a_w_in, a_conv_w, a_a_log, a_dt_bias, a_out_norm_w, a_w_out, kv_norm_w, w_kv,
                           b_norm_w, b_w_q, b_sinks, b_w_o, rel_bias_table, ffn_norm_w, ffn_w_up, ffn_conv_w,
                           ffn_conv_b, ffn_w_down, final_norm_w)))
    m = dict(zip(WEIGHTS, (m_a_norm_w, m_a_w_in, m_a_conv_w, m_a_a_log, m_a_dt_bias, m_a_out_norm_w, m_a_w_out,
                           m_kv_norm_w, m_w_kv, m_b_norm_w, m_b_w_q, m_b_sinks, m_b_w_o, m_rel_bias_table,
                           m_ffn_norm_w, m_ffn_w_up, m_ffn_conv_w, m_ffn_conv_b, m_ffn_w_down, m_final_norm_w)))
    v = dict(zip(WEIGHTS, (v_a_norm_w, v_a_w_in, v_a_conv_w, v_a_a_log, v_a_dt_bias, v_a_out_norm_w, v_a_w_out,
                           v_kv_norm_w, v_w_kv, v_b_norm_w, v_b_w_q, v_b_sinks, v_b_w_o, v_rel_bias_table,
                           v_ffn_norm_w, v_ffn_w_up, v_ffn_conv_w, v_ffn_conv_b, v_ffn_w_down, v_final_norm_w)))
    T = x.shape[1]
    chip = 2 * lax.axis_index("x") + lax.axis_index("y")

    early, late = _whole_weights(w)
    loss_part, dx, grads = _local_step(x.reshape(T, D), loss_target.reshape(T, D), early, late)

    core = lax.axis_index("c")
    whole = [_chip_major(n, grads[n]) for n in BIG]
    other = _pair_swap(whole)
    own = [lax.dynamic_slice_in_dim(g, core * (g.shape[1] // 2), g.shape[1] // 2, 1) for g in whole]
    pair = [_pair_add(f"pair_add_{n}", a, b) for n, a, b in zip(BIG, own, other)]
    parts = _chip_scatter(pair)
    halves = [_chip_add(f"chip_add_{n}", lax.dynamic_index_in_dim(a, chip, 0, keepdims=False), b)
              for n, a, b in zip(BIG, pair, parts)]
    quarter = _pair_share(halves)
    out_g, out_d, out_m, out_v = {}, {}, {}, {}
    for n, g2 in zip(BIG, quarter):
        res = _adamw(f"adamw_{n}", _quarter_2d(n, w[n]), g2, _quarter_2d(n, m[n]), _quarter_2d(n, v[n]))
        out_g[n] = g2.reshape(w[n].shape)
        out_d[n], out_m[n], out_v[n] = (r.reshape(w[n].shape) for r in res)

    whole = [grads[n] for n in SMALL]
    summed = _unpack(_small_all_reduce(_pack([loss_part[0:1, 0:1]] + whole)), [(1, 1)] + [a.shape for a in whole])
    loss = summed[0].reshape(())
    small_g = []
    for n, g in zip(SMALL, summed[1:]):
        if n in SMALL_SHARDED:
            axis = SMALL_SHARDED[n]
            g = g.reshape(w[n].shape[:axis] + (-1,) + w[n].shape[axis + 1:])
            size = w[n].shape[axis]
            g = lax.dynamic_slice_in_dim(g, chip * size, size, axis)
        small_g.append(g.reshape(w[n].shape))
    shapes = [w[n].shape for n in SMALL]
    res = _adamw("adamw_small", _pack([w[n] for n in SMALL]), _pack(small_g), _pack([m[n] for n in SMALL]),
                 _pack([v[n] for n in SMALL]))
    small_d, small_m, small_v = (_unpack(r, shapes) for r in res)
    for i, n in enumerate(SMALL):
        out_g[n], out_d[n], out_m[n], out_v[n] = small_g[i], small_d[i], small_m[i], small_v[i]

    return (loss, dx.reshape(x.shape), *[out_g[n] for n in WEIGHTS], *[out_d[n] for n in WEIGHTS],
            *[out_m[n] for n in WEIGHTS], *[out_v[n] for n in WEIGHTS])
```

```python
import functools
import math

import jax
import jax.numpy as jnp
from jax import lax
from jax.experimental import pallas as pl
from jax.experimental.pallas import tpu as pltpu

F32 = jnp.float32
BF16 = jnp.bfloat16
MESH = pl.DeviceIdType.MESH
HIGHEST = lax.Precision.HIGHEST

D = 1024
EPS = 1e-6
NEG_INF = -1e30
N_CHIPS = 4

GDN_QK_HEADS = 8
GDN_V_HEADS = 16
GDN_HD = 128
GDN_QK = GDN_QK_HEADS * GDN_HD
GDN_V = GDN_V_HEADS * GDN_HD
GDN_CONV = 2 * GDN_QK + GDN_V
GDN_MAIN = GDN_CONV + GDN_V
GDN_IN = GDN_MAIN + 2 * GDN_V_HEADS
GDN_IN_SHARD = GDN_IN // N_CHIPS
GDN_CHUNK = 64

SWA_Q_HEADS = 16
SWA_KV_HEADS = 4
SWA_GROUP = 4
SWA_HD = 64
SWA_BLOCK = 128
REL_BUCKETS = 32
REL_MAX_DISTANCE = 128

DFF = 2816
DFF2 = 2 * DFF
DFF2_SHARD = DFF2 // N_CHIPS
DFF_SHARD = DFF // N_CHIPS

ADAM_LR = 0.001
ADAM_B1 = 0.9
ADAM_B2 = 0.999
ADAM_EPS = 1e-08
ADAM_WD = 0.01
ADAM_STEP = 10

LANE = 128
SUBLANE = 8
VMEM_LIMIT = 56 * 1024 * 1024


def _params(sem, vmem=VMEM_LIMIT):
    return pltpu.CompilerParams(dimension_semantics=sem, vmem_limit_bytes=vmem)


def _rowcall(name, fn, T, tm, ins, outs):
    n = T // tm
    r8 = tm // SUBLANE
    last8 = T // SUBLANE - 1
    arrays, in_specs = [], []
    for arr, kind, cols in ins:
        arrays.append(arr)
        if kind == "full":
            in_specs.append(pl.BlockSpec(arr.shape, functools.partial(lambda nd, i: (0,) * nd, arr.ndim)))
        elif arr.ndim == 2:
            w, ci = cols if cols is not None else (arr.shape[1], 0)
            if kind == "row":
                in_specs.append(pl.BlockSpec((tm, w), functools.partial(lambda ci, i: (i, ci), ci)))
            elif kind == "prev":
                in_specs.append(pl.BlockSpec(
                    (SUBLANE, w), functools.partial(lambda ci, i: (jnp.maximum(i * r8 - 1, 0), ci), ci)))
            else:
                in_specs.append(pl.BlockSpec(
                    (SUBLANE, w), functools.partial(lambda ci, i: (jnp.minimum((i + 1) * r8, last8), ci), ci)))
        else:
            lead = arr.shape[:-2]
            in_specs.append(pl.BlockSpec(lead + (tm, arr.shape[-1]),
                                         functools.partial(lambda nl, i: (0,) * nl + (i, 0), len(lead))))
    out_shape, out_specs = [], []
    for shape, dtype, kind in outs:
        out_shape.append(jax.ShapeDtypeStruct(shape, dtype))
        if kind == "acc":
            out_specs.append(pl.BlockSpec(shape, functools.partial(lambda nd, i: (0,) * nd, len(shape))))
        else:
            lead = shape[:-2]
            out_specs.append(pl.BlockSpec(lead + (tm, shape[-1]),
                                          functools.partial(lambda nl, i: (0,) * nl + (i, 0), len(lead))))
    nin = len(arrays)

    def body(*refs):
        i = pl.program_id(0)
        vals = [r[...] for r in refs[:nin]]
        res = fn(i, *vals)
        for (shape, dtype, kind), o, r in zip(outs, refs[nin:], res):
            if kind == "row":
                o[...] = r.astype(dtype)
            else:
                @pl.when(i == 0)
                def _():
                    o[...] = r.astype(dtype)

                @pl.when(i > 0)
                def _():
                    o[...] += r.astype(dtype)

    res = pl.pallas_call(
        body, name=name, grid=(n,), in_specs=in_specs, out_specs=out_specs, out_shape=out_shape,
        compiler_params=_params(("arbitrary",)),
    )(*arrays)
    return res


def _mm(name, a, b, out_shape, out_dtype, grid, a_spec, b_spec, o_spec, dims, acc_shape, res=None, precision=None,
        into=None):
    nk = grid[2]
    n_in = 2 + (res is not None) + (into is not None)

    def body(*refs):
        a_ref, b_ref, o_ref = refs[0], refs[1], refs[n_in]
        r_ref = refs[2] if res is not None else None
        av, bv = a_ref[...], b_ref[...]
        if precision is None:
            av, bv = av.astype(BF16), bv.astype(BF16)
        p = lax.dot_general(av, bv, (dims, ((), ())), preferred_element_type=F32, precision=precision)

        def finish(x):
            if res is not None:
                x = x + r_ref[...].astype(F32)
            o_ref[...] = x.astype(out_dtype).reshape(o_ref.shape)

        if nk == 1:
            finish(p)
        else:
            acc = refs[-1]
            k = pl.program_id(2)

            @pl.when(k == 0)
            def _():
                acc[...] = p

            @pl.when(k > 0)
            def _():
                acc[...] += p

            @pl.when(k == nk - 1)
            def _():
                finish(acc[...])

    ops = [a, b] + ([res] if res is not None else []) + ([into] if into is not None else [])
    specs = [a_spec, b_spec] + ([o_spec] if res is not None else [])
    specs += [pl.BlockSpec(memory_space=pl.ANY)] if into is not None else []
    return pl.pallas_call(
        body, name=name, grid=grid, in_specs=specs, out_specs=o_spec,
        out_shape=jax.ShapeDtypeStruct(out_shape, out_dtype),
        input_output_aliases={n_in - 1: 0} if into is not None else {},
        scratch_shapes=[pltpu.VMEM(acc_shape, F32)] if nk > 1 else [],
        compiler_params=_params(("parallel", "parallel", "arbitrary")),
    )(*ops)


NN = ((1,), (0,))
NT = ((1,), (1,))
TN = ((0,), (0,))


BIG_TILES = (1024, 512, 256, 128)


def _tile(n, pref):
    for t in pref:
        if n % t == 0:
            return t
    return n


def _mm_nn(name, a, w, out_dtype, res=None, precision=None):
    M, K = a.shape
    N = w.shape[1]
    tm = _tile(M, BIG_TILES if K <= 2048 else BIG_TILES[1:])
    tn = _tile(N, BIG_TILES)
    return _mm(name, a, w, (M, N), out_dtype, (M // tm, N // tn, 1),
               pl.BlockSpec((tm, K), lambda i, j, k: (i, 0)), pl.BlockSpec((K, tn), lambda i, j, k: (0, j)),
               pl.BlockSpec((tm, tn), lambda i, j, k: (i, j)), NN, (tm, tn), res=res, precision=precision)


def _mm_nt(name, g, w, out_dtype, res=None, precision=None):
    M, N = g.shape
    K = w.shape[0]
    tm, tk = _tile(M, BIG_TILES), _tile(K, (1024, 1408, 512, 256, 128))
    tn = _tile(N, (1536,) + BIG_TILES)
    return _mm(name, g, w, (M, K), out_dtype, (M // tm, K // tk, N // tn),
               pl.BlockSpec((tm, tn), lambda i, j, k: (i, k)), pl.BlockSpec((tk, tn), lambda i, j, k: (j, k)),
               pl.BlockSpec((tm, tk), lambda i, j, k: (i, j)), NT, (tm, tk), res=res, precision=precision)


def _mm_tn(name, a, g, out_dtype=F32, precision=None):
    T, K = a.shape
    N = g.shape[1]
    tk, tn = _tile(K, (1024, 1408, 512, 256, 128)), _tile(N, BIG_TILES)
    tt = _tile(T, BIG_TILES)
    return _mm(name, a, g, (K, N), out_dtype, (K // tk, N // tn, T // tt),
               pl.BlockSpec((tt, tk), lambda i, j, k: (k, i)), pl.BlockSpec((tt, tn), lambda i, j, k: (k, j)),
               pl.BlockSpec((tk, tn), lambda i, j, k: (i, j)), TN, (tk, tn), precision=precision)


def _mm_up(name, n, wup, layer):
    T = n.shape[0]
    tm = _tile(T, BIG_TILES)
    return _mm(name, n, wup, (T, DFF2), BF16, (T // tm, N_CHIPS, 1),
               pl.BlockSpec((tm, D), lambda i, j, k: (i, 0)),
               pl.BlockSpec((None, None, D, DFF2_SHARD), lambda i, j, k: (j, layer, 0, 0)),
               pl.BlockSpec((tm, DFF2_SHARD), lambda i, j, k: (i, j)), NN, (tm, DFF2_SHARD))


def _mm_up_nt(name, du, wup, layer):
    T = du.shape[0]
    tm, tk = _tile(T, BIG_TILES), D
    return _mm(name, du, wup, (T, D), F32, (T // tm, D // tk, N_CHIPS),
               pl.BlockSpec((tm, DFF2_SHARD), lambda i, j, k: (i, k)),
               pl.BlockSpec((None, None, tk, DFF2_SHARD), lambda i, j, k: (k, layer, j, 0)),
               pl.BlockSpec((tm, tk), lambda i, j, k: (i, j)), NT, (tm, tk))


def _mm_up_tn(name, n, du, layer, into):
    T = n.shape[0]
    tk, tt = D, _tile(T, BIG_TILES)
    return _mm(name, n, du, (N_CHIPS, 2, D, DFF2_SHARD), F32, (D // tk, N_CHIPS, T // tt),
               pl.BlockSpec((tt, tk), lambda i, j, k: (k, i)), pl.BlockSpec((tt, DFF2_SHARD), lambda i, j, k: (k, j)),
               pl.BlockSpec((None, None, tk, DFF2_SHARD), lambda i, j, k: (j, layer, i, 0)), TN, (tk, DFF2_SHARD),
               into=into)


def _mm_down_tn(name, act, dout, layer, into):
    T = act.shape[0]
    tk, tn, tt = 2 * DFF_SHARD, _tile(D, BIG_TILES), _tile(T, BIG_TILES)
    return _mm(name, act, dout, (2, 2, 2, DFF_SHARD, D), F32, (DFF // tk, D // tn, T // tt),
               pl.BlockSpec((tt, tk), lambda i, j, k: (k, i)), pl.BlockSpec((tt, tn), lambda i, j, k: (k, j)),
               pl.BlockSpec((None, 2, None, DFF_SHARD, tn), lambda i, j, k: (i, 0, layer, 0, j)), TN, (tk, tn),
               into=into)


def _sigmoid(x):
    return 0.5 * jnp.tanh(0.5 * x) + 0.5


def _silu(x):
    return x * _sigmoid(x)


def _softplus(x):
    return jnp.maximum(x, 0.0) + jnp.log(1.0 + jnp.exp(-jnp.abs(x)))


def _rms_core(h, w):
    return h * lax.rsqrt(jnp.mean(h * h, axis=-1, keepdims=True) + EPS) * w


def _shift_down(x, halo, s, i):
    if s == 0:
        return x
    tm = x.shape[0]
    rolled = pltpu.roll(x, s, 0)
    patch = pltpu.roll(jnp.where(i == 0, 0.0, halo), s, 0)
    row = lax.broadcasted_iota(jnp.int32, patch.shape, 0)
    top = jnp.where(row < s, patch, rolled[:SUBLANE])
    return jnp.concatenate([top, rolled[SUBLANE:]], axis=0) if tm > SUBLANE else top


def _shift_up(x, halo, s, i, n):
    if s == 0:
        return x
    tm = x.shape[0]
    rolled = pltpu.roll(x, tm - s, 0)
    patch = pltpu.roll(jnp.where(i == n - 1, 0.0, halo), SUBLANE - s, 0)
    row = lax.broadcasted_iota(jnp.int32, patch.shape, 0)
    bottom = jnp.where(row >= SUBLANE - s, patch, rolled[tm - SUBLANE:])
    return jnp.concatenate([rolled[:tm - SUBLANE], bottom], axis=0) if tm > SUBLANE else bottom


def _taps(x, halo, K, i):
    return [_shift_down(x, halo, K - 1 - j, i) for j in range(K)]


def _conv_fwd(taps, w):
    y = w[0:1, :] * taps[0]
    for j in range(1, len(taps)):
        y = y + w[j:j + 1, :] * taps[j]
    return y


def _conv_dx(dy, halo_next, w, i, n):
    K = w.shape[0]
    dx = w[K - 1:K, :] * dy
    for j in range(K - 1):
        dx = dx + w[j:j + 1, :] * _shift_up(dy, halo_next, K - 1 - j, i, n)
    return dx


def _conv_dw(dy, taps):
    rows = [jnp.sum(dy * tap, axis=0, keepdims=True) for tap in taps]
    return jnp.concatenate(rows + [jnp.zeros((SUBLANE - len(taps), dy.shape[1]), F32)], axis=0)


def _rms_fwd(name, h, w, tm=512):
    T = h.shape[0]
    tm = min(tm, T)

    def fn(i, hv, wv):
        return (_rms_core(hv, wv),)

    return _rowcall(name, fn, T, tm, [(h, "row", None), (w, "full", None)], [((T, D), BF16, "row")])[0]


def _rms_bwd(name, h, pairs, adds, tm=256):
    T = h.shape[0]
    tm = min(tm, T)
    npair, nadd = len(pairs), len(adds)

    def fn(i, hv, *rest):
        ws, dns, ads = rest[:npair], rest[npair:2 * npair], rest[2 * npair:]
        dh = None
        dws = []
        for wv, dn in zip(ws, dns):
            _, vjp = jax.vjp(_rms_core, hv, wv)
            dhi, dwi = vjp(dn.astype(F32))
            dh = dhi if dh is None else dh + dhi
            dws.append(dwi)
        for a in ads:
            dh = dh + a.astype(F32)
        return (dh, *dws)

    ins = [(h, "row", None)] + [(w, "full", None) for w, _ in pairs] + [(dn, "row", None) for _, dn in pairs]
    ins += [(a, "row", None) for a in adds]
    outs = [((T, D), F32, "row")] + [((1, D), F32, "acc")] * npair
    return _rowcall(name, fn, T, tm, ins, outs)


def _l2(x):
    return x * lax.rsqrt(jnp.sum(x * x, axis=-1, keepdims=True) + EPS)


def _gdn_post_core(yq, yk, yv, pb, pa, a_log, dtb):
    qn = tuple(_l2(_silu(a)) * (GDN_HD ** -0.5) for a in yq)
    kn = tuple(_l2(_silu(a)) for a in yk)
    v = _silu(yv)
    beta = _sigmoid(pb)
    g = -jnp.exp(a_log) * _softplus(pa + dtb)
    return qn, kn, v, beta, g


def _heads(x, n):
    return tuple(x[:, GDN_HD * h:GDN_HD * (h + 1)] for h in range(n))


def _gdn_pre_fwd(pm, pba, conv_w, a_log, dtb, tm=128):
    T = pm.shape[0]
    tm = min(tm, T)

    def fn(i, x, halo, pbav, cw, al, db):
        y = _conv_fwd(_taps(x.astype(F32), halo.astype(F32), 4, i), cw)
        qn, kn, v, beta, g = _gdn_post_core(_heads(y[:, :GDN_QK], 8), _heads(y[:, GDN_QK:2 * GDN_QK], 8),
                                            y[:, 2 * GDN_QK:], pbav[:, :LANE], pbav[:, LANE:], al, db)
        return jnp.stack(qn), jnp.stack(kn), jnp.stack(_heads(v, GDN_V_HEADS)), beta, g

    ins = [(pm, "row", (GDN_CONV, 0)), (pm, "prev", (GDN_CONV, 0)), (pba, "row", None),
           (conv_w, "full", None), (a_log, "full", None), (dtb, "full", None)]
    outs = [((GDN_QK_HEADS, T, GDN_HD), BF16, "row"), ((GDN_QK_HEADS, T, GDN_HD), BF16, "row"),
            ((GDN_V_HEADS, T, GDN_HD), BF16, "row"), ((T, LANE), F32, "row"), ((T, LANE), F32, "row")]
    return _rowcall("gdn_pre_fwd", fn, T, tm, ins, outs)


def _gdn_pre_bwd(pm, pba, conv_w, a_log, dtb, dqn, dkn, dv, dbeta, dg, tm=128):
    T = pm.shape[0]
    tm = min(tm, T)

    def fn(i, x, halo, pbav, cw, al, db, dqv, dkv, dvv, dbv, dgv):
        taps = _taps(x.astype(F32), halo.astype(F32), 4, i)
        y = _conv_fwd(taps, cw)
        prim = (_heads(y[:, :GDN_QK], 8), _heads(y[:, GDN_QK:2 * GDN_QK], 8), y[:, 2 * GDN_QK:],
                pbav[:, :LANE], pbav[:, LANE:], al, db)
        _, vjp = jax.vjp(_gdn_post_core, *prim)
        cot = (tuple(dqv[h].astype(F32) for h in range(8)), tuple(dkv[h].astype(F32) for h in range(8)),
               jnp.concatenate([dvv[h].astype(F32) for h in range(GDN_V_HEADS)], axis=1), dbv, dgv)
        dyq, dyk, dyv, dpb, dpa, dal, ddb = vjp(cot)
        dy = jnp.concatenate(list(dyq) + list(dyk) + [dyv], axis=1)
        dcw = _conv_dw(dy, taps)
        return dy, jnp.concatenate([dpb, dpa], axis=1), dcw, dal, ddb

    ins = [(pm, "row", (GDN_CONV, 0)), (pm, "prev", (GDN_CONV, 0)), (pba, "row", None),
           (conv_w, "full", None), (a_log, "full", None), (dtb, "full", None),
           (dqn, "row", None), (dkn, "row", None), (dv, "row", None), (dbeta, "row", None), (dg, "row", None)]
    outs = [((T, GDN_CONV), BF16, "row"), ((T, 2 * LANE), F32, "row"), ((SUBLANE, GDN_CONV), F32, "acc"),
            ((1, LANE), F32, "acc"), ((1, LANE), F32, "acc")]
    return _rowcall("gdn_pre_bwd", fn, T, tm, ins, outs)


def _gdn_conv_bwd(dy, dz, conv_w, tm=256):
    T = dy.shape[0]
    tm = min(tm, T)
    n = T // tm

    def fn(i, dyv, halo, dzv, cw):
        dx = _conv_dx(dyv.astype(F32), halo.astype(F32), cw, i, n)
        return (jnp.concatenate([dx.astype(BF16), dzv.astype(BF16)], axis=1),)

    ins = [(dy, "row", None), (dy, "next", None), (dz, "row", None), (conv_w, "full", None)]
    return _rowcall("gdn_conv_bwd", fn, T, tm, ins, [((T, GDN_MAIN), BF16, "row")])[0]


def _bdot(a, b, dims=NN):
    return lax.dot_general(a.astype(BF16), b.astype(BF16), (dims, ((), ())), preferred_element_type=F32)


BNN = ((2,), (1,))
BNT = ((2,), (2,))
BTN = ((1,), (1,))


def _bmm(a, b, dims=BNN):
    return lax.dot_general(a.astype(BF16), b.astype(BF16), (dims, ((0,), (0,))), preferred_element_type=F32)


def _bmm3(a, b):
    ah, bh = a.astype(BF16), b.astype(BF16)
    al, bl = (a - ah.astype(F32)).astype(BF16), (b - bh.astype(F32)).astype(BF16)
    dn = (BNN, ((0,), (0,)))
    return (lax.dot_general(ah, bh, dn, preferred_element_type=F32)
            + lax.dot_general(al, bh, dn, preferred_element_type=F32)
            + lax.dot_general(ah, bl, dn, preferred_element_type=F32))


def _tri_inv(m):
    C = m.shape[-1]
    r = lax.broadcasted_iota(jnp.int32, (C, C), 0)
    c = lax.broadcasted_iota(jnp.int32, (C, C), 1)
    t = jnp.where(r == c, 1.0, 0.0) - m
    pw = _bmm3(m, m)
    t = t + _bmm3(t, pw)
    for _ in range(int(math.log2(C)) - 2):
        pw = _bmm(pw, pw)
        t = t + _bmm(t, pw)
    return t


def _tri_inv_vjp(t, dt):
    tt = jnp.swapaxes(t, 1, 2)
    return -_bmm(_bmm(tt, dt), tt)


def _twice(a):
    return jnp.broadcast_to(a[:, None], (a.shape[0], 2) + a.shape[1:]).reshape((2 * a.shape[0],) + a.shape[1:])


def _gdn_gates(grow, brow):
    C = grow.shape[2]
    r = lax.broadcasted_iota(jnp.int32, (C, C), 0)
    c = lax.broadcasted_iota(jnp.int32, (C, C), 1)
    tril, eye = r >= c, r == c
    gcol = jnp.sum(jnp.where(eye, grow, 0.0), axis=2, keepdims=True)
    bcol = jnp.sum(jnp.where(eye, brow, 0.0), axis=2, keepdims=True)
    gc_col = jnp.sum(jnp.where(tril, grow, 0.0), axis=2, keepdims=True)
    gc_row = jnp.sum(jnp.where(r <= c, gcol, 0.0), axis=1, keepdims=True)
    gc_last = jnp.sum(grow, axis=2, keepdims=True)
    decay = jnp.where(tril, jnp.exp(jnp.where(tril, gc_col - gc_row, 0.0)), 0.0)
    return bcol, gc_col, gc_last, decay


def _gdn_m(k, grow, brow):
    C = k.shape[1]
    strict = lax.broadcasted_iota(jnp.int32, (C, C), 0) > lax.broadcasted_iota(jnp.int32, (C, C), 1)
    bcol, _, _, decay = _gdn_gates(grow, brow)
    return jnp.where(strict, bcol * _twice(_bmm(k, k, BNT)) * decay, 0.0)


def _gdn_rest(q, k, v, grow, brow, t_mat, S):
    bcol, gc_col, gc_last, decay = _gdn_gates(grow, brow)
    qk = _twice(_bmm(q, k, BNT))
    k2, q2 = _twice(k), _twice(q)
    egc = jnp.exp(gc_col)
    u = _bmm(t_mat, v * bcol)
    w = _bmm(t_mat, k2 * (bcol * egc))
    v_new = u - _bmm(w, S)
    o = _bmm(q2 * egc, S) + _bmm(qk * decay, v_new)
    s_new = S * jnp.exp(gc_last) + _bmm(k2 * jnp.exp(gc_last - gc_col), v_new, BTN)
    return o, s_new


def _gdn_tb(T):
    return min(256, T)


def _gate_rows(g):
    T = g.shape[0]
    g = g[:, :GDN_V_HEADS].reshape(T // GDN_CHUNK, GDN_CHUNK, GDN_V_HEADS)
    return g.transpose(0, 2, 1)[:, :, None, :]


def _gate_cols(g):
    nc = g.shape[0]
    g = g[:, :, 0, :].transpose(0, 2, 1).reshape(nc * GDN_CHUNK, GDN_V_HEADS)
    return jnp.pad(g, ((0, 0), (0, LANE - GDN_V_HEADS)))


def _gdn_fwd(qn, kn, v, g, beta, gather=None):
    T = qn.shape[1]
    tb = _gdn_tb(T)
    nc = tb // GDN_CHUNK
    nsteps = T // tb
    quarters, buffers = gather if gather is not None else ((), ())
    ng = len(quarters)
    shapes = [a.shape for a in quarters]
    splits = [True] * ng

    def body(*refs):
        q_ref, k_ref, v_ref, g_ref, b_ref = refs[:5]
        src = refs[5:5 + ng]
        o_ref, sall_ref, tall_ref = refs[5 + 2 * ng:8 + 2 * ng]
        dst = refs[8 + 2 * ng:8 + 3 * ng]
        s_scr = refs[8 + 3 * ng]
        step = pl.program_id(0)

        @pl.when(step == 0)
        def _():
            s_scr[...] = jnp.zeros(s_scr.shape, F32)
            if ng:
                for cp in _gather_copies(shapes, splits, src, dst, *refs[9 + 3 * ng:])[0]:
                    cp.start()

        def chunk(ci, carry):
            rows = pl.ds(pl.multiple_of(ci * GDN_CHUNK, GDN_CHUNK), GDN_CHUNK)
            s = s_scr[...]
            sall_ref[ci] = s
            q, k = q_ref[:, rows, :].astype(F32), k_ref[:, rows, :].astype(F32)
            t_mat = _tri_inv(_gdn_m(k, g_ref[ci], b_ref[ci])).astype(BF16)
            tall_ref[ci] = t_mat
            o, s_new = _gdn_rest(q, k, v_ref[:, rows, :].astype(F32), g_ref[ci], b_ref[ci], t_mat.astype(F32), s)
            o_ref[:, rows, :] = o.astype(o_ref.dtype)
            s_scr[...] = s_new
            return carry

        lax.fori_loop(0, nc, chunk, 0)

        if ng:
            @pl.when(step == nsteps - 1)
            def _():
                _gather_arrival(shapes, splits, src, dst, *refs[9 + 3 * ng:])

    qk_spec = pl.BlockSpec((GDN_QK_HEADS, tb, GDN_HD), lambda i: (0, i, 0))
    v_spec = pl.BlockSpec((GDN_V_HEADS, tb, GDN_HD), lambda i: (0, i, 0))
    g_spec = pl.BlockSpec((nc, GDN_V_HEADS, 1, GDN_CHUNK), lambda i: (i, 0, 0, 0))
    anywhere = pl.BlockSpec(memory_space=pl.ANY)
    return pl.pallas_call(
        body, name="gdn_fwd", grid=(nsteps,),
        in_specs=[qk_spec, qk_spec, v_spec, g_spec, g_spec] + [anywhere] * (2 * ng),
        out_specs=[v_spec, pl.BlockSpec((nc, GDN_V_HEADS, GDN_HD, GDN_HD), lambda i: (i, 0, 0, 0)),
                   pl.BlockSpec((nc, GDN_V_HEADS, GDN_CHUNK, GDN_CHUNK), lambda i: (i, 0, 0, 0))] + [anywhere] * ng,
        out_shape=[jax.ShapeDtypeStruct((GDN_V_HEADS, T, GDN_HD), BF16),
                   jax.ShapeDtypeStruct((T // GDN_CHUNK, GDN_V_HEADS, GDN_HD, GDN_HD), F32),
                   jax.ShapeDtypeStruct((T // GDN_CHUNK, GDN_V_HEADS, GDN_CHUNK, GDN_CHUNK), BF16)]
        + [jax.ShapeDtypeStruct(b.shape, b.dtype) for b in buffers],
        input_output_aliases={5 + ng + a: 3 + a for a in range(ng)},
        scratch_shapes=[pltpu.VMEM((GDN_V_HEADS, GDN_HD, GDN_HD), F32)]
        + ([pltpu.SemaphoreType.DMA((6 * ng,)), pltpu.SemaphoreType.DMA((6 * ng,))] if ng else []),
        compiler_params=_params(("arbitrary",)),
    )(qn, kn, v, g, beta, *quarters, *buffers)


def _gdn_bwd(qn, kn, v, g, beta, sall, tall, do, scatter=()):
    T = qn.shape[1]
    tb = _gdn_tb(T)
    nc = tb // GDN_CHUNK
    nb = T // tb
    ns = len(scatter)

    def body(*refs):
        q_ref, k_ref, v_ref, g_ref, b_ref, sall_ref, tall_ref, do_ref = refs[:8]
        dq_ref, dk_ref, dv_ref, dg_ref, db_ref = refs[8 + ns:13 + ns]
        ds_scr = refs[13 + 2 * ns]
        comm = (refs[8:8 + ns], refs[13 + ns:13 + 2 * ns], *refs[14 + 2 * ns:])
        step = pl.program_id(0)

        @pl.when(step == 0)
        def _():
            ds_scr[...] = jnp.zeros(ds_scr.shape, F32)
            if ns:
                for cp in _scatter_copies(*comm):
                    cp.start()

        def chunk(cr, carry):
            ci = nc - 1 - cr
            rows = pl.ds(pl.multiple_of(ci * GDN_CHUNK, GDN_CHUNK), GDN_CHUNK)
            k, t_mat = k_ref[:, rows, :].astype(F32), tall_ref[ci].astype(F32)
            _, vjp = jax.vjp(_gdn_rest, q_ref[:, rows, :].astype(F32), k, v_ref[:, rows, :].astype(F32),
                             g_ref[ci], b_ref[ci], t_mat, sall_ref[ci])
            dq, dk, dv, dg, db, dt, ds = vjp((do_ref[:, rows, :].astype(F32), ds_scr[...]))
            _, vjp_m = jax.vjp(_gdn_m, k, g_ref[ci], b_ref[ci])
            dk_m, dg_m, db_m = vjp_m(_tri_inv_vjp(t_mat, dt))
            ds_scr[...] = ds
            dq_ref[:, rows, :] = dq
            dk_ref[:, rows, :] = dk + dk_m
            dv_ref[:, rows, :] = dv
            dg_ref[ci] = dg + dg_m
            db_ref[ci] = db + db_m
            return carry

        lax.fori_loop(0, nc, chunk, 0)

        if ns:
            @pl.when(step == nb - 1)
            def _():
                copies = _scatter_copies(*comm)
                for cp in copies:
                    cp.wait_recv()
                for cp in copies:
                    cp.wait_send()

    qk_spec = pl.BlockSpec((GDN_QK_HEADS, tb, GDN_HD), lambda i: (0, nb - 1 - i, 0))
    v_spec = pl.BlockSpec((GDN_V_HEADS, tb, GDN_HD), lambda i: (0, nb - 1 - i, 0))
    g_spec = pl.BlockSpec((nc, GDN_V_HEADS, 1, GDN_CHUNK), lambda i: (nb - 1 - i, 0, 0, 0))
    s_spec = pl.BlockSpec((nc, GDN_V_HEADS, GDN_HD, GDN_HD), lambda i: (nb - 1 - i, 0, 0, 0))
    t_spec = pl.BlockSpec((nc, GDN_V_HEADS, GDN_CHUNK, GDN_CHUNK), lambda i: (nb - 1 - i, 0, 0, 0))
    anywhere = pl.BlockSpec(memory_space=pl.ANY)
    return pl.pallas_call(
        body, name="gdn_bwd", grid=(nb,),
        in_specs=[qk_spec, qk_spec, v_spec, g_spec, g_spec, s_spec, t_spec, v_spec] + [anywhere] * ns,
        out_specs=[qk_spec, qk_spec, v_spec, g_spec, g_spec] + [anywhere] * ns,
        out_shape=[jax.ShapeDtypeStruct((GDN_QK_HEADS, T, GDN_HD), F32),
                   jax.ShapeDtypeStruct((GDN_QK_HEADS, T, GDN_HD), F32),
                   jax.ShapeDtypeStruct((GDN_V_HEADS, T, GDN_HD), F32),
                   jax.ShapeDtypeStruct(g.shape, F32), jax.ShapeDtypeStruct(g.shape, F32)]
        + _scatter_shapes(scatter),
        scratch_shapes=[pltpu.VMEM((GDN_V_HEADS, GDN_HD, GDN_HD), F32)]
        + ([pltpu.SemaphoreType.DMA((3 * ns,)), pltpu.SemaphoreType.DMA((3 * ns,))] if ns else []),
        compiler_params=_params(("arbitrary",)),
    )(qn, kn, v, g, beta, sall, tall, do, *scatter)


def _gnorm_core(o, z, w):
    return tuple(_rms_core(oh, w) * _silu(zh) for oh, zh in zip(o, z))


def _gnorm_fwd(o, pm, w, tm=256):
    T = pm.shape[0]
    tm = min(tm, T)

    def fn(i, ov, zv, wv):
        zf = zv.astype(F32)
        out = _gnorm_core(tuple(ov[h].astype(F32) for h in range(GDN_V_HEADS)), _heads(zf, GDN_V_HEADS), wv)
        return (jnp.concatenate(out, axis=1),)

    ins = [(o, "row", None), (pm, "row", (GDN_V, 2)), (w, "full", None)]
    return _rowcall("gnorm_fwd", fn, T, tm, ins, [((T, GDN_V), BF16, "row")])[0]


def _gnorm_bwd(o, pm, w, don, tm=128):
    T = pm.shape[0]
    tm = min(tm, T)

    def fn(i, ov, zv, wv, dv):
        zf, df = zv.astype(F32), dv.astype(F32)
        _, vjp = jax.vjp(_gnorm_core, tuple(ov[h].astype(F32) for h in range(GDN_V_HEADS)),
                         _heads(zf, GDN_V_HEADS), wv)
        do, dz, dw = vjp(_heads(df, GDN_V_HEADS))
        return jnp.stack(do), jnp.concatenate(dz, axis=1), dw

    ins = [(o, "row", None), (pm, "row", (GDN_V, 2)), (w, "full", None), (don, "row", None)]
    outs = [((GDN_V_HEADS, T, GDN_HD), BF16, "row"), ((T, GDN_V), BF16, "row"), ((1, GDN_HD), F32, "acc")]
    return _rowcall("gnorm_bwd", fn, T, tm, ins, outs)


def _ffn_act_fwd(name, up, conv_w, conv_b, tm=128):
    T = up.shape[0]
    tm = min(tm, T)

    def fn(i, x, halo, cw, cb):
        u = _conv_fwd(_taps(x.astype(F32), halo.astype(F32), 3, i), cw) + cb
        return (_silu(u[:, :DFF]) * u[:, DFF:],)

    ins = [(up, "row", None), (up, "prev", None), (conv_w, "full", None), (conv_b, "full", None)]
    return _rowcall(name, fn, T, tm, ins, [((T, DFF), BF16, "row")])[0]


def _ffn_act_bwd(name, up, conv_w, conv_b, dact, tm=128):
    T = up.shape[0]
    tm = min(tm, T)

    def fn(i, x, halo, cw, cb, da):
        taps = _taps(x.astype(F32), halo.astype(F32), 3, i)
        da = da.astype(F32)
        u = _conv_fwd(taps, cw) + cb
        gate, val = u[:, :DFF], u[:, DFF:]
        sg = _sigmoid(gate)
        dgate = da * val * sg * (1.0 + gate * (1.0 - sg))
        dval = da * gate * sg
        du = jnp.concatenate([dgate, dval], axis=1)
        return du, _conv_dw(du, taps), jnp.sum(du, axis=0, keepdims=True)

    ins = [(up, "row", None), (up, "prev", None), (conv_w, "full", None), (conv_b, "full", None),
           (dact, "row", None)]
    outs = [((T, DFF2), BF16, "row"), ((SUBLANE, DFF2), F32, "acc"), ((1, DFF2), F32, "acc")]
    return _rowcall(name, fn, T, tm, ins, outs)


def _ffn_conv_bwd(name, du, conv_w, tm=256):
    T = du.shape[0]
    tm = min(tm, T)
    n = T // tm

    def fn(i, dv, halo, cw):
        return (_conv_dx(dv.astype(F32), halo.astype(F32), cw, i, n),)

    ins = [(du, "row", None), (du, "next", None), (conv_w, "full", None)]
    return _rowcall(name, fn, T, tm, ins, [((T, DFF2), BF16, "row")])[0]


GROUP_ROWS = SWA_GROUP * SWA_BLOCK


def _attn_core(q, kp, kc, vp, vc, bias, sink, mask):
    kcat = jnp.concatenate([kp, kc], axis=0)
    vcat = jnp.concatenate([vp, vc], axis=0)
    s = _bdot(q * (SWA_HD ** -0.5), kcat, NT) + bias
    s = jnp.where(mask, s, NEG_INF)
    m = lax.stop_gradient(jnp.maximum(jnp.max(s, axis=-1, keepdims=True), sink))
    p = jnp.exp(s - m)
    denom = jnp.sum(p, axis=-1, keepdims=True) + jnp.exp(sink - m)
    return _bdot(p / denom, vcat)


def _attn_mask(i):
    qi = lax.broadcasted_iota(jnp.int32, (GROUP_ROWS, 2 * SWA_BLOCK), 0) & (SWA_BLOCK - 1)
    ki = lax.broadcasted_iota(jnp.int32, (GROUP_ROWS, 2 * SWA_BLOCK), 1)
    dist = qi + SWA_BLOCK - ki
    return (dist >= 0) & (dist < SWA_BLOCK) & ((ki >= SWA_BLOCK) | (i > 0))


def _attn_operands(j, q_ref, kc_ref, kp_ref, vc_ref, vp_ref, b_ref, s_ref):
    heads = slice(SWA_GROUP * j, SWA_GROUP * (j + 1))
    sink = jnp.concatenate([jnp.broadcast_to(s_ref[j, g:g + 1, 0:1], (SWA_BLOCK, 1)) for g in range(SWA_GROUP)],
                           axis=0)
    return (q_ref[heads].astype(F32).reshape(GROUP_ROWS, SWA_HD), kp_ref[j].astype(F32), kc_ref[j].astype(F32),
            vp_ref[j].astype(F32), vc_ref[j].astype(F32), b_ref[heads].reshape(GROUP_ROWS, 2 * SWA_BLOCK), sink)


def _attn_fwd(q, k, v, bias, sinks):
    T = q.shape[1]
    nb = T // SWA_BLOCK

    def body(q_ref, kc_ref, kp_ref, vc_ref, vp_ref, b_ref, s_ref, o_ref):
        mask = _attn_mask(pl.program_id(0))
        for j in range(SWA_KV_HEADS):
            out = _attn_core(*_attn_operands(j, q_ref, kc_ref, kp_ref, vc_ref, vp_ref, b_ref, s_ref), mask)
            o_ref[SWA_GROUP * j:SWA_GROUP * (j + 1)] = out.reshape(SWA_GROUP, SWA_BLOCK, SWA_HD).astype(o_ref.dtype)

    q_spec = pl.BlockSpec((SWA_Q_HEADS, SWA_BLOCK, SWA_HD), lambda i: (0, i, 0))
    cur = pl.BlockSpec((SWA_KV_HEADS, SWA_BLOCK, SWA_HD), lambda i: (0, i, 0))
    prev = pl.BlockSpec((SWA_KV_HEADS, SWA_BLOCK, SWA_HD), lambda i: (0, jnp.maximum(i - 1, 0), 0))
    return pl.pallas_call(
        body, name="attn_fwd", grid=(nb,),
        in_specs=[q_spec, cur, prev, cur, prev, pl.BlockSpec(bias.shape, lambda i: (0, 0, 0)),
                  pl.BlockSpec(sinks.shape, lambda i: (0, 0, 0))],
        out_specs=q_spec, out_shape=jax.ShapeDtypeStruct(q.shape, BF16),
        compiler_params=_params(("arbitrary",)),
    )(q, k, k, v, v, bias, sinks)


def _attn_bwd(q, k, v, bias, sinks, do):
    T = q.shape[1]
    nb = T // SWA_BLOCK

    def body(q_ref, kc_ref, kp_ref, vc_ref, vp_ref, b_ref, s_ref, do_ref,
             dq_ref, dk_ref, dv_ref, db_ref, dsk_ref, kcar, vcar):
        i = pl.program_id(0)

        @pl.when(i < nb)
        def _():
            mask = _attn_mask(i)
            for j in range(SWA_KV_HEADS):
                heads = slice(SWA_GROUP * j, SWA_GROUP * (j + 1))
                prim = _attn_operands(j, q_ref, kc_ref, kp_ref, vc_ref, vp_ref, b_ref, s_ref)
                _, vjp = jax.vjp(functools.partial(_attn_core, mask=mask), *prim)
                dq, dkp, dkc, dvp, dvc, db, dsc = vjp(do_ref[heads].astype(F32).reshape(GROUP_ROWS, SWA_HD))
                dq_ref[heads] = dq.reshape(SWA_GROUP, SWA_BLOCK, SWA_HD).astype(dq_ref.dtype)
                db = db.reshape(SWA_GROUP, SWA_BLOCK, 2 * SWA_BLOCK)
                dsk = jnp.concatenate(
                    [jnp.broadcast_to(jnp.sum(dsc[g * SWA_BLOCK:(g + 1) * SWA_BLOCK], axis=0, keepdims=True),
                                      (1, LANE)) for g in range(SWA_GROUP)], axis=0)

                @pl.when(i == 0)
                def _():
                    db_ref[heads] = db
                    dsk_ref[j] = dsk

                @pl.when(i > 0)
                def _():
                    db_ref[heads] += db
                    dsk_ref[j] += dsk
                    dk_ref[j] = (kcar[j] + dkp).astype(dk_ref.dtype)
                    dv_ref[j] = (vcar[j] + dvp).astype(dv_ref.dtype)

                kcar[j] = dkc
                vcar[j] = dvc

        @pl.when(i == nb)
        def _():
            dk_ref[...] = kcar[...].astype(dk_ref.dtype)
            dv_ref[...] = vcar[...].astype(dv_ref.dtype)

    last = nb - 1
    q_spec = pl.BlockSpec((SWA_Q_HEADS, SWA_BLOCK, SWA_HD), lambda i: (0, jnp.minimum(i, last), 0))
    cur = pl.BlockSpec((SWA_KV_HEADS, SWA_BLOCK, SWA_HD), lambda i: (0, jnp.minimum(i, last), 0))
    prev = pl.BlockSpec((SWA_KV_HEADS, SWA_BLOCK, SWA_HD), lambda i: (0, jnp.clip(i - 1, 0, last), 0))
    b_spec = pl.BlockSpec(bias.shape, lambda i: (0, 0, 0))
    s_spec = pl.BlockSpec(sinks.shape, lambda i: (0, 0, 0))
    carry = pltpu.VMEM((SWA_KV_HEADS, SWA_BLOCK, SWA_HD), F32)
    return pl.pallas_call(
        body, name="attn_bwd", grid=(nb + 1,),
        in_specs=[q_spec, cur, prev, cur, prev, b_spec, s_spec, q_spec],
        out_specs=[q_spec, prev, prev, b_spec, s_spec],
        out_shape=[jax.ShapeDtypeStruct(q.shape, BF16), jax.ShapeDtypeStruct(k.shape, BF16),
                   jax.ShapeDtypeStruct(k.shape, BF16), jax.ShapeDtypeStruct(bias.shape, F32),
                   jax.ShapeDtypeStruct(sinks.shape, F32)],
        scratch_shapes=[carry, carry],
        compiler_params=_params(("arbitrary",)),
    )(q, k, k, v, v, bias, sinks, do)


def _rel_onehot():
    qi = jnp.arange(SWA_BLOCK)[:, None]
    ki = jnp.arange(2 * SWA_BLOCK)[None, :]
    n = jnp.maximum(qi + SWA_BLOCK - ki, 0)
    max_exact = REL_BUCKETS // 2
    nf = jnp.maximum(n, 1).astype(F32)
    large = max_exact + (jnp.log(nf / max_exact) / math.log(REL_MAX_DISTANCE / max_exact)
                         * (REL_BUCKETS - max_exact)).astype(jnp.int32)
    bucket = jnp.where(n < max_exact, n, jnp.minimum(large, REL_BUCKETS - 1)).reshape(-1)
    return (bucket[None, :] == jnp.arange(REL_BUCKETS)[:, None]).astype(F32)


def _final(h, w, target, tm=256):
    T = h.shape[0]
    tm = min(tm, T)

    def fn(i, hv, wv, tv):
        y, vjp = jax.vjp(_rms_core, hv, wv)
        err = y - tv
        dh, dw = vjp(err * (1.0 / D))
        part = 0.5 * jnp.sum(jnp.sum(err * err, axis=1, keepdims=True) * (1.0 / D), axis=0, keepdims=True)
        return jnp.broadcast_to(part, (SUBLANE, LANE)), dh, dw

    ins = [(h, "row", None), (w, "full", None), (target, "row", None)]
    outs = [((SUBLANE, LANE), F32, "acc"), ((T, D), F32, "row"), ((1, D), F32, "acc")]
    return _rowcall("final", fn, T, tm, ins, outs)


def _heads_major(a, heads, hd):
    return a.reshape(a.shape[0], heads, hd).transpose(1, 0, 2)


def _heads_minor(a):
    return a.transpose(1, 0, 2).reshape(a.shape[1], a.shape[0] * a.shape[2])


def _ffn_fwd(tag, h, P, layer):
    n = _rms_fwd(f"{tag}_rms", h, P["ffn_norm_w"][layer:layer + 1])
    up = _mm_up(f"{tag}_up", n, P["w_up"], layer)
    act = _ffn_act_fwd(f"{tag}_act", up, P["ffn_conv_w"][layer], P["ffn_conv_b"][layer:layer + 1])
    out = _mm_nn(f"{tag}_down", act, P["w_down"][layer], F32, res=h)
    return out, (n, up, act)


def _ffn_bwd(tag, h, saved, dout, P, layer, into=(None, None)):
    n, up, act = saved
    cw, cb = P["ffn_conv_w"][layer], P["ffn_conv_b"][layer:layer + 1]
    dact = _mm_nt(f"{tag}_down_dx", dout, P["w_down"][layer], BF16)
    g_down = _mm_down_tn(f"{tag}_down_dw", act, dout, layer, into[1])
    du, dcw, dcb = _ffn_act_bwd(f"{tag}_act_bwd", up, cw, cb, dact)
    dup = _ffn_conv_bwd(f"{tag}_conv_bwd", du, cw)
    g_up = _mm_up_tn(f"{tag}_up_dw", n, dup, layer, into[0])
    dn = _mm_up_nt(f"{tag}_up_dx", dup, P["w_up"], layer)
    dh, dnw = _rms_bwd(f"{tag}_rms_bwd", h, [(P["ffn_norm_w"][layer:layer + 1], dn)], [dout])
    return dh, dict(w_down=g_down, w_up=g_up, conv_w=dcw[:3], conv_b=dcb, norm_w=dnw)


def _local_step(x, target, P, late=None, pair_sums=None):
    T = x.shape[0]
    n0 = _rms_fwd("a_rms", x, P["a_norm_w"])
    pm = _mm_nn("gdn_in", n0, P["w_in_main"], BF16)
    pba = _mm_nn("gdn_in_ba", n0, P["w_in_ba"], F32)
    qn, kn, v, beta, g = _gdn_pre_fwd(pm, pba, P["a_conv_w"], P["a_log"], P["dt_bias"])
    g_rows, beta_rows = _gate_rows(g), _gate_rows(beta)
    o, sall, tall, *gathered = _gdn_fwd(qn, kn, v, g_rows, beta_rows, gather=late)
    if late is not None:
        P = {**P, **_late_weights(gathered)}
    on = _gnorm_fwd(o, pm, P["a_out_norm_w"])
    h1 = _mm_nn("gdn_out", on, P["w_out"], F32, res=x)
    h2, ffn0 = _ffn_fwd("ffn0", h1, P, 0)
    nkv = _rms_fwd("kv_rms", h2, P["kv_norm_w"])
    kv = _mm_nn("kv_proj", nkv, P["w_kv"], BF16)
    nb = _rms_fwd("b_rms", h2, P["b_norm_w"])
    qp = _mm_nn("q_proj", nb, P["w_q"], BF16)
    q3 = _heads_major(qp, SWA_Q_HEADS, SWA_HD)
    k3 = _heads_major(kv[:, :SWA_KV_HEADS * SWA_HD], SWA_KV_HEADS, SWA_HD)
    v3 = _heads_major(kv[:, SWA_KV_HEADS * SWA_HD:], SWA_KV_HEADS, SWA_HD)
    onehot = _rel_onehot()
    bias = _mm_nn("rel_bias", P["rel_table_t"], onehot, F32, precision=HIGHEST)
    bias = bias.reshape(SWA_Q_HEADS, SWA_BLOCK, 2 * SWA_BLOCK)
    oa = _heads_minor(_attn_fwd(q3, k3, v3, bias, P["sinks"]))
    h3 = _mm_nn("o_proj", oa, P["w_o"], F32, res=h2)
    h4, ffn1 = _ffn_fwd("ffn1", h3, P, 1)
    loss, dh4, d_final = _final(h4, P["final_norm_w"], target)

    dh3, gf1 = _ffn_bwd("ffn1", h3, ffn1, dh4, P, 1)
    doa = _mm_nt("o_proj_dx", dh3, P["w_o"], BF16)
    g_wo = _mm_tn("o_proj_dw", oa, dh3)
    dq3, dk3, dv3, dbias, dsinks = _attn_bwd(q3, k3, v3, bias, P["sinks"], _heads_major(doa, SWA_Q_HEADS, SWA_HD))
    dqp = _heads_minor(dq3)
    dkv = jnp.concatenate([_heads_minor(dk3), _heads_minor(dv3)], axis=1)
    g_wq = _mm_tn("q_proj_dw", nb, dqp)
    dnb = _mm_nt("q_proj_dx", dqp, P["w_q"], F32)
    g_wkv = _mm_tn("kv_proj_dw", nkv, dkv)
    dnkv = _mm_nt("kv_proj_dx", dkv, P["w_kv"], F32)
    dh2, d_bnorm, d_kvnorm = _rms_bwd("b_kv_rms_bwd", h2, [(P["b_norm_w"], dnb), (P["kv_norm_w"], dnkv)], [dh3])
    g_table = _mm_nt("rel_bias_dw", onehot, dbias.reshape(SWA_Q_HEADS, -1), F32, precision=HIGHEST)
    dh1, gf0 = _ffn_bwd("ffn0", h1, ffn0, dh2, P, 0, into=(gf1["w_up"], gf1["w_down"]))
    don = _mm_nt("gdn_out_dx", dh1, P["w_out"], BF16)
    g_wout = _mm_tn("gdn_out_dw", on, dh1)
    do, dz, d_gnorm = _gnorm_bwd(o, pm, P["a_out_norm_w"], don)
    ready = dict(a_w_out=g_wout, w_kv=g_wkv, b_w_q=g_wq, b_w_o=g_wo, ffn_w_up=gf0["w_up"], ffn_w_down=gf0["w_down"])
    pairs = pair_sums(ready, "early") if pair_sums is not None else []
    dq, dk, dv, dg, dbeta, *parts = _gdn_bwd(qn, kn, v, g_rows, beta_rows, sall, tall, do, scatter=pairs)
    dy, dpba, d_aconv, d_alog, d_dtb = _gdn_pre_bwd(pm, pba, P["a_conv_w"], P["a_log"], P["dt_bias"],
                                                    dq, dk, dv, _gate_cols(dbeta), _gate_cols(dg))
    dpm = _gdn_conv_bwd(dy, dz, P["a_conv_w"])
    g_win_main = _mm_tn("gdn_in_dw", n0, dpm)
    g_win_ba = _mm_tn("gdn_in_ba_dw", n0, dpba)
    dn0 = _mm_nt("gdn_in_dx", dpm, P["w_in_main"], F32)
    dn0 = _mm_nt("gdn_in_ba_dx", dpba, P["w_in_ba"], F32, res=dn0)
    dx, d_anorm = _rms_bwd("a_rms_bwd", x, [(P["a_norm_w"], dn0)], [dh1])

    nh = GDN_V_HEADS
    grads = dict(
        a_norm_w=d_anorm,
        a_w_in=jnp.concatenate([g_win_main, g_win_ba[:, :nh], g_win_ba[:, LANE:LANE + nh]], axis=1),
        a_conv_w=d_aconv[:4], a_a_log=d_alog[:, :nh], a_dt_bias=d_dtb[:, :nh], a_out_norm_w=d_gnorm,
        a_w_out=g_wout, kv_norm_w=d_kvnorm, w_kv=g_wkv, b_norm_w=d_bnorm, b_w_q=g_wq,
        b_sinks=dsinks[:, :, 0].reshape(1, SWA_Q_HEADS), b_w_o=g_wo, rel_bias_table=g_table,
        ffn_norm_w=jnp.concatenate([gf0["norm_w"], gf1["norm_w"]], axis=0),
        ffn_w_up=gf0["w_up"],
        ffn_conv_w=jnp.stack([gf0["conv_w"], gf1["conv_w"]], axis=0),
        ffn_conv_b=jnp.concatenate([gf0["conv_b"], gf1["conv_b"]], axis=0),
        ffn_w_down=gf0["w_down"],
        final_norm_w=d_final,
    )
    return loss, dx, grads, dict(zip([n for n in BIG if n in ready], zip(pairs, parts)))


HBM_SPEC = pl.BlockSpec(memory_space=pltpu.HBM)
VMEM_SPEC = pl.BlockSpec(memory_space=pltpu.VMEM)


def _coords():
    return lax.axis_index("x"), lax.axis_index("y"), lax.axis_index("c")


def _remote(src, dst, send_sem, recv_sem, device):
    return pltpu.make_async_remote_copy(src_ref=src, dst_ref=dst, send_sem=send_sem, recv_sem=recv_sem,
                                        device_id=device, device_id_type=MESH)


def _other_chips(x, y):
    return [(1 - x, y), (x, 1 - y), (1 - x, 1 - y)]


def _gather_copies(shapes, split, ins, outs, send_sems, recv_sems):
    x, y, c = _coords()
    p = 2 * x + y
    ici, forwards, from_sibling = [], [], []
    for a, shape in enumerate(shapes):
        h = shape[0] // 2
        for j, chip in enumerate(_other_chips(x, y)):
            q = 2 * chip[0] + chip[1]
            if split[a]:
                mine, theirs = pl.ds(c * h, h), pl.ds((1 - c) * h, h)
                ici.append(_remote(ins[a].at[mine], outs[a].at[p, mine], send_sems.at[6 * a + j],
                                   recv_sems.at[6 * a + j], (*chip, c)))
                land = outs[a].at[q, mine]
                forwards.append(_remote(land, land, send_sems.at[6 * a + 3 + j], recv_sems.at[6 * a + 3 + j],
                                        (x, y, 1 - c)))
                land = outs[a].at[q, theirs]
                from_sibling.append(_remote(land, land, send_sems.at[6 * a + 3 + j], recv_sems.at[6 * a + 3 + j],
                                            (x, y, 1 - c)))
            else:
                ici.append(_remote(ins[a], outs[a].at[p], send_sems.at[6 * a + j], recv_sems.at[6 * a + j],
                                   (*chip, c)))
                forwards.append(None)
    return ici, forwards, from_sibling


def _gather_arrival(shapes, split, ins, outs, send_sems, recv_sems):
    x, y, c = _coords()
    ici, forwards, from_sibling = _gather_copies(shapes, split, ins, outs, send_sems, recv_sems)
    k = 0
    for a, shape in enumerate(shapes):
        h = shape[0] // 2
        for j, chip in enumerate(_other_chips(x, y)):
            q = 2 * chip[0] + chip[1]
            land = outs[a].at[q, pl.ds(c * h, h)] if split[a] else outs[a].at[q]
            _remote(land, land, send_sems.at[6 * a + j], recv_sems.at[6 * a + j], (*chip, c)).wait_recv()
            if forwards[k] is not None:
                forwards[k].start()
            k += 1
    for cp in from_sibling:
        cp.wait_recv()
    for cp in ici + [f for f in forwards if f is not None]:
        cp.wait_send()


def _all_gather(arrs, split, remote):
    n = len(arrs)
    now = [a for a in range(n) if remote[a]]
    shapes = [arrs[a].shape for a in now]
    splits = [split[a] for a in now]

    def body(*refs):
        ins, outs, stage = refs[:n], refs[n:2 * n], refs[2 * n:3 * n]
        send_sems, recv_sems, in_sems, out_sems = refs[3 * n:]
        p = 2 * lax.axis_index("x") + lax.axis_index("y")
        gathered = ([ins[a] for a in now], [outs[a] for a in now], send_sems, recv_sems)
        loads = [pltpu.make_async_copy(ins[a], stage[a], in_sems.at[a]) for a in range(n)]
        for cp in loads:
            cp.start()
        for cp in _gather_copies(shapes, splits, *gathered)[0]:
            cp.start()
        stores = [pltpu.make_async_copy(stage[a], outs[a].at[p], out_sems.at[a]) for a in range(n)]
        for a in range(n):
            loads[a].wait()
            stores[a].start()
        _gather_arrival(shapes, splits, *gathered)
        for cp in stores:
            cp.wait()

    return pl.pallas_call(
        body, name="weights_all_gather", in_specs=[HBM_SPEC] * n, out_specs=[HBM_SPEC] * n,
        out_shape=[jax.ShapeDtypeStruct((N_CHIPS,) + a.shape, a.dtype) for a in arrs],
        scratch_shapes=[pltpu.VMEM(a.shape, a.dtype) for a in arrs]
        + [pltpu.SemaphoreType.DMA((6 * len(now),)), pltpu.SemaphoreType.DMA((6 * len(now),)),
           pltpu.SemaphoreType.DMA((n,)), pltpu.SemaphoreType.DMA((n,))],
        compiler_params=pltpu.CompilerParams(vmem_limit_bytes=VMEM_LIMIT),
    )(*arrs)


PAIR_SWAP_PIECES = 2


def _pair_swap(gs, tag):
    n = len(gs)

    def body(*refs):
        ins, other = refs[:n], refs[n:2 * n]
        send_sems, recv_sems = refs[2 * n:]
        x, y, c = _coords()
        cps = []
        for a in range(n):
            h = gs[a].shape[1] // 2
            piece = h // PAIR_SWAP_PIECES
            for q in range(N_CHIPS):
                for r in range(PAIR_SWAP_PIECES):
                    k = (a * N_CHIPS + q) * PAIR_SWAP_PIECES + r
                    cp = _remote(ins[a].at[q, pl.ds((1 - c) * h + r * piece, piece)],
                                 other[a].at[q, pl.ds(r * piece, piece)], send_sems.at[k], recv_sems.at[k],
                                 (x, y, 1 - c))
                    cp.start()
                    cps.append(cp)
        for cp in cps:
            cp.wait()

    half = [jax.ShapeDtypeStruct((N_CHIPS, g.shape[1] // 2, g.shape[2]), g.dtype) for g in gs]
    nsem = n * N_CHIPS * PAIR_SWAP_PIECES
    return pl.pallas_call(
        body, name=f"grads_pair_swap_{tag}", in_specs=[HBM_SPEC] * n, out_specs=[HBM_SPEC] * n, out_shape=half,
        scratch_shapes=[pltpu.SemaphoreType.DMA((nsem,)), pltpu.SemaphoreType.DMA((nsem,))],
    )(*gs)


def _scatter_copies(ins, outs, send_sems, recv_sems):
    x, y, c = _coords()
    copies = []
    for a in range(len(ins)):
        for j, chip in enumerate(_other_chips(x, y)):
            q = 2 * chip[0] + chip[1]
            copies.append(_remote(ins[a].at[q], outs[a].at[j], send_sems.at[3 * a + j], recv_sems.at[3 * a + j],
                                  (*chip, c)))
    return copies


def _scatter_shapes(ps):
    return [jax.ShapeDtypeStruct((N_CHIPS - 1,) + a.shape[1:], a.dtype) for a in ps]


def _chip_scatter(ps, tag):
    n = len(ps)

    def body(*refs):
        copies = _scatter_copies(refs[:n], refs[n:2 * n], *refs[2 * n:])
        for cp in copies:
            cp.start()
        for cp in copies:
            cp.wait_recv()
        for cp in copies:
            cp.wait_send()

    return pl.pallas_call(
        body, name=f"grads_chip_scatter_{tag}", in_specs=[HBM_SPEC] * n, out_specs=[HBM_SPEC] * n,
        out_shape=_scatter_shapes(ps),
        scratch_shapes=[pltpu.SemaphoreType.DMA((3 * n,)), pltpu.SemaphoreType.DMA((3 * n,))],
    )(*ps)


def _pair_share(rs):
    n = len(rs)

    def body(*refs):
        ins, outs, stage = refs[:n], refs[n:2 * n], refs[2 * n:3 * n]
        send_sems, recv_sems, in_sems, out_sems = refs[3 * n:]
        x, y, c = _coords()

        def mine(a):
            h = rs[a].shape[0]
            return outs[a].at[pl.ds(c * h, h)]

        loads = [pltpu.make_async_copy(ins[a], stage[a], in_sems.at[a]) for a in range(n)]
        for cp in loads:
            cp.start()
        sends = [_remote(ins[a], mine(a), send_sems.at[a], recv_sems.at[a], (x, y, 1 - c)) for a in range(n)]
        for cp in sends:
            cp.start()
        stores = [pltpu.make_async_copy(stage[a], mine(a), out_sems.at[a]) for a in range(n)]
        for a in range(n):
            loads[a].wait()
            stores[a].start()
        for a in range(n):
            h = rs[a].shape[0]
            land = outs[a].at[pl.ds((1 - c) * h, h)]
            _remote(land, land, send_sems.at[a], recv_sems.at[a], (x, y, 1 - c)).wait_recv()
        for cp in sends:
            cp.wait_send()
        for cp in stores:
            cp.wait()

    return pl.pallas_call(
        body, name="grads_pair_share", in_specs=[HBM_SPEC] * n, out_specs=[HBM_SPEC] * n,
        out_shape=[jax.ShapeDtypeStruct((2 * a.shape[0], a.shape[1]), a.dtype) for a in rs],
        scratch_shapes=[pltpu.VMEM(a.shape, a.dtype) for a in rs] + [pltpu.SemaphoreType.DMA((n,))] * 4,
        compiler_params=pltpu.CompilerParams(vmem_limit_bytes=VMEM_LIMIT),
    )(*rs)


def _small_all_reduce(buf):
    R = buf.shape[0]
    ndev = 2 * N_CHIPS

    def body(in_ref, out_ref, gath, send_sems, recv_sems):
        x, y, c = _coords()
        me = 4 * x + 2 * y + c
        gath[me] = in_ref[...]
        peers = []
        for d in range(1, ndev):
            px = 1 - x if d & 4 else x
            py = 1 - y if d & 2 else y
            pc = 1 - c if d & 1 else c
            peers.append((px, py, pc))
        sends = []
        for d, peer in enumerate(peers):
            cp = _remote(in_ref, gath.at[me], send_sems.at[d], recv_sems.at[d], peer)
            cp.start()
            sends.append(cp)
        for d, peer in enumerate(peers):
            land = gath.at[4 * peer[0] + 2 * peer[1] + peer[2]]
            _remote(land, land, send_sems.at[d], recv_sems.at[d], peer).wait_recv()
        for cp in sends:
            cp.wait_send()
        acc = gath[0]
        for s in range(1, ndev):
            acc = acc + gath[s]
        out_ref[...] = acc

    return pl.pallas_call(
        body, name="small_all_reduce", in_specs=[VMEM_SPEC], out_specs=VMEM_SPEC,
        out_shape=jax.ShapeDtypeStruct(buf.shape, F32),
        scratch_shapes=[pltpu.VMEM((ndev, R, LANE), F32), pltpu.SemaphoreType.DMA((ndev - 1,)),
                        pltpu.SemaphoreType.DMA((ndev - 1,))],
    )(buf)


def _pair_add(name, own, other):
    h = own.shape[1]
    tm = _tile(h, (128, 64, 32, 16))

    def fn(i, a, b):
        return (a + b,)

    return _rowcall(name, fn, h, tm, [(own, "row", None), (other, "row", None)], [(own.shape, BF16, "row")])[0]


def _chip_add(name, own, parts):
    h = parts.shape[1]
    tm = _tile(h, (128, 64, 32, 16))

    def fn(i, o, a):
        a = a.astype(F32)
        return (((o.astype(F32) + a[0]) + a[1]) + a[2],)

    return _rowcall(name, fn, h, tm, [(own, "row", None), (parts, "row", None)], [(parts.shape[1:], F32, "row")])[0]


def _adamw(name, w, g, m, v):
    R = w.shape[0]
    tm = _tile(R, (256, 128, 64, 32, 16, 8))

    def fn(i, wv, gv, mv, vv):
        m2 = ADAM_B1 * mv + (1.0 - ADAM_B1) * gv
        v2 = ADAM_B2 * vv + (1.0 - ADAM_B2) * (gv * gv)
        m_hat = m2 / (1.0 - ADAM_B1 ** ADAM_STEP)
        v_hat = v2 / (1.0 - ADAM_B2 ** ADAM_STEP)
        delta = -ADAM_LR * (m_hat / (jnp.sqrt(v_hat) + ADAM_EPS) + ADAM_WD * wv)
        return delta, m2, v2

    ins = [(a, "row", None) for a in (w, g, m, v)]
    return _rowcall(name, fn, R, tm, ins, [(w.shape, F32, "row")] * 3)


def _pack(arrs):
    flat = jnp.concatenate([a.reshape(-1).astype(F32) for a in arrs])
    size = flat.shape[0]
    padded = -(-size // (SUBLANE * LANE)) * SUBLANE * LANE
    return jnp.pad(flat, (0, padded - size)).reshape(-1, LANE)


def _unpack(buf, shapes):
    flat = buf.reshape(-1)
    out, off = [], 0
    for s in shapes:
        size = math.prod(s)
        out.append(flat[off:off + size].reshape(s))
        off += size
    return out


BIG = ("a_w_in", "a_w_out", "w_kv", "b_w_q", "b_w_o", "ffn_w_up", "ffn_w_down")
WEIGHTS = ("a_norm_w", "a_w_in", "a_conv_w", "a_a_log", "a_dt_bias", "a_out_norm_w", "a_w_out", "kv_norm_w", "w_kv",
           "b_norm_w", "b_w_q", "b_sinks", "b_w_o", "rel_bias_table", "ffn_norm_w", "ffn_w_up", "ffn_conv_w",
           "ffn_conv_b", "ffn_w_down", "final_norm_w")
SMALL = tuple(n for n in WEIGHTS if n not in BIG)
SMALL_SHARDED = {"a_norm_w": 1, "a_conv_w": 2, "ffn_conv_w": 2}


def _quarter_2d(name, a):
    if name in ("ffn_w_up", "ffn_w_down"):
        return a.reshape(a.shape[0] * a.shape[1], a.shape[2])
    return a.reshape(a.shape[-2], a.shape[-1])


def _whole_weights(w):
    bigs = [_quarter_2d(n, w[n]).astype(BF16) for n in BIG]
    smalls = [w["a_norm_w"], w["a_conv_w"][0], w["ffn_conv_w"].reshape(6, DFF2_SHARD)]
    remote = [True] + [False] * (len(bigs) - 1) + [True] * len(smalls)
    g = _all_gather(bigs + smalls, [True] * len(bigs) + [False] * len(smalls), remote)
    w_in = g[0].transpose(1, 0, 2).reshape(D, GDN_IN)
    nh = GDN_V_HEADS
    zpad = jnp.zeros((D, LANE - nh), BF16)
    w_in_ba = jnp.concatenate([w_in[:, GDN_MAIN:GDN_MAIN + nh], zpad, w_in[:, GDN_MAIN + nh:], zpad], axis=1)
    lane_pad = lambda a: jnp.pad(a, ((0, 0), (0, LANE - nh)))
    early = dict(
        a_norm_w=g[7].reshape(1, D), w_in_main=w_in[:, :GDN_MAIN], w_in_ba=w_in_ba,
        a_conv_w=g[8].transpose(1, 0, 2).reshape(4, GDN_CONV), a_log=lane_pad(w["a_a_log"]),
        dt_bias=lane_pad(w["a_dt_bias"]), a_out_norm_w=w["a_out_norm_w"],
        kv_norm_w=w["kv_norm_w"].reshape(1, D), b_norm_w=w["b_norm_w"],
        sinks=jnp.broadcast_to(w["b_sinks"].reshape(SWA_KV_HEADS, SWA_GROUP, 1), (SWA_KV_HEADS, SWA_GROUP, LANE)),
        rel_table_t=w["rel_bias_table"].T, ffn_norm_w=w["ffn_norm_w"],
        ffn_conv_w=g[9].reshape(N_CHIPS, 2, 3, DFF2_SHARD).transpose(1, 2, 0, 3).reshape(2, 3, DFF2),
        ffn_conv_b=w["ffn_conv_b"], final_norm_w=w["final_norm_w"].reshape(1, D),
    )
    return early, (bigs[1:], g[1:len(bigs)])


def _late_weights(g):
    return dict(
        w_out=g[0].reshape(GDN_V, D), w_kv=g[1].reshape(D, 2 * SWA_KV_HEADS * SWA_HD), w_q=g[2].reshape(D, D),
        w_o=g[3].reshape(D, D), w_up=g[4].reshape(N_CHIPS, 2, D, DFF2_SHARD),
        w_down=g[5].reshape(N_CHIPS, 2, DFF_SHARD, D).transpose(1, 0, 2, 3).reshape(2, DFF, D),
    )


def _chip_major(name, g):
    if name == "a_w_in":
        return g.reshape(D, N_CHIPS, GDN_IN_SHARD).transpose(1, 0, 2)
    if name == "ffn_w_up":
        return g.reshape(N_CHIPS, 2 * D, DFF2_SHARD)
    if name == "ffn_w_down":
        return g.reshape(N_CHIPS, 2 * DFF_SHARD, D)
    return g.reshape(N_CHIPS, g.shape[0] // N_CHIPS, g.shape[1])


def kernel(x, a_norm_w, a_w_in, a_conv_w, a_a_log, a_dt_bias, a_out_norm_w, a_w_out, kv_norm_w, w_kv, b_norm_w, b_w_q, b_sinks, b_w_o, rel_bias_table, ffn_norm_w, ffn_w_up, ffn_conv_w, ffn_conv_b, ffn_w_down, final_norm_w, loss_target, m_a_norm_w, m_a_w_in, m_a_conv_w, m_a_a_log, m_a_dt_bias, m_a_out_norm_w, m_a_w_out, m_kv_norm_w, m_w_kv, m_b_norm_w, m_b_w_q, m_b_sinks, m_b_w_o, m_rel_bias_table, m_ffn_norm_w, m_ffn_w_up, m_ffn_conv_w, m_ffn_conv_b, m_ffn_w_down, m_final_norm_w, v_a_norm_w, v_a_w_in, v_a_conv_w, v_a_a_log, v_a_dt_bias, v_a_out_norm_w, v_a_w_out, v_kv_norm_w, v_w_kv, v_b_norm_w, v_b_w_q, v_b_sinks, v_b_w_o, v_rel_bias_table, v_ffn_norm_w, v_ffn_w_up, v_ffn_conv_w, v_ffn_conv_b, v_ffn_w_down, v_final_norm_w):
    w = dict(zip(WEIGHTS, (a_norm_w, a_w_in, a_conv_w, a_a_log, a_dt_bias, a_out_norm_w, a_w_out, kv_norm_w, w_kv,
                           b_norm_w, b_w_q, b_sinks, b_w_o, rel_bias_table, ffn_norm_w, ffn_w_up, ffn_conv_w,
                           ffn_conv_b, ffn_w_down, final_norm_w)))
    m = dict(zip(WEIGHTS, (m_a_norm_w, m_a_w_in, m_a_conv_w, m_a_a_log, m_a_dt_bias, m_a_out_norm_w, m_a_w_out,
                           m_kv_norm_w, m_w_kv, m_b_norm_w, m_b_w_q, m_b_sinks, m_b_w_o, m_rel_bias_table,
                           m_ffn_norm_w, m_ffn_w_up, m_ffn_conv_w, m_ffn_conv_b, m_ffn_w_down, m_final_norm_w)))
    v = dict(zip(WEIGHTS, (v_a_norm_w, v_a_w_in, v_a_conv_w, v_a_a_log, v_a_dt_bias, v_a_out_norm_w, v_a_w_out,
                           v_kv_norm_w, v_w_kv, v_b_norm_w, v_b_w_q, v_b_sinks, v_b_w_o, v_rel_bias_table,
                           v_ffn_norm_w, v_ffn_w_up, v_ffn_conv_w, v_ffn_conv_b, v_ffn_w_down, v_final_norm_w)))
    T = x.shape[1]
    chip = 2 * lax.axis_index("x") + lax.axis_index("y")

    core = lax.axis_index("c")

    def pair_sums(named, tag):
        names = [n for n in BIG if n in named]
        whole = [_chip_major(n, named[n]) for n in names]
        other = _pair_swap(whole, tag)
        own = [lax.dynamic_slice_in_dim(g, core * (g.shape[1] // 2), g.shape[1] // 2, 1) for g in whole]
        return [_pair_add(f"pair_add_{n}", a, b) for n, a, b in zip(names, own, other)]

    early, late = _whole_weights(w)
    loss_part, dx, grads, scattered = _local_step(x.reshape(T, D), loss_target.reshape(T, D), early, late, pair_sums)

    rest = [n for n in BIG if n not in scattered]
    pair = pair_sums({n: grads[n] for n in rest}, "late")
    scattered.update(zip(rest, zip(pair, _chip_scatter(pair, "late"))))
    halves = [_chip_add(f"chip_add_{n}", lax.dynamic_index_in_dim(scattered[n][0], chip, 0, keepdims=False),
                        scattered[n][1]) for n in BIG]
    quarter = _pair_share(halves)
    out_g, out_d, out_m, out_v = {}, {}, {}, {}
    for n, g2 in zip(BIG, quarter):
        res = _adamw(f"adamw_{n}", _quarter_2d(n, w[n]), g2, _quarter_2d(n, m[n]), _quarter_2d(n, v[n]))
        out_g[n] = g2.reshape(w[n].shape)
        out_d[n], out_m[n], out_v[n] = (r.reshape(w[n].shape) for r in res)

    whole = [grads[n] for n in SMALL]
    summed = _unpack(_small_all_reduce(_pack([loss_part[0:1, 0:1]] + whole)), [(1, 1)] + [a.shape for a in whole])
    loss = summed[0].reshape(())
    small_g = []
    for n, g in zip(SMALL, summed[1:]):
        if n in SMALL_SHARDED:
            axis = SMALL_SHARDED[n]
            g = g.reshape(w[n].shape[:axis] + (-1,) + w[n].shape[axis + 1:])
            size = w[n].shape[axis]
            g = lax.dynamic_slice_in_dim(g, chip * size, size, axis)
        small_g.append(g.reshape(w[n].shape))
    shapes = [w[n].shape for n in SMALL]
    res = _adamw("adamw_small", _pack([w[n] for n in SMALL]), _pack(small_g), _pack([m[n] for n in SMALL]),
                 _pack([v[n] for n in SMALL]))
    small_d, small_m, small_v = (_unpack(r, shapes) for r in res)
    for i, n in enumerate(SMALL):
        out_g[n], out_d[n], out_m[n], out_v[n] = small_g[i], small_d[i], small_m[i], small_v[i]

    return (loss, dx.reshape(x.shape), *[out_g[n] for n in WEIGHTS], *[out_d[n] for n in WEIGHTS],
            *[out_m[n] for n in WEIGHTS], *[out_v[n] for n in WEIGHTS])
```

```python
import functools
import math

import jax
import jax.numpy as jnp
from jax import lax
from jax.experimental import pallas as pl
from jax.experimental.pallas import tpu as pltpu

F32 = jnp.float32
BF16 = jnp.bfloat16
MESH = pl.DeviceIdType.MESH
HIGHEST = lax.Precision.HIGHEST

D = 1024
EPS = 1e-6
NEG_INF = -1e30
N_CHIPS = 4

GDN_QK_HEADS = 8
GDN_V_HEADS = 16
GDN_HD = 128
GDN_QK = GDN_QK_HEADS * GDN_HD
GDN_V = GDN_V_HEADS * GDN_HD
GDN_CONV = 2 * GDN_QK + GDN_V
GDN_MAIN = GDN_CONV + GDN_V
GDN_IN = GDN_MAIN + 2 * GDN_V_HEADS
GDN_IN_SHARD = GDN_IN // N_CHIPS
GDN_CHUNK = 64

SWA_Q_HEADS = 16
SWA_KV_HEADS = 4
SWA_GROUP = 4
SWA_HD = 64
SWA_BLOCK = 128
REL_BUCKETS = 32
REL_MAX_DISTANCE = 128

DFF = 2816
DFF2 = 2 * DFF
DFF2_SHARD = DFF2 // N_CHIPS
DFF_SHARD = DFF // N_CHIPS

ADAM_LR = 0.001
ADAM_B1 = 0.9
ADAM_B2 = 0.999
ADAM_EPS = 1e-08
ADAM_WD = 0.01
ADAM_STEP = 10

LANE = 128
SUBLANE = 8
VMEM_LIMIT = 56 * 1024 * 1024


def _params(sem, vmem=VMEM_LIMIT):
    return pltpu.CompilerParams(dimension_semantics=sem, vmem_limit_bytes=vmem)


def _rowcall(name, fn, T, tm, ins, outs):
    n = T // tm
    r8 = tm // SUBLANE
    last8 = T // SUBLANE - 1
    arrays, in_specs = [], []
    for arr, kind, cols in ins:
        arrays.append(arr)
        if kind == "full":
            in_specs.append(pl.BlockSpec(arr.shape, functools.partial(lambda nd, i: (0,) * nd, arr.ndim)))
        elif arr.ndim == 2:
            w, ci = cols if cols is not None else (arr.shape[1], 0)
            if kind == "row":
                in_specs.append(pl.BlockSpec((tm, w), functools.partial(lambda ci, i: (i, ci), ci)))
            elif kind == "prev":
                in_specs.append(pl.BlockSpec(
                    (SUBLANE, w), functools.partial(lambda ci, i: (jnp.maximum(i * r8 - 1, 0), ci), ci)))
            else:
                in_specs.append(pl.BlockSpec(
                    (SUBLANE, w), functools.partial(lambda ci, i: (jnp.minimum((i + 1) * r8, last8), ci), ci)))
        else:
            lead = arr.shape[:-2]
            in_specs.append(pl.BlockSpec(lead + (tm, arr.shape[-1]),
                                         functools.partial(lambda nl, i: (0,) * nl + (i, 0), len(lead))))
    out_shape, out_specs = [], []
    for shape, dtype, kind in outs:
        out_shape.append(jax.ShapeDtypeStruct(shape, dtype))
        if kind == "acc":
            out_specs.append(pl.BlockSpec(shape, functools.partial(lambda nd, i: (0,) * nd, len(shape))))
        else:
            lead = shape[:-2]
            out_specs.append(pl.BlockSpec(lead + (tm, shape[-1]),
                                          functools.partial(lambda nl, i: (0,) * nl + (i, 0), len(lead))))
    nin = len(arrays)

    def body(*refs):
        i = pl.program_id(0)
        vals = [r[...] for r in refs[:nin]]
        res = fn(i, *vals)
        for (shape, dtype, kind), o, r in zip(outs, refs[nin:], res):
            if kind == "row":
                o[...] = r.astype(dtype)
            else:
                @pl.when(i == 0)
                def _():
                    o[...] = r.astype(dtype)

                @pl.when(i > 0)
                def _():
                    o[...] += r.astype(dtype)

    res = pl.pallas_call(
        body, name=name, grid=(n,), in_specs=in_specs, out_specs=out_specs, out_shape=out_shape,
        compiler_params=_params(("arbitrary",)),
    )(*arrays)
    return res


def _mm(name, a, b, out_shape, out_dtype, grid, a_spec, b_spec, o_spec, dims, acc_shape, res=None, precision=None,
        into=None):
    nk = grid[2]
    n_in = 2 + (res is not None) + (into is not None)

    def body(*refs):
        a_ref, b_ref, o_ref = refs[0], refs[1], refs[n_in]
        r_ref = refs[2] if res is not None else None
        av, bv = a_ref[...], b_ref[...]
        if precision is None:
            av, bv = av.astype(BF16), bv.astype(BF16)
        p = lax.dot_general(av, bv, (dims, ((), ())), preferred_element_type=F32, precision=precision)

        def finish(x):
            if res is not None:
                x = x + r_ref[...].astype(F32)
            o_ref[...] = x.astype(out_dtype).reshape(o_ref.shape)

        if nk == 1:
            finish(p)
        else:
            acc = refs[-1]
            k = pl.program_id(2)

            @pl.when(k == 0)
            def _():
                acc[...] = p

            @pl.when(k > 0)
            def _():
                acc[...] += p

            @pl.when(k == nk - 1)
            def _():
                finish(acc[...])

    ops = [a, b] + ([res] if res is not None else []) + ([into] if into is not None else [])
    specs = [a_spec, b_spec] + ([o_spec] if res is not None else [])
    specs += [pl.BlockSpec(memory_space=pl.ANY)] if into is not None else []
    return pl.pallas_call(
        body, name=name, grid=grid, in_specs=specs, out_specs=o_spec,
        out_shape=jax.ShapeDtypeStruct(out_shape, out_dtype),
        input_output_aliases={n_in - 1: 0} if into is not None else {},
        scratch_shapes=[pltpu.VMEM(acc_shape, F32)] if nk > 1 else [],
        compiler_params=_params(("parallel", "parallel", "arbitrary")),
    )(*ops)


NN = ((1,), (0,))
NT = ((1,), (1,))
TN = ((0,), (0,))


BIG_TILES = (1024, 512, 256, 128)


def _tile(n, pref):
    for t in pref:
        if n % t == 0:
            return t
    return n


def _mm_nn(name, a, w, out_dtype, res=None, precision=None):
    M, K = a.shape
    N = w.shape[1]
    tm = _tile(M, BIG_TILES if K <= 2048 else BIG_TILES[1:])
    tn = _tile(N, BIG_TILES)
    return _mm(name, a, w, (M, N), out_dtype, (M // tm, N // tn, 1),
               pl.BlockSpec((tm, K), lambda i, j, k: (i, 0)), pl.BlockSpec((K, tn), lambda i, j, k: (0, j)),
               pl.BlockSpec((tm, tn), lambda i, j, k: (i, j)), NN, (tm, tn), res=res, precision=precision)


def _mm_nt(name, g, w, out_dtype, res=None, precision=None):
    M, N = g.shape
    K = w.shape[0]
    tm, tk = _tile(M, BIG_TILES), _tile(K, (1024, 1408, 512, 256, 128))
    tn = _tile(N, (1536,) + BIG_TILES)
    return _mm(name, g, w, (M, K), out_dtype, (M // tm, K // tk, N // tn),
               pl.BlockSpec((tm, tn), lambda i, j, k: (i, k)), pl.BlockSpec((tk, tn), lambda i, j, k: (j, k)),
               pl.BlockSpec((tm, tk), lambda i, j, k: (i, j)), NT, (tm, tk), res=res, precision=precision)


def _mm_tn(name, a, g, out_dtype=F32, precision=None):
    T, K = a.shape
    N = g.shape[1]
    tk, tn = _tile(K, (1024, 1408, 512, 256, 128)), _tile(N, BIG_TILES)
    tt = _tile(T, BIG_TILES)
    return _mm(name, a, g, (K, N), out_dtype, (K // tk, N // tn, T // tt),
               pl.BlockSpec((tt, tk), lambda i, j, k: (k, i)), pl.BlockSpec((tt, tn), lambda i, j, k: (k, j)),
               pl.BlockSpec((tk, tn), lambda i, j, k: (i, j)), TN, (tk, tn), precision=precision)


def _mm_up(name, n, wup, layer):
    T = n.shape[0]
    tm = _tile(T, BIG_TILES)
    return _mm(name, n, wup, (T, DFF2), BF16, (T // tm, N_CHIPS, 1),
               pl.BlockSpec((tm, D), lambda i, j, k: (i, 0)),
               pl.BlockSpec((None, None, D, DFF2_SHARD), lambda i, j, k: (j, layer, 0, 0)),
               pl.BlockSpec((tm, DFF2_SHARD), lambda i, j, k: (i, j)), NN, (tm, DFF2_SHARD))


def _mm_up_nt(name, du, wup, layer):
    T = du.shape[0]
    tm, tk = _tile(T, BIG_TILES), D
    return _mm(name, du, wup, (T, D), F32, (T // tm, D // tk, N_CHIPS),
               pl.BlockSpec((tm, DFF2_SHARD), lambda i, j, k: (i, k)),
               pl.BlockSpec((None, None, tk, DFF2_SHARD), lambda i, j, k: (k, layer, j, 0)),
               pl.BlockSpec((tm, tk), lambda i, j, k: (i, j)), NT, (tm, tk))


def _mm_up_tn(name, n, du, layer, into):
    T = n.shape[0]
    tk, tt = D, _tile(T, BIG_TILES)
    return _mm(name, n, du, (N_CHIPS, 2, D, DFF2_SHARD), F32, (D // tk, N_CHIPS, T // tt),
               pl.BlockSpec((tt, tk), lambda i, j, k: (k, i)), pl.BlockSpec((tt, DFF2_SHARD), lambda i, j, k: (k, j)),
               pl.BlockSpec((None, None, tk, DFF2_SHARD), lambda i, j, k: (j, layer, i, 0)), TN, (tk, DFF2_SHARD),
               into=into)


def _mm_down_tn(name, act, dout, layer, into):
    T = act.shape[0]
    tk, tn, tt = 2 * DFF_SHARD, _tile(D, BIG_TILES), _tile(T, BIG_TILES)
    return _mm(name, act, dout, (2, 2, 2, DFF_SHARD, D), F32, (DFF // tk, D // tn, T // tt),
               pl.BlockSpec((tt, tk), lambda i, j, k: (k, i)), pl.BlockSpec((tt, tn), lambda i, j, k: (k, j)),
               pl.BlockSpec((None, 2, None, DFF_SHARD, tn), lambda i, j, k: (i, 0, layer, 0, j)), TN, (tk, tn),
               into=into)


def _sigmoid(x):
    return 0.5 * jnp.tanh(0.5 * x) + 0.5


def _silu(x):
    return x * _sigmoid(x)


def _softplus(x):
    return jnp.maximum(x, 0.0) + jnp.log(1.0 + jnp.exp(-jnp.abs(x)))


def _rms_core(h, w):
    return h * lax.rsqrt(jnp.mean(h * h, axis=-1, keepdims=True) + EPS) * w


def _shift_down(x, halo, s, i):
    if s == 0:
        return x
    tm = x.shape[0]
    rolled = pltpu.roll(x, s, 0)
    patch = pltpu.roll(jnp.where(i == 0, 0.0, halo), s, 0)
    row = lax.broadcasted_iota(jnp.int32, patch.shape, 0)
    top = jnp.where(row < s, patch, rolled[:SUBLANE])
    return jnp.concatenate([top, rolled[SUBLANE:]], axis=0) if tm > SUBLANE else top


def _shift_up(x, halo, s, i, n):
    if s == 0:
        return x
    tm = x.shape[0]
    rolled = pltpu.roll(x, tm - s, 0)
    patch = pltpu.roll(jnp.where(i == n - 1, 0.0, halo), SUBLANE - s, 0)
    row = lax.broadcasted_iota(jnp.int32, patch.shape, 0)
    bottom = jnp.where(row >= SUBLANE - s, patch, rolled[tm - SUBLANE:])
    return jnp.concatenate([rolled[:tm - SUBLANE], bottom], axis=0) if tm > SUBLANE else bottom


def _taps(x, halo, K, i):
    return [_shift_down(x, halo, K - 1 - j, i) for j in range(K)]


def _conv_fwd(taps, w):
    y = w[0:1, :] * taps[0]
    for j in range(1, len(taps)):
        y = y + w[j:j + 1, :] * taps[j]
    return y


def _conv_dx(dy, halo_next, w, i, n):
    K = w.shape[0]
    dx = w[K - 1:K, :] * dy
    for j in range(K - 1):
        dx = dx + w[j:j + 1, :] * _shift_up(dy, halo_next, K - 1 - j, i, n)
    return dx


def _conv_dw(dy, taps):
    rows = [jnp.sum(dy * tap, axis=0, keepdims=True) for tap in taps]
    return jnp.concatenate(rows + [jnp.zeros((SUBLANE - len(taps), dy.shape[1]), F32)], axis=0)


def _rms_fwd(name, h, w, tm=512):
    T = h.shape[0]
    tm = min(tm, T)

    def fn(i, hv, wv):
        return (_rms_core(hv, wv),)

    return _rowcall(name, fn, T, tm, [(h, "row", None), (w, "full", None)], [((T, D), BF16, "row")])[0]


def _rms_bwd(name, h, pairs, adds, tm=256):
    T = h.shape[0]
    tm = min(tm, T)
    npair, nadd = len(pairs), len(adds)

    def fn(i, hv, *rest):
        ws, dns, ads = rest[:npair], rest[npair:2 * npair], rest[2 * npair:]
        dh = None
        dws = []
        for wv, dn in zip(ws, dns):
            _, vjp = jax.vjp(_rms_core, hv, wv)
            dhi, dwi = vjp(dn.astype(F32))
            dh = dhi if dh is None else dh + dhi
            dws.append(dwi)
        for a in ads:
            dh = dh + a.astype(F32)
        return (dh, *dws)

    ins = [(h, "row", None)] + [(w, "full", None) for w, _ in pairs] + [(dn, "row", None) for _, dn in pairs]
    ins += [(a, "row", None) for a in adds]
    outs = [((T, D), F32, "row")] + [((1, D), F32, "acc")] * npair
    return _rowcall(name, fn, T, tm, ins, outs)


def _l2(x):
    return x * lax.rsqrt(jnp.sum(x * x, axis=-1, keepdims=True) + EPS)


def _gdn_post_core(yq, yk, yv, pb, pa, a_log, dtb):
    qn = tuple(_l2(_silu(a)) * (GDN_HD ** -0.5) for a in yq)
    kn = tuple(_l2(_silu(a)) for a in yk)
    v = _silu(yv)
    beta = _sigmoid(pb)
    g = -jnp.exp(a_log) * _softplus(pa + dtb)
    return qn, kn, v, beta, g


def _heads(x, n):
    return tuple(x[:, GDN_HD * h:GDN_HD * (h + 1)] for h in range(n))


def _gdn_pre_fwd(pm, pba, conv_w, a_log, dtb, tm=128):
    T = pm.shape[0]
    tm = min(tm, T)

    def fn(i, x, halo, pbav, cw, al, db):
        y = _conv_fwd(_taps(x.astype(F32), halo.astype(F32), 4, i), cw)
        qn, kn, v, beta, g = _gdn_post_core(_heads(y[:, :GDN_QK], 8), _heads(y[:, GDN_QK:2 * GDN_QK], 8),
                                            y[:, 2 * GDN_QK:], pbav[:, :LANE], pbav[:, LANE:], al, db)
        return jnp.stack(qn), jnp.stack(kn), jnp.stack(_heads(v, GDN_V_HEADS)), beta, g

    ins = [(pm, "row", (GDN_CONV, 0)), (pm, "prev", (GDN_CONV, 0)), (pba, "row", None),
           (conv_w, "full", None), (a_log, "full", None), (dtb, "full", None)]
    outs = [((GDN_QK_HEADS, T, GDN_HD), BF16, "row"), ((GDN_QK_HEADS, T, GDN_HD), BF16, "row"),
            ((GDN_V_HEADS, T, GDN_HD), BF16, "row"), ((T, LANE), F32, "row"), ((T, LANE), F32, "row")]
    return _rowcall("gdn_pre_fwd", fn, T, tm, ins, outs)


def _gdn_pre_bwd(pm, pba, conv_w, a_log, dtb, dqn, dkn, dv, dbeta, dg, tm=128):
    T = pm.shape[0]
    tm = min(tm, T)

    def fn(i, x, halo, pbav, cw, al, db, dqv, dkv, dvv, dbv, dgv):
        taps = _taps(x.astype(F32), halo.astype(F32), 4, i)
        y = _conv_fwd(taps, cw)
        prim = (_heads(y[:, :GDN_QK], 8), _heads(y[:, GDN_QK:2 * GDN_QK], 8), y[:, 2 * GDN_QK:],
                pbav[:, :LANE], pbav[:, LANE:], al, db)
        _, vjp = jax.vjp(_gdn_post_core, *prim)
        cot = (tuple(dqv[h].astype(F32) for h in range(8)), tuple(dkv[h].astype(F32) for h in range(8)),
               jnp.concatenate([dvv[h].astype(F32) for h in range(GDN_V_HEADS)], axis=1), dbv, dgv)
        dyq, dyk, dyv, dpb, dpa, dal, ddb = vjp(cot)
        dy = jnp.concatenate(list(dyq) + list(dyk) + [dyv], axis=1)
        dcw = _conv_dw(dy, taps)
        return dy, jnp.concatenate([dpb, dpa], axis=1), dcw, dal, ddb

    ins = [(pm, "row", (GDN_CONV, 0)), (pm, "prev", (GDN_CONV, 0)), (pba, "row", None),
           (conv_w, "full", None), (a_log, "full", None), (dtb, "full", None),
           (dqn, "row", None), (dkn, "row", None), (dv, "row", None), (dbeta, "row", None), (dg, "row", None)]
    outs = [((T, GDN_CONV), BF16, "row"), ((T, 2 * LANE), F32, "row"), ((SUBLANE, GDN_CONV), F32, "acc"),
            ((1, LANE), F32, "acc"), ((1, LANE), F32, "acc")]
    return _rowcall("gdn_pre_bwd", fn, T, tm, ins, outs)


def _gdn_conv_bwd(dy, dz, conv_w, tm=256):
    T = dy.shape[0]
    tm = min(tm, T)
    n = T // tm

    def fn(i, dyv, halo, dzv, cw):
        dx = _conv_dx(dyv.astype(F32), halo.astype(F32), cw, i, n)
        return (jnp.concatenate([dx.astype(BF16), dzv.astype(BF16)], axis=1),)

    ins = [(dy, "row", None), (dy, "next", None), (dz, "row", None), (conv_w, "full", None)]
    return _rowcall("gdn_conv_bwd", fn, T, tm, ins, [((T, GDN_MAIN), BF16, "row")])[0]


def _bdot(a, b, dims=NN):
    return lax.dot_general(a.astype(BF16), b.astype(BF16), (dims, ((), ())), preferred_element_type=F32)


BNN = ((2,), (1,))
BNT = ((2,), (2,))
BTN = ((1,), (1,))


def _bmm(a, b, dims=BNN):
    return lax.dot_general(a.astype(BF16), b.astype(BF16), (dims, ((0,), (0,))), preferred_element_type=F32)


def _bmm3(a, b):
    ah, bh = a.astype(BF16), b.astype(BF16)
    al, bl = (a - ah.astype(F32)).astype(BF16), (b - bh.astype(F32)).astype(BF16)
    dn = (BNN, ((0,), (0,)))
    return (lax.dot_general(ah, bh, dn, preferred_element_type=F32)
            + lax.dot_general(al, bh, dn, preferred_element_type=F32)
            + lax.dot_general(ah, bl, dn, preferred_element_type=F32))


def _tri_inv(m):
    C = m.shape[-1]
    r = lax.broadcasted_iota(jnp.int32, (C, C), 0)
    c = lax.broadcasted_iota(jnp.int32, (C, C), 1)
    t = jnp.where(r == c, 1.0, 0.0) - m
    pw = _bmm3(m, m)
    t = t + _bmm3(t, pw)
    for _ in range(int(math.log2(C)) - 2):
        pw = _bmm(pw, pw)
        t = t + _bmm(t, pw)
    return t


def _tri_inv_vjp(t, dt):
    tt = jnp.swapaxes(t, 1, 2)
    return -_bmm(_bmm(tt, dt), tt)


def _twice(a):
    return jnp.broadcast_to(a[:, None], (a.shape[0], 2) + a.shape[1:]).reshape((2 * a.shape[0],) + a.shape[1:])


def _gdn_gates(grow, brow):
    C = grow.shape[2]
    r = lax.broadcasted_iota(jnp.int32, (C, C), 0)
    c = lax.broadcasted_iota(jnp.int32, (C, C), 1)
    tril, eye = r >= c, r == c
    gcol = jnp.sum(jnp.where(eye, grow, 0.0), axis=2, keepdims=True)
    bcol = jnp.sum(jnp.where(eye, brow, 0.0), axis=2, keepdims=True)
    gc_col = jnp.sum(jnp.where(tril, grow, 0.0), axis=2, keepdims=True)
    gc_row = jnp.sum(jnp.where(r <= c, gcol, 0.0), axis=1, keepdims=True)
    gc_last = jnp.sum(grow, axis=2, keepdims=True)
    decay = jnp.where(tril, jnp.exp(jnp.where(tril, gc_col - gc_row, 0.0)), 0.0)
    return bcol, gc_col, gc_last, decay


def _gdn_m(k, grow, brow):
    C = k.shape[1]
    strict = lax.broadcasted_iota(jnp.int32, (C, C), 0) > lax.broadcasted_iota(jnp.int32, (C, C), 1)
    bcol, _, _, decay = _gdn_gates(grow, brow)
    return jnp.where(strict, bcol * _twice(_bmm(k, k, BNT)) * decay, 0.0)


def _gdn_rest(q, k, v, grow, brow, t_mat, S):
    bcol, gc_col, gc_last, decay = _gdn_gates(grow, brow)
    qk = _twice(_bmm(q, k, BNT))
    k2, q2 = _twice(k), _twice(q)
    egc = jnp.exp(gc_col)
    u = _bmm(t_mat, v * bcol)
    w = _bmm(t_mat, k2 * (bcol * egc))
    v_new = u - _bmm(w, S)
    o = _bmm(q2 * egc, S) + _bmm(qk * decay, v_new)
    s_new = S * jnp.exp(gc_last) + _bmm(k2 * jnp.exp(gc_last - gc_col), v_new, BTN)
    return o, s_new


def _gdn_tb(T):
    return min(256, T)


def _gate_rows(g):
    T = g.shape[0]
    g = g[:, :GDN_V_HEADS].reshape(T // GDN_CHUNK, GDN_CHUNK, GDN_V_HEADS)
    return g.transpose(0, 2, 1)[:, :, None, :]


def _gate_cols(g):
    nc = g.shape[0]
    g = g[:, :, 0, :].transpose(0, 2, 1).reshape(nc * GDN_CHUNK, GDN_V_HEADS)
    return jnp.pad(g, ((0, 0), (0, LANE - GDN_V_HEADS)))


def _gdn_fwd(qn, kn, v, g, beta, gather=None):
    T = qn.shape[1]
    tb = _gdn_tb(T)
    nc = tb // GDN_CHUNK
    nsteps = T // tb
    quarters, buffers = gather if gather is not None else ((), ())
    ng = len(quarters)
    shapes = [a.shape for a in quarters]
    splits = [True] * ng

    def body(*refs):
        q_ref, k_ref, v_ref, g_ref, b_ref = refs[:5]
        src = refs[5:5 + ng]
        o_ref, sall_ref, tall_ref = refs[5 + 2 * ng:8 + 2 * ng]
        dst = refs[8 + 2 * ng:8 + 3 * ng]
        s_scr = refs[8 + 3 * ng]
        step = pl.program_id(0)

        @pl.when(step == 0)
        def _():
            s_scr[...] = jnp.zeros(s_scr.shape, F32)
            if ng:
                for cp in _gather_copies(shapes, splits, src, dst, *refs[9 + 3 * ng:])[0]:
                    cp.start()

        def chunk(ci, carry):
            rows = pl.ds(pl.multiple_of(ci * GDN_CHUNK, GDN_CHUNK), GDN_CHUNK)
            s = s_scr[...]
            sall_ref[ci] = s
            q, k = q_ref[:, rows, :].astype(F32), k_ref[:, rows, :].astype(F32)
            t_mat = _tri_inv(_gdn_m(k, g_ref[ci], b_ref[ci])).astype(BF16)
            tall_ref[ci] = t_mat
            o, s_new = _gdn_rest(q, k, v_ref[:, rows, :].astype(F32), g_ref[ci], b_ref[ci], t_mat.astype(F32), s)
            o_ref[:, rows, :] = o.astype(o_ref.dtype)
            s_scr[...] = s_new
            return carry

        lax.fori_loop(0, nc, chunk, 0)

        if ng:
            @pl.when(step == nsteps - 1)
            def _():
                _gather_arrival(shapes, splits, src, dst, *refs[9 + 3 * ng:])

    qk_spec = pl.BlockSpec((GDN_QK_HEADS, tb, GDN_HD), lambda i: (0, i, 0))
    v_spec = pl.BlockSpec((GDN_V_HEADS, tb, GDN_HD), lambda i: (0, i, 0))
    g_spec = pl.BlockSpec((nc, GDN_V_HEADS, 1, GDN_CHUNK), lambda i: (i, 0, 0, 0))
    anywhere = pl.BlockSpec(memory_space=pl.ANY)
    return pl.pallas_call(
        body, name="gdn_fwd", grid=(nsteps,),
        in_specs=[qk_spec, qk_spec, v_spec, g_spec, g_spec] + [anywhere] * (2 * ng),
        out_specs=[v_spec, pl.BlockSpec((nc, GDN_V_HEADS, GDN_HD, GDN_HD), lambda i: (i, 0, 0, 0)),
                   pl.BlockSpec((nc, GDN_V_HEADS, GDN_CHUNK, GDN_CHUNK), lambda i: (i, 0, 0, 0))] + [anywhere] * ng,
        out_shape=[jax.ShapeDtypeStruct((GDN_V_HEADS, T, GDN_HD), BF16),
                   jax.ShapeDtypeStruct((T // GDN_CHUNK, GDN_V_HEADS, GDN_HD, GDN_HD), F32),
                   jax.ShapeDtypeStruct((T // GDN_CHUNK, GDN_V_HEADS, GDN_CHUNK, GDN_CHUNK), BF16)]
        + [jax.ShapeDtypeStruct(b.shape, b.dtype) for b in buffers],
        input_output_aliases={5 + ng + a: 3 + a for a in range(ng)},
        scratch_shapes=[pltpu.VMEM((GDN_V_HEADS, GDN_HD, GDN_HD), F32)]
        + ([pltpu.SemaphoreType.DMA((6 * ng,)), pltpu.SemaphoreType.DMA((6 * ng,))] if ng else []),
        compiler_params=_params(("arbitrary",)),
    )(qn, kn, v, g, beta, *quarters, *buffers)


def _gdn_bwd(qn, kn, v, g, beta, sall, tall, do, scatter=()):
    T = qn.shape[1]
    tb = _gdn_tb(T)
    nc = tb // GDN_CHUNK
    nb = T // tb
    ns = len(scatter)

    def body(*refs):
        q_ref, k_ref, v_ref, g_ref, b_ref, sall_ref, tall_ref, do_ref = refs[:8]
        dq_ref, dk_ref, dv_ref, dg_ref, db_ref = refs[8 + ns:13 + ns]
        ds_scr = refs[13 + 2 * ns]
        comm = (refs[8:8 + ns], refs[13 + ns:13 + 2 * ns], *refs[14 + 2 * ns:])
        step = pl.program_id(0)

        @pl.when(step == 0)
        def _():
            ds_scr[...] = jnp.zeros(ds_scr.shape, F32)
            if ns:
                for cp in _scatter_copies(*comm):
                    cp.start()

        def chunk(cr, carry):
            ci = nc - 1 - cr
            rows = pl.ds(pl.multiple_of(ci * GDN_CHUNK, GDN_CHUNK), GDN_CHUNK)
            k, t_mat = k_ref[:, rows, :].astype(F32), tall_ref[ci].astype(F32)
            _, vjp = jax.vjp(_gdn_rest, q_ref[:, rows, :].astype(F32), k, v_ref[:, rows, :].astype(F32),
                             g_ref[ci], b_ref[ci], t_mat, sall_ref[ci])
            dq, dk, dv, dg, db, dt, ds = vjp((do_ref[:, rows, :].astype(F32), ds_scr[...]))
            _, vjp_m = jax.vjp(_gdn_m, k, g_ref[ci], b_ref[ci])
            dk_m, dg_m, db_m = vjp_m(_tri_inv_vjp(t_mat, dt))
            ds_scr[...] = ds
            dq_ref[:, rows, :] = dq
            dk_ref[:, rows, :] = dk + dk_m
            dv_ref[:, rows, :] = dv
            dg_ref[ci] = dg + dg_m
            db_ref[ci] = db + db_m
            return carry

        lax.fori_loop(0, nc, chunk, 0)

        if ns:
            @pl.when(step == nb - 1)
            def _():
                copies = _scatter_copies(*comm)
                for cp in copies:
                    cp.wait_recv()
                for cp in copies:
                    cp.wait_send()

    qk_spec = pl.BlockSpec((GDN_QK_HEADS, tb, GDN_HD), lambda i: (0, nb - 1 - i, 0))
    v_spec = pl.BlockSpec((GDN_V_HEADS, tb, GDN_HD), lambda i: (0, nb - 1 - i, 0))
    g_spec = pl.BlockSpec((nc, GDN_V_HEADS, 1, GDN_CHUNK), lambda i: (nb - 1 - i, 0, 0, 0))
    s_spec = pl.BlockSpec((nc, GDN_V_HEADS, GDN_HD, GDN_HD), lambda i: (nb - 1 - i, 0, 0, 0))
    t_spec = pl.BlockSpec((nc, GDN_V_HEADS, GDN_CHUNK, GDN_CHUNK), lambda i: (nb - 1 - i, 0, 0, 0))
    anywhere = pl.BlockSpec(memory_space=pl.ANY)
    return pl.pallas_call(
        body, name="gdn_bwd", grid=(nb,),
        in_specs=[qk_spec, qk_spec, v_spec, g_spec, g_spec, s_spec, t_spec, v_spec] + [anywhere] * ns,
        out_specs=[qk_spec, qk_spec, v_spec, g_spec, g_spec] + [anywhere] * ns,
        out_shape=[jax.ShapeDtypeStruct((GDN_QK_HEADS, T, GDN_HD), F32),
                   jax.ShapeDtypeStruct((GDN_QK_HEADS, T, GDN_HD), F32),
                   jax.ShapeDtypeStruct((GDN_V_HEADS, T, GDN_HD), F32),
                   jax.ShapeDtypeStruct(g.shape, F32), jax.ShapeDtypeStruct(g.shape, F32)]
        + _scatter_shapes(scatter),
        scratch_shapes=[pltpu.VMEM((GDN_V_HEADS, GDN_HD, GDN_HD), F32)]
        + ([pltpu.SemaphoreType.DMA((3 * ns,)), pltpu.SemaphoreType.DMA((3 * ns,))] if ns else []),
        compiler_params=_params(("arbitrary",)),
    )(qn, kn, v, g, beta, sall, tall, do, *scatter)


def _gnorm_core(o, z, w):
    return tuple(_rms_core(oh, w) * _silu(zh) for oh, zh in zip(o, z))


def _gnorm_fwd(o, pm, w, tm=256):
    T = pm.shape[0]
    tm = min(tm, T)

    def fn(i, ov, zv, wv):
        zf = zv.astype(F32)
        out = _gnorm_core(tuple(ov[h].astype(F32) for h in range(GDN_V_HEADS)), _heads(zf, GDN_V_HEADS), wv)
        return (jnp.concatenate(out, axis=1),)

    ins = [(o, "row", None), (pm, "row", (GDN_V, 2)), (w, "full", None)]
    return _rowcall("gnorm_fwd", fn, T, tm, ins, [((T, GDN_V), BF16, "row")])[0]


def _gnorm_bwd(o, pm, w, don, tm=128):
    T = pm.shape[0]
    tm = min(tm, T)

    def fn(i, ov, zv, wv, dv):
        zf, df = zv.astype(F32), dv.astype(F32)
        _, vjp = jax.vjp(_gnorm_core, tuple(ov[h].astype(F32) for h in range(GDN_V_HEADS)),
                         _heads(zf, GDN_V_HEADS), wv)
        do, dz, dw = vjp(_heads(df, GDN_V_HEADS))
        return jnp.stack(do), jnp.concatenate(dz, axis=1), dw

    ins = [(o, "row", None), (pm, "row", (GDN_V, 2)), (w, "full", None), (don, "row", None)]
    outs = [((GDN_V_HEADS, T, GDN_HD), BF16, "row"), ((T, GDN_V), BF16, "row"), ((1, GDN_HD), F32, "acc")]
    return _rowcall("gnorm_bwd", fn, T, tm, ins, outs)


def _ffn_act_fwd(name, up, conv_w, conv_b, tm=128):
    T = up.shape[0]
    tm = min(tm, T)

    def fn(i, x, halo, cw, cb):
        u = _conv_fwd(_taps(x.astype(F32), halo.astype(F32), 3, i), cw) + cb
        return (_silu(u[:, :DFF]) * u[:, DFF:],)

    ins = [(up, "row", None), (up, "prev", None), (conv_w, "full", None), (conv_b, "full", None)]
    return _rowcall(name, fn, T, tm, ins, [((T, DFF), BF16, "row")])[0]


def _ffn_act_bwd(name, up, conv_w, conv_b, dact, tm=128):
    T = up.shape[0]
    tm = min(tm, T)

    def fn(i, x, halo, cw, cb, da):
        taps = _taps(x.astype(F32), halo.astype(F32), 3, i)
        da = da.astype(F32)
        u = _conv_fwd(taps, cw) + cb
        gate, val = u[:, :DFF], u[:, DFF:]
        sg = _sigmoid(gate)
        dgate = da * val * sg * (1.0 + gate * (1.0 - sg))
        dval = da * gate * sg
        du = jnp.concatenate([dgate, dval], axis=1)
        return du, _conv_dw(du, taps), jnp.sum(du, axis=0, keepdims=True)

    ins = [(up, "row", None), (up, "prev", None), (conv_w, "full", None), (conv_b, "full", None),
           (dact, "row", None)]
    outs = [((T, DFF2), BF16, "row"), ((SUBLANE, DFF2), F32, "acc"), ((1, DFF2), F32, "acc")]
    return _rowcall(name, fn, T, tm, ins, outs)


def _ffn_conv_bwd(name, du, conv_w, tm=256):
    T = du.shape[0]
    tm = min(tm, T)
    n = T // tm

    def fn(i, dv, halo, cw):
        return (_conv_dx(dv.astype(F32), halo.astype(F32), cw, i, n),)

    ins = [(du, "row", None), (du, "next", None), (conv_w, "full", None)]
    return _rowcall(name, fn, T, tm, ins, [((T, DFF2), BF16, "row")])[0]


GROUP_ROWS = SWA_GROUP * SWA_BLOCK


def _attn_core(q, kp, kc, vp, vc, bias, sink, mask):
    kcat = jnp.concatenate([kp, kc], axis=0)
    vcat = jnp.concatenate([vp, vc], axis=0)
    s = _bdot(q * (SWA_HD ** -0.5), kcat, NT) + bias
    s = jnp.where(mask, s, NEG_INF)
    m = lax.stop_gradient(jnp.maximum(jnp.max(s, axis=-1, keepdims=True), sink))
    p = jnp.exp(s - m)
    denom = jnp.sum(p, axis=-1, keepdims=True) + jnp.exp(sink - m)
    return _bdot(p / denom, vcat)


def _attn_mask(i):
    qi = lax.broadcasted_iota(jnp.int32, (GROUP_ROWS, 2 * SWA_BLOCK), 0) & (SWA_BLOCK - 1)
    ki = lax.broadcasted_iota(jnp.int32, (GROUP_ROWS, 2 * SWA_BLOCK), 1)
    dist = qi + SWA_BLOCK - ki
    return (dist >= 0) & (dist < SWA_BLOCK) & ((ki >= SWA_BLOCK) | (i > 0))


def _head_cols(h):
    return slice(h * SWA_HD, (h + 1) * SWA_HD)


def _stacked_heads(ref, j):
    return jnp.concatenate([ref[:, _head_cols(SWA_GROUP * j + g)].astype(F32) for g in range(SWA_GROUP)], axis=0)


def _flat_operands(j, q_ref, kvc_ref, kvp_ref, b_ref, s_ref):
    heads = slice(SWA_GROUP * j, SWA_GROUP * (j + 1))
    sink = jnp.concatenate([jnp.broadcast_to(s_ref[j, g:g + 1, 0:1], (SWA_BLOCK, 1)) for g in range(SWA_GROUP)],
                           axis=0)
    k_cols, v_cols = _head_cols(j), _head_cols(SWA_KV_HEADS + j)
    return (_stacked_heads(q_ref, j), kvp_ref[:, k_cols].astype(F32), kvc_ref[:, k_cols].astype(F32),
            kvp_ref[:, v_cols].astype(F32), kvc_ref[:, v_cols].astype(F32),
            b_ref[heads].reshape(GROUP_ROWS, 2 * SWA_BLOCK), sink)


def _store_heads(ref, j, stacked):
    for g in range(SWA_GROUP):
        ref[:, _head_cols(SWA_GROUP * j + g)] = stacked[g * SWA_BLOCK:(g + 1) * SWA_BLOCK].astype(ref.dtype)


def _attn_fwd_flat(q, kv, bias, sinks):
    T = q.shape[0]
    nb = T // SWA_BLOCK

    def body(q_ref, kvc_ref, kvp_ref, b_ref, s_ref, o_ref):
        mask = _attn_mask(pl.program_id(0))
        for j in range(SWA_KV_HEADS):
            _store_heads(o_ref, j, _attn_core(*_flat_operands(j, q_ref, kvc_ref, kvp_ref, b_ref, s_ref), mask))

    q_spec = pl.BlockSpec((SWA_BLOCK, q.shape[1]), lambda i: (i, 0))
    cur = pl.BlockSpec((SWA_BLOCK, kv.shape[1]), lambda i: (i, 0))
    prev = pl.BlockSpec((SWA_BLOCK, kv.shape[1]), lambda i: (jnp.maximum(i - 1, 0), 0))
    return pl.pallas_call(
        body, name="attn_fwd", grid=(nb,),
        in_specs=[q_spec, cur, prev, pl.BlockSpec(bias.shape, lambda i: (0, 0, 0)),
                  pl.BlockSpec(sinks.shape, lambda i: (0, 0, 0))],
        out_specs=q_spec, out_shape=jax.ShapeDtypeStruct(q.shape, BF16),
        compiler_params=_params(("arbitrary",)),
    )(q, kv, kv, bias, sinks)


def _attn_bwd_flat(q, kv, bias, sinks, do):
    T = q.shape[0]
    nb = T // SWA_BLOCK

    def body(q_ref, kvc_ref, kvp_ref, b_ref, s_ref, do_ref, dq_ref, dkv_ref, db_ref, dsk_ref, carry):
        i = pl.program_id(0)

        @pl.when(i < nb)
        def _():
            mask = _attn_mask(i)
            for j in range(SWA_KV_HEADS):
                heads = slice(SWA_GROUP * j, SWA_GROUP * (j + 1))
                k_cols, v_cols = _head_cols(j), _head_cols(SWA_KV_HEADS + j)
                prim = _flat_operands(j, q_ref, kvc_ref, kvp_ref, b_ref, s_ref)
                _, vjp = jax.vjp(functools.partial(_attn_core, mask=mask), *prim)
                dq, dkp, dkc, dvp, dvc, db, dsc = vjp(_stacked_heads(do_ref, j))
                _store_heads(dq_ref, j, dq)
                db = db.reshape(SWA_GROUP, SWA_BLOCK, 2 * SWA_BLOCK)
                dsk = jnp.concatenate(
                    [jnp.broadcast_to(jnp.sum(dsc[g * SWA_BLOCK:(g + 1) * SWA_BLOCK], axis=0, keepdims=True),
                                      (1, LANE)) for g in range(SWA_GROUP)], axis=0)

                @pl.when(i == 0)
                def _():
                    db_ref[heads] = db
                    dsk_ref[j] = dsk

                @pl.when(i > 0)
                def _():
                    db_ref[heads] += db
                    dsk_ref[j] += dsk
                    dkv_ref[:, k_cols] = (carry[:, k_cols] + dkp).astype(dkv_ref.dtype)
                    dkv_ref[:, v_cols] = (carry[:, v_cols] + dvp).astype(dkv_ref.dtype)

                carry[:, k_cols] = dkc
                carry[:, v_cols] = dvc

        @pl.when(i == nb)
        def _():
            dkv_ref[...] = carry[...].astype(dkv_ref.dtype)

    last = nb - 1
    q_spec = pl.BlockSpec((SWA_BLOCK, q.shape[1]), lambda i: (jnp.minimum(i, last), 0))
    cur = pl.BlockSpec((SWA_BLOCK, kv.shape[1]), lambda i: (jnp.minimum(i, last), 0))
    prev = pl.BlockSpec((SWA_BLOCK, kv.shape[1]), lambda i: (jnp.clip(i - 1, 0, last), 0))
    b_spec = pl.BlockSpec(bias.shape, lambda i: (0, 0, 0))
    s_spec = pl.BlockSpec(sinks.shape, lambda i: (0, 0, 0))
    return pl.pallas_call(
        body, name="attn_bwd", grid=(nb + 1,),
        in_specs=[q_spec, cur, prev, b_spec, s_spec, q_spec],
        out_specs=[q_spec, prev, b_spec, s_spec],
        out_shape=[jax.ShapeDtypeStruct(q.shape, BF16), jax.ShapeDtypeStruct(kv.shape, BF16),
                   jax.ShapeDtypeStruct(bias.shape, F32), jax.ShapeDtypeStruct(sinks.shape, F32)],
        scratch_shapes=[pltpu.VMEM((SWA_BLOCK, kv.shape[1]), F32)],
        compiler_params=_params(("arbitrary",)),
    )(q, kv, kv, bias, sinks, do)


def _rel_onehot():
    qi = jnp.arange(SWA_BLOCK)[:, None]
    ki = jnp.arange(2 * SWA_BLOCK)[None, :]
    n = jnp.maximum(qi + SWA_BLOCK - ki, 0)
    max_exact = REL_BUCKETS // 2
    nf = jnp.maximum(n, 1).astype(F32)
    large = max_exact + (jnp.log(nf / max_exact) / math.log(REL_MAX_DISTANCE / max_exact)
                         * (REL_BUCKETS - max_exact)).astype(jnp.int32)
    bucket = jnp.where(n < max_exact, n, jnp.minimum(large, REL_BUCKETS - 1)).reshape(-1)
    return (bucket[None, :] == jnp.arange(REL_BUCKETS)[:, None]).astype(F32)


def _final(h, w, target, tm=256):
    T = h.shape[0]
    tm = min(tm, T)

    def fn(i, hv, wv, tv):
        y, vjp = jax.vjp(_rms_core, hv, wv)
        err = y - tv
        dh, dw = vjp(err * (1.0 / D))
        part = 0.5 * jnp.sum(jnp.sum(err * err, axis=1, keepdims=True) * (1.0 / D), axis=0, keepdims=True)
        return jnp.broadcast_to(part, (SUBLANE, LANE)), dh, dw

    ins = [(h, "row", None), (w, "full", None), (target, "row", None)]
    outs = [((SUBLANE, LANE), F32, "acc"), ((T, D), F32, "row"), ((1, D), F32, "acc")]
    return _rowcall("final", fn, T, tm, ins, outs)


def _ffn_fwd(tag, h, P, layer):
    n = _rms_fwd(f"{tag}_rms", h, P["ffn_norm_w"][layer:layer + 1])
    up = _mm_up(f"{tag}_up", n, P["w_up"], layer)
    act = _ffn_act_fwd(f"{tag}_act", up, P["ffn_conv_w"][layer], P["ffn_conv_b"][layer:layer + 1])
    out = _mm_nn(f"{tag}_down", act, P["w_down"][layer], F32, res=h)
    return out, (n, up, act)


def _ffn_bwd(tag, h, saved, dout, P, layer, into=(None, None)):
    n, up, act = saved
    cw, cb = P["ffn_conv_w"][layer], P["ffn_conv_b"][layer:layer + 1]
    dact = _mm_nt(f"{tag}_down_dx", dout, P["w_down"][layer], BF16)
    g_down = _mm_down_tn(f"{tag}_down_dw", act, dout, layer, into[1])
    du, dcw, dcb = _ffn_act_bwd(f"{tag}_act_bwd", up, cw, cb, dact)
    dup = _ffn_conv_bwd(f"{tag}_conv_bwd", du, cw)
    g_up = _mm_up_tn(f"{tag}_up_dw", n, dup, layer, into[0])
    dn = _mm_up_nt(f"{tag}_up_dx", dup, P["w_up"], layer)
    dh, dnw = _rms_bwd(f"{tag}_rms_bwd", h, [(P["ffn_norm_w"][layer:layer + 1], dn)], [dout])
    return dh, dict(w_down=g_down, w_up=g_up, conv_w=dcw[:3], conv_b=dcb, norm_w=dnw)


def _local_step(x, target, P, late=None, pair_sums=None):
    T = x.shape[0]
    n0 = _rms_fwd("a_rms", x, P["a_norm_w"])
    pm = _mm_nn("gdn_in", n0, P["w_in_main"], BF16)
    pba = _mm_nn("gdn_in_ba", n0, P["w_in_ba"], F32)
    qn, kn, v, beta, g = _gdn_pre_fwd(pm, pba, P["a_conv_w"], P["a_log"], P["dt_bias"])
    g_rows, beta_rows = _gate_rows(g), _gate_rows(beta)
    o, sall, tall, *gathered = _gdn_fwd(qn, kn, v, g_rows, beta_rows, gather=late)
    if late is not None:
        P = {**P, **_late_weights(gathered)}
    on = _gnorm_fwd(o, pm, P["a_out_norm_w"])
    h1 = _mm_nn("gdn_out", on, P["w_out"], F32, res=x)
    h2, ffn0 = _ffn_fwd("ffn0", h1, P, 0)
    nkv = _rms_fwd("kv_rms", h2, P["kv_norm_w"])
    kv = _mm_nn("kv_proj", nkv, P["w_kv"], BF16)
    nb = _rms_fwd("b_rms", h2, P["b_norm_w"])
    qp = _mm_nn("q_proj", nb, P["w_q"], BF16)
    onehot = _rel_onehot()
    bias = _mm_nn("rel_bias", P["rel_table_t"], onehot, F32, precision=HIGHEST)
    bias = bias.reshape(SWA_Q_HEADS, SWA_BLOCK, 2 * SWA_BLOCK)
    oa = _attn_fwd_flat(qp, kv, bias, P["sinks"])
    h3 = _mm_nn("o_proj", oa, P["w_o"], F32, res=h2)
    h4, ffn1 = _ffn_fwd("ffn1", h3, P, 1)
    loss, dh4, d_final = _final(h4, P["final_norm_w"], target)

    dh3, gf1 = _ffn_bwd("ffn1", h3, ffn1, dh4, P, 1)
    doa = _mm_nt("o_proj_dx", dh3, P["w_o"], BF16)
    g_wo = _mm_tn("o_proj_dw", oa, dh3)
    dqp, dkv, dbias, dsinks = _attn_bwd_flat(qp, kv, bias, P["sinks"], doa)
    g_wq = _mm_tn("q_proj_dw", nb, dqp)
    dnb = _mm_nt("q_proj_dx", dqp, P["w_q"], F32)
    g_wkv = _mm_tn("kv_proj_dw", nkv, dkv)
    dnkv = _mm_nt("kv_proj_dx", dkv, P["w_kv"], F32)
    dh2, d_bnorm, d_kvnorm = _rms_bwd("b_kv_rms_bwd", h2, [(P["b_norm_w"], dnb), (P["kv_norm_w"], dnkv)], [dh3])
    g_table = _mm_nt("rel_bias_dw", onehot, dbias.reshape(SWA_Q_HEADS, -1), F32, precision=HIGHEST)
    dh1, gf0 = _ffn_bwd("ffn0", h1, ffn0, dh2, P, 0, into=(gf1["w_up"], gf1["w_down"]))
    don = _mm_nt("gdn_out_dx", dh1, P["w_out"], BF16)
    g_wout = _mm_tn("gdn_out_dw", on, dh1)
    do, dz, d_gnorm = _gnorm_bwd(o, pm, P["a_out_norm_w"], don)
    ready = dict(a_w_out=g_wout, w_kv=g_wkv, b_w_q=g_wq, b_w_o=g_wo, ffn_w_up=gf0["w_up"], ffn_w_down=gf0["w_down"])
    pairs = pair_sums(ready, "early") if pair_sums is not None else []
    dq, dk, dv, dg, dbeta, *parts = _gdn_bwd(qn, kn, v, g_rows, beta_rows, sall, tall, do, scatter=pairs)
    dy, dpba, d_aconv, d_alog, d_dtb = _gdn_pre_bwd(pm, pba, P["a_conv_w"], P["a_log"], P["dt_bias"],
                                                    dq, dk, dv, _gate_cols(dbeta), _gate_cols(dg))
    dpm = _gdn_conv_bwd(dy, dz, P["a_conv_w"])
    g_win_main = _mm_tn("gdn_in_dw", n0, dpm)
    g_win_ba = _mm_tn("gdn_in_ba_dw", n0, dpba)
    dn0 = _mm_nt("gdn_in_dx", dpm, P["w_in_main"], F32)
    dn0 = _mm_nt("gdn_in_ba_dx", dpba, P["w_in_ba"], F32, res=dn0)
    dx, d_anorm = _rms_bwd("a_rms_bwd", x, [(P["a_norm_w"], dn0)], [dh1])

    nh = GDN_V_HEADS
    grads = dict(
        a_norm_w=d_anorm,
        a_w_in=jnp.concatenate([g_win_main, g_win_ba[:, :nh], g_win_ba[:, LANE:LANE + nh]], axis=1),
        a_conv_w=d_aconv[:4], a_a_log=d_alog[:, :nh], a_dt_bias=d_dtb[:, :nh], a_out_norm_w=d_gnorm,
        a_w_out=g_wout, kv_norm_w=d_kvnorm, w_kv=g_wkv, b_norm_w=d_bnorm, b_w_q=g_wq,
        b_sinks=dsinks[:, :, 0].reshape(1, SWA_Q_HEADS), b_w_o=g_wo, rel_bias_table=g_table,
        ffn_norm_w=jnp.concatenate([gf0["norm_w"], gf1["norm_w"]], axis=0),
        ffn_w_up=gf0["w_up"],
        ffn_conv_w=jnp.stack([gf0["conv_w"], gf1["conv_w"]], axis=0),
        ffn_conv_b=jnp.concatenate([gf0["conv_b"], gf1["conv_b"]], axis=0),
        ffn_w_down=gf0["w_down"],
        final_norm_w=d_final,
    )
    return loss, dx, grads, dict(zip([n for n in BIG if n in ready], zip(pairs, parts)))


HBM_SPEC = pl.BlockSpec(memory_space=pltpu.HBM)
VMEM_SPEC = pl.BlockSpec(memory_space=pltpu.VMEM)


def _coords():
    return lax.axis_index("x"), lax.axis_index("y"), lax.axis_index("c")


def _remote(src, dst, send_sem, recv_sem, device):
    return pltpu.make_async_remote_copy(src_ref=src, dst_ref=dst, send_sem=send_sem, recv_sem=recv_sem,
                                        device_id=device, device_id_type=MESH)


def _other_chips(x, y):
    return [(1 - x, y), (x, 1 - y), (1 - x, 1 - y)]


def _gather_copies(shapes, split, ins, outs, send_sems, recv_sems):
    x, y, c = _coords()
    p = 2 * x + y
    ici, forwards, from_sibling = [], [], []
    for a, shape in enumerate(shapes):
        h = shape[0] // 2
        for j, chip in enumerate(_other_chips(x, y)):
            q = 2 * chip[0] + chip[1]
            if split[a]:
                mine, theirs = pl.ds(c * h, h), pl.ds((1 - c) * h, h)
                ici.append(_remote(ins[a].at[mine], outs[a].at[p, mine], send_sems.at[6 * a + j],
                                   recv_sems.at[6 * a + j], (*chip, c)))
                land = outs[a].at[q, mine]
                forwards.append(_remote(land, land, send_sems.at[6 * a + 3 + j], recv_sems.at[6 * a + 3 + j],
                                        (x, y, 1 - c)))
                land = outs[a].at[q, theirs]
                from_sibling.append(_remote(land, land, send_sems.at[6 * a + 3 + j], recv_sems.at[6 * a + 3 + j],
                                            (x, y, 1 - c)))
            else:
                ici.append(_remote(ins[a], outs[a].at[p], send_sems.at[6 * a + j], recv_sems.at[6 * a + j],
                                   (*chip, c)))
                forwards.append(None)
    return ici, forwards, from_sibling


def _gather_arrival(shapes, split, ins, outs, send_sems, recv_sems):
    x, y, c = _coords()
    ici, forwards, from_sibling = _gather_copies(shapes, split, ins, outs, send_sems, recv_sems)
    k = 0
    for a, shape in enumerate(shapes):
        h = shape[0] // 2
        for j, chip in enumerate(_other_chips(x, y)):
            q = 2 * chip[0] + chip[1]
            land = outs[a].at[q, pl.ds(c * h, h)] if split[a] else outs[a].at[q]
            _remote(land, land, send_sems.at[6 * a + j], recv_sems.at[6 * a + j], (*chip, c)).wait_recv()
            if forwards[k] is not None:
                forwards[k].start()
            k += 1
    for cp in from_sibling:
        cp.wait_recv()
    for cp in ici + [f for f in forwards if f is not None]:
        cp.wait_send()


def _all_gather(arrs, split, remote):
    n = len(arrs)
    now = [a for a in range(n) if remote[a]]
    shapes = [arrs[a].shape for a in now]
    splits = [split[a] for a in now]

    def body(*refs):
        ins, outs, stage = refs[:n], refs[n:2 * n], refs[2 * n:3 * n]
        send_sems, recv_sems, in_sems, out_sems = refs[3 * n:]
        p = 2 * lax.axis_index("x") + lax.axis_index("y")
        gathered = ([ins[a] for a in now], [outs[a] for a in now], send_sems, recv_sems)
        loads = [pltpu.make_async_copy(ins[a], stage[a], in_sems.at[a]) for a in range(n)]
        for cp in loads:
            cp.start()
        for cp in _gather_copies(shapes, splits, *gathered)[0]:
            cp.start()
        stores = [pltpu.make_async_copy(stage[a], outs[a].at[p], out_sems.at[a]) for a in range(n)]
        for a in range(n):
            loads[a].wait()
            stores[a].start()
        _gather_arrival(shapes, splits, *gathered)
        for cp in stores:
            cp.wait()

    return pl.pallas_call(
        body, name="weights_all_gather", in_specs=[HBM_SPEC] * n, out_specs=[HBM_SPEC] * n,
        out_shape=[jax.ShapeDtypeStruct((N_CHIPS,) + a.shape, a.dtype) for a in arrs],
        scratch_shapes=[pltpu.VMEM(a.shape, a.dtype) for a in arrs]
        + [pltpu.SemaphoreType.DMA((6 * len(now),)), pltpu.SemaphoreType.DMA((6 * len(now),)),
           pltpu.SemaphoreType.DMA((n,)), pltpu.SemaphoreType.DMA((n,))],
        compiler_params=pltpu.CompilerParams(vmem_limit_bytes=VMEM_LIMIT),
    )(*arrs)


PAIR_SWAP_PIECES = 2


def _pair_swap(gs, tag):
    n = len(gs)

    def body(*refs):
        ins, other = refs[:n], refs[n:2 * n]
        send_sems, recv_sems = refs[2 * n:]
        x, y, c = _coords()
        cps = []
        for a in range(n):
            h = gs[a].shape[1] // 2
            piece = h // PAIR_SWAP_PIECES
            for q in range(N_CHIPS):
                for r in range(PAIR_SWAP_PIECES):
                    k = (a * N_CHIPS + q) * PAIR_SWAP_PIECES + r
                    cp = _remote(ins[a].at[q, pl.ds((1 - c) * h + r * piece, piece)],
                                 other[a].at[q, pl.ds(r * piece, piece)], send_sems.at[k], recv_sems.at[k],
                                 (x, y, 1 - c))
                    cp.start()
                    cps.append(cp)
        for cp in cps:
            cp.wait()

    half = [jax.ShapeDtypeStruct((N_CHIPS, g.shape[1] // 2, g.shape[2]), g.dtype) for g in gs]
    nsem = n * N_CHIPS * PAIR_SWAP_PIECES
    return pl.pallas_call(
        body, name=f"grads_pair_swap_{tag}", in_specs=[HBM_SPEC] * n, out_specs=[HBM_SPEC] * n, out_shape=half,
        scratch_shapes=[pltpu.SemaphoreType.DMA((nsem,)), pltpu.SemaphoreType.DMA((nsem,))],
    )(*gs)


def _scatter_copies(ins, outs, send_sems, recv_sems):
    x, y, c = _coords()
    copies = []
    for a in range(len(ins)):
        for j, chip in enumerate(_other_chips(x, y)):
            q = 2 * chip[0] + chip[1]
            copies.append(_remote(ins[a].at[q], outs[a].at[j], send_sems.at[3 * a + j], recv_sems.at[3 * a + j],
                                  (*chip, c)))
    return copies


def _scatter_shapes(ps):
    return [jax.ShapeDtypeStruct((N_CHIPS - 1,) + a.shape[1:], a.dtype) for a in ps]


def _chip_scatter(ps, tag):
    n = len(ps)

    def body(*refs):
        copies = _scatter_copies(refs[:n], refs[n:2 * n], *refs[2 * n:])
        for cp in copies:
            cp.start()
        for cp in copies:
            cp.wait_recv()
        for cp in copies:
            cp.wait_send()

    return pl.pallas_call(
        body, name=f"grads_chip_scatter_{tag}", in_specs=[HBM_SPEC] * n, out_specs=[HBM_SPEC] * n,
        out_shape=_scatter_shapes(ps),
        scratch_shapes=[pltpu.SemaphoreType.DMA((3 * n,)), pltpu.SemaphoreType.DMA((3 * n,))],
    )(*ps)


def _pair_share(rs):
    n = len(rs)

    def body(*refs):
        ins, outs, stage = refs[:n], refs[n:2 * n], refs[2 * n:3 * n]
        send_sems, recv_sems, in_sems, out_sems = refs[3 * n:]
        x, y, c = _coords()

        def mine(a):
            h = rs[a].shape[0]
            return outs[a].at[pl.ds(c * h, h)]

        loads = [pltpu.make_async_copy(ins[a], stage[a], in_sems.at[a]) for a in range(n)]
        for cp in loads:
            cp.start()
        sends = [_remote(ins[a], mine(a), send_sems.at[a], recv_sems.at[a], (x, y, 1 - c)) for a in range(n)]
        for cp in sends:
            cp.start()
        stores = [pltpu.make_async_copy(stage[a], mine(a), out_sems.at[a]) for a in range(n)]
        for a in range(n):
            loads[a].wait()
            stores[a].start()
        for a in range(n):
            h = rs[a].shape[0]
            land = outs[a].at[pl.ds((1 - c) * h, h)]
            _remote(land, land, send_sems.at[a], recv_sems.at[a], (x, y, 1 - c)).wait_recv()
        for cp in sends:
            cp.wait_send()
        for cp in stores:
            cp.wait()

    return pl.pallas_call(
        body, name="grads_pair_share", in_specs=[HBM_SPEC] * n, out_specs=[HBM_SPEC] * n,
        out_shape=[jax.ShapeDtypeStruct((2 * a.shape[0], a.shape[1]), a.dtype) for a in rs],
        scratch_shapes=[pltpu.VMEM(a.shape, a.dtype) for a in rs] + [pltpu.SemaphoreType.DMA((n,))] * 4,
        compiler_params=pltpu.CompilerParams(vmem_limit_bytes=VMEM_LIMIT),
    )(*rs)


def _small_all_reduce(buf):
    R = buf.shape[0]
    ndev = 2 * N_CHIPS

    def body(in_ref, out_ref, gath, send_sems, recv_sems):
        x, y, c = _coords()
        me = 4 * x + 2 * y + c
        gath[me] = in_ref[...]
        peers = []
        for d in range(1, ndev):
            px = 1 - x if d & 4 else x
            py = 1 - y if d & 2 else y
            pc = 1 - c if d & 1 else c
            peers.append((px, py, pc))
        sends = []
        for d, peer in enumerate(peers):
            cp = _remote(in_ref, gath.at[me], send_sems.at[d], recv_sems.at[d], peer)
            cp.start()
            sends.append(cp)
        for d, peer in enumerate(peers):
            land = gath.at[4 * peer[0] + 2 * peer[1] + peer[2]]
            _remote(land, land, send_sems.at[d], recv_sems.at[d], peer).wait_recv()
        for cp in sends:
            cp.wait_send()
        acc = gath[0]
        for s in range(1, ndev):
            acc = acc + gath[s]
        out_ref[...] = acc

    return pl.pallas_call(
        body, name="small_all_reduce", in_specs=[VMEM_SPEC], out_specs=VMEM_SPEC,
        out_shape=jax.ShapeDtypeStruct(buf.shape, F32),
        scratch_shapes=[pltpu.VMEM((ndev, R, LANE), F32), pltpu.SemaphoreType.DMA((ndev - 1,)),
                        pltpu.SemaphoreType.DMA((ndev - 1,))],
    )(buf)


def _pair_add(name, own, other):
    h = own.shape[1]
    tm = _tile(h, (128, 64, 32, 16))

    def fn(i, a, b):
        return (a + b,)

    return _rowcall(name, fn, h, tm, [(own, "row", None), (other, "row", None)], [(own.shape, BF16, "row")])[0]


def _chip_add(name, own, parts):
    h = parts.shape[1]
    tm = _tile(h, (128, 64, 32, 16))

    def fn(i, o, a):
        a = a.astype(F32)
        return (((o.astype(F32) + a[0]) + a[1]) + a[2],)

    return _rowcall(name, fn, h, tm, [(own, "row", None), (parts, "row", None)], [(parts.shape[1:], F32, "row")])[0]


def _adamw(name, w, g, m, v):
    R = w.shape[0]
    tm = _tile(R, (256, 128, 64, 32, 16, 8))

    def fn(i, wv, gv, mv, vv):
        m2 = ADAM_B1 * mv + (1.0 - ADAM_B1) * gv
        v2 = ADAM_B2 * vv + (1.0 - ADAM_B2) * (gv * gv)
        m_hat = m2 / (1.0 - ADAM_B1 ** ADAM_STEP)
        v_hat = v2 / (1.0 - ADAM_B2 ** ADAM_STEP)
        delta = -ADAM_LR * (m_hat / (jnp.sqrt(v_hat) + ADAM_EPS) + ADAM_WD * wv)
        return delta, m2, v2

    ins = [(a, "row", None) for a in (w, g, m, v)]
    return _rowcall(name, fn, R, tm, ins, [(w.shape, F32, "row")] * 3)


def _pack(arrs):
    flat = jnp.concatenate([a.reshape(-1).astype(F32) for a in arrs])
    size = flat.shape[0]
    padded = -(-size // (SUBLANE * LANE)) * SUBLANE * LANE
    return jnp.pad(flat, (0, padded - size)).reshape(-1, LANE)


def _unpack(buf, shapes):
    flat = buf.reshape(-1)
    out, off = [], 0
    for s in shapes:
        size = math.prod(s)
        out.append(flat[off:off + size].reshape(s))
        off += size
    return out


BIG = ("a_w_in", "a_w_out", "w_kv", "b_w_q", "b_w_o", "ffn_w_up", "ffn_w_down")
WEIGHTS = ("a_norm_w", "a_w_in", "a_conv_w", "a_a_log", "a_dt_bias", "a_out_norm_w", "a_w_out", "kv_norm_w", "w_kv",
           "b_norm_w", "b_w_q", "b_sinks", "b_w_o", "rel_bias_table", "ffn_norm_w", "ffn_w_up", "ffn_conv_w",
           "ffn_conv_b", "ffn_w_down", "final_norm_w")
SMALL = tuple(n for n in WEIGHTS if n not in BIG)
SMALL_SHARDED = {"a_norm_w": 1, "a_conv_w": 2, "ffn_conv_w": 2}


def _quarter_2d(name, a):
    if name in ("ffn_w_up", "ffn_w_down"):
        return a.reshape(a.shape[0] * a.shape[1], a.shape[2])
    return a.reshape(a.shape[-2], a.shape[-1])


def _whole_weights(w):
    bigs = [_quarter_2d(n, w[n]).astype(BF16) for n in BIG]
    smalls = [w["a_norm_w"], w["a_conv_w"][0], w["ffn_conv_w"].reshape(6, DFF2_SHARD)]
    remote = [True] + [False] * (len(bigs) - 1) + [True] * len(smalls)
    g = _all_gather(bigs + smalls, [True] * len(bigs) + [False] * len(smalls), remote)
    w_in = g[0].transpose(1, 0, 2).reshape(D, GDN_IN)
    nh = GDN_V_HEADS
    zpad = jnp.zeros((D, LANE - nh), BF16)
    w_in_ba = jnp.concatenate([w_in[:, GDN_MAIN:GDN_MAIN + nh], zpad, w_in[:, GDN_MAIN + nh:], zpad], axis=1)
    lane_pad = lambda a: jnp.pad(a, ((0, 0), (0, LANE - nh)))
    early = dict(
        a_norm_w=g[7].reshape(1, D), w_in_main=w_in[:, :GDN_MAIN], w_in_ba=w_in_ba,
        a_conv_w=g[8].transpose(1, 0, 2).reshape(4, GDN_CONV), a_log=lane_pad(w["a_a_log"]),
        dt_bias=lane_pad(w["a_dt_bias"]), a_out_norm_w=w["a_out_norm_w"],
        kv_norm_w=w["kv_norm_w"].reshape(1, D), b_norm_w=w["b_norm_w"],
        sinks=jnp.broadcast_to(w["b_sinks"].reshape(SWA_KV_HEADS, SWA_GROUP, 1), (SWA_KV_HEADS, SWA_GROUP, LANE)),
        rel_table_t=w["rel_bias_table"].T, ffn_norm_w=w["ffn_norm_w"],
        ffn_conv_w=g[9].reshape(N_CHIPS, 2, 3, DFF2_SHARD).transpose(1, 2, 0, 3).reshape(2, 3, DFF2),
        ffn_conv_b=w["ffn_conv_b"], final_norm_w=w["final_norm_w"].reshape(1, D),
    )
    return early, (bigs[1:], g[1:len(bigs)])


def _late_weights(g):
    return dict(
        w_out=g[0].reshape(GDN_V, D), w_kv=g[1].reshape(D, 2 * SWA_KV_HEADS * SWA_HD), w_q=g[2].reshape(D, D),
        w_o=g[3].reshape(D, D), w_up=g[4].reshape(N_CHIPS, 2, D, DFF2_SHARD),
        w_down=g[5].reshape(N_CHIPS, 2, DFF_SHARD, D).transpose(1, 0, 2, 3).reshape(2, DFF, D),
    )


def _chip_major(name, g):
    if name == "a_w_in":
        return g.reshape(D, N_CHIPS, GDN_IN_SHARD).transpose(1, 0, 2)
    if name == "ffn_w_up":
        return g.reshape(N_CHIPS, 2 * D, DFF2_SHARD)
    if name == "ffn_w_down":
        return g.reshape(N_CHIPS, 2 * DFF_SHARD, D)
    return g.reshape(N_CHIPS, g.shape[0] // N_CHIPS, g.shape[1])


def kernel(x, a_norm_w, a_w_in, a_conv_w, a_a_log, a_dt_bias, a_out_norm_w, a_w_out, kv_norm_w, w_kv, b_norm_w, b_w_q, b_sinks, b_w_o, rel_bias_table, ffn_norm_w, ffn_w_up, ffn_conv_w, ffn_conv_b, ffn_w_down, final_norm_w, loss_target, m_a_norm_w, m_a_w_in, m_a_conv_w, m_a_a_log, m_a_dt_bias, m_a_out_norm_w, m_a_w_out, m_kv_norm_w, m_w_kv, m_b_norm_w, m_b_w_q, m_b_sinks, m_b_w_o, m_rel_bias_table, m_ffn_norm_w, m_ffn_w_up, m_ffn_conv_w, m_ffn_conv_b, m_ffn_w_down, m_final_norm_w, v_a_norm_w, v_a_w_in, v_a_conv_w, v_a_a_log, v_a_dt_bias, v_a_out_norm_w, v_a_w_out, v_kv_norm_w, v_w_kv, v_b_norm_w, v_b_w_q, v_b_sinks, v_b_w_o, v_rel_bias_table, v_ffn_norm_w, v_ffn_w_up, v_ffn_conv_w, v_ffn_conv_b, v_ffn_w_down, v_final_norm_w):
    w = dict(zip(WEIGHTS, (a_norm_w, a_w_in, a_conv_w, a_a_log, a_dt_bias, a_out_norm_w, a_w_out, kv_norm_w, w_kv,
                           b_norm_w, b_w_q, b_sinks, b_w_o, rel_bias_table, ffn_norm_w, ffn_w_up, ffn_conv_w,
                           ffn_conv_b, ffn_w_down, final_norm_w)))
    m = dict(zip(WEIGHTS, (m_a_norm_w, m_a_w_in, m_a_conv_w, m_a_a_log, m_a_dt_bias, m_a_out_norm_w, m_a_w_out,
                           m_kv_norm_w, m_w_kv, m_b_norm_w, m_b_w_q, m_b_sinks, m_b_w_o, m_rel_bias_table,
                           m_ffn_norm_w, m_ffn_w_up, m_ffn_conv_w, m_ffn_conv_b, m_ffn_w_down, m_final_norm_w)))
    v = dict(zip(WEIGHTS, (v_a_norm_w, v_a_w_in, v_a_conv_w, v_a_a_log, v_a_dt_bias, v_a_out_norm_w, v_a_w_out,
                           v_kv_norm_w, v_w_kv, v_b_norm_w, v_b_w_q, v_b_sinks, v_b_w_o, v_rel_bias_table,
                           v_ffn_norm_w, v_ffn_w_up, v_ffn_conv_w, v_ffn_conv_b, v_ffn_w_down, v_final_norm_w)))
    T = x.shape[1]
    chip = 2 * lax.axis_index("x") + lax.axis_index("y")

    core = lax.axis_index("c")

    def pair_sums(named, tag):
        names = [n for n in BIG if n in named]
        whole = [_chip_major(n, named[n]) for n in names]
        other = _pair_swap(whole, tag)
        own = [lax.dynamic_slice_in_dim(g, core * (g.shape[1] // 2), g.shape[1] // 2, 1) for g in whole]
        return [_pair_add(f"pair_add_{n}", a, b) for n, a, b in zip(names, own, other)]

    early, late = _whole_weights(w)
    loss_part, dx, grads, scattered = _local_step(x.reshape(T, D), loss_target.reshape(T, D), early, late, pair_sums)

    rest = [n for n in BIG if n not in scattered]
    pair = pair_sums({n: grads[n] for n in rest}, "late")
    scattered.update(zip(rest, zip(pair, _chip_scatter(pair, "late"))))
    halves = [_chip_add(f"chip_add_{n}", lax.dynamic_index_in_dim(scattered[n][0], chip, 0, keepdims=False),
                        scattered[n][1]) for n in BIG]
    quarter = _pair_share(halves)
    out_g, out_d, out_m, out_v = {}, {}, {}, {}
    for n, g2 in zip(BIG, quarter):
        res = _adamw(f"adamw_{n}", _quarter_2d(n, w[n]), g2, _quarter_2d(n, m[n]), _quarter_2d(n, v[n]))
        out_g[n] = g2.reshape(w[n].shape)
        out_d[n], out_m[n], out_v[n] = (r.reshape(w[n].shape) for r in res)

    whole = [grads[n] for n in SMALL]
    summed = _unpack(_small_all_reduce(_pack([loss_part[0:1, 0:1]] + whole)), [(1, 1)] + [a.shape for a in whole])
    loss = summed[0].reshape(())
    small_g = []
    for n, g in zip(SMALL, summed[1:]):
        if n in SMALL_SHARDED:
            axis = SMALL_SHARDED[n]
            g = g.reshape(w[n].shape[:axis] + (-1,) + w[n].shape[axis + 1:])
            size = w[n].shape[axis]
            g = lax.dynamic_slice_in_dim(g, chip * size, size, axis)
        small_g.append(g.reshape(w[n].shape))
    shapes = [w[n].shape for n in SMALL]
    res = _adamw("adamw_small", _pack([w[n] for n in SMALL]), _pack(small_g), _pack([m[n] for n in SMALL]),
                 _pack([v[n] for n in SMALL]))
    small_d, small_m, small_v = (_unpack(r, shapes) for r in res)
    for i, n in enumerate(SMALL):
        out_g[n], out_d[n], out_m[n], out_v[n] = small_g[i], small_d[i], small_m[i], small_v[i]

    return (loss, dx.reshape(x.shape), *[out_g[n] for n in WEIGHTS], *[out_d[n] for n in WEIGHTS],
            *[out_m[n] for n in WEIGHTS], *[out_v[n] for n in WEIGHTS])
```

```python
import functools
import math

import jax
import jax.numpy as jnp
from jax import lax
from jax.experimental import pallas as pl
from jax.experimental.pallas import tpu as pltpu

F32 = jnp.float32
BF16 = jnp.bfloat16
MESH = pl.DeviceIdType.MESH
HIGHEST = lax.Precision.HIGHEST

D = 1024
EPS = 1e-6
NEG_INF = -1e30
N_CHIPS = 4

GDN_QK_HEADS = 8
GDN_V_HEADS = 16
GDN_HD = 128
GDN_QK = GDN_QK_HEADS * GDN_HD
GDN_V = GDN_V_HEADS * GDN_HD
GDN_CONV = 2 * GDN_QK + GDN_V
GDN_MAIN = GDN_CONV + GDN_V
GDN_IN = GDN_MAIN + 2 * GDN_V_HEADS
GDN_IN_SHARD = GDN_IN // N_CHIPS
GDN_CHUNK = 64

SWA_Q_HEADS = 16
SWA_KV_HEADS = 4
SWA_GROUP = 4
SWA_HD = 64
SWA_BLOCK = 128
REL_BUCKETS = 32
REL_MAX_DISTANCE = 128

DFF = 2816
DFF2 = 2 * DFF
DFF2_SHARD = DFF2 // N_CHIPS
DFF_SHARD = DFF // N_CHIPS

ADAM_LR = 0.001
ADAM_B1 = 0.9
ADAM_B2 = 0.999
ADAM_EPS = 1e-08
ADAM_WD = 0.01
ADAM_STEP = 10

LANE = 128
SUBLANE = 8
VMEM_LIMIT = 56 * 1024 * 1024


def _params(sem, vmem=VMEM_LIMIT):
    return pltpu.CompilerParams(dimension_semantics=sem, vmem_limit_bytes=vmem)


def _rowcall(name, fn, T, tm, ins, outs):
    n = T // tm
    r8 = tm // SUBLANE
    last8 = T // SUBLANE - 1
    arrays, in_specs = [], []
    for arr, kind, cols in ins:
        arrays.append(arr)
        if kind == "full":
            in_specs.append(pl.BlockSpec(arr.shape, functools.partial(lambda nd, i: (0,) * nd, arr.ndim)))
        elif arr.ndim == 2:
            w, ci = cols if cols is not None else (arr.shape[1], 0)
            if kind == "row":
                in_specs.append(pl.BlockSpec((tm, w), functools.partial(lambda ci, i: (i, ci), ci)))
            elif kind == "prev":
                in_specs.append(pl.BlockSpec(
                    (SUBLANE, w), functools.partial(lambda ci, i: (jnp.maximum(i * r8 - 1, 0), ci), ci)))
            else:
                in_specs.append(pl.BlockSpec(
                    (SUBLANE, w), functools.partial(lambda ci, i: (jnp.minimum((i + 1) * r8, last8), ci), ci)))
        else:
            lead = arr.shape[:-2]
            in_specs.append(pl.BlockSpec(lead + (tm, arr.shape[-1]),
                                         functools.partial(lambda nl, i: (0,) * nl + (i, 0), len(lead))))
    out_shape, out_specs = [], []
    for shape, dtype, kind in outs:
        out_shape.append(jax.ShapeDtypeStruct(shape, dtype))
        if kind == "acc":
            out_specs.append(pl.BlockSpec(shape, functools.partial(lambda nd, i: (0,) * nd, len(shape))))
        else:
            lead = shape[:-2]
            out_specs.append(pl.BlockSpec(lead + (tm, shape[-1]),
                                          functools.partial(lambda nl, i: (0,) * nl + (i, 0), len(lead))))
    nin = len(arrays)

    def body(*refs):
        i = pl.program_id(0)
        vals = [r[...] for r in refs[:nin]]
        res = fn(i, *vals)
        for (shape, dtype, kind), o, r in zip(outs, refs[nin:], res):
            if kind == "row":
                o[...] = r.astype(dtype)
            else:
                @pl.when(i == 0)
                def _():
                    o[...] = r.astype(dtype)

                @pl.when(i > 0)
                def _():
                    o[...] += r.astype(dtype)

    res = pl.pallas_call(
        body, name=name, grid=(n,), in_specs=in_specs, out_specs=out_specs, out_shape=out_shape,
        compiler_params=_params(("arbitrary",)),
    )(*arrays)
    return res


def _mm(name, a, b, out_shape, out_dtype, grid, a_spec, b_spec, o_spec, dims, acc_shape, res=None, precision=None,
        into=None):
    nk = grid[2]
    n_in = 2 + (res is not None) + (into is not None)

    def body(*refs):
        a_ref, b_ref, o_ref = refs[0], refs[1], refs[n_in]
        r_ref = refs[2] if res is not None else None
        av, bv = a_ref[...], b_ref[...]
        if precision is None:
            av, bv = av.astype(BF16), bv.astype(BF16)
        p = lax.dot_general(av, bv, (dims, ((), ())), preferred_element_type=F32, precision=precision)

        def finish(x):
            if res is not None:
                x = x + r_ref[...].astype(F32)
            o_ref[...] = x.astype(out_dtype).reshape(o_ref.shape)

        if nk == 1:
            finish(p)
        else:
            acc = refs[-1]
            k = pl.program_id(2)

            @pl.when(k == 0)
            def _():
                acc[...] = p

            @pl.when(k > 0)
            def _():
                acc[...] += p

            @pl.when(k == nk - 1)
            def _():
                finish(acc[...])

    ops = [a, b] + ([res] if res is not None else []) + ([into] if into is not None else [])
    specs = [a_spec, b_spec] + ([o_spec] if res is not None else [])
    specs += [pl.BlockSpec(memory_space=pl.ANY)] if into is not None else []
    return pl.pallas_call(
        body, name=name, grid=grid, in_specs=specs, out_specs=o_spec,
        out_shape=jax.ShapeDtypeStruct(out_shape, out_dtype),
        input_output_aliases={n_in - 1: 0} if into is not None else {},
        scratch_shapes=[pltpu.VMEM(acc_shape, F32)] if nk > 1 else [],
        compiler_params=_params(("parallel", "parallel", "arbitrary")),
    )(*ops)


NN = ((1,), (0,))
NT = ((1,), (1,))
TN = ((0,), (0,))


BIG_TILES = (1024, 512, 256, 128)


def _tile(n, pref):
    for t in pref:
        if n % t == 0:
            return t
    return n


def _mm_nn(name, a, w, out_dtype, res=None, precision=None):
    M, K = a.shape
    N = w.shape[1]
    tm = _tile(M, BIG_TILES if K <= 2048 else BIG_TILES[1:])
    tn = _tile(N, BIG_TILES)
    return _mm(name, a, w, (M, N), out_dtype, (M // tm, N // tn, 1),
               pl.BlockSpec((tm, K), lambda i, j, k: (i, 0)), pl.BlockSpec((K, tn), lambda i, j, k: (0, j)),
               pl.BlockSpec((tm, tn), lambda i, j, k: (i, j)), NN, (tm, tn), res=res, precision=precision)


def _mm_nt(name, g, w, out_dtype, res=None, precision=None):
    M, N = g.shape
    K = w.shape[0]
    tm, tk = _tile(M, BIG_TILES), _tile(K, (1024, 1408, 512, 256, 128))
    tn = _tile(N, (1536,) + BIG_TILES)
    return _mm(name, g, w, (M, K), out_dtype, (M // tm, K // tk, N // tn),
               pl.BlockSpec((tm, tn), lambda i, j, k: (i, k)), pl.BlockSpec((tk, tn), lambda i, j, k: (j, k)),
               pl.BlockSpec((tm, tk), lambda i, j, k: (i, j)), NT, (tm, tk), res=res, precision=precision)


def _mm_tn(name, a, g, out_dtype=F32, precision=None):
    T, K = a.shape
    N = g.shape[1]
    tk, tn = _tile(K, (1024, 1408, 512, 256, 128)), _tile(N, BIG_TILES)
    tt = _tile(T, BIG_TILES)
    return _mm(name, a, g, (K, N), out_dtype, (K // tk, N // tn, T // tt),
               pl.BlockSpec((tt, tk), lambda i, j, k: (k, i)), pl.BlockSpec((tt, tn), lambda i, j, k: (k, j)),
               pl.BlockSpec((tk, tn), lambda i, j, k: (i, j)), TN, (tk, tn), precision=precision)


def _mm_up(name, n, wup, layer):
    T = n.shape[0]
    tm = _tile(T, BIG_TILES)
    return _mm(name, n, wup, (T, DFF2), BF16, (T // tm, N_CHIPS, 1),
               pl.BlockSpec((tm, D), lambda i, j, k: (i, 0)),
               pl.BlockSpec((None, None, D, DFF2_SHARD), lambda i, j, k: (j, layer, 0, 0)),
               pl.BlockSpec((tm, DFF2_SHARD), lambda i, j, k: (i, j)), NN, (tm, DFF2_SHARD))


def _mm_up_nt(name, du, wup, layer):
    T = du.shape[0]
    tm, tk = _tile(T, BIG_TILES), D
    return _mm(name, du, wup, (T, D), F32, (T // tm, D // tk, N_CHIPS),
               pl.BlockSpec((tm, DFF2_SHARD), lambda i, j, k: (i, k)),
               pl.BlockSpec((None, None, tk, DFF2_SHARD), lambda i, j, k: (k, layer, j, 0)),
               pl.BlockSpec((tm, tk), lambda i, j, k: (i, j)), NT, (tm, tk))


def _mm_up_tn(name, n, du, layer, into):
    T = n.shape[0]
    tk, tt = D, _tile(T, BIG_TILES)
    return _mm(name, n, du, (N_CHIPS, 2, D, DFF2_SHARD), F32, (D // tk, N_CHIPS, T // tt),
               pl.BlockSpec((tt, tk), lambda i, j, k: (k, i)), pl.BlockSpec((tt, DFF2_SHARD), lambda i, j, k: (k, j)),
               pl.BlockSpec((None, None, tk, DFF2_SHARD), lambda i, j, k: (j, layer, i, 0)), TN, (tk, DFF2_SHARD),
               into=into)


def _mm_down_tn(name, act, dout, layer, into):
    T = act.shape[0]
    tk, tn, tt = 2 * DFF_SHARD, _tile(D, BIG_TILES), _tile(T, BIG_TILES)
    return _mm(name, act, dout, (2, 2, 2, DFF_SHARD, D), F32, (DFF // tk, D // tn, T // tt),
               pl.BlockSpec((tt, tk), lambda i, j, k: (k, i)), pl.BlockSpec((tt, tn), lambda i, j, k: (k, j)),
               pl.BlockSpec((None, 2, None, DFF_SHARD, tn), lambda i, j, k: (i, 0, layer, 0, j)), TN, (tk, tn),
               into=into)


def _sigmoid(x):
    return 0.5 * jnp.tanh(0.5 * x) + 0.5


def _silu(x):
    return x * _sigmoid(x)


def _softplus(x):
    return jnp.maximum(x, 0.0) + jnp.log(1.0 + jnp.exp(-jnp.abs(x)))


def _rms_core(h, w):
    return h * lax.rsqrt(jnp.mean(h * h, axis=-1, keepdims=True) + EPS) * w


def _shift_down(x, halo, s, i):
    if s == 0:
        return x
    tm = x.shape[0]
    rolled = pltpu.roll(x, s, 0)
    patch = pltpu.roll(jnp.where(i == 0, 0.0, halo), s, 0)
    row = lax.broadcasted_iota(jnp.int32, patch.shape, 0)
    top = jnp.where(row < s, patch, rolled[:SUBLANE])
    return jnp.concatenate([top, rolled[SUBLANE:]], axis=0) if tm > SUBLANE else top


def _shift_up(x, halo, s, i, n):
    if s == 0:
        return x
    tm = x.shape[0]
    rolled = pltpu.roll(x, tm - s, 0)
    patch = pltpu.roll(jnp.where(i == n - 1, 0.0, halo), SUBLANE - s, 0)
    row = lax.broadcasted_iota(jnp.int32, patch.shape, 0)
    bottom = jnp.where(row >= SUBLANE - s, patch, rolled[tm - SUBLANE:])
    return jnp.concatenate([rolled[:tm - SUBLANE], bottom], axis=0) if tm > SUBLANE else bottom


def _taps(x, halo, K, i):
    return [_shift_down(x, halo, K - 1 - j, i) for j in range(K)]


def _conv_fwd(taps, w):
    y = w[0:1, :] * taps[0]
    for j in range(1, len(taps)):
        y = y + w[j:j + 1, :] * taps[j]
    return y


def _conv_dx(dy, halo_next, w, i, n):
    K = w.shape[0]
    dx = w[K - 1:K, :] * dy
    for j in range(K - 1):
        dx = dx + w[j:j + 1, :] * _shift_up(dy, halo_next, K - 1 - j, i, n)
    return dx


def _conv_dw(dy, taps):
    rows = [jnp.sum(dy * tap, axis=0, keepdims=True) for tap in taps]
    return jnp.concatenate(rows + [jnp.zeros((SUBLANE - len(taps), dy.shape[1]), F32)], axis=0)


def _rms_fwd(name, h, w, tm=512):
    T = h.shape[0]
    tm = min(tm, T)

    def fn(i, hv, wv):
        return (_rms_core(hv, wv),)

    return _rowcall(name, fn, T, tm, [(h, "row", None), (w, "full", None)], [((T, D), BF16, "row")])[0]


def _rms_bwd(name, h, pairs, adds, tm=256):
    T = h.shape[0]
    tm = min(tm, T)
    npair, nadd = len(pairs), len(adds)

    def fn(i, hv, *rest):
        ws, dns, ads = rest[:npair], rest[npair:2 * npair], rest[2 * npair:]
        dh = None
        dws = []
        for wv, dn in zip(ws, dns):
            _, vjp = jax.vjp(_rms_core, hv, wv)
            dhi, dwi = vjp(dn.astype(F32))
            dh = dhi if dh is None else dh + dhi
            dws.append(dwi)
        for a in ads:
            dh = dh + a.astype(F32)
        return (dh, *dws)

    ins = [(h, "row", None)] + [(w, "full", None) for w, _ in pairs] + [(dn, "row", None) for _, dn in pairs]
    ins += [(a, "row", None) for a in adds]
    outs = [((T, D), F32, "row")] + [((1, D), F32, "acc")] * npair
    return _rowcall(name, fn, T, tm, ins, outs)


def _l2(x):
    return x * lax.rsqrt(jnp.sum(x * x, axis=-1, keepdims=True) + EPS)


def _gdn_post_core(yq, yk, yv, pb, pa, a_log, dtb):
    qn = tuple(_l2(_silu(a)) * (GDN_HD ** -0.5) for a in yq)
    kn = tuple(_l2(_silu(a)) for a in yk)
    v = _silu(yv)
    beta = _sigmoid(pb)
    g = -jnp.exp(a_log) * _softplus(pa + dtb)
    return qn, kn, v, beta, g


def _heads(x, n):
    return tuple(x[:, GDN_HD * h:GDN_HD * (h + 1)] for h in range(n))


def _gdn_pre_fwd(pm, pba, conv_w, a_log, dtb, tm=128):
    T = pm.shape[0]
    tm = min(tm, T)

    def fn(i, x, halo, pbav, cw, al, db):
        y = _conv_fwd(_taps(x.astype(F32), halo.astype(F32), 4, i), cw)
        qn, kn, v, beta, g = _gdn_post_core(_heads(y[:, :GDN_QK], 8), _heads(y[:, GDN_QK:2 * GDN_QK], 8),
                                            y[:, 2 * GDN_QK:], pbav[:, :LANE], pbav[:, LANE:], al, db)
        return jnp.stack(qn), jnp.stack(kn), jnp.stack(_heads(v, GDN_V_HEADS)), beta, g

    ins = [(pm, "row", (GDN_CONV, 0)), (pm, "prev", (GDN_CONV, 0)), (pba, "row", None),
           (conv_w, "full", None), (a_log, "full", None), (dtb, "full", None)]
    outs = [((GDN_QK_HEADS, T, GDN_HD), BF16, "row"), ((GDN_QK_HEADS, T, GDN_HD), BF16, "row"),
            ((GDN_V_HEADS, T, GDN_HD), BF16, "row"), ((T, LANE), F32, "row"), ((T, LANE), F32, "row")]
    return _rowcall("gdn_pre_fwd", fn, T, tm, ins, outs)


def _gdn_pre_bwd(pm, pba, conv_w, a_log, dtb, dqn, dkn, dv, dbeta, dg, tm=128):
    T = pm.shape[0]
    tm = min(tm, T)

    def fn(i, x, halo, pbav, cw, al, db, dqv, dkv, dvv, dbv, dgv):
        taps = _taps(x.astype(F32), halo.astype(F32), 4, i)
        y = _conv_fwd(taps, cw)
        prim = (_heads(y[:, :GDN_QK], 8), _heads(y[:, GDN_QK:2 * GDN_QK], 8), y[:, 2 * GDN_QK:],
                pbav[:, :LANE], pbav[:, LANE:], al, db)
        _, vjp = jax.vjp(_gdn_post_core, *prim)
        cot = (tuple(dqv[h].astype(F32) for h in range(8)), tuple(dkv[h].astype(F32) for h in range(8)),
               jnp.concatenate([dvv[h].astype(F32) for h in range(GDN_V_HEADS)], axis=1), dbv, dgv)
        dyq, dyk, dyv, dpb, dpa, dal, ddb = vjp(cot)
        dy = jnp.concatenate(list(dyq) + list(dyk) + [dyv], axis=1)
        dcw = _conv_dw(dy, taps)
        return dy, jnp.concatenate([dpb, dpa], axis=1), dcw, dal, ddb

    ins = [(pm, "row", (GDN_CONV, 0)), (pm, "prev", (GDN_CONV, 0)), (pba, "row", None),
           (conv_w, "full", None), (a_log, "full", None), (dtb, "full", None),
           (dqn, "row", None), (dkn, "row", None), (dv, "row", None), (dbeta, "row", None), (dg, "row", None)]
    outs = [((T, GDN_CONV), BF16, "row"), ((T, 2 * LANE), F32, "row"), ((SUBLANE, GDN_CONV), F32, "acc"),
            ((1, LANE), F32, "acc"), ((1, LANE), F32, "acc")]
    return _rowcall("gdn_pre_bwd", fn, T, tm, ins, outs)


def _gdn_conv_bwd(dy, dz, conv_w, tm=256):
    T = dy.shape[0]
    tm = min(tm, T)
    n = T // tm

    def fn(i, dyv, halo, dzv, cw):
        dx = _conv_dx(dyv.astype(F32), halo.astype(F32), cw, i, n)
        return (jnp.concatenate([dx.astype(BF16), dzv.astype(BF16)], axis=1),)

    ins = [(dy, "row", None), (dy, "next", None), (dz, "row", None), (conv_w, "full", None)]
    return _rowcall("gdn_conv_bwd", fn, T, tm, ins, [((T, GDN_MAIN), BF16, "row")])[0]


def _bdot(a, b, dims=NN):
    return lax.dot_general(a.astype(BF16), b.astype(BF16), (dims, ((), ())), preferred_element_type=F32)


BNN = ((2,), (1,))
BNT = ((2,), (2,))
BTN = ((1,), (1,))


def _bmm(a, b, dims=BNN):
    return lax.dot_general(a.astype(BF16), b.astype(BF16), (dims, ((0,), (0,))), preferred_element_type=F32)


def _bmm3(a, b):
    ah, bh = a.astype(BF16), b.astype(BF16)
    al, bl = (a - ah.astype(F32)).astype(BF16), (b - bh.astype(F32)).astype(BF16)
    dn = (BNN, ((0,), (0,)))
    return (lax.dot_general(ah, bh, dn, preferred_element_type=F32)
            + lax.dot_general(al, bh, dn, preferred_element_type=F32)
            + lax.dot_general(ah, bl, dn, preferred_element_type=F32))


def _tri_inv(m):
    C = m.shape[-1]
    r = lax.broadcasted_iota(jnp.int32, (C, C), 0)
    c = lax.broadcasted_iota(jnp.int32, (C, C), 1)
    t = jnp.where(r == c, 1.0, 0.0) - m
    pw = _bmm3(m, m)
    t = t + _bmm3(t, pw)
    for _ in range(int(math.log2(C)) - 2):
        pw = _bmm(pw, pw)
        t = t + _bmm(t, pw)
    return t


def _tri_inv_vjp(t, dt):
    tt = jnp.swapaxes(t, 1, 2)
    return -_bmm(_bmm(tt, dt), tt)


def _twice(a):
    return jnp.broadcast_to(a[:, None], (a.shape[0], 2) + a.shape[1:]).reshape((2 * a.shape[0],) + a.shape[1:])


def _gdn_gates(grow, brow):
    C = grow.shape[2]
    r = lax.broadcasted_iota(jnp.int32, (C, C), 0)
    c = lax.broadcasted_iota(jnp.int32, (C, C), 1)
    tril, eye = r >= c, r == c
    gcol = jnp.sum(jnp.where(eye, grow, 0.0), axis=2, keepdims=True)
    bcol = jnp.sum(jnp.where(eye, brow, 0.0), axis=2, keepdims=True)
    gc_col = jnp.sum(jnp.where(tril, grow, 0.0), axis=2, keepdims=True)
    gc_row = jnp.sum(jnp.where(r <= c, gcol, 0.0), axis=1, keepdims=True)
    gc_last = jnp.sum(grow, axis=2, keepdims=True)
    decay = jnp.where(tril, jnp.exp(jnp.where(tril, gc_col - gc_row, 0.0)), 0.0)
    return bcol, gc_col, gc_last, decay


def _gdn_m(k, grow, brow):
    C = k.shape[1]
    strict = lax.broadcasted_iota(jnp.int32, (C, C), 0) > lax.broadcasted_iota(jnp.int32, (C, C), 1)
    bcol, _, _, decay = _gdn_gates(grow, brow)
    return jnp.where(strict, bcol * _twice(_bmm(k, k, BNT)) * decay, 0.0)


def _gdn_rest(q, k, v, grow, brow, t_mat, S):
    bcol, gc_col, gc_last, decay = _gdn_gates(grow, brow)
    qk = _twice(_bmm(q, k, BNT))
    k2, q2 = _twice(k), _twice(q)
    egc = jnp.exp(gc_col)
    u = _bmm(t_mat, v * bcol)
    w = _bmm(t_mat, k2 * (bcol * egc))
    v_new = u - _bmm(w, S)
    o = _bmm(q2 * egc, S) + _bmm(qk * decay, v_new)
    s_new = S * jnp.exp(gc_last) + _bmm(k2 * jnp.exp(gc_last - gc_col), v_new, BTN)
    return o, s_new


def _gdn_tb(T):
    return min(256, T)


def _gate_rows(g):
    T = g.shape[0]
    g = g[:, :GDN_V_HEADS].reshape(T // GDN_CHUNK, GDN_CHUNK, GDN_V_HEADS)
    return g.transpose(0, 2, 1)[:, :, None, :]


def _gate_cols(g):
    nc = g.shape[0]
    g = g[:, :, 0, :].transpose(0, 2, 1).reshape(nc * GDN_CHUNK, GDN_V_HEADS)
    return jnp.pad(g, ((0, 0), (0, LANE - GDN_V_HEADS)))


def _gdn_fwd(qn, kn, v, g, beta, gather=None):
    T = qn.shape[1]
    tb = _gdn_tb(T)
    nc = tb // GDN_CHUNK
    nsteps = T // tb
    quarters, buffers = gather if gather is not None else ((), ())
    ng = len(quarters)
    shapes = [a.shape for a in quarters]
    splits = [True] * ng

    def body(*refs):
        q_ref, k_ref, v_ref, g_ref, b_ref = refs[:5]
        src = refs[5:5 + ng]
        o_ref, sall_ref, tall_ref = refs[5 + 2 * ng:8 + 2 * ng]
        dst = refs[8 + 2 * ng:8 + 3 * ng]
        s_scr = refs[8 + 3 * ng]
        step = pl.program_id(0)

        @pl.when(step == 0)
        def _():
            s_scr[...] = jnp.zeros(s_scr.shape, F32)
            if ng:
                for cp in _gather_copies(shapes, splits, src, dst, *refs[9 + 3 * ng:])[0]:
                    cp.start()

        def chunk(ci, carry):
            rows = pl.ds(pl.multiple_of(ci * GDN_CHUNK, GDN_CHUNK), GDN_CHUNK)
            s = s_scr[...]
            sall_ref[ci] = s
            q, k = q_ref[:, rows, :].astype(F32), k_ref[:, rows, :].astype(F32)
            t_mat = _tri_inv(_gdn_m(k, g_ref[ci], b_ref[ci])).astype(BF16)
            tall_ref[ci] = t_mat
            o, s_new = _gdn_rest(q, k, v_ref[:, rows, :].astype(F32), g_ref[ci], b_ref[ci], t_mat.astype(F32), s)
            o_ref[:, rows, :] = o.astype(o_ref.dtype)
            s_scr[...] = s_new
            return carry

        lax.fori_loop(0, nc, chunk, 0)

        if ng:
            @pl.when(step == nsteps - 1)
            def _():
                _gather_arrival(shapes, splits, src, dst, *refs[9 + 3 * ng:])

    qk_spec = pl.BlockSpec((GDN_QK_HEADS, tb, GDN_HD), lambda i: (0, i, 0))
    v_spec = pl.BlockSpec((GDN_V_HEADS, tb, GDN_HD), lambda i: (0, i, 0))
    g_spec = pl.BlockSpec((nc, GDN_V_HEADS, 1, GDN_CHUNK), lambda i: (i, 0, 0, 0))
    anywhere = pl.BlockSpec(memory_space=pl.ANY)
    return pl.pallas_call(
        body, name="gdn_fwd", grid=(nsteps,),
        in_specs=[qk_spec, qk_spec, v_spec, g_spec, g_spec] + [anywhere] * (2 * ng),
        out_specs=[v_spec, pl.BlockSpec((nc, GDN_V_HEADS, GDN_HD, GDN_HD), lambda i: (i, 0, 0, 0)),
                   pl.BlockSpec((nc, GDN_V_HEADS, GDN_CHUNK, GDN_CHUNK), lambda i: (i, 0, 0, 0))] + [anywhere] * ng,
        out_shape=[jax.ShapeDtypeStruct((GDN_V_HEADS, T, GDN_HD), BF16),
                   jax.ShapeDtypeStruct((T // GDN_CHUNK, GDN_V_HEADS, GDN_HD, GDN_HD), F32),
                   jax.ShapeDtypeStruct((T // GDN_CHUNK, GDN_V_HEADS, GDN_CHUNK, GDN_CHUNK), BF16)]
        + [jax.ShapeDtypeStruct(b.shape, b.dtype) for b in buffers],
        input_output_aliases={5 + ng + a: 3 + a for a in range(ng)},
        scratch_shapes=[pltpu.VMEM((GDN_V_HEADS, GDN_HD, GDN_HD), F32)]
        + ([pltpu.SemaphoreType.DMA((6 * ng,)), pltpu.SemaphoreType.DMA((6 * ng,))] if ng else []),
        compiler_params=_params(("arbitrary",)),
    )(qn, kn, v, g, beta, *quarters, *buffers)


def _gdn_bwd(qn, kn, v, g, beta, sall, tall, do, scatter=()):
    T = qn.shape[1]
    tb = _gdn_tb(T)
    nc = tb // GDN_CHUNK
    nb = T // tb
    ns = len(scatter)

    def body(*refs):
        q_ref, k_ref, v_ref, g_ref, b_ref, sall_ref, tall_ref, do_ref = refs[:8]
        dq_ref, dk_ref, dv_ref, dg_ref, db_ref = refs[8 + ns:13 + ns]
        ds_scr = refs[13 + 2 * ns]
        comm = (refs[8:8 + ns], refs[13 + ns:13 + 2 * ns], *refs[14 + 2 * ns:])
        step = pl.program_id(0)

        @pl.when(step == 0)
        def _():
            ds_scr[...] = jnp.zeros(ds_scr.shape, F32)
            if ns:
                for cp in _scatter_copies(*comm):
                    cp.start()

        def chunk(cr, carry):
            ci = nc - 1 - cr
            rows = pl.ds(pl.multiple_of(ci * GDN_CHUNK, GDN_CHUNK), GDN_CHUNK)
            k, t_mat = k_ref[:, rows, :].astype(F32), tall_ref[ci].astype(F32)
            _, vjp = jax.vjp(_gdn_rest, q_ref[:, rows, :].astype(F32), k, v_ref[:, rows, :].astype(F32),
                             g_ref[ci], b_ref[ci], t_mat, sall_ref[ci])
            dq, dk, dv, dg, db, dt, ds = vjp((do_ref[:, rows, :].astype(F32), ds_scr[...]))
            _, vjp_m = jax.vjp(_gdn_m, k, g_ref[ci], b_ref[ci])
            dk_m, dg_m, db_m = vjp_m(_tri_inv_vjp(t_mat, dt))
            ds_scr[...] = ds
            dq_ref[:, rows, :] = dq
            dk_ref[:, rows, :] = dk + dk_m
            dv_ref[:, rows, :] = dv
            dg_ref[ci] = dg + dg_m
            db_ref[ci] = db + db_m
            return carry

        lax.fori_loop(0, nc, chunk, 0)

        if ns:
            @pl.when(step == nb - 1)
            def _():
                copies = _scatter_copies(*comm)
                for cp in copies:
                    cp.wait_recv()
                for cp in copies:
                    cp.wait_send()

    qk_spec = pl.BlockSpec((GDN_QK_HEADS, tb, GDN_HD), lambda i: (0, nb - 1 - i, 0))
    v_spec = pl.BlockSpec((GDN_V_HEADS, tb, GDN_HD), lambda i: (0, nb - 1 - i, 0))
    g_spec = pl.BlockSpec((nc, GDN_V_HEADS, 1, GDN_CHUNK), lambda i: (nb - 1 - i, 0, 0, 0))
    s_spec = pl.BlockSpec((nc, GDN_V_HEADS, GDN_HD, GDN_HD), lambda i: (nb - 1 - i, 0, 0, 0))
    t_spec = pl.BlockSpec((nc, GDN_V_HEADS, GDN_CHUNK, GDN_CHUNK), lambda i: (nb - 1 - i, 0, 0, 0))
    anywhere = pl.BlockSpec(memory_space=pl.ANY)
    return pl.pallas_call(
        body, name="gdn_bwd", grid=(nb,),
        in_specs=[qk_spec, qk_spec, v_spec, g_spec, g_spec, s_spec, t_spec, v_spec] + [anywhere] * ns,
        out_specs=[qk_spec, qk_spec, v_spec, g_spec, g_spec] + [anywhere] * ns,
        out_shape=[jax.ShapeDtypeStruct((GDN_QK_HEADS, T, GDN_HD), F32),
                   jax.ShapeDtypeStruct((GDN_QK_HEADS, T, GDN_HD), F32),
                   jax.ShapeDtypeStruct((GDN_V_HEADS, T, GDN_HD), F32),
                   jax.ShapeDtypeStruct(g.shape, F32), jax.ShapeDtypeStruct(g.shape, F32)]
        + _scatter_shapes(scatter),
        scratch_shapes=[pltpu.VMEM((GDN_V_HEADS, GDN_HD, GDN_HD), F32)]
        + ([pltpu.SemaphoreType.DMA((3 * ns,)), pltpu.SemaphoreType.DMA((3 * ns,))] if ns else []),
        compiler_params=_params(("arbitrary",)),
    )(qn, kn, v, g, beta, sall, tall, do, *scatter)


def _gnorm_core(o, z, w):
    return tuple(_rms_core(oh, w) * _silu(zh) for oh, zh in zip(o, z))


def _gnorm_fwd(o, pm, w, tm=256):
    T = pm.shape[0]
    tm = min(tm, T)

    def fn(i, ov, zv, wv):
        zf = zv.astype(F32)
        out = _gnorm_core(tuple(ov[h].astype(F32) for h in range(GDN_V_HEADS)), _heads(zf, GDN_V_HEADS), wv)
        return (jnp.concatenate(out, axis=1),)

    ins = [(o, "row", None), (pm, "row", (GDN_V, 2)), (w, "full", None)]
    return _rowcall("gnorm_fwd", fn, T, tm, ins, [((T, GDN_V), BF16, "row")])[0]


def _gnorm_bwd(o, pm, w, don, tm=128):
    T = pm.shape[0]
    tm = min(tm, T)

    def fn(i, ov, zv, wv, dv):
        zf, df = zv.astype(F32), dv.astype(F32)
        _, vjp = jax.vjp(_gnorm_core, tuple(ov[h].astype(F32) for h in range(GDN_V_HEADS)),
                         _heads(zf, GDN_V_HEADS), wv)
        do, dz, dw = vjp(_heads(df, GDN_V_HEADS))
        return jnp.stack(do), jnp.concatenate(dz, axis=1), dw

    ins = [(o, "row", None), (pm, "row", (GDN_V, 2)), (w, "full", None), (don, "row", None)]
    outs = [((GDN_V_HEADS, T, GDN_HD), BF16, "row"), ((T, GDN_V), BF16, "row"), ((1, GDN_HD), F32, "acc")]
    return _rowcall("gnorm_bwd", fn, T, tm, ins, outs)


def _ffn_act_fwd(name, up, conv_w, conv_b, tm=128):
    T = up.shape[0]
    tm = min(tm, T)

    def fn(i, x, halo, cw, cb):
        u = _conv_fwd(_taps(x.astype(F32), halo.astype(F32), 3, i), cw) + cb
        return (_silu(u[:, :DFF]) * u[:, DFF:],)

    ins = [(up, "row", None), (up, "prev", None), (conv_w, "full", None), (conv_b, "full", None)]
    return _rowcall(name, fn, T, tm, ins, [((T, DFF), BF16, "row")])[0]


def _ffn_act_bwd(name, up, conv_w, conv_b, dact, tm=128):
    T = up.shape[0]
    tm = min(tm, T)

    def fn(i, x, halo, cw, cb, da):
        taps = _taps(x.astype(F32), halo.astype(F32), 3, i)
        da = da.astype(F32)
        u = _conv_fwd(taps, cw) + cb
        gate, val = u[:, :DFF], u[:, DFF:]
        sg = _sigmoid(gate)
        dgate = da * val * sg * (1.0 + gate * (1.0 - sg))
        dval = da * gate * sg
        du = jnp.concatenate([dgate, dval], axis=1)
        return du, _conv_dw(du, taps), jnp.sum(du, axis=0, keepdims=True)

    ins = [(up, "row", None), (up, "prev", None), (conv_w, "full", None), (conv_b, "full", None),
           (dact, "row", None)]
    outs = [((T, DFF2), BF16, "row"), ((SUBLANE, DFF2), F32, "acc"), ((1, DFF2), F32, "acc")]
    return _rowcall(name, fn, T, tm, ins, outs)


def _ffn_conv_bwd(name, du, conv_w, tm=256):
    T = du.shape[0]
    tm = min(tm, T)
    n = T // tm

    def fn(i, dv, halo, cw):
        return (_conv_dx(dv.astype(F32), halo.astype(F32), cw, i, n),)

    ins = [(du, "row", None), (du, "next", None), (conv_w, "full", None)]
    return _rowcall(name, fn, T, tm, ins, [((T, DFF2), BF16, "row")])[0]


GROUP_ROWS = SWA_GROUP * SWA_BLOCK


def _attn_core(q, kp, kc, vp, vc, bias, sink, mask):
    kcat = jnp.concatenate([kp, kc], axis=0)
    vcat = jnp.concatenate([vp, vc], axis=0)
    s = _bdot(q * (SWA_HD ** -0.5), kcat, NT) + bias
    s = jnp.where(mask, s, NEG_INF)
    m = lax.stop_gradient(jnp.maximum(jnp.max(s, axis=-1, keepdims=True), sink))
    p = jnp.exp(s - m)
    denom = jnp.sum(p, axis=-1, keepdims=True) + jnp.exp(sink - m)
    return _bdot(p / denom, vcat)


def _attn_mask(i):
    qi = lax.broadcasted_iota(jnp.int32, (GROUP_ROWS, 2 * SWA_BLOCK), 0) & (SWA_BLOCK - 1)
    ki = lax.broadcasted_iota(jnp.int32, (GROUP_ROWS, 2 * SWA_BLOCK), 1)
    dist = qi + SWA_BLOCK - ki
    return (dist >= 0) & (dist < SWA_BLOCK) & ((ki >= SWA_BLOCK) | (i > 0))


def _head_cols(h):
    return slice(h * SWA_HD, (h + 1) * SWA_HD)


def _stacked_heads(ref, j):
    return jnp.concatenate([ref[:, _head_cols(SWA_GROUP * j + g)].astype(F32) for g in range(SWA_GROUP)], axis=0)


def _flat_operands(j, q_ref, kvc_ref, kvp_ref, b_ref, s_ref):
    heads = slice(SWA_GROUP * j, SWA_GROUP * (j + 1))
    sink = jnp.concatenate([jnp.broadcast_to(s_ref[j, g:g + 1, 0:1], (SWA_BLOCK, 1)) for g in range(SWA_GROUP)],
                           axis=0)
    k_cols, v_cols = _head_cols(j), _head_cols(SWA_KV_HEADS + j)
    return (_stacked_heads(q_ref, j), kvp_ref[:, k_cols].astype(F32), kvc_ref[:, k_cols].astype(F32),
            kvp_ref[:, v_cols].astype(F32), kvc_ref[:, v_cols].astype(F32),
            b_ref[heads].reshape(GROUP_ROWS, 2 * SWA_BLOCK), sink)


def _store_heads(ref, j, stacked):
    for g in range(SWA_GROUP):
        ref[:, _head_cols(SWA_GROUP * j + g)] = stacked[g * SWA_BLOCK:(g + 1) * SWA_BLOCK].astype(ref.dtype)


def _attn_fwd_flat(q, kv, bias, sinks):
    T = q.shape[0]
    nb = T // SWA_BLOCK

    def body(q_ref, kvc_ref, kvp_ref, b_ref, s_ref, o_ref):
        mask = _attn_mask(pl.program_id(0))
        operands = [_flat_operands(j, q_ref, kvc_ref, kvp_ref, b_ref, s_ref) for j in range(SWA_KV_HEADS)]
        outs = [_attn_core(*ops, mask) for ops in operands]
        for j in range(SWA_KV_HEADS):
            _store_heads(o_ref, j, outs[j])

    q_spec = pl.BlockSpec((SWA_BLOCK, q.shape[1]), lambda i: (i, 0))
    cur = pl.BlockSpec((SWA_BLOCK, kv.shape[1]), lambda i: (i, 0))
    prev = pl.BlockSpec((SWA_BLOCK, kv.shape[1]), lambda i: (jnp.maximum(i - 1, 0), 0))
    return pl.pallas_call(
        body, name="attn_fwd", grid=(nb,),
        in_specs=[q_spec, cur, prev, pl.BlockSpec(bias.shape, lambda i: (0, 0, 0)),
                  pl.BlockSpec(sinks.shape, lambda i: (0, 0, 0))],
        out_specs=q_spec, out_shape=jax.ShapeDtypeStruct(q.shape, BF16),
        compiler_params=_params(("arbitrary",)),
    )(q, kv, kv, bias, sinks)


def _attn_bwd_flat(q, kv, bias, sinks, do):
    T = q.shape[0]
    nb = T // SWA_BLOCK

    def body(q_ref, kvc_ref, kvp_ref, b_ref, s_ref, do_ref, dq_ref, dkv_ref, db_ref, dsk_ref, carry):
        i = pl.program_id(0)

        @pl.when(i < nb)
        def _():
            mask = _attn_mask(i)
            operands = [_flat_operands(j, q_ref, kvc_ref, kvp_ref, b_ref, s_ref) for j in range(SWA_KV_HEADS)]
            cots = [_stacked_heads(do_ref, j) for j in range(SWA_KV_HEADS)]
            grads = [jax.vjp(functools.partial(_attn_core, mask=mask), *ops)[1](cot)
                     for ops, cot in zip(operands, cots)]
            for j, (dq, dkp, dkc, dvp, dvc, db, dsc) in enumerate(grads):
                heads = slice(SWA_GROUP * j, SWA_GROUP * (j + 1))
                k_cols, v_cols = _head_cols(j), _head_cols(SWA_KV_HEADS + j)
                _store_heads(dq_ref, j, dq)
                db = db.reshape(SWA_GROUP, SWA_BLOCK, 2 * SWA_BLOCK)
                dsk = jnp.concatenate(
                    [jnp.broadcast_to(jnp.sum(dsc[g * SWA_BLOCK:(g + 1) * SWA_BLOCK], axis=0, keepdims=True),
                                      (1, LANE)) for g in range(SWA_GROUP)], axis=0)

                @pl.when(i == 0)
                def _():
                    db_ref[heads] = db
                    dsk_ref[j] = dsk

                @pl.when(i > 0)
                def _():
                    db_ref[heads] += db
                    dsk_ref[j] += dsk
                    dkv_ref[:, k_cols] = (carry[:, k_cols] + dkp).astype(dkv_ref.dtype)
                    dkv_ref[:, v_cols] = (carry[:, v_cols] + dvp).astype(dkv_ref.dtype)

                carry[:, k_cols] = dkc
                carry[:, v_cols] = dvc

        @pl.when(i == nb)
        def _():
            dkv_ref[...] = carry[...].astype(dkv_ref.dtype)

    last = nb - 1
    q_spec = pl.BlockSpec((SWA_BLOCK, q.shape[1]), lambda i: (jnp.minimum(i, last), 0))
    cur = pl.BlockSpec((SWA_BLOCK, kv.shape[1]), lambda i: (jnp.minimum(i, last), 0))
    prev = pl.BlockSpec((SWA_BLOCK, kv.shape[1]), lambda i: (jnp.clip(i - 1, 0, last), 0))
    b_spec = pl.BlockSpec(bias.shape, lambda i: (0, 0, 0))
    s_spec = pl.BlockSpec(sinks.shape, lambda i: (0, 0, 0))
    return pl.pallas_call(
        body, name="attn_bwd", grid=(nb + 1,),
        in_specs=[q_spec, cur, prev, b_spec, s_spec, q_spec],
        out_specs=[q_spec, prev, b_spec, s_spec],
        out_shape=[jax.ShapeDtypeStruct(q.shape, BF16), jax.ShapeDtypeStruct(kv.shape, BF16),
                   jax.ShapeDtypeStruct(bias.shape, F32), jax.ShapeDtypeStruct(sinks.shape, F32)],
        scratch_shapes=[pltpu.VMEM((SWA_BLOCK, kv.shape[1]), F32)],
        compiler_params=_params(("arbitrary",)),
    )(q, kv, kv, bias, sinks, do)


def _rel_onehot():
    qi = jnp.arange(SWA_BLOCK)[:, None]
    ki = jnp.arange(2 * SWA_BLOCK)[None, :]
    n = jnp.maximum(qi + SWA_BLOCK - ki, 0)
    max_exact = REL_BUCKETS // 2
    nf = jnp.maximum(n, 1).astype(F32)
    large = max_exact + (jnp.log(nf / max_exact) / math.log(REL_MAX_DISTANCE / max_exact)
                         * (REL_BUCKETS - max_exact)).astype(jnp.int32)
    bucket = jnp.where(n < max_exact, n, jnp.minimum(large, REL_BUCKETS - 1)).reshape(-1)
    return (bucket[None, :] == jnp.arange(REL_BUCKETS)[:, None]).astype(F32)


def _final(h, w, target, tm=256):
    T = h.shape[0]
    tm = min(tm, T)

    def fn(i, hv, wv, tv):
        y, vjp = jax.vjp(_rms_core, hv, wv)
        err = y - tv
        dh, dw = vjp(err * (1.0 / D))
        part = 0.5 * jnp.sum(jnp.sum(err * err, axis=1, keepdims=True) * (1.0 / D), axis=0, keepdims=True)
        return jnp.broadcast_to(part, (SUBLANE, LANE)), dh, dw

    ins = [(h, "row", None), (w, "full", None), (target, "row", None)]
    outs = [((SUBLANE, LANE), F32, "acc"), ((T, D), F32, "row"), ((1, D), F32, "acc")]
    return _rowcall("final", fn, T, tm, ins, outs)


def _ffn_fwd(tag, h, P, layer):
    n = _rms_fwd(f"{tag}_rms", h, P["ffn_norm_w"][layer:layer + 1])
    up = _mm_up(f"{tag}_up", n, P["w_up"], layer)
    act = _ffn_act_fwd(f"{tag}_act", up, P["ffn_conv_w"][layer], P["ffn_conv_b"][layer:layer + 1])
    out = _mm_nn(f"{tag}_down", act, P["w_down"][layer], F32, res=h)
    return out, (n, up, act)


def _ffn_bwd(tag, h, saved, dout, P, layer, into=(None, None)):
    n, up, act = saved
    cw, cb = P["ffn_conv_w"][layer], P["ffn_conv_b"][layer:layer + 1]
    dact = _mm_nt(f"{tag}_down_dx", dout, P["w_down"][layer], BF16)
    g_down = _mm_down_tn(f"{tag}_down_dw", act, dout, layer, into[1])
    du, dcw, dcb = _ffn_act_bwd(f"{tag}_act_bwd", up, cw, cb, dact)
    dup = _ffn_conv_bwd(f"{tag}_conv_bwd", du, cw)
    g_up = _mm_up_tn(f"{tag}_up_dw", n, dup, layer, into[0])
    dn = _mm_up_nt(f"{tag}_up_dx", dup, P["w_up"], layer)
    dh, dnw = _rms_bwd(f"{tag}_rms_bwd", h, [(P["ffn_norm_w"][layer:layer + 1], dn)], [dout])
    return dh, dict(w_down=g_down, w_up=g_up, conv_w=dcw[:3], conv_b=dcb, norm_w=dnw)


def _local_step(x, target, P, late=None, pair_sums=None):
    T = x.shape[0]
    n0 = _rms_fwd("a_rms", x, P["a_norm_w"])
    pm = _mm_nn("gdn_in", n0, P["w_in_main"], BF16)
    pba = _mm_nn("gdn_in_ba", n0, P["w_in_ba"], F32)
    qn, kn, v, beta, g = _gdn_pre_fwd(pm, pba, P["a_conv_w"], P["a_log"], P["dt_bias"])
    g_rows, beta_rows = _gate_rows(g), _gate_rows(beta)
    o, sall, tall, *gathered = _gdn_fwd(qn, kn, v, g_rows, beta_rows, gather=late)
    if late is not None:
        P = {**P, **_late_weights(gathered)}
    on = _gnorm_fwd(o, pm, P["a_out_norm_w"])
    h1 = _mm_nn("gdn_out", on, P["w_out"], F32, res=x)
    h2, ffn0 = _ffn_fwd("ffn0", h1, P, 0)
    nkv = _rms_fwd("kv_rms", h2, P["kv_norm_w"])
    kv = _mm_nn("kv_proj", nkv, P["w_kv"], BF16)
    nb = _rms_fwd("b_rms", h2, P["b_norm_w"])
    qp = _mm_nn("q_proj", nb, P["w_q"], BF16)
    onehot = _rel_onehot()
    bias = _mm_nn("rel_bias", P["rel_table_t"], onehot, F32, precision=HIGHEST)
    bias = bias.reshape(SWA_Q_HEADS, SWA_BLOCK, 2 * SWA_BLOCK)
    oa = _attn_fwd_flat(qp, kv, bias, P["sinks"])
    h3 = _mm_nn("o_proj", oa, P["w_o"], F32, res=h2)
    h4, ffn1 = _ffn_fwd("ffn1", h3, P, 1)
    loss, dh4, d_final = _final(h4, P["final_norm_w"], target)

    dh3, gf1 = _ffn_bwd("ffn1", h3, ffn1, dh4, P, 1)
    doa = _mm_nt("o_proj_dx", dh3, P["w_o"], BF16)
    g_wo = _mm_tn("o_proj_dw", oa, dh3)
    dqp, dkv, dbias, dsinks = _attn_bwd_flat(qp, kv, bias, P["sinks"], doa)
    g_wq = _mm_tn("q_proj_dw", nb, dqp)
    dnb = _mm_nt("q_proj_dx", dqp, P["w_q"], F32)
    g_wkv = _mm_tn("kv_proj_dw", nkv, dkv)
    dnkv = _mm_nt("kv_proj_dx", dkv, P["w_kv"], F32)
    dh2, d_bnorm, d_kvnorm = _rms_bwd("b_kv_rms_bwd", h2, [(P["b_norm_w"], dnb), (P["kv_norm_w"], dnkv)], [dh3])
    g_table = _mm_nt("rel_bias_dw", onehot, dbias.reshape(SWA_Q_HEADS, -1), F32, precision=HIGHEST)
    dh1, gf0 = _ffn_bwd("ffn0", h1, ffn0, dh2, P, 0, into=(gf1["w_up"], gf1["w_down"]))
    don = _mm_nt("gdn_out_dx", dh1, P["w_out"], BF16)
    g_wout = _mm_tn("gdn_out_dw", on, dh1)
    do, dz, d_gnorm = _gnorm_bwd(o, pm, P["a_out_norm_w"], don)
    ready = dict(a_w_out=g_wout, w_kv=g_wkv, b_w_q=g_wq, b_w_o=g_wo, ffn_w_up=gf0["w_up"], ffn_w_down=gf0["w_down"])
    pairs = pair_sums(ready, "early") if pair_sums is not None else []
    dq, dk, dv, dg, dbeta, *parts = _gdn_bwd(qn, kn, v, g_rows, beta_rows, sall, tall, do, scatter=pairs)
    dy, dpba, d_aconv, d_alog, d_dtb = _gdn_pre_bwd(pm, pba, P["a_conv_w"], P["a_log"], P["dt_bias"],
                                                    dq, dk, dv, _gate_cols(dbeta), _gate_cols(dg))
    dpm = _gdn_conv_bwd(dy, dz, P["a_conv_w"])
    g_win_main = _mm_tn("gdn_in_dw", n0, dpm)
    g_win_ba = _mm_tn("gdn_in_ba_dw", n0, dpba)
    dn0 = _mm_nt("gdn_in_dx", dpm, P["w_in_main"], F32)
    dn0 = _mm_nt("gdn_in_ba_dx", dpba, P["w_in_ba"], F32, res=dn0)
    dx, d_anorm = _rms_bwd("a_rms_bwd", x, [(P["a_norm_w"], dn0)], [dh1])

    nh = GDN_V_HEADS
    grads = dict(
        a_norm_w=d_anorm,
        a_w_in=jnp.concatenate([g_win_main, g_win_ba[:, :nh], g_win_ba[:, LANE:LANE + nh]], axis=1),
        a_conv_w=d_aconv[:4], a_a_log=d_alog[:, :nh], a_dt_bias=d_dtb[:, :nh], a_out_norm_w=d_gnorm,
        a_w_out=g_wout, kv_norm_w=d_kvnorm, w_kv=g_wkv, b_norm_w=d_bnorm, b_w_q=g_wq,
        b_sinks=dsinks[:, :, 0].reshape(1, SWA_Q_HEADS), b_w_o=g_wo, rel_bias_table=g_table,
        ffn_norm_w=jnp.concatenate([gf0["norm_w"], gf1["norm_w"]], axis=0),
        ffn_w_up=gf0["w_up"],
        ffn_conv_w=jnp.stack([gf0["conv_w"], gf1["conv_w"]], axis=0),
        ffn_conv_b=jnp.concatenate([gf0["conv_b"], gf1["conv_b"]], axis=0),
        ffn_w_down=gf0["w_down"],
        final_norm_w=d_final,
    )
    return loss, dx, grads, dict(zip([n for n in BIG if n in ready], zip(pairs, parts)))


HBM_SPEC = pl.BlockSpec(memory_space=pltpu.HBM)
VMEM_SPEC = pl.BlockSpec(memory_space=pltpu.VMEM)


def _coords():
    return lax.axis_index("x"), lax.axis_index("y"), lax.axis_index("c")


def _remote(src, dst, send_sem, recv_sem, device):
    return pltpu.make_async_remote_copy(src_ref=src, dst_ref=dst, send_sem=send_sem, recv_sem=recv_sem,
                                        device_id=device, device_id_type=MESH)


def _other_chips(x, y):
    return [(1 - x, y), (x, 1 - y), (1 - x, 1 - y)]


def _gather_copies(shapes, split, ins, outs, send_sems, recv_sems):
    x, y, c = _coords()
    p = 2 * x + y
    ici, forwards, from_sibling = [], [], []
    for a, shape in enumerate(shapes):
        h = shape[0] // 2
        for j, chip in enumerate(_other_chips(x, y)):
            q = 2 * chip[0] + chip[1]
            if split[a]:
                mine, theirs = pl.ds(c * h, h), pl.ds((1 - c) * h, h)
                ici.append(_remote(ins[a].at[mine], outs[a].at[p, mine], send_sems.at[6 * a + j],
                                   recv_sems.at[6 * a + j], (*chip, c)))
                land = outs[a].at[q, mine]
                forwards.append(_remote(land, land, send_sems.at[6 * a + 3 + j], recv_sems.at[6 * a + 3 + j],
                                        (x, y, 1 - c)))
                land = outs[a].at[q, theirs]
                from_sibling.append(_remote(land, land, send_sems.at[6 * a + 3 + j], recv_sems.at[6 * a + 3 + j],
                                            (x, y, 1 - c)))
            else:
                ici.append(_remote(ins[a], outs[a].at[p], send_sems.at[6 * a + j], recv_sems.at[6 * a + j],
                                   (*chip, c)))
                forwards.append(None)
    return ici, forwards, from_sibling


def _gather_arrival(shapes, split, ins, outs, send_sems, recv_sems):
    x, y, c = _coords()
    ici, forwards, from_sibling = _gather_copies(shapes, split, ins, outs, send_sems, recv_sems)
    k = 0
    for a, shape in enumerate(shapes):
        h = shape[0] // 2
        for j, chip in enumerate(_other_chips(x, y)):
            q = 2 * chip[0] + chip[1]
            land = outs[a].at[q, pl.ds(c * h, h)] if split[a] else outs[a].at[q]
            _remote(land, land, send_sems.at[6 * a + j], recv_sems.at[6 * a + j], (*chip, c)).wait_recv()
            if forwards[k] is not None:
                forwards[k].start()
            k += 1
    for cp in from_sibling:
        cp.wait_recv()
    for cp in ici + [f for f in forwards if f is not None]:
        cp.wait_send()


def _all_gather(arrs, split, remote):
    n = len(arrs)
    now = [a for a in range(n) if remote[a]]
    shapes = [arrs[a].shape for a in now]
    splits = [split[a] for a in now]

    def body(*refs):
        ins, outs, stage = refs[:n], refs[n:2 * n], refs[2 * n:3 * n]
        send_sems, recv_sems, in_sems, out_sems = refs[3 * n:]
        p = 2 * lax.axis_index("x") + lax.axis_index("y")
        gathered = ([ins[a] for a in now], [outs[a] for a in now], send_sems, recv_sems)
        loads = [pltpu.make_async_copy(ins[a], stage[a], in_sems.at[a]) for a in range(n)]
        for cp in loads:
            cp.start()
        for cp in _gather_copies(shapes, splits, *gathered)[0]:
            cp.start()
        stores = [pltpu.make_async_copy(stage[a], outs[a].at[p], out_sems.at[a]) for a in range(n)]
        for a in range(n):
            loads[a].wait()
            stores[a].start()
        _gather_arrival(shapes, splits, *gathered)
        for cp in stores:
            cp.wait()

    return pl.pallas_call(
        body, name="weights_all_gather", in_specs=[HBM_SPEC] * n, out_specs=[HBM_SPEC] * n,
        out_shape=[jax.ShapeDtypeStruct((N_CHIPS,) + a.shape, a.dtype) for a in arrs],
        scratch_shapes=[pltpu.VMEM(a.shape, a.dtype) for a in arrs]
        + [pltpu.SemaphoreType.DMA((6 * len(now),)), pltpu.SemaphoreType.DMA((6 * len(now),)),
           pltpu.SemaphoreType.DMA((n,)), pltpu.SemaphoreType.DMA((n,))],
        compiler_params=pltpu.CompilerParams(vmem_limit_bytes=VMEM_LIMIT),
    )(*arrs)


PAIR_SWAP_PIECES = 2


def _pair_swap(gs, tag):
    n = len(gs)

    def body(*refs):
        ins, other = refs[:n], refs[n:2 * n]
        send_sems, recv_sems = refs[2 * n:]
        x, y, c = _coords()
        cps = []
        for a in range(n):
            h = gs[a].shape[1] // 2
            piece = h // PAIR_SWAP_PIECES
            for q in range(N_CHIPS):
                for r in range(PAIR_SWAP_PIECES):
                    k = (a * N_CHIPS + q) * PAIR_SWAP_PIECES + r
                    cp = _remote(ins[a].at[q, pl.ds((1 - c) * h + r * piece, piece)],
                                 other[a].at[q, pl.ds(r * piece, piece)], send_sems.at[k], recv_sems.at[k],
                                 (x, y, 1 - c))
                    cp.start()
                    cps.append(cp)
        for cp in cps:
            cp.wait()

    half = [jax.ShapeDtypeStruct((N_CHIPS, g.shape[1] // 2, g.shape[2]), g.dtype) for g in gs]
    nsem = n * N_CHIPS * PAIR_SWAP_PIECES
    return pl.pallas_call(
        body, name=f"grads_pair_swap_{tag}", in_specs=[HBM_SPEC] * n, out_specs=[HBM_SPEC] * n, out_shape=half,
        scratch_shapes=[pltpu.SemaphoreType.DMA((nsem,)), pltpu.SemaphoreType.DMA((nsem,))],
    )(*gs)


def _scatter_copies(ins, outs, send_sems, recv_sems):
    x, y, c = _coords()
    copies = []
    for a in range(len(ins)):
        for j, chip in enumerate(_other_chips(x, y)):
            q = 2 * chip[0] + chip[1]
            copies.append(_remote(ins[a].at[q], outs[a].at[j], send_sems.at[3 * a + j], recv_sems.at[3 * a + j],
                                  (*chip, c)))
    return copies


def _scatter_shapes(ps):
    return [jax.ShapeDtypeStruct((N_CHIPS - 1,) + a.shape[1:], a.dtype) for a in ps]


def _chip_scatter(ps, tag):
    n = len(ps)

    def body(*refs):
        copies = _scatter_copies(refs[:n], refs[n:2 * n], *refs[2 * n:])
        for cp in copies:
            cp.start()
        for cp in copies:
            cp.wait_recv()
        for cp in copies:
            cp.wait_send()

    return pl.pallas_call(
        body, name=f"grads_chip_scatter_{tag}", in_specs=[HBM_SPEC] * n, out_specs=[HBM_SPEC] * n,
        out_shape=_scatter_shapes(ps),
        scratch_shapes=[pltpu.SemaphoreType.DMA((3 * n,)), pltpu.SemaphoreType.DMA((3 * n,))],
    )(*ps)


def _pair_share(rs):
    n = len(rs)

    def body(*refs):
        ins, outs, stage = refs[:n], refs[n:2 * n], refs[2 * n:3 * n]
        send_sems, recv_sems, in_sems, out_sems = refs[3 * n:]
        x, y, c = _coords()

        def mine(a):
            h = rs[a].shape[0]
            return outs[a].at[pl.ds(c * h, h)]

        loads = [pltpu.make_async_copy(ins[a], stage[a], in_sems.at[a]) for a in range(n)]
        for cp in loads:
            cp.start()
        sends = [_remote(ins[a], mine(a), send_sems.at[a], recv_sems.at[a], (x, y, 1 - c)) for a in range(n)]
        for cp in sends:
            cp.start()
        stores = [pltpu.make_async_copy(stage[a], mine(a), out_sems.at[a]) for a in range(n)]
        for a in range(n):
            loads[a].wait()
            stores[a].start()
        for a in range(n):
            h = rs[a].shape[0]
            land = outs[a].at[pl.ds((1 - c) * h, h)]
            _remote(land, land, send_sems.at[a], recv_sems.at[a], (x, y, 1 - c)).wait_recv()
        for cp in sends:
            cp.wait_send()
        for cp in stores:
            cp.wait()

    return pl.pallas_call(
        body, name="grads_pair_share", in_specs=[HBM_SPEC] * n, out_specs=[HBM_SPEC] * n,
        out_shape=[jax.ShapeDtypeStruct((2 * a.shape[0], a.shape[1]), a.dtype) for a in rs],
        scratch_shapes=[pltpu.VMEM(a.shape, a.dtype) for a in rs] + [pltpu.SemaphoreType.DMA((n,))] * 4,
        compiler_params=pltpu.CompilerParams(vmem_limit_bytes=VMEM_LIMIT),
    )(*rs)


def _small_all_reduce(buf):
    R = buf.shape[0]
    ndev = 2 * N_CHIPS

    def body(in_ref, out_ref, gath, send_sems, recv_sems):
        x, y, c = _coords()
        me = 4 * x + 2 * y + c
        gath[me] = in_ref[...]
        peers = []
        for d in range(1, ndev):
            px = 1 - x if d & 4 else x
            py = 1 - y if d & 2 else y
            pc = 1 - c if d & 1 else c
            peers.append((px, py, pc))
        sends = []
        for d, peer in enumerate(peers):
            cp = _remote(in_ref, gath.at[me], send_sems.at[d], recv_sems.at[d], peer)
            cp.start()
            sends.append(cp)
        for d, peer in enumerate(peers):
            land = gath.at[4 * peer[0] + 2 * peer[1] + peer[2]]
            _remote(land, land, send_sems.at[d], recv_sems.at[d], peer).wait_recv()
        for cp in sends:
            cp.wait_send()
        acc = gath[0]
        for s in range(1, ndev):
            acc = acc + gath[s]
        out_ref[...] = acc

    return pl.pallas_call(
        body, name="small_all_reduce", in_specs=[VMEM_SPEC], out_specs=VMEM_SPEC,
        out_shape=jax.ShapeDtypeStruct(buf.shape, F32),
        scratch_shapes=[pltpu.VMEM((ndev, R, LANE), F32), pltpu.SemaphoreType.DMA((ndev - 1,)),
                        pltpu.SemaphoreType.DMA((ndev - 1,))],
    )(buf)


def _pair_add(name, own, other):
    h = own.shape[1]
    tm = _tile(h, (128, 64, 32, 16))

    def fn(i, a, b):
        return (a + b,)

    return _rowcall(name, fn, h, tm, [(own, "row", None), (other, "row", None)], [(own.shape, BF16, "row")])[0]


def _chip_add(name, own, parts):
    h = parts.shape[1]
    tm = _tile(h, (128, 64, 32, 16))

    def fn(i, o, a):
        a = a.astype(F32)
        return (((o.astype(F32) + a[0]) + a[1]) + a[2],)

    return _rowcall(name, fn, h, tm, [(own, "row", None), (parts, "row", None)], [(parts.shape[1:], F32, "row")])[0]


def _adamw(name, w, g, m, v):
    R = w.shape[0]
    tm = _tile(R, (256, 128, 64, 32, 16, 8))

    def fn(i, wv, gv, mv, vv):
        m2 = ADAM_B1 * mv + (1.0 - ADAM_B1) * gv
        v2 = ADAM_B2 * vv + (1.0 - ADAM_B2) * (gv * gv)
        m_hat = m2 / (1.0 - ADAM_B1 ** ADAM_STEP)
        v_hat = v2 / (1.0 - ADAM_B2 ** ADAM_STEP)
        delta = -ADAM_LR * (m_hat / (jnp.sqrt(v_hat) + ADAM_EPS) + ADAM_WD * wv)
        return delta, m2, v2

    ins = [(a, "row", None) for a in (w, g, m, v)]
    return _rowcall(name, fn, R, tm, ins, [(w.shape, F32, "row")] * 3)


def _pack(arrs):
    flat = jnp.concatenate([a.reshape(-1).astype(F32) for a in arrs])
    size = flat.shape[0]
    padded = -(-size // (SUBLANE * LANE)) * SUBLANE * LANE
    return jnp.pad(flat, (0, padded - size)).reshape(-1, LANE)


def _unpack(buf, shapes):
    flat = buf.reshape(-1)
    out, off = [], 0
    for s in shapes:
        size = math.prod(s)
        out.append(flat[off:off + size].reshape(s))
        off += size
    return out


BIG = ("a_w_in", "a_w_out", "w_kv", "b_w_q", "b_w_o", "ffn_w_up", "ffn_w_down")
WEIGHTS = ("a_norm_w", "a_w_in", "a_conv_w", "a_a_log", "a_dt_bias", "a_out_norm_w", "a_w_out", "kv_norm_w", "w_kv",
           "b_norm_w", "b_w_q", "b_sinks", "b_w_o", "rel_bias_table", "ffn_norm_w", "ffn_w_up", "ffn_conv_w",
           "ffn_conv_b", "ffn_w_down", "final_norm_w")
SMALL = tuple(n for n in WEIGHTS if n not in BIG)
SMALL_SHARDED = {"a_norm_w": 1, "a_conv_w": 2, "ffn_conv_w": 2}


def _quarter_2d(name, a):
    if name in ("ffn_w_up", "ffn_w_down"):
        return a.reshape(a.shape[0] * a.shape[1], a.shape[2])
    return a.reshape(a.shape[-2], a.shape[-1])


def _whole_weights(w):
    bigs = [_quarter_2d(n, w[n]).astype(BF16) for n in BIG]
    smalls = [w["a_norm_w"], w["a_conv_w"][0], w["ffn_conv_w"].reshape(6, DFF2_SHARD)]
    remote = [True] + [False] * (len(bigs) - 1) + [True] * len(smalls)
    g = _all_gather(bigs + smalls, [True] * len(bigs) + [False] * len(smalls), remote)
    w_in = g[0].transpose(1, 0, 2).reshape(D, GDN_IN)
    nh = GDN_V_HEADS
    zpad = jnp.zeros((D, LANE - nh), BF16)
    w_in_ba = jnp.concatenate([w_in[:, GDN_MAIN:GDN_MAIN + nh], zpad, w_in[:, GDN_MAIN + nh:], zpad], axis=1)
    lane_pad = lambda a: jnp.pad(a, ((0, 0), (0, LANE - nh)))
    early = dict(
        a_norm_w=g[7].reshape(1, D), w_in_main=w_in[:, :GDN_MAIN], w_in_ba=w_in_ba,
        a_conv_w=g[8].transpose(1, 0, 2).reshape(4, GDN_CONV), a_log=lane_pad(w["a_a_log"]),
        dt_bias=lane_pad(w["a_dt_bias"]), a_out_norm_w=w["a_out_norm_w"],
        kv_norm_w=w["kv_norm_w"].reshape(1, D), b_norm_w=w["b_norm_w"],
        sinks=jnp.broadcast_to(w["b_sinks"].reshape(SWA_KV_HEADS, SWA_GROUP, 1), (SWA_KV_HEADS, SWA_GROUP, LANE)),
        rel_table_t=w["rel_bias_table"].T, ffn_norm_w=w["ffn_norm_w"],
        ffn_conv_w=g[9].reshape(N_CHIPS, 2, 3, DFF2_SHARD).transpose(1, 2, 0, 3).reshape(2, 3, DFF2),
        ffn_conv_b=w["ffn_conv_b"], final_norm_w=w["final_norm_w"].reshape(1, D),
    )
    return early, (bigs[1:], g[1:len(bigs)])


def _late_weights(g):
    return dict(
        w_out=g[0].reshape(GDN_V, D), w_kv=g[1].reshape(D, 2 * SWA_KV_HEADS * SWA_HD), w_q=g[2].reshape(D, D),
        w_o=g[3].reshape(D, D), w_up=g[4].reshape(N_CHIPS, 2, D, DFF2_SHARD),
        w_down=g[5].reshape(N_CHIPS, 2, DFF_SHARD, D).transpose(1, 0, 2, 3).reshape(2, DFF, D),
    )


def _chip_major(name, g):
    if name == "a_w_in":
        return g.reshape(D, N_CHIPS, GDN_IN_SHARD).transpose(1, 0, 2)
    if name == "ffn_w_up":
        return g.reshape(N_CHIPS, 2 * D, DFF2_SHARD)
    if name == "ffn_w_down":
        return g.reshape(N_CHIPS, 2 * DFF_SHARD, D)
    return g.reshape(N_CHIPS, g.shape[0] // N_CHIPS, g.shape[1])


def kernel(x, a_norm_w, a_w_in, a_conv_w, a_a_log, a_dt_bias, a_out_norm_w, a_w_out, kv_norm_w, w_kv, b_norm_w, b_w_q, b_sinks, b_w_o, rel_bias_table, ffn_norm_w, ffn_w_up, ffn_conv_w, ffn_conv_b, ffn_w_down, final_norm_w, loss_target, m_a_norm_w, m_a_w_in, m_a_conv_w, m_a_a_log, m_a_dt_bias, m_a_out_norm_w, m_a_w_out, m_kv_norm_w, m_w_kv, m_b_norm_w, m_b_w_q, m_b_sinks, m_b_w_o, m_rel_bias_table, m_ffn_norm_w, m_ffn_w_up, m_ffn_conv_w, m_ffn_conv_b, m_ffn_w_down, m_final_norm_w, v_a_norm_w, v_a_w_in, v_a_conv_w, v_a_a_log, v_a_dt_bias, v_a_out_norm_w, v_a_w_out, v_kv_norm_w, v_w_kv, v_b_norm_w, v_b_w_q, v_b_sinks, v_b_w_o, v_rel_bias_table, v_ffn_norm_w, v_ffn_w_up, v_ffn_conv_w, v_ffn_conv_b, v_ffn_w_down, v_final_norm_w):
    w = dict(zip(WEIGHTS, (a_norm_w, a_w_in, a_conv_w, a_a_log, a_dt_bias, a_out_norm_w, a_w_out, kv_norm_w, w_kv,
                           b_norm_w, b_w_q, b_sinks, b_w_o, rel_bias_table, ffn_norm_w, ffn_w_up, ffn_conv_w,
                           ffn_conv_b, ffn_w_down, final_norm_w)))
    m = dict(zip(WEIGHTS, (m_a_norm_w, m_a_w_in, m_a_conv_w, m_a_a_log, m_a_dt_bias, m_a_out_norm_w, m_a_w_out,
                           m_kv_norm_w, m_w_kv, m_b_norm_w, m_b_w_q, m_b_sinks, m_b_w_o, m_rel_bias_table,
                           m_ffn_norm_w, m_ffn_w_up, m_ffn_conv_w, m_ffn_conv_b, m_ffn_w_down, m_final_norm_w)))
    v = dict(zip(WEIGHTS, (v_a_norm_w, v_a_w_in, v_a_conv_w, v_a_a_log, v_a_dt_bias, v_a_out_norm_w, v_a_w_out,
                           v_kv_norm_w, v_w_kv, v_b_norm_w, v_b_w_q, v_b_sinks, v_b_w_o, v_rel_bias_table,
                           v_ffn_norm_w, v_ffn_w_up, v_ffn_conv_w, v_ffn_conv_b, v_ffn_w_down, v_final_norm_w)))
    T = x.shape[1]
    chip = 2 * lax.axis_index("x") + lax.axis_index("y")

    core = lax.axis_index("c")

    def pair_sums(named, tag):
        names = [n for n in BIG if n in named]
        whole = [_chip_major(n, named[n]) for n in names]
        other = _pair_swap(whole, tag)
        own = [lax.dynamic_slice_in_dim(g, core * (g.shape[1] // 2), g.shape[1] // 2, 1) for g in whole]
        return [_pair_add(f"pair_add_{n}", a, b) for n, a, b in zip(names, own, other)]

    early, late = _whole_weights(w)
    loss_part, dx, grads, scattered = _local_step(x.reshape(T, D), loss_target.reshape(T, D), early, late, pair_sums)

    rest = [n for n in BIG if n not in scattered]
    pair = pair_sums({n: grads[n] for n in rest}, "late")
    scattered.update(zip(rest, zip(pair, _chip_scatter(pair, "late"))))
    halves = [_chip_add(f"chip_add_{n}", lax.dynamic_index_in_dim(scattered[n][0], chip, 0, keepdims=False),
                        scattered[n][1]) for n in BIG]
    quarter = _pair_share(halves)
    out_g, out_d, out_m, out_v = {}, {}, {}, {}
    for n, g2 in zip(BIG, quarter):
        res = _adamw(f"adamw_{n}", _quarter_2d(n, w[n]), g2, _quarter_2d(n, m[n]), _quarter_2d(n, v[n]))
        out_g[n] = g2.reshape(w[n].shape)
        out_d[n], out_m[n], out_v[n] = (r.reshape(w[n].shape) for r in res)

    whole = [grads[n] for n in SMALL]
    summed = _unpack(_small_all_reduce(_pack([loss_part[0:1, 0:1]] + whole)), [(1, 1)] + [a.shape for a in whole])
    loss = summed[0].reshape(())
    small_g = []
    for n, g in zip(SMALL, summed[1:]):
        if n in SMALL_SHARDED:
            axis = SMALL_SHARDED[n]
            g = g.reshape(w[n].shape[:axis] + (-1,) + w[n].shape[axis + 1:])
            size = w[n].shape[axis]
            g = lax.dynamic_slice_in_dim(g, chip * size, size, axis)
        small_g.append(g.reshape(w[n].shape))
    shapes = [w[n].shape for n in SMALL]
    res = _adamw("adamw_small", _pack([w[n] for n in SMALL]), _pack(small_g), _pack([m[n] for n in SMALL]),
                 _pack([v[n] for n in SMALL]))
    small_d, small_m, small_v = (_unpack(r, shapes) for r in res)
    for i, n in enumerate(SMALL):
        out_g[n], out_d[n], out_m[n], out_v[n] = small_g[i], small_d[i], small_m[i], small_v[i]

    return (loss, dx.reshape(x.shape), *[out_g[n] for n in WEIGHTS], *[out_d[n] for n in WEIGHTS],
            *[out_m[n] for n in WEIGHTS], *[out_v[n] for n in WEIGHTS])
```

```python
import functools
import math

import jax
import jax.numpy as jnp
from jax import lax
from jax.experimental import pallas as pl
from jax.experimental.pallas import tpu as pltpu

F32 = jnp.float32
BF16 = jnp.bfloat16
MESH = pl.DeviceIdType.MESH
HIGHEST = lax.Precision.HIGHEST

D = 1024
EPS = 1e-6
NEG_INF = -1e30
N_CHIPS = 4

GDN_QK_HEADS = 8
GDN_V_HEADS = 16
GDN_HD = 128
GDN_QK = GDN_QK_HEADS * GDN_HD
GDN_V = GDN_V_HEADS * GDN_HD
GDN_CONV = 2 * GDN_QK + GDN_V
GDN_MAIN = GDN_CONV + GDN_V
GDN_IN = GDN_MAIN + 2 * GDN_V_HEADS
GDN_IN_SHARD = GDN_IN // N_CHIPS
GDN_CHUNK = 64

SWA_Q_HEADS = 16
SWA_KV_HEADS = 4
SWA_GROUP = 4
SWA_HD = 64
SWA_BLOCK = 128
REL_BUCKETS = 32
REL_MAX_DISTANCE = 128

DFF = 2816
DFF2 = 2 * DFF
DFF2_SHARD = DFF2 // N_CHIPS
DFF_SHARD = DFF // N_CHIPS

ADAM_LR = 0.001
ADAM_B1 = 0.9
ADAM_B2 = 0.999
ADAM_EPS = 1e-08
ADAM_WD = 0.01
ADAM_STEP = 10

LANE = 128
SUBLANE = 8
VMEM_LIMIT = 56 * 1024 * 1024


def _params(sem, vmem=VMEM_LIMIT):
    return pltpu.CompilerParams(dimension_semantics=sem, vmem_limit_bytes=vmem)


def _rowcall(name, fn, T, tm, ins, outs):
    n = T // tm
    r8 = tm // SUBLANE
    last8 = T // SUBLANE - 1
    arrays, in_specs = [], []
    for arr, kind, cols in ins:
        arrays.append(arr)
        if kind == "full":
            in_specs.append(pl.BlockSpec(arr.shape, functools.partial(lambda nd, i: (0,) * nd, arr.ndim)))
        elif arr.ndim == 2:
            w, ci = cols if cols is not None else (arr.shape[1], 0)
            if kind == "row":
                in_specs.append(pl.BlockSpec((tm, w), functools.partial(lambda ci, i: (i, ci), ci)))
            elif kind == "prev":
                in_specs.append(pl.BlockSpec(
                    (SUBLANE, w), functools.partial(lambda ci, i: (jnp.maximum(i * r8 - 1, 0), ci), ci)))
            else:
                in_specs.append(pl.BlockSpec(
                    (SUBLANE, w), functools.partial(lambda ci, i: (jnp.minimum((i + 1) * r8, last8), ci), ci)))
        else:
            lead = arr.shape[:-2]
            in_specs.append(pl.BlockSpec(lead + (tm, arr.shape[-1]),
                                         functools.partial(lambda nl, i: (0,) * nl + (i, 0), len(lead))))
    out_shape, out_specs = [], []
    for shape, dtype, kind in outs:
        out_shape.append(jax.ShapeDtypeStruct(shape, dtype))
        if kind == "acc":
            out_specs.append(pl.BlockSpec(shape, functools.partial(lambda nd, i: (0,) * nd, len(shape))))
        else:
            lead = shape[:-2]
            out_specs.append(pl.BlockSpec(lead + (tm, shape[-1]),
                                          functools.partial(lambda nl, i: (0,) * nl + (i, 0), len(lead))))
    nin = len(arrays)

    def body(*refs):
        i = pl.program_id(0)
        vals = [r[...] for r in refs[:nin]]
        res = fn(i, *vals)
        for (shape, dtype, kind), o, r in zip(outs, refs[nin:], res):
            if kind == "row":
                o[...] = r.astype(dtype)
            else:
                @pl.when(i == 0)
                def _():
                    o[...] = r.astype(dtype)

                @pl.when(i > 0)
                def _():
                    o[...] += r.astype(dtype)

    res = pl.pallas_call(
        body, name=name, grid=(n,), in_specs=in_specs, out_specs=out_specs, out_shape=out_shape,
        compiler_params=_params(("arbitrary",)),
    )(*arrays)
    return res


def _mm(name, a, b, out_shape, out_dtype, grid, a_spec, b_spec, o_spec, dims, acc_shape, res=None, precision=None,
        into=None, scatter=()):
    nk = grid[2]
    ns = len(scatter)
    n_in = 2 + (res is not None) + (into is not None) + ns

    def body(*refs):
        a_ref, b_ref, o_ref = refs[0], refs[1], refs[n_in]
        r_ref = refs[2] if res is not None else None
        if ns:
            comm = (refs[n_in - ns:n_in], refs[n_in + 1:n_in + 1 + ns], refs[-2], refs[-1])
            steps = [pl.program_id(d) for d in range(3)]

            @pl.when((steps[0] == 0) & (steps[1] == 0) & (steps[2] == 0))
            def _():
                for cp in _scatter_copies(*comm):
                    cp.start()

            @pl.when((steps[0] == grid[0] - 1) & (steps[1] == grid[1] - 1) & (steps[2] == grid[2] - 1))
            def _():
                copies = _scatter_copies(*comm)
                for cp in copies:
                    cp.wait_recv()
                for cp in copies:
                    cp.wait_send()

        av, bv = a_ref[...], b_ref[...]
        if precision is None:
            av, bv = av.astype(BF16), bv.astype(BF16)
        p = lax.dot_general(av, bv, (dims, ((), ())), preferred_element_type=F32, precision=precision)

        def finish(x):
            if res is not None:
                x = x + r_ref[...].astype(F32)
            o_ref[...] = x.astype(out_dtype).reshape(o_ref.shape)

        if nk == 1:
            finish(p)
        else:
            acc = refs[n_in + 1 + ns]
            k = pl.program_id(2)

            @pl.when(k == 0)
            def _():
                acc[...] = p

            @pl.when(k > 0)
            def _():
                acc[...] += p

            @pl.when(k == nk - 1)
            def _():
                finish(acc[...])

    anywhere = pl.BlockSpec(memory_space=pl.ANY)
    ops = [a, b] + ([res] if res is not None else []) + ([into] if into is not None else []) + list(scatter)
    specs = [a_spec, b_spec] + ([o_spec] if res is not None else [])
    specs += ([anywhere] if into is not None else []) + [anywhere] * ns
    out = pl.pallas_call(
        body, name=name, grid=grid, in_specs=specs, out_specs=[o_spec] + [anywhere] * ns,
        out_shape=[jax.ShapeDtypeStruct(out_shape, out_dtype)] + _scatter_shapes(scatter),
        input_output_aliases={n_in - ns - 1: 0} if into is not None else {},
        scratch_shapes=([pltpu.VMEM(acc_shape, F32)] if nk > 1 else [])
        + ([pltpu.SemaphoreType.DMA((3 * ns,)), pltpu.SemaphoreType.DMA((3 * ns,))] if ns else []),
        compiler_params=_params(("arbitrary",) * 3 if ns else ("parallel", "parallel", "arbitrary")),
    )(*ops)
    return out if ns else out[0]


NN = ((1,), (0,))
NT = ((1,), (1,))
TN = ((0,), (0,))


BIG_TILES = (1024, 512, 256, 128)


def _tile(n, pref):
    for t in pref:
        if n % t == 0:
            return t
    return n


def _mm_nn(name, a, w, out_dtype, res=None, precision=None):
    M, K = a.shape
    N = w.shape[1]
    tm = _tile(M, BIG_TILES if K <= 2048 else BIG_TILES[1:])
    tn = _tile(N, BIG_TILES)
    return _mm(name, a, w, (M, N), out_dtype, (M // tm, N // tn, 1),
               pl.BlockSpec((tm, K), lambda i, j, k: (i, 0)), pl.BlockSpec((K, tn), lambda i, j, k: (0, j)),
               pl.BlockSpec((tm, tn), lambda i, j, k: (i, j)), NN, (tm, tn), res=res, precision=precision)


def _mm_nt(name, g, w, out_dtype, res=None, precision=None, scatter=()):
    M, N = g.shape
    K = w.shape[0]
    tm, tk = _tile(M, BIG_TILES), _tile(K, (1024, 1408, 512, 256, 128))
    tn = _tile(N, (1536,) + BIG_TILES)
    return _mm(name, g, w, (M, K), out_dtype, (M // tm, K // tk, N // tn),
               pl.BlockSpec((tm, tn), lambda i, j, k: (i, k)), pl.BlockSpec((tk, tn), lambda i, j, k: (j, k)),
               pl.BlockSpec((tm, tk), lambda i, j, k: (i, j)), NT, (tm, tk), res=res, precision=precision,
               scatter=scatter)


def _mm_tn(name, a, g, out_dtype=F32, precision=None):
    T, K = a.shape
    N = g.shape[1]
    tk, tn = _tile(K, (1024, 1408, 512, 256, 128)), _tile(N, BIG_TILES)
    tt = _tile(T, BIG_TILES)
    return _mm(name, a, g, (K, N), out_dtype, (K // tk, N // tn, T // tt),
               pl.BlockSpec((tt, tk), lambda i, j, k: (k, i)), pl.BlockSpec((tt, tn), lambda i, j, k: (k, j)),
               pl.BlockSpec((tk, tn), lambda i, j, k: (i, j)), TN, (tk, tn), precision=precision)


def _mm_up(name, n, wup, layer):
    T = n.shape[0]
    tm = _tile(T, BIG_TILES)
    return _mm(name, n, wup, (T, DFF2), BF16, (T // tm, N_CHIPS, 1),
               pl.BlockSpec((tm, D), lambda i, j, k: (i, 0)),
               pl.BlockSpec((None, None, D, DFF2_SHARD), lambda i, j, k: (j, layer, 0, 0)),
               pl.BlockSpec((tm, DFF2_SHARD), lambda i, j, k: (i, j)), NN, (tm, DFF2_SHARD))


def _mm_up_nt(name, du, wup, layer):
    T = du.shape[0]
    tm, tk = _tile(T, BIG_TILES), D
    return _mm(name, du, wup, (T, D), F32, (T // tm, D // tk, N_CHIPS),
               pl.BlockSpec((tm, DFF2_SHARD), lambda i, j, k: (i, k)),
               pl.BlockSpec((None, None, tk, DFF2_SHARD), lambda i, j, k: (k, layer, j, 0)),
               pl.BlockSpec((tm, tk), lambda i, j, k: (i, j)), NT, (tm, tk))


def _mm_up_tn(name, n, du, layer, into):
    T = n.shape[0]
    tk, tt = D, _tile(T, BIG_TILES)
    return _mm(name, n, du, (N_CHIPS, 2, D, DFF2_SHARD), F32, (D // tk, N_CHIPS, T // tt),
               pl.BlockSpec((tt, tk), lambda i, j, k: (k, i)), pl.BlockSpec((tt, DFF2_SHARD), lambda i, j, k: (k, j)),
               pl.BlockSpec((None, None, tk, DFF2_SHARD), lambda i, j, k: (j, layer, i, 0)), TN, (tk, DFF2_SHARD),
               into=into)


def _mm_down_tn(name, act, dout, layer, into):
    T = act.shape[0]
    tk, tn, tt = 2 * DFF_SHARD, _tile(D, BIG_TILES), _tile(T, BIG_TILES)
    return _mm(name, act, dout, (2, 2, 2, DFF_SHARD, D), F32, (DFF // tk, D // tn, T // tt),
               pl.BlockSpec((tt, tk), lambda i, j, k: (k, i)), pl.BlockSpec((tt, tn), lambda i, j, k: (k, j)),
               pl.BlockSpec((None, 2, None, DFF_SHARD, tn), lambda i, j, k: (i, 0, layer, 0, j)), TN, (tk, tn),
               into=into)


def _sigmoid(x):
    return 0.5 * jnp.tanh(0.5 * x) + 0.5


def _silu(x):
    return x * _sigmoid(x)


def _softplus(x):
    return jnp.maximum(x, 0.0) + jnp.log(1.0 + jnp.exp(-jnp.abs(x)))


def _rms_core(h, w):
    return h * lax.rsqrt(jnp.mean(h * h, axis=-1, keepdims=True) + EPS) * w


def _shift_down(x, halo, s, i):
    if s == 0:
        return x
    tm = x.shape[0]
    rolled = pltpu.roll(x, s, 0)
    patch = pltpu.roll(jnp.where(i == 0, 0.0, halo), s, 0)
    row = lax.broadcasted_iota(jnp.int32, patch.shape, 0)
    top = jnp.where(row < s, patch, rolled[:SUBLANE])
    return jnp.concatenate([top, rolled[SUBLANE:]], axis=0) if tm > SUBLANE else top


def _shift_up(x, halo, s, i, n):
    if s == 0:
        return x
    tm = x.shape[0]
    rolled = pltpu.roll(x, tm - s, 0)
    patch = pltpu.roll(jnp.where(i == n - 1, 0.0, halo), SUBLANE - s, 0)
    row = lax.broadcasted_iota(jnp.int32, patch.shape, 0)
    bottom = jnp.where(row >= SUBLANE - s, patch, rolled[tm - SUBLANE:])
    return jnp.concatenate([rolled[:tm - SUBLANE], bottom], axis=0) if tm > SUBLANE else bottom


def _taps(x, halo, K, i):
    return [_shift_down(x, halo, K - 1 - j, i) for j in range(K)]


def _conv_fwd(taps, w):
    y = w[0:1, :] * taps[0]
    for j in range(1, len(taps)):
        y = y + w[j:j + 1, :] * taps[j]
    return y


def _conv_dx(dy, halo_next, w, i, n):
    K = w.shape[0]
    dx = w[K - 1:K, :] * dy
    for j in range(K - 1):
        dx = dx + w[j:j + 1, :] * _shift_up(dy, halo_next, K - 1 - j, i, n)
    return dx


def _conv_dw(dy, taps):
    rows = [jnp.sum(dy * tap, axis=0, keepdims=True) for tap in taps]
    return jnp.concatenate(rows + [jnp.zeros((SUBLANE - len(taps), dy.shape[1]), F32)], axis=0)


def _rms_fwd(name, h, w, tm=512):
    T = h.shape[0]
    tm = min(tm, T)

    def fn(i, hv, wv):
        return (_rms_core(hv, wv),)

    return _rowcall(name, fn, T, tm, [(h, "row", None), (w, "full", None)], [((T, D), BF16, "row")])[0]


def _rms_bwd(name, h, pairs, adds, tm=256):
    T = h.shape[0]
    tm = min(tm, T)
    npair, nadd = len(pairs), len(adds)

    def fn(i, hv, *rest):
        ws, dns, ads = rest[:npair], rest[npair:2 * npair], rest[2 * npair:]
        dh = None
        dws = []
        for wv, dn in zip(ws, dns):
            _, vjp = jax.vjp(_rms_core, hv, wv)
            dhi, dwi = vjp(dn.astype(F32))
            dh = dhi if dh is None else dh + dhi
            dws.append(dwi)
        for a in ads:
            dh = dh + a.astype(F32)
        return (dh, *dws)

    ins = [(h, "row", None)] + [(w, "full", None) for w, _ in pairs] + [(dn, "row", None) for _, dn in pairs]
    ins += [(a, "row", None) for a in adds]
    outs = [((T, D), F32, "row")] + [((1, D), F32, "acc")] * npair
    return _rowcall(name, fn, T, tm, ins, outs)


def _l2(x):
    return x * lax.rsqrt(jnp.sum(x * x, axis=-1, keepdims=True) + EPS)


def _gdn_post_core(yq, yk, yv, pb, pa, a_log, dtb):
    qn = tuple(_l2(_silu(a)) * (GDN_HD ** -0.5) for a in yq)
    kn = tuple(_l2(_silu(a)) for a in yk)
    v = _silu(yv)
    beta = _sigmoid(pb)
    g = -jnp.exp(a_log) * _softplus(pa + dtb)
    return qn, kn, v, beta, g


def _heads(x, n):
    return tuple(x[:, GDN_HD * h:GDN_HD * (h + 1)] for h in range(n))


def _gdn_pre_fwd(pm, pba, conv_w, a_log, dtb, tm=128):
    T = pm.shape[0]
    tm = min(tm, T)

    def fn(i, x, halo, pbav, cw, al, db):
        y = _conv_fwd(_taps(x.astype(F32), halo.astype(F32), 4, i), cw)
        qn, kn, v, beta, g = _gdn_post_core(_heads(y[:, :GDN_QK], 8), _heads(y[:, GDN_QK:2 * GDN_QK], 8),
                                            y[:, 2 * GDN_QK:], pbav[:, :LANE], pbav[:, LANE:], al, db)
        return jnp.stack(qn), jnp.stack(kn), jnp.stack(_heads(v, GDN_V_HEADS)), beta, g

    ins = [(pm, "row", (GDN_CONV, 0)), (pm, "prev", (GDN_CONV, 0)), (pba, "row", None),
           (conv_w, "full", None), (a_log, "full", None), (dtb, "full", None)]
    outs = [((GDN_QK_HEADS, T, GDN_HD), BF16, "row"), ((GDN_QK_HEADS, T, GDN_HD), BF16, "row"),
            ((GDN_V_HEADS, T, GDN_HD), BF16, "row"), ((T, LANE), F32, "row"), ((T, LANE), F32, "row")]
    return _rowcall("gdn_pre_fwd", fn, T, tm, ins, outs)


def _gdn_pre_bwd(pm, pba, conv_w, a_log, dtb, dqn, dkn, dv, dbeta, dg, tm=128):
    T = pm.shape[0]
    tm = min(tm, T)

    def fn(i, x, halo, pbav, cw, al, db, dqv, dkv, dvv, dbv, dgv):
        taps = _taps(x.astype(F32), halo.astype(F32), 4, i)
        y = _conv_fwd(taps, cw)
        prim = (_heads(y[:, :GDN_QK], 8), _heads(y[:, GDN_QK:2 * GDN_QK], 8), y[:, 2 * GDN_QK:],
                pbav[:, :LANE], pbav[:, LANE:], al, db)
        _, vjp = jax.vjp(_gdn_post_core, *prim)
        cot = (tuple(dqv[h].astype(F32) for h in range(8)), tuple(dkv[h].astype(F32) for h in range(8)),
               jnp.concatenate([dvv[h].astype(F32) for h in range(GDN_V_HEADS)], axis=1), dbv, dgv)
        dyq, dyk, dyv, dpb, dpa, dal, ddb = vjp(cot)
        dy = jnp.concatenate(list(dyq) + list(dyk) + [dyv], axis=1)
        dcw = _conv_dw(dy, taps)
        return dy, jnp.concatenate([dpb, dpa], axis=1), dcw, dal, ddb

    ins = [(pm, "row", (GDN_CONV, 0)), (pm, "prev", (GDN_CONV, 0)), (pba, "row", None),
           (conv_w, "full", None), (a_log, "full", None), (dtb, "full", None),
           (dqn, "row", None), (dkn, "row", None), (dv, "row", None), (dbeta, "row", None), (dg, "row", None)]
    outs = [((T, GDN_CONV), BF16, "row"), ((T, 2 * LANE), F32, "row"), ((SUBLANE, GDN_CONV), F32, "acc"),
            ((1, LANE), F32, "acc"), ((1, LANE), F32, "acc")]
    return _rowcall("gdn_pre_bwd", fn, T, tm, ins, outs)


def _gdn_conv_bwd(dy, dz, conv_w, tm=256):
    T = dy.shape[0]
    tm = min(tm, T)
    n = T // tm

    def fn(i, dyv, halo, dzv, cw):
        dx = _conv_dx(dyv.astype(F32), halo.astype(F32), cw, i, n)
        return (jnp.concatenate([dx.astype(BF16), dzv.astype(BF16)], axis=1),)

    ins = [(dy, "row", None), (dy, "next", None), (dz, "row", None), (conv_w, "full", None)]
    return _rowcall("gdn_conv_bwd", fn, T, tm, ins, [((T, GDN_MAIN), BF16, "row")])[0]


def _bdot(a, b, dims=NN):
    return lax.dot_general(a.astype(BF16), b.astype(BF16), (dims, ((), ())), preferred_element_type=F32)


BNN = ((2,), (1,))
BNT = ((2,), (2,))
BTN = ((1,), (1,))


def _bmm(a, b, dims=BNN):
    return lax.dot_general(a.astype(BF16), b.astype(BF16), (dims, ((0,), (0,))), preferred_element_type=F32)


def _bmm3(a, b):
    ah, bh = a.astype(BF16), b.astype(BF16)
    al, bl = (a - ah.astype(F32)).astype(BF16), (b - bh.astype(F32)).astype(BF16)
    dn = (BNN, ((0,), (0,)))
    return (lax.dot_general(ah, bh, dn, preferred_element_type=F32)
            + lax.dot_general(al, bh, dn, preferred_element_type=F32)
            + lax.dot_general(ah, bl, dn, preferred_element_type=F32))


def _tri_inv(m):
    C = m.shape[-1]
    r = lax.broadcasted_iota(jnp.int32, (C, C), 0)
    c = lax.broadcasted_iota(jnp.int32, (C, C), 1)
    t = jnp.where(r == c, 1.0, 0.0) - m
    pw = _bmm3(m, m)
    t = t + _bmm3(t, pw)
    for _ in range(int(math.log2(C)) - 2):
        pw = _bmm(pw, pw)
        t = t + _bmm(t, pw)
    return t


def _tri_inv_vjp(t, dt):
    tt = jnp.swapaxes(t, 1, 2)
    return -_bmm(_bmm(tt, dt), tt)


def _twice(a):
    return jnp.broadcast_to(a[:, None], (a.shape[0], 2) + a.shape[1:]).reshape((2 * a.shape[0],) + a.shape[1:])


def _gdn_gates(grow, brow):
    C = grow.shape[2]
    r = lax.broadcasted_iota(jnp.int32, (C, C), 0)
    c = lax.broadcasted_iota(jnp.int32, (C, C), 1)
    tril, eye = r >= c, r == c
    gcol = jnp.sum(jnp.where(eye, grow, 0.0), axis=2, keepdims=True)
    bcol = jnp.sum(jnp.where(eye, brow, 0.0), axis=2, keepdims=True)
    gc_col = jnp.sum(jnp.where(tril, grow, 0.0), axis=2, keepdims=True)
    gc_row = jnp.sum(jnp.where(r <= c, gcol, 0.0), axis=1, keepdims=True)
    gc_last = jnp.sum(grow, axis=2, keepdims=True)
    decay = jnp.where(tril, jnp.exp(jnp.where(tril, gc_col - gc_row, 0.0)), 0.0)
    return bcol, gc_col, gc_last, decay


def _gdn_m(k, grow, brow):
    C = k.shape[1]
    strict = lax.broadcasted_iota(jnp.int32, (C, C), 0) > lax.broadcasted_iota(jnp.int32, (C, C), 1)
    bcol, _, _, decay = _gdn_gates(grow, brow)
    return jnp.where(strict, bcol * _twice(_bmm(k, k, BNT)) * decay, 0.0)


def _gdn_rest(q, k, v, grow, brow, t_mat, S):
    bcol, gc_col, gc_last, decay = _gdn_gates(grow, brow)
    qk = _twice(_bmm(q, k, BNT))
    k2, q2 = _twice(k), _twice(q)
    egc = jnp.exp(gc_col)
    u = _bmm(t_mat, v * bcol)
    w = _bmm(t_mat, k2 * (bcol * egc))
    v_new = u - _bmm(w, S)
    o = _bmm(q2 * egc, S) + _bmm(qk * decay, v_new)
    s_new = S * jnp.exp(gc_last) + _bmm(k2 * jnp.exp(gc_last - gc_col), v_new, BTN)
    return o, s_new


def _gdn_tb(T):
    return min(256, T)


def _gate_rows(g):
    T = g.shape[0]
    g = g[:, :GDN_V_HEADS].reshape(T // GDN_CHUNK, GDN_CHUNK, GDN_V_HEADS)
    return g.transpose(0, 2, 1)[:, :, None, :]


def _gate_cols(g):
    nc = g.shape[0]
    g = g[:, :, 0, :].transpose(0, 2, 1).reshape(nc * GDN_CHUNK, GDN_V_HEADS)
    return jnp.pad(g, ((0, 0), (0, LANE - GDN_V_HEADS)))


def _gdn_fwd(qn, kn, v, g, beta, gather=None):
    T = qn.shape[1]
    tb = _gdn_tb(T)
    nc = tb // GDN_CHUNK
    nsteps = T // tb
    quarters, buffers = gather if gather is not None else ((), ())
    ng = len(quarters)
    shapes = [a.shape for a in quarters]
    splits = [True] * ng

    def body(*refs):
        q_ref, k_ref, v_ref, g_ref, b_ref = refs[:5]
        src = refs[5:5 + ng]
        o_ref, sall_ref, tall_ref = refs[5 + 2 * ng:8 + 2 * ng]
        dst = refs[8 + 2 * ng:8 + 3 * ng]
        s_scr = refs[8 + 3 * ng]
        step = pl.program_id(0)

        @pl.when(step == 0)
        def _():
            s_scr[...] = jnp.zeros(s_scr.shape, F32)
            if ng:
                for cp in _gather_copies(shapes, splits, src, dst, *refs[9 + 3 * ng:])[0]:
                    cp.start()

        def chunk(ci, carry):
            rows = pl.ds(pl.multiple_of(ci * GDN_CHUNK, GDN_CHUNK), GDN_CHUNK)
            s = s_scr[...]
            sall_ref[ci] = s
            q, k = q_ref[:, rows, :].astype(F32), k_ref[:, rows, :].astype(F32)
            t_mat = _tri_inv(_gdn_m(k, g_ref[ci], b_ref[ci])).astype(BF16)
            tall_ref[ci] = t_mat
            o, s_new = _gdn_rest(q, k, v_ref[:, rows, :].astype(F32), g_ref[ci], b_ref[ci], t_mat.astype(F32), s)
            o_ref[:, rows, :] = o.astype(o_ref.dtype)
            s_scr[...] = s_new
            return carry

        lax.fori_loop(0, nc, chunk, 0)

        if ng:
            @pl.when(step == nsteps - 1)
            def _():
                _gather_arrival(shapes, splits, src, dst, *refs[9 + 3 * ng:])

    qk_spec = pl.BlockSpec((GDN_QK_HEADS, tb, GDN_HD), lambda i: (0, i, 0))
    v_spec = pl.BlockSpec((GDN_V_HEADS, tb, GDN_HD), lambda i: (0, i, 0))
    g_spec = pl.BlockSpec((nc, GDN_V_HEADS, 1, GDN_CHUNK), lambda i: (i, 0, 0, 0))
    anywhere = pl.BlockSpec(memory_space=pl.ANY)
    return pl.pallas_call(
        body, name="gdn_fwd", grid=(nsteps,),
        in_specs=[qk_spec, qk_spec, v_spec, g_spec, g_spec] + [anywhere] * (2 * ng),
        out_specs=[v_spec, pl.BlockSpec((nc, GDN_V_HEADS, GDN_HD, GDN_HD), lambda i: (i, 0, 0, 0)),
                   pl.BlockSpec((nc, GDN_V_HEADS, GDN_CHUNK, GDN_CHUNK), lambda i: (i, 0, 0, 0))] + [anywhere] * ng,
        out_shape=[jax.ShapeDtypeStruct((GDN_V_HEADS, T, GDN_HD), BF16),
                   jax.ShapeDtypeStruct((T // GDN_CHUNK, GDN_V_HEADS, GDN_HD, GDN_HD), F32),
                   jax.ShapeDtypeStruct((T // GDN_CHUNK, GDN_V_HEADS, GDN_CHUNK, GDN_CHUNK), BF16)]
        + [jax.ShapeDtypeStruct(b.shape, b.dtype) for b in buffers],
        input_output_aliases={5 + ng + a: 3 + a for a in range(ng)},
        scratch_shapes=[pltpu.VMEM((GDN_V_HEADS, GDN_HD, GDN_HD), F32)]
        + ([pltpu.SemaphoreType.DMA((6 * ng,)), pltpu.SemaphoreType.DMA((6 * ng,))] if ng else []),
        compiler_params=_params(("arbitrary",)),
    )(qn, kn, v, g, beta, *quarters, *buffers)


def _gdn_bwd(qn, kn, v, g, beta, sall, tall, do, scatter=()):
    T = qn.shape[1]
    tb = _gdn_tb(T)
    nc = tb // GDN_CHUNK
    nb = T // tb
    ns = len(scatter)

    def body(*refs):
        q_ref, k_ref, v_ref, g_ref, b_ref, sall_ref, tall_ref, do_ref = refs[:8]
        dq_ref, dk_ref, dv_ref, dg_ref, db_ref = refs[8 + ns:13 + ns]
        ds_scr = refs[13 + 2 * ns]
        comm = (refs[8:8 + ns], refs[13 + ns:13 + 2 * ns], *refs[14 + 2 * ns:])
        step = pl.program_id(0)

        @pl.when(step == 0)
        def _():
            ds_scr[...] = jnp.zeros(ds_scr.shape, F32)
            if ns:
                for cp in _scatter_copies(*comm):
                    cp.start()

        def chunk(cr, carry):
            ci = nc - 1 - cr
            rows = pl.ds(pl.multiple_of(ci * GDN_CHUNK, GDN_CHUNK), GDN_CHUNK)
            k, t_mat = k_ref[:, rows, :].astype(F32), tall_ref[ci].astype(F32)
            _, vjp = jax.vjp(_gdn_rest, q_ref[:, rows, :].astype(F32), k, v_ref[:, rows, :].astype(F32),
                             g_ref[ci], b_ref[ci], t_mat, sall_ref[ci])
            dq, dk, dv, dg, db, dt, ds = vjp((do_ref[:, rows, :].astype(F32), ds_scr[...]))
            _, vjp_m = jax.vjp(_gdn_m, k, g_ref[ci], b_ref[ci])
            dk_m, dg_m, db_m = vjp_m(_tri_inv_vjp(t_mat, dt))
            ds_scr[...] = ds
            dq_ref[:, rows, :] = dq
            dk_ref[:, rows, :] = dk + dk_m
            dv_ref[:, rows, :] = dv
            dg_ref[ci] = dg + dg_m
            db_ref[ci] = db + db_m
            return carry

        lax.fori_loop(0, nc, chunk, 0)

        if ns:
            @pl.when(step == nb - 1)
            def _():
                copies = _scatter_copies(*comm)
                for cp in copies:
                    cp.wait_recv()
                for cp in copies:
                    cp.wait_send()

    qk_spec = pl.BlockSpec((GDN_QK_HEADS, tb, GDN_HD), lambda i: (0, nb - 1 - i, 0))
    v_spec = pl.BlockSpec((GDN_V_HEADS, tb, GDN_HD), lambda i: (0, nb - 1 - i, 0))
    g_spec = pl.BlockSpec((nc, GDN_V_HEADS, 1, GDN_CHUNK), lambda i: (nb - 1 - i, 0, 0, 0))
    s_spec = pl.BlockSpec((nc, GDN_V_HEADS, GDN_HD, GDN_HD), lambda i: (nb - 1 - i, 0, 0, 0))
    t_spec = pl.BlockSpec((nc, GDN_V_HEADS, GDN_CHUNK, GDN_CHUNK), lambda i: (nb - 1 - i, 0, 0, 0))
    anywhere = pl.BlockSpec(memory_space=pl.ANY)
    return pl.pallas_call(
        body, name="gdn_bwd", grid=(nb,),
        in_specs=[qk_spec, qk_spec, v_spec, g_spec, g_spec, s_spec, t_spec, v_spec] + [anywhere] * ns,
        out_specs=[qk_spec, qk_spec, v_spec, g_spec, g_spec] + [anywhere] * ns,
        out_shape=[jax.ShapeDtypeStruct((GDN_QK_HEADS, T, GDN_HD), F32),
                   jax.ShapeDtypeStruct((GDN_QK_HEADS, T, GDN_HD), F32),
                   jax.ShapeDtypeStruct((GDN_V_HEADS, T, GDN_HD), F32),
                   jax.ShapeDtypeStruct(g.shape, F32), jax.ShapeDtypeStruct(g.shape, F32)]
        + _scatter_shapes(scatter),
        scratch_shapes=[pltpu.VMEM((GDN_V_HEADS, GDN_HD, GDN_HD), F32)]
        + ([pltpu.SemaphoreType.DMA((3 * ns,)), pltpu.SemaphoreType.DMA((3 * ns,))] if ns else []),
        compiler_params=_params(("arbitrary",)),
    )(qn, kn, v, g, beta, sall, tall, do, *scatter)


def _gnorm_core(o, z, w):
    return tuple(_rms_core(oh, w) * _silu(zh) for oh, zh in zip(o, z))


def _gnorm_fwd(o, pm, w, tm=256):
    T = pm.shape[0]
    tm = min(tm, T)

    def fn(i, ov, zv, wv):
        zf = zv.astype(F32)
        out = _gnorm_core(tuple(ov[h].astype(F32) for h in range(GDN_V_HEADS)), _heads(zf, GDN_V_HEADS), wv)
        return (jnp.concatenate(out, axis=1),)

    ins = [(o, "row", None), (pm, "row", (GDN_V, 2)), (w, "full", None)]
    return _rowcall("gnorm_fwd", fn, T, tm, ins, [((T, GDN_V), BF16, "row")])[0]


def _gnorm_bwd(o, pm, w, don, tm=128):
    T = pm.shape[0]
    tm = min(tm, T)

    def fn(i, ov, zv, wv, dv):
        zf, df = zv.astype(F32), dv.astype(F32)
        _, vjp = jax.vjp(_gnorm_core, tuple(ov[h].astype(F32) for h in range(GDN_V_HEADS)),
                         _heads(zf, GDN_V_HEADS), wv)
        do, dz, dw = vjp(_heads(df, GDN_V_HEADS))
        return jnp.stack(do), jnp.concatenate(dz, axis=1), dw

    ins = [(o, "row", None), (pm, "row", (GDN_V, 2)), (w, "full", None), (don, "row", None)]
    outs = [((GDN_V_HEADS, T, GDN_HD), BF16, "row"), ((T, GDN_V), BF16, "row"), ((1, GDN_HD), F32, "acc")]
    return _rowcall("gnorm_bwd", fn, T, tm, ins, outs)


def _ffn_act_fwd(name, up, conv_w, conv_b, tm=128):
    T = up.shape[0]
    tm = min(tm, T)

    def fn(i, x, halo, cw, cb):
        u = _conv_fwd(_taps(x.astype(F32), halo.astype(F32), 3, i), cw) + cb
        return (_silu(u[:, :DFF]) * u[:, DFF:],)

    ins = [(up, "row", None), (up, "prev", None), (conv_w, "full", None), (conv_b, "full", None)]
    return _rowcall(name, fn, T, tm, ins, [((T, DFF), BF16, "row")])[0]


def _ffn_act_bwd(name, up, conv_w, conv_b, dact, tm=128):
    T = up.shape[0]
    tm = min(tm, T)

    def fn(i, x, halo, cw, cb, da):
        taps = _taps(x.astype(F32), halo.astype(F32), 3, i)
        da = da.astype(F32)
        u = _conv_fwd(taps, cw) + cb
        gate, val = u[:, :DFF], u[:, DFF:]
        sg = _sigmoid(gate)
        dgate = da * val * sg * (1.0 + gate * (1.0 - sg))
        dval = da * gate * sg
        du = jnp.concatenate([dgate, dval], axis=1)
        return du, _conv_dw(du, taps), jnp.sum(du, axis=0, keepdims=True)

    ins = [(up, "row", None), (up, "prev", None), (conv_w, "full", None), (conv_b, "full", None),
           (dact, "row", None)]
    outs = [((T, DFF2), BF16, "row"), ((SUBLANE, DFF2), F32, "acc"), ((1, DFF2), F32, "acc")]
    return _rowcall(name, fn, T, tm, ins, outs)


def _ffn_conv_bwd(name, du, conv_w, tm=256):
    T = du.shape[0]
    tm = min(tm, T)
    n = T // tm

    def fn(i, dv, halo, cw):
        return (_conv_dx(dv.astype(F32), halo.astype(F32), cw, i, n),)

    ins = [(du, "row", None), (du, "next", None), (conv_w, "full", None)]
    return _rowcall(name, fn, T, tm, ins, [((T, DFF2), BF16, "row")])[0]


GROUP_ROWS = SWA_GROUP * SWA_BLOCK


def _attn_core(q, kp, kc, vp, vc, bias, sink, mask):
    kcat = jnp.concatenate([kp, kc], axis=0)
    vcat = jnp.concatenate([vp, vc], axis=0)
    s = _bdot(q * (SWA_HD ** -0.5), kcat, NT) + bias
    s = jnp.where(mask, s, NEG_INF)
    m = lax.stop_gradient(jnp.maximum(jnp.max(s, axis=-1, keepdims=True), sink))
    p = jnp.exp(s - m)
    denom = jnp.sum(p, axis=-1, keepdims=True) + jnp.exp(sink - m)
    return _bdot(p / denom, vcat)


def _attn_mask(i):
    qi = lax.broadcasted_iota(jnp.int32, (GROUP_ROWS, 2 * SWA_BLOCK), 0) & (SWA_BLOCK - 1)
    ki = lax.broadcasted_iota(jnp.int32, (GROUP_ROWS, 2 * SWA_BLOCK), 1)
    dist = qi + SWA_BLOCK - ki
    return (dist >= 0) & (dist < SWA_BLOCK) & ((ki >= SWA_BLOCK) | (i > 0))


def _head_cols(h):
    return slice(h * SWA_HD, (h + 1) * SWA_HD)


def _stacked_heads(ref, j):
    return jnp.concatenate([ref[:, _head_cols(SWA_GROUP * j + g)].astype(F32) for g in range(SWA_GROUP)], axis=0)


def _flat_operands(j, q_ref, kvc_ref, kvp_ref, b_ref, s_ref):
    heads = slice(SWA_GROUP * j, SWA_GROUP * (j + 1))
    sink = jnp.concatenate([jnp.broadcast_to(s_ref[j, g:g + 1, 0:1], (SWA_BLOCK, 1)) for g in range(SWA_GROUP)],
                           axis=0)
    k_cols, v_cols = _head_cols(j), _head_cols(SWA_KV_HEADS + j)
    return (_stacked_heads(q_ref, j), kvp_ref[:, k_cols].astype(F32), kvc_ref[:, k_cols].astype(F32),
            kvp_ref[:, v_cols].astype(F32), kvc_ref[:, v_cols].astype(F32),
            b_ref[heads].reshape(GROUP_ROWS, 2 * SWA_BLOCK), sink)


def _store_heads(ref, j, stacked):
    for g in range(SWA_GROUP):
        ref[:, _head_cols(SWA_GROUP * j + g)] = stacked[g * SWA_BLOCK:(g + 1) * SWA_BLOCK].astype(ref.dtype)


def _attn_fwd_flat(q, kv, bias, sinks):
    T = q.shape[0]
    nb = T // SWA_BLOCK

    def body(q_ref, kvc_ref, kvp_ref, b_ref, s_ref, o_ref):
        mask = _attn_mask(pl.program_id(0))
        operands = [_flat_operands(j, q_ref, kvc_ref, kvp_ref, b_ref, s_ref) for j in range(SWA_KV_HEADS)]
        outs = [_attn_core(*ops, mask) for ops in operands]
        for j in range(SWA_KV_HEADS):
            _store_heads(o_ref, j, outs[j])

    q_spec = pl.BlockSpec((SWA_BLOCK, q.shape[1]), lambda i: (i, 0))
    cur = pl.BlockSpec((SWA_BLOCK, kv.shape[1]), lambda i: (i, 0))
    prev = pl.BlockSpec((SWA_BLOCK, kv.shape[1]), lambda i: (jnp.maximum(i - 1, 0), 0))
    return pl.pallas_call(
        body, name="attn_fwd", grid=(nb,),
        in_specs=[q_spec, cur, prev, pl.BlockSpec(bias.shape, lambda i: (0, 0, 0)),
                  pl.BlockSpec(sinks.shape, lambda i: (0, 0, 0))],
        out_specs=q_spec, out_shape=jax.ShapeDtypeStruct(q.shape, BF16),
        compiler_params=_params(("arbitrary",)),
    )(q, kv, kv, bias, sinks)


def _attn_bwd_flat(q, kv, bias, sinks, do):
    T = q.shape[0]
    nb = T // SWA_BLOCK

    def body(q_ref, kvc_ref, kvp_ref, b_ref, s_ref, do_ref, dq_ref, dkv_ref, db_ref, dsk_ref, carry):
        i = pl.program_id(0)

        @pl.when(i < nb)
        def _():
            mask = _attn_mask(i)
            operands = [_flat_operands(j, q_ref, kvc_ref, kvp_ref, b_ref, s_ref) for j in range(SWA_KV_HEADS)]
            cots = [_stacked_heads(do_ref, j) for j in range(SWA_KV_HEADS)]
            grads = [jax.vjp(functools.partial(_attn_core, mask=mask), *ops)[1](cot)
                     for ops, cot in zip(operands, cots)]
            for j, (dq, dkp, dkc, dvp, dvc, db, dsc) in enumerate(grads):
                heads = slice(SWA_GROUP * j, SWA_GROUP * (j + 1))
                k_cols, v_cols = _head_cols(j), _head_cols(SWA_KV_HEADS + j)
                _store_heads(dq_ref, j, dq)
                db = db.reshape(SWA_GROUP, SWA_BLOCK, 2 * SWA_BLOCK)
                dsk = jnp.concatenate(
                    [jnp.broadcast_to(jnp.sum(dsc[g * SWA_BLOCK:(g + 1) * SWA_BLOCK], axis=0, keepdims=True),
                                      (1, LANE)) for g in range(SWA_GROUP)], axis=0)

                @pl.when(i == 0)
                def _():
                    db_ref[heads] = db
                    dsk_ref[j] = dsk

                @pl.when(i > 0)
                def _():
                    db_ref[heads] += db
                    dsk_ref[j] += dsk
                    dkv_ref[:, k_cols] = (carry[:, k_cols] + dkp).astype(dkv_ref.dtype)
                    dkv_ref[:, v_cols] = (carry[:, v_cols] + dvp).astype(dkv_ref.dtype)

                carry[:, k_cols] = dkc
                carry[:, v_cols] = dvc

        @pl.when(i == nb)
        def _():
            dkv_ref[...] = carry[...].astype(dkv_ref.dtype)

    last = nb - 1
    q_spec = pl.BlockSpec((SWA_BLOCK, q.shape[1]), lambda i: (jnp.minimum(i, last), 0))
    cur = pl.BlockSpec((SWA_BLOCK, kv.shape[1]), lambda i: (jnp.minimum(i, last), 0))
    prev = pl.BlockSpec((SWA_BLOCK, kv.shape[1]), lambda i: (jnp.clip(i - 1, 0, last), 0))
    b_spec = pl.BlockSpec(bias.shape, lambda i: (0, 0, 0))
    s_spec = pl.BlockSpec(sinks.shape, lambda i: (0, 0, 0))
    return pl.pallas_call(
        body, name="attn_bwd", grid=(nb + 1,),
        in_specs=[q_spec, cur, prev, b_spec, s_spec, q_spec],
        out_specs=[q_spec, prev, b_spec, s_spec],
        out_shape=[jax.ShapeDtypeStruct(q.shape, BF16), jax.ShapeDtypeStruct(kv.shape, BF16),
                   jax.ShapeDtypeStruct(bias.shape, F32), jax.ShapeDtypeStruct(sinks.shape, F32)],
        scratch_shapes=[pltpu.VMEM((SWA_BLOCK, kv.shape[1]), F32)],
        compiler_params=_params(("arbitrary",)),
    )(q, kv, kv, bias, sinks, do)


def _rel_onehot():
    qi = jnp.arange(SWA_BLOCK)[:, None]
    ki = jnp.arange(2 * SWA_BLOCK)[None, :]
    n = jnp.maximum(qi + SWA_BLOCK - ki, 0)
    max_exact = REL_BUCKETS // 2
    nf = jnp.maximum(n, 1).astype(F32)
    large = max_exact + (jnp.log(nf / max_exact) / math.log(REL_MAX_DISTANCE / max_exact)
                         * (REL_BUCKETS - max_exact)).astype(jnp.int32)
    bucket = jnp.where(n < max_exact, n, jnp.minimum(large, REL_BUCKETS - 1)).reshape(-1)
    return (bucket[None, :] == jnp.arange(REL_BUCKETS)[:, None]).astype(F32)


def _final(h, w, target, tm=256):
    T = h.shape[0]
    tm = min(tm, T)

    def fn(i, hv, wv, tv):
        y, vjp = jax.vjp(_rms_core, hv, wv)
        err = y - tv
        dh, dw = vjp(err * (1.0 / D))
        part = 0.5 * jnp.sum(jnp.sum(err * err, axis=1, keepdims=True) * (1.0 / D), axis=0, keepdims=True)
        return jnp.broadcast_to(part, (SUBLANE, LANE)), dh, dw

    ins = [(h, "row", None), (w, "full", None), (target, "row", None)]
    outs = [((SUBLANE, LANE), F32, "acc"), ((T, D), F32, "row"), ((1, D), F32, "acc")]
    return _rowcall("final", fn, T, tm, ins, outs)


def _ffn_fwd(tag, h, P, layer):
    n = _rms_fwd(f"{tag}_rms", h, P["ffn_norm_w"][layer:layer + 1])
    up = _mm_up(f"{tag}_up", n, P["w_up"], layer)
    act = _ffn_act_fwd(f"{tag}_act", up, P["ffn_conv_w"][layer], P["ffn_conv_b"][layer:layer + 1])
    out = _mm_nn(f"{tag}_down", act, P["w_down"][layer], F32, res=h)
    return out, (n, up, act)


def _ffn_bwd(tag, h, saved, dout, P, layer, into=(None, None)):
    n, up, act = saved
    cw, cb = P["ffn_conv_w"][layer], P["ffn_conv_b"][layer:layer + 1]
    dact = _mm_nt(f"{tag}_down_dx", dout, P["w_down"][layer], BF16)
    g_down = _mm_down_tn(f"{tag}_down_dw", act, dout, layer, into[1])
    du, dcw, dcb = _ffn_act_bwd(f"{tag}_act_bwd", up, cw, cb, dact)
    dup = _ffn_conv_bwd(f"{tag}_conv_bwd", du, cw)
    g_up = _mm_up_tn(f"{tag}_up_dw", n, dup, layer, into[0])
    dn = _mm_up_nt(f"{tag}_up_dx", dup, P["w_up"], layer)
    dh, dnw = _rms_bwd(f"{tag}_rms_bwd", h, [(P["ffn_norm_w"][layer:layer + 1], dn)], [dout])
    return dh, dict(w_down=g_down, w_up=g_up, conv_w=dcw[:3], conv_b=dcb, norm_w=dnw)


def _local_step(x, target, P, late=None, pair_sums=None):
    T = x.shape[0]
    n0 = _rms_fwd("a_rms", x, P["a_norm_w"])
    pm = _mm_nn("gdn_in", n0, P["w_in_main"], BF16)
    pba = _mm_nn("gdn_in_ba", n0, P["w_in_ba"], F32)
    qn, kn, v, beta, g = _gdn_pre_fwd(pm, pba, P["a_conv_w"], P["a_log"], P["dt_bias"])
    g_rows, beta_rows = _gate_rows(g), _gate_rows(beta)
    o, sall, tall, *gathered = _gdn_fwd(qn, kn, v, g_rows, beta_rows, gather=late)
    if late is not None:
        P = {**P, **_late_weights(gathered)}
    on = _gnorm_fwd(o, pm, P["a_out_norm_w"])
    h1 = _mm_nn("gdn_out", on, P["w_out"], F32, res=x)
    h2, ffn0 = _ffn_fwd("ffn0", h1, P, 0)
    nkv = _rms_fwd("kv_rms", h2, P["kv_norm_w"])
    kv = _mm_nn("kv_proj", nkv, P["w_kv"], BF16)
    nb = _rms_fwd("b_rms", h2, P["b_norm_w"])
    qp = _mm_nn("q_proj", nb, P["w_q"], BF16)
    onehot = _rel_onehot()
    bias = _mm_nn("rel_bias", P["rel_table_t"], onehot, F32, precision=HIGHEST)
    bias = bias.reshape(SWA_Q_HEADS, SWA_BLOCK, 2 * SWA_BLOCK)
    oa = _attn_fwd_flat(qp, kv, bias, P["sinks"])
    h3 = _mm_nn("o_proj", oa, P["w_o"], F32, res=h2)
    h4, ffn1 = _ffn_fwd("ffn1", h3, P, 1)
    loss, dh4, d_final = _final(h4, P["final_norm_w"], target)

    dh3, gf1 = _ffn_bwd("ffn1", h3, ffn1, dh4, P, 1)
    doa = _mm_nt("o_proj_dx", dh3, P["w_o"], BF16)
    g_wo = _mm_tn("o_proj_dw", oa, dh3)
    dqp, dkv, dbias, dsinks = _attn_bwd_flat(qp, kv, bias, P["sinks"], doa)
    g_wq = _mm_tn("q_proj_dw", nb, dqp)
    dnb = _mm_nt("q_proj_dx", dqp, P["w_q"], F32)
    g_wkv = _mm_tn("kv_proj_dw", nkv, dkv)
    dnkv = _mm_nt("kv_proj_dx", dkv, P["w_kv"], F32)
    dh2, d_bnorm, d_kvnorm = _rms_bwd("b_kv_rms_bwd", h2, [(P["b_norm_w"], dnb), (P["kv_norm_w"], dnkv)], [dh3])
    g_table = _mm_nt("rel_bias_dw", onehot, dbias.reshape(SWA_Q_HEADS, -1), F32, precision=HIGHEST)
    dh1, gf0 = _ffn_bwd("ffn0", h1, ffn0, dh2, P, 0, into=(gf1["w_up"], gf1["w_down"]))
    don = _mm_nt("gdn_out_dx", dh1, P["w_out"], BF16)
    g_wout = _mm_tn("gdn_out_dw", on, dh1)
    do, dz, d_gnorm = _gnorm_bwd(o, pm, P["a_out_norm_w"], don)
    ready = dict(a_w_out=g_wout, w_kv=g_wkv, b_w_q=g_wq, b_w_o=g_wo, ffn_w_up=gf0["w_up"], ffn_w_down=gf0["w_down"])
    pairs = pair_sums(ready, "early") if pair_sums is not None else []
    dq, dk, dv, dg, dbeta, *parts = _gdn_bwd(qn, kn, v, g_rows, beta_rows, sall, tall, do, scatter=pairs)
    dy, dpba, d_aconv, d_alog, d_dtb = _gdn_pre_bwd(pm, pba, P["a_conv_w"], P["a_log"], P["dt_bias"],
                                                    dq, dk, dv, _gate_cols(dbeta), _gate_cols(dg))
    dpm = _gdn_conv_bwd(dy, dz, P["a_conv_w"])
    g_win_main = _mm_tn("gdn_in_dw", n0, dpm)
    g_win_ba = _mm_tn("gdn_in_ba_dw", n0, dpba)
    nh = GDN_V_HEADS
    g_win = jnp.concatenate([g_win_main, g_win_ba[:, :nh], g_win_ba[:, LANE:LANE + nh]], axis=1)
    last_pair = pair_sums({"a_w_in": g_win}, "late") if pair_sums is not None else []
    dn0 = _mm_nt("gdn_in_dx", dpm, P["w_in_main"], F32, scatter=last_pair)
    dn0, last_parts = (dn0[0], dn0[1:]) if last_pair else (dn0, [])
    dn0 = _mm_nt("gdn_in_ba_dx", dpba, P["w_in_ba"], F32, res=dn0)
    dx, d_anorm = _rms_bwd("a_rms_bwd", x, [(P["a_norm_w"], dn0)], [dh1])

    nh = GDN_V_HEADS
    grads = dict(
        a_norm_w=d_anorm,
        a_w_in=g_win,
        a_conv_w=d_aconv[:4], a_a_log=d_alog[:, :nh], a_dt_bias=d_dtb[:, :nh], a_out_norm_w=d_gnorm,
        a_w_out=g_wout, kv_norm_w=d_kvnorm, w_kv=g_wkv, b_norm_w=d_bnorm, b_w_q=g_wq,
        b_sinks=dsinks[:, :, 0].reshape(1, SWA_Q_HEADS), b_w_o=g_wo, rel_bias_table=g_table,
        ffn_norm_w=jnp.concatenate([gf0["norm_w"], gf1["norm_w"]], axis=0),
        ffn_w_up=gf0["w_up"],
        ffn_conv_w=jnp.stack([gf0["conv_w"], gf1["conv_w"]], axis=0),
        ffn_conv_b=jnp.concatenate([gf0["conv_b"], gf1["conv_b"]], axis=0),
        ffn_w_down=gf0["w_down"],
        final_norm_w=d_final,
    )
    scattered = dict(zip([n for n in BIG if n in ready], zip(pairs, parts)))
    scattered.update(zip(["a_w_in"], zip(last_pair, last_parts)))
    return loss, dx, grads, scattered


HBM_SPEC = pl.BlockSpec(memory_space=pltpu.HBM)
VMEM_SPEC = pl.BlockSpec(memory_space=pltpu.VMEM)


def _coords():
    return lax.axis_index("x"), lax.axis_index("y"), lax.axis_index("c")


def _remote(src, dst, send_sem, recv_sem, device):
    return pltpu.make_async_remote_copy(src_ref=src, dst_ref=dst, send_sem=send_sem, recv_sem=recv_sem,
                                        device_id=device, device_id_type=MESH)


def _other_chips(x, y):
    return [(1 - x, y), (x, 1 - y), (1 - x, 1 - y)]


def _gather_copies(shapes, split, ins, outs, send_sems, recv_sems):
    x, y, c = _coords()
    p = 2 * x + y
    ici, forwards, from_sibling = [], [], []
    for a, shape in enumerate(shapes):
        h = shape[0] // 2
        for j, chip in enumerate(_other_chips(x, y)):
            q = 2 * chip[0] + chip[1]
            if split[a]:
                mine, theirs = pl.ds(c * h, h), pl.ds((1 - c) * h, h)
                ici.append(_remote(ins[a].at[mine], outs[a].at[p, mine], send_sems.at[6 * a + j],
                                   recv_sems.at[6 * a + j], (*chip, c)))
                land = outs[a].at[q, mine]
                forwards.append(_remote(land, land, send_sems.at[6 * a + 3 + j], recv_sems.at[6 * a + 3 + j],
                                        (x, y, 1 - c)))
                land = outs[a].at[q, theirs]
                from_sibling.append(_remote(land, land, send_sems.at[6 * a + 3 + j], recv_sems.at[6 * a + 3 + j],
                                            (x, y, 1 - c)))
            else:
                ici.append(_remote(ins[a], outs[a].at[p], send_sems.at[6 * a + j], recv_sems.at[6 * a + j],
                                   (*chip, c)))
                forwards.append(None)
    return ici, forwards, from_sibling


def _gather_arrival(shapes, split, ins, outs, send_sems, recv_sems):
    x, y, c = _coords()
    ici, forwards, from_sibling = _gather_copies(shapes, split, ins, outs, send_sems, recv_sems)
    k = 0
    for a, shape in enumerate(shapes):
        h = shape[0] // 2
        for j, chip in enumerate(_other_chips(x, y)):
            q = 2 * chip[0] + chip[1]
            land = outs[a].at[q, pl.ds(c * h, h)] if split[a] else outs[a].at[q]
            _remote(land, land, send_sems.at[6 * a + j], recv_sems.at[6 * a + j], (*chip, c)).wait_recv()
            if forwards[k] is not None:
                forwards[k].start()
            k += 1
    for cp in from_sibling:
        cp.wait_recv()
    for cp in ici + [f for f in forwards if f is not None]:
        cp.wait_send()


def _all_gather(arrs, split, remote):
    n = len(arrs)
    now = [a for a in range(n) if remote[a]]
    shapes = [arrs[a].shape for a in now]
    splits = [split[a] for a in now]

    def body(*refs):
        ins, outs, stage = refs[:n], refs[n:2 * n], refs[2 * n:3 * n]
        send_sems, recv_sems, in_sems, out_sems = refs[3 * n:]
        p = 2 * lax.axis_index("x") + lax.axis_index("y")
        gathered = ([ins[a] for a in now], [outs[a] for a in now], send_sems, recv_sems)
        loads = [pltpu.make_async_copy(ins[a], stage[a], in_sems.at[a]) for a in range(n)]
        for cp in loads:
            cp.start()
        for cp in _gather_copies(shapes, splits, *gathered)[0]:
            cp.start()
        stores = [pltpu.make_async_copy(stage[a], outs[a].at[p], out_sems.at[a]) for a in range(n)]
        for a in range(n):
            loads[a].wait()
            stores[a].start()
        _gather_arrival(shapes, splits, *gathered)
        for cp in stores:
            cp.wait()

    return pl.pallas_call(
        body, name="weights_all_gather", in_specs=[HBM_SPEC] * n, out_specs=[HBM_SPEC] * n,
        out_shape=[jax.ShapeDtypeStruct((N_CHIPS,) + a.shape, a.dtype) for a in arrs],
        scratch_shapes=[pltpu.VMEM(a.shape, a.dtype) for a in arrs]
        + [pltpu.SemaphoreType.DMA((6 * len(now),)), pltpu.SemaphoreType.DMA((6 * len(now),)),
           pltpu.SemaphoreType.DMA((n,)), pltpu.SemaphoreType.DMA((n,))],
        compiler_params=pltpu.CompilerParams(vmem_limit_bytes=VMEM_LIMIT),
    )(*arrs)


PAIR_SWAP_PIECES = 2


def _pair_swap(gs, tag):
    n = len(gs)

    def body(*refs):
        ins, other = refs[:n], refs[n:2 * n]
        send_sems, recv_sems = refs[2 * n:]
        x, y, c = _coords()
        cps = []
        for a in range(n):
            h = gs[a].shape[1] // 2
            piece = h // PAIR_SWAP_PIECES
            for q in range(N_CHIPS):
                for r in range(PAIR_SWAP_PIECES):
                    k = (a * N_CHIPS + q) * PAIR_SWAP_PIECES + r
                    cp = _remote(ins[a].at[q, pl.ds((1 - c) * h + r * piece, piece)],
                                 other[a].at[q, pl.ds(r * piece, piece)], send_sems.at[k], recv_sems.at[k],
                                 (x, y, 1 - c))
                    cp.start()
                    cps.append(cp)
        for cp in cps:
            cp.wait()

    half = [jax.ShapeDtypeStruct((N_CHIPS, g.shape[1] // 2, g.shape[2]), g.dtype) for g in gs]
    nsem = n * N_CHIPS * PAIR_SWAP_PIECES
    return pl.pallas_call(
        body, name=f"grads_pair_swap_{tag}", in_specs=[HBM_SPEC] * n, out_specs=[HBM_SPEC] * n, out_shape=half,
        scratch_shapes=[pltpu.SemaphoreType.DMA((nsem,)), pltpu.SemaphoreType.DMA((nsem,))],
    )(*gs)


def _scatter_copies(ins, outs, send_sems, recv_sems):
    x, y, c = _coords()
    copies = []
    for a in range(len(ins)):
        for j, chip in enumerate(_other_chips(x, y)):
            q = 2 * chip[0] + chip[1]
            copies.append(_remote(ins[a].at[q], outs[a].at[j], send_sems.at[3 * a + j], recv_sems.at[3 * a + j],
                                  (*chip, c)))
    return copies


def _scatter_shapes(ps):
    return [jax.ShapeDtypeStruct((N_CHIPS - 1,) + a.shape[1:], a.dtype) for a in ps]


def _pair_share(rs):
    n = len(rs)

    def body(*refs):
        ins, outs, stage = refs[:n], refs[n:2 * n], refs[2 * n:3 * n]
        send_sems, recv_sems, in_sems, out_sems = refs[3 * n:]
        x, y, c = _coords()

        def mine(a):
            h = rs[a].shape[0]
            return outs[a].at[pl.ds(c * h, h)]

        loads = [pltpu.make_async_copy(ins[a], stage[a], in_sems.at[a]) for a in range(n)]
        for cp in loads:
            cp.start()
        sends = [_remote(ins[a], mine(a), send_sems.at[a], recv_sems.at[a], (x, y, 1 - c)) for a in range(n)]
        for cp in sends:
            cp.start()
        stores = [pltpu.make_async_copy(stage[a], mine(a), out_sems.at[a]) for a in range(n)]
        for a in range(n):
            loads[a].wait()
            stores[a].start()
        for a in range(n):
            h = rs[a].shape[0]
            land = outs[a].at[pl.ds((1 - c) * h, h)]
            _remote(land, land, send_sems.at[a], recv_sems.at[a], (x, y, 1 - c)).wait_recv()
        for cp in sends:
            cp.wait_send()
        for cp in stores:
            cp.wait()

    return pl.pallas_call(
        body, name="grads_pair_share", in_specs=[HBM_SPEC] * n, out_specs=[HBM_SPEC] * n,
        out_shape=[jax.ShapeDtypeStruct((2 * a.shape[0], a.shape[1]), a.dtype) for a in rs],
        scratch_shapes=[pltpu.VMEM(a.shape, a.dtype) for a in rs] + [pltpu.SemaphoreType.DMA((n,))] * 4,
        compiler_params=pltpu.CompilerParams(vmem_limit_bytes=VMEM_LIMIT),
    )(*rs)


def _small_all_reduce(buf):
    R = buf.shape[0]
    ndev = 2 * N_CHIPS

    def body(in_ref, out_ref, gath, send_sems, recv_sems):
        x, y, c = _coords()
        me = 4 * x + 2 * y + c
        gath[me] = in_ref[...]
        peers = []
        for d in range(1, ndev):
            px = 1 - x if d & 4 else x
            py = 1 - y if d & 2 else y
            pc = 1 - c if d & 1 else c
            peers.append((px, py, pc))
        sends = []
        for d, peer in enumerate(peers):
            cp = _remote(in_ref, gath.at[me], send_sems.at[d], recv_sems.at[d], peer)
            cp.start()
            sends.append(cp)
        for d, peer in enumerate(peers):
            land = gath.at[4 * peer[0] + 2 * peer[1] + peer[2]]
            _remote(land, land, send_sems.at[d], recv_sems.at[d], peer).wait_recv()
        for cp in sends:
            cp.wait_send()
        acc = gath[0]
        for s in range(1, ndev):
            acc = acc + gath[s]
        out_ref[...] = acc

    return pl.pallas_call(
        body, name="small_all_reduce", in_specs=[VMEM_SPEC], out_specs=VMEM_SPEC,
        out_shape=jax.ShapeDtypeStruct(buf.shape, F32),
        scratch_shapes=[pltpu.VMEM((ndev, R, LANE), F32), pltpu.SemaphoreType.DMA((ndev - 1,)),
                        pltpu.SemaphoreType.DMA((ndev - 1,))],
    )(buf)


def _pair_add(name, own, other):
    h = own.shape[1]
    tm = _tile(h, (128, 64, 32, 16))

    def fn(i, a, b):
        return (a + b,)

    return _rowcall(name, fn, h, tm, [(own, "row", None), (other, "row", None)], [(own.shape, BF16, "row")])[0]


def _chip_add(name, own, parts):
    h = parts.shape[1]
    tm = _tile(h, (128, 64, 32, 16))

    def fn(i, o, a):
        a = a.astype(F32)
        return (((o.astype(F32) + a[0]) + a[1]) + a[2],)

    return _rowcall(name, fn, h, tm, [(own, "row", None), (parts, "row", None)], [(parts.shape[1:], F32, "row")])[0]


def _adamw(name, w, g, m, v):
    R = w.shape[0]
    tm = _tile(R, (256, 128, 64, 32, 16, 8))

    def fn(i, wv, gv, mv, vv):
        m2 = ADAM_B1 * mv + (1.0 - ADAM_B1) * gv
        v2 = ADAM_B2 * vv + (1.0 - ADAM_B2) * (gv * gv)
        m_hat = m2 / (1.0 - ADAM_B1 ** ADAM_STEP)
        v_hat = v2 / (1.0 - ADAM_B2 ** ADAM_STEP)
        delta = -ADAM_LR * (m_hat / (jnp.sqrt(v_hat) + ADAM_EPS) + ADAM_WD * wv)
        return delta, m2, v2

    ins = [(a, "row", None) for a in (w, g, m, v)]
    return _rowcall(name, fn, R, tm, ins, [(w.shape, F32, "row")] * 3)


def _pack(arrs):
    flat = jnp.concatenate([a.reshape(-1).astype(F32) for a in arrs])
    size = flat.shape[0]
    padded = -(-size // (SUBLANE * LANE)) * SUBLANE * LANE
    return jnp.pad(flat, (0, padded - size)).reshape(-1, LANE)


def _unpack(buf, shapes):
    flat = buf.reshape(-1)
    out, off = [], 0
    for s in shapes:
        size = math.prod(s)
        out.append(flat[off:off + size].reshape(s))
        off += size
    return out


BIG = ("a_w_in", "a_w_out", "w_kv", "b_w_q", "b_w_o", "ffn_w_up", "ffn_w_down")
WEIGHTS = ("a_norm_w", "a_w_in", "a_conv_w", "a_a_log", "a_dt_bias", "a_out_norm_w", "a_w_out", "kv_norm_w", "w_kv",
           "b_norm_w", "b_w_q", "b_sinks", "b_w_o", "rel_bias_table", "ffn_norm_w", "ffn_w_up", "ffn_conv_w",
           "ffn_conv_b", "ffn_w_down", "final_norm_w")
SMALL = tuple(n for n in WEIGHTS if n not in BIG)
SMALL_SHARDED = {"a_norm_w": 1, "a_conv_w": 2, "ffn_conv_w": 2}


def _quarter_2d(name, a):
    if name in ("ffn_w_up", "ffn_w_down"):
        return a.reshape(a.shape[0] * a.shape[1], a.shape[2])
    return a.reshape(a.shape[-2], a.shape[-1])


def _whole_weights(w):
    bigs = [_quarter_2d(n, w[n]).astype(BF16) for n in BIG]
    smalls = [w["a_norm_w"], w["a_conv_w"][0], w["ffn_conv_w"].reshape(6, DFF2_SHARD)]
    remote = [True] + [False] * (len(bigs) - 1) + [True] * len(smalls)
    g = _all_gather(bigs + smalls, [True] * len(bigs) + [False] * len(smalls), remote)
    w_in = g[0].transpose(1, 0, 2).reshape(D, GDN_IN)
    nh = GDN_V_HEADS
    zpad = jnp.zeros((D, LANE - nh), BF16)
    w_in_ba = jnp.concatenate([w_in[:, GDN_MAIN:GDN_MAIN + nh], zpad, w_in[:, GDN_MAIN + nh:], zpad], axis=1)
    lane_pad = lambda a: jnp.pad(a, ((0, 0), (0, LANE - nh)))
    early = dict(
        a_norm_w=g[7].reshape(1, D), w_in_main=w_in[:, :GDN_MAIN], w_in_ba=w_in_ba,
        a_conv_w=g[8].transpose(1, 0, 2).reshape(4, GDN_CONV), a_log=lane_pad(w["a_a_log"]),
        dt_bias=lane_pad(w["a_dt_bias"]), a_out_norm_w=w["a_out_norm_w"],
        kv_norm_w=w["kv_norm_w"].reshape(1, D), b_norm_w=w["b_norm_w"],
        sinks=jnp.broadcast_to(w["b_sinks"].reshape(SWA_KV_HEADS, SWA_GROUP, 1), (SWA_KV_HEADS, SWA_GROUP, LANE)),
        rel_table_t=w["rel_bias_table"].T, ffn_norm_w=w["ffn_norm_w"],
        ffn_conv_w=g[9].reshape(N_CHIPS, 2, 3, DFF2_SHARD).transpose(1, 2, 0, 3).reshape(2, 3, DFF2),
        ffn_conv_b=w["ffn_conv_b"], final_norm_w=w["final_norm_w"].reshape(1, D),
    )
    return early, (bigs[1:], g[1:len(bigs)])


def _late_weights(g):
    return dict(
        w_out=g[0].reshape(GDN_V, D), w_kv=g[1].reshape(D, 2 * SWA_KV_HEADS * SWA_HD), w_q=g[2].reshape(D, D),
        w_o=g[3].reshape(D, D), w_up=g[4].reshape(N_CHIPS, 2, D, DFF2_SHARD),
        w_down=g[5].reshape(N_CHIPS, 2, DFF_SHARD, D).transpose(1, 0, 2, 3).reshape(2, DFF, D),
    )


def _chip_major(name, g):
    if name == "a_w_in":
        return g.reshape(D, N_CHIPS, GDN_IN_SHARD).transpose(1, 0, 2)
    if name == "ffn_w_up":
        return g.reshape(N_CHIPS, 2 * D, DFF2_SHARD)
    if name == "ffn_w_down":
        return g.reshape(N_CHIPS, 2 * DFF_SHARD, D)
    return g.reshape(N_CHIPS, g.shape[0] // N_CHIPS, g.shape[1])


def kernel(x, a_norm_w, a_w_in, a_conv_w, a_a_log, a_dt_bias, a_out_norm_w, a_w_out, kv_norm_w, w_kv, b_norm_w, b_w_q, b_sinks, b_w_o, rel_bias_table, ffn_norm_w, ffn_w_up, ffn_conv_w, ffn_conv_b, ffn_w_down, final_norm_w, loss_target, m_a_norm_w, m_a_w_in, m_a_conv_w, m_a_a_log, m_a_dt_bias, m_a_out_norm_w, m_a_w_out, m_kv_norm_w, m_w_kv, m_b_norm_w, m_b_w_q, m_b_sinks, m_b_w_o, m_rel_bias_table, m_ffn_norm_w, m_ffn_w_up, m_ffn_conv_w, m_ffn_conv_b, m_ffn_w_down, m_final_norm_w, v_a_norm_w, v_a_w_in, v_a_conv_w, v_a_a_log, v_a_dt_bias, v_a_out_norm_w, v_a_w_out, v_kv_norm_w, v_w_kv, v_b_norm_w, v_b_w_q, v_b_sinks, v_b_w_o, v_rel_bias_table, v_ffn_norm_w, v_ffn_w_up, v_ffn_conv_w, v_ffn_conv_b, v_ffn_w_down, v_final_norm_w):
    w = dict(zip(WEIGHTS, (a_norm_w, a_w_in, a_conv_w, a_a_log, a_dt_bias, a_out_norm_w, a_w_out, kv_norm_w, w_kv,
                           b_norm_w, b_w_q, b_sinks, b_w_o, rel_bias_table, ffn_norm_w, ffn_w_up, ffn_conv_w,
                           ffn_conv_b, ffn_w_down, final_norm_w)))
    m = dict(zip(WEIGHTS, (m_a_norm_w, m_a_w_in, m_a_conv_w, m_a_a_log, m_a_dt_bias, m_a_out_norm_w, m_a_w_out,
                           m_kv_norm_w, m_w_kv, m_b_norm_w, m_b_w_q, m_b_sinks, m_b_w_o, m_rel_bias_table,
                           m_ffn_norm_w, m_ffn_w_up, m_ffn_conv_w, m_ffn_conv_b, m_ffn_w_down, m_final_norm_w)))
    v = dict(zip(WEIGHTS, (v_a_norm_w, v_a_w_in, v_a_conv_w, v_a_a_log, v_a_dt_bias, v_a_out_norm_w, v_a_w_out,
                           v_kv_norm_w, v_w_kv, v_b_norm_w, v_b_w_q, v_b_sinks, v_b_w_o, v_rel_bias_table,
                           v_ffn_norm_w, v_ffn_w_up, v_ffn_conv_w, v_ffn_conv_b, v_ffn_w_down, v_final_norm_w)))
    T = x.shape[1]
    chip = 2 * lax.axis_index("x") + lax.axis_index("y")

    core = lax.axis_index("c")

    def pair_sums(named, tag):
        names = [n for n in BIG if n in named]
        whole = [_chip_major(n, named[n]) for n in names]
        other = _pair_swap(whole, tag)
        own = [lax.dynamic_slice_in_dim(g, core * (g.shape[1] // 2), g.shape[1] // 2, 1) for g in whole]
        return [_pair_add(f"pair_add_{n}", a, b) for n, a, b in zip(names, own, other)]

    early, late = _whole_weights(w)
    loss_part, dx, grads, scattered = _local_step(x.reshape(T, D), loss_target.reshape(T, D), early, late, pair_sums)

    assert all(n in scattered for n in BIG)
    halves = [_chip_add(f"chip_add_{n}", lax.dynamic_index_in_dim(scattered[n][0], chip, 0, keepdims=False),
                        scattered[n][1]) for n in BIG]
    quarter = _pair_share(halves)
    out_g, out_d, out_m, out_v = {}, {}, {}, {}
    for n, g2 in zip(BIG, quarter):
        res = _adamw(f"adamw_{n}", _quarter_2d(n, w[n]), g2, _quarter_2d(n, m[n]), _quarter_2d(n, v[n]))
        out_g[n] = g2.reshape(w[n].shape)
        out_d[n], out_m[n], out_v[n] = (r.reshape(w[n].shape) for r in res)

    whole = [grads[n] for n in SMALL]
    summed = _unpack(_small_all_reduce(_pack([loss_part[0:1, 0:1]] + whole)), [(1, 1)] + [a.shape for a in whole])
    loss = summed[0].reshape(())
    small_g = []
    for n, g in zip(SMALL, summed[1:]):
        if n in SMALL_SHARDED:
            axis = SMALL_SHARDED[n]
            g = g.reshape(w[n].shape[:axis] + (-1,) + w[n].shape[axis + 1:])
            size = w[n].shape[axis]
            g = lax.dynamic_slice_in_dim(g, chip * size, size, axis)
        small_g.append(g.reshape(w[n].shape))
    shapes = [w[n].shape for n in SMALL]
    res = _adamw("adamw_small", _pack([w[n] for n in SMALL]), _pack(small_g), _pack([m[n] for n in SMALL]),
                 _pack([v[n] for n in SMALL]))
    small_d, small_m, small_v = (_unpack(r, shapes) for r in res)
    for i, n in enumerate(SMALL):
        out_g[n], out_d[n], out_m[n], out_v[n] = small_g[i], small_d[i], small_m[i], small_v[i]

    return (loss, dx.reshape(x.shape), *[out_g[n] for n in WEIGHTS], *[out_d[n] for n in WEIGHTS],
            *[out_m[n] for n in WEIGHTS], *[out_v[n] for n in WEIGHTS])
```

```python
import functools
import math

import jax
import jax.numpy as jnp
from jax import lax
from jax.experimental import pallas as pl
from jax.experimental.pallas import tpu as pltpu

F32 = jnp.float32
BF16 = jnp.bfloat16
MESH = pl.DeviceIdType.MESH
HIGHEST = lax.Precision.HIGHEST

D = 1024
EPS = 1e-6
NEG_INF = -1e30
N_CHIPS = 4

GDN_QK_HEADS = 8
GDN_V_HEADS = 16
GDN_HD = 128
GDN_QK = GDN_QK_HEADS * GDN_HD
GDN_V = GDN_V_HEADS * GDN_HD
GDN_CONV = 2 * GDN_QK + GDN_V
GDN_MAIN = GDN_CONV + GDN_V
GDN_IN = GDN_MAIN + 2 * GDN_V_HEADS
GDN_IN_SHARD = GDN_IN // N_CHIPS
GDN_CHUNK = 64

SWA_Q_HEADS = 16
SWA_KV_HEADS = 4
SWA_GROUP = 4
SWA_HD = 64
SWA_BLOCK = 128
REL_BUCKETS = 32
REL_MAX_DISTANCE = 128

DFF = 2816
DFF2 = 2 * DFF
DFF2_SHARD = DFF2 // N_CHIPS
DFF_SHARD = DFF // N_CHIPS

ADAM_LR = 0.001
ADAM_B1 = 0.9
ADAM_B2 = 0.999
ADAM_EPS = 1e-08
ADAM_WD = 0.01
ADAM_STEP = 10

LANE = 128
SUBLANE = 8
VMEM_LIMIT = 56 * 1024 * 1024


def _params(sem, vmem=VMEM_LIMIT):
    return pltpu.CompilerParams(dimension_semantics=sem, vmem_limit_bytes=vmem)


def _rowcall(name, fn, T, tm, ins, outs):
    n = T // tm
    r8 = tm // SUBLANE
    last8 = T // SUBLANE - 1
    arrays, in_specs = [], []
    for arr, kind, cols in ins:
        arrays.append(arr)
        if kind == "full":
            in_specs.append(pl.BlockSpec(arr.shape, functools.partial(lambda nd, i: (0,) * nd, arr.ndim)))
        elif arr.ndim == 2:
            w, ci = cols if cols is not None else (arr.shape[1], 0)
            if kind == "row":
                in_specs.append(pl.BlockSpec((tm, w), functools.partial(lambda ci, i: (i, ci), ci)))
            elif kind == "prev":
                in_specs.append(pl.BlockSpec(
                    (SUBLANE, w), functools.partial(lambda ci, i: (jnp.maximum(i * r8 - 1, 0), ci), ci)))
            else:
                in_specs.append(pl.BlockSpec(
                    (SUBLANE, w), functools.partial(lambda ci, i: (jnp.minimum((i + 1) * r8, last8), ci), ci)))
        else:
            lead = arr.shape[:-2]
            in_specs.append(pl.BlockSpec(lead + (tm, arr.shape[-1]),
                                         functools.partial(lambda nl, i: (0,) * nl + (i, 0), len(lead))))
    out_shape, out_specs = [], []
    for shape, dtype, kind in outs:
        out_shape.append(jax.ShapeDtypeStruct(shape, dtype))
        if kind == "acc":
            out_specs.append(pl.BlockSpec(shape, functools.partial(lambda nd, i: (0,) * nd, len(shape))))
        else:
            lead = shape[:-2]
            out_specs.append(pl.BlockSpec(lead + (tm, shape[-1]),
                                          functools.partial(lambda nl, i: (0,) * nl + (i, 0), len(lead))))
    nin = len(arrays)

    def body(*refs):
        i = pl.program_id(0)
        vals = [r[...] for r in refs[:nin]]
        res = fn(i, *vals)
        for (shape, dtype, kind), o, r in zip(outs, refs[nin:], res):
            if kind == "row":
                o[...] = r.astype(dtype)
            else:
                @pl.when(i == 0)
                def _():
                    o[...] = r.astype(dtype)

                @pl.when(i > 0)
                def _():
                    o[...] += r.astype(dtype)

    res = pl.pallas_call(
        body, name=name, grid=(n,), in_specs=in_specs, out_specs=out_specs, out_shape=out_shape,
        compiler_params=_params(("arbitrary",)),
    )(*arrays)
    return res


def _mm(name, a, b, out_shape, out_dtype, grid, a_spec, b_spec, o_spec, dims, acc_shape, res=None, precision=None,
        into=None, scatter=()):
    nk = grid[2]
    ns = len(scatter)
    a, norm_w = a if isinstance(a, tuple) else (a, None)
    normed = norm_w is not None
    n_in = 2 + normed + (res is not None) + (into is not None) + ns

    def body(*refs):
        a_ref, b_ref, o_ref = refs[0], refs[1], refs[n_in]
        r_ref = refs[2 + normed] if res is not None else None
        if ns:
            comm = (refs[n_in - ns:n_in], refs[n_in + 1:n_in + 1 + ns], refs[-2], refs[-1])
            steps = [pl.program_id(d) for d in range(3)]

            @pl.when((steps[0] == 0) & (steps[1] == 0) & (steps[2] == 0))
            def _():
                for cp in _scatter_copies(*comm):
                    cp.start()

            @pl.when((steps[0] == grid[0] - 1) & (steps[1] == grid[1] - 1) & (steps[2] == grid[2] - 1))
            def _():
                copies = _scatter_copies(*comm)
                for cp in copies:
                    cp.wait_recv()
                for cp in copies:
                    cp.wait_send()

        av, bv = a_ref[...], b_ref[...]
        if normed:
            av = _rms_core(av, refs[2][...])
        if precision is None:
            av, bv = av.astype(BF16), bv.astype(BF16)
        p = lax.dot_general(av, bv, (dims, ((), ())), preferred_element_type=F32, precision=precision)

        def finish(x):
            if res is not None:
                x = x + r_ref[...].astype(F32)
            o_ref[...] = x.astype(out_dtype).reshape(o_ref.shape)

        if nk == 1:
            finish(p)
        else:
            acc = refs[n_in + 1 + ns]
            k = pl.program_id(2)

            @pl.when(k == 0)
            def _():
                acc[...] = p

            @pl.when(k > 0)
            def _():
                acc[...] += p

            @pl.when(k == nk - 1)
            def _():
                finish(acc[...])

    anywhere = pl.BlockSpec(memory_space=pl.ANY)
    ops = [a, b] + ([norm_w] if normed else []) + ([res] if res is not None else [])
    ops += ([into] if into is not None else []) + list(scatter)
    specs = [a_spec, b_spec] + ([pl.BlockSpec(norm_w.shape, lambda i, j, k: (0, 0))] if normed else [])
    specs += [o_spec] if res is not None else []
    specs += ([anywhere] if into is not None else []) + [anywhere] * ns
    out = pl.pallas_call(
        body, name=name, grid=grid, in_specs=specs, out_specs=[o_spec] + [anywhere] * ns,
        out_shape=[jax.ShapeDtypeStruct(out_shape, out_dtype)] + _scatter_shapes(scatter),
        input_output_aliases={n_in - ns - 1: 0} if into is not None else {},
        scratch_shapes=([pltpu.VMEM(acc_shape, F32)] if nk > 1 else [])
        + ([pltpu.SemaphoreType.DMA((3 * ns,)), pltpu.SemaphoreType.DMA((3 * ns,))] if ns else []),
        compiler_params=_params(("arbitrary",) * 3 if ns else ("parallel", "parallel", "arbitrary")),
    )(*ops)
    return out if ns else out[0]


NN = ((1,), (0,))
NT = ((1,), (1,))
TN = ((0,), (0,))


BIG_TILES = (1024, 512, 256, 128)


def _tile(n, pref):
    for t in pref:
        if n % t == 0:
            return t
    return n


def _rows_of(a):
    return a[0] if isinstance(a, tuple) else a


def _mm_nn(name, a, w, out_dtype, res=None, precision=None):
    M, K = _rows_of(a).shape
    N = w.shape[1]
    tm = _tile(M, BIG_TILES if K <= 2048 else BIG_TILES[1:])
    tn = _tile(N, BIG_TILES)
    return _mm(name, a, w, (M, N), out_dtype, (M // tm, N // tn, 1),
               pl.BlockSpec((tm, K), lambda i, j, k: (i, 0)), pl.BlockSpec((K, tn), lambda i, j, k: (0, j)),
               pl.BlockSpec((tm, tn), lambda i, j, k: (i, j)), NN, (tm, tn), res=res, precision=precision)


def _mm_nt(name, g, w, out_dtype, res=None, precision=None, scatter=()):
    M, N = g.shape
    K = w.shape[0]
    tm, tk = _tile(M, BIG_TILES), _tile(K, (1024, 1408, 512, 256, 128))
    tn = _tile(N, (1536,) + BIG_TILES)
    return _mm(name, g, w, (M, K), out_dtype, (M // tm, K // tk, N // tn),
               pl.BlockSpec((tm, tn), lambda i, j, k: (i, k)), pl.BlockSpec((tk, tn), lambda i, j, k: (j, k)),
               pl.BlockSpec((tm, tk), lambda i, j, k: (i, j)), NT, (tm, tk), res=res, precision=precision,
               scatter=scatter)


def _mm_tn(name, a, g, out_dtype=F32, precision=None):
    T, K = _rows_of(a).shape
    N = g.shape[1]
    tk, tn = _tile(K, (1024, 1408, 512, 256, 128)), _tile(N, BIG_TILES)
    assert tk == K or not isinstance(a, tuple)
    tt = _tile(T, BIG_TILES)
    return _mm(name, a, g, (K, N), out_dtype, (K // tk, N // tn, T // tt),
               pl.BlockSpec((tt, tk), lambda i, j, k: (k, i)), pl.BlockSpec((tt, tn), lambda i, j, k: (k, j)),
               pl.BlockSpec((tk, tn), lambda i, j, k: (i, j)), TN, (tk, tn), precision=precision)


def _mm_up(name, n, wup, layer):
    T = _rows_of(n).shape[0]
    tm = _tile(T, BIG_TILES)
    return _mm(name, n, wup, (T, DFF2), BF16, (T // tm, N_CHIPS, 1),
               pl.BlockSpec((tm, D), lambda i, j, k: (i, 0)),
               pl.BlockSpec((None, None, D, DFF2_SHARD), lambda i, j, k: (j, layer, 0, 0)),
               pl.BlockSpec((tm, DFF2_SHARD), lambda i, j, k: (i, j)), NN, (tm, DFF2_SHARD))


def _mm_up_nt(name, du, wup, layer):
    T = du.shape[0]
    tm, tk = _tile(T, BIG_TILES), D
    return _mm(name, du, wup, (T, D), F32, (T // tm, D // tk, N_CHIPS),
               pl.BlockSpec((tm, DFF2_SHARD), lambda i, j, k: (i, k)),
               pl.BlockSpec((None, None, tk, DFF2_SHARD), lambda i, j, k: (k, layer, j, 0)),
               pl.BlockSpec((tm, tk), lambda i, j, k: (i, j)), NT, (tm, tk))


def _mm_up_tn(name, n, du, layer, into):
    T = _rows_of(n).shape[0]
    tk, tt = D, _tile(T, BIG_TILES)
    return _mm(name, n, du, (N_CHIPS, 2, D, DFF2_SHARD), F32, (D // tk, N_CHIPS, T // tt),
               pl.BlockSpec((tt, tk), lambda i, j, k: (k, i)), pl.BlockSpec((tt, DFF2_SHARD), lambda i, j, k: (k, j)),
               pl.BlockSpec((None, None, tk, DFF2_SHARD), lambda i, j, k: (j, layer, i, 0)), TN, (tk, DFF2_SHARD),
               into=into)


def _mm_down_tn(name, act, dout, layer, into):
    T = act.shape[0]
    tk, tn, tt = 2 * DFF_SHARD, _tile(D, BIG_TILES), _tile(T, BIG_TILES)
    return _mm(name, act, dout, (2, 2, 2, DFF_SHARD, D), F32, (DFF // tk, D // tn, T // tt),
               pl.BlockSpec((tt, tk), lambda i, j, k: (k, i)), pl.BlockSpec((tt, tn), lambda i, j, k: (k, j)),
               pl.BlockSpec((None, 2, None, DFF_SHARD, tn), lambda i, j, k: (i, 0, layer, 0, j)), TN, (tk, tn),
               into=into)


def _sigmoid(x):
    return 0.5 * jnp.tanh(0.5 * x) + 0.5


def _silu(x):
    return x * _sigmoid(x)


def _softplus(x):
    return jnp.maximum(x, 0.0) + jnp.log(1.0 + jnp.exp(-jnp.abs(x)))


def _rms_core(h, w):
    return h * lax.rsqrt(jnp.mean(h * h, axis=-1, keepdims=True) + EPS) * w


def _shift_down(x, halo, s, i):
    if s == 0:
        return x
    tm = x.shape[0]
    rolled = pltpu.roll(x, s, 0)
    patch = pltpu.roll(jnp.where(i == 0, 0.0, halo), s, 0)
    row = lax.broadcasted_iota(jnp.int32, patch.shape, 0)
    top = jnp.where(row < s, patch, rolled[:SUBLANE])
    return jnp.concatenate([top, rolled[SUBLANE:]], axis=0) if tm > SUBLANE else top


def _shift_up(x, halo, s, i, n):
    if s == 0:
        return x
    tm = x.shape[0]
    rolled = pltpu.roll(x, tm - s, 0)
    patch = pltpu.roll(jnp.where(i == n - 1, 0.0, halo), SUBLANE - s, 0)
    row = lax.broadcasted_iota(jnp.int32, patch.shape, 0)
    bottom = jnp.where(row >= SUBLANE - s, patch, rolled[tm - SUBLANE:])
    return jnp.concatenate([rolled[:tm - SUBLANE], bottom], axis=0) if tm > SUBLANE else bottom


def _taps(x, halo, K, i):
    return [_shift_down(x, halo, K - 1 - j, i) for j in range(K)]


def _conv_fwd(taps, w):
    y = w[0:1, :] * taps[0]
    for j in range(1, len(taps)):
        y = y + w[j:j + 1, :] * taps[j]
    return y


def _conv_dx(dy, halo_next, w, i, n):
    K = w.shape[0]
    dx = w[K - 1:K, :] * dy
    for j in range(K - 1):
        dx = dx + w[j:j + 1, :] * _shift_up(dy, halo_next, K - 1 - j, i, n)
    return dx


def _conv_dw(dy, taps):
    rows = [jnp.sum(dy * tap, axis=0, keepdims=True) for tap in taps]
    return jnp.concatenate(rows + [jnp.zeros((SUBLANE - len(taps), dy.shape[1]), F32)], axis=0)


def _rms_bwd(name, h, pairs, adds, tm=256):
    T = h.shape[0]
    tm = min(tm, T)
    npair, nadd = len(pairs), len(adds)

    def fn(i, hv, *rest):
        ws, dns, ads = rest[:npair], rest[npair:2 * npair], rest[2 * npair:]
        dh = None
        dws = []
        for wv, dn in zip(ws, dns):
            _, vjp = jax.vjp(_rms_core, hv, wv)
            dhi, dwi = vjp(dn.astype(F32))
            dh = dhi if dh is None else dh + dhi
            dws.append(dwi)
        for a in ads:
            dh = dh + a.astype(F32)
        return (dh, *dws)

    ins = [(h, "row", None)] + [(w, "full", None) for w, _ in pairs] + [(dn, "row", None) for _, dn in pairs]
    ins += [(a, "row", None) for a in adds]
    outs = [((T, D), F32, "row")] + [((1, D), F32, "acc")] * npair
    return _rowcall(name, fn, T, tm, ins, outs)


def _l2(x):
    return x * lax.rsqrt(jnp.sum(x * x, axis=-1, keepdims=True) + EPS)


def _gdn_post_core(yq, yk, yv, pb, pa, a_log, dtb):
    qn = tuple(_l2(_silu(a)) * (GDN_HD ** -0.5) for a in yq)
    kn = tuple(_l2(_silu(a)) for a in yk)
    v = _silu(yv)
    beta = _sigmoid(pb)
    g = -jnp.exp(a_log) * _softplus(pa + dtb)
    return qn, kn, v, beta, g


def _heads(x, n):
    return tuple(x[:, GDN_HD * h:GDN_HD * (h + 1)] for h in range(n))


def _gdn_pre_fwd(pm, pba, conv_w, a_log, dtb, tm=128):
    T = pm.shape[0]
    tm = min(tm, T)

    def fn(i, x, halo, pbav, cw, al, db):
        y = _conv_fwd(_taps(x.astype(F32), halo.astype(F32), 4, i), cw)
        qn, kn, v, beta, g = _gdn_post_core(_heads(y[:, :GDN_QK], 8), _heads(y[:, GDN_QK:2 * GDN_QK], 8),
                                            y[:, 2 * GDN_QK:], pbav[:, :LANE], pbav[:, LANE:], al, db)
        return jnp.stack(qn), jnp.stack(kn), jnp.stack(_heads(v, GDN_V_HEADS)), beta, g

    ins = [(pm, "row", (GDN_CONV, 0)), (pm, "prev", (GDN_CONV, 0)), (pba, "row", None),
           (conv_w, "full", None), (a_log, "full", None), (dtb, "full", None)]
    outs = [((GDN_QK_HEADS, T, GDN_HD), BF16, "row"), ((GDN_QK_HEADS, T, GDN_HD), BF16, "row"),
            ((GDN_V_HEADS, T, GDN_HD), BF16, "row"), ((T, LANE), F32, "row"), ((T, LANE), F32, "row")]
    return _rowcall("gdn_pre_fwd", fn, T, tm, ins, outs)


def _gdn_pre_bwd(pm, pba, conv_w, a_log, dtb, dqn, dkn, dv, dbeta, dg, tm=128):
    T = pm.shape[0]
    tm = min(tm, T)

    def fn(i, x, halo, pbav, cw, al, db, dqv, dkv, dvv, dbv, dgv):
        taps = _taps(x.astype(F32), halo.astype(F32), 4, i)
        y = _conv_fwd(taps, cw)
        prim = (_heads(y[:, :GDN_QK], 8), _heads(y[:, GDN_QK:2 * GDN_QK], 8), y[:, 2 * GDN_QK:],
                pbav[:, :LANE], pbav[:, LANE:], al, db)
        _, vjp = jax.vjp(_gdn_post_core, *prim)
        cot = (tuple(dqv[h].astype(F32) for h in range(8)), tuple(dkv[h].astype(F32) for h in range(8)),
               jnp.concatenate([dvv[h].astype(F32) for h in range(GDN_V_HEADS)], axis=1), dbv, dgv)
        dyq, dyk, dyv, dpb, dpa, dal, ddb = vjp(cot)
        dy = jnp.concatenate(list(dyq) + list(dyk) + [dyv], axis=1)
        dcw = _conv_dw(dy, taps)
        return dy, jnp.concatenate([dpb, dpa], axis=1), dcw, dal, ddb

    ins = [(pm, "row", (GDN_CONV, 0)), (pm, "prev", (GDN_CONV, 0)), (pba, "row", None),
           (conv_w, "full", None), (a_log, "full", None), (dtb, "full", None),
           (dqn, "row", None), (dkn, "row", None), (dv, "row", None), (dbeta, "row", None), (dg, "row", None)]
    outs = [((T, GDN_CONV), BF16, "row"), ((T, 2 * LANE), F32, "row"), ((SUBLANE, GDN_CONV), F32, "acc"),
            ((1, LANE), F32, "acc"), ((1, LANE), F32, "acc")]
    return _rowcall("gdn_pre_bwd", fn, T, tm, ins, outs)


def _gdn_conv_bwd(dy, dz, conv_w, tm=256):
    T = dy.shape[0]
    tm = min(tm, T)
    n = T // tm

    def fn(i, dyv, halo, dzv, cw):
        dx = _conv_dx(dyv.astype(F32), halo.astype(F32), cw, i, n)
        return (jnp.concatenate([dx.astype(BF16), dzv.astype(BF16)], axis=1),)

    ins = [(dy, "row", None), (dy, "next", None), (dz, "row", None), (conv_w, "full", None)]
    return _rowcall("gdn_conv_bwd", fn, T, tm, ins, [((T, GDN_MAIN), BF16, "row")])[0]


def _bdot(a, b, dims=NN):
    return lax.dot_general(a.astype(BF16), b.astype(BF16), (dims, ((), ())), preferred_element_type=F32)


BNN = ((2,), (1,))
BNT = ((2,), (2,))
BTN = ((1,), (1,))


def _bmm(a, b, dims=BNN):
    return lax.dot_general(a.astype(BF16), b.astype(BF16), (dims, ((0,), (0,))), preferred_element_type=F32)


def _bmm3(a, b):
    ah, bh = a.astype(BF16), b.astype(BF16)
    al, bl = (a - ah.astype(F32)).astype(BF16), (b - bh.astype(F32)).astype(BF16)
    dn = (BNN, ((0,), (0,)))
    return (lax.dot_general(ah, bh, dn, preferred_element_type=F32)
            + lax.dot_general(al, bh, dn, preferred_element_type=F32)
            + lax.dot_general(ah, bl, dn, preferred_element_type=F32))


def _tri_inv(m):
    C = m.shape[-1]
    r = lax.broadcasted_iota(jnp.int32, (C, C), 0)
    c = lax.broadcasted_iota(jnp.int32, (C, C), 1)
    t = jnp.where(r == c, 1.0, 0.0) - m
    pw = _bmm3(m, m)
    t = t + _bmm3(t, pw)
    for _ in range(int(math.log2(C)) - 2):
        pw = _bmm(pw, pw)
        t = t + _bmm(t, pw)
    return t


def _tri_inv_vjp(t, dt):
    tt = jnp.swapaxes(t, 1, 2)
    return -_bmm(_bmm(tt, dt), tt)


def _twice(a):
    return jnp.broadcast_to(a[:, None], (a.shape[0], 2) + a.shape[1:]).reshape((2 * a.shape[0],) + a.shape[1:])


def _gdn_gates(grow, brow):
    C = grow.shape[2]
    r = lax.broadcasted_iota(jnp.int32, (C, C), 0)
    c = lax.broadcasted_iota(jnp.int32, (C, C), 1)
    tril, eye = r >= c, r == c
    gcol = jnp.sum(jnp.where(eye, grow, 0.0), axis=2, keepdims=True)
    bcol = jnp.sum(jnp.where(eye, brow, 0.0), axis=2, keepdims=True)
    gc_col = jnp.sum(jnp.where(tril, grow, 0.0), axis=2, keepdims=True)
    gc_row = jnp.sum(jnp.where(r <= c, gcol, 0.0), axis=1, keepdims=True)
    gc_last = jnp.sum(grow, axis=2, keepdims=True)
    decay = jnp.where(tril, jnp.exp(jnp.where(tril, gc_col - gc_row, 0.0)), 0.0)
    return bcol, gc_col, gc_last, decay


def _gdn_m(k, bcol, decay):
    C = k.shape[1]
    strict = lax.broadcasted_iota(jnp.int32, (C, C), 0) > lax.broadcasted_iota(jnp.int32, (C, C), 1)
    return jnp.where(strict, bcol * _twice(_bmm(k, k, BNT)) * decay, 0.0)


def _gdn_rest(q, k, v, bcol, gc_col, gc_last, decay, t_mat, S):
    qk = _twice(_bmm(q, k, BNT))
    k2, q2 = _twice(k), _twice(q)
    egc = jnp.exp(gc_col)
    u = _bmm(t_mat, v * bcol)
    w = _bmm(t_mat, k2 * (bcol * egc))
    v_new = u - _bmm(w, S)
    o = _bmm(q2 * egc, S) + _bmm(qk * decay, v_new)
    s_new = S * jnp.exp(gc_last) + _bmm(k2 * jnp.exp(gc_last - gc_col), v_new, BTN)
    return o, s_new


def _gdn_tb(T):
    return min(256, T)


def _gate_rows(g):
    T = g.shape[0]
    g = g[:, :GDN_V_HEADS].reshape(T // GDN_CHUNK, GDN_CHUNK, GDN_V_HEADS)
    return g.transpose(0, 2, 1)[:, :, None, :]


def _gate_cols(g):
    nc = g.shape[0]
    g = g[:, :, 0, :].transpose(0, 2, 1).reshape(nc * GDN_CHUNK, GDN_V_HEADS)
    return jnp.pad(g, ((0, 0), (0, LANE - GDN_V_HEADS)))


def _gdn_fwd(qn, kn, v, g, beta, gather=None):
    T = qn.shape[1]
    tb = _gdn_tb(T)
    nc = tb // GDN_CHUNK
    nsteps = T // tb
    quarters, buffers = gather if gather is not None else ((), ())
    ng = len(quarters)
    shapes = [a.shape for a in quarters]
    splits = [True] * ng

    def body(*refs):
        q_ref, k_ref, v_ref, g_ref, b_ref = refs[:5]
        src = refs[5:5 + ng]
        o_ref, sall_ref, tall_ref = refs[5 + 2 * ng:8 + 2 * ng]
        dst = refs[8 + 2 * ng:8 + 3 * ng]
        s_scr = refs[8 + 3 * ng]
        step = pl.program_id(0)

        @pl.when(step == 0)
        def _():
            s_scr[...] = jnp.zeros(s_scr.shape, F32)
            if ng:
                for cp in _gather_copies(shapes, splits, src, dst, *refs[9 + 3 * ng:])[0]:
                    cp.start()

        def chunk(ci, carry):
            rows = pl.ds(pl.multiple_of(ci * GDN_CHUNK, GDN_CHUNK), GDN_CHUNK)
            s = s_scr[...]
            sall_ref[ci] = s
            q, k = q_ref[:, rows, :].astype(F32), k_ref[:, rows, :].astype(F32)
            bcol, gc_col, gc_last, decay = _gdn_gates(g_ref[ci], b_ref[ci])
            t_mat = _tri_inv(_gdn_m(k, bcol, decay)).astype(BF16)
            tall_ref[ci] = t_mat
            o, s_new = _gdn_rest(q, k, v_ref[:, rows, :].astype(F32), bcol, gc_col, gc_last, decay,
                                 t_mat.astype(F32), s)
            o_ref[:, rows, :] = o.astype(o_ref.dtype)
            s_scr[...] = s_new
            return carry

        lax.fori_loop(0, nc, chunk, 0)

        if ng:
            @pl.when(step == nsteps - 1)
            def _():
                _gather_arrival(shapes, splits, src, dst, *refs[9 + 3 * ng:])

    qk_spec = pl.BlockSpec((GDN_QK_HEADS, tb, GDN_HD), lambda i: (0, i, 0))
    v_spec = pl.BlockSpec((GDN_V_HEADS, tb, GDN_HD), lambda i: (0, i, 0))
    g_spec = pl.BlockSpec((nc, GDN_V_HEADS, 1, GDN_CHUNK), lambda i: (i, 0, 0, 0))
    anywhere = pl.BlockSpec(memory_space=pl.ANY)
    return pl.pallas_call(
        body, name="gdn_fwd", grid=(nsteps,),
        in_specs=[qk_spec, qk_spec, v_spec, g_spec, g_spec] + [anywhere] * (2 * ng),
        out_specs=[v_spec, pl.BlockSpec((nc, GDN_V_HEADS, GDN_HD, GDN_HD), lambda i: (i, 0, 0, 0)),
                   pl.BlockSpec((nc, GDN_V_HEADS, GDN_CHUNK, GDN_CHUNK), lambda i: (i, 0, 0, 0))] + [anywhere] * ng,
        out_shape=[jax.ShapeDtypeStruct((GDN_V_HEADS, T, GDN_HD), BF16),
                   jax.ShapeDtypeStruct((T // GDN_CHUNK, GDN_V_HEADS, GDN_HD, GDN_HD), F32),
                   jax.ShapeDtypeStruct((T // GDN_CHUNK, GDN_V_HEADS, GDN_CHUNK, GDN_CHUNK), BF16)]
        + [jax.ShapeDtypeStruct(b.shape, b.dtype) for b in buffers],
        input_output_aliases={5 + ng + a: 3 + a for a in range(ng)},
        scratch_shapes=[pltpu.VMEM((GDN_V_HEADS, GDN_HD, GDN_HD), F32)]
        + ([pltpu.SemaphoreType.DMA((6 * ng,)), pltpu.SemaphoreType.DMA((6 * ng,))] if ng else []),
        compiler_params=_params(("arbitrary",)),
    )(qn, kn, v, g, beta, *quarters, *buffers)


def _gdn_bwd(qn, kn, v, g, beta, sall, tall, do, scatter=()):
    T = qn.shape[1]
    tb = _gdn_tb(T)
    nc = tb // GDN_CHUNK
    nb = T // tb
    ns = len(scatter)

    def body(*refs):
        q_ref, k_ref, v_ref, g_ref, b_ref, sall_ref, tall_ref, do_ref = refs[:8]
        dq_ref, dk_ref, dv_ref, dg_ref, db_ref = refs[8 + ns:13 + ns]
        ds_scr = refs[13 + 2 * ns]
        comm = (refs[8:8 + ns], refs[13 + ns:13 + 2 * ns], *refs[14 + 2 * ns:])
        step = pl.program_id(0)

        @pl.when(step == 0)
        def _():
            ds_scr[...] = jnp.zeros(ds_scr.shape, F32)
            if ns:
                for cp in _scatter_copies(*comm):
                    cp.start()

        def chunk(cr, carry):
            ci = nc - 1 - cr
            rows = pl.ds(pl.multiple_of(ci * GDN_CHUNK, GDN_CHUNK), GDN_CHUNK)
            k, t_mat = k_ref[:, rows, :].astype(F32), tall_ref[ci].astype(F32)
            (bcol, gc_col, gc_last, decay), vjp_gates = jax.vjp(_gdn_gates, g_ref[ci], b_ref[ci])
            _, vjp = jax.vjp(_gdn_rest, q_ref[:, rows, :].astype(F32), k, v_ref[:, rows, :].astype(F32),
                             bcol, gc_col, gc_last, decay, t_mat, sall_ref[ci])
            dq, dk, dv, dbcol, dgc_col, dgc_last, ddecay, dt, ds = vjp((do_ref[:, rows, :].astype(F32), ds_scr[...]))
            _, vjp_m = jax.vjp(_gdn_m, k, bcol, decay)
            dk_m, dbcol_m, ddecay_m = vjp_m(_tri_inv_vjp(t_mat, dt))
            dg, db = vjp_gates((dbcol + dbcol_m, dgc_col, dgc_last, ddecay + ddecay_m))
            ds_scr[...] = ds
            dq_ref[:, rows, :] = dq
            dk_ref[:, rows, :] = dk + dk_m
            dv_ref[:, rows, :] = dv
            dg_ref[ci] = dg
            db_ref[ci] = db
            return carry

        lax.fori_loop(0, nc, chunk, 0)

        if ns:
            @pl.when(step == nb - 1)
            def _():
                copies = _scatter_copies(*comm)
                for cp in copies:
                    cp.wait_recv()
                for cp in copies:
                    cp.wait_send()

    qk_spec = pl.BlockSpec((GDN_QK_HEADS, tb, GDN_HD), lambda i: (0, nb - 1 - i, 0))
    v_spec = pl.BlockSpec((GDN_V_HEADS, tb, GDN_HD), lambda i: (0, nb - 1 - i, 0))
    g_spec = pl.BlockSpec((nc, GDN_V_HEADS, 1, GDN_CHUNK), lambda i: (nb - 1 - i, 0, 0, 0))
    s_spec = pl.BlockSpec((nc, GDN_V_HEADS, GDN_HD, GDN_HD), lambda i: (nb - 1 - i, 0, 0, 0))
    t_spec = pl.BlockSpec((nc, GDN_V_HEADS, GDN_CHUNK, GDN_CHUNK), lambda i: (nb - 1 - i, 0, 0, 0))
    anywhere = pl.BlockSpec(memory_space=pl.ANY)
    return pl.pallas_call(
        body, name="gdn_bwd", grid=(nb,),
        in_specs=[qk_spec, qk_spec, v_spec, g_spec, g_spec, s_spec, t_spec, v_spec] + [anywhere] * ns,
        out_specs=[qk_spec, qk_spec, v_spec, g_spec, g_spec] + [anywhere] * ns,
        out_shape=[jax.ShapeDtypeStruct((GDN_QK_HEADS, T, GDN_HD), F32),
                   jax.ShapeDtypeStruct((GDN_QK_HEADS, T, GDN_HD), F32),
                   jax.ShapeDtypeStruct((GDN_V_HEADS, T, GDN_HD), F32),
                   jax.ShapeDtypeStruct(g.shape, F32), jax.ShapeDtypeStruct(g.shape, F32)]
        + _scatter_shapes(scatter),
        scratch_shapes=[pltpu.VMEM((GDN_V_HEADS, GDN_HD, GDN_HD), F32)]
        + ([pltpu.SemaphoreType.DMA((3 * ns,)), pltpu.SemaphoreType.DMA((3 * ns,))] if ns else []),
        compiler_params=_params(("arbitrary",)),
    )(qn, kn, v, g, beta, sall, tall, do, *scatter)


def _gnorm_core(o, z, w):
    return tuple(_rms_core(oh, w) * _silu(zh) for oh, zh in zip(o, z))


def _gnorm_fwd(o, pm, w, tm=256):
    T = pm.shape[0]
    tm = min(tm, T)

    def fn(i, ov, zv, wv):
        zf = zv.astype(F32)
        out = _gnorm_core(tuple(ov[h].astype(F32) for h in range(GDN_V_HEADS)), _heads(zf, GDN_V_HEADS), wv)
        return (jnp.concatenate(out, axis=1),)

    ins = [(o, "row", None), (pm, "row", (GDN_V, 2)), (w, "full", None)]
    return _rowcall("gnorm_fwd", fn, T, tm, ins, [((T, GDN_V), BF16, "row")])[0]


def _gnorm_bwd(o, pm, w, don, tm=128):
    T = pm.shape[0]
    tm = min(tm, T)

    def fn(i, ov, zv, wv, dv):
        zf, df = zv.astype(F32), dv.astype(F32)
        _, vjp = jax.vjp(_gnorm_core, tuple(ov[h].astype(F32) for h in range(GDN_V_HEADS)),
                         _heads(zf, GDN_V_HEADS), wv)
        do, dz, dw = vjp(_heads(df, GDN_V_HEADS))
        return jnp.stack(do), jnp.concatenate(dz, axis=1), dw

    ins = [(o, "row", None), (pm, "row", (GDN_V, 2)), (w, "full", None), (don, "row", None)]
    outs = [((GDN_V_HEADS, T, GDN_HD), BF16, "row"), ((T, GDN_V), BF16, "row"), ((1, GDN_HD), F32, "acc")]
    return _rowcall("gnorm_bwd", fn, T, tm, ins, outs)


def _ffn_act_fwd(name, up, conv_w, conv_b, tm=128):
    T = up.shape[0]
    tm = min(tm, T)

    def fn(i, x, halo, cw, cb):
        u = _conv_fwd(_taps(x.astype(F32), halo.astype(F32), 3, i), cw) + cb
        return (_silu(u[:, :DFF]) * u[:, DFF:],)

    ins = [(up, "row", None), (up, "prev", None), (conv_w, "full", None), (conv_b, "full", None)]
    return _rowcall(name, fn, T, tm, ins, [((T, DFF), BF16, "row")])[0]


def _ffn_act_bwd(name, up, conv_w, conv_b, dact, tm=128):
    T = up.shape[0]
    tm = min(tm, T)

    def fn(i, x, halo, cw, cb, da):
        taps = _taps(x.astype(F32), halo.astype(F32), 3, i)
        da = da.astype(F32)
        u = _conv_fwd(taps, cw) + cb
        gate, val = u[:, :DFF], u[:, DFF:]
        sg = _sigmoid(gate)
        dgate = da * val * sg * (1.0 + gate * (1.0 - sg))
        dval = da * gate * sg
        du = jnp.concatenate([dgate, dval], axis=1)
        return du, _conv_dw(du, taps), jnp.sum(du, axis=0, keepdims=True)

    ins = [(up, "row", None), (up, "prev", None), (conv_w, "full", None), (conv_b, "full", None),
           (dact, "row", None)]
    outs = [((T, DFF2), BF16, "row"), ((SUBLANE, DFF2), F32, "acc"), ((1, DFF2), F32, "acc")]
    return _rowcall(name, fn, T, tm, ins, outs)


def _ffn_conv_bwd(name, du, conv_w, tm=256):
    T = du.shape[0]
    tm = min(tm, T)
    n = T // tm

    def fn(i, dv, halo, cw):
        return (_conv_dx(dv.astype(F32), halo.astype(F32), cw, i, n),)

    ins = [(du, "row", None), (du, "next", None), (conv_w, "full", None)]
    return _rowcall(name, fn, T, tm, ins, [((T, DFF2), BF16, "row")])[0]


GROUP_ROWS = SWA_GROUP * SWA_BLOCK


def _attn_core(q, kp, kc, vp, vc, bias, sink, mask):
    kcat = jnp.concatenate([kp, kc], axis=0)
    vcat = jnp.concatenate([vp, vc], axis=0)
    s = _bdot(q * (SWA_HD ** -0.5), kcat, NT) + bias
    s = jnp.where(mask, s, NEG_INF)
    m = lax.stop_gradient(jnp.maximum(jnp.max(s, axis=-1, keepdims=True), sink))
    p = jnp.exp(s - m)
    denom = jnp.sum(p, axis=-1, keepdims=True) + jnp.exp(sink - m)
    return _bdot(p / denom, vcat)


def _attn_mask(i):
    qi = lax.broadcasted_iota(jnp.int32, (GROUP_ROWS, 2 * SWA_BLOCK), 0) & (SWA_BLOCK - 1)
    ki = lax.broadcasted_iota(jnp.int32, (GROUP_ROWS, 2 * SWA_BLOCK), 1)
    dist = qi + SWA_BLOCK - ki
    return (dist >= 0) & (dist < SWA_BLOCK) & ((ki >= SWA_BLOCK) | (i > 0))


def _head_cols(h):
    return slice(h * SWA_HD, (h + 1) * SWA_HD)


def _stacked_heads(ref, j):
    return jnp.concatenate([ref[:, _head_cols(SWA_GROUP * j + g)].astype(F32) for g in range(SWA_GROUP)], axis=0)


def _flat_operands(j, q_ref, kvc_ref, kvp_ref, b_ref, s_ref):
    heads = slice(SWA_GROUP * j, SWA_GROUP * (j + 1))
    sink = jnp.concatenate([jnp.broadcast_to(s_ref[j, g:g + 1, 0:1], (SWA_BLOCK, 1)) for g in range(SWA_GROUP)],
                           axis=0)
    k_cols, v_cols = _head_cols(j), _head_cols(SWA_KV_HEADS + j)
    return (_stacked_heads(q_ref, j), kvp_ref[:, k_cols].astype(F32), kvc_ref[:, k_cols].astype(F32),
            kvp_ref[:, v_cols].astype(F32), kvc_ref[:, v_cols].astype(F32),
            b_ref[heads].reshape(GROUP_ROWS, 2 * SWA_BLOCK), sink)


def _store_heads(ref, j, stacked):
    for g in range(SWA_GROUP):
        ref[:, _head_cols(SWA_GROUP * j + g)] = stacked[g * SWA_BLOCK:(g + 1) * SWA_BLOCK].astype(ref.dtype)


def _attn_fwd_flat(q, kv, bias, sinks):
    T = q.shape[0]
    nb = T // SWA_BLOCK

    def body(q_ref, kvc_ref, kvp_ref, b_ref, s_ref, o_ref):
        mask = _attn_mask(pl.program_id(0))
        operands = [_flat_operands(j, q_ref, kvc_ref, kvp_ref, b_ref, s_ref) for j in range(SWA_KV_HEADS)]
        outs = [_attn_core(*ops, mask) for ops in operands]
        for j in range(SWA_KV_HEADS):
            _store_heads(o_ref, j, outs[j])

    q_spec = pl.BlockSpec((SWA_BLOCK, q.shape[1]), lambda i: (i, 0))
    cur = pl.BlockSpec((SWA_BLOCK, kv.shape[1]), lambda i: (i, 0))
    prev = pl.BlockSpec((SWA_BLOCK, kv.shape[1]), lambda i: (jnp.maximum(i - 1, 0), 0))
    return pl.pallas_call(
        body, name="attn_fwd", grid=(nb,),
        in_specs=[q_spec, cur, prev, pl.BlockSpec(bias.shape, lambda i: (0, 0, 0)),
                  pl.BlockSpec(sinks.shape, lambda i: (0, 0, 0))],
        out_specs=q_spec, out_shape=jax.ShapeDtypeStruct(q.shape, BF16),
        compiler_params=_params(("arbitrary",)),
    )(q, kv, kv, bias, sinks)


def _attn_bwd_flat(q, kv, bias, sinks, do):
    T = q.shape[0]
    nb = T // SWA_BLOCK

    def body(q_ref, kvc_ref, kvp_ref, b_ref, s_ref, do_ref, dq_ref, dkv_ref, db_ref, dsk_ref, carry):
        i = pl.program_id(0)

        @pl.when(i < nb)
        def _():
            mask = _attn_mask(i)
            operands = [_flat_operands(j, q_ref, kvc_ref, kvp_ref, b_ref, s_ref) for j in range(SWA_KV_HEADS)]
            cots = [_stacked_heads(do_ref, j) for j in range(SWA_KV_HEADS)]
            grads = [jax.vjp(functools.partial(_attn_core, mask=mask), *ops)[1](cot)
                     for ops, cot in zip(operands, cots)]
            for j, (dq, dkp, dkc, dvp, dvc, db, dsc) in enumerate(grads):
                heads = slice(SWA_GROUP * j, SWA_GROUP * (j + 1))
                k_cols, v_cols = _head_cols(j), _head_cols(SWA_KV_HEADS + j)
                _store_heads(dq_ref, j, dq)
                db = db.reshape(SWA_GROUP, SWA_BLOCK, 2 * SWA_BLOCK)
                dsk = jnp.concatenate(
                    [jnp.broadcast_to(jnp.sum(dsc[g * SWA_BLOCK:(g + 1) * SWA_BLOCK], axis=0, keepdims=True),
                                      (1, LANE)) for g in range(SWA_GROUP)], axis=0)

                @pl.when(i == 0)
                def _():
                    db_ref[heads] = db
                    dsk_ref[j] = dsk

                @pl.when(i > 0)
                def _():
                    db_ref[heads] += db
                    dsk_ref[j] += dsk
                    dkv_ref[:, k_cols] = (carry[:, k_cols] + dkp).astype(dkv_ref.dtype)
                    dkv_ref[:, v_cols] = (carry[:, v_cols] + dvp).astype(dkv_ref.dtype)

                carry[:, k_cols] = dkc
                carry[:, v_cols] = dvc

        @pl.when(i == nb)
        def _():
            dkv_ref[...] = carry[...].astype(dkv_ref.dtype)

    last = nb - 1
    q_spec = pl.BlockSpec((SWA_BLOCK, q.shape[1]), lambda i: (jnp.minimum(i, last), 0))
    cur = pl.BlockSpec((SWA_BLOCK, kv.shape[1]), lambda i: (jnp.minimum(i, last), 0))
    prev = pl.BlockSpec((SWA_BLOCK, kv.shape[1]), lambda i: (jnp.clip(i - 1, 0, last), 0))
    b_spec = pl.BlockSpec(bias.shape, lambda i: (0, 0, 0))
    s_spec = pl.BlockSpec(sinks.shape, lambda i: (0, 0, 0))
    return pl.pallas_call(
        body, name="attn_bwd", grid=(nb + 1,),
        in_specs=[q_spec, cur, prev, b_spec, s_spec, q_spec],
        out_specs=[q_spec, prev, b_spec, s_spec],
        out_shape=[jax.ShapeDtypeStruct(q.shape, BF16), jax.ShapeDtypeStruct(kv.shape, BF16),
                   jax.ShapeDtypeStruct(bias.shape, F32), jax.ShapeDtypeStruct(sinks.shape, F32)],
        scratch_shapes=[pltpu.VMEM((SWA_BLOCK, kv.shape[1]), F32)],
        compiler_params=_params(("arbitrary",)),
    )(q, kv, kv, bias, sinks, do)


def _rel_onehot():
    qi = jnp.arange(SWA_BLOCK)[:, None]
    ki = jnp.arange(2 * SWA_BLOCK)[None, :]
    n = jnp.maximum(qi + SWA_BLOCK - ki, 0)
    max_exact = REL_BUCKETS // 2
    nf = jnp.maximum(n, 1).astype(F32)
    large = max_exact + (jnp.log(nf / max_exact) / math.log(REL_MAX_DISTANCE / max_exact)
                         * (REL_BUCKETS - max_exact)).astype(jnp.int32)
    bucket = jnp.where(n < max_exact, n, jnp.minimum(large, REL_BUCKETS - 1)).reshape(-1)
    return (bucket[None, :] == jnp.arange(REL_BUCKETS)[:, None]).astype(F32)


def _final(h, w, target, tm=256):
    T = h.shape[0]
    tm = min(tm, T)

    def fn(i, hv, wv, tv):
        y, vjp = jax.vjp(_rms_core, hv, wv)
        err = y - tv
        dh, dw = vjp(err * (1.0 / D))
        part = 0.5 * jnp.sum(jnp.sum(err * err, axis=1, keepdims=True) * (1.0 / D), axis=0, keepdims=True)
        return jnp.broadcast_to(part, (SUBLANE, LANE)), dh, dw

    ins = [(h, "row", None), (w, "full", None), (target, "row", None)]
    outs = [((SUBLANE, LANE), F32, "acc"), ((T, D), F32, "row"), ((1, D), F32, "acc")]
    return _rowcall("final", fn, T, tm, ins, outs)


def _ffn_fwd(tag, h, P, layer):
    n = (h, P["ffn_norm_w"][layer:layer + 1])
    up = _mm_up(f"{tag}_up", n, P["w_up"], layer)
    act = _ffn_act_fwd(f"{tag}_act", up, P["ffn_conv_w"][layer], P["ffn_conv_b"][layer:layer + 1])
    out = _mm_nn(f"{tag}_down", act, P["w_down"][layer], F32, res=h)
    return out, (n, up, act)


def _ffn_bwd(tag, h, saved, dout, P, layer, into=(None, None)):
    n, up, act = saved
    cw, cb = P["ffn_conv_w"][layer], P["ffn_conv_b"][layer:layer + 1]
    dact = _mm_nt(f"{tag}_down_dx", dout, P["w_down"][layer], BF16)
    g_down = _mm_down_tn(f"{tag}_down_dw", act, dout, layer, into[1])
    du, dcw, dcb = _ffn_act_bwd(f"{tag}_act_bwd", up, cw, cb, dact)
    dup = _ffn_conv_bwd(f"{tag}_conv_bwd", du, cw)
    g_up = _mm_up_tn(f"{tag}_up_dw", n, dup, layer, into[0])
    dn = _mm_up_nt(f"{tag}_up_dx", dup, P["w_up"], layer)
    dh, dnw = _rms_bwd(f"{tag}_rms_bwd", h, [(P["ffn_norm_w"][layer:layer + 1], dn)], [dout])
    return dh, dict(w_down=g_down, w_up=g_up, conv_w=dcw[:3], conv_b=dcb, norm_w=dnw)


def _local_step(x, target, P, late=None, pair_sums=None):
    T = x.shape[0]
    n0 = (x, P["a_norm_w"])
    pm = _mm_nn("gdn_in", n0, P["w_in_main"], BF16)
    pba = _mm_nn("gdn_in_ba", n0, P["w_in_ba"], F32)
    qn, kn, v, beta, g = _gdn_pre_fwd(pm, pba, P["a_conv_w"], P["a_log"], P["dt_bias"])
    g_rows, beta_rows = _gate_rows(g), _gate_rows(beta)
    o, sall, tall, *gathered = _gdn_fwd(qn, kn, v, g_rows, beta_rows, gather=late)
    if late is not None:
        P = {**P, **_late_weights(gathered)}
    on = _gnorm_fwd(o, pm, P["a_out_norm_w"])
    h1 = _mm_nn("gdn_out", on, P["w_out"], F32, res=x)
    h2, ffn0 = _ffn_fwd("ffn0", h1, P, 0)
    nkv = (h2, P["kv_norm_w"])
    kv = _mm_nn("kv_proj", nkv, P["w_kv"], BF16)
    nb = (h2, P["b_norm_w"])
    qp = _mm_nn("q_proj", nb, P["w_q"], BF16)
    onehot = _rel_onehot()
    bias = _mm_nn("rel_bias", P["rel_table_t"], onehot, F32, precision=HIGHEST)
    bias = bias.reshape(SWA_Q_HEADS, SWA_BLOCK, 2 * SWA_BLOCK)
    oa = _attn_fwd_flat(qp, kv, bias, P["sinks"])
    h3 = _mm_nn("o_proj", oa, P["w_o"], F32, res=h2)
    h4, ffn1 = _ffn_fwd("ffn1", h3, P, 1)
    loss, dh4, d_final = _final(h4, P["final_norm_w"], target)

    dh3, gf1 = _ffn_bwd("ffn1", h3, ffn1, dh4, P, 1)
    doa = _mm_nt("o_proj_dx", dh3, P["w_o"], BF16)
    g_wo = _mm_tn("o_proj_dw", oa, dh3)
    dqp, dkv, dbias, dsinks = _attn_bwd_flat(qp, kv, bias, P["sinks"], doa)
    g_wq = _mm_tn("q_proj_dw", nb, dqp)
    dnb = _mm_nt("q_proj_dx", dqp, P["w_q"], F32)
    g_wkv = _mm_tn("kv_proj_dw", nkv, dkv)
    dnkv = _mm_nt("kv_proj_dx", dkv, P["w_kv"], F32)
    dh2, d_bnorm, d_kvnorm = _rms_bwd("b_kv_rms_bwd", h2, [(P["b_norm_w"], dnb), (P["kv_norm_w"], dnkv)], [dh3])
    g_table = _mm_nt("rel_bias_dw", onehot, dbias.reshape(SWA_Q_HEADS, -1), F32, precision=HIGHEST)
    dh1, gf0 = _ffn_bwd("ffn0", h1, ffn0, dh2, P, 0, into=(gf1["w_up"], gf1["w_down"]))
    don = _mm_nt("gdn_out_dx", dh1, P["w_out"], BF16)
    g_wout = _mm_tn("gdn_out_dw", on, dh1)
    do, dz, d_gnorm = _gnorm_bwd(o, pm, P["a_out_norm_w"], don)
    ready = dict(a_w_out=g_wout, w_kv=g_wkv, b_w_q=g_wq, b_w_o=g_wo, ffn_w_up=gf0["w_up"], ffn_w_down=gf0["w_down"])
    pairs = pair_sums(ready, "early") if pair_sums is not None else []
    dq, dk, dv, dg, dbeta, *parts = _gdn_bwd(qn, kn, v, g_rows, beta_rows, sall, tall, do, scatter=pairs)
    dy, dpba, d_aconv, d_alog, d_dtb = _gdn_pre_bwd(pm, pba, P["a_conv_w"], P["a_log"], P["dt_bias"],
                                                    dq, dk, dv, _gate_cols(dbeta), _gate_cols(dg))
    dpm = _gdn_conv_bwd(dy, dz, P["a_conv_w"])
    g_win_main = _mm_tn("gdn_in_dw", n0, dpm)
    g_win_ba = _mm_tn("gdn_in_ba_dw", n0, dpba)
    nh = GDN_V_HEADS
    g_win = jnp.concatenate([g_win_main, g_win_ba[:, :nh], g_win_ba[:, LANE:LANE + nh]], axis=1)
    last_pair = pair_sums({"a_w_in": g_win}, "late") if pair_sums is not None else []
    dn0 = _mm_nt("gdn_in_dx", dpm, P["w_in_main"], F32, scatter=last_pair)
    dn0, last_parts = (dn0[0], dn0[1:]) if last_pair else (dn0, [])
    dn0 = _mm_nt("gdn_in_ba_dx", dpba, P["w_in_ba"], F32, res=dn0)
    dx, d_anorm = _rms_bwd("a_rms_bwd", x, [(P["a_norm_w"], dn0)], [dh1])

    nh = GDN_V_HEADS
    grads = dict(
        a_norm_w=d_anorm,
        a_w_in=g_win,
        a_conv_w=d_aconv[:4], a_a_log=d_alog[:, :nh], a_dt_bias=d_dtb[:, :nh], a_out_norm_w=d_gnorm,
        a_w_out=g_wout, kv_norm_w=d_kvnorm, w_kv=g_wkv, b_norm_w=d_bnorm, b_w_q=g_wq,
        b_sinks=dsinks[:, :, 0].reshape(1, SWA_Q_HEADS), b_w_o=g_wo, rel_bias_table=g_table,
        ffn_norm_w=jnp.concatenate([gf0["norm_w"], gf1["norm_w"]], axis=0),
        ffn_w_up=gf0["w_up"],
        ffn_conv_w=jnp.stack([gf0["conv_w"], gf1["conv_w"]], axis=0),
        ffn_conv_b=jnp.concatenate([gf0["conv_b"], gf1["conv_b"]], axis=0),
        ffn_w_down=gf0["w_down"],
        final_norm_w=d_final,
    )
    scattered = dict(zip([n for n in BIG if n in ready], zip(pairs, parts)))
    scattered.update(zip(["a_w_in"], zip(last_pair, last_parts)))
    return loss, dx, grads, scattered


HBM_SPEC = pl.BlockSpec(memory_space=pltpu.HBM)
VMEM_SPEC = pl.BlockSpec(memory_space=pltpu.VMEM)


def _coords():
    return lax.axis_index("x"), lax.axis_index("y"), lax.axis_index("c")


def _remote(src, dst, send_sem, recv_sem, device):
    return pltpu.make_async_remote_copy(src_ref=src, dst_ref=dst, send_sem=send_sem, recv_sem=recv_sem,
                                        device_id=device, device_id_type=MESH)


def _other_chips(x, y):
    return [(1 - x, y), (x, 1 - y), (1 - x, 1 - y)]


def _gather_copies(shapes, split, ins, outs, send_sems, recv_sems):
    x, y, c = _coords()
    p = 2 * x + y
    ici, forwards, from_sibling = [], [], []
    for a, shape in enumerate(shapes):
        h = shape[0] // 2
        for j, chip in enumerate(_other_chips(x, y)):
            q = 2 * chip[0] + chip[1]
            if split[a]:
                mine, theirs = pl.ds(c * h, h), pl.ds((1 - c) * h, h)
                ici.append(_remote(ins[a].at[mine], outs[a].at[p, mine], send_sems.at[6 * a + j],
                                   recv_sems.at[6 * a + j], (*chip, c)))
                land = outs[a].at[q, mine]
                forwards.append(_remote(land, land, send_sems.at[6 * a + 3 + j], recv_sems.at[6 * a + 3 + j],
                                        (x, y, 1 - c)))
                land = outs[a].at[q, theirs]
                from_sibling.append(_remote(land, land, send_sems.at[6 * a + 3 + j], recv_sems.at[6 * a + 3 + j],
                                            (x, y, 1 - c)))
            else:
                ici.append(_remote(ins[a], outs[a].at[p], send_sems.at[6 * a + j], recv_sems.at[6 * a + j],
                                   (*chip, c)))
                forwards.append(None)
    return ici, forwards, from_sibling


def _gather_arrival(shapes, split, ins, outs, send_sems, recv_sems):
    x, y, c = _coords()
    ici, forwards, from_sibling = _gather_copies(shapes, split, ins, outs, send_sems, recv_sems)
    k = 0
    for a, shape in enumerate(shapes):
        h = shape[0] // 2
        for j, chip in enumerate(_other_chips(x, y)):
            q = 2 * chip[0] + chip[1]
            land = outs[a].at[q, pl.ds(c * h, h)] if split[a] else outs[a].at[q]
            _remote(land, land, send_sems.at[6 * a + j], recv_sems.at[6 * a + j], (*chip, c)).wait_recv()
            if forwards[k] is not None:
                forwards[k].start()
            k += 1
    for cp in from_sibling:
        cp.wait_recv()
    for cp in ici + [f for f in forwards if f is not None]:
        cp.wait_send()


def _all_gather(arrs, split, remote):
    n = len(arrs)
    now = [a for a in range(n) if remote[a]]
    shapes = [arrs[a].shape for a in now]
    splits = [split[a] for a in now]

    def body(*refs):
        ins, outs, stage = refs[:n], refs[n:2 * n], refs[2 * n:3 * n]
        send_sems, recv_sems, in_sems, out_sems = refs[3 * n:]
        p = 2 * lax.axis_index("x") + lax.axis_index("y")
        gathered = ([ins[a] for a in now], [outs[a] for a in now], send_sems, recv_sems)
        loads = [pltpu.make_async_copy(ins[a], stage[a], in_sems.at[a]) for a in range(n)]
        for cp in loads:
            cp.start()
        for cp in _gather_copies(shapes, splits, *gathered)[0]:
            cp.start()
        stores = [pltpu.make_async_copy(stage[a], outs[a].at[p], out_sems.at[a]) for a in range(n)]
        for a in range(n):
            loads[a].wait()
            stores[a].start()
        _gather_arrival(shapes, splits, *gathered)
        for cp in stores:
            cp.wait()

    return pl.pallas_call(
        body, name="weights_all_gather", in_specs=[HBM_SPEC] * n, out_specs=[HBM_SPEC] * n,
        out_shape=[jax.ShapeDtypeStruct((N_CHIPS,) + a.shape, a.dtype) for a in arrs],
        scratch_shapes=[pltpu.VMEM(a.shape, a.dtype) for a in arrs]
        + [pltpu.SemaphoreType.DMA((6 * len(now),)), pltpu.SemaphoreType.DMA((6 * len(now),)),
           pltpu.SemaphoreType.DMA((n,)), pltpu.SemaphoreType.DMA((n,))],
        compiler_params=pltpu.CompilerParams(vmem_limit_bytes=VMEM_LIMIT),
    )(*arrs)


PAIR_SWAP_PIECES = 2


def _pair_swap(gs, tag):
    n = len(gs)

    def body(*refs):
        ins, other = refs[:n], refs[n:2 * n]
        send_sems, recv_sems = refs[2 * n:]
        x, y, c = _coords()
        cps = []
        for a in range(n):
            h = gs[a].shape[1] // 2
            piece = h // PAIR_SWAP_PIECES
            for q in range(N_CHIPS):
                for r in range(PAIR_SWAP_PIECES):
                    k = (a * N_CHIPS + q) * PAIR_SWAP_PIECES + r
                    cp = _remote(ins[a].at[q, pl.ds((1 - c) * h + r * piece, piece)],
                                 other[a].at[q, pl.ds(r * piece, piece)], send_sems.at[k], recv_sems.at[k],
                                 (x, y, 1 - c))
                    cp.start()
                    cps.append(cp)
        for cp in cps:
            cp.wait()

    half = [jax.ShapeDtypeStruct((N_CHIPS, g.shape[1] // 2, g.shape[2]), g.dtype) for g in gs]
    nsem = n * N_CHIPS * PAIR_SWAP_PIECES
    return pl.pallas_call(
        body, name=f"grads_pair_swap_{tag}", in_specs=[HBM_SPEC] * n, out_specs=[HBM_SPEC] * n, out_shape=half,
        scratch_shapes=[pltpu.SemaphoreType.DMA((nsem,)), pltpu.SemaphoreType.DMA((nsem,))],
    )(*gs)


def _scatter_copies(ins, outs, send_sems, recv_sems):
    x, y, c = _coords()
    copies = []
    for a in range(len(ins)):
        for j, chip in enumerate(_other_chips(x, y)):
            q = 2 * chip[0] + chip[1]
            copies.append(_remote(ins[a].at[q], outs[a].at[j], send_sems.at[3 * a + j], recv_sems.at[3 * a + j],
                                  (*chip, c)))
    return copies


def _scatter_shapes(ps):
    return [jax.ShapeDtypeStruct((N_CHIPS - 1,) + a.shape[1:], a.dtype) for a in ps]


def _pair_share(rs):
    n = len(rs)

    def body(*refs):
        ins, outs, stage = refs[:n], refs[n:2 * n], refs[2 * n:3 * n]
        send_sems, recv_sems, in_sems, out_sems = refs[3 * n:]
        x, y, c = _coords()

        def mine(a):
            h = rs[a].shape[0]
            return outs[a].at[pl.ds(c * h, h)]

        loads = [pltpu.make_async_copy(ins[a], stage[a], in_sems.at[a]) for a in range(n)]
        for cp in loads:
            cp.start()
        sends = [_remote(ins[a], mine(a), send_sems.at[a], recv_sems.at[a], (x, y, 1 - c)) for a in range(n)]
        for cp in sends:
            cp.start()
        stores = [pltpu.make_async_copy(stage[a], mine(a), out_sems.at[a]) for a in range(n)]
        for a in range(n):
            loads[a].wait()
            stores[a].start()
        for a in range(n):
            h = rs[a].shape[0]
            land = outs[a].at[pl.ds((1 - c) * h, h)]
            _remote(land, land, send_sems.at[a], recv_sems.at[a], (x, y, 1 - c)).wait_recv()
        for cp in sends:
            cp.wait_send()
        for cp in stores:
            cp.wait()

    return pl.pallas_call(
        body, name="grads_pair_share", in_specs=[HBM_SPEC] * n, out_specs=[HBM_SPEC] * n,
        out_shape=[jax.ShapeDtypeStruct((2 * a.shape[0], a.shape[1]), a.dtype) for a in rs],
        scratch_shapes=[pltpu.VMEM(a.shape, a.dtype) for a in rs] + [pltpu.SemaphoreType.DMA((n,))] * 4,
        compiler_params=pltpu.CompilerParams(vmem_limit_bytes=VMEM_LIMIT),
    )(*rs)


def _small_all_reduce(buf):
    R = buf.shape[0]
    ndev = 2 * N_CHIPS

    def body(in_ref, out_ref, gath, send_sems, recv_sems):
        x, y, c = _coords()
        me = 4 * x + 2 * y + c
        gath[me] = in_ref[...]
        peers = []
        for d in range(1, ndev):
            px = 1 - x if d & 4 else x
            py = 1 - y if d & 2 else y
            pc = 1 - c if d & 1 else c
            peers.append((px, py, pc))
        sends = []
        for d, peer in enumerate(peers):
            cp = _remote(in_ref, gath.at[me], send_sems.at[d], recv_sems.at[d], peer)
            cp.start()
            sends.append(cp)
        for d, peer in enumerate(peers):
            land = gath.at[4 * peer[0] + 2 * peer[1] + peer[2]]
            _remote(land, land, send_sems.at[d], recv_sems.at[d], peer).wait_recv()
        for cp in sends:
            cp.wait_send()
        acc = gath[0]
        for s in range(1, ndev):
            acc = acc + gath[s]
        out_ref[...] = acc

    return pl.pallas_call(
        body, name="small_all_reduce", in_specs=[VMEM_SPEC], out_specs=VMEM_SPEC,
        out_shape=jax.ShapeDtypeStruct(buf.shape, F32),
        scratch_shapes=[pltpu.VMEM((ndev, R, LANE), F32), pltpu.SemaphoreType.DMA((ndev - 1,)),
                        pltpu.SemaphoreType.DMA((ndev - 1,))],
    )(buf)


def _pair_add(name, own, other):
    h = own.shape[1]
    tm = _tile(h, (128, 64, 32, 16))

    def fn(i, a, b):
        return (a + b,)

    return _rowcall(name, fn, h, tm, [(own, "row", None), (other, "row", None)], [(own.shape, BF16, "row")])[0]


def _chip_add(name, own, parts):
    h = parts.shape[1]
    tm = _tile(h, (128, 64, 32, 16))

    def fn(i, o, a):
        a = a.astype(F32)
        return (((o.astype(F32) + a[0]) + a[1]) + a[2],)

    return _rowcall(name, fn, h, tm, [(own, "row", None), (parts, "row", None)], [(parts.shape[1:], F32, "row")])[0]


def _adamw(name, w, g, m, v):
    R = w.shape[0]
    tm = _tile(R, (256, 128, 64, 32, 16, 8))

    def fn(i, wv, gv, mv, vv):
        m2 = ADAM_B1 * mv + (1.0 - ADAM_B1) * gv
        v2 = ADAM_B2 * vv + (1.0 - ADAM_B2) * (gv * gv)
        m_hat = m2 / (1.0 - ADAM_B1 ** ADAM_STEP)
        v_hat = v2 / (1.0 - ADAM_B2 ** ADAM_STEP)
        delta = -ADAM_LR * (m_hat / (jnp.sqrt(v_hat) + ADAM_EPS) + ADAM_WD * wv)
        return delta, m2, v2

    ins = [(a, "row", None) for a in (w, g, m, v)]
    return _rowcall(name, fn, R, tm, ins, [(w.shape, F32, "row")] * 3)


def _pack(arrs):
    flat = jnp.concatenate([a.reshape(-1).astype(F32) for a in arrs])
    size = flat.shape[0]
    padded = -(-size // (SUBLANE * LANE)) * SUBLANE * LANE
    return jnp.pad(flat, (0, padded - size)).reshape(-1, LANE)


def _unpack(buf, shapes):
    flat = buf.reshape(-1)
    out, off = [], 0
    for s in shapes:
        size = math.prod(s)
        out.append(flat[off:off + size].reshape(s))
        off += size
    return out


BIG = ("a_w_in", "a_w_out", "w_kv", "b_w_q", "b_w_o", "ffn_w_up", "ffn_w_down")
WEIGHTS = ("a_norm_w", "a_w_in", "a_conv_w", "a_a_log", "a_dt_bias", "a_out_norm_w", "a_w_out", "kv_norm_w", "w_kv",
           "b_norm_w", "b_w_q", "b_sinks", "b_w_o", "rel_bias_table", "ffn_norm_w", "ffn_w_up", "ffn_conv_w",
           "ffn_conv_b", "ffn_w_down", "final_norm_w")
SMALL = tuple(n for n in WEIGHTS if n not in BIG)
SMALL_SHARDED = {"a_norm_w": 1, "a_conv_w": 2, "ffn_conv_w": 2}


def _quarter_2d(name, a):
    if name in ("ffn_w_up", "ffn_w_down"):
        return a.reshape(a.shape[0] * a.shape[1], a.shape[2])
    return a.reshape(a.shape[-2], a.shape[-1])


def _whole_weights(w):
    bigs = [_quarter_2d(n, w[n]).astype(BF16) for n in BIG]
    smalls = [w["a_norm_w"], w["a_conv_w"][0], w["ffn_conv_w"].reshape(6, DFF2_SHARD)]
    remote = [True] + [False] * (len(bigs) - 1) + [True] * len(smalls)
    g = _all_gather(bigs + smalls, [True] * len(bigs) + [False] * len(smalls), remote)
    w_in = g[0].transpose(1, 0, 2).reshape(D, GDN_IN)
    nh = GDN_V_HEADS
    zpad = jnp.zeros((D, LANE - nh), BF16)
    w_in_ba = jnp.concatenate([w_in[:, GDN_MAIN:GDN_MAIN + nh], zpad, w_in[:, GDN_MAIN + nh:], zpad], axis=1)
    lane_pad = lambda a: jnp.pad(a, ((0, 0), (0, LANE - nh)))
    early = dict(
        a_norm_w=g[7].reshape(1, D), w_in_main=w_in[:, :GDN_MAIN], w_in_ba=w_in_ba,
        a_conv_w=g[8].transpose(1, 0, 2).reshape(4, GDN_CONV), a_log=lane_pad(w["a_a_log"]),
        dt_bias=lane_pad(w["a_dt_bias"]), a_out_norm_w=w["a_out_norm_w"],
        kv_norm_w=w["kv_norm_w"].reshape(1, D), b_norm_w=w["b_norm_w"],
        sinks=jnp.broadcast_to(w["b_sinks"].reshape(SWA_KV_HEADS, SWA_GROUP, 1), (SWA_KV_HEADS, SWA_GROUP, LANE)),
        rel_table_t=w["rel_bias_table"].T, ffn_norm_w=w["ffn_norm_w"],
        ffn_conv_w=g[9].reshape(N_CHIPS, 2, 3, DFF2_SHARD).transpose(1, 2, 0, 3).reshape(2, 3, DFF2),
        ffn_conv_b=w["ffn_conv_b"], final_norm_w=w["final_norm_w"].reshape(1, D),
    )
    return early, (bigs[1:], g[1:len(bigs)])


def _late_weights(g):
    return dict(
        w_out=g[0].reshape(GDN_V, D), w_kv=g[1].reshape(D, 2 * SWA_KV_HEADS * SWA_HD), w_q=g[2].reshape(D, D),
        w_o=g[3].reshape(D, D), w_up=g[4].reshape(N_CHIPS, 2, D, DFF2_SHARD),
        w_down=g[5].reshape(N_CHIPS, 2, DFF_SHARD, D).transpose(1, 0, 2, 3).reshape(2, DFF, D),
    )


def _chip_major(name, g):
    if name == "a_w_in":
        return g.reshape(D, N_CHIPS, GDN_IN_SHARD).transpose(1, 0, 2)
    if name == "ffn_w_up":
        return g.reshape(N_CHIPS, 2 * D, DFF2_SHARD)
    if name == "ffn_w_down":
        return g.reshape(N_CHIPS, 2 * DFF_SHARD, D)
    return g.reshape(N_CHIPS, g.shape[0] // N_CHIPS, g.shape[1])


def kernel(x, a_norm_w, a_w_in, a_conv_w, a_a_log, a_dt_bias, a_out_norm_w, a_w_out, kv_norm_w, w_kv, b_norm_w, b_w_q, b_sinks, b_w_o, rel_bias_table, ffn_norm_w, ffn_w_up, ffn_conv_w, ffn_conv_b, ffn_w_down, final_norm_w, loss_target, m_a_norm_w, m_a_w_in, m_a_conv_w, m_a_a_log, m_a_dt_bias, m_a_out_norm_w, m_a_w_out, m_kv_norm_w, m_w_kv, m_b_norm_w, m_b_w_q, m_b_sinks, m_b_w_o, m_rel_bias_table, m_ffn_norm_w, m_ffn_w_up, m_ffn_conv_w, m_ffn_conv_b, m_ffn_w_down, m_final_norm_w, v_a_norm_w, v_a_w_in, v_a_conv_w, v_a_a_log, v_a_dt_bias, v_a_out_norm_w, v_a_w_out, v_kv_norm_w, v_w_kv, v_b_norm_w, v_b_w_q, v_b_sinks, v_b_w_o, v_rel_bias_table, v_ffn_norm_w, v_ffn_w_up, v_ffn_conv_w, v_ffn_conv_b, v_ffn_w_down, v_final_norm_w):
    w = dict(zip(WEIGHTS, (a_norm_w, a_w_in, a_conv_w, a_a_log, a_dt_bias, a_out_norm_w, a_w_out, kv_norm_w, w_kv,
                           b_norm_w, b_w_q, b_sinks, b_w_o, rel_bias_table, ffn_norm_w, ffn_w_up, ffn_conv_w,
                           ffn_conv_b, ffn_w_down, final_norm_w)))
    m = dict(zip(WEIGHTS, (m_a_norm_w, m_a_w_in, m_a_conv_w, m_a_a_log, m_a_dt_bias, m_a_out_norm_w, m_a_w_out,
                           m_kv_norm_w, m_w_kv, m_b_norm_w, m_b_w_q, m_b_sinks, m_b_w_o, m_rel_bias_table,
                           m_ffn_norm_w, m_ffn_w_up, m_ffn_conv_w, m_ffn_conv_b, m_ffn_w_down, m_final_norm_w)))
    v = dict(zip(WEIGHTS, (v_a_norm_w, v_a_w_in, v_a_conv_w, v_a_a_log, v_a_dt_bias, v_a_out_norm_w, v_a_w_out,
                           v_kv_norm_w, v_w_kv, v_b_norm_w, v_b_w_q, v_b_sinks, v_b_w_o, v_rel_bias_table,
                           v_ffn_norm_w, v_ffn_w_up, v_ffn_conv_w, v_ffn_conv_b, v_ffn_w_down, v_final_norm_w)))
    T = x.shape[1]
    chip = 2 * lax.axis_index("x") + lax.axis_index("y")

    core = lax.axis_index("c")

    def pair_sums(named, tag):
        names = [n for n in BIG if n in named]
        whole = [_chip_major(n, named[n]) for n in names]
        other = _pair_swap(whole, tag)
        own = [lax.dynamic_slice_in_dim(g, core * (g.shape[1] // 2), g.shape[1] // 2, 1) for g in whole]
        return [_pair_add(f"pair_add_{n}", a, b) for n, a, b in zip(names, own, other)]

    early, late = _whole_weights(w)
    loss_part, dx, grads, scattered = _local_step(x.reshape(T, D), loss_target.reshape(T, D), early, late, pair_sums)

    assert all(n in scattered for n in BIG)
    halves = [_chip_add(f"chip_add_{n}", lax.dynamic_index_in_dim(scattered[n][0], chip, 0, keepdims=False),
                        scattered[n][1]) for n in BIG]
    quarter = _pair_share(halves)
    out_g, out_d, out_m, out_v = {}, {}, {}, {}
    for n, g2 in zip(BIG, quarter):
        res = _adamw(f"adamw_{n}", _quarter_2d(n, w[n]), g2, _quarter_2d(n, m[n]), _quarter_2d(n, v[n]))
        out_g[n] = g2.reshape(w[n].shape)
        out_d[n], out_m[n], out_v[n] = (r.reshape(w[n].shape) for r in res)

    whole = [grads[n] for n in SMALL]
    summed = _unpack(_small_all_reduce(_pack([loss_part[0:1, 0:1]] + whole)), [(1, 1)] + [a.shape for a in whole])
    loss = summed[0].reshape(())
    small_g = []
    for n, g in zip(SMALL, summed[1:]):
        if n in SMALL_SHARDED:
            axis = SMALL_SHARDED[n]
            g = g.reshape(w[n].shape[:axis] + (-1,) + w[n].shape[axis + 1:])
            size = w[n].shape[axis]
            g = lax.dynamic_slice_in_dim(g, chip * size, size, axis)
        small_g.append(g.reshape(w[n].shape))
    shapes = [w[n].shape for n in SMALL]
    res = _adamw("adamw_small", _pack([w[n] for n in SMALL]), _pack(small_g), _pack([m[n] for n in SMALL]),
                 _pack([v[n] for n in SMALL]))
    small_d, small_m, small_v = (_unpack(r, shapes) for r in res)
    for i, n in enumerate(SMALL):
        out_g[n], out_d[n], out_m[n], out_v[n] = small_g[i], small_d[i], small_m[i], small_v[i]

    return (loss, dx.reshape(x.shape), *[out_g[n] for n in WEIGHTS], *[out_d[n] for n in WEIGHTS],
            *[out_m[n] for n in WEIGHTS], *[out_v[n] for n in WEIGHTS])
```

```python
import functools
import math

import jax
import jax.numpy as jnp
from jax import lax
from jax.experimental import pallas as pl
from jax.experimental.pallas import tpu as pltpu

F32 = jnp.float32
BF16 = jnp.bfloat16
MESH = pl.DeviceIdType.MESH
HIGHEST = lax.Precision.HIGHEST

D = 1024
EPS = 1e-6
NEG_INF = -1e30
N_CHIPS = 4

GDN_QK_HEADS = 8
GDN_V_HEADS = 16
GDN_HD = 128
GDN_QK = GDN_QK_HEADS * GDN_HD
GDN_V = GDN_V_HEADS * GDN_HD
GDN_CONV = 2 * GDN_QK + GDN_V
GDN_MAIN = GDN_CONV + GDN_V
GDN_IN = GDN_MAIN + 2 * GDN_V_HEADS
GDN_IN_SHARD = GDN_IN // N_CHIPS
GDN_CHUNK = 64

SWA_Q_HEADS = 16
SWA_KV_HEADS = 4
SWA_GROUP = 4
SWA_HD = 64
SWA_BLOCK = 128
REL_BUCKETS = 32
REL_MAX_DISTANCE = 128

DFF = 2816
DFF2 = 2 * DFF
DFF2_SHARD = DFF2 // N_CHIPS
DFF_SHARD = DFF // N_CHIPS

ADAM_LR = 0.001
ADAM_B1 = 0.9
ADAM_B2 = 0.999
ADAM_EPS = 1e-08
ADAM_WD = 0.01
ADAM_STEP = 10

LANE = 128
SUBLANE = 8
VMEM_LIMIT = 56 * 1024 * 1024


def _params(sem, vmem=VMEM_LIMIT):
    return pltpu.CompilerParams(dimension_semantics=sem, vmem_limit_bytes=vmem)


def _rowcall(name, fn, T, tm, ins, outs, swap=()):
    n = T // tm
    nswap = len(swap)
    swap_shapes = [g.shape for g in swap]
    r8 = tm // SUBLANE
    last8 = T // SUBLANE - 1
    arrays, in_specs = [], []
    for arr, kind, cols in ins:
        arrays.append(arr)
        if kind == "full":
            in_specs.append(pl.BlockSpec(arr.shape, functools.partial(lambda nd, i: (0,) * nd, arr.ndim)))
        elif arr.ndim == 2:
            w, ci = cols if cols is not None else (arr.shape[1], 0)
            if kind == "row":
                in_specs.append(pl.BlockSpec((tm, w), functools.partial(lambda ci, i: (i, ci), ci)))
            elif kind == "prev":
                in_specs.append(pl.BlockSpec(
                    (SUBLANE, w), functools.partial(lambda ci, i: (jnp.maximum(i * r8 - 1, 0), ci), ci)))
            else:
                in_specs.append(pl.BlockSpec(
                    (SUBLANE, w), functools.partial(lambda ci, i: (jnp.minimum((i + 1) * r8, last8), ci), ci)))
        else:
            lead = arr.shape[:-2]
            in_specs.append(pl.BlockSpec(lead + (tm, arr.shape[-1]),
                                         functools.partial(lambda nl, i: (0,) * nl + (i, 0), len(lead))))
    out_shape, out_specs = [], []
    for shape, dtype, kind in outs:
        out_shape.append(jax.ShapeDtypeStruct(shape, dtype))
        if kind == "acc":
            out_specs.append(pl.BlockSpec(shape, functools.partial(lambda nd, i: (0,) * nd, len(shape))))
        else:
            lead = shape[:-2]
            out_specs.append(pl.BlockSpec(lead + (tm, shape[-1]),
                                          functools.partial(lambda nl, i: (0,) * nl + (i, 0), len(lead))))
    nin = len(arrays)

    nout = len(outs)

    def body(*refs):
        i = pl.program_id(0)
        if nswap:
            comm = (swap_shapes, refs[nin:nin + nswap], refs[nin + nswap + nout:nin + 2 * nswap + nout], refs[-2],
                    refs[-1])

            @pl.when(i == 0)
            def _():
                for cp in _swap_copies(*comm):
                    cp.start()

            @pl.when(i == n - 1)
            def _():
                for cp in _swap_copies(*comm):
                    cp.wait()

        vals = [r[...] for r in refs[:nin]]
        res = fn(i, *vals)
        for (shape, dtype, kind), o, r in zip(outs, refs[nin + nswap:], res):
            if kind == "row":
                o[...] = r.astype(dtype)
            else:
                @pl.when(i == 0)
                def _():
                    o[...] = r.astype(dtype)

                @pl.when(i > 0)
                def _():
                    o[...] += r.astype(dtype)

    anywhere = pl.BlockSpec(memory_space=pl.ANY)
    nsem = nswap * N_CHIPS * PAIR_SWAP_PIECES
    return pl.pallas_call(
        body, name=name, grid=(n,), in_specs=in_specs + [anywhere] * nswap, out_specs=out_specs + [anywhere] * nswap,
        out_shape=out_shape + _swap_out_shapes(swap),
        scratch_shapes=[pltpu.SemaphoreType.DMA((nsem,)), pltpu.SemaphoreType.DMA((nsem,))] if nswap else [],
        compiler_params=_params(("arbitrary",)),
    )(*arrays, *swap)


def _mm(name, a, b, out_shape, out_dtype, grid, a_spec, b_spec, o_spec, dims, acc_shape, res=None, precision=None,
        into=None, scatter=()):
    nk = grid[2]
    ns = len(scatter)
    a, norm_w = a if isinstance(a, tuple) else (a, None)
    normed = norm_w is not None
    n_in = 2 + normed + (res is not None) + (into is not None) + ns

    def body(*refs):
        a_ref, b_ref, o_ref = refs[0], refs[1], refs[n_in]
        r_ref = refs[2 + normed] if res is not None else None
        if ns:
            comm = (refs[n_in - ns:n_in], refs[n_in + 1:n_in + 1 + ns], refs[-2], refs[-1])
            steps = [pl.program_id(d) for d in range(3)]

            @pl.when((steps[0] == 0) & (steps[1] == 0) & (steps[2] == 0))
            def _():
                for cp in _scatter_copies(*comm):
                    cp.start()

            @pl.when((steps[0] == grid[0] - 1) & (steps[1] == grid[1] - 1) & (steps[2] == grid[2] - 1))
            def _():
                copies = _scatter_copies(*comm)
                for cp in copies:
                    cp.wait_recv()
                for cp in copies:
                    cp.wait_send()

        av, bv = a_ref[...], b_ref[...]
        if normed:
            av = _rms_core(av, refs[2][...])
        if precision is None:
            av, bv = av.astype(BF16), bv.astype(BF16)
        p = lax.dot_general(av, bv, (dims, ((), ())), preferred_element_type=F32, precision=precision)

        def finish(x):
            if res is not None:
                x = x + r_ref[...].astype(F32)
            o_ref[...] = x.astype(out_dtype).reshape(o_ref.shape)

        if nk == 1:
            finish(p)
        else:
            acc = refs[n_in + 1 + ns]
            k = pl.program_id(2)

            @pl.when(k == 0)
            def _():
                acc[...] = p

            @pl.when(k > 0)
            def _():
                acc[...] += p

            @pl.when(k == nk - 1)
            def _():
                finish(acc[...])

    anywhere = pl.BlockSpec(memory_space=pl.ANY)
    ops = [a, b] + ([norm_w] if normed else []) + ([res] if res is not None else [])
    ops += ([into] if into is not None else []) + list(scatter)
    specs = [a_spec, b_spec] + ([pl.BlockSpec(norm_w.shape, lambda i, j, k: (0, 0))] if normed else [])
    specs += [o_spec] if res is not None else []
    specs += ([anywhere] if into is not None else []) + [anywhere] * ns
    out = pl.pallas_call(
        body, name=name, grid=grid, in_specs=specs, out_specs=[o_spec] + [anywhere] * ns,
        out_shape=[jax.ShapeDtypeStruct(out_shape, out_dtype)] + _scatter_shapes(scatter),
        input_output_aliases={n_in - ns - 1: 0} if into is not None else {},
        scratch_shapes=([pltpu.VMEM(acc_shape, F32)] if nk > 1 else [])
        + ([pltpu.SemaphoreType.DMA((3 * ns,)), pltpu.SemaphoreType.DMA((3 * ns,))] if ns else []),
        compiler_params=_params(("arbitrary",) * 3 if ns else ("parallel", "parallel", "arbitrary")),
    )(*ops)
    return out if ns else out[0]


NN = ((1,), (0,))
NT = ((1,), (1,))
TN = ((0,), (0,))


BIG_TILES = (1024, 512, 256, 128)


def _tile(n, pref):
    for t in pref:
        if n % t == 0:
            return t
    return n


def _rows_of(a):
    return a[0] if isinstance(a, tuple) else a


def _mm_nn(name, a, w, out_dtype, res=None, precision=None):
    M, K = _rows_of(a).shape
    N = w.shape[1]
    tm = _tile(M, BIG_TILES if K <= 2048 else BIG_TILES[1:])
    tn = _tile(N, BIG_TILES)
    return _mm(name, a, w, (M, N), out_dtype, (M // tm, N // tn, 1),
               pl.BlockSpec((tm, K), lambda i, j, k: (i, 0)), pl.BlockSpec((K, tn), lambda i, j, k: (0, j)),
               pl.BlockSpec((tm, tn), lambda i, j, k: (i, j)), NN, (tm, tn), res=res, precision=precision)


def _mm_nt(name, g, w, out_dtype, res=None, precision=None, scatter=()):
    M, N = g.shape
    K = w.shape[0]
    tm, tk = _tile(M, BIG_TILES), _tile(K, (1024, 1408, 512, 256, 128))
    tn = _tile(N, (1536,) + BIG_TILES)
    return _mm(name, g, w, (M, K), out_dtype, (M // tm, K // tk, N // tn),
               pl.BlockSpec((tm, tn), lambda i, j, k: (i, k)), pl.BlockSpec((tk, tn), lambda i, j, k: (j, k)),
               pl.BlockSpec((tm, tk), lambda i, j, k: (i, j)), NT, (tm, tk), res=res, precision=precision,
               scatter=scatter)


def _mm_tn(name, a, g, out_dtype=F32, precision=None):
    T, K = _rows_of(a).shape
    N = g.shape[1]
    tk, tn = _tile(K, (1024, 1408, 512, 256, 128)), _tile(N, BIG_TILES)
    assert tk == K or not isinstance(a, tuple)
    tt = _tile(T, BIG_TILES)
    return _mm(name, a, g, (K, N), out_dtype, (K // tk, N // tn, T // tt),
               pl.BlockSpec((tt, tk), lambda i, j, k: (k, i)), pl.BlockSpec((tt, tn), lambda i, j, k: (k, j)),
               pl.BlockSpec((tk, tn), lambda i, j, k: (i, j)), TN, (tk, tn), precision=precision)


def _mm_up(name, n, wup, layer):
    T = _rows_of(n).shape[0]
    tm = _tile(T, BIG_TILES)
    return _mm(name, n, wup, (T, DFF2), BF16, (T // tm, N_CHIPS, 1),
               pl.BlockSpec((tm, D), lambda i, j, k: (i, 0)),
               pl.BlockSpec((None, None, D, DFF2_SHARD), lambda i, j, k: (j, layer, 0, 0)),
               pl.BlockSpec((tm, DFF2_SHARD), lambda i, j, k: (i, j)), NN, (tm, DFF2_SHARD))


def _mm_up_nt(name, du, wup, layer):
    T = du.shape[0]
    tm, tk = _tile(T, BIG_TILES), D
    return _mm(name, du, wup, (T, D), F32, (T // tm, D // tk, N_CHIPS),
               pl.BlockSpec((tm, DFF2_SHARD), lambda i, j, k: (i, k)),
               pl.BlockSpec((None, None, tk, DFF2_SHARD), lambda i, j, k: (k, layer, j, 0)),
               pl.BlockSpec((tm, tk), lambda i, j, k: (i, j)), NT, (tm, tk))


def _mm_up_tn(name, n, du, layer, into):
    T = _rows_of(n).shape[0]
    tk, tt = D, _tile(T, BIG_TILES)
    return _mm(name, n, du, (N_CHIPS, 2, D, DFF2_SHARD), F32, (D // tk, N_CHIPS, T // tt),
               pl.BlockSpec((tt, tk), lambda i, j, k: (k, i)), pl.BlockSpec((tt, DFF2_SHARD), lambda i, j, k: (k, j)),
               pl.BlockSpec((None, None, tk, DFF2_SHARD), lambda i, j, k: (j, layer, i, 0)), TN, (tk, DFF2_SHARD),
               into=into)


def _mm_down_tn(name, act, dout, layer, into):
    T = act.shape[0]
    tk, tn, tt = 2 * DFF_SHARD, _tile(D, BIG_TILES), _tile(T, BIG_TILES)
    return _mm(name, act, dout, (2, 2, 2, DFF_SHARD, D), F32, (DFF // tk, D // tn, T // tt),
               pl.BlockSpec((tt, tk), lambda i, j, k: (k, i)), pl.BlockSpec((tt, tn), lambda i, j, k: (k, j)),
               pl.BlockSpec((None, 2, None, DFF_SHARD, tn), lambda i, j, k: (i, 0, layer, 0, j)), TN, (tk, tn),
               into=into)


def _sigmoid(x):
    return 0.5 * jnp.tanh(0.5 * x) + 0.5


def _silu(x):
    return x * _sigmoid(x)


def _softplus(x):
    return jnp.maximum(x, 0.0) + jnp.log(1.0 + jnp.exp(-jnp.abs(x)))


def _rms_core(h, w):
    return h * lax.rsqrt(jnp.mean(h * h, axis=-1, keepdims=True) + EPS) * w


def _shift_down(x, halo, s, i):
    if s == 0:
        return x
    tm = x.shape[0]
    rolled = pltpu.roll(x, s, 0)
    patch = pltpu.roll(jnp.where(i == 0, 0.0, halo), s, 0)
    row = lax.broadcasted_iota(jnp.int32, patch.shape, 0)
    top = jnp.where(row < s, patch, rolled[:SUBLANE])
    return jnp.concatenate([top, rolled[SUBLANE:]], axis=0) if tm > SUBLANE else top


def _shift_up(x, halo, s, i, n):
    if s == 0:
        return x
    tm = x.shape[0]
    rolled = pltpu.roll(x, tm - s, 0)
    patch = pltpu.roll(jnp.where(i == n - 1, 0.0, halo), SUBLANE - s, 0)
    row = lax.broadcasted_iota(jnp.int32, patch.shape, 0)
    bottom = jnp.where(row >= SUBLANE - s, patch, rolled[tm - SUBLANE:])
    return jnp.concatenate([rolled[:tm - SUBLANE], bottom], axis=0) if tm > SUBLANE else bottom


def _taps(x, halo, K, i):
    return [_shift_down(x, halo, K - 1 - j, i) for j in range(K)]


def _conv_fwd(taps, w):
    y = w[0:1, :] * taps[0]
    for j in range(1, len(taps)):
        y = y + w[j:j + 1, :] * taps[j]
    return y


def _conv_dx(dy, halo_next, w, i, n):
    K = w.shape[0]
    dx = w[K - 1:K, :] * dy
    for j in range(K - 1):
        dx = dx + w[j:j + 1, :] * _shift_up(dy, halo_next, K - 1 - j, i, n)
    return dx


def _conv_dw(dy, taps):
    rows = [jnp.sum(dy * tap, axis=0, keepdims=True) for tap in taps]
    return jnp.concatenate(rows + [jnp.zeros((SUBLANE - len(taps), dy.shape[1]), F32)], axis=0)


def _rms_bwd(name, h, pairs, adds, tm=256):
    T = h.shape[0]
    tm = min(tm, T)
    npair, nadd = len(pairs), len(adds)

    def fn(i, hv, *rest):
        ws, dns, ads = rest[:npair], rest[npair:2 * npair], rest[2 * npair:]
        dh = None
        dws = []
        for wv, dn in zip(ws, dns):
            _, vjp = jax.vjp(_rms_core, hv, wv)
            dhi, dwi = vjp(dn.astype(F32))
            dh = dhi if dh is None else dh + dhi
            dws.append(dwi)
        for a in ads:
            dh = dh + a.astype(F32)
        return (dh, *dws)

    ins = [(h, "row", None)] + [(w, "full", None) for w, _ in pairs] + [(dn, "row", None) for _, dn in pairs]
    ins += [(a, "row", None) for a in adds]
    outs = [((T, D), F32, "row")] + [((1, D), F32, "acc")] * npair
    return _rowcall(name, fn, T, tm, ins, outs)


def _l2(x):
    return x * lax.rsqrt(jnp.sum(x * x, axis=-1, keepdims=True) + EPS)


def _gdn_post_core(yq, yk, yv, pb, pa, a_log, dtb):
    qn = tuple(_l2(_silu(a)) * (GDN_HD ** -0.5) for a in yq)
    kn = tuple(_l2(_silu(a)) for a in yk)
    v = _silu(yv)
    beta = _sigmoid(pb)
    g = -jnp.exp(a_log) * _softplus(pa + dtb)
    return qn, kn, v, beta, g


def _heads(x, n):
    return tuple(x[:, GDN_HD * h:GDN_HD * (h + 1)] for h in range(n))


def _gdn_pre_fwd(pm, pba, conv_w, a_log, dtb, tm=128):
    T = pm.shape[0]
    tm = min(tm, T)

    def fn(i, x, halo, pbav, cw, al, db):
        y = _conv_fwd(_taps(x.astype(F32), halo.astype(F32), 4, i), cw)
        qn, kn, v, beta, g = _gdn_post_core(_heads(y[:, :GDN_QK], 8), _heads(y[:, GDN_QK:2 * GDN_QK], 8),
                                            y[:, 2 * GDN_QK:], pbav[:, :LANE], pbav[:, LANE:], al, db)
        return jnp.stack(qn), jnp.stack(kn), jnp.stack(_heads(v, GDN_V_HEADS)), beta, g

    ins = [(pm, "row", (GDN_CONV, 0)), (pm, "prev", (GDN_CONV, 0)), (pba, "row", None),
           (conv_w, "full", None), (a_log, "full", None), (dtb, "full", None)]
    outs = [((GDN_QK_HEADS, T, GDN_HD), BF16, "row"), ((GDN_QK_HEADS, T, GDN_HD), BF16, "row"),
            ((GDN_V_HEADS, T, GDN_HD), BF16, "row"), ((T, LANE), F32, "row"), ((T, LANE), F32, "row")]
    return _rowcall("gdn_pre_fwd", fn, T, tm, ins, outs)


def _gdn_pre_bwd(pm, pba, conv_w, a_log, dtb, dqn, dkn, dv, dbeta, dg, tm=128):
    T = pm.shape[0]
    tm = min(tm, T)

    def fn(i, x, halo, pbav, cw, al, db, dqv, dkv, dvv, dbv, dgv):
        taps = _taps(x.astype(F32), halo.astype(F32), 4, i)
        y = _conv_fwd(taps, cw)
        prim = (_heads(y[:, :GDN_QK], 8), _heads(y[:, GDN_QK:2 * GDN_QK], 8), y[:, 2 * GDN_QK:],
                pbav[:, :LANE], pbav[:, LANE:], al, db)
        _, vjp = jax.vjp(_gdn_post_core, *prim)
        cot = (tuple(dqv[h].astype(F32) for h in range(8)), tuple(dkv[h].astype(F32) for h in range(8)),
               jnp.concatenate([dvv[h].astype(F32) for h in range(GDN_V_HEADS)], axis=1), dbv, dgv)
        dyq, dyk, dyv, dpb, dpa, dal, ddb = vjp(cot)
        dy = jnp.concatenate(list(dyq) + list(dyk) + [dyv], axis=1)
        dcw = _conv_dw(dy, taps)
        return dy, jnp.concatenate([dpb, dpa], axis=1), dcw, dal, ddb

    ins = [(pm, "row", (GDN_CONV, 0)), (pm, "prev", (GDN_CONV, 0)), (pba, "row", None),
           (conv_w, "full", None), (a_log, "full", None), (dtb, "full", None),
           (dqn, "row", None), (dkn, "row", None), (dv, "row", None), (dbeta, "row", None), (dg, "row", None)]
    outs = [((T, GDN_CONV), BF16, "row"), ((T, 2 * LANE), F32, "row"), ((SUBLANE, GDN_CONV), F32, "acc"),
            ((1, LANE), F32, "acc"), ((1, LANE), F32, "acc")]
    return _rowcall("gdn_pre_bwd", fn, T, tm, ins, outs)


def _gdn_conv_bwd(dy, dz, conv_w, tm=256):
    T = dy.shape[0]
    tm = min(tm, T)
    n = T // tm

    def fn(i, dyv, halo, dzv, cw):
        dx = _conv_dx(dyv.astype(F32), halo.astype(F32), cw, i, n)
        return (jnp.concatenate([dx.astype(BF16), dzv.astype(BF16)], axis=1),)

    ins = [(dy, "row", None), (dy, "next", None), (dz, "row", None), (conv_w, "full", None)]
    return _rowcall("gdn_conv_bwd", fn, T, tm, ins, [((T, GDN_MAIN), BF16, "row")])[0]


def _bdot(a, b, dims=NN):
    return lax.dot_general(a.astype(BF16), b.astype(BF16), (dims, ((), ())), preferred_element_type=F32)


BNN = ((2,), (1,))
BNT = ((2,), (2,))
BTN = ((1,), (1,))


def _bmm(a, b, dims=BNN):
    return lax.dot_general(a.astype(BF16), b.astype(BF16), (dims, ((0,), (0,))), preferred_element_type=F32)


def _bmm3(a, b):
    ah, bh = a.astype(BF16), b.astype(BF16)
    al, bl = (a - ah.astype(F32)).astype(BF16), (b - bh.astype(F32)).astype(BF16)
    dn = (BNN, ((0,), (0,)))
    return (lax.dot_general(ah, bh, dn, preferred_element_type=F32)
            + lax.dot_general(al, bh, dn, preferred_element_type=F32)
            + lax.dot_general(ah, bl, dn, preferred_element_type=F32))


def _tri_inv(m):
    C = m.shape[-1]
    r = lax.broadcasted_iota(jnp.int32, (C, C), 0)
    c = lax.broadcasted_iota(jnp.int32, (C, C), 1)
    t = jnp.where(r == c, 1.0, 0.0) - m
    pw = _bmm3(m, m)
    t = t + _bmm3(t, pw)
    for _ in range(int(math.log2(C)) - 2):
        pw = _bmm(pw, pw)
        t = t + _bmm(t, pw)
    return t


def _tri_inv_vjp(t, dt):
    tt = jnp.swapaxes(t, 1, 2)
    return -_bmm(_bmm(tt, dt), tt)


def _twice(a):
    return jnp.broadcast_to(a[:, None], (a.shape[0], 2) + a.shape[1:]).reshape((2 * a.shape[0],) + a.shape[1:])


def _gdn_gates(grow, brow):
    C = grow.shape[2]
    r = lax.broadcasted_iota(jnp.int32, (C, C), 0)
    c = lax.broadcasted_iota(jnp.int32, (C, C), 1)
    tril, eye = r >= c, r == c
    gcol = jnp.sum(jnp.where(eye, grow, 0.0), axis=2, keepdims=True)
    bcol = jnp.sum(jnp.where(eye, brow, 0.0), axis=2, keepdims=True)
    gc_col = jnp.sum(jnp.where(tril, grow, 0.0), axis=2, keepdims=True)
    gc_row = jnp.sum(jnp.where(r <= c, gcol, 0.0), axis=1, keepdims=True)
    gc_last = jnp.sum(grow, axis=2, keepdims=True)
    decay = jnp.where(tril, jnp.exp(jnp.where(tril, gc_col - gc_row, 0.0)), 0.0)
    return bcol, gc_col, gc_last, decay


def _gdn_m(k, bcol, decay):
    C = k.shape[1]
    strict = lax.broadcasted_iota(jnp.int32, (C, C), 0) > lax.broadcasted_iota(jnp.int32, (C, C), 1)
    return jnp.where(strict, bcol * _twice(_bmm(k, k, BNT)) * decay, 0.0)


def _gdn_rest(q, k, v, bcol, gc_col, gc_last, decay, t_mat, S):
    qk = _twice(_bmm(q, k, BNT))
    k2, q2 = _twice(k), _twice(q)
    egc = jnp.exp(gc_col)
    u = _bmm(t_mat, v * bcol)
    w = _bmm(t_mat, k2 * (bcol * egc))
    v_new = u - _bmm(w, S)
    o = _bmm(q2 * egc, S) + _bmm(qk * decay, v_new)
    s_new = S * jnp.exp(gc_last) + _bmm(k2 * jnp.exp(gc_last - gc_col), v_new, BTN)
    return o, s_new


def _gdn_tb(T):
    return min(256, T)


def _gate_rows(g):
    T = g.shape[0]
    g = g[:, :GDN_V_HEADS].reshape(T // GDN_CHUNK, GDN_CHUNK, GDN_V_HEADS)
    return g.transpose(0, 2, 1)[:, :, None, :]


def _gate_cols(g):
    nc = g.shape[0]
    g = g[:, :, 0, :].transpose(0, 2, 1).reshape(nc * GDN_CHUNK, GDN_V_HEADS)
    return jnp.pad(g, ((0, 0), (0, LANE - GDN_V_HEADS)))


def _gdn_fwd(qn, kn, v, g, beta, gather=None):
    T = qn.shape[1]
    tb = _gdn_tb(T)
    nc = tb // GDN_CHUNK
    nsteps = T // tb
    quarters, buffers = gather if gather is not None else ((), ())
    ng = len(quarters)
    shapes = [a.shape for a in quarters]
    splits = [True] * ng

    def body(*refs):
        q_ref, k_ref, v_ref, g_ref, b_ref = refs[:5]
        src = refs[5:5 + ng]
        o_ref, sall_ref, tall_ref = refs[5 + 2 * ng:8 + 2 * ng]
        dst = refs[8 + 2 * ng:8 + 3 * ng]
        s_scr = refs[8 + 3 * ng]
        step = pl.program_id(0)

        @pl.when(step == 0)
        def _():
            s_scr[...] = jnp.zeros(s_scr.shape, F32)
            if ng:
                for cp in _gather_copies(shapes, splits, src, dst, *refs[9 + 3 * ng:])[0]:
                    cp.start()

        def chunk(ci, carry):
            rows = pl.ds(pl.multiple_of(ci * GDN_CHUNK, GDN_CHUNK), GDN_CHUNK)
            s = s_scr[...]
            sall_ref[ci] = s
            q, k = q_ref[:, rows, :].astype(F32), k_ref[:, rows, :].astype(F32)
            bcol, gc_col, gc_last, decay = _gdn_gates(g_ref[ci], b_ref[ci])
            t_mat = _tri_inv(_gdn_m(k, bcol, decay)).astype(BF16)
            tall_ref[ci] = t_mat
            o, s_new = _gdn_rest(q, k, v_ref[:, rows, :].astype(F32), bcol, gc_col, gc_last, decay,
                                 t_mat.astype(F32), s)
            o_ref[:, rows, :] = o.astype(o_ref.dtype)
            s_scr[...] = s_new
            return carry

        lax.fori_loop(0, nc, chunk, 0)

        if ng:
            @pl.when(step == nsteps - 1)
            def _():
                _gather_arrival(shapes, splits, src, dst, *refs[9 + 3 * ng:])

    qk_spec = pl.BlockSpec((GDN_QK_HEADS, tb, GDN_HD), lambda i: (0, i, 0))
    v_spec = pl.BlockSpec((GDN_V_HEADS, tb, GDN_HD), lambda i: (0, i, 0))
    g_spec = pl.BlockSpec((nc, GDN_V_HEADS, 1, GDN_CHUNK), lambda i: (i, 0, 0, 0))
    anywhere = pl.BlockSpec(memory_space=pl.ANY)
    return pl.pallas_call(
        body, name="gdn_fwd", grid=(nsteps,),
        in_specs=[qk_spec, qk_spec, v_spec, g_spec, g_spec] + [anywhere] * (2 * ng),
        out_specs=[v_spec, pl.BlockSpec((nc, GDN_V_HEADS, GDN_HD, GDN_HD), lambda i: (i, 0, 0, 0)),
                   pl.BlockSpec((nc, GDN_V_HEADS, GDN_CHUNK, GDN_CHUNK), lambda i: (i, 0, 0, 0))] + [anywhere] * ng,
        out_shape=[jax.ShapeDtypeStruct((GDN_V_HEADS, T, GDN_HD), BF16),
                   jax.ShapeDtypeStruct((T // GDN_CHUNK, GDN_V_HEADS, GDN_HD, GDN_HD), F32),
                   jax.ShapeDtypeStruct((T // GDN_CHUNK, GDN_V_HEADS, GDN_CHUNK, GDN_CHUNK), BF16)]
        + [jax.ShapeDtypeStruct(b.shape, b.dtype) for b in buffers],
        input_output_aliases={5 + ng + a: 3 + a for a in range(ng)},
        scratch_shapes=[pltpu.VMEM((GDN_V_HEADS, GDN_HD, GDN_HD), F32)]
        + ([pltpu.SemaphoreType.DMA((6 * ng,)), pltpu.SemaphoreType.DMA((6 * ng,))] if ng else []),
        compiler_params=_params(("arbitrary",)),
    )(qn, kn, v, g, beta, *quarters, *buffers)


def _gdn_bwd(qn, kn, v, g, beta, sall, tall, do, scatter=()):
    T = qn.shape[1]
    tb = _gdn_tb(T)
    nc = tb // GDN_CHUNK
    nb = T // tb
    ns = len(scatter)

    def body(*refs):
        q_ref, k_ref, v_ref, g_ref, b_ref, sall_ref, tall_ref, do_ref = refs[:8]
        dq_ref, dk_ref, dv_ref, dg_ref, db_ref = refs[8 + ns:13 + ns]
        ds_scr = refs[13 + 2 * ns]
        comm = (refs[8:8 + ns], refs[13 + ns:13 + 2 * ns], *refs[14 + 2 * ns:])
        step = pl.program_id(0)

        @pl.when(step == 0)
        def _():
            ds_scr[...] = jnp.zeros(ds_scr.shape, F32)
            if ns:
                for cp in _scatter_copies(*comm):
                    cp.start()

        def chunk(cr, carry):
            ci = nc - 1 - cr
            rows = pl.ds(pl.multiple_of(ci * GDN_CHUNK, GDN_CHUNK), GDN_CHUNK)
            k, t_mat = k_ref[:, rows, :].astype(F32), tall_ref[ci].astype(F32)
            (bcol, gc_col, gc_last, decay), vjp_gates = jax.vjp(_gdn_gates, g_ref[ci], b_ref[ci])
            _, vjp = jax.vjp(_gdn_rest, q_ref[:, rows, :].astype(F32), k, v_ref[:, rows, :].astype(F32),
                             bcol, gc_col, gc_last, decay, t_mat, sall_ref[ci])
            dq, dk, dv, dbcol, dgc_col, dgc_last, ddecay, dt, ds = vjp((do_ref[:, rows, :].astype(F32), ds_scr[...]))
            _, vjp_m = jax.vjp(_gdn_m, k, bcol, decay)
            dk_m, dbcol_m, ddecay_m = vjp_m(_tri_inv_vjp(t_mat, dt))
            dg, db = vjp_gates((dbcol + dbcol_m, dgc_col, dgc_last, ddecay + ddecay_m))
            ds_scr[...] = ds
            dq_ref[:, rows, :] = dq
            dk_ref[:, rows, :] = dk + dk_m
            dv_ref[:, rows, :] = dv
            dg_ref[ci] = dg
            db_ref[ci] = db
            return carry

        lax.fori_loop(0, nc, chunk, 0)

        if ns:
            @pl.when(step == nb - 1)
            def _():
                copies = _scatter_copies(*comm)
                for cp in copies:
                    cp.wait_recv()
                for cp in copies:
                    cp.wait_send()

    qk_spec = pl.BlockSpec((GDN_QK_HEADS, tb, GDN_HD), lambda i: (0, nb - 1 - i, 0))
    v_spec = pl.BlockSpec((GDN_V_HEADS, tb, GDN_HD), lambda i: (0, nb - 1 - i, 0))
    g_spec = pl.BlockSpec((nc, GDN_V_HEADS, 1, GDN_CHUNK), lambda i: (nb - 1 - i, 0, 0, 0))
    s_spec = pl.BlockSpec((nc, GDN_V_HEADS, GDN_HD, GDN_HD), lambda i: (nb - 1 - i, 0, 0, 0))
    t_spec = pl.BlockSpec((nc, GDN_V_HEADS, GDN_CHUNK, GDN_CHUNK), lambda i: (nb - 1 - i, 0, 0, 0))
    anywhere = pl.BlockSpec(memory_space=pl.ANY)
    return pl.pallas_call(
        body, name="gdn_bwd", grid=(nb,),
        in_specs=[qk_spec, qk_spec, v_spec, g_spec, g_spec, s_spec, t_spec, v_spec] + [anywhere] * ns,
        out_specs=[qk_spec, qk_spec, v_spec, g_spec, g_spec] + [anywhere] * ns,
        out_shape=[jax.ShapeDtypeStruct((GDN_QK_HEADS, T, GDN_HD), F32),
                   jax.ShapeDtypeStruct((GDN_QK_HEADS, T, GDN_HD), F32),
                   jax.ShapeDtypeStruct((GDN_V_HEADS, T, GDN_HD), F32),
                   jax.ShapeDtypeStruct(g.shape, F32), jax.ShapeDtypeStruct(g.shape, F32)]
        + _scatter_shapes(scatter),
        scratch_shapes=[pltpu.VMEM((GDN_V_HEADS, GDN_HD, GDN_HD), F32)]
        + ([pltpu.SemaphoreType.DMA((3 * ns,)), pltpu.SemaphoreType.DMA((3 * ns,))] if ns else []),
        compiler_params=_params(("arbitrary",)),
    )(qn, kn, v, g, beta, sall, tall, do, *scatter)


def _gnorm_core(o, z, w):
    return tuple(_rms_core(oh, w) * _silu(zh) for oh, zh in zip(o, z))


def _gnorm_fwd(o, pm, w, tm=256):
    T = pm.shape[0]
    tm = min(tm, T)

    def fn(i, ov, zv, wv):
        zf = zv.astype(F32)
        out = _gnorm_core(tuple(ov[h].astype(F32) for h in range(GDN_V_HEADS)), _heads(zf, GDN_V_HEADS), wv)
        return (jnp.concatenate(out, axis=1),)

    ins = [(o, "row", None), (pm, "row", (GDN_V, 2)), (w, "full", None)]
    return _rowcall("gnorm_fwd", fn, T, tm, ins, [((T, GDN_V), BF16, "row")])[0]


def _gnorm_bwd(o, pm, w, don, tm=128, swap=()):
    T = pm.shape[0]
    tm = min(tm, T)

    def fn(i, ov, zv, wv, dv):
        zf, df = zv.astype(F32), dv.astype(F32)
        _, vjp = jax.vjp(_gnorm_core, tuple(ov[h].astype(F32) for h in range(GDN_V_HEADS)),
                         _heads(zf, GDN_V_HEADS), wv)
        do, dz, dw = vjp(_heads(df, GDN_V_HEADS))
        return jnp.stack(do), jnp.concatenate(dz, axis=1), dw

    ins = [(o, "row", None), (pm, "row", (GDN_V, 2)), (w, "full", None), (don, "row", None)]
    outs = [((GDN_V_HEADS, T, GDN_HD), BF16, "row"), ((T, GDN_V), BF16, "row"), ((1, GDN_HD), F32, "acc")]
    return _rowcall("gnorm_bwd", fn, T, tm, ins, outs, swap=swap)


def _ffn_act_fwd(name, up, conv_w, conv_b, tm=128):
    T = up.shape[0]
    tm = min(tm, T)

    def fn(i, x, halo, cw, cb):
        u = _conv_fwd(_taps(x.astype(F32), halo.astype(F32), 3, i), cw) + cb
        return (_silu(u[:, :DFF]) * u[:, DFF:],)

    ins = [(up, "row", None), (up, "prev", None), (conv_w, "full", None), (conv_b, "full", None)]
    return _rowcall(name, fn, T, tm, ins, [((T, DFF), BF16, "row")])[0]


def _ffn_act_bwd(name, up, conv_w, conv_b, dact, tm=128):
    T = up.shape[0]
    tm = min(tm, T)

    def fn(i, x, halo, cw, cb, da):
        taps = _taps(x.astype(F32), halo.astype(F32), 3, i)
        da = da.astype(F32)
        u = _conv_fwd(taps, cw) + cb
        gate, val = u[:, :DFF], u[:, DFF:]
        sg = _sigmoid(gate)
        dgate = da * val * sg * (1.0 + gate * (1.0 - sg))
        dval = da * gate * sg
        du = jnp.concatenate([dgate, dval], axis=1)
        return du, _conv_dw(du, taps), jnp.sum(du, axis=0, keepdims=True)

    ins = [(up, "row", None), (up, "prev", None), (conv_w, "full", None), (conv_b, "full", None),
           (dact, "row", None)]
    outs = [((T, DFF2), BF16, "row"), ((SUBLANE, DFF2), F32, "acc"), ((1, DFF2), F32, "acc")]
    return _rowcall(name, fn, T, tm, ins, outs)


def _ffn_conv_bwd(name, du, conv_w, tm=256):
    T = du.shape[0]
    tm = min(tm, T)
    n = T // tm

    def fn(i, dv, halo, cw):
        return (_conv_dx(dv.astype(F32), halo.astype(F32), cw, i, n),)

    ins = [(du, "row", None), (du, "next", None), (conv_w, "full", None)]
    return _rowcall(name, fn, T, tm, ins, [((T, DFF2), BF16, "row")])[0]


GROUP_ROWS = SWA_GROUP * SWA_BLOCK


def _attn_core(q, kp, kc, vp, vc, bias, sink, mask):
    kcat = jnp.concatenate([kp, kc], axis=0)
    vcat = jnp.concatenate([vp, vc], axis=0)
    s = _bdot(q * (SWA_HD ** -0.5), kcat, NT) + bias
    s = jnp.where(mask, s, NEG_INF)
    m = lax.stop_gradient(jnp.maximum(jnp.max(s, axis=-1, keepdims=True), sink))
    p = jnp.exp(s - m)
    denom = jnp.sum(p, axis=-1, keepdims=True) + jnp.exp(sink - m)
    return _bdot(p / denom, vcat)


def _attn_mask(i):
    qi = lax.broadcasted_iota(jnp.int32, (GROUP_ROWS, 2 * SWA_BLOCK), 0) & (SWA_BLOCK - 1)
    ki = lax.broadcasted_iota(jnp.int32, (GROUP_ROWS, 2 * SWA_BLOCK), 1)
    dist = qi + SWA_BLOCK - ki
    return (dist >= 0) & (dist < SWA_BLOCK) & ((ki >= SWA_BLOCK) | (i > 0))


def _head_cols(h):
    return slice(h * SWA_HD, (h + 1) * SWA_HD)


def _stacked_heads(ref, j):
    return jnp.concatenate([ref[:, _head_cols(SWA_GROUP * j + g)].astype(F32) for g in range(SWA_GROUP)], axis=0)


def _flat_operands(j, q_ref, kvc_ref, kvp_ref, b_ref, s_ref):
    heads = slice(SWA_GROUP * j, SWA_GROUP * (j + 1))
    sink = jnp.concatenate([jnp.broadcast_to(s_ref[j, g:g + 1, 0:1], (SWA_BLOCK, 1)) for g in range(SWA_GROUP)],
                           axis=0)
    k_cols, v_cols = _head_cols(j), _head_cols(SWA_KV_HEADS + j)
    return (_stacked_heads(q_ref, j), kvp_ref[:, k_cols].astype(F32), kvc_ref[:, k_cols].astype(F32),
            kvp_ref[:, v_cols].astype(F32), kvc_ref[:, v_cols].astype(F32),
            b_ref[heads].reshape(GROUP_ROWS, 2 * SWA_BLOCK), sink)


def _store_heads(ref, j, stacked):
    for g in range(SWA_GROUP):
        ref[:, _head_cols(SWA_GROUP * j + g)] = stacked[g * SWA_BLOCK:(g + 1) * SWA_BLOCK].astype(ref.dtype)


def _attn_fwd_flat(q, kv, bias, sinks):
    T = q.shape[0]
    nb = T // SWA_BLOCK

    def body(q_ref, kvc_ref, kvp_ref, b_ref, s_ref, o_ref):
        mask = _attn_mask(pl.program_id(0))
        operands = [_flat_operands(j, q_ref, kvc_ref, kvp_ref, b_ref, s_ref) for j in range(SWA_KV_HEADS)]
        outs = [_attn_core(*ops, mask) for ops in operands]
        for j in range(SWA_KV_HEADS):
            _store_heads(o_ref, j, outs[j])

    q_spec = pl.BlockSpec((SWA_BLOCK, q.shape[1]), lambda i: (i, 0))
    cur = pl.BlockSpec((SWA_BLOCK, kv.shape[1]), lambda i: (i, 0))
    prev = pl.BlockSpec((SWA_BLOCK, kv.shape[1]), lambda i: (jnp.maximum(i - 1, 0), 0))
    return pl.pallas_call(
        body, name="attn_fwd", grid=(nb,),
        in_specs=[q_spec, cur, prev, pl.BlockSpec(bias.shape, lambda i: (0, 0, 0)),
                  pl.BlockSpec(sinks.shape, lambda i: (0, 0, 0))],
        out_specs=q_spec, out_shape=jax.ShapeDtypeStruct(q.shape, BF16),
        compiler_params=_params(("arbitrary",)),
    )(q, kv, kv, bias, sinks)


def _attn_bwd_flat(q, kv, bias, sinks, do):
    T = q.shape[0]
    nb = T // SWA_BLOCK

    def body(q_ref, kvc_ref, kvp_ref, b_ref, s_ref, do_ref, dq_ref, dkv_ref, db_ref, dsk_ref, carry):
        i = pl.program_id(0)

        @pl.when(i < nb)
        def _():
            mask = _attn_mask(i)
            operands = [_flat_operands(j, q_ref, kvc_ref, kvp_ref, b_ref, s_ref) for j in range(SWA_KV_HEADS)]
            cots = [_stacked_heads(do_ref, j) for j in range(SWA_KV_HEADS)]
            grads = [jax.vjp(functools.partial(_attn_core, mask=mask), *ops)[1](cot)
                     for ops, cot in zip(operands, cots)]
            for j, (dq, dkp, dkc, dvp, dvc, db, dsc) in enumerate(grads):
                heads = slice(SWA_GROUP * j, SWA_GROUP * (j + 1))
                k_cols, v_cols = _head_cols(j), _head_cols(SWA_KV_HEADS + j)
                _store_heads(dq_ref, j, dq)
                db = db.reshape(SWA_GROUP, SWA_BLOCK, 2 * SWA_BLOCK)
                dsk = jnp.concatenate(
                    [jnp.broadcast_to(jnp.sum(dsc[g * SWA_BLOCK:(g + 1) * SWA_BLOCK], axis=0, keepdims=True),
                                      (1, LANE)) for g in range(SWA_GROUP)], axis=0)

                @pl.when(i == 0)
                def _():
                    db_ref[heads] = db
                    dsk_ref[j] = dsk

                @pl.when(i > 0)
                def _():
                    db_ref[heads] += db
                    dsk_ref[j] += dsk
                    dkv_ref[:, k_cols] = (carry[:, k_cols] + dkp).astype(dkv_ref.dtype)
                    dkv_ref[:, v_cols] = (carry[:, v_cols] + dvp).astype(dkv_ref.dtype)

                carry[:, k_cols] = dkc
                carry[:, v_cols] = dvc

        @pl.when(i == nb)
        def _():
            dkv_ref[...] = carry[...].astype(dkv_ref.dtype)

    last = nb - 1
    q_spec = pl.BlockSpec((SWA_BLOCK, q.shape[1]), lambda i: (jnp.minimum(i, last), 0))
    cur = pl.BlockSpec((SWA_BLOCK, kv.shape[1]), lambda i: (jnp.minimum(i, last), 0))
    prev = pl.BlockSpec((SWA_BLOCK, kv.shape[1]), lambda i: (jnp.clip(i - 1, 0, last), 0))
    b_spec = pl.BlockSpec(bias.shape, lambda i: (0, 0, 0))
    s_spec = pl.BlockSpec(sinks.shape, lambda i: (0, 0, 0))
    return pl.pallas_call(
        body, name="attn_bwd", grid=(nb + 1,),
        in_specs=[q_spec, cur, prev, b_spec, s_spec, q_spec],
        out_specs=[q_spec, prev, b_spec, s_spec],
        out_shape=[jax.ShapeDtypeStruct(q.shape, BF16), jax.ShapeDtypeStruct(kv.shape, BF16),
                   jax.ShapeDtypeStruct(bias.shape, F32), jax.ShapeDtypeStruct(sinks.shape, F32)],
        scratch_shapes=[pltpu.VMEM((SWA_BLOCK, kv.shape[1]), F32)],
        compiler_params=_params(("arbitrary",)),
    )(q, kv, kv, bias, sinks, do)


def _rel_onehot():
    qi = jnp.arange(SWA_BLOCK)[:, None]
    ki = jnp.arange(2 * SWA_BLOCK)[None, :]
    n = jnp.maximum(qi + SWA_BLOCK - ki, 0)
    max_exact = REL_BUCKETS // 2
    nf = jnp.maximum(n, 1).astype(F32)
    large = max_exact + (jnp.log(nf / max_exact) / math.log(REL_MAX_DISTANCE / max_exact)
                         * (REL_BUCKETS - max_exact)).astype(jnp.int32)
    bucket = jnp.where(n < max_exact, n, jnp.minimum(large, REL_BUCKETS - 1)).reshape(-1)
    return (bucket[None, :] == jnp.arange(REL_BUCKETS)[:, None]).astype(F32)


def _final(h, w, target, tm=256):
    T = h.shape[0]
    tm = min(tm, T)

    def fn(i, hv, wv, tv):
        y, vjp = jax.vjp(_rms_core, hv, wv)
        err = y - tv
        dh, dw = vjp(err * (1.0 / D))
        part = 0.5 * jnp.sum(jnp.sum(err * err, axis=1, keepdims=True) * (1.0 / D), axis=0, keepdims=True)
        return jnp.broadcast_to(part, (SUBLANE, LANE)), dh, dw

    ins = [(h, "row", None), (w, "full", None), (target, "row", None)]
    outs = [((SUBLANE, LANE), F32, "acc"), ((T, D), F32, "row"), ((1, D), F32, "acc")]
    return _rowcall("final", fn, T, tm, ins, outs)


def _ffn_fwd(tag, h, P, layer):
    n = (h, P["ffn_norm_w"][layer:layer + 1])
    up = _mm_up(f"{tag}_up", n, P["w_up"], layer)
    act = _ffn_act_fwd(f"{tag}_act", up, P["ffn_conv_w"][layer], P["ffn_conv_b"][layer:layer + 1])
    out = _mm_nn(f"{tag}_down", act, P["w_down"][layer], F32, res=h)
    return out, (n, up, act)


def _ffn_bwd(tag, h, saved, dout, P, layer, into=(None, None)):
    n, up, act = saved
    cw, cb = P["ffn_conv_w"][layer], P["ffn_conv_b"][layer:layer + 1]
    dact = _mm_nt(f"{tag}_down_dx", dout, P["w_down"][layer], BF16)
    g_down = _mm_down_tn(f"{tag}_down_dw", act, dout, layer, into[1])
    du, dcw, dcb = _ffn_act_bwd(f"{tag}_act_bwd", up, cw, cb, dact)
    dup = _ffn_conv_bwd(f"{tag}_conv_bwd", du, cw)
    g_up = _mm_up_tn(f"{tag}_up_dw", n, dup, layer, into[0])
    dn = _mm_up_nt(f"{tag}_up_dx", dup, P["w_up"], layer)
    dh, dnw = _rms_bwd(f"{tag}_rms_bwd", h, [(P["ffn_norm_w"][layer:layer + 1], dn)], [dout])
    return dh, dict(w_down=g_down, w_up=g_up, conv_w=dcw[:3], conv_b=dcb, norm_w=dnw)


def _local_step(x, target, P, late=None, pair_sums=None):
    T = x.shape[0]
    n0 = (x, P["a_norm_w"])
    pm = _mm_nn("gdn_in", n0, P["w_in_main"], BF16)
    pba = _mm_nn("gdn_in_ba", n0, P["w_in_ba"], F32)
    qn, kn, v, beta, g = _gdn_pre_fwd(pm, pba, P["a_conv_w"], P["a_log"], P["dt_bias"])
    g_rows, beta_rows = _gate_rows(g), _gate_rows(beta)
    o, sall, tall, *gathered = _gdn_fwd(qn, kn, v, g_rows, beta_rows, gather=late)
    if late is not None:
        P = {**P, **_late_weights(gathered)}
    on = _gnorm_fwd(o, pm, P["a_out_norm_w"])
    h1 = _mm_nn("gdn_out", on, P["w_out"], F32, res=x)
    h2, ffn0 = _ffn_fwd("ffn0", h1, P, 0)
    nkv = (h2, P["kv_norm_w"])
    kv = _mm_nn("kv_proj", nkv, P["w_kv"], BF16)
    nb = (h2, P["b_norm_w"])
    qp = _mm_nn("q_proj", nb, P["w_q"], BF16)
    onehot = _rel_onehot()
    bias = _mm_nn("rel_bias", P["rel_table_t"], onehot, F32, precision=HIGHEST)
    bias = bias.reshape(SWA_Q_HEADS, SWA_BLOCK, 2 * SWA_BLOCK)
    oa = _attn_fwd_flat(qp, kv, bias, P["sinks"])
    h3 = _mm_nn("o_proj", oa, P["w_o"], F32, res=h2)
    h4, ffn1 = _ffn_fwd("ffn1", h3, P, 1)
    loss, dh4, d_final = _final(h4, P["final_norm_w"], target)

    dh3, gf1 = _ffn_bwd("ffn1", h3, ffn1, dh4, P, 1)
    doa = _mm_nt("o_proj_dx", dh3, P["w_o"], BF16)
    g_wo = _mm_tn("o_proj_dw", oa, dh3)
    dqp, dkv, dbias, dsinks = _attn_bwd_flat(qp, kv, bias, P["sinks"], doa)
    g_wq = _mm_tn("q_proj_dw", nb, dqp)
    dnb = _mm_nt("q_proj_dx", dqp, P["w_q"], F32)
    g_wkv = _mm_tn("kv_proj_dw", nkv, dkv)
    dnkv = _mm_nt("kv_proj_dx", dkv, P["w_kv"], F32)
    dh2, d_bnorm, d_kvnorm = _rms_bwd("b_kv_rms_bwd", h2, [(P["b_norm_w"], dnb), (P["kv_norm_w"], dnkv)], [dh3])
    g_table = _mm_nt("rel_bias_dw", onehot, dbias.reshape(SWA_Q_HEADS, -1), F32, precision=HIGHEST)
    dh1, gf0 = _ffn_bwd("ffn0", h1, ffn0, dh2, P, 0, into=(gf1["w_up"], gf1["w_down"]))
    don = _mm_nt("gdn_out_dx", dh1, P["w_out"], BF16)
    g_wout = _mm_tn("gdn_out_dw", on, dh1)
    ready = dict(a_w_out=g_wout, w_kv=g_wkv, b_w_q=g_wq, b_w_o=g_wo, ffn_w_up=gf0["w_up"], ffn_w_down=gf0["w_down"])
    names = [n for n in BIG if n in ready]
    whole = [_chip_major(n, ready[n]) for n in names] if pair_sums is not None else []
    do, dz, d_gnorm, *other = _gnorm_bwd(o, pm, P["a_out_norm_w"], don, swap=whole)
    pairs = pair_sums(names, whole, other) if pair_sums is not None else []
    dq, dk, dv, dg, dbeta, *parts = _gdn_bwd(qn, kn, v, g_rows, beta_rows, sall, tall, do, scatter=pairs)
    dy, dpba, d_aconv, d_alog, d_dtb = _gdn_pre_bwd(pm, pba, P["a_conv_w"], P["a_log"], P["dt_bias"],
                                                    dq, dk, dv, _gate_cols(dbeta), _gate_cols(dg))
    dpm = _gdn_conv_bwd(dy, dz, P["a_conv_w"])
    g_win_main = _mm_tn("gdn_in_dw", n0, dpm)
    g_win_ba = _mm_tn("gdn_in_ba_dw", n0, dpba)
    nh = GDN_V_HEADS
    g_win = jnp.concatenate([g_win_main, g_win_ba[:, :nh], g_win_ba[:, LANE:LANE + nh]], axis=1)
    last_whole = [_chip_major("a_w_in", g_win)]
    last_pair = pair_sums(["a_w_in"], last_whole, _pair_swap(last_whole, "late")) if pair_sums is not None else []
    dn0 = _mm_nt("gdn_in_dx", dpm, P["w_in_main"], F32, scatter=last_pair)
    dn0, last_parts = (dn0[0], dn0[1:]) if last_pair else (dn0, [])
    dn0 = _mm_nt("gdn_in_ba_dx", dpba, P["w_in_ba"], F32, res=dn0)
    dx, d_anorm = _rms_bwd("a_rms_bwd", x, [(P["a_norm_w"], dn0)], [dh1])

    nh = GDN_V_HEADS
    grads = dict(
        a_norm_w=d_anorm,
        a_w_in=g_win,
        a_conv_w=d_aconv[:4], a_a_log=d_alog[:, :nh], a_dt_bias=d_dtb[:, :nh], a_out_norm_w=d_gnorm,
        a_w_out=g_wout, kv_norm_w=d_kvnorm, w_kv=g_wkv, b_norm_w=d_bnorm, b_w_q=g_wq,
        b_sinks=dsinks[:, :, 0].reshape(1, SWA_Q_HEADS), b_w_o=g_wo, rel_bias_table=g_table,
        ffn_norm_w=jnp.concatenate([gf0["norm_w"], gf1["norm_w"]], axis=0),
        ffn_w_up=gf0["w_up"],
        ffn_conv_w=jnp.stack([gf0["conv_w"], gf1["conv_w"]], axis=0),
        ffn_conv_b=jnp.concatenate([gf0["conv_b"], gf1["conv_b"]], axis=0),
        ffn_w_down=gf0["w_down"],
        final_norm_w=d_final,
    )
    scattered = dict(zip([n for n in BIG if n in ready], zip(pairs, parts)))
    scattered.update(zip(["a_w_in"], zip(last_pair, last_parts)))
    return loss, dx, grads, scattered


HBM_SPEC = pl.BlockSpec(memory_space=pltpu.HBM)
VMEM_SPEC = pl.BlockSpec(memory_space=pltpu.VMEM)


def _coords():
    return lax.axis_index("x"), lax.axis_index("y"), lax.axis_index("c")


def _remote(src, dst, send_sem, recv_sem, device):
    return pltpu.make_async_remote_copy(src_ref=src, dst_ref=dst, send_sem=send_sem, recv_sem=recv_sem,
                                        device_id=device, device_id_type=MESH)


def _other_chips(x, y):
    return [(1 - x, y), (x, 1 - y), (1 - x, 1 - y)]


def _gather_copies(shapes, split, ins, outs, send_sems, recv_sems):
    x, y, c = _coords()
    p = 2 * x + y
    ici, forwards, from_sibling = [], [], []
    for a, shape in enumerate(shapes):
        h = shape[0] // 2
        for j, chip in enumerate(_other_chips(x, y)):
            q = 2 * chip[0] + chip[1]
            if split[a]:
                mine, theirs = pl.ds(c * h, h), pl.ds((1 - c) * h, h)
                ici.append(_remote(ins[a].at[mine], outs[a].at[p, mine], send_sems.at[6 * a + j],
                                   recv_sems.at[6 * a + j], (*chip, c)))
                land = outs[a].at[q, mine]
                forwards.append(_remote(land, land, send_sems.at[6 * a + 3 + j], recv_sems.at[6 * a + 3 + j],
                                        (x, y, 1 - c)))
                land = outs[a].at[q, theirs]
                from_sibling.append(_remote(land, land, send_sems.at[6 * a + 3 + j], recv_sems.at[6 * a + 3 + j],
                                            (x, y, 1 - c)))
            else:
                ici.append(_remote(ins[a], outs[a].at[p], send_sems.at[6 * a + j], recv_sems.at[6 * a + j],
                                   (*chip, c)))
                forwards.append(None)
    return ici, forwards, from_sibling


def _gather_arrival(shapes, split, ins, outs, send_sems, recv_sems):
    x, y, c = _coords()
    ici, forwards, from_sibling = _gather_copies(shapes, split, ins, outs, send_sems, recv_sems)
    k = 0
    for a, shape in enumerate(shapes):
        h = shape[0] // 2
        for j, chip in enumerate(_other_chips(x, y)):
            q = 2 * chip[0] + chip[1]
            land = outs[a].at[q, pl.ds(c * h, h)] if split[a] else outs[a].at[q]
            _remote(land, land, send_sems.at[6 * a + j], recv_sems.at[6 * a + j], (*chip, c)).wait_recv()
            if forwards[k] is not None:
                forwards[k].start()
            k += 1
    for cp in from_sibling:
        cp.wait_recv()
    for cp in ici + [f for f in forwards if f is not None]:
        cp.wait_send()


def _all_gather(arrs, split, remote):
    n = len(arrs)
    now = [a for a in range(n) if remote[a]]
    shapes = [arrs[a].shape for a in now]
    splits = [split[a] for a in now]

    def body(*refs):
        ins, outs, stage = refs[:n], refs[n:2 * n], refs[2 * n:3 * n]
        send_sems, recv_sems, in_sems, out_sems = refs[3 * n:]
        p = 2 * lax.axis_index("x") + lax.axis_index("y")
        gathered = ([ins[a] for a in now], [outs[a] for a in now], send_sems, recv_sems)
        loads = [pltpu.make_async_copy(ins[a], stage[a], in_sems.at[a]) for a in range(n)]
        for cp in loads:
            cp.start()
        for cp in _gather_copies(shapes, splits, *gathered)[0]:
            cp.start()
        stores = [pltpu.make_async_copy(stage[a], outs[a].at[p], out_sems.at[a]) for a in range(n)]
        for a in range(n):
            loads[a].wait()
            stores[a].start()
        _gather_arrival(shapes, splits, *gathered)
        for cp in stores:
            cp.wait()

    return pl.pallas_call(
        body, name="weights_all_gather", in_specs=[HBM_SPEC] * n, out_specs=[HBM_SPEC] * n,
        out_shape=[jax.ShapeDtypeStruct((N_CHIPS,) + a.shape, a.dtype) for a in arrs],
        scratch_shapes=[pltpu.VMEM(a.shape, a.dtype) for a in arrs]
        + [pltpu.SemaphoreType.DMA((6 * len(now),)), pltpu.SemaphoreType.DMA((6 * len(now),)),
           pltpu.SemaphoreType.DMA((n,)), pltpu.SemaphoreType.DMA((n,))],
        compiler_params=pltpu.CompilerParams(vmem_limit_bytes=VMEM_LIMIT),
    )(*arrs)


PAIR_SWAP_PIECES = 2


def _swap_copies(shapes, ins, other, send_sems, recv_sems):
    x, y, c = _coords()
    copies = []
    for a, shape in enumerate(shapes):
        h = shape[1] // 2
        piece = h // PAIR_SWAP_PIECES
        for q in range(N_CHIPS):
            for r in range(PAIR_SWAP_PIECES):
                k = (a * N_CHIPS + q) * PAIR_SWAP_PIECES + r
                copies.append(_remote(ins[a].at[q, pl.ds((1 - c) * h + r * piece, piece)],
                                      other[a].at[q, pl.ds(r * piece, piece)], send_sems.at[k], recv_sems.at[k],
                                      (x, y, 1 - c)))
    return copies


def _swap_out_shapes(gs):
    return [jax.ShapeDtypeStruct((N_CHIPS, g.shape[1] // 2, g.shape[2]), g.dtype) for g in gs]


def _pair_swap(gs, tag):
    n = len(gs)
    shapes = [g.shape for g in gs]

    def body(*refs):
        copies = _swap_copies(shapes, refs[:n], refs[n:2 * n], *refs[2 * n:])
        for cp in copies:
            cp.start()
        for cp in copies:
            cp.wait()

    nsem = n * N_CHIPS * PAIR_SWAP_PIECES
    return pl.pallas_call(
        body, name=f"grads_pair_swap_{tag}", in_specs=[HBM_SPEC] * n, out_specs=[HBM_SPEC] * n,
        out_shape=_swap_out_shapes(gs),
        scratch_shapes=[pltpu.SemaphoreType.DMA((nsem,)), pltpu.SemaphoreType.DMA((nsem,))],
    )(*gs)


def _scatter_copies(ins, outs, send_sems, recv_sems):
    x, y, c = _coords()
    copies = []
    for a in range(len(ins)):
        for j, chip in enumerate(_other_chips(x, y)):
            q = 2 * chip[0] + chip[1]
            copies.append(_remote(ins[a].at[q], outs[a].at[j], send_sems.at[3 * a + j], recv_sems.at[3 * a + j],
                                  (*chip, c)))
    return copies


def _scatter_shapes(ps):
    return [jax.ShapeDtypeStruct((N_CHIPS - 1,) + a.shape[1:], a.dtype) for a in ps]


def _pair_share(rs):
    n = len(rs)

    def body(*refs):
        ins, outs, stage = refs[:n], refs[n:2 * n], refs[2 * n:3 * n]
        send_sems, recv_sems, in_sems, out_sems = refs[3 * n:]
        x, y, c = _coords()

        def mine(a):
            h = rs[a].shape[0]
            return outs[a].at[pl.ds(c * h, h)]

        loads = [pltpu.make_async_copy(ins[a], stage[a], in_sems.at[a]) for a in range(n)]
        for cp in loads:
            cp.start()
        sends = [_remote(ins[a], mine(a), send_sems.at[a], recv_sems.at[a], (x, y, 1 - c)) for a in range(n)]
        for cp in sends:
            cp.start()
        stores = [pltpu.make_async_copy(stage[a], mine(a), out_sems.at[a]) for a in range(n)]
        for a in range(n):
            loads[a].wait()
            stores[a].start()
        for a in range(n):
            h = rs[a].shape[0]
            land = outs[a].at[pl.ds((1 - c) * h, h)]
            _remote(land, land, send_sems.at[a], recv_sems.at[a], (x, y, 1 - c)).wait_recv()
        for cp in sends:
            cp.wait_send()
        for cp in stores:
            cp.wait()

    return pl.pallas_call(
        body, name="grads_pair_share", in_specs=[HBM_SPEC] * n, out_specs=[HBM_SPEC] * n,
        out_shape=[jax.ShapeDtypeStruct((2 * a.shape[0], a.shape[1]), a.dtype) for a in rs],
        scratch_shapes=[pltpu.VMEM(a.shape, a.dtype) for a in rs] + [pltpu.SemaphoreType.DMA((n,))] * 4,
        compiler_params=pltpu.CompilerParams(vmem_limit_bytes=VMEM_LIMIT),
    )(*rs)


def _small_all_reduce(buf):
    R = buf.shape[0]
    ndev = 2 * N_CHIPS

    def body(in_ref, out_ref, gath, send_sems, recv_sems):
        x, y, c = _coords()
        me = 4 * x + 2 * y + c
        gath[me] = in_ref[...]
        peers = []
        for d in range(1, ndev):
            px = 1 - x if d & 4 else x
            py = 1 - y if d & 2 else y
            pc = 1 - c if d & 1 else c
            peers.append((px, py, pc))
        sends = []
        for d, peer in enumerate(peers):
            cp = _remote(in_ref, gath.at[me], send_sems.at[d], recv_sems.at[d], peer)
            cp.start()
            sends.append(cp)
        for d, peer in enumerate(peers):
            land = gath.at[4 * peer[0] + 2 * peer[1] + peer[2]]
            _remote(land, land, send_sems.at[d], recv_sems.at[d], peer).wait_recv()
        for cp in sends:
            cp.wait_send()
        acc = gath[0]
        for s in range(1, ndev):
            acc = acc + gath[s]
        out_ref[...] = acc

    return pl.pallas_call(
        body, name="small_all_reduce", in_specs=[VMEM_SPEC], out_specs=VMEM_SPEC,
        out_shape=jax.ShapeDtypeStruct(buf.shape, F32),
        scratch_shapes=[pltpu.VMEM((ndev, R, LANE), F32), pltpu.SemaphoreType.DMA((ndev - 1,)),
                        pltpu.SemaphoreType.DMA((ndev - 1,))],
    )(buf)


def _pair_add(name, own, other):
    h = own.shape[1]
    tm = _tile(h, (128, 64, 32, 16))

    def fn(i, a, b):
        return (a + b,)

    return _rowcall(name, fn, h, tm, [(own, "row", None), (other, "row", None)], [(own.shape, BF16, "row")])[0]


def _chip_add(name, own, parts):
    h = parts.shape[1]
    tm = _tile(h, (128, 64, 32, 16))

    def fn(i, o, a):
        a = a.astype(F32)
        return (((o.astype(F32) + a[0]) + a[1]) + a[2],)

    return _rowcall(name, fn, h, tm, [(own, "row", None), (parts, "row", None)], [(parts.shape[1:], F32, "row")])[0]


def _adamw(name, w, g, m, v):
    R = w.shape[0]
    tm = _tile(R, (256, 128, 64, 32, 16, 8))

    def fn(i, wv, gv, mv, vv):
        m2 = ADAM_B1 * mv + (1.0 - ADAM_B1) * gv
        v2 = ADAM_B2 * vv + (1.0 - ADAM_B2) * (gv * gv)
        m_hat = m2 / (1.0 - ADAM_B1 ** ADAM_STEP)
        v_hat = v2 / (1.0 - ADAM_B2 ** ADAM_STEP)
        delta = -ADAM_LR * (m_hat / (jnp.sqrt(v_hat) + ADAM_EPS) + ADAM_WD * wv)
        return delta, m2, v2

    ins = [(a, "row", None) for a in (w, g, m, v)]
    return _rowcall(name, fn, R, tm, ins, [(w.shape, F32, "row")] * 3)


def _pack(arrs):
    flat = jnp.concatenate([a.reshape(-1).astype(F32) for a in arrs])
    size = flat.shape[0]
    padded = -(-size // (SUBLANE * LANE)) * SUBLANE * LANE
    return jnp.pad(flat, (0, padded - size)).reshape(-1, LANE)


def _unpack(buf, shapes):
    flat = buf.reshape(-1)
    out, off = [], 0
    for s in shapes:
        size = math.prod(s)
        out.append(flat[off:off + size].reshape(s))
        off += size
    return out


BIG = ("a_w_in", "a_w_out", "w_kv", "b_w_q", "b_w_o", "ffn_w_up", "ffn_w_down")
WEIGHTS = ("a_norm_w", "a_w_in", "a_conv_w", "a_a_log", "a_dt_bias", "a_out_norm_w", "a_w_out", "kv_norm_w", "w_kv",
           "b_norm_w", "b_w_q", "b_sinks", "b_w_o", "rel_bias_table", "ffn_norm_w", "ffn_w_up", "ffn_conv_w",
           "ffn_conv_b", "ffn_w_down", "final_norm_w")
SMALL = tuple(n for n in WEIGHTS if n not in BIG)
SMALL_SHARDED = {"a_norm_w": 1, "a_conv_w": 2, "ffn_conv_w": 2}


def _quarter_2d(name, a):
    if name in ("ffn_w_up", "ffn_w_down"):
        return a.reshape(a.shape[0] * a.shape[1], a.shape[2])
    return a.reshape(a.shape[-2], a.shape[-1])


def _whole_weights(w):
    bigs = [_quarter_2d(n, w[n]).astype(BF16) for n in BIG]
    smalls = [w["a_norm_w"], w["a_conv_w"][0], w["ffn_conv_w"].reshape(6, DFF2_SHARD)]
    remote = [True] + [False] * (len(bigs) - 1) + [True] * len(smalls)
    g = _all_gather(bigs + smalls, [True] * len(bigs) + [False] * len(smalls), remote)
    w_in = g[0].transpose(1, 0, 2).reshape(D, GDN_IN)
    nh = GDN_V_HEADS
    zpad = jnp.zeros((D, LANE - nh), BF16)
    w_in_ba = jnp.concatenate([w_in[:, GDN_MAIN:GDN_MAIN + nh], zpad, w_in[:, GDN_MAIN + nh:], zpad], axis=1)
    lane_pad = lambda a: jnp.pad(a, ((0, 0), (0, LANE - nh)))
    early = dict(
        a_norm_w=g[7].reshape(1, D), w_in_main=w_in[:, :GDN_MAIN], w_in_ba=w_in_ba,
        a_conv_w=g[8].transpose(1, 0, 2).reshape(4, GDN_CONV), a_log=lane_pad(w["a_a_log"]),
        dt_bias=lane_pad(w["a_dt_bias"]), a_out_norm_w=w["a_out_norm_w"],
        kv_norm_w=w["kv_norm_w"].reshape(1, D), b_norm_w=w["b_norm_w"],
        sinks=jnp.broadcast_to(w["b_sinks"].reshape(SWA_KV_HEADS, SWA_GROUP, 1), (SWA_KV_HEADS, SWA_GROUP, LANE)),
        rel_table_t=w["rel_bias_table"].T, ffn_norm_w=w["ffn_norm_w"],
        ffn_conv_w=g[9].reshape(N_CHIPS, 2, 3, DFF2_SHARD).transpose(1, 2, 0, 3).reshape(2, 3, DFF2),
        ffn_conv_b=w["ffn_conv_b"], final_norm_w=w["final_norm_w"].reshape(1, D),
    )
    return early, (bigs[1:], g[1:len(bigs)])


def _late_weights(g):
    return dict(
        w_out=g[0].reshape(GDN_V, D), w_kv=g[1].reshape(D, 2 * SWA_KV_HEADS * SWA_HD), w_q=g[2].reshape(D, D),
        w_o=g[3].reshape(D, D), w_up=g[4].reshape(N_CHIPS, 2, D, DFF2_SHARD),
        w_down=g[5].reshape(N_CHIPS, 2, DFF_SHARD, D).transpose(1, 0, 2, 3).reshape(2, DFF, D),
    )


def _chip_major(name, g):
    if name == "a_w_in":
        return g.reshape(D, N_CHIPS, GDN_IN_SHARD).transpose(1, 0, 2)
    if name == "ffn_w_up":
        return g.reshape(N_CHIPS, 2 * D, DFF2_SHARD)
    if name == "ffn_w_down":
        return g.reshape(N_CHIPS, 2 * DFF_SHARD, D)
    return g.reshape(N_CHIPS, g.shape[0] // N_CHIPS, g.shape[1])


def kernel(x, a_norm_w, a_w_in, a_conv_w, a_a_log, a_dt_bias, a_out_norm_w, a_w_out, kv_norm_w, w_kv, b_norm_w, b_w_q, b_sinks, b_w_o, rel_bias_table, ffn_norm_w, ffn_w_up, ffn_conv_w, ffn_conv_b, ffn_w_down, final_norm_w, loss_target, m_a_norm_w, m_a_w_in, m_a_conv_w, m_a_a_log, m_a_dt_bias, m_a_out_norm_w, m_a_w_out, m_kv_norm_w, m_w_kv, m_b_norm_w, m_b_w_q, m_b_sinks, m_b_w_o, m_rel_bias_table, m_ffn_norm_w, m_ffn_w_up, m_ffn_conv_w, m_ffn_conv_b, m_ffn_w_down, m_final_norm_w, v_a_norm_w, v_a_w_in, v_a_conv_w, v_a_a_log, v_a_dt_bias, v_a_out_norm_w, v_a_w_out, v_kv_norm_w, v_w_kv, v_b_norm_w, v_b_w_q, v_b_sinks, v_b_w_o, v_rel_bias_table, v_ffn_norm_w, v_ffn_w_up, v_ffn_conv_w, v_ffn_conv_b, v_ffn_w_down, v_final_norm_w):
    w = dict(zip(WEIGHTS, (a_norm_w, a_w_in, a_conv_w, a_a_log, a_dt_bias, a_out_norm_w, a_w_out, kv_norm_w, w_kv,
                           b_norm_w, b_w_q, b_sinks, b_w_o, rel_bias_table, ffn_norm_w, ffn_w_up, ffn_conv_w,
                           ffn_conv_b, ffn_w_down, final_norm_w)))
    m = dict(zip(WEIGHTS, (m_a_norm_w, m_a_w_in, m_a_conv_w, m_a_a_log, m_a_dt_bias, m_a_out_norm_w, m_a_w_out,
                           m_kv_norm_w, m_w_kv, m_b_norm_w, m_b_w_q, m_b_sinks, m_b_w_o, m_rel_bias_table,
                           m_ffn_norm_w, m_ffn_w_up, m_ffn_conv_w, m_ffn_conv_b, m_ffn_w_down, m_final_norm_w)))
    v = dict(zip(WEIGHTS, (v_a_norm_w, v_a_w_in, v_a_conv_w, v_a_a_log, v_a_dt_bias, v_a_out_norm_w, v_a_w_out,
                           v_kv_norm_w, v_w_kv, v_b_norm_w, v_b_w_q, v_b_sinks, v_b_w_o, v_rel_bias_table,
                           v_ffn_norm_w, v_ffn_w_up, v_ffn_conv_w, v_ffn_conv_b, v_ffn_w_down, v_final_norm_w)))
    T = x.shape[1]
    chip = 2 * lax.axis_index("x") + lax.axis_index("y")

    core = lax.axis_index("c")

    def pair_sums(names, whole, other):
        own = [lax.dynamic_slice_in_dim(g, core * (g.shape[1] // 2), g.shape[1] // 2, 1) for g in whole]
        return [_pair_add(f"pair_add_{n}", a, b) for n, a, b in zip(names, own, other)]

    early, late = _whole_weights(w)
    loss_part, dx, grads, scattered = _local_step(x.reshape(T, D), loss_target.reshape(T, D), early, late, pair_sums)

    assert all(n in scattered for n in BIG)
    halves = [_chip_add(f"chip_add_{n}", lax.dynamic_index_in_dim(scattered[n][0], chip, 0, keepdims=False),
                        scattered[n][1]) for n in BIG]
    quarter = _pair_share(halves)
    out_g, out_d, out_m, out_v = {}, {}, {}, {}
    for n, g2 in zip(BIG, quarter):
        res = _adamw(f"adamw_{n}", _quarter_2d(n, w[n]), g2, _quarter_2d(n, m[n]), _quarter_2d(n, v[n]))
        out_g[n] = g2.reshape(w[n].shape)
        out_d[n], out_m[n], out_v[n] = (r.reshape(w[n].shape) for r in res)

    whole = [grads[n] for n in SMALL]
    summed = _unpack(_small_all_reduce(_pack([loss_part[0:1, 0:1]] + whole)), [(1, 1)] + [a.shape for a in whole])
    loss = summed[0].reshape(())
    small_g = []
    for n, g in zip(SMALL, summed[1:]):
        if n in SMALL_SHARDED:
            axis = SMALL_SHARDED[n]
            g = g.reshape(w[n].shape[:axis] + (-1,) + w[n].shape[axis + 1:])
            size = w[n].shape[axis]
            g = lax.dynamic_slice_in_dim(g, chip * size, size, axis)
        small_g.append(g.reshape(w[n].shape))
    shapes = [w[n].shape for n in SMALL]
    res = _adamw("adamw_small", _pack([w[n] for n in SMALL]), _pack(small_g), _pack([m[n] for n in SMALL]),
                 _pack([v[n] for n in SMALL]))
    small_d, small_m, small_v = (_unpack(r, shapes) for r in res)
    for i, n in enumerate(SMALL):
        out_g[n], out_d[n], out_m[n], out_v[n] = small_g[i], small_d[i], small_m[i], small_v[i]

    return (loss, dx.reshape(x.shape), *[out_g[n] for n in WEIGHTS], *[out_d[n] for n in WEIGHTS],
            *[out_m[n] for n in WEIGHTS], *[out_v[n] for n in WEIGHTS])
```

```python
import functools
import math

import jax
import jax.numpy as jnp
from jax import lax
from jax.experimental import pallas as pl
from jax.experimental.pallas import tpu as pltpu

F32 = jnp.float32
BF16 = jnp.bfloat16
MESH = pl.DeviceIdType.MESH
HIGHEST = lax.Precision.HIGHEST

D = 1024
EPS = 1e-6
NEG_INF = -1e30
N_CHIPS = 4

GDN_QK_HEADS = 8
GDN_V_HEADS = 16
GDN_HD = 128
GDN_QK = GDN_QK_HEADS * GDN_HD
GDN_V = GDN_V_HEADS * GDN_HD
GDN_CONV = 2 * GDN_QK + GDN_V
GDN_MAIN = GDN_CONV + GDN_V
GDN_IN = GDN_MAIN + 2 * GDN_V_HEADS
GDN_IN_SHARD = GDN_IN // N_CHIPS
GDN_CHUNK = 64

SWA_Q_HEADS = 16
SWA_KV_HEADS = 4
SWA_GROUP = 4
SWA_HD = 64
SWA_BLOCK = 128
REL_BUCKETS = 32
REL_MAX_DISTANCE = 128

DFF = 2816
DFF2 = 2 * DFF
DFF2_SHARD = DFF2 // N_CHIPS
DFF_SHARD = DFF // N_CHIPS

ADAM_LR = 0.001
ADAM_B1 = 0.9
ADAM_B2 = 0.999
ADAM_EPS = 1e-08
ADAM_WD = 0.01
ADAM_STEP = 10

LANE = 128
SUBLANE = 8
VMEM_LIMIT = 56 * 1024 * 1024


def _params(sem, vmem=VMEM_LIMIT):
    return pltpu.CompilerParams(dimension_semantics=sem, vmem_limit_bytes=vmem)


def _rowcall(name, fn, T, tm, ins, outs, swap=()):
    n = T // tm
    nswap = len(swap)
    swap_shapes = [g.shape for g in swap]
    r8 = tm // SUBLANE
    last8 = T // SUBLANE - 1
    arrays, in_specs = [], []
    for arr, kind, cols in ins:
        arrays.append(arr)
        if kind == "full":
            in_specs.append(pl.BlockSpec(arr.shape, functools.partial(lambda nd, i: (0,) * nd, arr.ndim)))
        elif arr.ndim == 2:
            w, ci = cols if cols is not None else (arr.shape[1], 0)
            if kind == "row":
                in_specs.append(pl.BlockSpec((tm, w), functools.partial(lambda ci, i: (i, ci), ci)))
            elif kind == "prev":
                in_specs.append(pl.BlockSpec(
                    (SUBLANE, w), functools.partial(lambda ci, i: (jnp.maximum(i * r8 - 1, 0), ci), ci)))
            else:
                in_specs.append(pl.BlockSpec(
                    (SUBLANE, w), functools.partial(lambda ci, i: (jnp.minimum((i + 1) * r8, last8), ci), ci)))
        else:
            lead = arr.shape[:-2]
            in_specs.append(pl.BlockSpec(lead + (tm, arr.shape[-1]),
                                         functools.partial(lambda nl, i: (0,) * nl + (i, 0), len(lead))))
    out_shape, out_specs = [], []
    for shape, dtype, kind in outs:
        out_shape.append(jax.ShapeDtypeStruct(shape, dtype))
        if kind == "acc":
            out_specs.append(pl.BlockSpec(shape, functools.partial(lambda nd, i: (0,) * nd, len(shape))))
        else:
            lead = shape[:-2]
            out_specs.append(pl.BlockSpec(lead + (tm, shape[-1]),
                                          functools.partial(lambda nl, i: (0,) * nl + (i, 0), len(lead))))
    nin = len(arrays)

    nout = len(outs)

    def body(*refs):
        i = pl.program_id(0)
        if nswap:
            comm = (swap_shapes, refs[nin:nin + nswap], refs[nin + nswap + nout:nin + 2 * nswap + nout], refs[-2],
                    refs[-1])

            @pl.when(i == 0)
            def _():
                for cp in _swap_copies(*comm):
                    cp.start()

            @pl.when(i == n - 1)
            def _():
                for cp in _swap_copies(*comm):
                    cp.wait()

        vals = [r[...] for r in refs[:nin]]
        res = fn(i, *vals)
        for (shape, dtype, kind), o, r in zip(outs, refs[nin + nswap:], res):
            if kind == "row":
                o[...] = r.astype(dtype)
            else:
                @pl.when(i == 0)
                def _():
                    o[...] = r.astype(dtype)

                @pl.when(i > 0)
                def _():
                    o[...] += r.astype(dtype)

    anywhere = pl.BlockSpec(memory_space=pl.ANY)
    nsem = nswap * N_CHIPS * PAIR_SWAP_PIECES
    return pl.pallas_call(
        body, name=name, grid=(n,), in_specs=in_specs + [anywhere] * nswap, out_specs=out_specs + [anywhere] * nswap,
        out_shape=out_shape + _swap_out_shapes(swap),
        scratch_shapes=[pltpu.SemaphoreType.DMA((nsem,)), pltpu.SemaphoreType.DMA((nsem,))] if nswap else [],
        compiler_params=_params(("arbitrary",)),
    )(*arrays, *swap)


def _mm(name, a, b, out_shape, out_dtype, grid, a_spec, b_spec, o_spec, dims, acc_shape, res=None, precision=None,
        into=None, scatter=(), post=None):
    nk = grid[2]
    ns = len(scatter)
    a, norm_w = a if isinstance(a, tuple) else (a, None)
    normed = norm_w is not None
    posted = post is not None
    n_in = 2 + normed + (res is not None) + (into is not None) + 3 * posted + ns
    n_out = 1 + posted + ns

    def body(*refs):
        a_ref, b_ref, o_ref = refs[0], refs[1], refs[n_in]
        r_ref = refs[2 + normed] if res is not None else None
        first = pl.program_id(0) == 0
        if ns:
            comm = (refs[n_in - ns:n_in], refs[n_in + n_out - ns:n_in + n_out], refs[-2], refs[-1])
            steps = [pl.program_id(d) for d in range(3)]

            @pl.when((steps[0] == 0) & (steps[1] == 0) & (steps[2] == 0))
            def _():
                for cp in _scatter_copies(*comm):
                    cp.start()

            @pl.when((steps[0] == grid[0] - 1) & (steps[1] == grid[1] - 1) & (steps[2] == grid[2] - 1))
            def _():
                copies = _scatter_copies(*comm)
                for cp in copies:
                    cp.wait_recv()
                for cp in copies:
                    cp.wait_send()

        av, bv = a_ref[...], b_ref[...]
        if normed:
            av = _rms_core(av, refs[2][...])
        if precision is None:
            av, bv = av.astype(BF16), bv.astype(BF16)
        p = lax.dot_general(av, bv, (dims, ((), ())), preferred_element_type=F32, precision=precision)

        def finish(x):
            if res is not None:
                x = x + r_ref[...].astype(F32)
            if posted:
                h_ref, w_ref, add_ref = refs[n_in - ns - 3:n_in - ns]
                dh, dw = jax.vjp(_rms_core, h_ref[...], w_ref[...])[1](x)
                x = dh + add_ref[...]
                dw_ref = refs[n_in + 1]

                @pl.when(first)
                def _():
                    dw_ref[...] = dw

                @pl.when(jnp.logical_not(first))
                def _():
                    dw_ref[...] += dw

            o_ref[...] = x.astype(out_dtype).reshape(o_ref.shape)

        if nk == 1:
            finish(p)
        else:
            acc = refs[n_in + n_out]
            k = pl.program_id(2)

            @pl.when(k == 0)
            def _():
                acc[...] = p

            @pl.when(k > 0)
            def _():
                acc[...] += p

            @pl.when(k == nk - 1)
            def _():
                finish(acc[...])

    anywhere = pl.BlockSpec(memory_space=pl.ANY)
    ops = [a, b] + ([norm_w] if normed else []) + ([res] if res is not None else [])
    ops += ([into] if into is not None else []) + (list(post) if posted else []) + list(scatter)
    whole = lambda arr: pl.BlockSpec(arr.shape, lambda i, j, k: (0, 0))
    specs = [a_spec, b_spec] + ([whole(norm_w)] if normed else [])
    specs += [o_spec] if res is not None else []
    specs += [anywhere] if into is not None else []
    specs += ([o_spec, whole(post[1]), o_spec] if posted else []) + [anywhere] * ns
    out = pl.pallas_call(
        body, name=name, grid=grid, in_specs=specs,
        out_specs=[o_spec] + ([whole(post[1])] if posted else []) + [anywhere] * ns,
        out_shape=[jax.ShapeDtypeStruct(out_shape, out_dtype)]
        + ([jax.ShapeDtypeStruct(post[1].shape, F32)] if posted else []) + _scatter_shapes(scatter),
        input_output_aliases={2 + normed + (res is not None): 0} if into is not None else {},
        scratch_shapes=([pltpu.VMEM(acc_shape, F32)] if nk > 1 else [])
        + ([pltpu.SemaphoreType.DMA((3 * ns,)), pltpu.SemaphoreType.DMA((3 * ns,))] if ns else []),
        compiler_params=_params(("arbitrary",) * 3 if ns or posted else ("parallel", "parallel", "arbitrary")),
    )(*ops)
    return out if n_out > 1 else out[0]


NN = ((1,), (0,))
NT = ((1,), (1,))
TN = ((0,), (0,))


BIG_TILES = (1024, 512, 256, 128)


def _tile(n, pref):
    for t in pref:
        if n % t == 0:
            return t
    return n


def _rows_of(a):
    return a[0] if isinstance(a, tuple) else a


def _mm_nn(name, a, w, out_dtype, res=None, precision=None):
    M, K = _rows_of(a).shape
    N = w.shape[1]
    tm = _tile(M, BIG_TILES if K <= 2048 else BIG_TILES[1:])
    tn = _tile(N, BIG_TILES)
    return _mm(name, a, w, (M, N), out_dtype, (M // tm, N // tn, 1),
               pl.BlockSpec((tm, K), lambda i, j, k: (i, 0)), pl.BlockSpec((K, tn), lambda i, j, k: (0, j)),
               pl.BlockSpec((tm, tn), lambda i, j, k: (i, j)), NN, (tm, tn), res=res, precision=precision)


def _mm_nt(name, g, w, out_dtype, res=None, precision=None, scatter=(), post=None):
    M, N = g.shape
    K = w.shape[0]
    tm, tk = _tile(M, BIG_TILES if post is None else BIG_TILES[1:]), _tile(K, (1024, 1408, 512, 256, 128))
    tn = _tile(N, (1536,) + BIG_TILES)
    return _mm(name, g, w, (M, K), out_dtype, (M // tm, K // tk, N // tn),
               pl.BlockSpec((tm, tn), lambda i, j, k: (i, k)), pl.BlockSpec((tk, tn), lambda i, j, k: (j, k)),
               pl.BlockSpec((tm, tk), lambda i, j, k: (i, j)), NT, (tm, tk), res=res, precision=precision,
               scatter=scatter, post=post)


def _mm_tn(name, a, g, out_dtype=F32, precision=None):
    T, K = _rows_of(a).shape
    N = g.shape[1]
    tk, tn = _tile(K, (1024, 1408, 512, 256, 128)), _tile(N, BIG_TILES)
    assert tk == K or not isinstance(a, tuple)
    tt = _tile(T, BIG_TILES)
    return _mm(name, a, g, (K, N), out_dtype, (K // tk, N // tn, T // tt),
               pl.BlockSpec((tt, tk), lambda i, j, k: (k, i)), pl.BlockSpec((tt, tn), lambda i, j, k: (k, j)),
               pl.BlockSpec((tk, tn), lambda i, j, k: (i, j)), TN, (tk, tn), precision=precision)


def _mm_up(name, n, wup, layer):
    T = _rows_of(n).shape[0]
    tm = _tile(T, BIG_TILES)
    return _mm(name, n, wup, (T, DFF2), BF16, (T // tm, N_CHIPS, 1),
               pl.BlockSpec((tm, D), lambda i, j, k: (i, 0)),
               pl.BlockSpec((None, None, D, DFF2_SHARD), lambda i, j, k: (j, layer, 0, 0)),
               pl.BlockSpec((tm, DFF2_SHARD), lambda i, j, k: (i, j)), NN, (tm, DFF2_SHARD))


def _mm_up_nt(name, du, wup, layer, post):
    T = du.shape[0]
    tm, tk = _tile(T, BIG_TILES[1:]), D
    return _mm(name, du, wup, (T, D), F32, (T // tm, D // tk, N_CHIPS),
               pl.BlockSpec((tm, DFF2_SHARD), lambda i, j, k: (i, k)),
               pl.BlockSpec((None, None, tk, DFF2_SHARD), lambda i, j, k: (k, layer, j, 0)),
               pl.BlockSpec((tm, tk), lambda i, j, k: (i, j)), NT, (tm, tk), post=post)


def _mm_up_tn(name, n, du, layer, into):
    T = _rows_of(n).shape[0]
    tk, tt = D, _tile(T, BIG_TILES)
    return _mm(name, n, du, (N_CHIPS, 2, D, DFF2_SHARD), F32, (D // tk, N_CHIPS, T // tt),
               pl.BlockSpec((tt, tk), lambda i, j, k: (k, i)), pl.BlockSpec((tt, DFF2_SHARD), lambda i, j, k: (k, j)),
               pl.BlockSpec((None, None, tk, DFF2_SHARD), lambda i, j, k: (j, layer, i, 0)), TN, (tk, DFF2_SHARD),
               into=into)


def _mm_down_tn(name, act, dout, layer, into):
    T = act.shape[0]
    tk, tn, tt = 2 * DFF_SHARD, _tile(D, BIG_TILES), _tile(T, BIG_TILES)
    return _mm(name, act, dout, (2, 2, 2, DFF_SHARD, D), F32, (DFF // tk, D // tn, T // tt),
               pl.BlockSpec((tt, tk), lambda i, j, k: (k, i)), pl.BlockSpec((tt, tn), lambda i, j, k: (k, j)),
               pl.BlockSpec((None, 2, None, DFF_SHARD, tn), lambda i, j, k: (i, 0, layer, 0, j)), TN, (tk, tn),
               into=into)


def _sigmoid(x):
    return 0.5 * jnp.tanh(0.5 * x) + 0.5


def _silu(x):
    return x * _sigmoid(x)


def _softplus(x):
    return jnp.maximum(x, 0.0) + jnp.log(1.0 + jnp.exp(-jnp.abs(x)))


def _rms_core(h, w):
    return h * lax.rsqrt(jnp.mean(h * h, axis=-1, keepdims=True) + EPS) * w


def _shift_down(x, halo, s, i):
    if s == 0:
        return x
    tm = x.shape[0]
    rolled = pltpu.roll(x, s, 0)
    patch = pltpu.roll(jnp.where(i == 0, 0.0, halo), s, 0)
    row = lax.broadcasted_iota(jnp.int32, patch.shape, 0)
    top = jnp.where(row < s, patch, rolled[:SUBLANE])
    return jnp.concatenate([top, rolled[SUBLANE:]], axis=0) if tm > SUBLANE else top


def _shift_up(x, halo, s, i, n):
    if s == 0:
        return x
    tm = x.shape[0]
    rolled = pltpu.roll(x, tm - s, 0)
    patch = pltpu.roll(jnp.where(i == n - 1, 0.0, halo), SUBLANE - s, 0)
    row = lax.broadcasted_iota(jnp.int32, patch.shape, 0)
    bottom = jnp.where(row >= SUBLANE - s, patch, rolled[tm - SUBLANE:])
    return jnp.concatenate([rolled[:tm - SUBLANE], bottom], axis=0) if tm > SUBLANE else bottom


def _taps(x, halo, K, i):
    return [_shift_down(x, halo, K - 1 - j, i) for j in range(K)]


def _conv_fwd(taps, w):
    y = w[0:1, :] * taps[0]
    for j in range(1, len(taps)):
        y = y + w[j:j + 1, :] * taps[j]
    return y


def _conv_dx(dy, halo_next, w, i, n):
    K = w.shape[0]
    dx = w[K - 1:K, :] * dy
    for j in range(K - 1):
        dx = dx + w[j:j + 1, :] * _shift_up(dy, halo_next, K - 1 - j, i, n)
    return dx


def _conv_dw(dy, taps):
    rows = [jnp.sum(dy * tap, axis=0, keepdims=True) for tap in taps]
    return jnp.concatenate(rows + [jnp.zeros((SUBLANE - len(taps), dy.shape[1]), F32)], axis=0)


def _rms_bwd(name, h, pairs, adds, tm=256):
    T = h.shape[0]
    tm = min(tm, T)
    npair, nadd = len(pairs), len(adds)

    def fn(i, hv, *rest):
        ws, dns, ads = rest[:npair], rest[npair:2 * npair], rest[2 * npair:]
        dh = None
        dws = []
        for wv, dn in zip(ws, dns):
            _, vjp = jax.vjp(_rms_core, hv, wv)
            dhi, dwi = vjp(dn.astype(F32))
            dh = dhi if dh is None else dh + dhi
            dws.append(dwi)
        for a in ads:
            dh = dh + a.astype(F32)
        return (dh, *dws)

    ins = [(h, "row", None)] + [(w, "full", None) for w, _ in pairs] + [(dn, "row", None) for _, dn in pairs]
    ins += [(a, "row", None) for a in adds]
    outs = [((T, D), F32, "row")] + [((1, D), F32, "acc")] * npair
    return _rowcall(name, fn, T, tm, ins, outs)


def _l2(x):
    return x * lax.rsqrt(jnp.sum(x * x, axis=-1, keepdims=True) + EPS)


def _gdn_post_core(yq, yk, yv, pb, pa, a_log, dtb):
    qn = tuple(_l2(_silu(a)) * (GDN_HD ** -0.5) for a in yq)
    kn = tuple(_l2(_silu(a)) for a in yk)
    v = _silu(yv)
    beta = _sigmoid(pb)
    g = -jnp.exp(a_log) * _softplus(pa + dtb)
    return qn, kn, v, beta, g


def _heads(x, n):
    return tuple(x[:, GDN_HD * h:GDN_HD * (h + 1)] for h in range(n))


def _gdn_pre_fwd(pm, pba, conv_w, a_log, dtb, tm=128):
    T = pm.shape[0]
    tm = min(tm, T)

    def fn(i, x, halo, pbav, cw, al, db):
        y = _conv_fwd(_taps(x.astype(F32), halo.astype(F32), 4, i), cw)
        qn, kn, v, beta, g = _gdn_post_core(_heads(y[:, :GDN_QK], 8), _heads(y[:, GDN_QK:2 * GDN_QK], 8),
                                            y[:, 2 * GDN_QK:], pbav[:, :LANE], pbav[:, LANE:], al, db)
        return jnp.stack(qn), jnp.stack(kn), jnp.stack(_heads(v, GDN_V_HEADS)), beta, g

    ins = [(pm, "row", (GDN_CONV, 0)), (pm, "prev", (GDN_CONV, 0)), (pba, "row", None),
           (conv_w, "full", None), (a_log, "full", None), (dtb, "full", None)]
    outs = [((GDN_QK_HEADS, T, GDN_HD), BF16, "row"), ((GDN_QK_HEADS, T, GDN_HD), BF16, "row"),
            ((GDN_V_HEADS, T, GDN_HD), BF16, "row"), ((T, LANE), F32, "row"), ((T, LANE), F32, "row")]
    return _rowcall("gdn_pre_fwd", fn, T, tm, ins, outs)


def _gdn_pre_bwd(pm, pba, conv_w, a_log, dtb, dqn, dkn, dv, dbeta, dg, tm=128):
    T = pm.shape[0]
    tm = min(tm, T)

    def fn(i, x, halo, pbav, cw, al, db, dqv, dkv, dvv, dbv, dgv):
        taps = _taps(x.astype(F32), halo.astype(F32), 4, i)
        y = _conv_fwd(taps, cw)
        prim = (_heads(y[:, :GDN_QK], 8), _heads(y[:, GDN_QK:2 * GDN_QK], 8), y[:, 2 * GDN_QK:],
                pbav[:, :LANE], pbav[:, LANE:], al, db)
        _, vjp = jax.vjp(_gdn_post_core, *prim)
        cot = (tuple(dqv[h].astype(F32) for h in range(8)), tuple(dkv[h].astype(F32) for h in range(8)),
               jnp.concatenate([dvv[h].astype(F32) for h in range(GDN_V_HEADS)], axis=1), dbv, dgv)
        dyq, dyk, dyv, dpb, dpa, dal, ddb = vjp(cot)
        dy = jnp.concatenate(list(dyq) + list(dyk) + [dyv], axis=1)
        dcw = _conv_dw(dy, taps)
        return dy, jnp.concatenate([dpb, dpa], axis=1), dcw, dal, ddb

    ins = [(pm, "row", (GDN_CONV, 0)), (pm, "prev", (GDN_CONV, 0)), (pba, "row", None),
           (conv_w, "full", None), (a_log, "full", None), (dtb, "full", None),
           (dqn, "row", None), (dkn, "row", None), (dv, "row", None), (dbeta, "row", None), (dg, "row", None)]
    outs = [((T, GDN_CONV), BF16, "row"), ((T, 2 * LANE), F32, "row"), ((SUBLANE, GDN_CONV), F32, "acc"),
            ((1, LANE), F32, "acc"), ((1, LANE), F32, "acc")]
    return _rowcall("gdn_pre_bwd", fn, T, tm, ins, outs)


def _gdn_conv_bwd(dy, dz, conv_w, tm=256):
    T = dy.shape[0]
    tm = min(tm, T)
    n = T // tm

    def fn(i, dyv, halo, dzv, cw):
        dx = _conv_dx(dyv.astype(F32), halo.astype(F32), cw, i, n)
        return (jnp.concatenate([dx.astype(BF16), dzv.astype(BF16)], axis=1),)

    ins = [(dy, "row", None), (dy, "next", None), (dz, "row", None), (conv_w, "full", None)]
    return _rowcall("gdn_conv_bwd", fn, T, tm, ins, [((T, GDN_MAIN), BF16, "row")])[0]


def _bdot(a, b, dims=NN):
    return lax.dot_general(a.astype(BF16), b.astype(BF16), (dims, ((), ())), preferred_element_type=F32)


BNN = ((2,), (1,))
BNT = ((2,), (2,))
BTN = ((1,), (1,))


def _bmm(a, b, dims=BNN):
    return lax.dot_general(a.astype(BF16), b.astype(BF16), (dims, ((0,), (0,))), preferred_element_type=F32)


def _bmm3(a, b):
    ah, bh = a.astype(BF16), b.astype(BF16)
    al, bl = (a - ah.astype(F32)).astype(BF16), (b - bh.astype(F32)).astype(BF16)
    dn = (BNN, ((0,), (0,)))
    return (lax.dot_general(ah, bh, dn, preferred_element_type=F32)
            + lax.dot_general(al, bh, dn, preferred_element_type=F32)
            + lax.dot_general(ah, bl, dn, preferred_element_type=F32))


def _tri_inv(m):
    C = m.shape[-1]
    r = lax.broadcasted_iota(jnp.int32, (C, C), 0)
    c = lax.broadcasted_iota(jnp.int32, (C, C), 1)
    t = jnp.where(r == c, 1.0, 0.0) - m
    pw = _bmm3(m, m)
    t = t + _bmm3(t, pw)
    for _ in range(int(math.log2(C)) - 2):
        pw = _bmm(pw, pw)
        t = t + _bmm(t, pw)
    return t


def _tri_inv_vjp(t, dt):
    tt = jnp.swapaxes(t, 1, 2)
    return -_bmm(_bmm(tt, dt), tt)


def _twice(a):
    return jnp.broadcast_to(a[:, None], (a.shape[0], 2) + a.shape[1:]).reshape((2 * a.shape[0],) + a.shape[1:])


def _gdn_gates(grow, brow):
    C = grow.shape[2]
    r = lax.broadcasted_iota(jnp.int32, (C, C), 0)
    c = lax.broadcasted_iota(jnp.int32, (C, C), 1)
    tril, eye = r >= c, r == c
    gcol = jnp.sum(jnp.where(eye, grow, 0.0), axis=2, keepdims=True)
    bcol = jnp.sum(jnp.where(eye, brow, 0.0), axis=2, keepdims=True)
    gc_col = jnp.sum(jnp.where(tril, grow, 0.0), axis=2, keepdims=True)
    gc_row = jnp.sum(jnp.where(r <= c, gcol, 0.0), axis=1, keepdims=True)
    gc_last = jnp.sum(grow, axis=2, keepdims=True)
    decay = jnp.where(tril, jnp.exp(jnp.where(tril, gc_col - gc_row, 0.0)), 0.0)
    return bcol, gc_col, gc_last, decay


def _gdn_m(k, bcol, decay):
    C = k.shape[1]
    strict = lax.broadcasted_iota(jnp.int32, (C, C), 0) > lax.broadcasted_iota(jnp.int32, (C, C), 1)
    return jnp.where(strict, bcol * _twice(_bmm(k, k, BNT)) * decay, 0.0)


def _gdn_rest(q, k, v, bcol, gc_col, gc_last, decay, t_mat, S):
    qk = _twice(_bmm(q, k, BNT))
    k2, q2 = _twice(k), _twice(q)
    egc = jnp.exp(gc_col)
    u = _bmm(t_mat, v * bcol)
    w = _bmm(t_mat, k2 * (bcol * egc))
    v_new = u - _bmm(w, S)
    o = _bmm(q2 * egc, S) + _bmm(qk * decay, v_new)
    s_new = S * jnp.exp(gc_last) + _bmm(k2 * jnp.exp(gc_last - gc_col), v_new, BTN)
    return o, s_new


def _gdn_tb(T):
    return min(256, T)


def _gate_rows(g):
    T = g.shape[0]
    g = g[:, :GDN_V_HEADS].reshape(T // GDN_CHUNK, GDN_CHUNK, GDN_V_HEADS)
    return g.transpose(0, 2, 1)[:, :, None, :]


def _gate_cols(g):
    nc = g.shape[0]
    g = g[:, :, 0, :].transpose(0, 2, 1).reshape(nc * GDN_CHUNK, GDN_V_HEADS)
    return jnp.pad(g, ((0, 0), (0, LANE - GDN_V_HEADS)))


def _gdn_fwd(qn, kn, v, g, beta, gather=None):
    T = qn.shape[1]
    tb = _gdn_tb(T)
    nc = tb // GDN_CHUNK
    nsteps = T // tb
    quarters, buffers = gather if gather is not None else ((), ())
    ng = len(quarters)
    shapes = [a.shape for a in quarters]
    splits = [True] * ng

    def body(*refs):
        q_ref, k_ref, v_ref, g_ref, b_ref = refs[:5]
        src = refs[5:5 + ng]
        o_ref, sall_ref, tall_ref = refs[5 + 2 * ng:8 + 2 * ng]
        dst = refs[8 + 2 * ng:8 + 3 * ng]
        s_scr = refs[8 + 3 * ng]
        step = pl.program_id(0)

        @pl.when(step == 0)
        def _():
            s_scr[...] = jnp.zeros(s_scr.shape, F32)
            if ng:
                for cp in _gather_copies(shapes, splits, src, dst, *refs[9 + 3 * ng:])[0]:
                    cp.start()

        def chunk(ci, carry):
            rows = pl.ds(pl.multiple_of(ci * GDN_CHUNK, GDN_CHUNK), GDN_CHUNK)
            s = s_scr[...]
            sall_ref[ci] = s
            q, k = q_ref[:, rows, :].astype(F32), k_ref[:, rows, :].astype(F32)
            bcol, gc_col, gc_last, decay = _gdn_gates(g_ref[ci], b_ref[ci])
            t_mat = _tri_inv(_gdn_m(k, bcol, decay)).astype(BF16)
            tall_ref[ci] = t_mat
            o, s_new = _gdn_rest(q, k, v_ref[:, rows, :].astype(F32), bcol, gc_col, gc_last, decay,
                                 t_mat.astype(F32), s)
            o_ref[:, rows, :] = o.astype(o_ref.dtype)
            s_scr[...] = s_new
            return carry

        lax.fori_loop(0, nc, chunk, 0)

        if ng:
            @pl.when(step == nsteps - 1)
            def _():
                _gather_arrival(shapes, splits, src, dst, *refs[9 + 3 * ng:])

    qk_spec = pl.BlockSpec((GDN_QK_HEADS, tb, GDN_HD), lambda i: (0, i, 0))
    v_spec = pl.BlockSpec((GDN_V_HEADS, tb, GDN_HD), lambda i: (0, i, 0))
    g_spec = pl.BlockSpec((nc, GDN_V_HEADS, 1, GDN_CHUNK), lambda i: (i, 0, 0, 0))
    anywhere = pl.BlockSpec(memory_space=pl.ANY)
    return pl.pallas_call(
        body, name="gdn_fwd", grid=(nsteps,),
        in_specs=[qk_spec, qk_spec, v_spec, g_spec, g_spec] + [anywhere] * (2 * ng),
        out_specs=[v_spec, pl.BlockSpec((nc, GDN_V_HEADS, GDN_HD, GDN_HD), lambda i: (i, 0, 0, 0)),
                   pl.BlockSpec((nc, GDN_V_HEADS, GDN_CHUNK, GDN_CHUNK), lambda i: (i, 0, 0, 0))] + [anywhere] * ng,
        out_shape=[jax.ShapeDtypeStruct((GDN_V_HEADS, T, GDN_HD), BF16),
                   jax.ShapeDtypeStruct((T // GDN_CHUNK, GDN_V_HEADS, GDN_HD, GDN_HD), F32),
                   jax.ShapeDtypeStruct((T // GDN_CHUNK, GDN_V_HEADS, GDN_CHUNK, GDN_CHUNK), BF16)]
        + [jax.ShapeDtypeStruct(b.shape, b.dtype) for b in buffers],
        input_output_aliases={5 + ng + a: 3 + a for a in range(ng)},
        scratch_shapes=[pltpu.VMEM((GDN_V_HEADS, GDN_HD, GDN_HD), F32)]
        + ([pltpu.SemaphoreType.DMA((6 * ng,)), pltpu.SemaphoreType.DMA((6 * ng,))] if ng else []),
        compiler_params=_params(("arbitrary",)),
    )(qn, kn, v, g, beta, *quarters, *buffers)


def _gdn_bwd(qn, kn, v, g, beta, sall, tall, do, scatter=()):
    T = qn.shape[1]
    tb = _gdn_tb(T)
    nc = tb // GDN_CHUNK
    nb = T // tb
    ns = len(scatter)

    def body(*refs):
        q_ref, k_ref, v_ref, g_ref, b_ref, sall_ref, tall_ref, do_ref = refs[:8]
        dq_ref, dk_ref, dv_ref, dg_ref, db_ref = refs[8 + ns:13 + ns]
        ds_scr = refs[13 + 2 * ns]
        comm = (refs[8:8 + ns], refs[13 + ns:13 + 2 * ns], *refs[14 + 2 * ns:])
        step = pl.program_id(0)

        @pl.when(step == 0)
        def _():
            ds_scr[...] = jnp.zeros(ds_scr.shape, F32)
            if ns:
                for cp in _scatter_copies(*comm):
                    cp.start()

        def chunk(cr, carry):
            ci = nc - 1 - cr
            rows = pl.ds(pl.multiple_of(ci * GDN_CHUNK, GDN_CHUNK), GDN_CHUNK)
            k, t_mat = k_ref[:, rows, :].astype(F32), tall_ref[ci].astype(F32)
            (bcol, gc_col, gc_last, decay), vjp_gates = jax.vjp(_gdn_gates, g_ref[ci], b_ref[ci])
            _, vjp = jax.vjp(_gdn_rest, q_ref[:, rows, :].astype(F32), k, v_ref[:, rows, :].astype(F32),
                             bcol, gc_col, gc_last, decay, t_mat, sall_ref[ci])
            dq, dk, dv, dbcol, dgc_col, dgc_last, ddecay, dt, ds = vjp((do_ref[:, rows, :].astype(F32), ds_scr[...]))
            _, vjp_m = jax.vjp(_gdn_m, k, bcol, decay)
            dk_m, dbcol_m, ddecay_m = vjp_m(_tri_inv_vjp(t_mat, dt))
            dg, db = vjp_gates((dbcol + dbcol_m, dgc_col, dgc_last, ddecay + ddecay_m))
            ds_scr[...] = ds
            dq_ref[:, rows, :] = dq
            dk_ref[:, rows, :] = dk + dk_m
            dv_ref[:, rows, :] = dv
            dg_ref[ci] = dg
            db_ref[ci] = db
            return carry

        lax.fori_loop(0, nc, chunk, 0)

        if ns:
            @pl.when(step == nb - 1)
            def _():
                copies = _scatter_copies(*comm)
                for cp in copies:
                    cp.wait_recv()
                for cp in copies:
                    cp.wait_send()

    qk_spec = pl.BlockSpec((GDN_QK_HEADS, tb, GDN_HD), lambda i: (0, nb - 1 - i, 0))
    v_spec = pl.BlockSpec((GDN_V_HEADS, tb, GDN_HD), lambda i: (0, nb - 1 - i, 0))
    g_spec = pl.BlockSpec((nc, GDN_V_HEADS, 1, GDN_CHUNK), lambda i: (nb - 1 - i, 0, 0, 0))
    s_spec = pl.BlockSpec((nc, GDN_V_HEADS, GDN_HD, GDN_HD), lambda i: (nb - 1 - i, 0, 0, 0))
    t_spec = pl.BlockSpec((nc, GDN_V_HEADS, GDN_CHUNK, GDN_CHUNK), lambda i: (nb - 1 - i, 0, 0, 0))
    anywhere = pl.BlockSpec(memory_space=pl.ANY)
    return pl.pallas_call(
        body, name="gdn_bwd", grid=(nb,),
        in_specs=[qk_spec, qk_spec, v_spec, g_spec, g_spec, s_spec, t_spec, v_spec] + [anywhere] * ns,
        out_specs=[qk_spec, qk_spec, v_spec, g_spec, g_spec] + [anywhere] * ns,
        out_shape=[jax.ShapeDtypeStruct((GDN_QK_HEADS, T, GDN_HD), F32),
                   jax.ShapeDtypeStruct((GDN_QK_HEADS, T, GDN_HD), F32),
                   jax.ShapeDtypeStruct((GDN_V_HEADS, T, GDN_HD), F32),
                   jax.ShapeDtypeStruct(g.shape, F32), jax.ShapeDtypeStruct(g.shape, F32)]
        + _scatter_shapes(scatter),
        scratch_shapes=[pltpu.VMEM((GDN_V_HEADS, GDN_HD, GDN_HD), F32)]
        + ([pltpu.SemaphoreType.DMA((3 * ns,)), pltpu.SemaphoreType.DMA((3 * ns,))] if ns else []),
        compiler_params=_params(("arbitrary",)),
    )(qn, kn, v, g, beta, sall, tall, do, *scatter)


def _gnorm_core(o, z, w):
    return tuple(_rms_core(oh, w) * _silu(zh) for oh, zh in zip(o, z))


def _gnorm_fwd(o, pm, w, tm=256):
    T = pm.shape[0]
    tm = min(tm, T)

    def fn(i, ov, zv, wv):
        zf = zv.astype(F32)
        out = _gnorm_core(tuple(ov[h].astype(F32) for h in range(GDN_V_HEADS)), _heads(zf, GDN_V_HEADS), wv)
        return (jnp.concatenate(out, axis=1),)

    ins = [(o, "row", None), (pm, "row", (GDN_V, 2)), (w, "full", None)]
    return _rowcall("gnorm_fwd", fn, T, tm, ins, [((T, GDN_V), BF16, "row")])[0]


def _gnorm_bwd(o, pm, w, don, tm=128, swap=()):
    T = pm.shape[0]
    tm = min(tm, T)

    def fn(i, ov, zv, wv, dv):
        zf, df = zv.astype(F32), dv.astype(F32)
        _, vjp = jax.vjp(_gnorm_core, tuple(ov[h].astype(F32) for h in range(GDN_V_HEADS)),
                         _heads(zf, GDN_V_HEADS), wv)
        do, dz, dw = vjp(_heads(df, GDN_V_HEADS))
        return jnp.stack(do), jnp.concatenate(dz, axis=1), dw

    ins = [(o, "row", None), (pm, "row", (GDN_V, 2)), (w, "full", None), (don, "row", None)]
    outs = [((GDN_V_HEADS, T, GDN_HD), BF16, "row"), ((T, GDN_V), BF16, "row"), ((1, GDN_HD), F32, "acc")]
    return _rowcall("gnorm_bwd", fn, T, tm, ins, outs, swap=swap)


def _ffn_act_fwd(name, up, conv_w, conv_b, tm=128):
    T = up.shape[0]
    tm = min(tm, T)

    def fn(i, x, halo, cw, cb):
        u = _conv_fwd(_taps(x.astype(F32), halo.astype(F32), 3, i), cw) + cb
        return (_silu(u[:, :DFF]) * u[:, DFF:],)

    ins = [(up, "row", None), (up, "prev", None), (conv_w, "full", None), (conv_b, "full", None)]
    return _rowcall(name, fn, T, tm, ins, [((T, DFF), BF16, "row")])[0]


def _ffn_act_bwd(name, up, conv_w, conv_b, dact, tm=128):
    T = up.shape[0]
    tm = min(tm, T)

    def fn(i, x, halo, cw, cb, da):
        taps = _taps(x.astype(F32), halo.astype(F32), 3, i)
        da = da.astype(F32)
        u = _conv_fwd(taps, cw) + cb
        gate, val = u[:, :DFF], u[:, DFF:]
        sg = _sigmoid(gate)
        dgate = da * val * sg * (1.0 + gate * (1.0 - sg))
        dval = da * gate * sg
        du = jnp.concatenate([dgate, dval], axis=1)
        return du, _conv_dw(du, taps), jnp.sum(du, axis=0, keepdims=True)

    ins = [(up, "row", None), (up, "prev", None), (conv_w, "full", None), (conv_b, "full", None),
           (dact, "row", None)]
    outs = [((T, DFF2), BF16, "row"), ((SUBLANE, DFF2), F32, "acc"), ((1, DFF2), F32, "acc")]
    return _rowcall(name, fn, T, tm, ins, outs)


def _ffn_conv_bwd(name, du, conv_w, tm=256):
    T = du.shape[0]
    tm = min(tm, T)
    n = T // tm

    def fn(i, dv, halo, cw):
        return (_conv_dx(dv.astype(F32), halo.astype(F32), cw, i, n),)

    ins = [(du, "row", None), (du, "next", None), (conv_w, "full", None)]
    return _rowcall(name, fn, T, tm, ins, [((T, DFF2), BF16, "row")])[0]


GROUP_ROWS = SWA_GROUP * SWA_BLOCK


def _attn_core(q, kp, kc, vp, vc, bias, sink, mask):
    kcat = jnp.concatenate([kp, kc], axis=0)
    vcat = jnp.concatenate([vp, vc], axis=0)
    s = _bdot(q * (SWA_HD ** -0.5), kcat, NT) + bias
    s = jnp.where(mask, s, NEG_INF)
    m = lax.stop_gradient(jnp.maximum(jnp.max(s, axis=-1, keepdims=True), sink))
    p = jnp.exp(s - m)
    denom = jnp.sum(p, axis=-1, keepdims=True) + jnp.exp(sink - m)
    return _bdot(p / denom, vcat)


def _attn_mask(i):
    qi = lax.broadcasted_iota(jnp.int32, (GROUP_ROWS, 2 * SWA_BLOCK), 0) & (SWA_BLOCK - 1)
    ki = lax.broadcasted_iota(jnp.int32, (GROUP_ROWS, 2 * SWA_BLOCK), 1)
    dist = qi + SWA_BLOCK - ki
    return (dist >= 0) & (dist < SWA_BLOCK) & ((ki >= SWA_BLOCK) | (i > 0))


def _head_cols(h):
    return slice(h * SWA_HD, (h + 1) * SWA_HD)


def _stacked_heads(ref, j):
    return jnp.concatenate([ref[:, _head_cols(SWA_GROUP * j + g)].astype(F32) for g in range(SWA_GROUP)], axis=0)


def _flat_operands(j, q_ref, kvc_ref, kvp_ref, b_ref, s_ref):
    heads = slice(SWA_GROUP * j, SWA_GROUP * (j + 1))
    sink = jnp.concatenate([jnp.broadcast_to(s_ref[j, g:g + 1, 0:1], (SWA_BLOCK, 1)) for g in range(SWA_GROUP)],
                           axis=0)
    k_cols, v_cols = _head_cols(j), _head_cols(SWA_KV_HEADS + j)
    return (_stacked_heads(q_ref, j), kvp_ref[:, k_cols].astype(F32), kvc_ref[:, k_cols].astype(F32),
            kvp_ref[:, v_cols].astype(F32), kvc_ref[:, v_cols].astype(F32),
            b_ref[heads].reshape(GROUP_ROWS, 2 * SWA_BLOCK), sink)


def _store_heads(ref, j, stacked):
    for g in range(SWA_GROUP):
        ref[:, _head_cols(SWA_GROUP * j + g)] = stacked[g * SWA_BLOCK:(g + 1) * SWA_BLOCK].astype(ref.dtype)


def _attn_fwd_flat(q, kv, bias, sinks):
    T = q.shape[0]
    nb = T // SWA_BLOCK

    def body(q_ref, kvc_ref, kvp_ref, b_ref, s_ref, o_ref):
        mask = _attn_mask(pl.program_id(0))
        operands = [_flat_operands(j, q_ref, kvc_ref, kvp_ref, b_ref, s_ref) for j in range(SWA_KV_HEADS)]
        outs = [_attn_core(*ops, mask) for ops in operands]
        for j in range(SWA_KV_HEADS):
            _store_heads(o_ref, j, outs[j])

    q_spec = pl.BlockSpec((SWA_BLOCK, q.shape[1]), lambda i: (i, 0))
    cur = pl.BlockSpec((SWA_BLOCK, kv.shape[1]), lambda i: (i, 0))
    prev = pl.BlockSpec((SWA_BLOCK, kv.shape[1]), lambda i: (jnp.maximum(i - 1, 0), 0))
    return pl.pallas_call(
        body, name="attn_fwd", grid=(nb,),
        in_specs=[q_spec, cur, prev, pl.BlockSpec(bias.shape, lambda i: (0, 0, 0)),
                  pl.BlockSpec(sinks.shape, lambda i: (0, 0, 0))],
        out_specs=q_spec, out_shape=jax.ShapeDtypeStruct(q.shape, BF16),
        compiler_params=_params(("arbitrary",)),
    )(q, kv, kv, bias, sinks)


def _attn_bwd_flat(q, kv, bias, sinks, do):
    T = q.shape[0]
    nb = T // SWA_BLOCK

    def body(q_ref, kvc_ref, kvp_ref, b_ref, s_ref, do_ref, dq_ref, dkv_ref, db_ref, dsk_ref, carry):
        i = pl.program_id(0)

        @pl.when(i < nb)
        def _():
            mask = _attn_mask(i)
            operands = [_flat_operands(j, q_ref, kvc_ref, kvp_ref, b_ref, s_ref) for j in range(SWA_KV_HEADS)]
            cots = [_stacked_heads(do_ref, j) for j in range(SWA_KV_HEADS)]
            grads = [jax.vjp(functools.partial(_attn_core, mask=mask), *ops)[1](cot)
                     for ops, cot in zip(operands, cots)]
            for j, (dq, dkp, dkc, dvp, dvc, db, dsc) in enumerate(grads):
                heads = slice(SWA_GROUP * j, SWA_GROUP * (j + 1))
                k_cols, v_cols = _head_cols(j), _head_cols(SWA_KV_HEADS + j)
                _store_heads(dq_ref, j, dq)
                db = db.reshape(SWA_GROUP, SWA_BLOCK, 2 * SWA_BLOCK)
                dsk = jnp.concatenate(
                    [jnp.broadcast_to(jnp.sum(dsc[g * SWA_BLOCK:(g + 1) * SWA_BLOCK], axis=0, keepdims=True),
                                      (1, LANE)) for g in range(SWA_GROUP)], axis=0)

                @pl.when(i == 0)
                def _():
                    db_ref[heads] = db
                    dsk_ref[j] = dsk

                @pl.when(i > 0)
                def _():
                    db_ref[heads] += db
                    dsk_ref[j] += dsk
                    dkv_ref[:, k_cols] = (carry[:, k_cols] + dkp).astype(dkv_ref.dtype)
                    dkv_ref[:, v_cols] = (carry[:, v_cols] + dvp).astype(dkv_ref.dtype)

                carry[:, k_cols] = dkc
                carry[:, v_cols] = dvc

        @pl.when(i == nb)
        def _():
            dkv_ref[...] = carry[...].astype(dkv_ref.dtype)

    last = nb - 1
    q_spec = pl.BlockSpec((SWA_BLOCK, q.shape[1]), lambda i: (jnp.minimum(i, last), 0))
    cur = pl.BlockSpec((SWA_BLOCK, kv.shape[1]), lambda i: (jnp.minimum(i, last), 0))
    prev = pl.BlockSpec((SWA_BLOCK, kv.shape[1]), lambda i: (jnp.clip(i - 1, 0, last), 0))
    b_spec = pl.BlockSpec(bias.shape, lambda i: (0, 0, 0))
    s_spec = pl.BlockSpec(sinks.shape, lambda i: (0, 0, 0))
    return pl.pallas_call(
        body, name="attn_bwd", grid=(nb + 1,),
        in_specs=[q_spec, cur, prev, b_spec, s_spec, q_spec],
        out_specs=[q_spec, prev, b_spec, s_spec],
        out_shape=[jax.ShapeDtypeStruct(q.shape, BF16), jax.ShapeDtypeStruct(kv.shape, BF16),
                   jax.ShapeDtypeStruct(bias.shape, F32), jax.ShapeDtypeStruct(sinks.shape, F32)],
        scratch_shapes=[pltpu.VMEM((SWA_BLOCK, kv.shape[1]), F32)],
        compiler_params=_params(("arbitrary",)),
    )(q, kv, kv, bias, sinks, do)


def _rel_onehot():
    qi = jnp.arange(SWA_BLOCK)[:, None]
    ki = jnp.arange(2 * SWA_BLOCK)[None, :]
    n = jnp.maximum(qi + SWA_BLOCK - ki, 0)
    max_exact = REL_BUCKETS // 2
    nf = jnp.maximum(n, 1).astype(F32)
    large = max_exact + (jnp.log(nf / max_exact) / math.log(REL_MAX_DISTANCE / max_exact)
                         * (REL_BUCKETS - max_exact)).astype(jnp.int32)
    bucket = jnp.where(n < max_exact, n, jnp.minimum(large, REL_BUCKETS - 1)).reshape(-1)
    return (bucket[None, :] == jnp.arange(REL_BUCKETS)[:, None]).astype(F32)


def _final(h, w, target, tm=256):
    T = h.shape[0]
    tm = min(tm, T)

    def fn(i, hv, wv, tv):
        y, vjp = jax.vjp(_rms_core, hv, wv)
        err = y - tv
        dh, dw = vjp(err * (1.0 / D))
        part = 0.5 * jnp.sum(jnp.sum(err * err, axis=1, keepdims=True) * (1.0 / D), axis=0, keepdims=True)
        return jnp.broadcast_to(part, (SUBLANE, LANE)), dh, dw

    ins = [(h, "row", None), (w, "full", None), (target, "row", None)]
    outs = [((SUBLANE, LANE), F32, "acc"), ((T, D), F32, "row"), ((1, D), F32, "acc")]
    return _rowcall("final", fn, T, tm, ins, outs)


def _ffn_fwd(tag, h, P, layer):
    n = (h, P["ffn_norm_w"][layer:layer + 1])
    up = _mm_up(f"{tag}_up", n, P["w_up"], layer)
    act = _ffn_act_fwd(f"{tag}_act", up, P["ffn_conv_w"][layer], P["ffn_conv_b"][layer:layer + 1])
    out = _mm_nn(f"{tag}_down", act, P["w_down"][layer], F32, res=h)
    return out, (n, up, act)


def _ffn_bwd(tag, h, saved, dout, P, layer, into=(None, None)):
    n, up, act = saved
    cw, cb = P["ffn_conv_w"][layer], P["ffn_conv_b"][layer:layer + 1]
    dact = _mm_nt(f"{tag}_down_dx", dout, P["w_down"][layer], BF16)
    g_down = _mm_down_tn(f"{tag}_down_dw", act, dout, layer, into[1])
    du, dcw, dcb = _ffn_act_bwd(f"{tag}_act_bwd", up, cw, cb, dact)
    dup = _ffn_conv_bwd(f"{tag}_conv_bwd", du, cw)
    g_up = _mm_up_tn(f"{tag}_up_dw", n, dup, layer, into[0])
    dh, dnw = _mm_up_nt(f"{tag}_up_dx", dup, P["w_up"], layer, post=(h, P["ffn_norm_w"][layer:layer + 1], dout))
    return dh, dict(w_down=g_down, w_up=g_up, conv_w=dcw[:3], conv_b=dcb, norm_w=dnw)


def _local_step(x, target, P, late=None, pair_sums=None):
    T = x.shape[0]
    n0 = (x, P["a_norm_w"])
    pm = _mm_nn("gdn_in", n0, P["w_in_main"], BF16)
    pba = _mm_nn("gdn_in_ba", n0, P["w_in_ba"], F32)
    qn, kn, v, beta, g = _gdn_pre_fwd(pm, pba, P["a_conv_w"], P["a_log"], P["dt_bias"])
    g_rows, beta_rows = _gate_rows(g), _gate_rows(beta)
    o, sall, tall, *gathered = _gdn_fwd(qn, kn, v, g_rows, beta_rows, gather=late)
    if late is not None:
        P = {**P, **_late_weights(gathered)}
    on = _gnorm_fwd(o, pm, P["a_out_norm_w"])
    h1 = _mm_nn("gdn_out", on, P["w_out"], F32, res=x)
    h2, ffn0 = _ffn_fwd("ffn0", h1, P, 0)
    nkv = (h2, P["kv_norm_w"])
    kv = _mm_nn("kv_proj", nkv, P["w_kv"], BF16)
    nb = (h2, P["b_norm_w"])
    qp = _mm_nn("q_proj", nb, P["w_q"], BF16)
    onehot = _rel_onehot()
    bias = _mm_nn("rel_bias", P["rel_table_t"], onehot, F32, precision=HIGHEST)
    bias = bias.reshape(SWA_Q_HEADS, SWA_BLOCK, 2 * SWA_BLOCK)
    oa = _attn_fwd_flat(qp, kv, bias, P["sinks"])
    h3 = _mm_nn("o_proj", oa, P["w_o"], F32, res=h2)
    h4, ffn1 = _ffn_fwd("ffn1", h3, P, 1)
    loss, dh4, d_final = _final(h4, P["final_norm_w"], target)

    dh3, gf1 = _ffn_bwd("ffn1", h3, ffn1, dh4, P, 1)
    doa = _mm_nt("o_proj_dx", dh3, P["w_o"], BF16)
    g_wo = _mm_tn("o_proj_dw", oa, dh3)
    dqp, dkv, dbias, dsinks = _attn_bwd_flat(qp, kv, bias, P["sinks"], doa)
    g_wq = _mm_tn("q_proj_dw", nb, dqp)
    dnb = _mm_nt("q_proj_dx", dqp, P["w_q"], F32)
    g_wkv = _mm_tn("kv_proj_dw", nkv, dkv)
    dnkv = _mm_nt("kv_proj_dx", dkv, P["w_kv"], F32)
    dh2, d_bnorm, d_kvnorm = _rms_bwd("b_kv_rms_bwd", h2, [(P["b_norm_w"], dnb), (P["kv_norm_w"], dnkv)], [dh3])
    g_table = _mm_nt("rel_bias_dw", onehot, dbias.reshape(SWA_Q_HEADS, -1), F32, precision=HIGHEST)
    dh1, gf0 = _ffn_bwd("ffn0", h1, ffn0, dh2, P, 0, into=(gf1["w_up"], gf1["w_down"]))
    don = _mm_nt("gdn_out_dx", dh1, P["w_out"], BF16)
    g_wout = _mm_tn("gdn_out_dw", on, dh1)
    ready = dict(a_w_out=g_wout, w_kv=g_wkv, b_w_q=g_wq, b_w_o=g_wo, ffn_w_up=gf0["w_up"], ffn_w_down=gf0["w_down"])
    names = [n for n in BIG if n in ready]
    whole = [_chip_major(n, ready[n]) for n in names] if pair_sums is not None else []
    do, dz, d_gnorm, *other = _gnorm_bwd(o, pm, P["a_out_norm_w"], don, swap=whole)
    pairs = pair_sums(names, whole, other) if pair_sums is not None else []
    dq, dk, dv, dg, dbeta, *parts = _gdn_bwd(qn, kn, v, g_rows, beta_rows, sall, tall, do, scatter=pairs)
    dy, dpba, d_aconv, d_alog, d_dtb = _gdn_pre_bwd(pm, pba, P["a_conv_w"], P["a_log"], P["dt_bias"],
                                                    dq, dk, dv, _gate_cols(dbeta), _gate_cols(dg))
    dpm = _gdn_conv_bwd(dy, dz, P["a_conv_w"])
    g_win_main = _mm_tn("gdn_in_dw", n0, dpm)
    g_win_ba = _mm_tn("gdn_in_ba_dw", n0, dpba)
    nh = GDN_V_HEADS
    g_win = jnp.concatenate([g_win_main, g_win_ba[:, :nh], g_win_ba[:, LANE:LANE + nh]], axis=1)
    last_whole = [_chip_major("a_w_in", g_win)]
    last_pair = pair_sums(["a_w_in"], last_whole, _pair_swap(last_whole, "late")) if pair_sums is not None else []
    dn0 = _mm_nt("gdn_in_dx", dpm, P["w_in_main"], F32, scatter=last_pair)
    dn0, last_parts = (dn0[0], dn0[1:]) if last_pair else (dn0, [])
    dx, d_anorm = _mm_nt("gdn_in_ba_dx", dpba, P["w_in_ba"], F32, res=dn0, post=(x, P["a_norm_w"], dh1))

    nh = GDN_V_HEADS
    grads = dict(
        a_norm_w=d_anorm,
        a_w_in=g_win,
        a_conv_w=d_aconv[:4], a_a_log=d_alog[:, :nh], a_dt_bias=d_dtb[:, :nh], a_out_norm_w=d_gnorm,
        a_w_out=g_wout, kv_norm_w=d_kvnorm, w_kv=g_wkv, b_norm_w=d_bnorm, b_w_q=g_wq,
        b_sinks=dsinks[:, :, 0].reshape(1, SWA_Q_HEADS), b_w_o=g_wo, rel_bias_table=g_table,
        ffn_norm_w=jnp.concatenate([gf0["norm_w"], gf1["norm_w"]], axis=0),
        ffn_w_up=gf0["w_up"],
        ffn_conv_w=jnp.stack([gf0["conv_w"], gf1["conv_w"]], axis=0),
        ffn_conv_b=jnp.concatenate([gf0["conv_b"], gf1["conv_b"]], axis=0),
        ffn_w_down=gf0["w_down"],
        final_norm_w=d_final,
    )
    scattered = dict(zip([n for n in BIG if n in ready], zip(pairs, parts)))
    scattered.update(zip(["a_w_in"], zip(last_pair, last_parts)))
    return loss, dx, grads, scattered


HBM_SPEC = pl.BlockSpec(memory_space=pltpu.HBM)
VMEM_SPEC = pl.BlockSpec(memory_space=pltpu.VMEM)


def _coords():
    return lax.axis_index("x"), lax.axis_index("y"), lax.axis_index("c")


def _remote(src, dst, send_sem, recv_sem, device):
    return pltpu.make_async_remote_copy(src_ref=src, dst_ref=dst, send_sem=send_sem, recv_sem=recv_sem,
                                        device_id=device, device_id_type=MESH)


def _other_chips(x, y):
    return [(1 - x, y), (x, 1 - y), (1 - x, 1 - y)]


def _gather_copies(shapes, split, ins, outs, send_sems, recv_sems):
    x, y, c = _coords()
    p = 2 * x + y
    ici, forwards, from_sibling = [], [], []
    for a, shape in enumerate(shapes):
        h = shape[0] // 2
        for j, chip in enumerate(_other_chips(x, y)):
            q = 2 * chip[0] + chip[1]
            if split[a]:
                mine, theirs = pl.ds(c * h, h), pl.ds((1 - c) * h, h)
                ici.append(_remote(ins[a].at[mine], outs[a].at[p, mine], send_sems.at[6 * a + j],
                                   recv_sems.at[6 * a + j], (*chip, c)))
                land = outs[a].at[q, mine]
                forwards.append(_remote(land, land, send_sems.at[6 * a + 3 + j], recv_sems.at[6 * a + 3 + j],
                                        (x, y, 1 - c)))
                land = outs[a].at[q, theirs]
                from_sibling.append(_remote(land, land, send_sems.at[6 * a + 3 + j], recv_sems.at[6 * a + 3 + j],
                                            (x, y, 1 - c)))
            else:
                ici.append(_remote(ins[a], outs[a].at[p], send_sems.at[6 * a + j], recv_sems.at[6 * a + j],
                                   (*chip, c)))
                forwards.append(None)
    return ici, forwards, from_sibling


def _gather_arrival(shapes, split, ins, outs, send_sems, recv_sems):
    x, y, c = _coords()
    ici, forwards, from_sibling = _gather_copies(shapes, split, ins, outs, send_sems, recv_sems)
    k = 0
    for a, shape in enumerate(shapes):
        h = shape[0] // 2
        for j, chip in enumerate(_other_chips(x, y)):
            q = 2 * chip[0] + chip[1]
            land = outs[a].at[q, pl.ds(c * h, h)] if split[a] else outs[a].at[q]
            _remote(land, land, send_sems.at[6 * a + j], recv_sems.at[6 * a + j], (*chip, c)).wait_recv()
            if forwards[k] is not None:
                forwards[k].start()
            k += 1
    for cp in from_sibling:
        cp.wait_recv()
    for cp in ici + [f for f in forwards if f is not None]:
        cp.wait_send()


def _all_gather(arrs, split, remote):
    n = len(arrs)
    now = [a for a in range(n) if remote[a]]
    shapes = [arrs[a].shape for a in now]
    splits = [split[a] for a in now]

    def body(*refs):
        ins, outs, stage = refs[:n], refs[n:2 * n], refs[2 * n:3 * n]
        send_sems, recv_sems, in_sems, out_sems = refs[3 * n:]
        p = 2 * lax.axis_index("x") + lax.axis_index("y")
        gathered = ([ins[a] for a in now], [outs[a] for a in now], send_sems, recv_sems)
        loads = [pltpu.make_async_copy(ins[a], stage[a], in_sems.at[a]) for a in range(n)]
        for cp in loads:
            cp.start()
        for cp in _gather_copies(shapes, splits, *gathered)[0]:
            cp.start()
        stores = [pltpu.make_async_copy(stage[a], outs[a].at[p], out_sems.at[a]) for a in range(n)]
        for a in range(n):
            loads[a].wait()
            stores[a].start()
        _gather_arrival(shapes, splits, *gathered)
        for cp in stores:
            cp.wait()

    return pl.pallas_call(
        body, name="weights_all_gather", in_specs=[HBM_SPEC] * n, out_specs=[HBM_SPEC] * n,
        out_shape=[jax.ShapeDtypeStruct((N_CHIPS,) + a.shape, a.dtype) for a in arrs],
        scratch_shapes=[pltpu.VMEM(a.shape, a.dtype) for a in arrs]
        + [pltpu.SemaphoreType.DMA((6 * len(now),)), pltpu.SemaphoreType.DMA((6 * len(now),)),
           pltpu.SemaphoreType.DMA((n,)), pltpu.SemaphoreType.DMA((n,))],
        compiler_params=pltpu.CompilerParams(vmem_limit_bytes=VMEM_LIMIT),
    )(*arrs)


PAIR_SWAP_PIECES = 2


def _swap_copies(shapes, ins, other, send_sems, recv_sems):
    x, y, c = _coords()
    copies = []
    for a, shape in enumerate(shapes):
        h = shape[1] // 2
        piece = h // PAIR_SWAP_PIECES
        for q in range(N_CHIPS):
            for r in range(PAIR_SWAP_PIECES):
                k = (a * N_CHIPS + q) * PAIR_SWAP_PIECES + r
                copies.append(_remote(ins[a].at[q, pl.ds((1 - c) * h + r * piece, piece)],
                                      other[a].at[q, pl.ds(r * piece, piece)], send_sems.at[k], recv_sems.at[k],
                                      (x, y, 1 - c)))
    return copies


def _swap_out_shapes(gs):
    return [jax.ShapeDtypeStruct((N_CHIPS, g.shape[1] // 2, g.shape[2]), g.dtype) for g in gs]


def _pair_swap(gs, tag):
    n = len(gs)
    shapes = [g.shape for g in gs]

    def body(*refs):
        copies = _swap_copies(shapes, refs[:n], refs[n:2 * n], *refs[2 * n:])
        for cp in copies:
            cp.start()
        for cp in copies:
            cp.wait()

    nsem = n * N_CHIPS * PAIR_SWAP_PIECES
    return pl.pallas_call(
        body, name=f"grads_pair_swap_{tag}", in_specs=[HBM_SPEC] * n, out_specs=[HBM_SPEC] * n,
        out_shape=_swap_out_shapes(gs),
        scratch_shapes=[pltpu.SemaphoreType.DMA((nsem,)), pltpu.SemaphoreType.DMA((nsem,))],
    )(*gs)


def _scatter_copies(ins, outs, send_sems, recv_sems):
    x, y, c = _coords()
    copies = []
    for a in range(len(ins)):
        for j, chip in enumerate(_other_chips(x, y)):
            q = 2 * chip[0] + chip[1]
            copies.append(_remote(ins[a].at[q], outs[a].at[j], send_sems.at[3 * a + j], recv_sems.at[3 * a + j],
                                  (*chip, c)))
    return copies


def _scatter_shapes(ps):
    return [jax.ShapeDtypeStruct((N_CHIPS - 1,) + a.shape[1:], a.dtype) for a in ps]


def _pair_share(rs):
    n = len(rs)

    def body(*refs):
        ins, outs, stage = refs[:n], refs[n:2 * n], refs[2 * n:3 * n]
        send_sems, recv_sems, in_sems, out_sems = refs[3 * n:]
        x, y, c = _coords()

        def mine(a):
            h = rs[a].shape[0]
            return outs[a].at[pl.ds(c * h, h)]

        loads = [pltpu.make_async_copy(ins[a], stage[a], in_sems.at[a]) for a in range(n)]
        for cp in loads:
            cp.start()
        sends = [_remote(ins[a], mine(a), send_sems.at[a], recv_sems.at[a], (x, y, 1 - c)) for a in range(n)]
        for cp in sends:
            cp.start()
        stores = [pltpu.make_async_copy(stage[a], mine(a), out_sems.at[a]) for a in range(n)]
        for a in range(n):
            loads[a].wait()
            stores[a].start()
        for a in range(n):
            h = rs[a].shape[0]
            land = outs[a].at[pl.ds((1 - c) * h, h)]
            _remote(land, land, send_sems.at[a], recv_sems.at[a], (x, y, 1 - c)).wait_recv()
        for cp in sends:
            cp.wait_send()
        for cp in stores:
            cp.wait()

    return pl.pallas_call(
        body, name="grads_pair_share", in_specs=[HBM_SPEC] * n, out_specs=[HBM_SPEC] * n,
        out_shape=[jax.ShapeDtypeStruct((2 * a.shape[0], a.shape[1]), a.dtype) for a in rs],
        scratch_shapes=[pltpu.VMEM(a.shape, a.dtype) for a in rs] + [pltpu.SemaphoreType.DMA((n,))] * 4,
        compiler_params=pltpu.CompilerParams(vmem_limit_bytes=VMEM_LIMIT),
    )(*rs)


def _small_all_reduce(buf):
    R = buf.shape[0]
    ndev = 2 * N_CHIPS

    def body(in_ref, out_ref, gath, send_sems, recv_sems):
        x, y, c = _coords()
        me = 4 * x + 2 * y + c
        gath[me] = in_ref[...]
        peers = []
        for d in range(1, ndev):
            px = 1 - x if d & 4 else x
            py = 1 - y if d & 2 else y
            pc = 1 - c if d & 1 else c
            peers.append((px, py, pc))
        sends = []
        for d, peer in enumerate(peers):
            cp = _remote(in_ref, gath.at[me], send_sems.at[d], recv_sems.at[d], peer)
            cp.start()
            sends.append(cp)
        for d, peer in enumerate(peers):
            land = gath.at[4 * peer[0] + 2 * peer[1] + peer[2]]
            _remote(land, land, send_sems.at[d], recv_sems.at[d], peer).wait_recv()
        for cp in sends:
            cp.wait_send()
        acc = gath[0]
        for s in range(1, ndev):
            acc = acc + gath[s]
        out_ref[...] = acc

    return pl.pallas_call(
        body, name="small_all_reduce", in_specs=[VMEM_SPEC], out_specs=VMEM_SPEC,
        out_shape=jax.ShapeDtypeStruct(buf.shape, F32),
        scratch_shapes=[pltpu.VMEM((ndev, R, LANE), F32), pltpu.SemaphoreType.DMA((ndev - 1,)),
                        pltpu.SemaphoreType.DMA((ndev - 1,))],
    )(buf)


def _pair_add(name, own, other):
    h = own.shape[1]
    tm = _tile(h, (128, 64, 32, 16))

    def fn(i, a, b):
        return (a + b,)

    return _rowcall(name, fn, h, tm, [(own, "row", None), (other, "row", None)], [(own.shape, BF16, "row")])[0]


def _chip_add(name, own, parts):
    h = parts.shape[1]
    tm = _tile(h, (128, 64, 32, 16))

    def fn(i, o, a):
        a = a.astype(F32)
        return (((o.astype(F32) + a[0]) + a[1]) + a[2],)

    return _rowcall(name, fn, h, tm, [(own, "row", None), (parts, "row", None)], [(parts.shape[1:], F32, "row")])[0]


def _adamw(name, w, g, m, v):
    R = w.shape[0]
    tm = _tile(R, (256, 128, 64, 32, 16, 8))

    def fn(i, wv, gv, mv, vv):
        m2 = ADAM_B1 * mv + (1.0 - ADAM_B1) * gv
        v2 = ADAM_B2 * vv + (1.0 - ADAM_B2) * (gv * gv)
        m_hat = m2 / (1.0 - ADAM_B1 ** ADAM_STEP)
        v_hat = v2 / (1.0 - ADAM_B2 ** ADAM_STEP)
        delta = -ADAM_LR * (m_hat / (jnp.sqrt(v_hat) + ADAM_EPS) + ADAM_WD * wv)
        return delta, m2, v2

    ins = [(a, "row", None) for a in (w, g, m, v)]
    return _rowcall(name, fn, R, tm, ins, [(w.shape, F32, "row")] * 3)


def _pack(arrs):
    flat = jnp.concatenate([a.reshape(-1).astype(F32) for a in arrs])
    size = flat.shape[0]
    padded = -(-size // (SUBLANE * LANE)) * SUBLANE * LANE
    return jnp.pad(flat, (0, padded - size)).reshape(-1, LANE)


def _unpack(buf, shapes):
    flat = buf.reshape(-1)
    out, off = [], 0
    for s in shapes:
        size = math.prod(s)
        out.append(flat[off:off + size].reshape(s))
        off += size
    return out


BIG = ("a_w_in", "a_w_out", "w_kv", "b_w_q", "b_w_o", "ffn_w_up", "ffn_w_down")
WEIGHTS = ("a_norm_w", "a_w_in", "a_conv_w", "a_a_log", "a_dt_bias", "a_out_norm_w", "a_w_out", "kv_norm_w", "w_kv",
           "b_norm_w", "b_w_q", "b_sinks", "b_w_o", "rel_bias_table", "ffn_norm_w", "ffn_w_up", "ffn_conv_w",
           "ffn_conv_b", "ffn_w_down", "final_norm_w")
SMALL = tuple(n for n in WEIGHTS if n not in BIG)
SMALL_SHARDED = {"a_norm_w": 1, "a_conv_w": 2, "ffn_conv_w": 2}


def _quarter_2d(name, a):
    if name in ("ffn_w_up", "ffn_w_down"):
        return a.reshape(a.shape[0] * a.shape[1], a.shape[2])
    return a.reshape(a.shape[-2], a.shape[-1])


def _whole_weights(w):
    bigs = [_quarter_2d(n, w[n]).astype(BF16) for n in BIG]
    smalls = [w["a_norm_w"], w["a_conv_w"][0], w["ffn_conv_w"].reshape(6, DFF2_SHARD)]
    remote = [True] + [False] * (len(bigs) - 1) + [True] * len(smalls)
    g = _all_gather(bigs + smalls, [True] * len(bigs) + [False] * len(smalls), remote)
    w_in = g[0].transpose(1, 0, 2).reshape(D, GDN_IN)
    nh = GDN_V_HEADS
    zpad = jnp.zeros((D, LANE - nh), BF16)
    w_in_ba = jnp.concatenate([w_in[:, GDN_MAIN:GDN_MAIN + nh], zpad, w_in[:, GDN_MAIN + nh:], zpad], axis=1)
    lane_pad = lambda a: jnp.pad(a, ((0, 0), (0, LANE - nh)))
    early = dict(
        a_norm_w=g[7].reshape(1, D), w_in_main=w_in[:, :GDN_MAIN], w_in_ba=w_in_ba,
        a_conv_w=g[8].transpose(1, 0, 2).reshape(4, GDN_CONV), a_log=lane_pad(w["a_a_log"]),
        dt_bias=lane_pad(w["a_dt_bias"]), a_out_norm_w=w["a_out_norm_w"],
        kv_norm_w=w["kv_norm_w"].reshape(1, D), b_norm_w=w["b_norm_w"],
        sinks=jnp.broadcast_to(w["b_sinks"].reshape(SWA_KV_HEADS, SWA_GROUP, 1), (SWA_KV_HEADS, SWA_GROUP, LANE)),
        rel_table_t=w["rel_bias_table"].T, ffn_norm_w=w["ffn_norm_w"],
        ffn_conv_w=g[9].reshape(N_CHIPS, 2, 3, DFF2_SHARD).transpose(1, 2, 0, 3).reshape(2, 3, DFF2),
        ffn_conv_b=w["ffn_conv_b"], final_norm_w=w["final_norm_w"].reshape(1, D),
    )
    return early, (bigs[1:], g[1:len(bigs)])


def _late_weights(g):
    return dict(
        w_out=g[0].reshape(GDN_V, D), w_kv=g[1].reshape(D, 2 * SWA_KV_HEADS * SWA_HD), w_q=g[2].reshape(D, D),
        w_o=g[3].reshape(D, D), w_up=g[4].reshape(N_CHIPS, 2, D, DFF2_SHARD),
        w_down=g[5].reshape(N_CHIPS, 2, DFF_SHARD, D).transpose(1, 0, 2, 3).reshape(2, DFF, D),
    )


def _chip_major(name, g):
    if name == "a_w_in":
        return g.reshape(D, N_CHIPS, GDN_IN_SHARD).transpose(1, 0, 2)
    if name == "ffn_w_up":
        return g.reshape(N_CHIPS, 2 * D, DFF2_SHARD)
    if name == "ffn_w_down":
        return g.reshape(N_CHIPS, 2 * DFF_SHARD, D)
    return g.reshape(N_CHIPS, g.shape[0] // N_CHIPS, g.shape[1])


def kernel(x, a_norm_w, a_w_in, a_conv_w, a_a_log, a_dt_bias, a_out_norm_w, a_w_out, kv_norm_w, w_kv, b_norm_w, b_w_q, b_sinks, b_w_o, rel_bias_table, ffn_norm_w, ffn_w_up, ffn_conv_w, ffn_conv_b, ffn_w_down, final_norm_w, loss_target, m_a_norm_w, m_a_w_in, m_a_conv_w, m_a_a_log, m_a_dt_bias, m_a_out_norm_w, m_a_w_out, m_kv_norm_w, m_w_kv, m_b_norm_w, m_b_w_q, m_b_sinks, m_b_w_o, m_rel_bias_table, m_ffn_norm_w, m_ffn_w_up, m_ffn_conv_w, m_ffn_conv_b, m_ffn_w_down, m_final_norm_w, v_a_norm_w, v_a_w_in, v_a_conv_w, v_a_a_log, v_a_dt_bias, v_a_out_norm_w, v_a_w_out, v_kv_norm_w, v_w_kv, v_b_norm_w, v_b_w_q, v_b_sinks, v_b_w_o, v_rel_bias_table, v_ffn_norm_w, v_ffn_w_up, v_ffn_conv_w, v_ffn_conv_b, v_ffn_w_down, v_final_norm_w):
    w = dict(zip(WEIGHTS, (a_norm_w, a_w_in, a_conv_w, a_a_log, a_dt_bias, a_out_norm_w, a_w_out, kv_norm_w, w_kv,
                           b_norm_w, b_w_q, b_sinks, b_w_o, rel_bias_table, ffn_norm_w, ffn_w_up, ffn_conv_w,
                           ffn_conv_b, ffn_w_down, final_norm_w)))
    m = dict(zip(WEIGHTS, (m_a_norm_w, m_a_w_in, m_a_conv_w, m_a_a_log, m_a_dt_bias, m_a_out_norm_w, m_a_w_out,
                           m_kv_norm_w, m_w_kv, m_b_norm_w, m_b_w_q, m_b_sinks, m_b_w_o, m_rel_bias_table,
                           m_ffn_norm_w, m_ffn_w_up, m_ffn_conv_w, m_ffn_conv_b, m_ffn_w_down, m_final_norm_w)))
    v = dict(zip(WEIGHTS, (v_a_norm_w, v_a_w_in, v_a_conv_w, v_a_a_log, v_a_dt_bias, v_a_out_norm_w, v_a_w_out,
                           v_kv_norm_w, v_w_kv, v_b_norm_w, v_b_w_q, v_b_sinks, v_b_w_o, v_rel_bias_table,
                           v_ffn_norm_w, v_ffn_w_up, v_ffn_conv_w, v_ffn_conv_b, v_ffn_w_down, v_final_norm_w)))
    T = x.shape[1]
    chip = 2 * lax.axis_index("x") + lax.axis_index("y")

    core = lax.axis_index("c")

    def pair_sums(names, whole, other):
        own = [lax.dynamic_slice_in_dim(g, core * (g.shape[1] // 2), g.shape[1] // 2, 1) for g in whole]
        return [_pair_add(f"pair_add_{n}", a, b) for n, a, b in zip(names, own, other)]

    early, late = _whole_weights(w)
    loss_part, dx, grads, scattered = _local_step(x.reshape(T, D), loss_target.reshape(T, D), early, late, pair_sums)

    assert all(n in scattered for n in BIG)
    halves = [_chip_add(f"chip_add_{n}", lax.dynamic_index_in_dim(scattered[n][0], chip, 0, keepdims=False),
                        scattered[n][1]) for n in BIG]
    quarter = _pair_share(halves)
    out_g, out_d, out_m, out_v = {}, {}, {}, {}
    for n, g2 in zip(BIG, quarter):
        res = _adamw(f"adamw_{n}", _quarter_2d(n, w[n]), g2, _quarter_2d(n, m[n]), _quarter_2d(n, v[n]))
        out_g[n] = g2.reshape(w[n].shape)
        out_d[n], out_m[n], out_v[n] = (r.reshape(w[n].shape) for r in res)

    whole = [grads[n] for n in SMALL]
    summed = _unpack(_small_all_reduce(_pack([loss_part[0:1, 0:1]] + whole)), [(1, 1)] + [a.shape for a in whole])
    loss = summed[0].reshape(())
    small_g = []
    for n, g in zip(SMALL, summed[1:]):
        if n in SMALL_SHARDED:
            axis = SMALL_SHARDED[n]
            g = g.reshape(w[n].shape[:axis] + (-1,) + w[n].shape[axis + 1:])
            size = w[n].shape[axis]
            g = lax.dynamic_slice_in_dim(g, chip * size, size, axis)
        small_g.append(g.reshape(w[n].shape))
    shapes = [w[n].shape for n in SMALL]
    res = _adamw("adamw_small", _pack([w[n] for n in SMALL]), _pack(small_g), _pack([m[n] for n in SMALL]),
                 _pack([v[n] for n in SMALL]))
    small_d, small_m, small_v = (_unpack(r, shapes) for r in res)
    for i, n in enumerate(SMALL):
        out_g[n], out_d[n], out_m[n], out_v[n] = small_g[i], small_d[i], small_m[i], small_v[i]

    return (loss, dx.reshape(x.shape), *[out_g[n] for n in WEIGHTS], *[out_d[n] for n in WEIGHTS],
            *[out_m[n] for n in WEIGHTS], *[out_v[n] for n in WEIGHTS])
```

```python
import functools
import math

import jax
import jax.numpy as jnp
from jax import lax
from jax.experimental import pallas as pl
from jax.experimental.pallas import tpu as pltpu

F32 = jnp.float32
BF16 = jnp.bfloat16
MESH = pl.DeviceIdType.MESH
HIGHEST = lax.Precision.HIGHEST

D = 1024
EPS = 1e-6
NEG_INF = -1e30
N_CHIPS = 4

GDN_QK_HEADS = 8
GDN_V_HEADS = 16
GDN_HD = 128
GDN_QK = GDN_QK_HEADS * GDN_HD
GDN_V = GDN_V_HEADS * GDN_HD
GDN_CONV = 2 * GDN_QK + GDN_V
GDN_MAIN = GDN_CONV + GDN_V
GDN_IN = GDN_MAIN + 2 * GDN_V_HEADS
GDN_IN_SHARD = GDN_IN // N_CHIPS
GDN_CHUNK = 64

SWA_Q_HEADS = 16
SWA_KV_HEADS = 4
SWA_GROUP = 4
SWA_HD = 64
SWA_BLOCK = 128
REL_BUCKETS = 32
REL_MAX_DISTANCE = 128

DFF = 2816
DFF2 = 2 * DFF
DFF2_SHARD = DFF2 // N_CHIPS
DFF_SHARD = DFF // N_CHIPS

ADAM_LR = 0.001
ADAM_B1 = 0.9
ADAM_B2 = 0.999
ADAM_EPS = 1e-08
ADAM_WD = 0.01
ADAM_STEP = 10

LANE = 128
SUBLANE = 8
VMEM_LIMIT = 56 * 1024 * 1024


def _params(sem, vmem=VMEM_LIMIT):
    return pltpu.CompilerParams(dimension_semantics=sem, vmem_limit_bytes=vmem)


def _rowcall(name, fn, T, tm, ins, outs, swap=()):
    n = T // tm
    nswap = len(swap)
    swap_shapes = [g.shape for g in swap]
    r8 = tm // SUBLANE
    last8 = T // SUBLANE - 1
    arrays, in_specs = [], []
    for arr, kind, cols in ins:
        arrays.append(arr)
        if kind == "full":
            in_specs.append(pl.BlockSpec(arr.shape, functools.partial(lambda nd, i: (0,) * nd, arr.ndim)))
        elif arr.ndim == 2:
            w, ci = cols if cols is not None else (arr.shape[1], 0)
            if kind == "row":
                in_specs.append(pl.BlockSpec((tm, w), functools.partial(lambda ci, i: (i, ci), ci)))
            elif kind == "prev":
                in_specs.append(pl.BlockSpec(
                    (SUBLANE, w), functools.partial(lambda ci, i: (jnp.maximum(i * r8 - 1, 0), ci), ci)))
            else:
                in_specs.append(pl.BlockSpec(
                    (SUBLANE, w), functools.partial(lambda ci, i: (jnp.minimum((i + 1) * r8, last8), ci), ci)))
        else:
            lead = arr.shape[:-2]
            in_specs.append(pl.BlockSpec(lead + (tm, arr.shape[-1]),
                                         functools.partial(lambda nl, i: (0,) * nl + (i, 0), len(lead))))
    out_shape, out_specs = [], []
    for shape, dtype, kind in outs:
        out_shape.append(jax.ShapeDtypeStruct(shape, dtype))
        if kind == "acc":
            out_specs.append(pl.BlockSpec(shape, functools.partial(lambda nd, i: (0,) * nd, len(shape))))
        else:
            lead = shape[:-2]
            out_specs.append(pl.BlockSpec(lead + (tm, shape[-1]),
                                          functools.partial(lambda nl, i: (0,) * nl + (i, 0), len(lead))))
    nin = len(arrays)

    nout = len(outs)

    def body(*refs):
        i = pl.program_id(0)
        if nswap:
            comm = (swap_shapes, refs[nin:nin + nswap], refs[nin + nswap + nout:nin + 2 * nswap + nout], refs[-2],
                    refs[-1])

            @pl.when(i == 0)
            def _():
                for cp in _swap_copies(*comm):
                    cp.start()

            @pl.when(i == n - 1)
            def _():
                for cp in _swap_copies(*comm):
                    cp.wait()

        vals = [r[...] for r in refs[:nin]]
        res = fn(i, *vals)
        for (shape, dtype, kind), o, r in zip(outs, refs[nin + nswap:], res):
            if kind == "row":
                o[...] = r.astype(dtype)
            else:
                @pl.when(i == 0)
                def _():
                    o[...] = r.astype(dtype)

                @pl.when(i > 0)
                def _():
                    o[...] += r.astype(dtype)

    anywhere = pl.BlockSpec(memory_space=pl.ANY)
    nsem = nswap * N_CHIPS * PAIR_SWAP_PIECES
    return pl.pallas_call(
        body, name=name, grid=(n,), in_specs=in_specs + [anywhere] * nswap, out_specs=out_specs + [anywhere] * nswap,
        out_shape=out_shape + _swap_out_shapes(swap),
        scratch_shapes=[pltpu.SemaphoreType.DMA((nsem,)), pltpu.SemaphoreType.DMA((nsem,))] if nswap else [],
        compiler_params=_params(("arbitrary",)),
    )(*arrays, *swap)


def _mm(name, a, b, out_shape, out_dtype, grid, a_spec, b_spec, o_spec, dims, acc_shape, res=None, precision=None,
        into=None, scatter=(), post=None):
    nk = grid[2]
    ns = len(scatter)
    a, norm_w = a if isinstance(a, tuple) else (a, None)
    normed = norm_w is not None
    posted = post is not None
    n_in = 2 + normed + (res is not None) + (into is not None) + 3 * posted + ns
    n_out = 1 + posted + ns

    def body(*refs):
        a_ref, b_ref, o_ref = refs[0], refs[1], refs[n_in]
        r_ref = refs[2 + normed] if res is not None else None
        first = pl.program_id(0) == 0
        if ns:
            comm = (refs[n_in - ns:n_in], refs[n_in + n_out - ns:n_in + n_out], refs[-2], refs[-1])
            steps = [pl.program_id(d) for d in range(3)]

            @pl.when((steps[0] == 0) & (steps[1] == 0) & (steps[2] == 0))
            def _():
                for cp in _scatter_copies(*comm):
                    cp.start()

            @pl.when((steps[0] == grid[0] - 1) & (steps[1] == grid[1] - 1) & (steps[2] == grid[2] - 1))
            def _():
                copies = _scatter_copies(*comm)
                for cp in copies:
                    cp.wait_recv()
                for cp in copies:
                    cp.wait_send()

        av, bv = a_ref[...], b_ref[...]
        if normed:
            av = _rms_core(av, refs[2][...])
        if precision is None:
            av, bv = av.astype(BF16), bv.astype(BF16)
        p = lax.dot_general(av, bv, (dims, ((), ())), preferred_element_type=F32, precision=precision)

        def finish(x):
            if res is not None:
                x = x + r_ref[...].astype(F32)
            if posted:
                h_ref, w_ref, add_ref = refs[n_in - ns - 3:n_in - ns]
                dh, dw = jax.vjp(_rms_core, h_ref[...], w_ref[...])[1](x)
                x = dh + add_ref[...]
                dw_ref = refs[n_in + 1]

                @pl.when(first)
                def _():
                    dw_ref[...] = dw

                @pl.when(jnp.logical_not(first))
                def _():
                    dw_ref[...] += dw

            o_ref[...] = x.astype(out_dtype).reshape(o_ref.shape)

        if nk == 1:
            finish(p)
        else:
            acc = refs[n_in + n_out]
            k = pl.program_id(2)

            @pl.when(k == 0)
            def _():
                acc[...] = p

            @pl.when(k > 0)
            def _():
                acc[...] += p

            @pl.when(k == nk - 1)
            def _():
                finish(acc[...])

    anywhere = pl.BlockSpec(memory_space=pl.ANY)
    ops = [a, b] + ([norm_w] if normed else []) + ([res] if res is not None else [])
    ops += ([into] if into is not None else []) + (list(post) if posted else []) + list(scatter)
    whole = lambda arr: pl.BlockSpec(arr.shape, lambda i, j, k: (0, 0))
    specs = [a_spec, b_spec] + ([whole(norm_w)] if normed else [])
    specs += [o_spec] if res is not None else []
    specs += [anywhere] if into is not None else []
    specs += ([o_spec, whole(post[1]), o_spec] if posted else []) + [anywhere] * ns
    out = pl.pallas_call(
        body, name=name, grid=grid, in_specs=specs,
        out_specs=[o_spec] + ([whole(post[1])] if posted else []) + [anywhere] * ns,
        out_shape=[jax.ShapeDtypeStruct(out_shape, out_dtype)]
        + ([jax.ShapeDtypeStruct(post[1].shape, F32)] if posted else []) + _scatter_shapes(scatter),
        input_output_aliases={2 + normed + (res is not None): 0} if into is not None else {},
        scratch_shapes=([pltpu.VMEM(acc_shape, F32)] if nk > 1 else [])
        + ([pltpu.SemaphoreType.DMA((3 * ns,)), pltpu.SemaphoreType.DMA((3 * ns,))] if ns else []),
        compiler_params=_params(("arbitrary",) * 3 if ns or posted else ("parallel", "parallel", "arbitrary")),
    )(*ops)
    return out if n_out > 1 else out[0]


NN = ((1,), (0,))
NT = ((1,), (1,))
TN = ((0,), (0,))


BIG_TILES = (1024, 512, 256, 128)


def _tile(n, pref):
    for t in pref:
        if n % t == 0:
            return t
    return n


def _rows_of(a):
    return a[0] if isinstance(a, tuple) else a


def _mm_nn(name, a, w, out_dtype, res=None, precision=None):
    M, K = _rows_of(a).shape
    N = w.shape[1]
    tm = _tile(M, BIG_TILES if K <= 2048 else BIG_TILES[1:])
    tn = _tile(N, BIG_TILES)
    return _mm(name, a, w, (M, N), out_dtype, (M // tm, N // tn, 1),
               pl.BlockSpec((tm, K), lambda i, j, k: (i, 0)), pl.BlockSpec((K, tn), lambda i, j, k: (0, j)),
               pl.BlockSpec((tm, tn), lambda i, j, k: (i, j)), NN, (tm, tn), res=res, precision=precision)


def _mm_nt(name, g, w, out_dtype, res=None, precision=None, scatter=(), post=None):
    M, N = g.shape
    K = w.shape[0]
    tm, tk = _tile(M, BIG_TILES if post is None else BIG_TILES[1:]), _tile(K, (1024, 1408, 512, 256, 128))
    tn = _tile(N, (1536,) + BIG_TILES)
    return _mm(name, g, w, (M, K), out_dtype, (M // tm, K // tk, N // tn),
               pl.BlockSpec((tm, tn), lambda i, j, k: (i, k)), pl.BlockSpec((tk, tn), lambda i, j, k: (j, k)),
               pl.BlockSpec((tm, tk), lambda i, j, k: (i, j)), NT, (tm, tk), res=res, precision=precision,
               scatter=scatter, post=post)


def _mm_tn(name, a, g, out_dtype=F32, precision=None):
    T, K = _rows_of(a).shape
    N = g.shape[1]
    tk, tn = _tile(K, (1024, 1408, 512, 256, 128)), _tile(N, BIG_TILES)
    assert tk == K or not isinstance(a, tuple)
    tt = _tile(T, BIG_TILES)
    return _mm(name, a, g, (K, N), out_dtype, (K // tk, N // tn, T // tt),
               pl.BlockSpec((tt, tk), lambda i, j, k: (k, i)), pl.BlockSpec((tt, tn), lambda i, j, k: (k, j)),
               pl.BlockSpec((tk, tn), lambda i, j, k: (i, j)), TN, (tk, tn), precision=precision)


def _mm_up(name, n, wup, layer):
    T = _rows_of(n).shape[0]
    tm = _tile(T, BIG_TILES)
    return _mm(name, n, wup, (T, DFF2), BF16, (T // tm, N_CHIPS, 1),
               pl.BlockSpec((tm, D), lambda i, j, k: (i, 0)),
               pl.BlockSpec((None, None, D, DFF2_SHARD), lambda i, j, k: (j, layer, 0, 0)),
               pl.BlockSpec((tm, DFF2_SHARD), lambda i, j, k: (i, j)), NN, (tm, DFF2_SHARD))


def _mm_up_nt(name, du, wup, layer, post):
    T = du.shape[0]
    tm, tk = _tile(T, BIG_TILES[1:]), D
    return _mm(name, du, wup, (T, D), F32, (T // tm, D // tk, N_CHIPS),
               pl.BlockSpec((tm, DFF2_SHARD), lambda i, j, k: (i, k)),
               pl.BlockSpec((None, None, tk, DFF2_SHARD), lambda i, j, k: (k, layer, j, 0)),
               pl.BlockSpec((tm, tk), lambda i, j, k: (i, j)), NT, (tm, tk), post=post)


def _mm_up_tn(name, n, du, layer, into):
    T = _rows_of(n).shape[0]
    tk, tt = D, _tile(T, BIG_TILES)
    return _mm(name, n, du, (N_CHIPS, 2, D, DFF2_SHARD), F32, (D // tk, N_CHIPS, T // tt),
               pl.BlockSpec((tt, tk), lambda i, j, k: (k, i)), pl.BlockSpec((tt, DFF2_SHARD), lambda i, j, k: (k, j)),
               pl.BlockSpec((None, None, tk, DFF2_SHARD), lambda i, j, k: (j, layer, i, 0)), TN, (tk, DFF2_SHARD),
               into=into)


def _mm_down_tn(name, act, dout, layer, into):
    T = act.shape[0]
    tk, tn, tt = 2 * DFF_SHARD, _tile(D, BIG_TILES), _tile(T, BIG_TILES)
    return _mm(name, act, dout, (2, 2, 2, DFF_SHARD, D), F32, (DFF // tk, D // tn, T // tt),
               pl.BlockSpec((tt, tk), lambda i, j, k: (k, i)), pl.BlockSpec((tt, tn), lambda i, j, k: (k, j)),
               pl.BlockSpec((None, 2, None, DFF_SHARD, tn), lambda i, j, k: (i, 0, layer, 0, j)), TN, (tk, tn),
               into=into)


def _sigmoid(x):
    return 0.5 * jnp.tanh(0.5 * x) + 0.5


def _silu(x):
    return x * _sigmoid(x)


def _softplus(x):
    return jnp.maximum(x, 0.0) + jnp.log(1.0 + jnp.exp(-jnp.abs(x)))


def _rms_core(h, w):
    return h * lax.rsqrt(jnp.mean(h * h, axis=-1, keepdims=True) + EPS) * w


def _shift_down(x, halo, s, i):
    if s == 0:
        return x
    tm = x.shape[0]
    rolled = pltpu.roll(x, s, 0)
    patch = pltpu.roll(jnp.where(i == 0, 0.0, halo), s, 0)
    row = lax.broadcasted_iota(jnp.int32, patch.shape, 0)
    top = jnp.where(row < s, patch, rolled[:SUBLANE])
    return jnp.concatenate([top, rolled[SUBLANE:]], axis=0) if tm > SUBLANE else top


def _shift_up(x, halo, s, i, n):
    if s == 0:
        return x
    tm = x.shape[0]
    rolled = pltpu.roll(x, tm - s, 0)
    patch = pltpu.roll(jnp.where(i == n - 1, 0.0, halo), SUBLANE - s, 0)
    row = lax.broadcasted_iota(jnp.int32, patch.shape, 0)
    bottom = jnp.where(row >= SUBLANE - s, patch, rolled[tm - SUBLANE:])
    return jnp.concatenate([rolled[:tm - SUBLANE], bottom], axis=0) if tm > SUBLANE else bottom


def _taps(x, halo, K, i):
    return [_shift_down(x, halo, K - 1 - j, i) for j in range(K)]


def _conv_fwd(taps, w):
    y = w[0:1, :] * taps[0]
    for j in range(1, len(taps)):
        y = y + w[j:j + 1, :] * taps[j]
    return y


def _conv_dx(dy, halo_next, w, i, n):
    K = w.shape[0]
    dx = w[K - 1:K, :] * dy
    for j in range(K - 1):
        dx = dx + w[j:j + 1, :] * _shift_up(dy, halo_next, K - 1 - j, i, n)
    return dx


def _conv_dw(dy, taps):
    rows = [jnp.sum(dy * tap, axis=0, keepdims=True) for tap in taps]
    return jnp.concatenate(rows + [jnp.zeros((SUBLANE - len(taps), dy.shape[1]), F32)], axis=0)


def _l2(x):
    return x * lax.rsqrt(jnp.sum(x * x, axis=-1, keepdims=True) + EPS)


def _gdn_post_core(yq, yk, yv, pb, pa, a_log, dtb):
    qn = tuple(_l2(_silu(a)) * (GDN_HD ** -0.5) for a in yq)
    kn = tuple(_l2(_silu(a)) for a in yk)
    v = _silu(yv)
    beta = _sigmoid(pb)
    g = -jnp.exp(a_log) * _softplus(pa + dtb)
    return qn, kn, v, beta, g


def _heads(x, n):
    return tuple(x[:, GDN_HD * h:GDN_HD * (h + 1)] for h in range(n))


def _gdn_pre_fwd(pm, pba, conv_w, a_log, dtb, tm=128):
    T = pm.shape[0]
    tm = min(tm, T)

    def fn(i, x, halo, pbav, cw, al, db):
        y = _conv_fwd(_taps(x.astype(F32), halo.astype(F32), 4, i), cw)
        qn, kn, v, beta, g = _gdn_post_core(_heads(y[:, :GDN_QK], 8), _heads(y[:, GDN_QK:2 * GDN_QK], 8),
                                            y[:, 2 * GDN_QK:], pbav[:, :LANE], pbav[:, LANE:], al, db)
        return jnp.stack(qn), jnp.stack(kn), jnp.stack(_heads(v, GDN_V_HEADS)), beta, g

    ins = [(pm, "row", (GDN_CONV, 0)), (pm, "prev", (GDN_CONV, 0)), (pba, "row", None),
           (conv_w, "full", None), (a_log, "full", None), (dtb, "full", None)]
    outs = [((GDN_QK_HEADS, T, GDN_HD), BF16, "row"), ((GDN_QK_HEADS, T, GDN_HD), BF16, "row"),
            ((GDN_V_HEADS, T, GDN_HD), BF16, "row"), ((T, LANE), F32, "row"), ((T, LANE), F32, "row")]
    return _rowcall("gdn_pre_fwd", fn, T, tm, ins, outs)


def _gdn_pre_bwd(pm, pba, conv_w, a_log, dtb, dqn, dkn, dv, dbeta, dg, tm=128):
    T = pm.shape[0]
    tm = min(tm, T)

    def fn(i, x, halo, pbav, cw, al, db, dqv, dkv, dvv, dbv, dgv):
        taps = _taps(x.astype(F32), halo.astype(F32), 4, i)
        y = _conv_fwd(taps, cw)
        prim = (_heads(y[:, :GDN_QK], 8), _heads(y[:, GDN_QK:2 * GDN_QK], 8), y[:, 2 * GDN_QK:],
                pbav[:, :LANE], pbav[:, LANE:], al, db)
        _, vjp = jax.vjp(_gdn_post_core, *prim)
        cot = (tuple(dqv[h].astype(F32) for h in range(8)), tuple(dkv[h].astype(F32) for h in range(8)),
               jnp.concatenate([dvv[h].astype(F32) for h in range(GDN_V_HEADS)], axis=1), dbv, dgv)
        dyq, dyk, dyv, dpb, dpa, dal, ddb = vjp(cot)
        dy = jnp.concatenate(list(dyq) + list(dyk) + [dyv], axis=1)
        dcw = _conv_dw(dy, taps)
        return dy, jnp.concatenate([dpb, dpa], axis=1), dcw, dal, ddb

    ins = [(pm, "row", (GDN_CONV, 0)), (pm, "prev", (GDN_CONV, 0)), (pba, "row", None),
           (conv_w, "full", None), (a_log, "full", None), (dtb, "full", None),
           (dqn, "row", None), (dkn, "row", None), (dv, "row", None), (dbeta, "row", None), (dg, "row", None)]
    outs = [((T, GDN_CONV), BF16, "row"), ((T, 2 * LANE), F32, "row"), ((SUBLANE, GDN_CONV), F32, "acc"),
            ((1, LANE), F32, "acc"), ((1, LANE), F32, "acc")]
    return _rowcall("gdn_pre_bwd", fn, T, tm, ins, outs)


def _gdn_conv_bwd(dy, dz, conv_w, tm=256):
    T = dy.shape[0]
    tm = min(tm, T)
    n = T // tm

    def fn(i, dyv, halo, dzv, cw):
        dx = _conv_dx(dyv.astype(F32), halo.astype(F32), cw, i, n)
        return (jnp.concatenate([dx.astype(BF16), dzv.astype(BF16)], axis=1),)

    ins = [(dy, "row", None), (dy, "next", None), (dz, "row", None), (conv_w, "full", None)]
    return _rowcall("gdn_conv_bwd", fn, T, tm, ins, [((T, GDN_MAIN), BF16, "row")])[0]


def _bdot(a, b, dims=NN):
    return lax.dot_general(a.astype(BF16), b.astype(BF16), (dims, ((), ())), preferred_element_type=F32)


BNN = ((2,), (1,))
BNT = ((2,), (2,))
BTN = ((1,), (1,))


def _bmm(a, b, dims=BNN):
    return lax.dot_general(a.astype(BF16), b.astype(BF16), (dims, ((0,), (0,))), preferred_element_type=F32)


def _bmm3(a, b):
    ah, bh = a.astype(BF16), b.astype(BF16)
    al, bl = (a - ah.astype(F32)).astype(BF16), (b - bh.astype(F32)).astype(BF16)
    dn = (BNN, ((0,), (0,)))
    return (lax.dot_general(ah, bh, dn, preferred_element_type=F32)
            + lax.dot_general(al, bh, dn, preferred_element_type=F32)
            + lax.dot_general(ah, bl, dn, preferred_element_type=F32))


def _tri_inv(m):
    C = m.shape[-1]
    r = lax.broadcasted_iota(jnp.int32, (C, C), 0)
    c = lax.broadcasted_iota(jnp.int32, (C, C), 1)
    t = jnp.where(r == c, 1.0, 0.0) - m
    pw = _bmm3(m, m)
    t = t + _bmm3(t, pw)
    for _ in range(int(math.log2(C)) - 2):
        pw = _bmm(pw, pw)
        t = t + _bmm(t, pw)
    return t


def _tri_inv_vjp(t, dt):
    tt = jnp.swapaxes(t, 1, 2)
    return -_bmm(_bmm(tt, dt), tt)


def _twice(a):
    return jnp.broadcast_to(a[:, None], (a.shape[0], 2) + a.shape[1:]).reshape((2 * a.shape[0],) + a.shape[1:])


def _gdn_gates(grow, brow):
    C = grow.shape[2]
    r = lax.broadcasted_iota(jnp.int32, (C, C), 0)
    c = lax.broadcasted_iota(jnp.int32, (C, C), 1)
    tril, eye = r >= c, r == c
    gcol = jnp.sum(jnp.where(eye, grow, 0.0), axis=2, keepdims=True)
    bcol = jnp.sum(jnp.where(eye, brow, 0.0), axis=2, keepdims=True)
    gc_col = jnp.sum(jnp.where(tril, grow, 0.0), axis=2, keepdims=True)
    gc_row = jnp.sum(jnp.where(r <= c, gcol, 0.0), axis=1, keepdims=True)
    gc_last = jnp.sum(grow, axis=2, keepdims=True)
    decay = jnp.where(tril, jnp.exp(jnp.where(tril, gc_col - gc_row, 0.0)), 0.0)
    return bcol, gc_col, gc_last, decay


def _gdn_m(k, bcol, decay):
    C = k.shape[1]
    strict = lax.broadcasted_iota(jnp.int32, (C, C), 0) > lax.broadcasted_iota(jnp.int32, (C, C), 1)
    return jnp.where(strict, bcol * _twice(_bmm(k, k, BNT)) * decay, 0.0)


def _gdn_rest(q, k, v, bcol, gc_col, gc_last, decay, t_mat, S):
    qk = _twice(_bmm(q, k, BNT))
    k2, q2 = _twice(k), _twice(q)
    egc = jnp.exp(gc_col)
    u = _bmm(t_mat, v * bcol)
    w = _bmm(t_mat, k2 * (bcol * egc))
    v_new = u - _bmm(w, S)
    o = _bmm(q2 * egc, S) + _bmm(qk * decay, v_new)
    s_new = S * jnp.exp(gc_last) + _bmm(k2 * jnp.exp(gc_last - gc_col), v_new, BTN)
    return o, s_new


def _gdn_tb(T):
    return min(256, T)


def _gate_rows(g):
    T = g.shape[0]
    g = g[:, :GDN_V_HEADS].reshape(T // GDN_CHUNK, GDN_CHUNK, GDN_V_HEADS)
    return g.transpose(0, 2, 1)[:, :, None, :]


def _gate_cols(g):
    nc = g.shape[0]
    g = g[:, :, 0, :].transpose(0, 2, 1).reshape(nc * GDN_CHUNK, GDN_V_HEADS)
    return jnp.pad(g, ((0, 0), (0, LANE - GDN_V_HEADS)))


def _gdn_fwd(qn, kn, v, g, beta, gather=None):
    T = qn.shape[1]
    tb = _gdn_tb(T)
    nc = tb // GDN_CHUNK
    nsteps = T // tb
    quarters, buffers = gather if gather is not None else ((), ())
    ng = len(quarters)
    shapes = [a.shape for a in quarters]
    splits = [True] * ng

    def body(*refs):
        q_ref, k_ref, v_ref, g_ref, b_ref = refs[:5]
        src = refs[5:5 + ng]
        o_ref, sall_ref, tall_ref = refs[5 + 2 * ng:8 + 2 * ng]
        dst = refs[8 + 2 * ng:8 + 3 * ng]
        s_scr = refs[8 + 3 * ng]
        step = pl.program_id(0)

        @pl.when(step == 0)
        def _():
            s_scr[...] = jnp.zeros(s_scr.shape, F32)
            if ng:
                for cp in _gather_copies(shapes, splits, src, dst, *refs[9 + 3 * ng:])[0]:
                    cp.start()

        def chunk(ci, carry):
            rows = pl.ds(pl.multiple_of(ci * GDN_CHUNK, GDN_CHUNK), GDN_CHUNK)
            s = s_scr[...]
            sall_ref[ci] = s
            q, k = q_ref[:, rows, :].astype(F32), k_ref[:, rows, :].astype(F32)
            bcol, gc_col, gc_last, decay = _gdn_gates(g_ref[ci], b_ref[ci])
            t_mat = _tri_inv(_gdn_m(k, bcol, decay)).astype(BF16)
            tall_ref[ci] = t_mat
            o, s_new = _gdn_rest(q, k, v_ref[:, rows, :].astype(F32), bcol, gc_col, gc_last, decay,
                                 t_mat.astype(F32), s)
            o_ref[:, rows, :] = o.astype(o_ref.dtype)
            s_scr[...] = s_new
            return carry

        lax.fori_loop(0, nc, chunk, 0)

        if ng:
            @pl.when(step == nsteps - 1)
            def _():
                _gather_arrival(shapes, splits, src, dst, *refs[9 + 3 * ng:])

    qk_spec = pl.BlockSpec((GDN_QK_HEADS, tb, GDN_HD), lambda i: (0, i, 0))
    v_spec = pl.BlockSpec((GDN_V_HEADS, tb, GDN_HD), lambda i: (0, i, 0))
    g_spec = pl.BlockSpec((nc, GDN_V_HEADS, 1, GDN_CHUNK), lambda i: (i, 0, 0, 0))
    anywhere = pl.BlockSpec(memory_space=pl.ANY)
    return pl.pallas_call(
        body, name="gdn_fwd", grid=(nsteps,),
        in_specs=[qk_spec, qk_spec, v_spec, g_spec, g_spec] + [anywhere] * (2 * ng),
        out_specs=[v_spec, pl.BlockSpec((nc, GDN_V_HEADS, GDN_HD, GDN_HD), lambda i: (i, 0, 0, 0)),
                   pl.BlockSpec((nc, GDN_V_HEADS, GDN_CHUNK, GDN_CHUNK), lambda i: (i, 0, 0, 0))] + [anywhere] * ng,
        out_shape=[jax.ShapeDtypeStruct((GDN_V_HEADS, T, GDN_HD), BF16),
                   jax.ShapeDtypeStruct((T // GDN_CHUNK, GDN_V_HEADS, GDN_HD, GDN_HD), F32),
                   jax.ShapeDtypeStruct((T // GDN_CHUNK, GDN_V_HEADS, GDN_CHUNK, GDN_CHUNK), BF16)]
        + [jax.ShapeDtypeStruct(b.shape, b.dtype) for b in buffers],
        input_output_aliases={5 + ng + a: 3 + a for a in range(ng)},
        scratch_shapes=[pltpu.VMEM((GDN_V_HEADS, GDN_HD, GDN_HD), F32)]
        + ([pltpu.SemaphoreType.DMA((6 * ng,)), pltpu.SemaphoreType.DMA((6 * ng,))] if ng else []),
        compiler_params=_params(("arbitrary",)),
    )(qn, kn, v, g, beta, *quarters, *buffers)


def _gdn_bwd(qn, kn, v, g, beta, sall, tall, do, scatter=()):
    T = qn.shape[1]
    tb = _gdn_tb(T)
    nc = tb // GDN_CHUNK
    nb = T // tb
    ns = len(scatter)

    def body(*refs):
        q_ref, k_ref, v_ref, g_ref, b_ref, sall_ref, tall_ref, do_ref = refs[:8]
        dq_ref, dk_ref, dv_ref, dg_ref, db_ref = refs[8 + ns:13 + ns]
        ds_scr = refs[13 + 2 * ns]
        comm = (refs[8:8 + ns], refs[13 + ns:13 + 2 * ns], *refs[14 + 2 * ns:])
        step = pl.program_id(0)

        @pl.when(step == 0)
        def _():
            ds_scr[...] = jnp.zeros(ds_scr.shape, F32)
            if ns:
                for cp in _scatter_copies(*comm):
                    cp.start()

        def chunk(cr, carry):
            ci = nc - 1 - cr
            rows = pl.ds(pl.multiple_of(ci * GDN_CHUNK, GDN_CHUNK), GDN_CHUNK)
            k, t_mat = k_ref[:, rows, :].astype(F32), tall_ref[ci].astype(F32)
            (bcol, gc_col, gc_last, decay), vjp_gates = jax.vjp(_gdn_gates, g_ref[ci], b_ref[ci])
            _, vjp = jax.vjp(_gdn_rest, q_ref[:, rows, :].astype(F32), k, v_ref[:, rows, :].astype(F32),
                             bcol, gc_col, gc_last, decay, t_mat, sall_ref[ci])
            dq, dk, dv, dbcol, dgc_col, dgc_last, ddecay, dt, ds = vjp((do_ref[:, rows, :].astype(F32), ds_scr[...]))
            _, vjp_m = jax.vjp(_gdn_m, k, bcol, decay)
            dk_m, dbcol_m, ddecay_m = vjp_m(_tri_inv_vjp(t_mat, dt))
            dg, db = vjp_gates((dbcol + dbcol_m, dgc_col, dgc_last, ddecay + ddecay_m))
            ds_scr[...] = ds
            dq_ref[:, rows, :] = dq
            dk_ref[:, rows, :] = dk + dk_m
            dv_ref[:, rows, :] = dv
            dg_ref[ci] = dg
            db_ref[ci] = db
            return carry

        lax.fori_loop(0, nc, chunk, 0)

        if ns:
            @pl.when(step == nb - 1)
            def _():
                copies = _scatter_copies(*comm)
                for cp in copies:
                    cp.wait_recv()
                for cp in copies:
                    cp.wait_send()

    qk_spec = pl.BlockSpec((GDN_QK_HEADS, tb, GDN_HD), lambda i: (0, nb - 1 - i, 0))
    v_spec = pl.BlockSpec((GDN_V_HEADS, tb, GDN_HD), lambda i: (0, nb - 1 - i, 0))
    g_spec = pl.BlockSpec((nc, GDN_V_HEADS, 1, GDN_CHUNK), lambda i: (nb - 1 - i, 0, 0, 0))
    s_spec = pl.BlockSpec((nc, GDN_V_HEADS, GDN_HD, GDN_HD), lambda i: (nb - 1 - i, 0, 0, 0))
    t_spec = pl.BlockSpec((nc, GDN_V_HEADS, GDN_CHUNK, GDN_CHUNK), lambda i: (nb - 1 - i, 0, 0, 0))
    anywhere = pl.BlockSpec(memory_space=pl.ANY)
    return pl.pallas_call(
        body, name="gdn_bwd", grid=(nb,),
        in_specs=[qk_spec, qk_spec, v_spec, g_spec, g_spec, s_spec, t_spec, v_spec] + [anywhere] * ns,
        out_specs=[qk_spec, qk_spec, v_spec, g_spec, g_spec] + [anywhere] * ns,
        out_shape=[jax.ShapeDtypeStruct((GDN_QK_HEADS, T, GDN_HD), F32),
                   jax.ShapeDtypeStruct((GDN_QK_HEADS, T, GDN_HD), F32),
                   jax.ShapeDtypeStruct((GDN_V_HEADS, T, GDN_HD), F32),
                   jax.ShapeDtypeStruct(g.shape, F32), jax.ShapeDtypeStruct(g.shape, F32)]
        + _scatter_shapes(scatter),
        scratch_shapes=[pltpu.VMEM((GDN_V_HEADS, GDN_HD, GDN_HD), F32)]
        + ([pltpu.SemaphoreType.DMA((3 * ns,)), pltpu.SemaphoreType.DMA((3 * ns,))] if ns else []),
        compiler_params=_params(("arbitrary",)),
    )(qn, kn, v, g, beta, sall, tall, do, *scatter)


def _gnorm_core(o, z, w):
    return tuple(_rms_core(oh, w) * _silu(zh) for oh, zh in zip(o, z))


def _gnorm_fwd(o, pm, w, tm=256):
    T = pm.shape[0]
    tm = min(tm, T)

    def fn(i, ov, zv, wv):
        zf = zv.astype(F32)
        out = _gnorm_core(tuple(ov[h].astype(F32) for h in range(GDN_V_HEADS)), _heads(zf, GDN_V_HEADS), wv)
        return (jnp.concatenate(out, axis=1),)

    ins = [(o, "row", None), (pm, "row", (GDN_V, 2)), (w, "full", None)]
    return _rowcall("gnorm_fwd", fn, T, tm, ins, [((T, GDN_V), BF16, "row")])[0]


def _gnorm_bwd(o, pm, w, don, tm=128, swap=()):
    T = pm.shape[0]
    tm = min(tm, T)

    def fn(i, ov, zv, wv, dv):
        zf, df = zv.astype(F32), dv.astype(F32)
        _, vjp = jax.vjp(_gnorm_core, tuple(ov[h].astype(F32) for h in range(GDN_V_HEADS)),
                         _heads(zf, GDN_V_HEADS), wv)
        do, dz, dw = vjp(_heads(df, GDN_V_HEADS))
        return jnp.stack(do), jnp.concatenate(dz, axis=1), dw

    ins = [(o, "row", None), (pm, "row", (GDN_V, 2)), (w, "full", None), (don, "row", None)]
    outs = [((GDN_V_HEADS, T, GDN_HD), BF16, "row"), ((T, GDN_V), BF16, "row"), ((1, GDN_HD), F32, "acc")]
    return _rowcall("gnorm_bwd", fn, T, tm, ins, outs, swap=swap)


def _ffn_act_fwd(name, up, conv_w, conv_b, tm=256):
    T = up.shape[0]
    tm = min(tm, T)

    def fn(i, x, halo, cw, cb):
        u = _conv_fwd(_taps(x.astype(F32), halo.astype(F32), 3, i), cw) + cb
        return (_silu(u[:, :DFF]) * u[:, DFF:],)

    ins = [(up, "row", None), (up, "prev", None), (conv_w, "full", None), (conv_b, "full", None)]
    return _rowcall(name, fn, T, tm, ins, [((T, DFF), BF16, "row")])[0]


def _ffn_act_bwd(name, up, conv_w, conv_b, dact, tm=128):
    T = up.shape[0]
    tm = min(tm, T)

    def fn(i, x, halo, cw, cb, da):
        taps = _taps(x.astype(F32), halo.astype(F32), 3, i)
        da = da.astype(F32)
        u = _conv_fwd(taps, cw) + cb
        gate, val = u[:, :DFF], u[:, DFF:]
        sg = _sigmoid(gate)
        dgate = da * val * sg * (1.0 + gate * (1.0 - sg))
        dval = da * gate * sg
        du = jnp.concatenate([dgate, dval], axis=1)
        return du, _conv_dw(du, taps), jnp.sum(du, axis=0, keepdims=True)

    ins = [(up, "row", None), (up, "prev", None), (conv_w, "full", None), (conv_b, "full", None),
           (dact, "row", None)]
    outs = [((T, DFF2), BF16, "row"), ((SUBLANE, DFF2), F32, "acc"), ((1, DFF2), F32, "acc")]
    return _rowcall(name, fn, T, tm, ins, outs)


def _ffn_conv_bwd(name, du, conv_w, tm=256):
    T = du.shape[0]
    tm = min(tm, T)
    n = T // tm

    def fn(i, dv, halo, cw):
        return (_conv_dx(dv.astype(F32), halo.astype(F32), cw, i, n),)

    ins = [(du, "row", None), (du, "next", None), (conv_w, "full", None)]
    return _rowcall(name, fn, T, tm, ins, [((T, DFF2), BF16, "row")])[0]


GROUP_ROWS = SWA_GROUP * SWA_BLOCK


def _attn_core(q, kp, kc, vp, vc, bias, sink, mask):
    kcat = jnp.concatenate([kp, kc], axis=0)
    vcat = jnp.concatenate([vp, vc], axis=0)
    s = _bdot(q * (SWA_HD ** -0.5), kcat, NT) + bias
    s = jnp.where(mask, s, NEG_INF)
    m = lax.stop_gradient(jnp.maximum(jnp.max(s, axis=-1, keepdims=True), sink))
    p = jnp.exp(s - m)
    denom = jnp.sum(p, axis=-1, keepdims=True) + jnp.exp(sink - m)
    return _bdot(p / denom, vcat)


def _attn_mask(i):
    qi = lax.broadcasted_iota(jnp.int32, (GROUP_ROWS, 2 * SWA_BLOCK), 0) & (SWA_BLOCK - 1)
    ki = lax.broadcasted_iota(jnp.int32, (GROUP_ROWS, 2 * SWA_BLOCK), 1)
    dist = qi + SWA_BLOCK - ki
    return (dist >= 0) & (dist < SWA_BLOCK) & ((ki >= SWA_BLOCK) | (i > 0))


def _head_cols(h):
    return slice(h * SWA_HD, (h + 1) * SWA_HD)


def _stacked_heads(ref, j):
    return jnp.concatenate([ref[:, _head_cols(SWA_GROUP * j + g)].astype(F32) for g in range(SWA_GROUP)], axis=0)


def _flat_operands(j, q_ref, kvc_ref, kvp_ref, b_ref, s_ref):
    heads = slice(SWA_GROUP * j, SWA_GROUP * (j + 1))
    sink = jnp.concatenate([jnp.broadcast_to(s_ref[j, g:g + 1, 0:1], (SWA_BLOCK, 1)) for g in range(SWA_GROUP)],
                           axis=0)
    k_cols, v_cols = _head_cols(j), _head_cols(SWA_KV_HEADS + j)
    return (_stacked_heads(q_ref, j), kvp_ref[:, k_cols].astype(F32), kvc_ref[:, k_cols].astype(F32),
            kvp_ref[:, v_cols].astype(F32), kvc_ref[:, v_cols].astype(F32),
            b_ref[heads].reshape(GROUP_ROWS, 2 * SWA_BLOCK), sink)


def _store_heads(ref, j, stacked):
    for g in range(SWA_GROUP):
        ref[:, _head_cols(SWA_GROUP * j + g)] = stacked[g * SWA_BLOCK:(g + 1) * SWA_BLOCK].astype(ref.dtype)


def _attn_fwd_flat(q, kv, bias, sinks):
    T = q.shape[0]
    nb = T // SWA_BLOCK

    def body(q_ref, kvc_ref, kvp_ref, b_ref, s_ref, o_ref):
        mask = _attn_mask(pl.program_id(0))
        operands = [_flat_operands(j, q_ref, kvc_ref, kvp_ref, b_ref, s_ref) for j in range(SWA_KV_HEADS)]
        outs = [_attn_core(*ops, mask) for ops in operands]
        for j in range(SWA_KV_HEADS):
            _store_heads(o_ref, j, outs[j])

    q_spec = pl.BlockSpec((SWA_BLOCK, q.shape[1]), lambda i: (i, 0))
    cur = pl.BlockSpec((SWA_BLOCK, kv.shape[1]), lambda i: (i, 0))
    prev = pl.BlockSpec((SWA_BLOCK, kv.shape[1]), lambda i: (jnp.maximum(i - 1, 0), 0))
    return pl.pallas_call(
        body, name="attn_fwd", grid=(nb,),
        in_specs=[q_spec, cur, prev, pl.BlockSpec(bias.shape, lambda i: (0, 0, 0)),
                  pl.BlockSpec(sinks.shape, lambda i: (0, 0, 0))],
        out_specs=q_spec, out_shape=jax.ShapeDtypeStruct(q.shape, BF16),
        compiler_params=_params(("arbitrary",)),
    )(q, kv, kv, bias, sinks)


def _attn_bwd_flat(q, kv, bias, sinks, do):
    T = q.shape[0]
    nb = T // SWA_BLOCK

    def body(q_ref, kvc_ref, kvp_ref, b_ref, s_ref, do_ref, dq_ref, dkv_ref, db_ref, dsk_ref, carry):
        i = pl.program_id(0)

        @pl.when(i < nb)
        def _():
            mask = _attn_mask(i)
            operands = [_flat_operands(j, q_ref, kvc_ref, kvp_ref, b_ref, s_ref) for j in range(SWA_KV_HEADS)]
            cots = [_stacked_heads(do_ref, j) for j in range(SWA_KV_HEADS)]
            grads = [jax.vjp(functools.partial(_attn_core, mask=mask), *ops)[1](cot)
                     for ops, cot in zip(operands, cots)]
            for j, (dq, dkp, dkc, dvp, dvc, db, dsc) in enumerate(grads):
                heads = slice(SWA_GROUP * j, SWA_GROUP * (j + 1))
                k_cols, v_cols = _head_cols(j), _head_cols(SWA_KV_HEADS + j)
                _store_heads(dq_ref, j, dq)
                db = db.reshape(SWA_GROUP, SWA_BLOCK, 2 * SWA_BLOCK)
                dsk = jnp.concatenate(
                    [jnp.broadcast_to(jnp.sum(dsc[g * SWA_BLOCK:(g + 1) * SWA_BLOCK], axis=0, keepdims=True),
                                      (1, LANE)) for g in range(SWA_GROUP)], axis=0)

                @pl.when(i == 0)
                def _():
                    db_ref[heads] = db
                    dsk_ref[j] = dsk

                @pl.when(i > 0)
                def _():
                    db_ref[heads] += db
                    dsk_ref[j] += dsk
                    dkv_ref[:, k_cols] = (carry[:, k_cols] + dkp).astype(dkv_ref.dtype)
                    dkv_ref[:, v_cols] = (carry[:, v_cols] + dvp).astype(dkv_ref.dtype)

                carry[:, k_cols] = dkc
                carry[:, v_cols] = dvc

        @pl.when(i == nb)
        def _():
            dkv_ref[...] = carry[...].astype(dkv_ref.dtype)

    last = nb - 1
    q_spec = pl.BlockSpec((SWA_BLOCK, q.shape[1]), lambda i: (jnp.minimum(i, last), 0))
    cur = pl.BlockSpec((SWA_BLOCK, kv.shape[1]), lambda i: (jnp.minimum(i, last), 0))
    prev = pl.BlockSpec((SWA_BLOCK, kv.shape[1]), lambda i: (jnp.clip(i - 1, 0, last), 0))
    b_spec = pl.BlockSpec(bias.shape, lambda i: (0, 0, 0))
    s_spec = pl.BlockSpec(sinks.shape, lambda i: (0, 0, 0))
    return pl.pallas_call(
        body, name="attn_bwd", grid=(nb + 1,),
        in_specs=[q_spec, cur, prev, b_spec, s_spec, q_spec],
        out_specs=[q_spec, prev, b_spec, s_spec],
        out_shape=[jax.ShapeDtypeStruct(q.shape, BF16), jax.ShapeDtypeStruct(kv.shape, BF16),
                   jax.ShapeDtypeStruct(bias.shape, F32), jax.ShapeDtypeStruct(sinks.shape, F32)],
        scratch_shapes=[pltpu.VMEM((SWA_BLOCK, kv.shape[1]), F32)],
        compiler_params=_params(("arbitrary",)),
    )(q, kv, kv, bias, sinks, do)


def _rel_onehot():
    qi = jnp.arange(SWA_BLOCK)[:, None]
    ki = jnp.arange(2 * SWA_BLOCK)[None, :]
    n = jnp.maximum(qi + SWA_BLOCK - ki, 0)
    max_exact = REL_BUCKETS // 2
    nf = jnp.maximum(n, 1).astype(F32)
    large = max_exact + (jnp.log(nf / max_exact) / math.log(REL_MAX_DISTANCE / max_exact)
                         * (REL_BUCKETS - max_exact)).astype(jnp.int32)
    bucket = jnp.where(n < max_exact, n, jnp.minimum(large, REL_BUCKETS - 1)).reshape(-1)
    return (bucket[None, :] == jnp.arange(REL_BUCKETS)[:, None]).astype(F32)


def _final(h, w, target, tm=256):
    T = h.shape[0]
    tm = min(tm, T)

    def fn(i, hv, wv, tv):
        y, vjp = jax.vjp(_rms_core, hv, wv)
        err = y - tv
        dh, dw = vjp(err * (1.0 / D))
        part = 0.5 * jnp.sum(jnp.sum(err * err, axis=1, keepdims=True) * (1.0 / D), axis=0, keepdims=True)
        return jnp.broadcast_to(part, (SUBLANE, LANE)), dh, dw

    ins = [(h, "row", None), (w, "full", None), (target, "row", None)]
    outs = [((SUBLANE, LANE), F32, "acc"), ((T, D), F32, "row"), ((1, D), F32, "acc")]
    return _rowcall("final", fn, T, tm, ins, outs)


def _ffn_fwd(tag, h, P, layer):
    n = (h, P["ffn_norm_w"][layer:layer + 1])
    up = _mm_up(f"{tag}_up", n, P["w_up"], layer)
    act = _ffn_act_fwd(f"{tag}_act", up, P["ffn_conv_w"][layer], P["ffn_conv_b"][layer:layer + 1])
    out = _mm_nn(f"{tag}_down", act, P["w_down"][layer], F32, res=h)
    return out, (n, up, act)


def _ffn_bwd(tag, h, saved, dout, P, layer, into=(None, None)):
    n, up, act = saved
    cw, cb = P["ffn_conv_w"][layer], P["ffn_conv_b"][layer:layer + 1]
    dact = _mm_nt(f"{tag}_down_dx", dout, P["w_down"][layer], BF16)
    g_down = _mm_down_tn(f"{tag}_down_dw", act, dout, layer, into[1])
    du, dcw, dcb = _ffn_act_bwd(f"{tag}_act_bwd", up, cw, cb, dact)
    dup = _ffn_conv_bwd(f"{tag}_conv_bwd", du, cw)
    g_up = _mm_up_tn(f"{tag}_up_dw", n, dup, layer, into[0])
    dh, dnw = _mm_up_nt(f"{tag}_up_dx", dup, P["w_up"], layer, post=(h, P["ffn_norm_w"][layer:layer + 1], dout))
    return dh, dict(w_down=g_down, w_up=g_up, conv_w=dcw[:3], conv_b=dcb, norm_w=dnw)


def _local_step(x, target, P, late=None, pair_sums=None):
    T = x.shape[0]
    n0 = (x, P["a_norm_w"])
    pm = _mm_nn("gdn_in", n0, P["w_in_main"], BF16)
    pba = _mm_nn("gdn_in_ba", n0, P["w_in_ba"], F32)
    qn, kn, v, beta, g = _gdn_pre_fwd(pm, pba, P["a_conv_w"], P["a_log"], P["dt_bias"])
    g_rows, beta_rows = _gate_rows(g), _gate_rows(beta)
    o, sall, tall, *gathered = _gdn_fwd(qn, kn, v, g_rows, beta_rows, gather=late)
    if late is not None:
        P = {**P, **_late_weights(gathered)}
    on = _gnorm_fwd(o, pm, P["a_out_norm_w"])
    h1 = _mm_nn("gdn_out", on, P["w_out"], F32, res=x)
    h2, ffn0 = _ffn_fwd("ffn0", h1, P, 0)
    nkv = (h2, P["kv_norm_w"])
    kv = _mm_nn("kv_proj", nkv, P["w_kv"], BF16)
    nb = (h2, P["b_norm_w"])
    qp = _mm_nn("q_proj", nb, P["w_q"], BF16)
    onehot = _rel_onehot()
    bias = _mm_nn("rel_bias", P["rel_table_t"], onehot, F32, precision=HIGHEST)
    bias = bias.reshape(SWA_Q_HEADS, SWA_BLOCK, 2 * SWA_BLOCK)
    oa = _attn_fwd_flat(qp, kv, bias, P["sinks"])
    h3 = _mm_nn("o_proj", oa, P["w_o"], F32, res=h2)
    h4, ffn1 = _ffn_fwd("ffn1", h3, P, 1)
    loss, dh4, d_final = _final(h4, P["final_norm_w"], target)

    dh3, gf1 = _ffn_bwd("ffn1", h3, ffn1, dh4, P, 1)
    doa = _mm_nt("o_proj_dx", dh3, P["w_o"], BF16)
    g_wo = _mm_tn("o_proj_dw", oa, dh3)
    dqp, dkv, dbias, dsinks = _attn_bwd_flat(qp, kv, bias, P["sinks"], doa)
    g_wq = _mm_tn("q_proj_dw", nb, dqp)
    g_wkv = _mm_tn("kv_proj_dw", nkv, dkv)
    dh2, d_bnorm = _mm_nt("q_proj_dx", dqp, P["w_q"], F32, post=(h2, P["b_norm_w"], dh3))
    dh2, d_kvnorm = _mm_nt("kv_proj_dx", dkv, P["w_kv"], F32, post=(h2, P["kv_norm_w"], dh2))
    g_table = _mm_nt("rel_bias_dw", onehot, dbias.reshape(SWA_Q_HEADS, -1), F32, precision=HIGHEST)
    dh1, gf0 = _ffn_bwd("ffn0", h1, ffn0, dh2, P, 0, into=(gf1["w_up"], gf1["w_down"]))
    don = _mm_nt("gdn_out_dx", dh1, P["w_out"], BF16)
    g_wout = _mm_tn("gdn_out_dw", on, dh1)
    ready = dict(a_w_out=g_wout, w_kv=g_wkv, b_w_q=g_wq, b_w_o=g_wo, ffn_w_up=gf0["w_up"], ffn_w_down=gf0["w_down"])
    names = [n for n in BIG if n in ready]
    whole = [_chip_major(n, ready[n]) for n in names] if pair_sums is not None else []
    do, dz, d_gnorm, *other = _gnorm_bwd(o, pm, P["a_out_norm_w"], don, swap=whole)
    pairs = pair_sums(names, whole, other) if pair_sums is not None else []
    dq, dk, dv, dg, dbeta, *parts = _gdn_bwd(qn, kn, v, g_rows, beta_rows, sall, tall, do, scatter=pairs)
    dy, dpba, d_aconv, d_alog, d_dtb = _gdn_pre_bwd(pm, pba, P["a_conv_w"], P["a_log"], P["dt_bias"],
                                                    dq, dk, dv, _gate_cols(dbeta), _gate_cols(dg))
    dpm = _gdn_conv_bwd(dy, dz, P["a_conv_w"])
    g_win_main = _mm_tn("gdn_in_dw", n0, dpm)
    g_win_ba = _mm_tn("gdn_in_ba_dw", n0, dpba)
    nh = GDN_V_HEADS
    g_win = jnp.concatenate([g_win_main, g_win_ba[:, :nh], g_win_ba[:, LANE:LANE + nh]], axis=1)
    last_whole = [_chip_major("a_w_in", g_win)]
    last_pair = pair_sums(["a_w_in"], last_whole, _pair_swap(last_whole, "late")) if pair_sums is not None else []
    dn0 = _mm_nt("gdn_in_dx", dpm, P["w_in_main"], F32, scatter=last_pair)
    dn0, last_parts = (dn0[0], dn0[1:]) if last_pair else (dn0, [])
    dx, d_anorm = _mm_nt("gdn_in_ba_dx", dpba, P["w_in_ba"], F32, res=dn0, post=(x, P["a_norm_w"], dh1))

    nh = GDN_V_HEADS
    grads = dict(
        a_norm_w=d_anorm,
        a_w_in=g_win,
        a_conv_w=d_aconv[:4], a_a_log=d_alog[:, :nh], a_dt_bias=d_dtb[:, :nh], a_out_norm_w=d_gnorm,
        a_w_out=g_wout, kv_norm_w=d_kvnorm, w_kv=g_wkv, b_norm_w=d_bnorm, b_w_q=g_wq,
        b_sinks=dsinks[:, :, 0].reshape(1, SWA_Q_HEADS), b_w_o=g_wo, rel_bias_table=g_table,
        ffn_norm_w=jnp.concatenate([gf0["norm_w"], gf1["norm_w"]], axis=0),
        ffn_w_up=gf0["w_up"],
        ffn_conv_w=jnp.stack([gf0["conv_w"], gf1["conv_w"]], axis=0),
        ffn_conv_b=jnp.concatenate([gf0["conv_b"], gf1["conv_b"]], axis=0),
        ffn_w_down=gf0["w_down"],
        final_norm_w=d_final,
    )
    scattered = dict(zip([n for n in BIG if n in ready], zip(pairs, parts)))
    scattered.update(zip(["a_w_in"], zip(last_pair, last_parts)))
    return loss, dx, grads, scattered


HBM_SPEC = pl.BlockSpec(memory_space=pltpu.HBM)
VMEM_SPEC = pl.BlockSpec(memory_space=pltpu.VMEM)


def _coords():
    return lax.axis_index("x"), lax.axis_index("y"), lax.axis_index("c")


def _remote(src, dst, send_sem, recv_sem, device):
    return pltpu.make_async_remote_copy(src_ref=src, dst_ref=dst, send_sem=send_sem, recv_sem=recv_sem,
                                        device_id=device, device_id_type=MESH)


def _other_chips(x, y):
    return [(1 - x, y), (x, 1 - y), (1 - x, 1 - y)]


def _gather_copies(shapes, split, ins, outs, send_sems, recv_sems):
    x, y, c = _coords()
    p = 2 * x + y
    ici, forwards, from_sibling = [], [], []
    for a, shape in enumerate(shapes):
        h = shape[0] // 2
        for j, chip in enumerate(_other_chips(x, y)):
            q = 2 * chip[0] + chip[1]
            if split[a]:
                mine, theirs = pl.ds(c * h, h), pl.ds((1 - c) * h, h)
                ici.append(_remote(ins[a].at[mine], outs[a].at[p, mine], send_sems.at[6 * a + j],
                                   recv_sems.at[6 * a + j], (*chip, c)))
                land = outs[a].at[q, mine]
                forwards.append(_remote(land, land, send_sems.at[6 * a + 3 + j], recv_sems.at[6 * a + 3 + j],
                                        (x, y, 1 - c)))
                land = outs[a].at[q, theirs]
                from_sibling.append(_remote(land, land, send_sems.at[6 * a + 3 + j], recv_sems.at[6 * a + 3 + j],
                                            (x, y, 1 - c)))
            else:
                ici.append(_remote(ins[a], outs[a].at[p], send_sems.at[6 * a + j], recv_sems.at[6 * a + j],
                                   (*chip, c)))
                forwards.append(None)
    return ici, forwards, from_sibling


def _gather_arrival(shapes, split, ins, outs, send_sems, recv_sems):
    x, y, c = _coords()
    ici, forwards, from_sibling = _gather_copies(shapes, split, ins, outs, send_sems, recv_sems)
    k = 0
    for a, shape in enumerate(shapes):
        h = shape[0] // 2
        for j, chip in enumerate(_other_chips(x, y)):
            q = 2 * chip[0] + chip[1]
            land = outs[a].at[q, pl.ds(c * h, h)] if split[a] else outs[a].at[q]
            _remote(land, land, send_sems.at[6 * a + j], recv_sems.at[6 * a + j], (*chip, c)).wait_recv()
            if forwards[k] is not None:
                forwards[k].start()
            k += 1
    for cp in from_sibling:
        cp.wait_recv()
    for cp in ici + [f for f in forwards if f is not None]:
        cp.wait_send()


def _all_gather(arrs, split, remote):
    n = len(arrs)
    now = [a for a in range(n) if remote[a]]
    shapes = [arrs[a].shape for a in now]
    splits = [split[a] for a in now]

    def body(*refs):
        ins, outs, stage = refs[:n], refs[n:2 * n], refs[2 * n:3 * n]
        send_sems, recv_sems, in_sems, out_sems = refs[3 * n:]
        p = 2 * lax.axis_index("x") + lax.axis_index("y")
        gathered = ([ins[a] for a in now], [outs[a] for a in now], send_sems, recv_sems)
        loads = [pltpu.make_async_copy(ins[a], stage[a], in_sems.at[a]) for a in range(n)]
        for cp in loads:
            cp.start()
        for cp in _gather_copies(shapes, splits, *gathered)[0]:
            cp.start()
        stores = [pltpu.make_async_copy(stage[a], outs[a].at[p], out_sems.at[a]) for a in range(n)]
        for a in range(n):
            loads[a].wait()
            stores[a].start()
        _gather_arrival(shapes, splits, *gathered)
        for cp in stores:
            cp.wait()

    return pl.pallas_call(
        body, name="weights_all_gather", in_specs=[HBM_SPEC] * n, out_specs=[HBM_SPEC] * n,
        out_shape=[jax.ShapeDtypeStruct((N_CHIPS,) + a.shape, a.dtype) for a in arrs],
        scratch_shapes=[pltpu.VMEM(a.shape, a.dtype) for a in arrs]
        + [pltpu.SemaphoreType.DMA((6 * len(now),)), pltpu.SemaphoreType.DMA((6 * len(now),)),
           pltpu.SemaphoreType.DMA((n,)), pltpu.SemaphoreType.DMA((n,))],
        compiler_params=pltpu.CompilerParams(vmem_limit_bytes=VMEM_LIMIT),
    )(*arrs)


PAIR_SWAP_PIECES = 2


def _swap_copies(shapes, ins, other, send_sems, recv_sems):
    x, y, c = _coords()
    copies = []
    for a, shape in enumerate(shapes):
        h = shape[1] // 2
        piece = h // PAIR_SWAP_PIECES
        for q in range(N_CHIPS):
            for r in range(PAIR_SWAP_PIECES):
                k = (a * N_CHIPS + q) * PAIR_SWAP_PIECES + r
                copies.append(_remote(ins[a].at[q, pl.ds((1 - c) * h + r * piece, piece)],
                                      other[a].at[q, pl.ds(r * piece, piece)], send_sems.at[k], recv_sems.at[k],
                                      (x, y, 1 - c)))
    return copies


def _swap_out_shapes(gs):
    return [jax.ShapeDtypeStruct((N_CHIPS, g.shape[1] // 2, g.shape[2]), g.dtype) for g in gs]


def _pair_swap(gs, tag):
    n = len(gs)
    shapes = [g.shape for g in gs]

    def body(*refs):
        copies = _swap_copies(shapes, refs[:n], refs[n:2 * n], *refs[2 * n:])
        for cp in copies:
            cp.start()
        for cp in copies:
            cp.wait()

    nsem = n * N_CHIPS * PAIR_SWAP_PIECES
    return pl.pallas_call(
        body, name=f"grads_pair_swap_{tag}", in_specs=[HBM_SPEC] * n, out_specs=[HBM_SPEC] * n,
        out_shape=_swap_out_shapes(gs),
        scratch_shapes=[pltpu.SemaphoreType.DMA((nsem,)), pltpu.SemaphoreType.DMA((nsem,))],
    )(*gs)


def _scatter_copies(ins, outs, send_sems, recv_sems):
    x, y, c = _coords()
    copies = []
    for a in range(len(ins)):
        for j, chip in enumerate(_other_chips(x, y)):
            q = 2 * chip[0] + chip[1]
            copies.append(_remote(ins[a].at[q], outs[a].at[j], send_sems.at[3 * a + j], recv_sems.at[3 * a + j],
                                  (*chip, c)))
    return copies


def _scatter_shapes(ps):
    return [jax.ShapeDtypeStruct((N_CHIPS - 1,) + a.shape[1:], a.dtype) for a in ps]


def _pair_share(rs):
    n = len(rs)

    def body(*refs):
        ins, outs, stage = refs[:n], refs[n:2 * n], refs[2 * n:3 * n]
        send_sems, recv_sems, in_sems, out_sems = refs[3 * n:]
        x, y, c = _coords()

        def mine(a):
            h = rs[a].shape[0]
            return outs[a].at[pl.ds(c * h, h)]

        loads = [pltpu.make_async_copy(ins[a], stage[a], in_sems.at[a]) for a in range(n)]
        for cp in loads:
            cp.start()
        sends = [_remote(ins[a], mine(a), send_sems.at[a], recv_sems.at[a], (x, y, 1 - c)) for a in range(n)]
        for cp in sends:
            cp.start()
        stores = [pltpu.make_async_copy(stage[a], mine(a), out_sems.at[a]) for a in range(n)]
        for a in range(n):
            loads[a].wait()
            stores[a].start()
        for a in range(n):
            h = rs[a].shape[0]
            land = outs[a].at[pl.ds((1 - c) * h, h)]
            _remote(land, land, send_sems.at[a], recv_sems.at[a], (x, y, 1 - c)).wait_recv()
        for cp in sends:
            cp.wait_send()
        for cp in stores:
            cp.wait()

    return pl.pallas_call(
        body, name="grads_pair_share", in_specs=[HBM_SPEC] * n, out_specs=[HBM_SPEC] * n,
        out_shape=[jax.ShapeDtypeStruct((2 * a.shape[0], a.shape[1]), a.dtype) for a in rs],
        scratch_shapes=[pltpu.VMEM(a.shape, a.dtype) for a in rs] + [pltpu.SemaphoreType.DMA((n,))] * 4,
        compiler_params=pltpu.CompilerParams(vmem_limit_bytes=VMEM_LIMIT),
    )(*rs)


def _small_all_reduce(buf):
    R = buf.shape[0]
    ndev = 2 * N_CHIPS

    def body(in_ref, out_ref, gath, send_sems, recv_sems):
        x, y, c = _coords()
        me = 4 * x + 2 * y + c
        gath[me] = in_ref[...]
        peers = []
        for d in range(1, ndev):
            px = 1 - x if d & 4 else x
            py = 1 - y if d & 2 else y
            pc = 1 - c if d & 1 else c
            peers.append((px, py, pc))
        sends = []
        for d, peer in enumerate(peers):
            cp = _remote(in_ref, gath.at[me], send_sems.at[d], recv_sems.at[d], peer)
            cp.start()
            sends.append(cp)
        for d, peer in enumerate(peers):
            land = gath.at[4 * peer[0] + 2 * peer[1] + peer[2]]
            _remote(land, land, send_sems.at[d], recv_sems.at[d], peer).wait_recv()
        for cp in sends:
            cp.wait_send()
        acc = gath[0]
        for s in range(1, ndev):
            acc = acc + gath[s]
        out_ref[...] = acc

    return pl.pallas_call(
        body, name="small_all_reduce", in_specs=[VMEM_SPEC], out_specs=VMEM_SPEC,
        out_shape=jax.ShapeDtypeStruct(buf.shape, F32),
        scratch_shapes=[pltpu.VMEM((ndev, R, LANE), F32), pltpu.SemaphoreType.DMA((ndev - 1,)),
                        pltpu.SemaphoreType.DMA((ndev - 1,))],
    )(buf)


def _pair_add(name, own, other):
    h = own.shape[1]
    tm = _tile(h, (128, 64, 32, 16))

    def fn(i, a, b):
        return (a + b,)

    return _rowcall(name, fn, h, tm, [(own, "row", None), (other, "row", None)], [(own.shape, BF16, "row")])[0]


def _chip_add(name, own, parts):
    h = parts.shape[1]
    tm = _tile(h, (128, 64, 32, 16))

    def fn(i, o, a):
        a = a.astype(F32)
        return (((o.astype(F32) + a[0]) + a[1]) + a[2],)

    return _rowcall(name, fn, h, tm, [(own, "row", None), (parts, "row", None)], [(parts.shape[1:], F32, "row")])[0]


def _adamw(name, w, g, m, v):
    R = w.shape[0]
    tm = _tile(R, (256, 128, 64, 32, 16, 8))

    def fn(i, wv, gv, mv, vv):
        m2 = ADAM_B1 * mv + (1.0 - ADAM_B1) * gv
        v2 = ADAM_B2 * vv + (1.0 - ADAM_B2) * (gv * gv)
        m_hat = m2 / (1.0 - ADAM_B1 ** ADAM_STEP)
        v_hat = v2 / (1.0 - ADAM_B2 ** ADAM_STEP)
        delta = -ADAM_LR * (m_hat / (jnp.sqrt(v_hat) + ADAM_EPS) + ADAM_WD * wv)
        return delta, m2, v2

    ins = [(a, "row", None) for a in (w, g, m, v)]
    return _rowcall(name, fn, R, tm, ins, [(w.shape, F32, "row")] * 3)


def _pack(arrs):
    flat = jnp.concatenate([a.reshape(-1).astype(F32) for a in arrs])
    size = flat.shape[0]
    padded = -(-size // (SUBLANE * LANE)) * SUBLANE * LANE
    return jnp.pad(flat, (0, padded - size)).reshape(-1, LANE)


def _unpack(buf, shapes):
    flat = buf.reshape(-1)
    out, off = [], 0
    for s in shapes:
        size = math.prod(s)
        out.append(flat[off:off + size].reshape(s))
        off += size
    return out


BIG = ("a_w_in", "a_w_out", "w_kv", "b_w_q", "b_w_o", "ffn_w_up", "ffn_w_down")
WEIGHTS = ("a_norm_w", "a_w_in", "a_conv_w", "a_a_log", "a_dt_bias", "a_out_norm_w", "a_w_out", "kv_norm_w", "w_kv",
           "b_norm_w", "b_w_q", "b_sinks", "b_w_o", "rel_bias_table", "ffn_norm_w", "ffn_w_up", "ffn_conv_w",
           "ffn_conv_b", "ffn_w_down", "final_norm_w")
SMALL = tuple(n for n in WEIGHTS if n not in BIG)
SMALL_SHARDED = {"a_norm_w": 1, "a_conv_w": 2, "ffn_conv_w": 2}


def _quarter_2d(name, a):
    if name in ("ffn_w_up", "ffn_w_down"):
        return a.reshape(a.shape[0] * a.shape[1], a.shape[2])
    return a.reshape(a.shape[-2], a.shape[-1])


def _whole_weights(w):
    bigs = [_quarter_2d(n, w[n]).astype(BF16) for n in BIG]
    smalls = [w["a_norm_w"], w["a_conv_w"][0], w["ffn_conv_w"].reshape(6, DFF2_SHARD)]
    remote = [True] + [False] * (len(bigs) - 1) + [True] * len(smalls)
    g = _all_gather(bigs + smalls, [True] * len(bigs) + [False] * len(smalls), remote)
    w_in = g[0].transpose(1, 0, 2).reshape(D, GDN_IN)
    nh = GDN_V_HEADS
    zpad = jnp.zeros((D, LANE - nh), BF16)
    w_in_ba = jnp.concatenate([w_in[:, GDN_MAIN:GDN_MAIN + nh], zpad, w_in[:, GDN_MAIN + nh:], zpad], axis=1)
    lane_pad = lambda a: jnp.pad(a, ((0, 0), (0, LANE - nh)))
    early = dict(
        a_norm_w=g[7].reshape(1, D), w_in_main=w_in[:, :GDN_MAIN], w_in_ba=w_in_ba,
        a_conv_w=g[8].transpose(1, 0, 2).reshape(4, GDN_CONV), a_log=lane_pad(w["a_a_log"]),
        dt_bias=lane_pad(w["a_dt_bias"]), a_out_norm_w=w["a_out_norm_w"],
        kv_norm_w=w["kv_norm_w"].reshape(1, D), b_norm_w=w["b_norm_w"],
        sinks=jnp.broadcast_to(w["b_sinks"].reshape(SWA_KV_HEADS, SWA_GROUP, 1), (SWA_KV_HEADS, SWA_GROUP, LANE)),
        rel_table_t=w["rel_bias_table"].T, ffn_norm_w=w["ffn_norm_w"],
        ffn_conv_w=g[9].reshape(N_CHIPS, 2, 3, DFF2_SHARD).transpose(1, 2, 0, 3).reshape(2, 3, DFF2),
        ffn_conv_b=w["ffn_conv_b"], final_norm_w=w["final_norm_w"].reshape(1, D),
    )
    return early, (bigs[1:], g[1:len(bigs)])


def _late_weights(g):
    return dict(
        w_out=g[0].reshape(GDN_V, D), w_kv=g[1].reshape(D, 2 * SWA_KV_HEADS * SWA_HD), w_q=g[2].reshape(D, D),
        w_o=g[3].reshape(D, D), w_up=g[4].reshape(N_CHIPS, 2, D, DFF2_SHARD),
        w_down=g[5].reshape(N_CHIPS, 2, DFF_SHARD, D).transpose(1, 0, 2, 3).reshape(2, DFF, D),
    )


def _chip_major(name, g):
    if name == "a_w_in":
        return g.reshape(D, N_CHIPS, GDN_IN_SHARD).transpose(1, 0, 2)
    if name == "ffn_w_up":
        return g.reshape(N_CHIPS, 2 * D, DFF2_SHARD)
    if name == "ffn_w_down":
        return g.reshape(N_CHIPS, 2 * DFF_SHARD, D)
    return g.reshape(N_CHIPS, g.shape[0] // N_CHIPS, g.shape[1])


def kernel(x, a_norm_w, a_w_in, a_conv_w, a_a_log, a_dt_bias, a_out_norm_w, a_w_out, kv_norm_w, w_kv, b_norm_w, b_w_q, b_sinks, b_w_o, rel_bias_table, ffn_norm_w, ffn_w_up, ffn_conv_w, ffn_conv_b, ffn_w_down, final_norm_w, loss_target, m_a_norm_w, m_a_w_in, m_a_conv_w, m_a_a_log, m_a_dt_bias, m_a_out_norm_w, m_a_w_out, m_kv_norm_w, m_w_kv, m_b_norm_w, m_b_w_q, m_b_sinks, m_b_w_o, m_rel_bias_table, m_ffn_norm_w, m_ffn_w_up, m_ffn_conv_w, m_ffn_conv_b, m_ffn_w_down, m_final_norm_w, v_a_norm_w, v_a_w_in, v_a_conv_w, v_a_a_log, v_a_dt_bias, v_a_out_norm_w, v_a_w_out, v_kv_norm_w, v_w_kv, v_b_norm_w, v_b_w_q, v_b_sinks, v_b_w_o, v_rel_bias_table, v_ffn_norm_w, v_ffn_w_up, v_ffn_conv_w, v_ffn_conv_b, v_ffn_w_down, v_final_norm_w):
    w = dict(zip(WEIGHTS, (a_norm_w, a_w_in, a_conv_w, a_a_log, a_dt_bias, a_out_norm_w, a_w_out, kv_norm_w, w_kv,
                           b_norm_w, b_w_q, b_sinks, b_w_o, rel_bias_table, ffn_norm_w, ffn_w_up, ffn_conv_w,
                           ffn_conv_b, ffn_w_down, final_norm_w)))
    m = dict(zip(WEIGHTS, (m_a_norm_w, m_a_w_in, m_a_conv_w, m_a_a_log, m_a_dt_bias, m_a_out_norm_w, m_a_w_out,
                           m_kv_norm_w, m_w_kv, m_b_norm_w, m_b_w_q, m_b_sinks, m_b_w_o, m_rel_bias_table,
                           m_ffn_norm_w, m_ffn_w_up, m_ffn_conv_w, m_ffn_conv_b, m_ffn_w_down, m_final_norm_w)))
    v = dict(zip(WEIGHTS, (v_a_norm_w, v_a_w_in, v_a_conv_w, v_a_a_log, v_a_dt_bias, v_a_out_norm_w, v_a_w_out,
                           v_kv_norm_w, v_w_kv, v_b_norm_w, v_b_w_q, v_b_sinks, v_b_w_o, v_rel_bias_table,
                           v_ffn_norm_w, v_ffn_w_up, v_ffn_conv_w, v_ffn_conv_b, v_ffn_w_down, v_final_norm_w)))
    T = x.shape[1]
    chip = 2 * lax.axis_index("x") + lax.axis_index("y")

    core = lax.axis_index("c")

    def pair_sums(names, whole, other):
        own = [lax.dynamic_slice_in_dim(g, core * (g.shape[1] // 2), g.shape[1] // 2, 1) for g in whole]
        return [_pair_add(f"pair_add_{n}", a, b) for n, a, b in zip(names, own, other)]

    early, late = _whole_weights(w)
    loss_part, dx, grads, scattered = _local_step(x.reshape(T, D), loss_target.reshape(T, D), early, late, pair_sums)

    assert all(n in scattered for n in BIG)
    halves = [_chip_add(f"chip_add_{n}", lax.dynamic_index_in_dim(scattered[n][0], chip, 0, keepdims=False),
                        scattered[n][1]) for n in BIG]
    quarter = _pair_share(halves)
    out_g, out_d, out_m, out_v = {}, {}, {}, {}
    for n, g2 in zip(BIG, quarter):
        res = _adamw(f"adamw_{n}", _quarter_2d(n, w[n]), g2, _quarter_2d(n, m[n]), _quarter_2d(n, v[n]))
        out_g[n] = g2.reshape(w[n].shape)
        out_d[n], out_m[n], out_v[n] = (r.reshape(w[n].shape) for r in res)

    whole = [grads[n] for n in SMALL]
    summed = _unpack(_small_all_reduce(_pack([loss_part[0:1, 0:1]] + whole)), [(1, 1)] + [a.shape for a in whole])
    loss = summed[0].reshape(())
    small_g = []
    for n, g in zip(SMALL, summed[1:]):
        if n in SMALL_SHARDED:
            axis = SMALL_SHARDED[n]
            g = g.reshape(w[n].shape[:axis] + (-1,) + w[n].shape[axis + 1:])
            size = w[n].shape[axis]
            g = lax.dynamic_slice_in_dim(g, chip * size, size, axis)
        small_g.append(g.reshape(w[n].shape))
    shapes = [w[n].shape for n in SMALL]
    res = _adamw("adamw_small", _pack([w[n] for n in SMALL]), _pack(small_g), _pack([m[n] for n in SMALL]),
                 _pack([v[n] for n in SMALL]))
    small_d, small_m, small_v = (_unpack(r, shapes) for r in res)
    for i, n in enumerate(SMALL):
        out_g[n], out_d[n], out_m[n], out_v[n] = small_g[i], small_d[i], small_m[i], small_v[i]

    return (loss, dx.reshape(x.shape), *[out_g[n] for n in WEIGHTS], *[out_d[n] for n in WEIGHTS],
            *[out_m[n] for n in WEIGHTS], *[out_v[n] for n in WEIGHTS])
```

```python
import functools
import math

import jax
import jax.numpy as jnp
from jax import lax
from jax.experimental import pallas as pl
from jax.experimental.pallas import tpu as pltpu

F32 = jnp.float32
BF16 = jnp.bfloat16
MESH = pl.DeviceIdType.MESH
HIGHEST = lax.Precision.HIGHEST

D = 1024
EPS = 1e-6
NEG_INF = -1e30
N_CHIPS = 4

GDN_QK_HEADS = 8
GDN_V_HEADS = 16
GDN_HD = 128
GDN_QK = GDN_QK_HEADS * GDN_HD
GDN_V = GDN_V_HEADS * GDN_HD
GDN_CONV = 2 * GDN_QK + GDN_V
GDN_MAIN = GDN_CONV + GDN_V
GDN_IN = GDN_MAIN + 2 * GDN_V_HEADS
GDN_IN_SHARD = GDN_IN // N_CHIPS
GDN_CHUNK = 64

SWA_Q_HEADS = 16
SWA_KV_HEADS = 4
SWA_GROUP = 4
SWA_HD = 64
SWA_BLOCK = 128
REL_BUCKETS = 32
REL_MAX_DISTANCE = 128

DFF = 2816
DFF2 = 2 * DFF
DFF2_SHARD = DFF2 // N_CHIPS
DFF_SHARD = DFF // N_CHIPS

ADAM_LR = 0.001
ADAM_B1 = 0.9
ADAM_B2 = 0.999
ADAM_EPS = 1e-08
ADAM_WD = 0.01
ADAM_STEP = 10

LANE = 128
SUBLANE = 8
VMEM_LIMIT = 56 * 1024 * 1024


def _params(sem, vmem=VMEM_LIMIT):
    return pltpu.CompilerParams(dimension_semantics=sem, vmem_limit_bytes=vmem)


def _rowcall(name, fn, T, tm, ins, outs, swap=()):
    n = T // tm
    nswap = len(swap)
    swap_shapes = [g.shape for g in swap]
    r8 = tm // SUBLANE
    last8 = T // SUBLANE - 1
    arrays, in_specs = [], []
    for arr, kind, cols in ins:
        arrays.append(arr)
        if kind == "full":
            in_specs.append(pl.BlockSpec(arr.shape, functools.partial(lambda nd, i: (0,) * nd, arr.ndim)))
        elif arr.ndim == 2:
            w, ci = cols if cols is not None else (arr.shape[1], 0)
            if kind == "row":
                in_specs.append(pl.BlockSpec((tm, w), functools.partial(lambda ci, i: (i, ci), ci)))
            elif kind == "prev":
                in_specs.append(pl.BlockSpec(
                    (SUBLANE, w), functools.partial(lambda ci, i: (jnp.maximum(i * r8 - 1, 0), ci), ci)))
            else:
                in_specs.append(pl.BlockSpec(
                    (SUBLANE, w), functools.partial(lambda ci, i: (jnp.minimum((i + 1) * r8, last8), ci), ci)))
        else:
            lead = arr.shape[:-2]
            in_specs.append(pl.BlockSpec(lead + (tm, arr.shape[-1]),
                                         functools.partial(lambda nl, i: (0,) * nl + (i, 0), len(lead))))
    out_shape, out_specs = [], []
    for shape, dtype, kind in outs:
        out_shape.append(jax.ShapeDtypeStruct(shape, dtype))
        if kind == "acc":
            out_specs.append(pl.BlockSpec(shape, functools.partial(lambda nd, i: (0,) * nd, len(shape))))
        else:
            lead = shape[:-2]
            out_specs.append(pl.BlockSpec(lead + (tm, shape[-1]),
                                          functools.partial(lambda nl, i: (0,) * nl + (i, 0), len(lead))))
    nin = len(arrays)

    nout = len(outs)

    def body(*refs):
        i = pl.program_id(0)
        if nswap:
            comm = (swap_shapes, refs[nin:nin + nswap], refs[nin + nswap + nout:nin + 2 * nswap + nout], refs[-2],
                    refs[-1])

            @pl.when(i == 0)
            def _():
                for cp in _swap_copies(*comm):
                    cp.start()

            @pl.when(i == n - 1)
            def _():
                for cp in _swap_copies(*comm):
                    cp.wait()

        vals = [r[...] for r in refs[:nin]]
        res = fn(i, *vals)
        for (shape, dtype, kind), o, r in zip(outs, refs[nin + nswap:], res):
            if kind == "row":
                o[...] = r.astype(dtype)
            else:
                @pl.when(i == 0)
                def _():
                    o[...] = r.astype(dtype)

                @pl.when(i > 0)
                def _():
                    o[...] += r.astype(dtype)

    anywhere = pl.BlockSpec(memory_space=pl.ANY)
    nsem = nswap * N_CHIPS * PAIR_SWAP_PIECES
    return pl.pallas_call(
        body, name=name, grid=(n,), in_specs=in_specs + [anywhere] * nswap, out_specs=out_specs + [anywhere] * nswap,
        out_shape=out_shape + _swap_out_shapes(swap),
        scratch_shapes=[pltpu.SemaphoreType.DMA((nsem,)), pltpu.SemaphoreType.DMA((nsem,))] if nswap else [],
        compiler_params=_params(("arbitrary",)),
    )(*arrays, *swap)


def _mm(name, a, b, out_shape, out_dtype, grid, a_spec, b_spec, o_spec, dims, acc_shape, res=None, precision=None,
        into=None, scatter=(), post=None):
    nk = grid[2]
    ns = len(scatter)
    a, norm_w = a if isinstance(a, tuple) else (a, None)
    normed = norm_w is not None
    posted = post is not None
    n_in = 2 + normed + (res is not None) + (into is not None) + 3 * posted + ns
    n_out = 1 + posted + ns

    def body(*refs):
        a_ref, b_ref, o_ref = refs[0], refs[1], refs[n_in]
        r_ref = refs[2 + normed] if res is not None else None
        first = pl.program_id(0) == 0
        if ns:
            comm = (refs[n_in - ns:n_in], refs[n_in + n_out - ns:n_in + n_out], refs[-2], refs[-1])
            steps = [pl.program_id(d) for d in range(3)]

            @pl.when((steps[0] == 0) & (steps[1] == 0) & (steps[2] == 0))
            def _():
                for cp in _scatter_copies(*comm):
                    cp.start()

            @pl.when((steps[0] == grid[0] - 1) & (steps[1] == grid[1] - 1) & (steps[2] == grid[2] - 1))
            def _():
                copies = _scatter_copies(*comm)
                for cp in copies:
                    cp.wait_recv()
                for cp in copies:
                    cp.wait_send()

        av, bv = a_ref[...], b_ref[...]
        if normed:
            av = _rms_core(av, refs[2][...])
        if precision is None:
            av, bv = av.astype(BF16), bv.astype(BF16)
        p = lax.dot_general(av, bv, (dims, ((), ())), preferred_element_type=F32, precision=precision)

        def finish(x):
            if res is not None:
                x = x + r_ref[...].astype(F32)
            if posted:
                h_ref, w_ref, add_ref = refs[n_in - ns - 3:n_in - ns]
                dh, dw = jax.vjp(_rms_core, h_ref[...], w_ref[...])[1](x)
                x = dh + add_ref[...]
                dw_ref = refs[n_in + 1]

                @pl.when(first)
                def _():
                    dw_ref[...] = dw

                @pl.when(jnp.logical_not(first))
                def _():
                    dw_ref[...] += dw

            o_ref[...] = x.astype(out_dtype).reshape(o_ref.shape)

        if nk == 1:
            finish(p)
        else:
            acc = refs[n_in + n_out]
            k = pl.program_id(2)

            @pl.when(k == 0)
            def _():
                acc[...] = p

            @pl.when(k > 0)
            def _():
                acc[...] += p

            @pl.when(k == nk - 1)
            def _():
                finish(acc[...])

    anywhere = pl.BlockSpec(memory_space=pl.ANY)
    ops = [a, b] + ([norm_w] if normed else []) + ([res] if res is not None else [])
    ops += ([into] if into is not None else []) + (list(post) if posted else []) + list(scatter)
    whole = lambda arr: pl.BlockSpec(arr.shape, lambda i, j, k: (0, 0))
    specs = [a_spec, b_spec] + ([whole(norm_w)] if normed else [])
    specs += [o_spec] if res is not None else []
    specs += [anywhere] if into is not None else []
    specs += ([o_spec, whole(post[1]), o_spec] if posted else []) + [anywhere] * ns
    out = pl.pallas_call(
        body, name=name, grid=grid, in_specs=specs,
        out_specs=[o_spec] + ([whole(post[1])] if posted else []) + [anywhere] * ns,
        out_shape=[jax.ShapeDtypeStruct(out_shape, out_dtype)]
        + ([jax.ShapeDtypeStruct(post[1].shape, F32)] if posted else []) + _scatter_shapes(scatter),
        input_output_aliases={2 + normed + (res is not None): 0} if into is not None else {},
        scratch_shapes=([pltpu.VMEM(acc_shape, F32)] if nk > 1 else [])
        + ([pltpu.SemaphoreType.DMA((3 * ns,)), pltpu.SemaphoreType.DMA((3 * ns,))] if ns else []),
        compiler_params=_params(("arbitrary",) * 3 if ns or posted else ("parallel", "parallel", "arbitrary")),
    )(*ops)
    return out if n_out > 1 else out[0]


NN = ((1,), (0,))
NT = ((1,), (1,))
TN = ((0,), (0,))


BIG_TILES = (1024, 512, 256, 128)


def _tile(n, pref):
    for t in pref:
        if n % t == 0:
            return t
    return n


def _rows_of(a):
    return a[0] if isinstance(a, tuple) else a


def _mm_nn(name, a, w, out_dtype, res=None, precision=None):
    M, K = _rows_of(a).shape
    N = w.shape[1]
    tm = _tile(M, BIG_TILES if K <= 2048 else BIG_TILES[1:])
    tn = _tile(N, BIG_TILES)
    return _mm(name, a, w, (M, N), out_dtype, (M // tm, N // tn, 1),
               pl.BlockSpec((tm, K), lambda i, j, k: (i, 0)), pl.BlockSpec((K, tn), lambda i, j, k: (0, j)),
               pl.BlockSpec((tm, tn), lambda i, j, k: (i, j)), NN, (tm, tn), res=res, precision=precision)


def _mm_nt(name, g, w, out_dtype, res=None, precision=None, scatter=(), post=None):
    M, N = g.shape
    K = w.shape[0]
    tm, tk = _tile(M, BIG_TILES if post is None else BIG_TILES[1:]), _tile(K, (1024, 1408, 512, 256, 128))
    tn = _tile(N, (1536,) + BIG_TILES)
    return _mm(name, g, w, (M, K), out_dtype, (M // tm, K // tk, N // tn),
               pl.BlockSpec((tm, tn), lambda i, j, k: (i, k)), pl.BlockSpec((tk, tn), lambda i, j, k: (j, k)),
               pl.BlockSpec((tm, tk), lambda i, j, k: (i, j)), NT, (tm, tk), res=res, precision=precision,
               scatter=scatter, post=post)


def _mm_tn(name, a, g, out_dtype=F32, precision=None):
    T, K = _rows_of(a).shape
    N = g.shape[1]
    tk, tn = _tile(K, (1024, 1408, 512, 256, 128)), _tile(N, BIG_TILES)
    assert tk == K or not isinstance(a, tuple)
    tt = _tile(T, BIG_TILES)
    return _mm(name, a, g, (K, N), out_dtype, (K // tk, N // tn, T // tt),
               pl.BlockSpec((tt, tk), lambda i, j, k: (k, i)), pl.BlockSpec((tt, tn), lambda i, j, k: (k, j)),
               pl.BlockSpec((tk, tn), lambda i, j, k: (i, j)), TN, (tk, tn), precision=precision)


def _mm_up(name, n, wup, layer):
    T = _rows_of(n).shape[0]
    tm = _tile(T, BIG_TILES)
    return _mm(name, n, wup, (T, DFF2), BF16, (T // tm, N_CHIPS, 1),
               pl.BlockSpec((tm, D), lambda i, j, k: (i, 0)),
               pl.BlockSpec((None, None, D, DFF2_SHARD), lambda i, j, k: (j, layer, 0, 0)),
               pl.BlockSpec((tm, DFF2_SHARD), lambda i, j, k: (i, j)), NN, (tm, DFF2_SHARD))


def _mm_up_nt(name, du, wup, layer, post):
    T = du.shape[0]
    tm, tk = _tile(T, BIG_TILES[1:]), D
    return _mm(name, du, wup, (T, D), F32, (T // tm, D // tk, N_CHIPS),
               pl.BlockSpec((tm, DFF2_SHARD), lambda i, j, k: (i, k)),
               pl.BlockSpec((None, None, tk, DFF2_SHARD), lambda i, j, k: (k, layer, j, 0)),
               pl.BlockSpec((tm, tk), lambda i, j, k: (i, j)), NT, (tm, tk), post=post)


def _mm_up_tn(name, n, du, layer, into):
    T = _rows_of(n).shape[0]
    tk, tt = D, _tile(T, BIG_TILES)
    return _mm(name, n, du, (N_CHIPS, 2, D, DFF2_SHARD), F32, (D // tk, N_CHIPS, T // tt),
               pl.BlockSpec((tt, tk), lambda i, j, k: (k, i)), pl.BlockSpec((tt, DFF2_SHARD), lambda i, j, k: (k, j)),
               pl.BlockSpec((None, None, tk, DFF2_SHARD), lambda i, j, k: (j, layer, i, 0)), TN, (tk, DFF2_SHARD),
               into=into)


def _mm_down_tn(name, act, dout, layer, into):
    T = act.shape[0]
    tk, tn, tt = 2 * DFF_SHARD, _tile(D, BIG_TILES), _tile(T, BIG_TILES)
    return _mm(name, act, dout, (2, 2, 2, DFF_SHARD, D), F32, (DFF // tk, D // tn, T // tt),
               pl.BlockSpec((tt, tk), lambda i, j, k: (k, i)), pl.BlockSpec((tt, tn), lambda i, j, k: (k, j)),
               pl.BlockSpec((None, 2, None, DFF_SHARD, tn), lambda i, j, k: (i, 0, layer, 0, j)), TN, (tk, tn),
               into=into)


def _sigmoid(x):
    return 0.5 * jnp.tanh(0.5 * x) + 0.5


def _silu(x):
    return x * _sigmoid(x)


def _softplus(x):
    return jnp.maximum(x, 0.0) + jnp.log(1.0 + jnp.exp(-jnp.abs(x)))


def _rms_core(h, w):
    return h * lax.rsqrt(jnp.mean(h * h, axis=-1, keepdims=True) + EPS) * w


def _shift_down(x, halo, s, i):
    if s == 0:
        return x
    tm = x.shape[0]
    rolled = pltpu.roll(x, s, 0)
    patch = pltpu.roll(jnp.where(i == 0, 0.0, halo), s, 0)
    row = lax.broadcasted_iota(jnp.int32, patch.shape, 0)
    top = jnp.where(row < s, patch, rolled[:SUBLANE])
    return jnp.concatenate([top, rolled[SUBLANE:]], axis=0) if tm > SUBLANE else top


def _shift_up(x, halo, s, i, n):
    if s == 0:
        return x
    tm = x.shape[0]
    rolled = pltpu.roll(x, tm - s, 0)
    patch = pltpu.roll(jnp.where(i == n - 1, 0.0, halo), SUBLANE - s, 0)
    row = lax.broadcasted_iota(jnp.int32, patch.shape, 0)
    bottom = jnp.where(row >= SUBLANE - s, patch, rolled[tm - SUBLANE:])
    return jnp.concatenate([rolled[:tm - SUBLANE], bottom], axis=0) if tm > SUBLANE else bottom


def _taps(x, halo, K, i):
    return [_shift_down(x, halo, K - 1 - j, i) for j in range(K)]


def _conv_fwd(taps, w):
    y = w[0:1, :] * taps[0]
    for j in range(1, len(taps)):
        y = y + w[j:j + 1, :] * taps[j]
    return y


def _conv_dx(dy, halo_next, w, i, n):
    K = w.shape[0]
    dx = w[K - 1:K, :] * dy
    for j in range(K - 1):
        dx = dx + w[j:j + 1, :] * _shift_up(dy, halo_next, K - 1 - j, i, n)
    return dx


def _conv_dw(dy, taps):
    rows = [jnp.sum(dy * tap, axis=0, keepdims=True) for tap in taps]
    return jnp.concatenate(rows + [jnp.zeros((SUBLANE - len(taps), dy.shape[1]), F32)], axis=0)


def _l2(x):
    return x * lax.rsqrt(jnp.sum(x * x, axis=-1, keepdims=True) + EPS)


def _gdn_post_core(yq, yk, yv, pb, pa, a_log, dtb):
    qn = tuple(_l2(_silu(a)) * (GDN_HD ** -0.5) for a in yq)
    kn = tuple(_l2(_silu(a)) for a in yk)
    v = _silu(yv)
    beta = _sigmoid(pb)
    g = -jnp.exp(a_log) * _softplus(pa + dtb)
    return qn, kn, v, beta, g


def _heads(x, n):
    return tuple(x[:, GDN_HD * h:GDN_HD * (h + 1)] for h in range(n))


def _gdn_pre_fwd(pm, pba, conv_w, a_log, dtb, tm=128):
    T = pm.shape[0]
    tm = min(tm, T)

    def fn(i, x, halo, pbav, cw, al, db):
        y = _conv_fwd(_taps(x.astype(F32), halo.astype(F32), 4, i), cw)
        qn, kn, v, beta, g = _gdn_post_core(_heads(y[:, :GDN_QK], 8), _heads(y[:, GDN_QK:2 * GDN_QK], 8),
                                            y[:, 2 * GDN_QK:], pbav[:, :LANE], pbav[:, LANE:], al, db)
        return jnp.stack(qn), jnp.stack(kn), jnp.stack(_heads(v, GDN_V_HEADS)), beta, g

    ins = [(pm, "row", (GDN_CONV, 0)), (pm, "prev", (GDN_CONV, 0)), (pba, "row", None),
           (conv_w, "full", None), (a_log, "full", None), (dtb, "full", None)]
    outs = [((GDN_QK_HEADS, T, GDN_HD), BF16, "row"), ((GDN_QK_HEADS, T, GDN_HD), BF16, "row"),
            ((GDN_V_HEADS, T, GDN_HD), BF16, "row"), ((T, LANE), F32, "row"), ((T, LANE), F32, "row")]
    return _rowcall("gdn_pre_fwd", fn, T, tm, ins, outs)


def _gdn_pre_bwd(pm, pba, conv_w, a_log, dtb, dqn, dkn, dv, dbeta, dg, tm=128):
    T = pm.shape[0]
    tm = min(tm, T)

    def fn(i, x, halo, pbav, cw, al, db, dqv, dkv, dvv, dbv, dgv):
        taps = _taps(x.astype(F32), halo.astype(F32), 4, i)
        y = _conv_fwd(taps, cw)
        prim = (_heads(y[:, :GDN_QK], 8), _heads(y[:, GDN_QK:2 * GDN_QK], 8), y[:, 2 * GDN_QK:],
                pbav[:, :LANE], pbav[:, LANE:], al, db)
        _, vjp = jax.vjp(_gdn_post_core, *prim)
        cot = (tuple(dqv[h].astype(F32) for h in range(8)), tuple(dkv[h].astype(F32) for h in range(8)),
               jnp.concatenate([dvv[h].astype(F32) for h in range(GDN_V_HEADS)], axis=1), dbv, dgv)
        dyq, dyk, dyv, dpb, dpa, dal, ddb = vjp(cot)
        dy = jnp.concatenate(list(dyq) + list(dyk) + [dyv], axis=1)
        dcw = _conv_dw(dy, taps)
        return dy, jnp.concatenate([dpb, dpa], axis=1), dcw, dal, ddb

    ins = [(pm, "row", (GDN_CONV, 0)), (pm, "prev", (GDN_CONV, 0)), (pba, "row", None),
           (conv_w, "full", None), (a_log, "full", None), (dtb, "full", None),
           (dqn, "row", None), (dkn, "row", None), (dv, "row", None), (dbeta, "row", None), (dg, "row", None)]
    outs = [((T, GDN_CONV), BF16, "row"), ((T, 2 * LANE), F32, "row"), ((SUBLANE, GDN_CONV), F32, "acc"),
            ((1, LANE), F32, "acc"), ((1, LANE), F32, "acc")]
    return _rowcall("gdn_pre_bwd", fn, T, tm, ins, outs)


def _gdn_conv_bwd(dy, dz, conv_w, tm=256):
    T = dy.shape[0]
    tm = min(tm, T)
    n = T // tm

    def fn(i, dyv, halo, dzv, cw):
        dx = _conv_dx(dyv.astype(F32), halo.astype(F32), cw, i, n)
        return (jnp.concatenate([dx.astype(BF16), dzv.astype(BF16)], axis=1),)

    ins = [(dy, "row", None), (dy, "next", None), (dz, "row", None), (conv_w, "full", None)]
    return _rowcall("gdn_conv_bwd", fn, T, tm, ins, [((T, GDN_MAIN), BF16, "row")])[0]


def _bdot(a, b, dims=NN):
    return lax.dot_general(a.astype(BF16), b.astype(BF16), (dims, ((), ())), preferred_element_type=F32)


BNN = ((2,), (1,))
BNT = ((2,), (2,))
BTN = ((1,), (1,))


def _bmm(a, b, dims=BNN):
    return lax.dot_general(a.astype(BF16), b.astype(BF16), (dims, ((0,), (0,))), preferred_element_type=F32)


def _bmm3(a, b):
    ah, bh = a.astype(BF16), b.astype(BF16)
    al, bl = (a - ah.astype(F32)).astype(BF16), (b - bh.astype(F32)).astype(BF16)
    dn = (BNN, ((0,), (0,)))
    return (lax.dot_general(ah, bh, dn, preferred_element_type=F32)
            + lax.dot_general(al, bh, dn, preferred_element_type=F32)
            + lax.dot_general(ah, bl, dn, preferred_element_type=F32))


def _tri_inv(m):
    C = m.shape[-1]
    r = lax.broadcasted_iota(jnp.int32, (C, C), 0)
    c = lax.broadcasted_iota(jnp.int32, (C, C), 1)
    t = jnp.where(r == c, 1.0, 0.0) - m
    pw = _bmm3(m, m)
    t = t + _bmm3(t, pw)
    for _ in range(int(math.log2(C)) - 2):
        pw = _bmm(pw, pw)
        t = t + _bmm(t, pw)
    return t


def _tri_inv_vjp(t, dt):
    tt = jnp.swapaxes(t, 1, 2)
    return -_bmm(_bmm(tt, dt), tt)


def _twice(a):
    return jnp.broadcast_to(a[:, None], (a.shape[0], 2) + a.shape[1:]).reshape((2 * a.shape[0],) + a.shape[1:])


def _gdn_gates(grow, brow):
    C = grow.shape[2]
    r = lax.broadcasted_iota(jnp.int32, (C, C), 0)
    c = lax.broadcasted_iota(jnp.int32, (C, C), 1)
    tril, eye = r >= c, r == c
    gcol = jnp.sum(jnp.where(eye, grow, 0.0), axis=2, keepdims=True)
    bcol = jnp.sum(jnp.where(eye, brow, 0.0), axis=2, keepdims=True)
    gc_col = jnp.sum(jnp.where(tril, grow, 0.0), axis=2, keepdims=True)
    gc_row = jnp.sum(jnp.where(r <= c, gcol, 0.0), axis=1, keepdims=True)
    gc_last = jnp.sum(grow, axis=2, keepdims=True)
    decay = jnp.where(tril, jnp.exp(jnp.where(tril, gc_col - gc_row, 0.0)), 0.0)
    return bcol, gc_col, gc_last, decay


def _gdn_m(k, bcol, decay):
    C = k.shape[1]
    strict = lax.broadcasted_iota(jnp.int32, (C, C), 0) > lax.broadcasted_iota(jnp.int32, (C, C), 1)
    return jnp.where(strict, bcol * _twice(_bmm(k, k, BNT)) * decay, 0.0)


def _gdn_rest(q, k, v, bcol, gc_col, gc_last, decay, t_mat, S):
    qk = _twice(_bmm(q, k, BNT))
    k2, q2 = _twice(k), _twice(q)
    egc = jnp.exp(gc_col)
    u = _bmm(t_mat, v * bcol)
    w = _bmm(t_mat, k2 * (bcol * egc))
    v_new = u - _bmm(w, S)
    o = _bmm(q2 * egc, S) + _bmm(qk * decay, v_new)
    s_new = S * jnp.exp(gc_last) + _bmm(k2 * jnp.exp(gc_last - gc_col), v_new, BTN)
    return o, s_new


def _gdn_tb(T):
    return min(256, T)


def _gate_rows(g):
    T = g.shape[0]
    g = g[:, :GDN_V_HEADS].reshape(T // GDN_CHUNK, GDN_CHUNK, GDN_V_HEADS)
    return g.transpose(0, 2, 1)[:, :, None, :]


def _gate_cols(g):
    nc = g.shape[0]
    g = g[:, :, 0, :].transpose(0, 2, 1).reshape(nc * GDN_CHUNK, GDN_V_HEADS)
    return jnp.pad(g, ((0, 0), (0, LANE - GDN_V_HEADS)))


def _gdn_fwd(qn, kn, v, g, beta, gather=None):
    T = qn.shape[1]
    tb = _gdn_tb(T)
    nc = tb // GDN_CHUNK
    nsteps = T // tb
    quarters, buffers = gather if gather is not None else ((), ())
    ng = len(quarters)
    shapes = [a.shape for a in quarters]
    splits = [True] * ng

    def body(*refs):
        q_ref, k_ref, v_ref, g_ref, b_ref = refs[:5]
        src = refs[5:5 + ng]
        o_ref, sall_ref, tall_ref = refs[5 + 2 * ng:8 + 2 * ng]
        dst = refs[8 + 2 * ng:8 + 3 * ng]
        s_scr = refs[8 + 3 * ng]
        step = pl.program_id(0)

        @pl.when(step == 0)
        def _():
            s_scr[...] = jnp.zeros(s_scr.shape, F32)
            if ng:
                for cp in _gather_copies(shapes, splits, src, dst, *refs[9 + 3 * ng:])[0]:
                    cp.start()

        def chunk(ci, carry):
            rows = pl.ds(pl.multiple_of(ci * GDN_CHUNK, GDN_CHUNK), GDN_CHUNK)
            s = s_scr[...]
            sall_ref[ci] = s
            q, k = q_ref[:, rows, :].astype(F32), k_ref[:, rows, :].astype(F32)
            bcol, gc_col, gc_last, decay = _gdn_gates(g_ref[ci], b_ref[ci])
            t_mat = _tri_inv(_gdn_m(k, bcol, decay)).astype(BF16)
            tall_ref[ci] = t_mat
            o, s_new = _gdn_rest(q, k, v_ref[:, rows, :].astype(F32), bcol, gc_col, gc_last, decay,
                                 t_mat.astype(F32), s)
            o_ref[:, rows, :] = o.astype(o_ref.dtype)
            s_scr[...] = s_new
            return carry

        lax.fori_loop(0, nc, chunk, 0)

        if ng:
            @pl.when(step == nsteps - 1)
            def _():
                _gather_arrival(shapes, splits, src, dst, *refs[9 + 3 * ng:])

    qk_spec = pl.BlockSpec((GDN_QK_HEADS, tb, GDN_HD), lambda i: (0, i, 0))
    v_spec = pl.BlockSpec((GDN_V_HEADS, tb, GDN_HD), lambda i: (0, i, 0))
    g_spec = pl.BlockSpec((nc, GDN_V_HEADS, 1, GDN_CHUNK), lambda i: (i, 0, 0, 0))
    anywhere = pl.BlockSpec(memory_space=pl.ANY)
    return pl.pallas_call(
        body, name="gdn_fwd", grid=(nsteps,),
        in_specs=[qk_spec, qk_spec, v_spec, g_spec, g_spec] + [anywhere] * (2 * ng),
        out_specs=[v_spec, pl.BlockSpec((nc, GDN_V_HEADS, GDN_HD, GDN_HD), lambda i: (i, 0, 0, 0)),
                   pl.BlockSpec((nc, GDN_V_HEADS, GDN_CHUNK, GDN_CHUNK), lambda i: (i, 0, 0, 0))] + [anywhere] * ng,
        out_shape=[jax.ShapeDtypeStruct((GDN_V_HEADS, T, GDN_HD), BF16),
                   jax.ShapeDtypeStruct((T // GDN_CHUNK, GDN_V_HEADS, GDN_HD, GDN_HD), F32),
                   jax.ShapeDtypeStruct((T // GDN_CHUNK, GDN_V_HEADS, GDN_CHUNK, GDN_CHUNK), BF16)]
        + [jax.ShapeDtypeStruct(b.shape, b.dtype) for b in buffers],
        input_output_aliases={5 + ng + a: 3 + a for a in range(ng)},
        scratch_shapes=[pltpu.VMEM((GDN_V_HEADS, GDN_HD, GDN_HD), F32)]
        + ([pltpu.SemaphoreType.DMA((6 * ng,)), pltpu.SemaphoreType.DMA((6 * ng,))] if ng else []),
        compiler_params=_params(("arbitrary",)),
    )(qn, kn, v, g, beta, *quarters, *buffers)


def _gdn_bwd(qn, kn, v, g, beta, sall, tall, do, scatter=()):
    T = qn.shape[1]
    tb = _gdn_tb(T)
    nc = tb // GDN_CHUNK
    nb = T // tb
    ns = len(scatter)

    def body(*refs):
        q_ref, k_ref, v_ref, g_ref, b_ref, sall_ref, tall_ref, do_ref = refs[:8]
        dq_ref, dk_ref, dv_ref, dg_ref, db_ref = refs[8 + ns:13 + ns]
        ds_scr = refs[13 + 2 * ns]
        comm = (refs[8:8 + ns], refs[13 + ns:13 + 2 * ns], *refs[14 + 2 * ns:])
        step = pl.program_id(0)

        @pl.when(step == 0)
        def _():
            ds_scr[...] = jnp.zeros(ds_scr.shape, F32)
            if ns:
                for cp in _scatter_copies(*comm):
                    cp.start()

        def chunk(cr, carry):
            ci = nc - 1 - cr
            rows = pl.ds(pl.multiple_of(ci * GDN_CHUNK, GDN_CHUNK), GDN_CHUNK)
            k, t_mat = k_ref[:, rows, :].astype(F32), tall_ref[ci].astype(F32)
            (bcol, gc_col, gc_last, decay), vjp_gates = jax.vjp(_gdn_gates, g_ref[ci], b_ref[ci])
            _, vjp = jax.vjp(_gdn_rest, q_ref[:, rows, :].astype(F32), k, v_ref[:, rows, :].astype(F32),
                             bcol, gc_col, gc_last, decay, t_mat, sall_ref[ci])
            dq, dk, dv, dbcol, dgc_col, dgc_last, ddecay, dt, ds = vjp((do_ref[:, rows, :].astype(F32), ds_scr[...]))
            _, vjp_m = jax.vjp(_gdn_m, k, bcol, decay)
            dk_m, dbcol_m, ddecay_m = vjp_m(_tri_inv_vjp(t_mat, dt))
            dg, db = vjp_gates((dbcol + dbcol_m, dgc_col, dgc_last, ddecay + ddecay_m))
            ds_scr[...] = ds
            dq_ref[:, rows, :] = dq
            dk_ref[:, rows, :] = dk + dk_m
            dv_ref[:, rows, :] = dv
            dg_ref[ci] = dg
            db_ref[ci] = db
            return carry

        lax.fori_loop(0, nc, chunk, 0)

        if ns:
            @pl.when(step == nb - 1)
            def _():
                copies = _scatter_copies(*comm)
                for cp in copies:
                    cp.wait_recv()
                for cp in copies:
                    cp.wait_send()

    qk_spec = pl.BlockSpec((GDN_QK_HEADS, tb, GDN_HD), lambda i: (0, nb - 1 - i, 0))
    v_spec = pl.BlockSpec((GDN_V_HEADS, tb, GDN_HD), lambda i: (0, nb - 1 - i, 0))
    g_spec = pl.BlockSpec((nc, GDN_V_HEADS, 1, GDN_CHUNK), lambda i: (nb - 1 - i, 0, 0, 0))
    s_spec = pl.BlockSpec((nc, GDN_V_HEADS, GDN_HD, GDN_HD), lambda i: (nb - 1 - i, 0, 0, 0))
    t_spec = pl.BlockSpec((nc, GDN_V_HEADS, GDN_CHUNK, GDN_CHUNK), lambda i: (nb - 1 - i, 0, 0, 0))
    anywhere = pl.BlockSpec(memory_space=pl.ANY)
    return pl.pallas_call(
        body, name="gdn_bwd", grid=(nb,),
        in_specs=[qk_spec, qk_spec, v_spec, g_spec, g_spec, s_spec, t_spec, v_spec] + [anywhere] * ns,
        out_specs=[qk_spec, qk_spec, v_spec, g_spec, g_spec] + [anywhere] * ns,
        out_shape=[jax.ShapeDtypeStruct((GDN_QK_HEADS, T, GDN_HD), F32),
                   jax.ShapeDtypeStruct((GDN_QK_HEADS, T, GDN_HD), F32),
                   jax.ShapeDtypeStruct((GDN_V_HEADS, T, GDN_HD), F32),
                   jax.ShapeDtypeStruct(g.shape, F32), jax.ShapeDtypeStruct(g.shape, F32)]
        + _scatter_shapes(scatter),
        scratch_shapes=[pltpu.VMEM((GDN_V_HEADS, GDN_HD, GDN_HD), F32)]
        + ([pltpu.SemaphoreType.DMA((3 * ns,)), pltpu.SemaphoreType.DMA((3 * ns,))] if ns else []),
        compiler_params=_params(("arbitrary",)),
    )(qn, kn, v, g, beta, sall, tall, do, *scatter)


def _gnorm_core(o, z, w):
    return tuple(_rms_core(oh, w) * _silu(zh) for oh, zh in zip(o, z))


def _gnorm_fwd(o, pm, w, tm=256):
    T = pm.shape[0]
    tm = min(tm, T)

    def fn(i, ov, zv, wv):
        zf = zv.astype(F32)
        out = _gnorm_core(tuple(ov[h].astype(F32) for h in range(GDN_V_HEADS)), _heads(zf, GDN_V_HEADS), wv)
        return (jnp.concatenate(out, axis=1),)

    ins = [(o, "row", None), (pm, "row", (GDN_V, 2)), (w, "full", None)]
    return _rowcall("gnorm_fwd", fn, T, tm, ins, [((T, GDN_V), BF16, "row")])[0]


def _gnorm_bwd(o, pm, w, don, tm=128, swap=()):
    T = pm.shape[0]
    tm = min(tm, T)

    def fn(i, ov, zv, wv, dv):
        zf, df = zv.astype(F32), dv.astype(F32)
        _, vjp = jax.vjp(_gnorm_core, tuple(ov[h].astype(F32) for h in range(GDN_V_HEADS)),
                         _heads(zf, GDN_V_HEADS), wv)
        do, dz, dw = vjp(_heads(df, GDN_V_HEADS))
        return jnp.stack(do), jnp.concatenate(dz, axis=1), dw

    ins = [(o, "row", None), (pm, "row", (GDN_V, 2)), (w, "full", None), (don, "row", None)]
    outs = [((GDN_V_HEADS, T, GDN_HD), BF16, "row"), ((T, GDN_V), BF16, "row"), ((1, GDN_HD), F32, "acc")]
    return _rowcall("gnorm_bwd", fn, T, tm, ins, outs, swap=swap)


def _ffn_act_fwd(name, up, conv_w, conv_b, tm=256):
    T = up.shape[0]
    tm = min(tm, T)

    def fn(i, x, halo, cw, cb):
        u = _conv_fwd(_taps(x.astype(F32), halo.astype(F32), 3, i), cw) + cb
        return (_silu(u[:, :DFF]) * u[:, DFF:],)

    ins = [(up, "row", None), (up, "prev", None), (conv_w, "full", None), (conv_b, "full", None)]
    return _rowcall(name, fn, T, tm, ins, [((T, DFF), BF16, "row")])[0]


def _ffn_act_bwd(name, up, conv_w, conv_b, dact, tm=128):
    T = up.shape[0]
    tm = min(tm, T)

    def fn(i, x, halo, cw, cb, da):
        taps = _taps(x.astype(F32), halo.astype(F32), 3, i)
        da = da.astype(F32)
        u = _conv_fwd(taps, cw) + cb
        gate, val = u[:, :DFF], u[:, DFF:]
        sg = _sigmoid(gate)
        dgate = da * val * sg * (1.0 + gate * (1.0 - sg))
        dval = da * gate * sg
        du = jnp.concatenate([dgate, dval], axis=1)
        return du, _conv_dw(du, taps), jnp.sum(du, axis=0, keepdims=True)

    ins = [(up, "row", None), (up, "prev", None), (conv_w, "full", None), (conv_b, "full", None),
           (dact, "row", None)]
    outs = [((T, DFF2), BF16, "row"), ((SUBLANE, DFF2), F32, "acc"), ((1, DFF2), F32, "acc")]
    return _rowcall(name, fn, T, tm, ins, outs)


def _ffn_conv_bwd(name, du, conv_w, tm=256):
    T = du.shape[0]
    tm = min(tm, T)
    n = T // tm

    def fn(i, dv, halo, cw):
        return (_conv_dx(dv.astype(F32), halo.astype(F32), cw, i, n),)

    ins = [(du, "row", None), (du, "next", None), (conv_w, "full", None)]
    return _rowcall(name, fn, T, tm, ins, [((T, DFF2), BF16, "row")])[0]


GROUP_ROWS = SWA_GROUP * SWA_BLOCK


def _attn_core(q, kp, kc, vp, vc, bias, sink, mask):
    kcat = jnp.concatenate([kp, kc], axis=0)
    vcat = jnp.concatenate([vp, vc], axis=0)
    s = _bdot(q * (SWA_HD ** -0.5), kcat, NT) + bias
    s = jnp.where(mask, s, NEG_INF)
    m = lax.stop_gradient(jnp.maximum(jnp.max(s, axis=-1, keepdims=True), sink))
    p = jnp.exp(s - m)
    denom = jnp.sum(p, axis=-1, keepdims=True) + jnp.exp(sink - m)
    return _bdot(p / denom, vcat)


def _attn_mask(i):
    qi = lax.broadcasted_iota(jnp.int32, (GROUP_ROWS, 2 * SWA_BLOCK), 0) & (SWA_BLOCK - 1)
    ki = lax.broadcasted_iota(jnp.int32, (GROUP_ROWS, 2 * SWA_BLOCK), 1)
    dist = qi + SWA_BLOCK - ki
    return (dist >= 0) & (dist < SWA_BLOCK) & ((ki >= SWA_BLOCK) | (i > 0))


def _head_cols(h):
    return slice(h * SWA_HD, (h + 1) * SWA_HD)


FIRST, SECOND = slice(0, SWA_BLOCK), slice(SWA_BLOCK, 2 * SWA_BLOCK)


def _stacked_heads(ref, rows, j):
    return jnp.concatenate([ref[rows, _head_cols(SWA_GROUP * j + g)].astype(F32) for g in range(SWA_GROUP)], axis=0)


def _store_heads(ref, rows, j, stacked):
    for g in range(SWA_GROUP):
        ref[rows, _head_cols(SWA_GROUP * j + g)] = stacked[g * SWA_BLOCK:(g + 1) * SWA_BLOCK].astype(ref.dtype)


def _attn_chains(i, q_ref, kvc_ref, kvp_ref, b_ref, s_ref):
    chains = []
    for rows in (FIRST, SECOND):
        mask = _attn_mask(i) if rows is FIRST else _attn_mask(1)
        before = kvp_ref if rows is FIRST else kvc_ref
        for j in range(SWA_KV_HEADS):
            heads = slice(SWA_GROUP * j, SWA_GROUP * (j + 1))
            k_cols, v_cols = _head_cols(j), _head_cols(SWA_KV_HEADS + j)
            sink = jnp.concatenate(
                [jnp.broadcast_to(s_ref[j, g:g + 1, 0:1], (SWA_BLOCK, 1)) for g in range(SWA_GROUP)], axis=0)
            ops = (_stacked_heads(q_ref, rows, j), before[FIRST, k_cols].astype(F32), kvc_ref[rows, k_cols].astype(F32),
                   before[FIRST, v_cols].astype(F32), kvc_ref[rows, v_cols].astype(F32),
                   b_ref[heads].reshape(GROUP_ROWS, 2 * SWA_BLOCK), sink)
            chains.append((rows, j, ops, mask))
    return chains


def _attn_fwd_flat(q, kv, bias, sinks):
    T = q.shape[0]
    nb = T // (2 * SWA_BLOCK)

    def body(q_ref, kvc_ref, kvp_ref, b_ref, s_ref, o_ref):
        chains = _attn_chains(pl.program_id(0), q_ref, kvc_ref, kvp_ref, b_ref, s_ref)
        outs = [_attn_core(*ops, mask) for _, _, ops, mask in chains]
        for (rows, j, _, _), out in zip(chains, outs):
            _store_heads(o_ref, rows, j, out)

    q_spec = pl.BlockSpec((2 * SWA_BLOCK, q.shape[1]), lambda i: (i, 0))
    cur = pl.BlockSpec((2 * SWA_BLOCK, kv.shape[1]), lambda i: (i, 0))
    prev = pl.BlockSpec((SWA_BLOCK, kv.shape[1]), lambda i: (jnp.maximum(2 * i - 1, 0), 0))
    return pl.pallas_call(
        body, name="attn_fwd", grid=(nb,),
        in_specs=[q_spec, cur, prev, pl.BlockSpec(bias.shape, lambda i: (0, 0, 0)),
                  pl.BlockSpec(sinks.shape, lambda i: (0, 0, 0))],
        out_specs=q_spec, out_shape=jax.ShapeDtypeStruct(q.shape, BF16),
        compiler_params=_params(("arbitrary",)),
    )(q, kv, kv, bias, sinks)


def _attn_bwd_flat(q, kv, bias, sinks, do):
    T = q.shape[0]
    nb = T // (2 * SWA_BLOCK)
    width = kv.shape[1]

    def body(q_ref, kvc_ref, kvp_ref, b_ref, s_ref, do_ref, dq_ref, even_ref, odd_ref, db_ref, dsk_ref, carry):
        i = pl.program_id(0)

        @pl.when(i < nb)
        def _():
            chains = _attn_chains(i, q_ref, kvc_ref, kvp_ref, b_ref, s_ref)
            cots = [_stacked_heads(do_ref, rows, j) for rows, j, _, _ in chains]
            grads = [jax.vjp(functools.partial(_attn_core, mask=mask), *ops)[1](cot)
                     for (_, _, ops, mask), cot in zip(chains, cots)]
            for j in range(SWA_KV_HEADS):
                heads = slice(SWA_GROUP * j, SWA_GROUP * (j + 1))
                k_cols, v_cols = _head_cols(j), _head_cols(SWA_KV_HEADS + j)
                (dq0, dkp0, dkc0, dvp0, dvc0, db0, dsc0), (dq1, dkp1, dkc1, dvp1, dvc1, db1, dsc1) = (
                    grads[j], grads[SWA_KV_HEADS + j])
                _store_heads(dq_ref, FIRST, j, dq0)
                _store_heads(dq_ref, SECOND, j, dq1)
                db = (db0 + db1).reshape(SWA_GROUP, SWA_BLOCK, 2 * SWA_BLOCK)
                dsc = dsc0 + dsc1
                dsk = jnp.concatenate(
                    [jnp.broadcast_to(jnp.sum(dsc[g * SWA_BLOCK:(g + 1) * SWA_BLOCK], axis=0, keepdims=True),
                                      (1, LANE)) for g in range(SWA_GROUP)], axis=0)
                even_ref[:, k_cols] = (dkc0 + dkp1).astype(even_ref.dtype)
                even_ref[:, v_cols] = (dvc0 + dvp1).astype(even_ref.dtype)

                @pl.when(i == 0)
                def _():
                    db_ref[heads] = db
                    dsk_ref[j] = dsk

                @pl.when(i > 0)
                def _():
                    db_ref[heads] += db
                    dsk_ref[j] += dsk
                    odd_ref[:, k_cols] = (carry[:, k_cols] + dkp0).astype(odd_ref.dtype)
                    odd_ref[:, v_cols] = (carry[:, v_cols] + dvp0).astype(odd_ref.dtype)

                carry[:, k_cols] = dkc1
                carry[:, v_cols] = dvc1

        @pl.when(i == nb)
        def _():
            odd_ref[...] = carry[...].astype(odd_ref.dtype)

    last = nb - 1
    q_spec = pl.BlockSpec((2 * SWA_BLOCK, q.shape[1]), lambda i: (jnp.minimum(i, last), 0))
    cur = pl.BlockSpec((2 * SWA_BLOCK, width), lambda i: (jnp.minimum(i, last), 0))
    prev = pl.BlockSpec((SWA_BLOCK, width), lambda i: (jnp.clip(2 * i - 1, 0, 2 * nb - 1), 0))
    even = pl.BlockSpec((None, SWA_BLOCK, width), lambda i: (jnp.minimum(i, last), 0, 0))
    odd = pl.BlockSpec((None, SWA_BLOCK, width), lambda i: (jnp.clip(i - 1, 0, last), 0, 0))
    b_spec = pl.BlockSpec(bias.shape, lambda i: (0, 0, 0))
    s_spec = pl.BlockSpec(sinks.shape, lambda i: (0, 0, 0))
    halves = jax.ShapeDtypeStruct((nb, SWA_BLOCK, width), BF16)
    dq, dkv_even, dkv_odd, dbias, dsinks = pl.pallas_call(
        body, name="attn_bwd", grid=(nb + 1,),
        in_specs=[q_spec, cur, prev, b_spec, s_spec, q_spec],
        out_specs=[q_spec, even, odd, b_spec, s_spec],
        out_shape=[jax.ShapeDtypeStruct(q.shape, BF16), halves, halves,
                   jax.ShapeDtypeStruct(bias.shape, F32), jax.ShapeDtypeStruct(sinks.shape, F32)],
        scratch_shapes=[pltpu.VMEM((SWA_BLOCK, width), F32)],
        compiler_params=_params(("arbitrary",)),
    )(q, kv, kv, bias, sinks, do)
    return dq, jnp.stack([dkv_even, dkv_odd], axis=1).reshape(T, width), dbias, dsinks


def _rel_onehot():
    qi = jnp.arange(SWA_BLOCK)[:, None]
    ki = jnp.arange(2 * SWA_BLOCK)[None, :]
    n = jnp.maximum(qi + SWA_BLOCK - ki, 0)
    max_exact = REL_BUCKETS // 2
    nf = jnp.maximum(n, 1).astype(F32)
    large = max_exact + (jnp.log(nf / max_exact) / math.log(REL_MAX_DISTANCE / max_exact)
                         * (REL_BUCKETS - max_exact)).astype(jnp.int32)
    bucket = jnp.where(n < max_exact, n, jnp.minimum(large, REL_BUCKETS - 1)).reshape(-1)
    return (bucket[None, :] == jnp.arange(REL_BUCKETS)[:, None]).astype(F32)


def _final(h, w, target, tm=256):
    T = h.shape[0]
    tm = min(tm, T)

    def fn(i, hv, wv, tv):
        y, vjp = jax.vjp(_rms_core, hv, wv)
        err = y - tv
        dh, dw = vjp(err * (1.0 / D))
        part = 0.5 * jnp.sum(jnp.sum(err * err, axis=1, keepdims=True) * (1.0 / D), axis=0, keepdims=True)
        return jnp.broadcast_to(part, (SUBLANE, LANE)), dh, dw

    ins = [(h, "row", None), (w, "full", None), (target, "row", None)]
    outs = [((SUBLANE, LANE), F32, "acc"), ((T, D), F32, "row"), ((1, D), F32, "acc")]
    return _rowcall("final", fn, T, tm, ins, outs)


def _ffn_fwd(tag, h, P, layer):
    n = (h, P["ffn_norm_w"][layer:layer + 1])
    up = _mm_up(f"{tag}_up", n, P["w_up"], layer)
    act = _ffn_act_fwd(f"{tag}_act", up, P["ffn_conv_w"][layer], P["ffn_conv_b"][layer:layer + 1])
    out = _mm_nn(f"{tag}_down", act, P["w_down"][layer], F32, res=h)
    return out, (n, up, act)


def _ffn_bwd(tag, h, saved, dout, P, layer, into=(None, None)):
    n, up, act = saved
    cw, cb = P["ffn_conv_w"][layer], P["ffn_conv_b"][layer:layer + 1]
    dact = _mm_nt(f"{tag}_down_dx", dout, P["w_down"][layer], BF16)
    g_down = _mm_down_tn(f"{tag}_down_dw", act, dout, layer, into[1])
    du, dcw, dcb = _ffn_act_bwd(f"{tag}_act_bwd", up, cw, cb, dact)
    dup = _ffn_conv_bwd(f"{tag}_conv_bwd", du, cw)
    g_up = _mm_up_tn(f"{tag}_up_dw", n, dup, layer, into[0])
    dh, dnw = _mm_up_nt(f"{tag}_up_dx", dup, P["w_up"], layer, post=(h, P["ffn_norm_w"][layer:layer + 1], dout))
    return dh, dict(w_down=g_down, w_up=g_up, conv_w=dcw[:3], conv_b=dcb, norm_w=dnw)


def _local_step(x, target, P, late=None, pair_sums=None):
    T = x.shape[0]
    n0 = (x, P["a_norm_w"])
    pm = _mm_nn("gdn_in", n0, P["w_in_main"], BF16)
    pba = _mm_nn("gdn_in_ba", n0, P["w_in_ba"], F32)
    qn, kn, v, beta, g = _gdn_pre_fwd(pm, pba, P["a_conv_w"], P["a_log"], P["dt_bias"])
    g_rows, beta_rows = _gate_rows(g), _gate_rows(beta)
    o, sall, tall, *gathered = _gdn_fwd(qn, kn, v, g_rows, beta_rows, gather=late)
    if late is not None:
        P = {**P, **_late_weights(gathered)}
    on = _gnorm_fwd(o, pm, P["a_out_norm_w"])
    h1 = _mm_nn("gdn_out", on, P["w_out"], F32, res=x)
    h2, ffn0 = _ffn_fwd("ffn0", h1, P, 0)
    nkv = (h2, P["kv_norm_w"])
    kv = _mm_nn("kv_proj", nkv, P["w_kv"], BF16)
    nb = (h2, P["b_norm_w"])
    qp = _mm_nn("q_proj", nb, P["w_q"], BF16)
    onehot = _rel_onehot()
    bias = _mm_nn("rel_bias", P["rel_table_t"], onehot, F32, precision=HIGHEST)
    bias = bias.reshape(SWA_Q_HEADS, SWA_BLOCK, 2 * SWA_BLOCK)
    oa = _attn_fwd_flat(qp, kv, bias, P["sinks"])
    h3 = _mm_nn("o_proj", oa, P["w_o"], F32, res=h2)
    h4, ffn1 = _ffn_fwd("ffn1", h3, P, 1)
    loss, dh4, d_final = _final(h4, P["final_norm_w"], target)

    dh3, gf1 = _ffn_bwd("ffn1", h3, ffn1, dh4, P, 1)
    doa = _mm_nt("o_proj_dx", dh3, P["w_o"], BF16)
    g_wo = _mm_tn("o_proj_dw", oa, dh3)
    dqp, dkv, dbias, dsinks = _attn_bwd_flat(qp, kv, bias, P["sinks"], doa)
    g_wq = _mm_tn("q_proj_dw", nb, dqp)
    g_wkv = _mm_tn("kv_proj_dw", nkv, dkv)
    dh2, d_bnorm = _mm_nt("q_proj_dx", dqp, P["w_q"], F32, post=(h2, P["b_norm_w"], dh3))
    dh2, d_kvnorm = _mm_nt("kv_proj_dx", dkv, P["w_kv"], F32, post=(h2, P["kv_norm_w"], dh2))
    g_table = _mm_nt("rel_bias_dw", onehot, dbias.reshape(SWA_Q_HEADS, -1), F32, precision=HIGHEST)
    dh1, gf0 = _ffn_bwd("ffn0", h1, ffn0, dh2, P, 0, into=(gf1["w_up"], gf1["w_down"]))
    don = _mm_nt("gdn_out_dx", dh1, P["w_out"], BF16)
    g_wout = _mm_tn("gdn_out_dw", on, dh1)
    ready = dict(a_w_out=g_wout, w_kv=g_wkv, b_w_q=g_wq, b_w_o=g_wo, ffn_w_up=gf0["w_up"], ffn_w_down=gf0["w_down"])
    names = [n for n in BIG if n in ready]
    whole = [_chip_major(n, ready[n]) for n in names] if pair_sums is not None else []
    do, dz, d_gnorm, *other = _gnorm_bwd(o, pm, P["a_out_norm_w"], don, swap=whole)
    pairs = pair_sums(names, whole, other) if pair_sums is not None else []
    dq, dk, dv, dg, dbeta, *parts = _gdn_bwd(qn, kn, v, g_rows, beta_rows, sall, tall, do, scatter=pairs)
    dy, dpba, d_aconv, d_alog, d_dtb = _gdn_pre_bwd(pm, pba, P["a_conv_w"], P["a_log"], P["dt_bias"],
                                                    dq, dk, dv, _gate_cols(dbeta), _gate_cols(dg))
    dpm = _gdn_conv_bwd(dy, dz, P["a_conv_w"])
    g_win_main = _mm_tn("gdn_in_dw", n0, dpm)
    g_win_ba = _mm_tn("gdn_in_ba_dw", n0, dpba)
    nh = GDN_V_HEADS
    g_win = jnp.concatenate([g_win_main, g_win_ba[:, :nh], g_win_ba[:, LANE:LANE + nh]], axis=1)
    last_whole = [_chip_major("a_w_in", g_win)]
    last_pair = pair_sums(["a_w_in"], last_whole, _pair_swap(last_whole, "late")) if pair_sums is not None else []
    dn0 = _mm_nt("gdn_in_dx", dpm, P["w_in_main"], F32, scatter=last_pair)
    dn0, last_parts = (dn0[0], dn0[1:]) if last_pair else (dn0, [])
    dx, d_anorm = _mm_nt("gdn_in_ba_dx", dpba, P["w_in_ba"], F32, res=dn0, post=(x, P["a_norm_w"], dh1))

    nh = GDN_V_HEADS
    grads = dict(
        a_norm_w=d_anorm,
        a_w_in=g_win,
        a_conv_w=d_aconv[:4], a_a_log=d_alog[:, :nh], a_dt_bias=d_dtb[:, :nh], a_out_norm_w=d_gnorm,
        a_w_out=g_wout, kv_norm_w=d_kvnorm, w_kv=g_wkv, b_norm_w=d_bnorm, b_w_q=g_wq,
        b_sinks=dsinks[:, :, 0].reshape(1, SWA_Q_HEADS), b_w_o=g_wo, rel_bias_table=g_table,
        ffn_norm_w=jnp.concatenate([gf0["norm_w"], gf1["norm_w"]], axis=0),
        ffn_w_up=gf0["w_up"],
        ffn_conv_w=jnp.stack([gf0["conv_w"], gf1["conv_w"]], axis=0),
        ffn_conv_b=jnp.concatenate([gf0["conv_b"], gf1["conv_b"]], axis=0),
        ffn_w_down=gf0["w_down"],
        final_norm_w=d_final,
    )
    scattered = dict(zip([n for n in BIG if n in ready], zip(pairs, parts)))
    scattered.update(zip(["a_w_in"], zip(last_pair, last_parts)))
    return loss, dx, grads, scattered


HBM_SPEC = pl.BlockSpec(memory_space=pltpu.HBM)
VMEM_SPEC = pl.BlockSpec(memory_space=pltpu.VMEM)


def _coords():
    return lax.axis_index("x"), lax.axis_index("y"), lax.axis_index("c")


def _remote(src, dst, send_sem, recv_sem, device):
    return pltpu.make_async_remote_copy(src_ref=src, dst_ref=dst, send_sem=send_sem, recv_sem=recv_sem,
                                        device_id=device, device_id_type=MESH)


def _other_chips(x, y):
    return [(1 - x, y), (x, 1 - y), (1 - x, 1 - y)]


def _gather_copies(shapes, split, ins, outs, send_sems, recv_sems):
    x, y, c = _coords()
    p = 2 * x + y
    ici, forwards, from_sibling = [], [], []
    for a, shape in enumerate(shapes):
        h = shape[0] // 2
        for j, chip in enumerate(_other_chips(x, y)):
            q = 2 * chip[0] + chip[1]
            if split[a]:
                mine, theirs = pl.ds(c * h, h), pl.ds((1 - c) * h, h)
                ici.append(_remote(ins[a].at[mine], outs[a].at[p, mine], send_sems.at[6 * a + j],
                                   recv_sems.at[6 * a + j], (*chip, c)))
                land = outs[a].at[q, mine]
                forwards.append(_remote(land, land, send_sems.at[6 * a + 3 + j], recv_sems.at[6 * a + 3 + j],
                                        (x, y, 1 - c)))
                land = outs[a].at[q, theirs]
                from_sibling.append(_remote(land, land, send_sems.at[6 * a + 3 + j], recv_sems.at[6 * a + 3 + j],
                                            (x, y, 1 - c)))
            else:
                ici.append(_remote(ins[a], outs[a].at[p], send_sems.at[6 * a + j], recv_sems.at[6 * a + j],
                                   (*chip, c)))
                forwards.append(None)
    return ici, forwards, from_sibling


def _gather_arrival(shapes, split, ins, outs, send_sems, recv_sems):
    x, y, c = _coords()
    ici, forwards, from_sibling = _gather_copies(shapes, split, ins, outs, send_sems, recv_sems)
    k = 0
    for a, shape in enumerate(shapes):
        h = shape[0] // 2
        for j, chip in enumerate(_other_chips(x, y)):
            q = 2 * chip[0] + chip[1]
            land = outs[a].at[q, pl.ds(c * h, h)] if split[a] else outs[a].at[q]
            _remote(land, land, send_sems.at[6 * a + j], recv_sems.at[6 * a + j], (*chip, c)).wait_recv()
            if forwards[k] is not None:
                forwards[k].start()
            k += 1
    for cp in from_sibling:
        cp.wait_recv()
    for cp in ici + [f for f in forwards if f is not None]:
        cp.wait_send()


def _all_gather(arrs, split, remote):
    n = len(arrs)
    now = [a for a in range(n) if remote[a]]
    shapes = [arrs[a].shape for a in now]
    splits = [split[a] for a in now]

    def body(*refs):
        ins, outs, stage = refs[:n], refs[n:2 * n], refs[2 * n:3 * n]
        send_sems, recv_sems, in_sems, out_sems = refs[3 * n:]
        p = 2 * lax.axis_index("x") + lax.axis_index("y")
        gathered = ([ins[a] for a in now], [outs[a] for a in now], send_sems, recv_sems)
        loads = [pltpu.make_async_copy(ins[a], stage[a], in_sems.at[a]) for a in range(n)]
        for cp in loads:
            cp.start()
        for cp in _gather_copies(shapes, splits, *gathered)[0]:
            cp.start()
        stores = [pltpu.make_async_copy(stage[a], outs[a].at[p], out_sems.at[a]) for a in range(n)]
        for a in range(n):
            loads[a].wait()
            stores[a].start()
        _gather_arrival(shapes, splits, *gathered)
        for cp in stores:
            cp.wait()

    return pl.pallas_call(
        body, name="weights_all_gather", in_specs=[HBM_SPEC] * n, out_specs=[HBM_SPEC] * n,
        out_shape=[jax.ShapeDtypeStruct((N_CHIPS,) + a.shape, a.dtype) for a in arrs],
        scratch_shapes=[pltpu.VMEM(a.shape, a.dtype) for a in arrs]
        + [pltpu.SemaphoreType.DMA((6 * len(now),)), pltpu.SemaphoreType.DMA((6 * len(now),)),
           pltpu.SemaphoreType.DMA((n,)), pltpu.SemaphoreType.DMA((n,))],
        compiler_params=pltpu.CompilerParams(vmem_limit_bytes=VMEM_LIMIT),
    )(*arrs)


PAIR_SWAP_PIECES = 2


def _swap_copies(shapes, ins, other, send_sems, recv_sems):
    x, y, c = _coords()
    copies = []
    for a, shape in enumerate(shapes):
        h = shape[1] // 2
        piece = h // PAIR_SWAP_PIECES
        for q in range(N_CHIPS):
            for r in range(PAIR_SWAP_PIECES):
                k = (a * N_CHIPS + q) * PAIR_SWAP_PIECES + r
                copies.append(_remote(ins[a].at[q, pl.ds((1 - c) * h + r * piece, piece)],
                                      other[a].at[q, pl.ds(r * piece, piece)], send_sems.at[k], recv_sems.at[k],
                                      (x, y, 1 - c)))
    return copies


def _swap_out_shapes(gs):
    return [jax.ShapeDtypeStruct((N_CHIPS, g.shape[1] // 2, g.shape[2]), g.dtype) for g in gs]


def _pair_swap(gs, tag):
    n = len(gs)
    shapes = [g.shape for g in gs]

    def body(*refs):
        copies = _swap_copies(shapes, refs[:n], refs[n:2 * n], *refs[2 * n:])
        for cp in copies:
            cp.start()
        for cp in copies:
            cp.wait()

    nsem = n * N_CHIPS * PAIR_SWAP_PIECES
    return pl.pallas_call(
        body, name=f"grads_pair_swap_{tag}", in_specs=[HBM_SPEC] * n, out_specs=[HBM_SPEC] * n,
        out_shape=_swap_out_shapes(gs),
        scratch_shapes=[pltpu.SemaphoreType.DMA((nsem,)), pltpu.SemaphoreType.DMA((nsem,))],
    )(*gs)


def _scatter_copies(ins, outs, send_sems, recv_sems):
    x, y, c = _coords()
    copies = []
    for a in range(len(ins)):
        for j, chip in enumerate(_other_chips(x, y)):
            q = 2 * chip[0] + chip[1]
            copies.append(_remote(ins[a].at[q], outs[a].at[j], send_sems.at[3 * a + j], recv_sems.at[3 * a + j],
                                  (*chip, c)))
    return copies


def _scatter_shapes(ps):
    return [jax.ShapeDtypeStruct((N_CHIPS - 1,) + a.shape[1:], a.dtype) for a in ps]


def _pair_share(rs):
    n = len(rs)

    def body(*refs):
        ins, outs, stage = refs[:n], refs[n:2 * n], refs[2 * n:3 * n]
        send_sems, recv_sems, in_sems, out_sems = refs[3 * n:]
        x, y, c = _coords()

        def mine(a):
            h = rs[a].shape[0]
            return outs[a].at[pl.ds(c * h, h)]

        loads = [pltpu.make_async_copy(ins[a], stage[a], in_sems.at[a]) for a in range(n)]
        for cp in loads:
            cp.start()
        sends = [_remote(ins[a], mine(a), send_sems.at[a], recv_sems.at[a], (x, y, 1 - c)) for a in range(n)]
        for cp in sends:
            cp.start()
        stores = [pltpu.make_async_copy(stage[a], mine(a), out_sems.at[a]) for a in range(n)]
        for a in range(n):
            loads[a].wait()
            stores[a].start()
        for a in range(n):
            h = rs[a].shape[0]
            land = outs[a].at[pl.ds((1 - c) * h, h)]
            _remote(land, land, send_sems.at[a], recv_sems.at[a], (x, y, 1 - c)).wait_recv()
        for cp in sends:
            cp.wait_send()
        for cp in stores:
            cp.wait()

    return pl.pallas_call(
        body, name="grads_pair_share", in_specs=[HBM_SPEC] * n, out_specs=[HBM_SPEC] * n,
        out_shape=[jax.ShapeDtypeStruct((2 * a.shape[0], a.shape[1]), a.dtype) for a in rs],
        scratch_shapes=[pltpu.VMEM(a.shape, a.dtype) for a in rs] + [pltpu.SemaphoreType.DMA((n,))] * 4,
        compiler_params=pltpu.CompilerParams(vmem_limit_bytes=VMEM_LIMIT),
    )(*rs)


def _small_all_reduce(buf):
    R = buf.shape[0]
    ndev = 2 * N_CHIPS

    def body(in_ref, out_ref, gath, send_sems, recv_sems):
        x, y, c = _coords()
        me = 4 * x + 2 * y + c
        gath[me] = in_ref[...]
        peers = []
        for d in range(1, ndev):
            px = 1 - x if d & 4 else x
            py = 1 - y if d & 2 else y
            pc = 1 - c if d & 1 else c
            peers.append((px, py, pc))
        sends = []
        for d, peer in enumerate(peers):
            cp = _remote(in_ref, gath.at[me], send_sems.at[d], recv_sems.at[d], peer)
            cp.start()
            sends.append(cp)
        for d, peer in enumerate(peers):
            land = gath.at[4 * peer[0] + 2 * peer[1] + peer[2]]
            _remote(land, land, send_sems.at[d], recv_sems.at[d], peer).wait_recv()
        for cp in sends:
            cp.wait_send()
        acc = gath[0]
        for s in range(1, ndev):
            acc = acc + gath[s]
        out_ref[...] = acc

    return pl.pallas_call(
        body, name="small_all_reduce", in_specs=[VMEM_SPEC], out_specs=VMEM_SPEC,
        out_shape=jax.ShapeDtypeStruct(buf.shape, F32),
        scratch_shapes=[pltpu.VMEM((ndev, R, LANE), F32), pltpu.SemaphoreType.DMA((ndev - 1,)),
                        pltpu.SemaphoreType.DMA((ndev - 1,))],
    )(buf)


def _pair_add(name, own, other):
    h = own.shape[1]
    tm = _tile(h, (128, 64, 32, 16))

    def fn(i, a, b):
        return (a + b,)

    return _rowcall(name, fn, h, tm, [(own, "row", None), (other, "row", None)], [(own.shape, BF16, "row")])[0]


def _chip_add(name, own, parts):
    h = parts.shape[1]
    tm = _tile(h, (128, 64, 32, 16))

    def fn(i, o, a):
        a = a.astype(F32)
        return (((o.astype(F32) + a[0]) + a[1]) + a[2],)

    return _rowcall(name, fn, h, tm, [(own, "row", None), (parts, "row", None)], [(parts.shape[1:], F32, "row")])[0]


def _adamw(name, w, g, m, v):
    R = w.shape[0]
    tm = _tile(R, (256, 128, 64, 32, 16, 8))

    def fn(i, wv, gv, mv, vv):
        m2 = ADAM_B1 * mv + (1.0 - ADAM_B1) * gv
        v2 = ADAM_B2 * vv + (1.0 - ADAM_B2) * (gv * gv)
        m_hat = m2 / (1.0 - ADAM_B1 ** ADAM_STEP)
        v_hat = v2 / (1.0 - ADAM_B2 ** ADAM_STEP)
        delta = -ADAM_LR * (m_hat / (jnp.sqrt(v_hat) + ADAM_EPS) + ADAM_WD * wv)
        return delta, m2, v2

    ins = [(a, "row", None) for a in (w, g, m, v)]
    return _rowcall(name, fn, R, tm, ins, [(w.shape, F32, "row")] * 3)


def _pack(arrs):
    flat = jnp.concatenate([a.reshape(-1).astype(F32) for a in arrs])
    size = flat.shape[0]
    padded = -(-size // (SUBLANE * LANE)) * SUBLANE * LANE
    return jnp.pad(flat, (0, padded - size)).reshape(-1, LANE)


def _unpack(buf, shapes):
    flat = buf.reshape(-1)
    out, off = [], 0
    for s in shapes:
        size = math.prod(s)
        out.append(flat[off:off + size].reshape(s))
        off += size
    return out


BIG = ("a_w_in", "a_w_out", "w_kv", "b_w_q", "b_w_o", "ffn_w_up", "ffn_w_down")
WEIGHTS = ("a_norm_w", "a_w_in", "a_conv_w", "a_a_log", "a_dt_bias", "a_out_norm_w", "a_w_out", "kv_norm_w", "w_kv",
           "b_norm_w", "b_w_q", "b_sinks", "b_w_o", "rel_bias_table", "ffn_norm_w", "ffn_w_up", "ffn_conv_w",
           "ffn_conv_b", "ffn_w_down", "final_norm_w")
SMALL = tuple(n for n in WEIGHTS if n not in BIG)
SMALL_SHARDED = {"a_norm_w": 1, "a_conv_w": 2, "ffn_conv_w": 2}


def _quarter_2d(name, a):
    if name in ("ffn_w_up", "ffn_w_down"):
        return a.reshape(a.shape[0] * a.shape[1], a.shape[2])
    return a.reshape(a.shape[-2], a.shape[-1])


def _whole_weights(w):
    bigs = [_quarter_2d(n, w[n]).astype(BF16) for n in BIG]
    smalls = [w["a_norm_w"], w["a_conv_w"][0], w["ffn_conv_w"].reshape(6, DFF2_SHARD)]
    remote = [True] + [False] * (len(bigs) - 1) + [True] * len(smalls)
    g = _all_gather(bigs + smalls, [True] * len(bigs) + [False] * len(smalls), remote)
    w_in = g[0].transpose(1, 0, 2).reshape(D, GDN_IN)
    nh = GDN_V_HEADS
    zpad = jnp.zeros((D, LANE - nh), BF16)
    w_in_ba = jnp.concatenate([w_in[:, GDN_MAIN:GDN_MAIN + nh], zpad, w_in[:, GDN_MAIN + nh:], zpad], axis=1)
    lane_pad = lambda a: jnp.pad(a, ((0, 0), (0, LANE - nh)))
    early = dict(
        a_norm_w=g[7].reshape(1, D), w_in_main=w_in[:, :GDN_MAIN], w_in_ba=w_in_ba,
        a_conv_w=g[8].transpose(1, 0, 2).reshape(4, GDN_CONV), a_log=lane_pad(w["a_a_log"]),
        dt_bias=lane_pad(w["a_dt_bias"]), a_out_norm_w=w["a_out_norm_w"],
        kv_norm_w=w["kv_norm_w"].reshape(1, D), b_norm_w=w["b_norm_w"],
        sinks=jnp.broadcast_to(w["b_sinks"].reshape(SWA_KV_HEADS, SWA_GROUP, 1), (SWA_KV_HEADS, SWA_GROUP, LANE)),
        rel_table_t=w["rel_bias_table"].T, ffn_norm_w=w["ffn_norm_w"],
        ffn_conv_w=g[9].reshape(N_CHIPS, 2, 3, DFF2_SHARD).transpose(1, 2, 0, 3).reshape(2, 3, DFF2),
        ffn_conv_b=w["ffn_conv_b"], final_norm_w=w["final_norm_w"].reshape(1, D),
    )
    return early, (bigs[1:], g[1:len(bigs)])


def _late_weights(g):
    return dict(
        w_out=g[0].reshape(GDN_V, D), w_kv=g[1].reshape(D, 2 * SWA_KV_HEADS * SWA_HD), w_q=g[2].reshape(D, D),
        w_o=g[3].reshape(D, D), w_up=g[4].reshape(N_CHIPS, 2, D, DFF2_SHARD),
        w_down=g[5].reshape(N_CHIPS, 2, DFF_SHARD, D).transpose(1, 0, 2, 3).reshape(2, DFF, D),
    )


def _chip_major(name, g):
    if name == "a_w_in":
        return g.reshape(D, N_CHIPS, GDN_IN_SHARD).transpose(1, 0, 2)
    if name == "ffn_w_up":
        return g.reshape(N_CHIPS, 2 * D, DFF2_SHARD)
    if name == "ffn_w_down":
        return g.reshape(N_CHIPS, 2 * DFF_SHARD, D)
    return g.reshape(N_CHIPS, g.shape[0] // N_CHIPS, g.shape[1])


def kernel(x, a_norm_w, a_w_in, a_conv_w, a_a_log, a_dt_bias, a_out_norm_w, a_w_out, kv_norm_w, w_kv, b_norm_w, b_w_q, b_sinks, b_w_o, rel_bias_table, ffn_norm_w, ffn_w_up, ffn_conv_w, ffn_conv_b, ffn_w_down, final_norm_w, loss_target, m_a_norm_w, m_a_w_in, m_a_conv_w, m_a_a_log, m_a_dt_bias, m_a_out_norm_w, m_a_w_out, m_kv_norm_w, m_w_kv, m_b_norm_w, m_b_w_q, m_b_sinks, m_b_w_o, m_rel_bias_table, m_ffn_norm_w, m_ffn_w_up, m_ffn_conv_w, m_ffn_conv_b, m_ffn_w_down, m_final_norm_w, v_a_norm_w, v_a_w_in, v_a_conv_w, v_a_a_log, v_a_dt_bias, v_a_out_norm_w, v_a_w_out, v_kv_norm_w, v_w_kv, v_b_norm_w, v_b_w_q, v_b_sinks, v_b_w_o, v_rel_bias_table, v_ffn_norm_w, v_ffn_w_up, v_ffn_conv_w, v_ffn_conv_b, v_ffn_w_down, v_final_norm_w):
    w = dict(zip(WEIGHTS, (a_norm_w, a_w_in, a_conv_w, a_a_log, a_dt_bias, a_out_norm_w, a_w_out, kv_norm_w, w_kv,
                           b_norm_w, b_w_q, b_sinks, b_w_o, rel_bias_table, ffn_norm_w, ffn_w_up, ffn_conv_w,
                           ffn_conv_b, ffn_w_down, final_norm_w)))
    m = dict(zip(WEIGHTS, (m_a_norm_w, m_a_w_in, m_a_conv_w, m_a_a_log, m_a_dt_bias, m_a_out_norm_w, m_a_w_out,
                           m_kv_norm_w, m_w_kv, m_b_norm_w, m_b_w_q, m_b_sinks, m_b_w_o, m_rel_bias_table,
                           m_ffn_norm_w, m_ffn_w_up, m_ffn_conv_w, m_ffn_conv_b, m_ffn_w_down, m_final_norm_w)))
    v = dict(zip(WEIGHTS, (v_a_norm_w, v_a_w_in, v_a_conv_w, v_a_a_log, v_a_dt_bias, v_a_out_norm_w, v_a_w_out,
                           v_kv_norm_w, v_w_kv, v_b_norm_w, v_b_w_q, v_b_sinks, v_b_w_o, v_rel_bias_table,
                           v_ffn_norm_w, v_ffn_w_up, v_ffn_conv_w, v_ffn_conv_b, v_ffn_w_down, v_final_norm_w)))
    T = x.shape[1]
    chip = 2 * lax.axis_index("x") + lax.axis_index("y")

    core = lax.axis_index("c")

    def pair_sums(names, whole, other):
        own = [lax.dynamic_slice_in_dim(g, core * (g.shape[1] // 2), g.shape[1] // 2, 1) for g in whole]
        return [_pair_add(f"pair_add_{n}", a, b) for n, a, b in zip(names, own, other)]

    early, late = _whole_weights(w)
    loss_part, dx, grads, scattered = _local_step(x.reshape(T, D), loss_target.reshape(T, D), early, late, pair_sums)

    assert all(n in scattered for n in BIG)
    halves = [_chip_add(f"chip_add_{n}", lax.dynamic_index_in_dim(scattered[n][0], chip, 0, keepdims=False),
                        scattered[n][1]) for n in BIG]
    quarter = _pair_share(halves)
    out_g, out_d, out_m, out_v = {}, {}, {}, {}
    for n, g2 in zip(BIG, quarter):
        res = _adamw(f"adamw_{n}", _quarter_2d(n, w[n]), g2, _quarter_2d(n, m[n]), _quarter_2d(n, v[n]))
        out_g[n] = g2.reshape(w[n].shape)
        out_d[n], out_m[n], out_v[n] = (r.reshape(w[n].shape) for r in res)

    whole = [grads[n] for n in SMALL]
    summed = _unpack(_small_all_reduce(_pack([loss_part[0:1, 0:1]] + whole)), [(1, 1)] + [a.shape for a in whole])
    loss = summed[0].reshape(())
    small_g = []
    for n, g in zip(SMALL, summed[1:]):
        if n in SMALL_SHARDED:
            axis = SMALL_SHARDED[n]
            g = g.reshape(w[n].shape[:axis] + (-1,) + w[n].shape[axis + 1:])
            size = w[n].shape[axis]
            g = lax.dynamic_slice_in_dim(g, chip * size, size, axis)
        small_g.append(g.reshape(w[n].shape))
    shapes = [w[n].shape for n in SMALL]
    res = _adamw("adamw_small", _pack([w[n] for n in SMALL]), _pack(small_g), _pack([m[n] for n in SMALL]),
                 _pack([v[n] for n in SMALL]))
    small_d, small_m, small_v = (_unpack(r, shapes) for r in res)
    for i, n in enumerate(SMALL):
        out_g[n], out_d[n], out_m[n], out_v[n] = small_g[i], small_d[i], small_m[i], small_v[i]

    return (loss, dx.reshape(x.shape), *[out_g[n] for n in WEIGHTS], *[out_d[n] for n in WEIGHTS],
            *[out_m[n] for n in WEIGHTS], *[out_v[n] for n in WEIGHTS])
```

```python
import functools
import math

import jax
import jax.numpy as jnp
from jax import lax
from jax.experimental import pallas as pl
from jax.experimental.pallas import tpu as pltpu

F32 = jnp.float32
BF16 = jnp.bfloat16
MESH = pl.DeviceIdType.MESH
HIGHEST = lax.Precision.HIGHEST

D = 1024
EPS = 1e-6
NEG_INF = -1e30
N_CHIPS = 4

GDN_QK_HEADS = 8
GDN_V_HEADS = 16
GDN_HD = 128
GDN_QK = GDN_QK_HEADS * GDN_HD
GDN_V = GDN_V_HEADS * GDN_HD
GDN_CONV = 2 * GDN_QK + GDN_V
GDN_MAIN = GDN_CONV + GDN_V
GDN_IN = GDN_MAIN + 2 * GDN_V_HEADS
GDN_IN_SHARD = GDN_IN // N_CHIPS
GDN_CHUNK = 64

SWA_Q_HEADS = 16
SWA_KV_HEADS = 4
SWA_GROUP = 4
SWA_HD = 64
SWA_BLOCK = 128
REL_BUCKETS = 32
REL_MAX_DISTANCE = 128

DFF = 2816
DFF2 = 2 * DFF
DFF2_SHARD = DFF2 // N_CHIPS
DFF_SHARD = DFF // N_CHIPS

ADAM_LR = 0.001
ADAM_B1 = 0.9
ADAM_B2 = 0.999
ADAM_EPS = 1e-08
ADAM_WD = 0.01
ADAM_STEP = 10

LANE = 128
SUBLANE = 8
VMEM_LIMIT = 56 * 1024 * 1024


def _params(sem, vmem=VMEM_LIMIT):
    return pltpu.CompilerParams(dimension_semantics=sem, vmem_limit_bytes=vmem)


def _rowcall(name, fn, T, tm, ins, outs, swap=()):
    n = T // tm
    nswap = len(swap)
    swap_shapes = [g.shape for g in swap]
    r8 = tm // SUBLANE
    last8 = T // SUBLANE - 1
    arrays, in_specs = [], []
    for arr, kind, cols in ins:
        arrays.append(arr)
        if kind == "full":
            in_specs.append(pl.BlockSpec(arr.shape, functools.partial(lambda nd, i: (0,) * nd, arr.ndim)))
        elif arr.ndim == 2:
            w, ci = cols if cols is not None else (arr.shape[1], 0)
            if kind == "row":
                in_specs.append(pl.BlockSpec((tm, w), functools.partial(lambda ci, i: (i, ci), ci)))
            elif kind == "prev":
                in_specs.append(pl.BlockSpec(
                    (SUBLANE, w), functools.partial(lambda ci, i: (jnp.maximum(i * r8 - 1, 0), ci), ci)))
            else:
                in_specs.append(pl.BlockSpec(
                    (SUBLANE, w), functools.partial(lambda ci, i: (jnp.minimum((i + 1) * r8, last8), ci), ci)))
        else:
            lead = arr.shape[:-2]
            in_specs.append(pl.BlockSpec(lead + (tm, arr.shape[-1]),
                                         functools.partial(lambda nl, i: (0,) * nl + (i, 0), len(lead))))
    out_shape, out_specs = [], []
    for shape, dtype, kind in outs:
        out_shape.append(jax.ShapeDtypeStruct(shape, dtype))
        if kind == "acc":
            out_specs.append(pl.BlockSpec(shape, functools.partial(lambda nd, i: (0,) * nd, len(shape))))
        else:
            lead = shape[:-2]
            out_specs.append(pl.BlockSpec(lead + (tm, shape[-1]),
                                          functools.partial(lambda nl, i: (0,) * nl + (i, 0), len(lead))))
    nin = len(arrays)

    nout = len(outs)

    def body(*refs):
        i = pl.program_id(0)
        if nswap:
            comm = (swap_shapes, refs[nin:nin + nswap], refs[nin + nswap + nout:nin + 2 * nswap + nout], refs[-2],
                    refs[-1])

            @pl.when(i == 0)
            def _():
                for cp in _swap_copies(*comm):
                    cp.start()

            @pl.when(i == n - 1)
            def _():
                for cp in _swap_copies(*comm):
                    cp.wait()

        vals = [r[...] for r in refs[:nin]]
        res = fn(i, *vals)
        for (shape, dtype, kind), o, r in zip(outs, refs[nin + nswap:], res):
            if kind == "row":
                o[...] = r.astype(dtype)
            else:
                @pl.when(i == 0)
                def _():
                    o[...] = r.astype(dtype)

                @pl.when(i > 0)
                def _():
                    o[...] += r.astype(dtype)

    anywhere = pl.BlockSpec(memory_space=pl.ANY)
    nsem = nswap * N_CHIPS * PAIR_SWAP_PIECES
    return pl.pallas_call(
        body, name=name, grid=(n,), in_specs=in_specs + [anywhere] * nswap, out_specs=out_specs + [anywhere] * nswap,
        out_shape=out_shape + _swap_out_shapes(swap),
        scratch_shapes=[pltpu.SemaphoreType.DMA((nsem,)), pltpu.SemaphoreType.DMA((nsem,))] if nswap else [],
        compiler_params=_params(("arbitrary",)),
    )(*arrays, *swap)


def _mm(name, a, b, out_shape, out_dtype, grid, a_spec, b_spec, o_spec, dims, acc_shape, res=None, precision=None,
        into=None, scatter=(), post=None):
    nk = grid[2]
    ns = len(scatter)
    a, norm_w = a if isinstance(a, tuple) else (a, None)
    normed = norm_w is not None
    posted = post is not None
    n_in = 2 + normed + (res is not None) + (into is not None) + 3 * posted + ns
    n_out = 1 + posted + ns

    def body(*refs):
        a_ref, b_ref, o_ref = refs[0], refs[1], refs[n_in]
        r_ref = refs[2 + normed] if res is not None else None
        first = pl.program_id(0) == 0
        if ns:
            comm = (refs[n_in - ns:n_in], refs[n_in + n_out - ns:n_in + n_out], refs[-2], refs[-1])
            steps = [pl.program_id(d) for d in range(3)]

            @pl.when((steps[0] == 0) & (steps[1] == 0) & (steps[2] == 0))
            def _():
                for cp in _scatter_copies(*comm):
                    cp.start()

            @pl.when((steps[0] == grid[0] - 1) & (steps[1] == grid[1] - 1) & (steps[2] == grid[2] - 1))
            def _():
                copies = _scatter_copies(*comm)
                for cp in copies:
                    cp.wait_recv()
                for cp in copies:
                    cp.wait_send()

        av, bv = a_ref[...], b_ref[...]
        if normed:
            av = _rms_core(av, refs[2][...])
        if precision is None:
            av, bv = av.astype(BF16), bv.astype(BF16)
        p = lax.dot_general(av, bv, (dims, ((), ())), preferred_element_type=F32, precision=precision)

        def finish(x):
            if res is not None:
                x = x + r_ref[...].astype(F32)
            if posted:
                h_ref, w_ref, add_ref = refs[n_in - ns - 3:n_in - ns]
                dh, dw = jax.vjp(_rms_core, h_ref[...], w_ref[...])[1](x)
                x = dh + add_ref[...]
                dw_ref = refs[n_in + 1]

                @pl.when(first)
                def _():
                    dw_ref[...] = dw

                @pl.when(jnp.logical_not(first))
                def _():
                    dw_ref[...] += dw

            o_ref[...] = x.astype(out_dtype).reshape(o_ref.shape)

        if nk == 1:
            finish(p)
        else:
            acc = refs[n_in + n_out]
            k = pl.program_id(2)

            @pl.when(k == 0)
            def _():
                acc[...] = p

            @pl.when(k > 0)
            def _():
                acc[...] += p

            @pl.when(k == nk - 1)
            def _():
                finish(acc[...])

    anywhere = pl.BlockSpec(memory_space=pl.ANY)
    ops = [a, b] + ([norm_w] if normed else []) + ([res] if res is not None else [])
    ops += ([into] if into is not None else []) + (list(post) if posted else []) + list(scatter)
    whole = lambda arr: pl.BlockSpec(arr.shape, lambda i, j, k: (0, 0))
    specs = [a_spec, b_spec] + ([whole(norm_w)] if normed else [])
    specs += [o_spec] if res is not None else []
    specs += [anywhere] if into is not None else []
    specs += ([o_spec, whole(post[1]), o_spec] if posted else []) + [anywhere] * ns
    out = pl.pallas_call(
        body, name=name, grid=grid, in_specs=specs,
        out_specs=[o_spec] + ([whole(post[1])] if posted else []) + [anywhere] * ns,
        out_shape=[jax.ShapeDtypeStruct(out_shape, out_dtype)]
        + ([jax.ShapeDtypeStruct(post[1].shape, F32)] if posted else []) + _scatter_shapes(scatter),
        input_output_aliases={2 + normed + (res is not None): 0} if into is not None else {},
        scratch_shapes=([pltpu.VMEM(acc_shape, F32)] if nk > 1 else [])
        + ([pltpu.SemaphoreType.DMA((3 * ns,)), pltpu.SemaphoreType.DMA((3 * ns,))] if ns else []),
        compiler_params=_params(("arbitrary",) * 3 if ns or posted else ("parallel", "parallel", "arbitrary")),
    )(*ops)
    return out if n_out > 1 else out[0]


NN = ((1,), (0,))
NT = ((1,), (1,))
TN = ((0,), (0,))


BIG_TILES = (1024, 512, 256, 128)


def _tile(n, pref):
    for t in pref:
        if n % t == 0:
            return t
    return n


def _rows_of(a):
    return a[0] if isinstance(a, tuple) else a


def _mm_nn(name, a, w, out_dtype, res=None, precision=None):
    M, K = _rows_of(a).shape
    N = w.shape[1]
    tm = _tile(M, BIG_TILES if K <= 2048 else BIG_TILES[1:])
    tn = _tile(N, BIG_TILES)
    return _mm(name, a, w, (M, N), out_dtype, (M // tm, N // tn, 1),
               pl.BlockSpec((tm, K), lambda i, j, k: (i, 0)), pl.BlockSpec((K, tn), lambda i, j, k: (0, j)),
               pl.BlockSpec((tm, tn), lambda i, j, k: (i, j)), NN, (tm, tn), res=res, precision=precision)


def _mm_nt(name, g, w, out_dtype, res=None, precision=None, scatter=(), post=None):
    M, N = g.shape
    K = w.shape[0]
    tm, tk = _tile(M, BIG_TILES if post is None else BIG_TILES[1:]), _tile(K, (1024, 1408, 512, 256, 128))
    tn = _tile(N, (1536,) + BIG_TILES)
    return _mm(name, g, w, (M, K), out_dtype, (M // tm, K // tk, N // tn),
               pl.BlockSpec((tm, tn), lambda i, j, k: (i, k)), pl.BlockSpec((tk, tn), lambda i, j, k: (j, k)),
               pl.BlockSpec((tm, tk), lambda i, j, k: (i, j)), NT, (tm, tk), res=res, precision=precision,
               scatter=scatter, post=post)


def _mm_tn(name, a, g, out_dtype=F32, precision=None):
    T, K = _rows_of(a).shape
    N = g.shape[1]
    tk, tn = _tile(K, (1024, 1408, 512, 256, 128)), _tile(N, BIG_TILES)
    assert tk == K or not isinstance(a, tuple)
    tt = _tile(T, BIG_TILES)
    return _mm(name, a, g, (K, N), out_dtype, (K // tk, N // tn, T // tt),
               pl.BlockSpec((tt, tk), lambda i, j, k: (k, i)), pl.BlockSpec((tt, tn), lambda i, j, k: (k, j)),
               pl.BlockSpec((tk, tn), lambda i, j, k: (i, j)), TN, (tk, tn), precision=precision)


def _mm_up(name, n, wup, layer):
    T = _rows_of(n).shape[0]
    tm = _tile(T, BIG_TILES)
    return _mm(name, n, wup, (T, DFF2), BF16, (T // tm, N_CHIPS, 1),
               pl.BlockSpec((tm, D), lambda i, j, k: (i, 0)),
               pl.BlockSpec((None, None, D, DFF2_SHARD), lambda i, j, k: (j, layer, 0, 0)),
               pl.BlockSpec((tm, DFF2_SHARD), lambda i, j, k: (i, j)), NN, (tm, DFF2_SHARD))


def _mm_up_nt(name, du, wup, layer, post):
    T = du.shape[0]
    tm, tk = _tile(T, BIG_TILES[1:]), D
    return _mm(name, du, wup, (T, D), F32, (T // tm, D // tk, N_CHIPS),
               pl.BlockSpec((tm, DFF2_SHARD), lambda i, j, k: (i, k)),
               pl.BlockSpec((None, None, tk, DFF2_SHARD), lambda i, j, k: (k, layer, j, 0)),
               pl.BlockSpec((tm, tk), lambda i, j, k: (i, j)), NT, (tm, tk), post=post)


def _mm_up_tn(name, n, du, layer, into):
    T = _rows_of(n).shape[0]
    tk, tt = D, _tile(T, BIG_TILES)
    return _mm(name, n, du, (N_CHIPS, 2, D, DFF2_SHARD), F32, (D // tk, N_CHIPS, T // tt),
               pl.BlockSpec((tt, tk), lambda i, j, k: (k, i)), pl.BlockSpec((tt, DFF2_SHARD), lambda i, j, k: (k, j)),
               pl.BlockSpec((None, None, tk, DFF2_SHARD), lambda i, j, k: (j, layer, i, 0)), TN, (tk, DFF2_SHARD),
               into=into)


def _mm_down_tn(name, act, dout, layer, into):
    T = act.shape[0]
    tk, tn, tt = 2 * DFF_SHARD, _tile(D, BIG_TILES), _tile(T, BIG_TILES)
    return _mm(name, act, dout, (2, 2, 2, DFF_SHARD, D), F32, (DFF // tk, D // tn, T // tt),
               pl.BlockSpec((tt, tk), lambda i, j, k: (k, i)), pl.BlockSpec((tt, tn), lambda i, j, k: (k, j)),
               pl.BlockSpec((None, 2, None, DFF_SHARD, tn), lambda i, j, k: (i, 0, layer, 0, j)), TN, (tk, tn),
               into=into)


def _sigmoid(x):
    return 0.5 * jnp.tanh(0.5 * x) + 0.5


def _silu(x):
    return x * _sigmoid(x)


def _softplus(x):
    return jnp.maximum(x, 0.0) + jnp.log(1.0 + jnp.exp(-jnp.abs(x)))


def _rms_core(h, w):
    return h * lax.rsqrt(jnp.mean(h * h, axis=-1, keepdims=True) + EPS) * w


def _shift_down(x, halo, s, i):
    if s == 0:
        return x
    tm = x.shape[0]
    rolled = pltpu.roll(x, s, 0)
    patch = pltpu.roll(jnp.where(i == 0, 0.0, halo), s, 0)
    row = lax.broadcasted_iota(jnp.int32, patch.shape, 0)
    top = jnp.where(row < s, patch, rolled[:SUBLANE])
    return jnp.concatenate([top, rolled[SUBLANE:]], axis=0) if tm > SUBLANE else top


def _shift_up(x, halo, s, i, n):
    if s == 0:
        return x
    tm = x.shape[0]
    rolled = pltpu.roll(x, tm - s, 0)
    patch = pltpu.roll(jnp.where(i == n - 1, 0.0, halo), SUBLANE - s, 0)
    row = lax.broadcasted_iota(jnp.int32, patch.shape, 0)
    bottom = jnp.where(row >= SUBLANE - s, patch, rolled[tm - SUBLANE:])
    return jnp.concatenate([rolled[:tm - SUBLANE], bottom], axis=0) if tm > SUBLANE else bottom


def _taps(x, halo, K, i):
    return [_shift_down(x, halo, K - 1 - j, i) for j in range(K)]


def _conv_fwd(taps, w):
    y = w[0:1, :] * taps[0]
    for j in range(1, len(taps)):
        y = y + w[j:j + 1, :] * taps[j]
    return y


def _conv_dx(dy, halo_next, w, i, n):
    K = w.shape[0]
    dx = w[K - 1:K, :] * dy
    for j in range(K - 1):
        dx = dx + w[j:j + 1, :] * _shift_up(dy, halo_next, K - 1 - j, i, n)
    return dx


def _conv_dw(dy, taps):
    rows = [jnp.sum(dy * tap, axis=0, keepdims=True) for tap in taps]
    return jnp.concatenate(rows + [jnp.zeros((SUBLANE - len(taps), dy.shape[1]), F32)], axis=0)


def _l2(x):
    return x * lax.rsqrt(jnp.sum(x * x, axis=-1, keepdims=True) + EPS)


def _gdn_post_core(yq, yk, yv, pb, pa, a_log, dtb):
    qn = tuple(_l2(_silu(a)) * (GDN_HD ** -0.5) for a in yq)
    kn = tuple(_l2(_silu(a)) for a in yk)
    v = _silu(yv)
    beta = _sigmoid(pb)
    g = -jnp.exp(a_log) * _softplus(pa + dtb)
    return qn, kn, v, beta, g


def _heads(x, n):
    return tuple(x[:, GDN_HD * h:GDN_HD * (h + 1)] for h in range(n))


def _gdn_pre_fwd(pm, pba, conv_w, a_log, dtb, tm=128):
    T = pm.shape[0]
    tm = min(tm, T)

    def fn(i, x, halo, pbav, cw, al, db):
        y = _conv_fwd(_taps(x.astype(F32), halo.astype(F32), 4, i), cw)
        qn, kn, v, beta, g = _gdn_post_core(_heads(y[:, :GDN_QK], 8), _heads(y[:, GDN_QK:2 * GDN_QK], 8),
                                            y[:, 2 * GDN_QK:], pbav[:, :LANE], pbav[:, LANE:], al, db)
        return jnp.stack(qn), jnp.stack(kn), jnp.stack(_heads(v, GDN_V_HEADS)), beta, g

    ins = [(pm, "row", (GDN_CONV, 0)), (pm, "prev", (GDN_CONV, 0)), (pba, "row", None),
           (conv_w, "full", None), (a_log, "full", None), (dtb, "full", None)]
    outs = [((GDN_QK_HEADS, T, GDN_HD), BF16, "row"), ((GDN_QK_HEADS, T, GDN_HD), BF16, "row"),
            ((GDN_V_HEADS, T, GDN_HD), BF16, "row"), ((T, LANE), F32, "row"), ((T, LANE), F32, "row")]
    return _rowcall("gdn_pre_fwd", fn, T, tm, ins, outs)


def _gdn_pre_bwd(pm, pba, conv_w, a_log, dtb, dqn, dkn, dv, dbeta, dg, tm=128):
    T = pm.shape[0]
    tm = min(tm, T)

    def fn(i, x, halo, pbav, cw, al, db, dqv, dkv, dvv, dbv, dgv):
        taps = _taps(x.astype(F32), halo.astype(F32), 4, i)
        y = _conv_fwd(taps, cw)
        prim = (_heads(y[:, :GDN_QK], 8), _heads(y[:, GDN_QK:2 * GDN_QK], 8), y[:, 2 * GDN_QK:],
                pbav[:, :LANE], pbav[:, LANE:], al, db)
        _, vjp = jax.vjp(_gdn_post_core, *prim)
        cot = (tuple(dqv[h].astype(F32) for h in range(8)), tuple(dkv[h].astype(F32) for h in range(8)),
               jnp.concatenate([dvv[h].astype(F32) for h in range(GDN_V_HEADS)], axis=1), dbv, dgv)
        dyq, dyk, dyv, dpb, dpa, dal, ddb = vjp(cot)
        dy = jnp.concatenate(list(dyq) + list(dyk) + [dyv], axis=1)
        dcw = _conv_dw(dy, taps)
        return dy, jnp.concatenate([dpb, dpa], axis=1), dcw, dal, ddb

    ins = [(pm, "row", (GDN_CONV, 0)), (pm, "prev", (GDN_CONV, 0)), (pba, "row", None),
           (conv_w, "full", None), (a_log, "full", None), (dtb, "full", None),
           (dqn, "row", None), (dkn, "row", None), (dv, "row", None), (dbeta, "row", None), (dg, "row", None)]
    outs = [((T, GDN_CONV), BF16, "row"), ((T, 2 * LANE), F32, "row"), ((SUBLANE, GDN_CONV), F32, "acc"),
            ((1, LANE), F32, "acc"), ((1, LANE), F32, "acc")]
    return _rowcall("gdn_pre_bwd", fn, T, tm, ins, outs)


def _gdn_conv_bwd(dy, dz, conv_w, tm=256):
    T = dy.shape[0]
    tm = min(tm, T)
    n = T // tm

    def fn(i, dyv, halo, dzv, cw):
        dx = _conv_dx(dyv.astype(F32), halo.astype(F32), cw, i, n)
        return (jnp.concatenate([dx.astype(BF16), dzv.astype(BF16)], axis=1),)

    ins = [(dy, "row", None), (dy, "next", None), (dz, "row", None), (conv_w, "full", None)]
    return _rowcall("gdn_conv_bwd", fn, T, tm, ins, [((T, GDN_MAIN), BF16, "row")])[0]


def _bdot(a, b, dims=NN):
    return lax.dot_general(a.astype(BF16), b.astype(BF16), (dims, ((), ())), preferred_element_type=F32)


BNN = ((2,), (1,))
BNT = ((2,), (2,))
BTN = ((1,), (1,))


def _bmm(a, b, dims=BNN):
    return lax.dot_general(a.astype(BF16), b.astype(BF16), (dims, ((0,), (0,))), preferred_element_type=F32)


def _bmm3(a, b):
    ah, bh = a.astype(BF16), b.astype(BF16)
    al, bl = (a - ah.astype(F32)).astype(BF16), (b - bh.astype(F32)).astype(BF16)
    dn = (BNN, ((0,), (0,)))
    return (lax.dot_general(ah, bh, dn, preferred_element_type=F32)
            + lax.dot_general(al, bh, dn, preferred_element_type=F32)
            + lax.dot_general(ah, bl, dn, preferred_element_type=F32))


def _tri_inv(m):
    C = m.shape[-1]
    r = lax.broadcasted_iota(jnp.int32, (C, C), 0)
    c = lax.broadcasted_iota(jnp.int32, (C, C), 1)
    t = jnp.where(r == c, 1.0, 0.0) - m
    pw = _bmm3(m, m)
    t = t + _bmm3(t, pw)
    for _ in range(int(math.log2(C)) - 2):
        pw = _bmm(pw, pw)
        t = t + _bmm(t, pw)
    return t


def _tri_inv_vjp(t, dt):
    tt = jnp.swapaxes(t, 1, 2)
    return -_bmm(_bmm(tt, dt), tt)


def _twice(a):
    return jnp.broadcast_to(a[:, None], (a.shape[0], 2) + a.shape[1:]).reshape((2 * a.shape[0],) + a.shape[1:])


def _gdn_gates(grow, brow):
    C = grow.shape[2]
    r = lax.broadcasted_iota(jnp.int32, (C, C), 0)
    c = lax.broadcasted_iota(jnp.int32, (C, C), 1)
    tril, eye = r >= c, r == c
    gcol = jnp.sum(jnp.where(eye, grow, 0.0), axis=2, keepdims=True)
    bcol = jnp.sum(jnp.where(eye, brow, 0.0), axis=2, keepdims=True)
    gc_col = jnp.sum(jnp.where(tril, grow, 0.0), axis=2, keepdims=True)
    gc_row = jnp.sum(jnp.where(r <= c, gcol, 0.0), axis=1, keepdims=True)
    gc_last = jnp.sum(grow, axis=2, keepdims=True)
    decay = jnp.where(tril, jnp.exp(jnp.where(tril, gc_col - gc_row, 0.0)), 0.0)
    return bcol, gc_col, gc_last, decay


def _gdn_m(k, bcol, decay):
    C = k.shape[1]
    strict = lax.broadcasted_iota(jnp.int32, (C, C), 0) > lax.broadcasted_iota(jnp.int32, (C, C), 1)
    return jnp.where(strict, bcol * _twice(_bmm(k, k, BNT)) * decay, 0.0)


def _gdn_rest(q, k, v, bcol, gc_col, gc_last, decay, t_mat, S):
    qk = _twice(_bmm(q, k, BNT))
    k2, q2 = _twice(k), _twice(q)
    egc = jnp.exp(gc_col)
    u = _bmm(t_mat, v * bcol)
    w = _bmm(t_mat, k2 * (bcol * egc))
    v_new = u - _bmm(w, S)
    o = _bmm(q2 * egc, S) + _bmm(qk * decay, v_new)
    s_new = S * jnp.exp(gc_last) + _bmm(k2 * jnp.exp(gc_last - gc_col), v_new, BTN)
    return o, s_new


def _gdn_tb(T):
    return min(256, T)


def _gate_rows(g):
    T = g.shape[0]
    g = g[:, :GDN_V_HEADS].reshape(T // GDN_CHUNK, GDN_CHUNK, GDN_V_HEADS)
    return g.transpose(0, 2, 1)[:, :, None, :]


def _gate_cols(g):
    nc = g.shape[0]
    g = g[:, :, 0, :].transpose(0, 2, 1).reshape(nc * GDN_CHUNK, GDN_V_HEADS)
    return jnp.pad(g, ((0, 0), (0, LANE - GDN_V_HEADS)))


def _gdn_fwd(qn, kn, v, g, beta, gather=None):
    T = qn.shape[1]
    tb = _gdn_tb(T)
    nc = tb // GDN_CHUNK
    nsteps = T // tb
    quarters, buffers = gather if gather is not None else ((), ())
    ng = len(quarters)
    shapes = [a.shape for a in quarters]
    splits = [True] * ng

    def body(*refs):
        q_ref, k_ref, v_ref, g_ref, b_ref = refs[:5]
        src = refs[5:5 + ng]
        o_ref, sall_ref, tall_ref = refs[5 + 2 * ng:8 + 2 * ng]
        dst = refs[8 + 2 * ng:8 + 3 * ng]
        s_scr = refs[8 + 3 * ng]
        step = pl.program_id(0)

        @pl.when(step == 0)
        def _():
            s_scr[...] = jnp.zeros(s_scr.shape, F32)
            if ng:
                for cp in _gather_copies(shapes, splits, src, dst, *refs[9 + 3 * ng:])[0]:
                    cp.start()

        def chunk(ci, carry):
            rows = pl.ds(pl.multiple_of(ci * GDN_CHUNK, GDN_CHUNK), GDN_CHUNK)
            s = s_scr[...]
            sall_ref[ci] = s
            q, k = q_ref[:, rows, :].astype(F32), k_ref[:, rows, :].astype(F32)
            bcol, gc_col, gc_last, decay = _gdn_gates(g_ref[ci], b_ref[ci])
            t_mat = _tri_inv(_gdn_m(k, bcol, decay)).astype(BF16)
            tall_ref[ci] = t_mat
            o, s_new = _gdn_rest(q, k, v_ref[:, rows, :].astype(F32), bcol, gc_col, gc_last, decay,
                                 t_mat.astype(F32), s)
            o_ref[:, rows, :] = o.astype(o_ref.dtype)
            s_scr[...] = s_new
            return carry

        lax.fori_loop(0, nc, chunk, 0)

        if ng:
            @pl.when(step == nsteps - 1)
            def _():
                _gather_arrival(shapes, splits, src, dst, *refs[9 + 3 * ng:])

    qk_spec = pl.BlockSpec((GDN_QK_HEADS, tb, GDN_HD), lambda i: (0, i, 0))
    v_spec = pl.BlockSpec((GDN_V_HEADS, tb, GDN_HD), lambda i: (0, i, 0))
    g_spec = pl.BlockSpec((nc, GDN_V_HEADS, 1, GDN_CHUNK), lambda i: (i, 0, 0, 0))
    anywhere = pl.BlockSpec(memory_space=pl.ANY)
    return pl.pallas_call(
        body, name="gdn_fwd", grid=(nsteps,),
        in_specs=[qk_spec, qk_spec, v_spec, g_spec, g_spec] + [anywhere] * (2 * ng),
        out_specs=[v_spec, pl.BlockSpec((nc, GDN_V_HEADS, GDN_HD, GDN_HD), lambda i: (i, 0, 0, 0)),
                   pl.BlockSpec((nc, GDN_V_HEADS, GDN_CHUNK, GDN_CHUNK), lambda i: (i, 0, 0, 0))] + [anywhere] * ng,
        out_shape=[jax.ShapeDtypeStruct((GDN_V_HEADS, T, GDN_HD), BF16),
                   jax.ShapeDtypeStruct((T // GDN_CHUNK, GDN_V_HEADS, GDN_HD, GDN_HD), F32),
                   jax.ShapeDtypeStruct((T // GDN_CHUNK, GDN_V_HEADS, GDN_CHUNK, GDN_CHUNK), BF16)]
        + [jax.ShapeDtypeStruct(b.shape, b.dtype) for b in buffers],
        input_output_aliases={5 + ng + a: 3 + a for a in range(ng)},
        scratch_shapes=[pltpu.VMEM((GDN_V_HEADS, GDN_HD, GDN_HD), F32)]
        + ([pltpu.SemaphoreType.DMA((6 * ng,)), pltpu.SemaphoreType.DMA((6 * ng,))] if ng else []),
        compiler_params=_params(("arbitrary",)),
    )(qn, kn, v, g, beta, *quarters, *buffers)


def _gdn_bwd(qn, kn, v, g, beta, sall, tall, do, scatter=()):
    T = qn.shape[1]
    tb = _gdn_tb(T)
    nc = tb // GDN_CHUNK
    nb = T // tb
    ns = len(scatter)

    def body(*refs):
        q_ref, k_ref, v_ref, g_ref, b_ref, sall_ref, tall_ref, do_ref = refs[:8]
        dq_ref, dk_ref, dv_ref, dg_ref, db_ref = refs[8 + ns:13 + ns]
        ds_scr = refs[13 + 2 * ns]
        comm = (refs[8:8 + ns], refs[13 + ns:13 + 2 * ns], *refs[14 + 2 * ns:])
        step = pl.program_id(0)

        @pl.when(step == 0)
        def _():
            ds_scr[...] = jnp.zeros(ds_scr.shape, F32)
            if ns:
                for cp in _scatter_copies(*comm):
                    cp.start()

        def chunk(cr, carry):
            ci = nc - 1 - cr
            rows = pl.ds(pl.multiple_of(ci * GDN_CHUNK, GDN_CHUNK), GDN_CHUNK)
            k, t_mat = k_ref[:, rows, :].astype(F32), tall_ref[ci].astype(F32)
            (bcol, gc_col, gc_last, decay), vjp_gates = jax.vjp(_gdn_gates, g_ref[ci], b_ref[ci])
            _, vjp = jax.vjp(_gdn_rest, q_ref[:, rows, :].astype(F32), k, v_ref[:, rows, :].astype(F32),
                             bcol, gc_col, gc_last, decay, t_mat, sall_ref[ci])
            dq, dk, dv, dbcol, dgc_col, dgc_last, ddecay, dt, ds = vjp((do_ref[:, rows, :].astype(F32), ds_scr[...]))
            _, vjp_m = jax.vjp(_gdn_m, k, bcol, decay)
            dk_m, dbcol_m, ddecay_m = vjp_m(_tri_inv_vjp(t_mat, dt))
            dg, db = vjp_gates((dbcol + dbcol_m, dgc_col, dgc_last, ddecay + ddecay_m))
            ds_scr[...] = ds
            dq_ref[:, rows, :] = dq
            dk_ref[:, rows, :] = dk + dk_m
            dv_ref[:, rows, :] = dv
            dg_ref[ci] = dg
            db_ref[ci] = db
            return carry

        lax.fori_loop(0, nc, chunk, 0)

        if ns:
            @pl.when(step == nb - 1)
            def _():
                copies = _scatter_copies(*comm)
                for cp in copies:
                    cp.wait_recv()
                for cp in copies:
                    cp.wait_send()

    qk_spec = pl.BlockSpec((GDN_QK_HEADS, tb, GDN_HD), lambda i: (0, nb - 1 - i, 0))
    v_spec = pl.BlockSpec((GDN_V_HEADS, tb, GDN_HD), lambda i: (0, nb - 1 - i, 0))
    g_spec = pl.BlockSpec((nc, GDN_V_HEADS, 1, GDN_CHUNK), lambda i: (nb - 1 - i, 0, 0, 0))
    s_spec = pl.BlockSpec((nc, GDN_V_HEADS, GDN_HD, GDN_HD), lambda i: (nb - 1 - i, 0, 0, 0))
    t_spec = pl.BlockSpec((nc, GDN_V_HEADS, GDN_CHUNK, GDN_CHUNK), lambda i: (nb - 1 - i, 0, 0, 0))
    anywhere = pl.BlockSpec(memory_space=pl.ANY)
    return pl.pallas_call(
        body, name="gdn_bwd", grid=(nb,),
        in_specs=[qk_spec, qk_spec, v_spec, g_spec, g_spec, s_spec, t_spec, v_spec] + [anywhere] * ns,
        out_specs=[qk_spec, qk_spec, v_spec, g_spec, g_spec] + [anywhere] * ns,
        out_shape=[jax.ShapeDtypeStruct((GDN_QK_HEADS, T, GDN_HD), F32),
                   jax.ShapeDtypeStruct((GDN_QK_HEADS, T, GDN_HD), F32),
                   jax.ShapeDtypeStruct((GDN_V_HEADS, T, GDN_HD), F32),
                   jax.ShapeDtypeStruct(g.shape, F32), jax.ShapeDtypeStruct(g.shape, F32)]
        + _scatter_shapes(scatter),
        scratch_shapes=[pltpu.VMEM((GDN_V_HEADS, GDN_HD, GDN_HD), F32)]
        + ([pltpu.SemaphoreType.DMA((3 * ns,)), pltpu.SemaphoreType.DMA((3 * ns,))] if ns else []),
        compiler_params=_params(("arbitrary",)),
    )(qn, kn, v, g, beta, sall, tall, do, *scatter)


def _gnorm_core(o, z, w):
    return tuple(_rms_core(oh, w) * _silu(zh) for oh, zh in zip(o, z))


def _gnorm_fwd(o, pm, w, tm=256):
    T = pm.shape[0]
    tm = min(tm, T)

    def fn(i, ov, zv, wv):
        zf = zv.astype(F32)
        out = _gnorm_core(tuple(ov[h].astype(F32) for h in range(GDN_V_HEADS)), _heads(zf, GDN_V_HEADS), wv)
        return (jnp.concatenate(out, axis=1),)

    ins = [(o, "row", None), (pm, "row", (GDN_V, 2)), (w, "full", None)]
    return _rowcall("gnorm_fwd", fn, T, tm, ins, [((T, GDN_V), BF16, "row")])[0]


def _gnorm_bwd(o, pm, w, don, tm=128, swap=()):
    T = pm.shape[0]
    tm = min(tm, T)

    def fn(i, ov, zv, wv, dv):
        zf, df = zv.astype(F32), dv.astype(F32)
        _, vjp = jax.vjp(_gnorm_core, tuple(ov[h].astype(F32) for h in range(GDN_V_HEADS)),
                         _heads(zf, GDN_V_HEADS), wv)
        do, dz, dw = vjp(_heads(df, GDN_V_HEADS))
        return jnp.stack(do), jnp.concatenate(dz, axis=1), dw

    ins = [(o, "row", None), (pm, "row", (GDN_V, 2)), (w, "full", None), (don, "row", None)]
    outs = [((GDN_V_HEADS, T, GDN_HD), BF16, "row"), ((T, GDN_V), BF16, "row"), ((1, GDN_HD), F32, "acc")]
    return _rowcall("gnorm_bwd", fn, T, tm, ins, outs, swap=swap)


def _ffn_act_fwd(name, up, conv_w, conv_b, tm=256):
    T = up.shape[0]
    tm = min(tm, T)

    def fn(i, x, halo, cw, cb):
        u = _conv_fwd(_taps(x.astype(F32), halo.astype(F32), 3, i), cw) + cb
        return (_silu(u[:, :DFF]) * u[:, DFF:],)

    ins = [(up, "row", None), (up, "prev", None), (conv_w, "full", None), (conv_b, "full", None)]
    return _rowcall(name, fn, T, tm, ins, [((T, DFF), BF16, "row")])[0]


def _ffn_act_bwd(name, up, conv_w, conv_b, dact, tm=128):
    T = up.shape[0]
    tm = min(tm, T)

    def fn(i, x, halo, cw, cb, da):
        taps = _taps(x.astype(F32), halo.astype(F32), 3, i)
        da = da.astype(F32)
        u = _conv_fwd(taps, cw) + cb
        gate, val = u[:, :DFF], u[:, DFF:]
        sg = _sigmoid(gate)
        dgate = da * val * sg * (1.0 + gate * (1.0 - sg))
        dval = da * gate * sg
        du = jnp.concatenate([dgate, dval], axis=1)
        return du, _conv_dw(du, taps), jnp.sum(du, axis=0, keepdims=True)

    ins = [(up, "row", None), (up, "prev", None), (conv_w, "full", None), (conv_b, "full", None),
           (dact, "row", None)]
    outs = [((T, DFF2), BF16, "row"), ((SUBLANE, DFF2), F32, "acc"), ((1, DFF2), F32, "acc")]
    return _rowcall(name, fn, T, tm, ins, outs)


def _ffn_conv_bwd(name, du, conv_w, tm=256):
    T = du.shape[0]
    tm = min(tm, T)
    n = T // tm

    def fn(i, dv, halo, cw):
        return (_conv_dx(dv.astype(F32), halo.astype(F32), cw, i, n),)

    ins = [(du, "row", None), (du, "next", None), (conv_w, "full", None)]
    return _rowcall(name, fn, T, tm, ins, [((T, DFF2), BF16, "row")])[0]


GROUP_ROWS = SWA_GROUP * SWA_BLOCK


def _attn_core(q, kp, kc, vp, vc, bias, sink, mask):
    kcat = jnp.concatenate([kp, kc], axis=0)
    vcat = jnp.concatenate([vp, vc], axis=0)
    s = _bdot(q * (SWA_HD ** -0.5), kcat, NT) + bias
    s = jnp.where(mask, s, NEG_INF)
    m = lax.stop_gradient(jnp.maximum(jnp.max(s, axis=-1, keepdims=True), sink))
    p = jnp.exp(s - m)
    denom = jnp.sum(p, axis=-1, keepdims=True) + jnp.exp(sink - m)
    return _bdot(p / denom, vcat)


def _attn_mask(i):
    qi = lax.broadcasted_iota(jnp.int32, (GROUP_ROWS, 2 * SWA_BLOCK), 0) & (SWA_BLOCK - 1)
    ki = lax.broadcasted_iota(jnp.int32, (GROUP_ROWS, 2 * SWA_BLOCK), 1)
    dist = qi + SWA_BLOCK - ki
    return (dist >= 0) & (dist < SWA_BLOCK) & ((ki >= SWA_BLOCK) | (i > 0))


def _head_cols(h):
    return slice(h * SWA_HD, (h + 1) * SWA_HD)


def _block_rows(b):
    return slice(b * SWA_BLOCK, (b + 1) * SWA_BLOCK)


def _stacked_heads(ref, rows, j):
    return jnp.concatenate([ref[rows, _head_cols(SWA_GROUP * j + g)].astype(F32) for g in range(SWA_GROUP)], axis=0)


def _store_heads(ref, rows, j, stacked):
    for g in range(SWA_GROUP):
        ref[rows, _head_cols(SWA_GROUP * j + g)] = stacked[g * SWA_BLOCK:(g + 1) * SWA_BLOCK].astype(ref.dtype)


def _attn_chains(i, nblk, q_ref, kvc_ref, kvp_ref, b_ref, s_ref):
    chains = []
    for b in range(nblk):
        rows = _block_rows(b)
        mask = _attn_mask(i) if b == 0 else _attn_mask(1)
        before, before_rows = (kvp_ref, _block_rows(0)) if b == 0 else (kvc_ref, _block_rows(b - 1))
        for j in range(SWA_KV_HEADS):
            heads = slice(SWA_GROUP * j, SWA_GROUP * (j + 1))
            k_cols, v_cols = _head_cols(j), _head_cols(SWA_KV_HEADS + j)
            sink = jnp.concatenate(
                [jnp.broadcast_to(s_ref[j, g:g + 1, 0:1], (SWA_BLOCK, 1)) for g in range(SWA_GROUP)], axis=0)
            ops = (_stacked_heads(q_ref, rows, j), before[before_rows, k_cols].astype(F32),
                   kvc_ref[rows, k_cols].astype(F32), before[before_rows, v_cols].astype(F32),
                   kvc_ref[rows, v_cols].astype(F32), b_ref[heads].reshape(GROUP_ROWS, 2 * SWA_BLOCK), sink)
            chains.append((rows, j, ops, mask))
    return chains


def _attn_fwd_flat(q, kv, bias, sinks):
    T = q.shape[0]
    nblk = _tile(T // SWA_BLOCK, (4, 2, 1))
    rows = nblk * SWA_BLOCK

    def body(q_ref, kvc_ref, kvp_ref, b_ref, s_ref, o_ref):
        chains = _attn_chains(pl.program_id(0), nblk, q_ref, kvc_ref, kvp_ref, b_ref, s_ref)
        outs = [_attn_core(*ops, mask) for _, _, ops, mask in chains]
        for (blk, j, _, _), out in zip(chains, outs):
            _store_heads(o_ref, blk, j, out)

    q_spec = pl.BlockSpec((rows, q.shape[1]), lambda i: (i, 0))
    cur = pl.BlockSpec((rows, kv.shape[1]), lambda i: (i, 0))
    prev = pl.BlockSpec((SWA_BLOCK, kv.shape[1]), lambda i: (jnp.maximum(nblk * i - 1, 0), 0))
    return pl.pallas_call(
        body, name="attn_fwd", grid=(T // rows,),
        in_specs=[q_spec, cur, prev, pl.BlockSpec(bias.shape, lambda i: (0, 0, 0)),
                  pl.BlockSpec(sinks.shape, lambda i: (0, 0, 0))],
        out_specs=q_spec, out_shape=jax.ShapeDtypeStruct(q.shape, BF16),
        compiler_params=_params(("arbitrary",)),
    )(q, kv, kv, bias, sinks)


def _attn_bwd_flat(q, kv, bias, sinks, do):
    T = q.shape[0]
    nb = T // SWA_BLOCK
    rows = _block_rows(0)

    def body(q_ref, kvc_ref, kvp_ref, b_ref, s_ref, do_ref, dq_ref, dkv_ref, db_ref, dsk_ref, carry):
        i = pl.program_id(0)

        @pl.when(i < nb)
        def _():
            chains = _attn_chains(i, 1, q_ref, kvc_ref, kvp_ref, b_ref, s_ref)
            cots = [_stacked_heads(do_ref, rows, j) for _, j, _, _ in chains]
            grads = [jax.vjp(functools.partial(_attn_core, mask=mask), *ops)[1](cot)
                     for (_, _, ops, mask), cot in zip(chains, cots)]
            for j, (dq, dkp, dkc, dvp, dvc, db, dsc) in enumerate(grads):
                heads = slice(SWA_GROUP * j, SWA_GROUP * (j + 1))
                k_cols, v_cols = _head_cols(j), _head_cols(SWA_KV_HEADS + j)
                _store_heads(dq_ref, rows, j, dq)
                db = db.reshape(SWA_GROUP, SWA_BLOCK, 2 * SWA_BLOCK)
                dsk = jnp.concatenate(
                    [jnp.broadcast_to(jnp.sum(dsc[g * SWA_BLOCK:(g + 1) * SWA_BLOCK], axis=0, keepdims=True),
                                      (1, LANE)) for g in range(SWA_GROUP)], axis=0)

                @pl.when(i == 0)
                def _():
                    db_ref[heads] = db
                    dsk_ref[j] = dsk

                @pl.when(i > 0)
                def _():
                    db_ref[heads] += db
                    dsk_ref[j] += dsk
                    dkv_ref[:, k_cols] = (carry[:, k_cols] + dkp).astype(dkv_ref.dtype)
                    dkv_ref[:, v_cols] = (carry[:, v_cols] + dvp).astype(dkv_ref.dtype)

                carry[:, k_cols] = dkc
                carry[:, v_cols] = dvc

        @pl.when(i == nb)
        def _():
            dkv_ref[...] = carry[...].astype(dkv_ref.dtype)

    last = nb - 1
    q_spec = pl.BlockSpec((SWA_BLOCK, q.shape[1]), lambda i: (jnp.minimum(i, last), 0))
    cur = pl.BlockSpec((SWA_BLOCK, kv.shape[1]), lambda i: (jnp.minimum(i, last), 0))
    prev = pl.BlockSpec((SWA_BLOCK, kv.shape[1]), lambda i: (jnp.clip(i - 1, 0, last), 0))
    b_spec = pl.BlockSpec(bias.shape, lambda i: (0, 0, 0))
    s_spec = pl.BlockSpec(sinks.shape, lambda i: (0, 0, 0))
    return pl.pallas_call(
        body, name="attn_bwd", grid=(nb + 1,),
        in_specs=[q_spec, cur, prev, b_spec, s_spec, q_spec],
        out_specs=[q_spec, prev, b_spec, s_spec],
        out_shape=[jax.ShapeDtypeStruct(q.shape, BF16), jax.ShapeDtypeStruct(kv.shape, BF16),
                   jax.ShapeDtypeStruct(bias.shape, F32), jax.ShapeDtypeStruct(sinks.shape, F32)],
        scratch_shapes=[pltpu.VMEM((SWA_BLOCK, kv.shape[1]), F32)],
        compiler_params=_params(("arbitrary",)),
    )(q, kv, kv, bias, sinks, do)


def _rel_onehot():
    qi = jnp.arange(SWA_BLOCK)[:, None]
    ki = jnp.arange(2 * SWA_BLOCK)[None, :]
    n = jnp.maximum(qi + SWA_BLOCK - ki, 0)
    max_exact = REL_BUCKETS // 2
    nf = jnp.maximum(n, 1).astype(F32)
    large = max_exact + (jnp.log(nf / max_exact) / math.log(REL_MAX_DISTANCE / max_exact)
                         * (REL_BUCKETS - max_exact)).astype(jnp.int32)
    bucket = jnp.where(n < max_exact, n, jnp.minimum(large, REL_BUCKETS - 1)).reshape(-1)
    return (bucket[None, :] == jnp.arange(REL_BUCKETS)[:, None]).astype(F32)


def _final(h, w, target, tm=256):
    T = h.shape[0]
    tm = min(tm, T)

    def fn(i, hv, wv, tv):
        y, vjp = jax.vjp(_rms_core, hv, wv)
        err = y - tv
        dh, dw = vjp(err * (1.0 / D))
        part = 0.5 * jnp.sum(jnp.sum(err * err, axis=1, keepdims=True) * (1.0 / D), axis=0, keepdims=True)
        return jnp.broadcast_to(part, (SUBLANE, LANE)), dh, dw

    ins = [(h, "row", None), (w, "full", None), (target, "row", None)]
    outs = [((SUBLANE, LANE), F32, "acc"), ((T, D), F32, "row"), ((1, D), F32, "acc")]
    return _rowcall("final", fn, T, tm, ins, outs)


def _ffn_fwd(tag, h, P, layer):
    n = (h, P["ffn_norm_w"][layer:layer + 1])
    up = _mm_up(f"{tag}_up", n, P["w_up"], layer)
    act = _ffn_act_fwd(f"{tag}_act", up, P["ffn_conv_w"][layer], P["ffn_conv_b"][layer:layer + 1])
    out = _mm_nn(f"{tag}_down", act, P["w_down"][layer], F32, res=h)
    return out, (n, up, act)


def _ffn_bwd(tag, h, saved, dout, P, layer, into=(None, None)):
    n, up, act = saved
    cw, cb = P["ffn_conv_w"][layer], P["ffn_conv_b"][layer:layer + 1]
    dact = _mm_nt(f"{tag}_down_dx", dout, P["w_down"][layer], BF16)
    g_down = _mm_down_tn(f"{tag}_down_dw", act, dout, layer, into[1])
    du, dcw, dcb = _ffn_act_bwd(f"{tag}_act_bwd", up, cw, cb, dact)
    dup = _ffn_conv_bwd(f"{tag}_conv_bwd", du, cw)
    g_up = _mm_up_tn(f"{tag}_up_dw", n, dup, layer, into[0])
    dh, dnw = _mm_up_nt(f"{tag}_up_dx", dup, P["w_up"], layer, post=(h, P["ffn_norm_w"][layer:layer + 1], dout))
    return dh, dict(w_down=g_down, w_up=g_up, conv_w=dcw[:3], conv_b=dcb, norm_w=dnw)


def _local_step(x, target, P, late=None, pair_sums=None):
    T = x.shape[0]
    n0 = (x, P["a_norm_w"])
    pm = _mm_nn("gdn_in", n0, P["w_in_main"], BF16)
    pba = _mm_nn("gdn_in_ba", n0, P["w_in_ba"], F32)
    qn, kn, v, beta, g = _gdn_pre_fwd(pm, pba, P["a_conv_w"], P["a_log"], P["dt_bias"])
    g_rows, beta_rows = _gate_rows(g), _gate_rows(beta)
    o, sall, tall, *gathered = _gdn_fwd(qn, kn, v, g_rows, beta_rows, gather=late)
    if late is not None:
        P = {**P, **_late_weights(gathered)}
    on = _gnorm_fwd(o, pm, P["a_out_norm_w"])
    h1 = _mm_nn("gdn_out", on, P["w_out"], F32, res=x)
    h2, ffn0 = _ffn_fwd("ffn0", h1, P, 0)
    nkv = (h2, P["kv_norm_w"])
    kv = _mm_nn("kv_proj", nkv, P["w_kv"], BF16)
    nb = (h2, P["b_norm_w"])
    qp = _mm_nn("q_proj", nb, P["w_q"], BF16)
    onehot = _rel_onehot()
    bias = _mm_nn("rel_bias", P["rel_table_t"], onehot, F32, precision=HIGHEST)
    bias = bias.reshape(SWA_Q_HEADS, SWA_BLOCK, 2 * SWA_BLOCK)
    oa = _attn_fwd_flat(qp, kv, bias, P["sinks"])
    h3 = _mm_nn("o_proj", oa, P["w_o"], F32, res=h2)
    h4, ffn1 = _ffn_fwd("ffn1", h3, P, 1)
    loss, dh4, d_final = _final(h4, P["final_norm_w"], target)

    dh3, gf1 = _ffn_bwd("ffn1", h3, ffn1, dh4, P, 1)
    doa = _mm_nt("o_proj_dx", dh3, P["w_o"], BF16)
    g_wo = _mm_tn("o_proj_dw", oa, dh3)
    dqp, dkv, dbias, dsinks = _attn_bwd_flat(qp, kv, bias, P["sinks"], doa)
    g_wq = _mm_tn("q_proj_dw", nb, dqp)
    g_wkv = _mm_tn("kv_proj_dw", nkv, dkv)
    dh2, d_bnorm = _mm_nt("q_proj_dx", dqp, P["w_q"], F32, post=(h2, P["b_norm_w"], dh3))
    dh2, d_kvnorm = _mm_nt("kv_proj_dx", dkv, P["w_kv"], F32, post=(h2, P["kv_norm_w"], dh2))
    g_table = _mm_nt("rel_bias_dw", onehot, dbias.reshape(SWA_Q_HEADS, -1), F32, precision=HIGHEST)
    dh1, gf0 = _ffn_bwd("ffn0", h1, ffn0, dh2, P, 0, into=(gf1["w_up"], gf1["w_down"]))
    don = _mm_nt("gdn_out_dx", dh1, P["w_out"], BF16)
    g_wout = _mm_tn("gdn_out_dw", on, dh1)
    ready = dict(a_w_out=g_wout, w_kv=g_wkv, b_w_q=g_wq, b_w_o=g_wo, ffn_w_up=gf0["w_up"], ffn_w_down=gf0["w_down"])
    names = [n for n in BIG if n in ready]
    whole = [_chip_major(n, ready[n]) for n in names] if pair_sums is not None else []
    do, dz, d_gnorm, *other = _gnorm_bwd(o, pm, P["a_out_norm_w"], don, swap=whole)
    pairs = pair_sums(names, whole, other) if pair_sums is not None else []
    dq, dk, dv, dg, dbeta, *parts = _gdn_bwd(qn, kn, v, g_rows, beta_rows, sall, tall, do, scatter=pairs)
    dy, dpba, d_aconv, d_alog, d_dtb = _gdn_pre_bwd(pm, pba, P["a_conv_w"], P["a_log"], P["dt_bias"],
                                                    dq, dk, dv, _gate_cols(dbeta), _gate_cols(dg))
    dpm = _gdn_conv_bwd(dy, dz, P["a_conv_w"])
    g_win_main = _mm_tn("gdn_in_dw", n0, dpm)
    g_win_ba = _mm_tn("gdn_in_ba_dw", n0, dpba)
    nh = GDN_V_HEADS
    g_win = jnp.concatenate([g_win_main, g_win_ba[:, :nh], g_win_ba[:, LANE:LANE + nh]], axis=1)
    last_whole = [_chip_major("a_w_in", g_win)]
    last_pair = pair_sums(["a_w_in"], last_whole, _pair_swap(last_whole, "late")) if pair_sums is not None else []
    dn0 = _mm_nt("gdn_in_dx", dpm, P["w_in_main"], F32, scatter=last_pair)
    dn0, last_parts = (dn0[0], dn0[1:]) if last_pair else (dn0, [])
    dx, d_anorm = _mm_nt("gdn_in_ba_dx", dpba, P["w_in_ba"], F32, res=dn0, post=(x, P["a_norm_w"], dh1))

    nh = GDN_V_HEADS
    grads = dict(
        a_norm_w=d_anorm,
        a_w_in=g_win,
        a_conv_w=d_aconv[:4], a_a_log=d_alog[:, :nh], a_dt_bias=d_dtb[:, :nh], a_out_norm_w=d_gnorm,
        a_w_out=g_wout, kv_norm_w=d_kvnorm, w_kv=g_wkv, b_norm_w=d_bnorm, b_w_q=g_wq,
        b_sinks=dsinks[:, :, 0].reshape(1, SWA_Q_HEADS), b_w_o=g_wo, rel_bias_table=g_table,
        ffn_norm_w=jnp.concatenate([gf0["norm_w"], gf1["norm_w"]], axis=0),
        ffn_w_up=gf0["w_up"],
        ffn_conv_w=jnp.stack([gf0["conv_w"], gf1["conv_w"]], axis=0),
        ffn_conv_b=jnp.concatenate([gf0["conv_b"], gf1["conv_b"]], axis=0),
        ffn_w_down=gf0["w_down"],
        final_norm_w=d_final,
    )
    scattered = dict(zip([n for n in BIG if n in ready], zip(pairs, parts)))
    scattered.update(zip(["a_w_in"], zip(last_pair, last_parts)))
    return loss, dx, grads, scattered


HBM_SPEC = pl.BlockSpec(memory_space=pltpu.HBM)
VMEM_SPEC = pl.BlockSpec(memory_space=pltpu.VMEM)


def _coords():
    return lax.axis_index("x"), lax.axis_index("y"), lax.axis_index("c")


def _remote(src, dst, send_sem, recv_sem, device):
    return pltpu.make_async_remote_copy(src_ref=src, dst_ref=dst, send_sem=send_sem, recv_sem=recv_sem,
                                        device_id=device, device_id_type=MESH)


def _other_chips(x, y):
    return [(1 - x, y), (x, 1 - y), (1 - x, 1 - y)]


def _gather_copies(shapes, split, ins, outs, send_sems, recv_sems):
    x, y, c = _coords()
    p = 2 * x + y
    ici, forwards, from_sibling = [], [], []
    for a, shape in enumerate(shapes):
        h = shape[0] // 2
        for j, chip in enumerate(_other_chips(x, y)):
            q = 2 * chip[0] + chip[1]
            if split[a]:
                mine, theirs = pl.ds(c * h, h), pl.ds((1 - c) * h, h)
                ici.append(_remote(ins[a].at[mine], outs[a].at[p, mine], send_sems.at[6 * a + j],
                                   recv_sems.at[6 * a + j], (*chip, c)))
                land = outs[a].at[q, mine]
                forwards.append(_remote(land, land, send_sems.at[6 * a + 3 + j], recv_sems.at[6 * a + 3 + j],
                                        (x, y, 1 - c)))
                land = outs[a].at[q, theirs]
                from_sibling.append(_remote(land, land, send_sems.at[6 * a + 3 + j], recv_sems.at[6 * a + 3 + j],
                                            (x, y, 1 - c)))
            else:
                ici.append(_remote(ins[a], outs[a].at[p], send_sems.at[6 * a + j], recv_sems.at[6 * a + j],
                                   (*chip, c)))
                forwards.append(None)
    return ici, forwards, from_sibling


def _gather_arrival(shapes, split, ins, outs, send_sems, recv_sems):
    x, y, c = _coords()
    ici, forwards, from_sibling = _gather_copies(shapes, split, ins, outs, send_sems, recv_sems)
    k = 0
    for a, shape in enumerate(shapes):
        h = shape[0] // 2
        for j, chip in enumerate(_other_chips(x, y)):
            q = 2 * chip[0] + chip[1]
            land = outs[a].at[q, pl.ds(c * h, h)] if split[a] else outs[a].at[q]
            _remote(land, land, send_sems.at[6 * a + j], recv_sems.at[6 * a + j], (*chip, c)).wait_recv()
            if forwards[k] is not None:
                forwards[k].start()
            k += 1
    for cp in from_sibling:
        cp.wait_recv()
    for cp in ici + [f for f in forwards if f is not None]:
        cp.wait_send()


def _all_gather(arrs, split, remote):
    n = len(arrs)
    now = [a for a in range(n) if remote[a]]
    shapes = [arrs[a].shape for a in now]
    splits = [split[a] for a in now]

    def body(*refs):
        ins, outs, stage = refs[:n], refs[n:2 * n], refs[2 * n:3 * n]
        send_sems, recv_sems, in_sems, out_sems = refs[3 * n:]
        p = 2 * lax.axis_index("x") + lax.axis_index("y")
        gathered = ([ins[a] for a in now], [outs[a] for a in now], send_sems, recv_sems)
        loads = [pltpu.make_async_copy(ins[a], stage[a], in_sems.at[a]) for a in range(n)]
        for cp in loads:
            cp.start()
        for cp in _gather_copies(shapes, splits, *gathered)[0]:
            cp.start()
        stores = [pltpu.make_async_copy(stage[a], outs[a].at[p], out_sems.at[a]) for a in range(n)]
        for a in range(n):
            loads[a].wait()
            stores[a].start()
        _gather_arrival(shapes, splits, *gathered)
        for cp in stores:
            cp.wait()

    return pl.pallas_call(
        body, name="weights_all_gather", in_specs=[HBM_SPEC] * n, out_specs=[HBM_SPEC] * n,
        out_shape=[jax.ShapeDtypeStruct((N_CHIPS,) + a.shape, a.dtype) for a in arrs],
        scratch_shapes=[pltpu.VMEM(a.shape, a.dtype) for a in arrs]
        + [pltpu.SemaphoreType.DMA((6 * len(now),)), pltpu.SemaphoreType.DMA((6 * len(now),)),
           pltpu.SemaphoreType.DMA((n,)), pltpu.SemaphoreType.DMA((n,))],
        compiler_params=pltpu.CompilerParams(vmem_limit_bytes=VMEM_LIMIT),
    )(*arrs)


PAIR_SWAP_PIECES = 2


def _swap_copies(shapes, ins, other, send_sems, recv_sems):
    x, y, c = _coords()
    copies = []
    for a, shape in enumerate(shapes):
        h = shape[1] // 2
        piece = h // PAIR_SWAP_PIECES
        for q in range(N_CHIPS):
            for r in range(PAIR_SWAP_PIECES):
                k = (a * N_CHIPS + q) * PAIR_SWAP_PIECES + r
                copies.append(_remote(ins[a].at[q, pl.ds((1 - c) * h + r * piece, piece)],
                                      other[a].at[q, pl.ds(r * piece, piece)], send_sems.at[k], recv_sems.at[k],
                                      (x, y, 1 - c)))
    return copies


def _swap_out_shapes(gs):
    return [jax.ShapeDtypeStruct((N_CHIPS, g.shape[1] // 2, g.shape[2]), g.dtype) for g in gs]


def _pair_swap(gs, tag):
    n = len(gs)
    shapes = [g.shape for g in gs]

    def body(*refs):
        copies = _swap_copies(shapes, refs[:n], refs[n:2 * n], *refs[2 * n:])
        for cp in copies:
            cp.start()
        for cp in copies:
            cp.wait()

    nsem = n * N_CHIPS * PAIR_SWAP_PIECES
    return pl.pallas_call(
        body, name=f"grads_pair_swap_{tag}", in_specs=[HBM_SPEC] * n, out_specs=[HBM_SPEC] * n,
        out_shape=_swap_out_shapes(gs),
        scratch_shapes=[pltpu.SemaphoreType.DMA((nsem,)), pltpu.SemaphoreType.DMA((nsem,))],
    )(*gs)


def _scatter_copies(ins, outs, send_sems, recv_sems):
    x, y, c = _coords()
    copies = []
    for a in range(len(ins)):
        for j, chip in enumerate(_other_chips(x, y)):
            q = 2 * chip[0] + chip[1]
            copies.append(_remote(ins[a].at[q], outs[a].at[j], send_sems.at[3 * a + j], recv_sems.at[3 * a + j],
                                  (*chip, c)))
    return copies


def _scatter_shapes(ps):
    return [jax.ShapeDtypeStruct((N_CHIPS - 1,) + a.shape[1:], a.dtype) for a in ps]


def _pair_share(rs):
    n = len(rs)

    def body(*refs):
        ins, outs, stage = refs[:n], refs[n:2 * n], refs[2 * n:3 * n]
        send_sems, recv_sems, in_sems, out_sems = refs[3 * n:]
        x, y, c = _coords()

        def mine(a):
            h = rs[a].shape[0]
            return outs[a].at[pl.ds(c * h, h)]

        loads = [pltpu.make_async_copy(ins[a], stage[a], in_sems.at[a]) for a in range(n)]
        for cp in loads:
            cp.start()
        sends = [_remote(ins[a], mine(a), send_sems.at[a], recv_sems.at[a], (x, y, 1 - c)) for a in range(n)]
        for cp in sends:
            cp.start()
        stores = [pltpu.make_async_copy(stage[a], mine(a), out_sems.at[a]) for a in range(n)]
        for a in range(n):
            loads[a].wait()
            stores[a].start()
        for a in range(n):
            h = rs[a].shape[0]
            land = outs[a].at[pl.ds((1 - c) * h, h)]
            _remote(land, land, send_sems.at[a], recv_sems.at[a], (x, y, 1 - c)).wait_recv()
        for cp in sends:
            cp.wait_send()
        for cp in stores:
            cp.wait()

    return pl.pallas_call(
        body, name="grads_pair_share", in_specs=[HBM_SPEC] * n, out_specs=[HBM_SPEC] * n,
        out_shape=[jax.ShapeDtypeStruct((2 * a.shape[0], a.shape[1]), a.dtype) for a in rs],
        scratch_shapes=[pltpu.VMEM(a.shape, a.dtype) for a in rs] + [pltpu.SemaphoreType.DMA((n,))] * 4,
        compiler_params=pltpu.CompilerParams(vmem_limit_bytes=VMEM_LIMIT),
    )(*rs)


def _small_all_reduce(buf):
    R = buf.shape[0]
    ndev = 2 * N_CHIPS

    def body(in_ref, out_ref, gath, send_sems, recv_sems):
        x, y, c = _coords()
        me = 4 * x + 2 * y + c
        gath[me] = in_ref[...]
        peers = []
        for d in range(1, ndev):
            px = 1 - x if d & 4 else x
            py = 1 - y if d & 2 else y
            pc = 1 - c if d & 1 else c
            peers.append((px, py, pc))
        sends = []
        for d, peer in enumerate(peers):
            cp = _remote(in_ref, gath.at[me], send_sems.at[d], recv_sems.at[d], peer)
            cp.start()
            sends.append(cp)
        for d, peer in enumerate(peers):
            land = gath.at[4 * peer[0] + 2 * peer[1] + peer[2]]
            _remote(land, land, send_sems.at[d], recv_sems.at[d], peer).wait_recv()
        for cp in sends:
            cp.wait_send()
        acc = gath[0]
        for s in range(1, ndev):
            acc = acc + gath[s]
        out_ref[...] = acc

    return pl.pallas_call(
        body, name="small_all_reduce", in_specs=[VMEM_SPEC], out_specs=VMEM_SPEC,
        out_shape=jax.ShapeDtypeStruct(buf.shape, F32),
        scratch_shapes=[pltpu.VMEM((ndev, R, LANE), F32), pltpu.SemaphoreType.DMA((ndev - 1,)),
                        pltpu.SemaphoreType.DMA((ndev - 1,))],
    )(buf)


def _pair_add(name, own, other):
    h = own.shape[1]
    tm = _tile(h, (128, 64, 32, 16))

    def fn(i, a, b):
        return (a + b,)

    return _rowcall(name, fn, h, tm, [(own, "row", None), (other, "row", None)], [(own.shape, BF16, "row")])[0]


def _chip_add(name, own, parts):
    h = parts.shape[1]
    tm = _tile(h, (128, 64, 32, 16))

    def fn(i, o, a):
        a = a.astype(F32)
        return (((o.astype(F32) + a[0]) + a[1]) + a[2],)

    return _rowcall(name, fn, h, tm, [(own, "row", None), (parts, "row", None)], [(parts.shape[1:], F32, "row")])[0]


def _adamw(name, w, g, m, v):
    R = w.shape[0]
    tm = _tile(R, (256, 128, 64, 32, 16, 8))

    def fn(i, wv, gv, mv, vv):
        m2 = ADAM_B1 * mv + (1.0 - ADAM_B1) * gv
        v2 = ADAM_B2 * vv + (1.0 - ADAM_B2) * (gv * gv)
        m_hat = m2 / (1.0 - ADAM_B1 ** ADAM_STEP)
        v_hat = v2 / (1.0 - ADAM_B2 ** ADAM_STEP)
        delta = -ADAM_LR * (m_hat / (jnp.sqrt(v_hat) + ADAM_EPS) + ADAM_WD * wv)
        return delta, m2, v2

    ins = [(a, "row", None) for a in (w, g, m, v)]
    return _rowcall(name, fn, R, tm, ins, [(w.shape, F32, "row")] * 3)


def _pack(arrs):
    flat = jnp.concatenate([a.reshape(-1).astype(F32) for a in arrs])
    size = flat.shape[0]
    padded = -(-size // (SUBLANE * LANE)) * SUBLANE * LANE
    return jnp.pad(flat, (0, padded - size)).reshape(-1, LANE)


def _unpack(buf, shapes):
    flat = buf.reshape(-1)
    out, off = [], 0
    for s in shapes:
        size = math.prod(s)
        out.append(flat[off:off + size].reshape(s))
        off += size
    return out


BIG = ("a_w_in", "a_w_out", "w_kv", "b_w_q", "b_w_o", "ffn_w_up", "ffn_w_down")
WEIGHTS = ("a_norm_w", "a_w_in", "a_conv_w", "a_a_log", "a_dt_bias", "a_out_norm_w", "a_w_out", "kv_norm_w", "w_kv",
           "b_norm_w", "b_w_q", "b_sinks", "b_w_o", "rel_bias_table", "ffn_norm_w", "ffn_w_up", "ffn_conv_w",
           "ffn_conv_b", "ffn_w_down", "final_norm_w")
SMALL = tuple(n for n in WEIGHTS if n not in BIG)
SMALL_SHARDED = {"a_norm_w": 1, "a_conv_w": 2, "ffn_conv_w": 2}


def _quarter_2d(name, a):
    if name in ("ffn_w_up", "ffn_w_down"):
        return a.reshape(a.shape[0] * a.shape[1], a.shape[2])
    return a.reshape(a.shape[-2], a.shape[-1])


def _whole_weights(w):
    bigs = [_quarter_2d(n, w[n]).astype(BF16) for n in BIG]
    smalls = [w["a_norm_w"], w["a_conv_w"][0], w["ffn_conv_w"].reshape(6, DFF2_SHARD)]
    remote = [True] + [False] * (len(bigs) - 1) + [True] * len(smalls)
    g = _all_gather(bigs + smalls, [True] * len(bigs) + [False] * len(smalls), remote)
    w_in = g[0].transpose(1, 0, 2).reshape(D, GDN_IN)
    nh = GDN_V_HEADS
    zpad = jnp.zeros((D, LANE - nh), BF16)
    w_in_ba = jnp.concatenate([w_in[:, GDN_MAIN:GDN_MAIN + nh], zpad, w_in[:, GDN_MAIN + nh:], zpad], axis=1)
    lane_pad = lambda a: jnp.pad(a, ((0, 0), (0, LANE - nh)))
    early = dict(
        a_norm_w=g[7].reshape(1, D), w_in_main=w_in[:, :GDN_MAIN], w_in_ba=w_in_ba,
        a_conv_w=g[8].transpose(1, 0, 2).reshape(4, GDN_CONV), a_log=lane_pad(w["a_a_log"]),
        dt_bias=lane_pad(w["a_dt_bias"]), a_out_norm_w=w["a_out_norm_w"],
        kv_norm_w=w["kv_norm_w"].reshape(1, D), b_norm_w=w["b_norm_w"],
        sinks=jnp.broadcast_to(w["b_sinks"].reshape(SWA_KV_HEADS, SWA_GROUP, 1), (SWA_KV_HEADS, SWA_GROUP, LANE)),
        rel_table_t=w["rel_bias_table"].T, ffn_norm_w=w["ffn_norm_w"],
        ffn_conv_w=g[9].reshape(N_CHIPS, 2, 3, DFF2_SHARD).transpose(1, 2, 0, 3).reshape(2, 3, DFF2),
        ffn_conv_b=w["ffn_conv_b"], final_norm_w=w["final_norm_w"].reshape(1, D),
    )
    return early, (bigs[1:], g[1:len(bigs)])


def _late_weights(g):
    return dict(
        w_out=g[0].reshape(GDN_V, D), w_kv=g[1].reshape(D, 2 * SWA_KV_HEADS * SWA_HD), w_q=g[2].reshape(D, D),
        w_o=g[3].reshape(D, D), w_up=g[4].reshape(N_CHIPS, 2, D, DFF2_SHARD),
        w_down=g[5].reshape(N_CHIPS, 2, DFF_SHARD, D).transpose(1, 0, 2, 3).reshape(2, DFF, D),
    )


def _chip_major(name, g):
    if name == "a_w_in":
        return g.reshape(D, N_CHIPS, GDN_IN_SHARD).transpose(1, 0, 2)
    if name == "ffn_w_up":
        return g.reshape(N_CHIPS, 2 * D, DFF2_SHARD)
    if name == "ffn_w_down":
        return g.reshape(N_CHIPS, 2 * DFF_SHARD, D)
    return g.reshape(N_CHIPS, g.shape[0] // N_CHIPS, g.shape[1])


def kernel(x, a_norm_w, a_w_in, a_conv_w, a_a_log, a_dt_bias, a_out_norm_w, a_w_out, kv_norm_w, w_kv, b_norm_w, b_w_q, b_sinks, b_w_o, rel_bias_table, ffn_norm_w, ffn_w_up, ffn_conv_w, ffn_conv_b, ffn_w_down, final_norm_w, loss_target, m_a_norm_w, m_a_w_in, m_a_conv_w, m_a_a_log, m_a_dt_bias, m_a_out_norm_w, m_a_w_out, m_kv_norm_w, m_w_kv, m_b_norm_w, m_b_w_q, m_b_sinks, m_b_w_o, m_rel_bias_table, m_ffn_norm_w, m_ffn_w_up, m_ffn_conv_w, m_ffn_conv_b, m_ffn_w_down, m_final_norm_w, v_a_norm_w, v_a_w_in, v_a_conv_w, v_a_a_log, v_a_dt_bias, v_a_out_norm_w, v_a_w_out, v_kv_norm_w, v_w_kv, v_b_norm_w, v_b_w_q, v_b_sinks, v_b_w_o, v_rel_bias_table, v_ffn_norm_w, v_ffn_w_up, v_ffn_conv_w, v_ffn_conv_b, v_ffn_w_down, v_final_norm_w):
    w = dict(zip(WEIGHTS, (a_norm_w, a_w_in, a_conv_w, a_a_log, a_dt_bias, a_out_norm_w, a_w_out, kv_norm_w, w_kv,
                           b_norm_w, b_w_q, b_sinks, b_w_o, rel_bias_table, ffn_norm_w, ffn_w_up, ffn_conv_w,
                           ffn_conv_b, ffn_w_down, final_norm_w)))
    m = dict(zip(WEIGHTS, (m_a_norm_w, m_a_w_in, m_a_conv_w, m_a_a_log, m_a_dt_bias, m_a_out_norm_w, m_a_w_out,
                           m_kv_norm_w, m_w_kv, m_b_norm_w, m_b_w_q, m_b_sinks, m_b_w_o, m_rel_bias_table,
                           m_ffn_norm_w, m_ffn_w_up, m_ffn_conv_w, m_ffn_conv_b, m_ffn_w_down, m_final_norm_w)))
    v = dict(zip(WEIGHTS, (v_a_norm_w, v_a_w_in, v_a_conv_w, v_a_a_log, v_a_dt_bias, v_a_out_norm_w, v_a_w_out,
                           v_kv_norm_w, v_w_kv, v_b_norm_w, v_b_w_q, v_b_sinks, v_b_w_o, v_rel_bias_table,
                           v_ffn_norm_w, v_ffn_w_up, v_ffn_conv_w, v_ffn_conv_b, v_ffn_w_down, v_final_norm_w)))
    T = x.shape[1]
    chip = 2 * lax.axis_index("x") + lax.axis_index("y")

    core = lax.axis_index("c")

    def pair_sums(names, whole, other):
        own = [lax.dynamic_slice_in_dim(g, core * (g.shape[1] // 2), g.shape[1] // 2, 1) for g in whole]
        return [_pair_add(f"pair_add_{n}", a, b) for n, a, b in zip(names, own, other)]

    early, late = _whole_weights(w)
    loss_part, dx, grads, scattered = _local_step(x.reshape(T, D), loss_target.reshape(T, D), early, late, pair_sums)

    assert all(n in scattered for n in BIG)
    halves = [_chip_add(f"chip_add_{n}", lax.dynamic_index_in_dim(scattered[n][0], chip, 0, keepdims=False),
                        scattered[n][1]) for n in BIG]
    quarter = _pair_share(halves)
    out_g, out_d, out_m, out_v = {}, {}, {}, {}
    for n, g2 in zip(BIG, quarter):
        res = _adamw(f"adamw_{n}", _quarter_2d(n, w[n]), g2, _quarter_2d(n, m[n]), _quarter_2d(n, v[n]))
        out_g[n] = g2.reshape(w[n].shape)
        out_d[n], out_m[n], out_v[n] = (r.reshape(w[n].shape) for r in res)

    whole = [grads[n] for n in SMALL]
    summed = _unpack(_small_all_reduce(_pack([loss_part[0:1, 0:1]] + whole)), [(1, 1)] + [a.shape for a in whole])
    loss = summed[0].reshape(())
    small_g = []
    for n, g in zip(SMALL, summed[1:]):
        if n in SMALL_SHARDED:
            axis = SMALL_SHARDED[n]
            g = g.reshape(w[n].shape[:axis] + (-1,) + w[n].shape[axis + 1:])
            size = w[n].shape[axis]
            g = lax.dynamic_slice_in_dim(g, chip * size, size, axis)
        small_g.append(g.reshape(w[n].shape))
    shapes = [w[n].shape for n in SMALL]
    res = _adamw("adamw_small", _pack([w[n] for n in SMALL]), _pack(small_g), _pack([m[n] for n in SMALL]),
                 _pack([v[n] for n in SMALL]))
    small_d, small_m, small_v = (_unpack(r, shapes) for r in res)
    for i, n in enumerate(SMALL):
        out_g[n], out_d[n], out_m[n], out_v[n] = small_g[i], small_d[i], small_m[i], small_v[i]

    return (loss, dx.reshape(x.shape), *[out_g[n] for n in WEIGHTS], *[out_d[n] for n in WEIGHTS],
            *[out_m[n] for n in WEIGHTS], *[out_v[n] for n in WEIGHTS])
```

```python
import functools
import math

import jax
import jax.numpy as jnp
from jax import lax
from jax.experimental import pallas as pl
from jax.experimental.pallas import tpu as pltpu

F32 = jnp.float32
BF16 = jnp.bfloat16
MESH = pl.DeviceIdType.MESH
HIGHEST = lax.Precision.HIGHEST

D = 1024
EPS = 1e-6
NEG_INF = -1e30
N_CHIPS = 4

GDN_QK_HEADS = 8
GDN_V_HEADS = 16
GDN_HD = 128
GDN_QK = GDN_QK_HEADS * GDN_HD
GDN_V = GDN_V_HEADS * GDN_HD
GDN_CONV = 2 * GDN_QK + GDN_V
GDN_MAIN = GDN_CONV + GDN_V
GDN_IN = GDN_MAIN + 2 * GDN_V_HEADS
GDN_IN_SHARD = GDN_IN // N_CHIPS
GDN_CHUNK = 64

SWA_Q_HEADS = 16
SWA_KV_HEADS = 4
SWA_GROUP = 4
SWA_HD = 64
SWA_BLOCK = 128
REL_BUCKETS = 32
REL_MAX_DISTANCE = 128

DFF = 2816
DFF2 = 2 * DFF
DFF2_SHARD = DFF2 // N_CHIPS
DFF_SHARD = DFF // N_CHIPS

ADAM_LR = 0.001
ADAM_B1 = 0.9
ADAM_B2 = 0.999
ADAM_EPS = 1e-08
ADAM_WD = 0.01
ADAM_STEP = 10

LANE = 128
SUBLANE = 8
VMEM_LIMIT = 56 * 1024 * 1024


def _params(sem, vmem=VMEM_LIMIT):
    return pltpu.CompilerParams(dimension_semantics=sem, vmem_limit_bytes=vmem)


def _rowcall(name, fn, T, tm, ins, outs, swap=()):
    n = T // tm
    nswap = len(swap)
    swap_shapes = [g.shape for g in swap]
    r8 = tm // SUBLANE
    last8 = T // SUBLANE - 1
    arrays, in_specs = [], []
    for arr, kind, cols in ins:
        arrays.append(arr)
        if kind == "full":
            in_specs.append(pl.BlockSpec(arr.shape, functools.partial(lambda nd, i: (0,) * nd, arr.ndim)))
        elif arr.ndim == 2:
            w, ci = cols if cols is not None else (arr.shape[1], 0)
            if kind == "row":
                in_specs.append(pl.BlockSpec((tm, w), functools.partial(lambda ci, i: (i, ci), ci)))
            elif kind == "prev":
                in_specs.append(pl.BlockSpec(
                    (SUBLANE, w), functools.partial(lambda ci, i: (jnp.maximum(i * r8 - 1, 0), ci), ci)))
            else:
                in_specs.append(pl.BlockSpec(
                    (SUBLANE, w), functools.partial(lambda ci, i: (jnp.minimum((i + 1) * r8, last8), ci), ci)))
        else:
            lead = arr.shape[:-2]
            in_specs.append(pl.BlockSpec(lead + (tm, arr.shape[-1]),
                                         functools.partial(lambda nl, i: (0,) * nl + (i, 0), len(lead))))
    out_shape, out_specs = [], []
    for shape, dtype, kind in outs:
        out_shape.append(jax.ShapeDtypeStruct(shape, dtype))
        if kind == "acc":
            out_specs.append(pl.BlockSpec(shape, functools.partial(lambda nd, i: (0,) * nd, len(shape))))
        else:
            lead = shape[:-2]
            out_specs.append(pl.BlockSpec(lead + (tm, shape[-1]),
                                          functools.partial(lambda nl, i: (0,) * nl + (i, 0), len(lead))))
    nin = len(arrays)

    nout = len(outs)

    def body(*refs):
        i = pl.program_id(0)
        if nswap:
            comm = (swap_shapes, refs[nin:nin + nswap], refs[nin + nswap + nout:nin + 2 * nswap + nout], refs[-2],
                    refs[-1])

            @pl.when(i == 0)
            def _():
                for cp in _swap_copies(*comm):
                    cp.start()

            @pl.when(i == n - 1)
            def _():
                for cp in _swap_copies(*comm):
                    cp.wait()

        vals = [r[...] for r in refs[:nin]]
        res = fn(i, *vals)
        for (shape, dtype, kind), o, r in zip(outs, refs[nin + nswap:], res):
            if kind == "row":
                o[...] = r.astype(dtype)
            else:
                @pl.when(i == 0)
                def _():
                    o[...] = r.astype(dtype)

                @pl.when(i > 0)
                def _():
                    o[...] += r.astype(dtype)

    anywhere = pl.BlockSpec(memory_space=pl.ANY)
    nsem = nswap * N_CHIPS * PAIR_SWAP_PIECES
    return pl.pallas_call(
        body, name=name, grid=(n,), in_specs=in_specs + [anywhere] * nswap, out_specs=out_specs + [anywhere] * nswap,
        out_shape=out_shape + _swap_out_shapes(swap),
        scratch_shapes=[pltpu.SemaphoreType.DMA((nsem,)), pltpu.SemaphoreType.DMA((nsem,))] if nswap else [],
        compiler_params=_params(("arbitrary",)),
    )(*arrays, *swap)


def _mm(name, a, b, out_shape, out_dtype, grid, a_spec, b_spec, o_spec, dims, acc_shape, res=None, precision=None,
        into=None, scatter=(), post=None):
    nk = grid[2]
    ns = len(scatter)
    a, norm_w = a if isinstance(a, tuple) else (a, None)
    normed = norm_w is not None
    posted = post is not None
    n_in = 2 + normed + (res is not None) + (into is not None) + 3 * posted + ns
    n_out = 1 + posted + ns

    def body(*refs):
        a_ref, b_ref, o_ref = refs[0], refs[1], refs[n_in]
        r_ref = refs[2 + normed] if res is not None else None
        first = pl.program_id(0) == 0
        if ns:
            comm = (refs[n_in - ns:n_in], refs[n_in + n_out - ns:n_in + n_out], refs[-2], refs[-1])
            steps = [pl.program_id(d) for d in range(3)]

            @pl.when((steps[0] == 0) & (steps[1] == 0) & (steps[2] == 0))
            def _():
                for cp in _scatter_copies(*comm):
                    cp.start()

            @pl.when((steps[0] == grid[0] - 1) & (steps[1] == grid[1] - 1) & (steps[2] == grid[2] - 1))
            def _():
                copies = _scatter_copies(*comm)
                for cp in copies:
                    cp.wait_recv()
                for cp in copies:
                    cp.wait_send()

        av, bv = a_ref[...], b_ref[...]
        if normed:
            av = _rms_core(av, refs[2][...])
        if precision is None:
            av, bv = av.astype(BF16), bv.astype(BF16)
        p = lax.dot_general(av, bv, (dims, ((), ())), preferred_element_type=F32, precision=precision)

        def finish(x):
            if res is not None:
                x = x + r_ref[...].astype(F32)
            if posted:
                h_ref, w_ref, add_ref = refs[n_in - ns - 3:n_in - ns]
                dh, dw = jax.vjp(_rms_core, h_ref[...], w_ref[...])[1](x)
                x = dh + add_ref[...]
                dw_ref = refs[n_in + 1]

                @pl.when(first)
                def _():
                    dw_ref[...] = dw

                @pl.when(jnp.logical_not(first))
                def _():
                    dw_ref[...] += dw

            o_ref[...] = x.astype(out_dtype).reshape(o_ref.shape)

        if nk == 1:
            finish(p)
        else:
            acc = refs[n_in + n_out]
            k = pl.program_id(2)

            @pl.when(k == 0)
            def _():
                acc[...] = p

            @pl.when(k > 0)
            def _():
                acc[...] += p

            @pl.when(k == nk - 1)
            def _():
                finish(acc[...])

    anywhere = pl.BlockSpec(memory_space=pl.ANY)
    ops = [a, b] + ([norm_w] if normed else []) + ([res] if res is not None else [])
    ops += ([into] if into is not None else []) + (list(post) if posted else []) + list(scatter)
    whole = lambda arr: pl.BlockSpec(arr.shape, lambda i, j, k: (0, 0))
    specs = [a_spec, b_spec] + ([whole(norm_w)] if normed else [])
    specs += [o_spec] if res is not None else []
    specs += [anywhere] if into is not None else []
    specs += ([o_spec, whole(post[1]), o_spec] if posted else []) + [anywhere] * ns
    out = pl.pallas_call(
        body, name=name, grid=grid, in_specs=specs,
        out_specs=[o_spec] + ([whole(post[1])] if posted else []) + [anywhere] * ns,
        out_shape=[jax.ShapeDtypeStruct(out_shape, out_dtype)]
        + ([jax.ShapeDtypeStruct(post[1].shape, F32)] if posted else []) + _scatter_shapes(scatter),
        input_output_aliases={2 + normed + (res is not None): 0} if into is not None else {},
        scratch_shapes=([pltpu.VMEM(acc_shape, F32)] if nk > 1 else [])
        + ([pltpu.SemaphoreType.DMA((3 * ns,)), pltpu.SemaphoreType.DMA((3 * ns,))] if ns else []),
        compiler_params=_params(("arbitrary",) * 3 if ns or posted else ("parallel", "parallel", "arbitrary")),
    )(*ops)
    return out if n_out > 1 else out[0]


NN = ((1,), (0,))
NT = ((1,), (1,))
TN = ((0,), (0,))


BIG_TILES = (1024, 512, 256, 128)


def _tile(n, pref):
    for t in pref:
        if n % t == 0:
            return t
    return n


def _rows_of(a):
    return a[0] if isinstance(a, tuple) else a


def _mm_nn(name, a, w, out_dtype, res=None, precision=None):
    M, K = _rows_of(a).shape
    N = w.shape[1]
    tm = _tile(M, BIG_TILES if K <= 2048 else BIG_TILES[1:])
    tn = _tile(N, BIG_TILES)
    return _mm(name, a, w, (M, N), out_dtype, (M // tm, N // tn, 1),
               pl.BlockSpec((tm, K), lambda i, j, k: (i, 0)), pl.BlockSpec((K, tn), lambda i, j, k: (0, j)),
               pl.BlockSpec((tm, tn), lambda i, j, k: (i, j)), NN, (tm, tn), res=res, precision=precision)


def _mm_nt(name, g, w, out_dtype, res=None, precision=None, scatter=(), post=None):
    M, N = g.shape
    K = w.shape[0]
    tm, tk = _tile(M, BIG_TILES if post is None else BIG_TILES[1:]), _tile(K, (1024, 1408, 512, 256, 128))
    tn = _tile(N, (1536,) + BIG_TILES)
    return _mm(name, g, w, (M, K), out_dtype, (M // tm, K // tk, N // tn),
               pl.BlockSpec((tm, tn), lambda i, j, k: (i, k)), pl.BlockSpec((tk, tn), lambda i, j, k: (j, k)),
               pl.BlockSpec((tm, tk), lambda i, j, k: (i, j)), NT, (tm, tk), res=res, precision=precision,
               scatter=scatter, post=post)


def _mm_tn(name, a, g, out_dtype=F32, precision=None):
    T, K = _rows_of(a).shape
    N = g.shape[1]
    tk, tn = _tile(K, (1024, 1408, 512, 256, 128)), _tile(N, BIG_TILES)
    assert tk == K or not isinstance(a, tuple)
    tt = _tile(T, BIG_TILES)
    return _mm(name, a, g, (K, N), out_dtype, (K // tk, N // tn, T // tt),
               pl.BlockSpec((tt, tk), lambda i, j, k: (k, i)), pl.BlockSpec((tt, tn), lambda i, j, k: (k, j)),
               pl.BlockSpec((tk, tn), lambda i, j, k: (i, j)), TN, (tk, tn), precision=precision)


def _mm_up(name, n, wup, layer):
    T = _rows_of(n).shape[0]
    tm = _tile(T, BIG_TILES)
    return _mm(name, n, wup, (T, DFF2), BF16, (T // tm, N_CHIPS, 1),
               pl.BlockSpec((tm, D), lambda i, j, k: (i, 0)),
               pl.BlockSpec((None, None, D, DFF2_SHARD), lambda i, j, k: (j, layer, 0, 0)),
               pl.BlockSpec((tm, DFF2_SHARD), lambda i, j, k: (i, j)), NN, (tm, DFF2_SHARD))


def _mm_up_nt(name, du, wup, layer, post):
    T = du.shape[0]
    tm, tk = _tile(T, BIG_TILES[1:]), D
    return _mm(name, du, wup, (T, D), F32, (T // tm, D // tk, N_CHIPS),
               pl.BlockSpec((tm, DFF2_SHARD), lambda i, j, k: (i, k)),
               pl.BlockSpec((None, None, tk, DFF2_SHARD), lambda i, j, k: (k, layer, j, 0)),
               pl.BlockSpec((tm, tk), lambda i, j, k: (i, j)), NT, (tm, tk), post=post)


def _mm_up_tn(name, n, du, layer, into):
    T = _rows_of(n).shape[0]
    tk, tt = D, _tile(T, BIG_TILES)
    return _mm(name, n, du, (N_CHIPS, 2, D, DFF2_SHARD), F32, (D // tk, N_CHIPS, T // tt),
               pl.BlockSpec((tt, tk), lambda i, j, k: (k, i)), pl.BlockSpec((tt, DFF2_SHARD), lambda i, j, k: (k, j)),
               pl.BlockSpec((None, None, tk, DFF2_SHARD), lambda i, j, k: (j, layer, i, 0)), TN, (tk, DFF2_SHARD),
               into=into)


def _mm_down_tn(name, act, dout, layer, into):
    T = act.shape[0]
    tk, tn, tt = 2 * DFF_SHARD, _tile(D, BIG_TILES), _tile(T, BIG_TILES)
    return _mm(name, act, dout, (2, 2, 2, DFF_SHARD, D), F32, (DFF // tk, D // tn, T // tt),
               pl.BlockSpec((tt, tk), lambda i, j, k: (k, i)), pl.BlockSpec((tt, tn), lambda i, j, k: (k, j)),
               pl.BlockSpec((None, 2, None, DFF_SHARD, tn), lambda i, j, k: (i, 0, layer, 0, j)), TN, (tk, tn),
               into=into)


def _sigmoid(x):
    return 0.5 * jnp.tanh(0.5 * x) + 0.5


def _silu(x):
    return x * _sigmoid(x)


def _softplus(x):
    return jnp.maximum(x, 0.0) + jnp.log(1.0 + jnp.exp(-jnp.abs(x)))


def _rms_core(h, w):
    return h * lax.rsqrt(jnp.mean(h * h, axis=-1, keepdims=True) + EPS) * w


def _shift_down(x, halo, s, i):
    if s == 0:
        return x
    tm = x.shape[0]
    rolled = pltpu.roll(x, s, 0)
    patch = pltpu.roll(jnp.where(i == 0, 0.0, halo), s, 0)
    row = lax.broadcasted_iota(jnp.int32, patch.shape, 0)
    top = jnp.where(row < s, patch, rolled[:SUBLANE])
    return jnp.concatenate([top, rolled[SUBLANE:]], axis=0) if tm > SUBLANE else top


def _shift_up(x, halo, s, i, n):
    if s == 0:
        return x
    tm = x.shape[0]
    rolled = pltpu.roll(x, tm - s, 0)
    patch = pltpu.roll(jnp.where(i == n - 1, 0.0, halo), SUBLANE - s, 0)
    row = lax.broadcasted_iota(jnp.int32, patch.shape, 0)
    bottom = jnp.where(row >= SUBLANE - s, patch, rolled[tm - SUBLANE:])
    return jnp.concatenate([rolled[:tm - SUBLANE], bottom], axis=0) if tm > SUBLANE else bottom


def _taps(x, halo, K, i):
    return [_shift_down(x, halo, K - 1 - j, i) for j in range(K)]


def _conv_fwd(taps, w):
    y = w[0:1, :] * taps[0]
    for j in range(1, len(taps)):
        y = y + w[j:j + 1, :] * taps[j]
    return y


def _conv_dx(dy, halo_next, w, i, n):
    K = w.shape[0]
    dx = w[K - 1:K, :] * dy
    for j in range(K - 1):
        dx = dx + w[j:j + 1, :] * _shift_up(dy, halo_next, K - 1 - j, i, n)
    return dx


def _conv_dw(dy, taps):
    rows = [jnp.sum(dy * tap, axis=0, keepdims=True) for tap in taps]
    return jnp.concatenate(rows + [jnp.zeros((SUBLANE - len(taps), dy.shape[1]), F32)], axis=0)


def _l2(x):
    return x * lax.rsqrt(jnp.sum(x * x, axis=-1, keepdims=True) + EPS)


def _gdn_post_core(yq, yk, yv, pb, pa, a_log, dtb):
    qn = tuple(_l2(_silu(a)) * (GDN_HD ** -0.5) for a in yq)
    kn = tuple(_l2(_silu(a)) for a in yk)
    v = _silu(yv)
    beta = _sigmoid(pb)
    g = -jnp.exp(a_log) * _softplus(pa + dtb)
    return qn, kn, v, beta, g


def _heads(x, n):
    return tuple(x[:, GDN_HD * h:GDN_HD * (h + 1)] for h in range(n))


def _gdn_pre_fwd(pm, pba, conv_w, a_log, dtb, tm=128):
    T = pm.shape[0]
    tm = min(tm, T)

    def fn(i, x, halo, pbav, cw, al, db):
        y = _conv_fwd(_taps(x.astype(F32), halo.astype(F32), 4, i), cw)
        qn, kn, v, beta, g = _gdn_post_core(_heads(y[:, :GDN_QK], 8), _heads(y[:, GDN_QK:2 * GDN_QK], 8),
                                            y[:, 2 * GDN_QK:], pbav[:, :LANE], pbav[:, LANE:], al, db)
        return jnp.stack(qn), jnp.stack(kn), jnp.stack(_heads(v, GDN_V_HEADS)), beta, g

    ins = [(pm, "row", (GDN_CONV, 0)), (pm, "prev", (GDN_CONV, 0)), (pba, "row", None),
           (conv_w, "full", None), (a_log, "full", None), (dtb, "full", None)]
    outs = [((GDN_QK_HEADS, T, GDN_HD), BF16, "row"), ((GDN_QK_HEADS, T, GDN_HD), BF16, "row"),
            ((GDN_V_HEADS, T, GDN_HD), BF16, "row"), ((T, LANE), F32, "row"), ((T, LANE), F32, "row")]
    return _rowcall("gdn_pre_fwd", fn, T, tm, ins, outs)


def _gdn_pre_bwd(pm, pba, conv_w, a_log, dtb, dqn, dkn, dv, dbeta, dg, tm=128):
    T = pm.shape[0]
    tm = min(tm, T)

    def fn(i, x, halo, pbav, cw, al, db, dqv, dkv, dvv, dbv, dgv):
        taps = _taps(x.astype(F32), halo.astype(F32), 4, i)
        y = _conv_fwd(taps, cw)
        prim = (_heads(y[:, :GDN_QK], 8), _heads(y[:, GDN_QK:2 * GDN_QK], 8), y[:, 2 * GDN_QK:],
                pbav[:, :LANE], pbav[:, LANE:], al, db)
        _, vjp = jax.vjp(_gdn_post_core, *prim)
        cot = (tuple(dqv[h].astype(F32) for h in range(8)), tuple(dkv[h].astype(F32) for h in range(8)),
               jnp.concatenate([dvv[h].astype(F32) for h in range(GDN_V_HEADS)], axis=1), dbv, dgv)
        dyq, dyk, dyv, dpb, dpa, dal, ddb = vjp(cot)
        dy = jnp.concatenate(list(dyq) + list(dyk) + [dyv], axis=1)
        dcw = _conv_dw(dy, taps)
        return dy, jnp.concatenate([dpb, dpa], axis=1), dcw, dal, ddb

    ins = [(pm, "row", (GDN_CONV, 0)), (pm, "prev", (GDN_CONV, 0)), (pba, "row", None),
           (conv_w, "full", None), (a_log, "full", None), (dtb, "full", None),
           (dqn, "row", None), (dkn, "row", None), (dv, "row", None), (dbeta, "row", None), (dg, "row", None)]
    outs = [((T, GDN_CONV), BF16, "row"), ((T, 2 * LANE), F32, "row"), ((SUBLANE, GDN_CONV), F32, "acc"),
            ((1, LANE), F32, "acc"), ((1, LANE), F32, "acc")]
    return _rowcall("gdn_pre_bwd", fn, T, tm, ins, outs)


def _gdn_conv_bwd(dy, dz, conv_w, tm=256):
    T = dy.shape[0]
    tm = min(tm, T)
    n = T // tm

    def fn(i, dyv, halo, dzv, cw):
        dx = _conv_dx(dyv.astype(F32), halo.astype(F32), cw, i, n)
        return (jnp.concatenate([dx.astype(BF16), dzv.astype(BF16)], axis=1),)

    ins = [(dy, "row", None), (dy, "next", None), (dz, "row", None), (conv_w, "full", None)]
    return _rowcall("gdn_conv_bwd", fn, T, tm, ins, [((T, GDN_MAIN), BF16, "row")])[0]


def _bdot(a, b, dims=NN):
    return lax.dot_general(a.astype(BF16), b.astype(BF16), (dims, ((), ())), preferred_element_type=F32)


BNN = ((2,), (1,))
BNT = ((2,), (2,))
BTN = ((1,), (1,))


def _bmm(a, b, dims=BNN):
    return lax.dot_general(a.astype(BF16), b.astype(BF16), (dims, ((0,), (0,))), preferred_element_type=F32)


def _bmm3(a, b):
    ah, bh = a.astype(BF16), b.astype(BF16)
    al, bl = (a - ah.astype(F32)).astype(BF16), (b - bh.astype(F32)).astype(BF16)
    dn = (BNN, ((0,), (0,)))
    return (lax.dot_general(ah, bh, dn, preferred_element_type=F32)
            + lax.dot_general(al, bh, dn, preferred_element_type=F32)
            + lax.dot_general(ah, bl, dn, preferred_element_type=F32))


def _tri_inv(m):
    C = m.shape[-1]
    r = lax.broadcasted_iota(jnp.int32, (C, C), 0)
    c = lax.broadcasted_iota(jnp.int32, (C, C), 1)
    t = jnp.where(r == c, 1.0, 0.0) - m
    pw = _bmm3(m, m)
    t = t + _bmm3(t, pw)
    for _ in range(int(math.log2(C)) - 2):
        pw = _bmm(pw, pw)
        t = t + _bmm(t, pw)
    return t


def _tri_inv_vjp(t, dt):
    tt = jnp.swapaxes(t, 1, 2)
    return -_bmm(_bmm(tt, dt), tt)


def _twice(a):
    return jnp.broadcast_to(a[:, None], (a.shape[0], 2) + a.shape[1:]).reshape((2 * a.shape[0],) + a.shape[1:])


def _gdn_gates(grow, brow):
    C = grow.shape[2]
    r = lax.broadcasted_iota(jnp.int32, (C, C), 0)
    c = lax.broadcasted_iota(jnp.int32, (C, C), 1)
    tril, eye = r >= c, r == c
    gcol = jnp.sum(jnp.where(eye, grow, 0.0), axis=2, keepdims=True)
    bcol = jnp.sum(jnp.where(eye, brow, 0.0), axis=2, keepdims=True)
    gc_col = jnp.sum(jnp.where(tril, grow, 0.0), axis=2, keepdims=True)
    gc_row = jnp.sum(jnp.where(r <= c, gcol, 0.0), axis=1, keepdims=True)
    gc_last = jnp.sum(grow, axis=2, keepdims=True)
    decay = jnp.where(tril, jnp.exp(jnp.where(tril, gc_col - gc_row, 0.0)), 0.0)
    return bcol, gc_col, gc_last, decay


def _gdn_m(k, bcol, decay):
    C = k.shape[1]
    strict = lax.broadcasted_iota(jnp.int32, (C, C), 0) > lax.broadcasted_iota(jnp.int32, (C, C), 1)
    return jnp.where(strict, bcol * _twice(_bmm(k, k, BNT)) * decay, 0.0)


def _gdn_rest(q, k, v, bcol, gc_col, gc_last, decay, t_mat, S):
    qk = _twice(_bmm(q, k, BNT))
    k2, q2 = _twice(k), _twice(q)
    egc = jnp.exp(gc_col)
    u = _bmm(t_mat, v * bcol)
    w = _bmm(t_mat, k2 * (bcol * egc))
    v_new = u - _bmm(w, S)
    o = _bmm(q2 * egc, S) + _bmm(qk * decay, v_new)
    s_new = S * jnp.exp(gc_last) + _bmm(k2 * jnp.exp(gc_last - gc_col), v_new, BTN)
    return o, s_new


def _gdn_tb(T):
    return min(256, T)


def _gate_rows(g):
    T = g.shape[0]
    g = g[:, :GDN_V_HEADS].reshape(T // GDN_CHUNK, GDN_CHUNK, GDN_V_HEADS)
    return g.transpose(0, 2, 1)[:, :, None, :]


def _gate_cols(g):
    nc = g.shape[0]
    g = g[:, :, 0, :].transpose(0, 2, 1).reshape(nc * GDN_CHUNK, GDN_V_HEADS)
    return jnp.pad(g, ((0, 0), (0, LANE - GDN_V_HEADS)))


def _gdn_fwd(qn, kn, v, g, beta, gather=None):
    T = qn.shape[1]
    tb = _gdn_tb(T)
    nc = tb // GDN_CHUNK
    nsteps = T // tb
    quarters, buffers = gather if gather is not None else ((), ())
    ng = len(quarters)
    shapes = [a.shape for a in quarters]
    splits = [True] * ng

    def body(*refs):
        q_ref, k_ref, v_ref, g_ref, b_ref = refs[:5]
        src = refs[5:5 + ng]
        o_ref, sall_ref, tall_ref = refs[5 + 2 * ng:8 + 2 * ng]
        dst = refs[8 + 2 * ng:8 + 3 * ng]
        s_scr = refs[8 + 3 * ng]
        step = pl.program_id(0)

        @pl.when(step == 0)
        def _():
            s_scr[...] = jnp.zeros(s_scr.shape, F32)
            if ng:
                for cp in _gather_copies(shapes, splits, src, dst, *refs[9 + 3 * ng:])[0]:
                    cp.start()

        def chunk(ci, carry):
            rows = pl.ds(pl.multiple_of(ci * GDN_CHUNK, GDN_CHUNK), GDN_CHUNK)
            s = s_scr[...]
            sall_ref[ci] = s
            q, k = q_ref[:, rows, :].astype(F32), k_ref[:, rows, :].astype(F32)
            bcol, gc_col, gc_last, decay = _gdn_gates(g_ref[ci], b_ref[ci])
            t_mat = _tri_inv(_gdn_m(k, bcol, decay)).astype(BF16)
            tall_ref[ci] = t_mat
            o, s_new = _gdn_rest(q, k, v_ref[:, rows, :].astype(F32), bcol, gc_col, gc_last, decay,
                                 t_mat.astype(F32), s)
            o_ref[:, rows, :] = o.astype(o_ref.dtype)
            s_scr[...] = s_new
            return carry

        lax.fori_loop(0, nc, chunk, 0)

        if ng:
            @pl.when(step == nsteps - 1)
            def _():
                _gather_arrival(shapes, splits, src, dst, *refs[9 + 3 * ng:])

    qk_spec = pl.BlockSpec((GDN_QK_HEADS, tb, GDN_HD), lambda i: (0, i, 0))
    v_spec = pl.BlockSpec((GDN_V_HEADS, tb, GDN_HD), lambda i: (0, i, 0))
    g_spec = pl.BlockSpec((nc, GDN_V_HEADS, 1, GDN_CHUNK), lambda i: (i, 0, 0, 0))
    anywhere = pl.BlockSpec(memory_space=pl.ANY)
    return pl.pallas_call(
        body, name="gdn_fwd", grid=(nsteps,),
        in_specs=[qk_spec, qk_spec, v_spec, g_spec, g_spec] + [anywhere] * (2 * ng),
        out_specs=[v_spec, pl.BlockSpec((nc, GDN_V_HEADS, GDN_HD, GDN_HD), lambda i: (i, 0, 0, 0)),
                   pl.BlockSpec((nc, GDN_V_HEADS, GDN_CHUNK, GDN_CHUNK), lambda i: (i, 0, 0, 0))] + [anywhere] * ng,
        out_shape=[jax.ShapeDtypeStruct((GDN_V_HEADS, T, GDN_HD), BF16),
                   jax.ShapeDtypeStruct((T // GDN_CHUNK, GDN_V_HEADS, GDN_HD, GDN_HD), F32),
                   jax.ShapeDtypeStruct((T // GDN_CHUNK, GDN_V_HEADS, GDN_CHUNK, GDN_CHUNK), BF16)]
        + [jax.ShapeDtypeStruct(b.shape, b.dtype) for b in buffers],
        input_output_aliases={5 + ng + a: 3 + a for a in range(ng)},
        scratch_shapes=[pltpu.VMEM((GDN_V_HEADS, GDN_HD, GDN_HD), F32)]
        + ([pltpu.SemaphoreType.DMA((6 * ng,)), pltpu.SemaphoreType.DMA((6 * ng,))] if ng else []),
        compiler_params=_params(("arbitrary",)),
    )(qn, kn, v, g, beta, *quarters, *buffers)


def _gdn_bwd(qn, kn, v, g, beta, sall, tall, do, scatter=()):
    T = qn.shape[1]
    tb = _gdn_tb(T)
    nc = tb // GDN_CHUNK
    nb = T // tb
    ns = len(scatter)

    def body(*refs):
        q_ref, k_ref, v_ref, g_ref, b_ref, sall_ref, tall_ref, do_ref = refs[:8]
        dq_ref, dk_ref, dv_ref, dg_ref, db_ref = refs[8 + ns:13 + ns]
        ds_scr = refs[13 + 2 * ns]
        comm = (refs[8:8 + ns], refs[13 + ns:13 + 2 * ns], *refs[14 + 2 * ns:])
        step = pl.program_id(0)

        @pl.when(step == 0)
        def _():
            ds_scr[...] = jnp.zeros(ds_scr.shape, F32)
            if ns:
                for cp in _scatter_copies(*comm):
                    cp.start()

        def chunk(cr, carry):
            ci = nc - 1 - cr
            rows = pl.ds(pl.multiple_of(ci * GDN_CHUNK, GDN_CHUNK), GDN_CHUNK)
            k, t_mat = k_ref[:, rows, :].astype(F32), tall_ref[ci].astype(F32)
            (bcol, gc_col, gc_last, decay), vjp_gates = jax.vjp(_gdn_gates, g_ref[ci], b_ref[ci])
            _, vjp = jax.vjp(_gdn_rest, q_ref[:, rows, :].astype(F32), k, v_ref[:, rows, :].astype(F32),
                             bcol, gc_col, gc_last, decay, t_mat, sall_ref[ci])
            dq, dk, dv, dbcol, dgc_col, dgc_last, ddecay, dt, ds = vjp((do_ref[:, rows, :].astype(F32), ds_scr[...]))
            _, vjp_m = jax.vjp(_gdn_m, k, bcol, decay)
            dk_m, dbcol_m, ddecay_m = vjp_m(_tri_inv_vjp(t_mat, dt))
            dg, db = vjp_gates((dbcol + dbcol_m, dgc_col, dgc_last, ddecay + ddecay_m))
            ds_scr[...] = ds
            dq_ref[:, rows, :] = dq
            dk_ref[:, rows, :] = dk + dk_m
            dv_ref[:, rows, :] = dv
            dg_ref[ci] = dg
            db_ref[ci] = db
            return carry

        lax.fori_loop(0, nc, chunk, 0)

        if ns:
            @pl.when(step == nb - 1)
            def _():
                copies = _scatter_copies(*comm)
                for cp in copies:
                    cp.wait_recv()
                for cp in copies:
                    cp.wait_send()

    qk_spec = pl.BlockSpec((GDN_QK_HEADS, tb, GDN_HD), lambda i: (0, nb - 1 - i, 0))
    v_spec = pl.BlockSpec((GDN_V_HEADS, tb, GDN_HD), lambda i: (0, nb - 1 - i, 0))
    g_spec = pl.BlockSpec((nc, GDN_V_HEADS, 1, GDN_CHUNK), lambda i: (nb - 1 - i, 0, 0, 0))
    s_spec = pl.BlockSpec((nc, GDN_V_HEADS, GDN_HD, GDN_HD), lambda i: (nb - 1 - i, 0, 0, 0))
    t_spec = pl.BlockSpec((nc, GDN_V_HEADS, GDN_CHUNK, GDN_CHUNK), lambda i: (nb - 1 - i, 0, 0, 0))
    anywhere = pl.BlockSpec(memory_space=pl.ANY)
    return pl.pallas_call(
        body, name="gdn_bwd", grid=(nb,),
        in_specs=[qk_spec, qk_spec, v_spec, g_spec, g_spec, s_spec, t_spec, v_spec] + [anywhere] * ns,
        out_specs=[qk_spec, qk_spec, v_spec, g_spec, g_spec] + [anywhere] * ns,
        out_shape=[jax.ShapeDtypeStruct((GDN_QK_HEADS, T, GDN_HD), F32),
                   jax.ShapeDtypeStruct((GDN_QK_HEADS, T, GDN_HD), F32),
                   jax.ShapeDtypeStruct((GDN_V_HEADS, T, GDN_HD), F32),
                   jax.ShapeDtypeStruct(g.shape, F32), jax.ShapeDtypeStruct(g.shape, F32)]
        + _scatter_shapes(scatter),
        scratch_shapes=[pltpu.VMEM((GDN_V_HEADS, GDN_HD, GDN_HD), F32)]
        + ([pltpu.SemaphoreType.DMA((3 * ns,)), pltpu.SemaphoreType.DMA((3 * ns,))] if ns else []),
        compiler_params=_params(("arbitrary",)),
    )(qn, kn, v, g, beta, sall, tall, do, *scatter)


def _gnorm_core(o, z, w):
    return tuple(_rms_core(oh, w) * _silu(zh) for oh, zh in zip(o, z))


def _gnorm_fwd(o, pm, w, tm=256):
    T = pm.shape[0]
    tm = min(tm, T)

    def fn(i, ov, zv, wv):
        zf = zv.astype(F32)
        out = _gnorm_core(tuple(ov[h].astype(F32) for h in range(GDN_V_HEADS)), _heads(zf, GDN_V_HEADS), wv)
        return (jnp.concatenate(out, axis=1),)

    ins = [(o, "row", None), (pm, "row", (GDN_V, 2)), (w, "full", None)]
    return _rowcall("gnorm_fwd", fn, T, tm, ins, [((T, GDN_V), BF16, "row")])[0]


def _gnorm_bwd(o, pm, w, don, tm=128, swap=()):
    T = pm.shape[0]
    tm = min(tm, T)

    def fn(i, ov, zv, wv, dv):
        zf, df = zv.astype(F32), dv.astype(F32)
        _, vjp = jax.vjp(_gnorm_core, tuple(ov[h].astype(F32) for h in range(GDN_V_HEADS)),
                         _heads(zf, GDN_V_HEADS), wv)
        do, dz, dw = vjp(_heads(df, GDN_V_HEADS))
        return jnp.stack(do), jnp.concatenate(dz, axis=1), dw

    ins = [(o, "row", None), (pm, "row", (GDN_V, 2)), (w, "full", None), (don, "row", None)]
    outs = [((GDN_V_HEADS, T, GDN_HD), BF16, "row"), ((T, GDN_V), BF16, "row"), ((1, GDN_HD), F32, "acc")]
    return _rowcall("gnorm_bwd", fn, T, tm, ins, outs, swap=swap)


def _ffn_act_fwd(name, up, conv_w, conv_b, tm=256):
    T = up.shape[0]
    tm = min(tm, T)

    def fn(i, x, halo, cw, cb):
        u = _conv_fwd(_taps(x.astype(F32), halo.astype(F32), 3, i), cw) + cb
        return (_silu(u[:, :DFF]) * u[:, DFF:],)

    ins = [(up, "row", None), (up, "prev", None), (conv_w, "full", None), (conv_b, "full", None)]
    return _rowcall(name, fn, T, tm, ins, [((T, DFF), BF16, "row")])[0]


def _ffn_act_bwd(name, up, conv_w, conv_b, dact, tm=256):
    T = up.shape[0]
    tm = min(tm, T)

    def fn(i, x, halo, cw, cb, da):
        taps = _taps(x.astype(F32), halo.astype(F32), 3, i)
        da = da.astype(F32)
        u = _conv_fwd(taps, cw) + cb
        gate, val = u[:, :DFF], u[:, DFF:]
        sg = _sigmoid(gate)
        dgate = da * val * sg * (1.0 + gate * (1.0 - sg))
        dval = da * gate * sg
        du = jnp.concatenate([dgate, dval], axis=1)
        return du, _conv_dw(du, taps), jnp.sum(du, axis=0, keepdims=True)

    ins = [(up, "row", None), (up, "prev", None), (conv_w, "full", None), (conv_b, "full", None),
           (dact, "row", None)]
    outs = [((T, DFF2), BF16, "row"), ((SUBLANE, DFF2), F32, "acc"), ((1, DFF2), F32, "acc")]
    return _rowcall(name, fn, T, tm, ins, outs)


def _ffn_conv_bwd(name, du, conv_w, tm=256):
    T = du.shape[0]
    tm = min(tm, T)
    n = T // tm

    def fn(i, dv, halo, cw):
        return (_conv_dx(dv.astype(F32), halo.astype(F32), cw, i, n),)

    ins = [(du, "row", None), (du, "next", None), (conv_w, "full", None)]
    return _rowcall(name, fn, T, tm, ins, [((T, DFF2), BF16, "row")])[0]


GROUP_ROWS = SWA_GROUP * SWA_BLOCK


def _attn_core(q, kp, kc, vp, vc, bias, sink, mask):
    kcat = jnp.concatenate([kp, kc], axis=0)
    vcat = jnp.concatenate([vp, vc], axis=0)
    s = _bdot(q * (SWA_HD ** -0.5), kcat, NT) + bias
    s = jnp.where(mask, s, NEG_INF)
    m = lax.stop_gradient(jnp.maximum(jnp.max(s, axis=-1, keepdims=True), sink))
    p = jnp.exp(s - m)
    denom = jnp.sum(p, axis=-1, keepdims=True) + jnp.exp(sink - m)
    return _bdot(p / denom, vcat)


def _attn_mask(i):
    qi = lax.broadcasted_iota(jnp.int32, (GROUP_ROWS, 2 * SWA_BLOCK), 0) & (SWA_BLOCK - 1)
    ki = lax.broadcasted_iota(jnp.int32, (GROUP_ROWS, 2 * SWA_BLOCK), 1)
    dist = qi + SWA_BLOCK - ki
    return (dist >= 0) & (dist < SWA_BLOCK) & ((ki >= SWA_BLOCK) | (i > 0))


def _head_cols(h):
    return slice(h * SWA_HD, (h + 1) * SWA_HD)


def _block_rows(b):
    return slice(b * SWA_BLOCK, (b + 1) * SWA_BLOCK)


def _stacked_heads(ref, rows, j):
    return jnp.concatenate([ref[rows, _head_cols(SWA_GROUP * j + g)].astype(F32) for g in range(SWA_GROUP)], axis=0)


def _store_heads(ref, rows, j, stacked):
    for g in range(SWA_GROUP):
        ref[rows, _head_cols(SWA_GROUP * j + g)] = stacked[g * SWA_BLOCK:(g + 1) * SWA_BLOCK].astype(ref.dtype)


def _attn_chains(i, nblk, q_ref, kvc_ref, kvp_ref, b_ref, s_ref):
    chains = []
    for b in range(nblk):
        rows = _block_rows(b)
        mask = _attn_mask(i) if b == 0 else _attn_mask(1)
        before, before_rows = (kvp_ref, _block_rows(0)) if b == 0 else (kvc_ref, _block_rows(b - 1))
        for j in range(SWA_KV_HEADS):
            heads = slice(SWA_GROUP * j, SWA_GROUP * (j + 1))
            k_cols, v_cols = _head_cols(j), _head_cols(SWA_KV_HEADS + j)
            sink = jnp.concatenate(
                [jnp.broadcast_to(s_ref[j, g:g + 1, 0:1], (SWA_BLOCK, 1)) for g in range(SWA_GROUP)], axis=0)
            ops = (_stacked_heads(q_ref, rows, j), before[before_rows, k_cols].astype(F32),
                   kvc_ref[rows, k_cols].astype(F32), before[before_rows, v_cols].astype(F32),
                   kvc_ref[rows, v_cols].astype(F32), b_ref[heads].reshape(GROUP_ROWS, 2 * SWA_BLOCK), sink)
            chains.append((rows, j, ops, mask))
    return chains


def _attn_fwd_flat(q, kv, bias, sinks):
    T = q.shape[0]
    nblk = _tile(T // SWA_BLOCK, (4, 2, 1))
    rows = nblk * SWA_BLOCK

    def body(q_ref, kvc_ref, kvp_ref, b_ref, s_ref, o_ref):
        chains = _attn_chains(pl.program_id(0), nblk, q_ref, kvc_ref, kvp_ref, b_ref, s_ref)
        outs = [_attn_core(*ops, mask) for _, _, ops, mask in chains]
        for (blk, j, _, _), out in zip(chains, outs):
            _store_heads(o_ref, blk, j, out)

    q_spec = pl.BlockSpec((rows, q.shape[1]), lambda i: (i, 0))
    cur = pl.BlockSpec((rows, kv.shape[1]), lambda i: (i, 0))
    prev = pl.BlockSpec((SWA_BLOCK, kv.shape[1]), lambda i: (jnp.maximum(nblk * i - 1, 0), 0))
    return pl.pallas_call(
        body, name="attn_fwd", grid=(T // rows,),
        in_specs=[q_spec, cur, prev, pl.BlockSpec(bias.shape, lambda i: (0, 0, 0)),
                  pl.BlockSpec(sinks.shape, lambda i: (0, 0, 0))],
        out_specs=q_spec, out_shape=jax.ShapeDtypeStruct(q.shape, BF16),
        compiler_params=_params(("arbitrary",)),
    )(q, kv, kv, bias, sinks)


def _attn_bwd_flat(q, kv, bias, sinks, do):
    T = q.shape[0]
    nb = T // SWA_BLOCK
    rows = _block_rows(0)

    def body(q_ref, kvc_ref, kvp_ref, b_ref, s_ref, do_ref, dq_ref, dkv_ref, db_ref, dsk_ref, carry):
        i = pl.program_id(0)

        @pl.when(i < nb)
        def _():
            chains = _attn_chains(i, 1, q_ref, kvc_ref, kvp_ref, b_ref, s_ref)
            cots = [_stacked_heads(do_ref, rows, j) for _, j, _, _ in chains]
            grads = [jax.vjp(functools.partial(_attn_core, mask=mask), *ops)[1](cot)
                     for (_, _, ops, mask), cot in zip(chains, cots)]
            for j, (dq, dkp, dkc, dvp, dvc, db, dsc) in enumerate(grads):
                heads = slice(SWA_GROUP * j, SWA_GROUP * (j + 1))
                k_cols, v_cols = _head_cols(j), _head_cols(SWA_KV_HEADS + j)
                _store_heads(dq_ref, rows, j, dq)
                db = db.reshape(SWA_GROUP, SWA_BLOCK, 2 * SWA_BLOCK)
                dsk = jnp.concatenate(
                    [jnp.broadcast_to(jnp.sum(dsc[g * SWA_BLOCK:(g + 1) * SWA_BLOCK], axis=0, keepdims=True),
                                      (1, LANE)) for g in range(SWA_GROUP)], axis=0)

                @pl.when(i == 0)
                def _():
                    db_ref[heads] = db
                    dsk_ref[j] = dsk

                @pl.when(i > 0)
                def _():
                    db_ref[heads] += db
                    dsk_ref[j] += dsk
                    dkv_ref[:, k_cols] = (carry[:, k_cols] + dkp).astype(dkv_ref.dtype)
                    dkv_ref[:, v_cols] = (carry[:, v_cols] + dvp).astype(dkv_ref.dtype)

                carry[:, k_cols] = dkc
                carry[:, v_cols] = dvc

        @pl.when(i == nb)
        def _():
            dkv_ref[...] = carry[...].astype(dkv_ref.dtype)

    last = nb - 1
    q_spec = pl.BlockSpec((SWA_BLOCK, q.shape[1]), lambda i: (jnp.minimum(i, last), 0))
    cur = pl.BlockSpec((SWA_BLOCK, kv.shape[1]), lambda i: (jnp.minimum(i, last), 0))
    prev = pl.BlockSpec((SWA_BLOCK, kv.shape[1]), lambda i: (jnp.clip(i - 1, 0, last), 0))
    b_spec = pl.BlockSpec(bias.shape, lambda i: (0, 0, 0))
    s_spec = pl.BlockSpec(sinks.shape, lambda i: (0, 0, 0))
    return pl.pallas_call(
        body, name="attn_bwd", grid=(nb + 1,),
        in_specs=[q_spec, cur, prev, b_spec, s_spec, q_spec],
        out_specs=[q_spec, prev, b_spec, s_spec],
        out_shape=[jax.ShapeDtypeStruct(q.shape, BF16), jax.ShapeDtypeStruct(kv.shape, BF16),
                   jax.ShapeDtypeStruct(bias.shape, F32), jax.ShapeDtypeStruct(sinks.shape, F32)],
        scratch_shapes=[pltpu.VMEM((SWA_BLOCK, kv.shape[1]), F32)],
        compiler_params=_params(("arbitrary",)),
    )(q, kv, kv, bias, sinks, do)


def _rel_onehot():
    qi = jnp.arange(SWA_BLOCK)[:, None]
    ki = jnp.arange(2 * SWA_BLOCK)[None, :]
    n = jnp.maximum(qi + SWA_BLOCK - ki, 0)
    max_exact = REL_BUCKETS // 2
    nf = jnp.maximum(n, 1).astype(F32)
    large = max_exact + (jnp.log(nf / max_exact) / math.log(REL_MAX_DISTANCE / max_exact)
                         * (REL_BUCKETS - max_exact)).astype(jnp.int32)
    bucket = jnp.where(n < max_exact, n, jnp.minimum(large, REL_BUCKETS - 1)).reshape(-1)
    return (bucket[None, :] == jnp.arange(REL_BUCKETS)[:, None]).astype(F32)


def _final(h, w, target, tm=256):
    T = h.shape[0]
    tm = min(tm, T)

    def fn(i, hv, wv, tv):
        y, vjp = jax.vjp(_rms_core, hv, wv)
        err = y - tv
        dh, dw = vjp(err * (1.0 / D))
        part = 0.5 * jnp.sum(jnp.sum(err * err, axis=1, keepdims=True) * (1.0 / D), axis=0, keepdims=True)
        return jnp.broadcast_to(part, (SUBLANE, LANE)), dh, dw

    ins = [(h, "row", None), (w, "full", None), (target, "row", None)]
    outs = [((SUBLANE, LANE), F32, "acc"), ((T, D), F32, "row"), ((1, D), F32, "acc")]
    return _rowcall("final", fn, T, tm, ins, outs)


def _ffn_fwd(tag, h, P, layer):
    n = (h, P["ffn_norm_w"][layer:layer + 1])
    up = _mm_up(f"{tag}_up", n, P["w_up"], layer)
    act = _ffn_act_fwd(f"{tag}_act", up, P["ffn_conv_w"][layer], P["ffn_conv_b"][layer:layer + 1])
    out = _mm_nn(f"{tag}_down", act, P["w_down"][layer], F32, res=h)
    return out, (n, up, act)


def _ffn_bwd(tag, h, saved, dout, P, layer, into=(None, None)):
    n, up, act = saved
    cw, cb = P["ffn_conv_w"][layer], P["ffn_conv_b"][layer:layer + 1]
    dact = _mm_nt(f"{tag}_down_dx", dout, P["w_down"][layer], BF16)
    g_down = _mm_down_tn(f"{tag}_down_dw", act, dout, layer, into[1])
    du, dcw, dcb = _ffn_act_bwd(f"{tag}_act_bwd", up, cw, cb, dact)
    dup = _ffn_conv_bwd(f"{tag}_conv_bwd", du, cw)
    g_up = _mm_up_tn(f"{tag}_up_dw", n, dup, layer, into[0])
    dh, dnw = _mm_up_nt(f"{tag}_up_dx", dup, P["w_up"], layer, post=(h, P["ffn_norm_w"][layer:layer + 1], dout))
    return dh, dict(w_down=g_down, w_up=g_up, conv_w=dcw[:3], conv_b=dcb, norm_w=dnw)


def _local_step(x, target, P, late=None, pair_sums=None):
    T = x.shape[0]
    n0 = (x, P["a_norm_w"])
    pm = _mm_nn("gdn_in", n0, P["w_in_main"], BF16)
    pba = _mm_nn("gdn_in_ba", n0, P["w_in_ba"], F32)
    qn, kn, v, beta, g = _gdn_pre_fwd(pm, pba, P["a_conv_w"], P["a_log"], P["dt_bias"])
    g_rows, beta_rows = _gate_rows(g), _gate_rows(beta)
    o, sall, tall, *gathered = _gdn_fwd(qn, kn, v, g_rows, beta_rows, gather=late)
    if late is not None:
        P = {**P, **_late_weights(gathered)}
    on = _gnorm_fwd(o, pm, P["a_out_norm_w"])
    h1 = _mm_nn("gdn_out", on, P["w_out"], F32, res=x)
    h2, ffn0 = _ffn_fwd("ffn0", h1, P, 0)
    nkv = (h2, P["kv_norm_w"])
    kv = _mm_nn("kv_proj", nkv, P["w_kv"], BF16)
    nb = (h2, P["b_norm_w"])
    qp = _mm_nn("q_proj", nb, P["w_q"], BF16)
    onehot = _rel_onehot()
    bias = _mm_nn("rel_bias", P["rel_table_t"], onehot, F32, precision=HIGHEST)
    bias = bias.reshape(SWA_Q_HEADS, SWA_BLOCK, 2 * SWA_BLOCK)
    oa = _attn_fwd_flat(qp, kv, bias, P["sinks"])
    h3 = _mm_nn("o_proj", oa, P["w_o"], F32, res=h2)
    h4, ffn1 = _ffn_fwd("ffn1", h3, P, 1)
    loss, dh4, d_final = _final(h4, P["final_norm_w"], target)

    dh3, gf1 = _ffn_bwd("ffn1", h3, ffn1, dh4, P, 1)
    doa = _mm_nt("o_proj_dx", dh3, P["w_o"], BF16)
    g_wo = _mm_tn("o_proj_dw", oa, dh3)
    dqp, dkv, dbias, dsinks = _attn_bwd_flat(qp, kv, bias, P["sinks"], doa)
    g_wq = _mm_tn("q_proj_dw", nb, dqp)
    g_wkv = _mm_tn("kv_proj_dw", nkv, dkv)
    dh2, d_bnorm = _mm_nt("q_proj_dx", dqp, P["w_q"], F32, post=(h2, P["b_norm_w"], dh3))
    dh2, d_kvnorm = _mm_nt("kv_proj_dx", dkv, P["w_kv"], F32, post=(h2, P["kv_norm_w"], dh2))
    g_table = _mm_nt("rel_bias_dw", onehot, dbias.reshape(SWA_Q_HEADS, -1), F32, precision=HIGHEST)
    dh1, gf0 = _ffn_bwd("ffn0", h1, ffn0, dh2, P, 0, into=(gf1["w_up"], gf1["w_down"]))
    don = _mm_nt("gdn_out_dx", dh1, P["w_out"], BF16)
    g_wout = _mm_tn("gdn_out_dw", on, dh1)
    ready = dict(a_w_out=g_wout, w_kv=g_wkv, b_w_q=g_wq, b_w_o=g_wo, ffn_w_up=gf0["w_up"], ffn_w_down=gf0["w_down"])
    names = [n for n in BIG if n in ready]
    whole = [_chip_major(n, ready[n]) for n in names] if pair_sums is not None else []
    do, dz, d_gnorm, *other = _gnorm_bwd(o, pm, P["a_out_norm_w"], don, swap=whole)
    pairs = pair_sums(names, whole, other) if pair_sums is not None else []
    dq, dk, dv, dg, dbeta, *parts = _gdn_bwd(qn, kn, v, g_rows, beta_rows, sall, tall, do, scatter=pairs)
    dy, dpba, d_aconv, d_alog, d_dtb = _gdn_pre_bwd(pm, pba, P["a_conv_w"], P["a_log"], P["dt_bias"],
                                                    dq, dk, dv, _gate_cols(dbeta), _gate_cols(dg))
    dpm = _gdn_conv_bwd(dy, dz, P["a_conv_w"])
    g_win_main = _mm_tn("gdn_in_dw", n0, dpm)
    g_win_ba = _mm_tn("gdn_in_ba_dw", n0, dpba)
    nh = GDN_V_HEADS
    g_win = jnp.concatenate([g_win_main, g_win_ba[:, :nh], g_win_ba[:, LANE:LANE + nh]], axis=1)
    last_whole = [_chip_major("a_w_in", g_win)]
    last_pair = pair_sums(["a_w_in"], last_whole, _pair_swap(last_whole, "late")) if pair_sums is not None else []
    dn0 = _mm_nt("gdn_in_dx", dpm, P["w_in_main"], F32, scatter=last_pair)
    dn0, last_parts = (dn0[0], dn0[1:]) if last_pair else (dn0, [])
    dx, d_anorm = _mm_nt("gdn_in_ba_dx", dpba, P["w_in_ba"], F32, res=dn0, post=(x, P["a_norm_w"], dh1))

    nh = GDN_V_HEADS
    grads = dict(
        a_norm_w=d_anorm,
        a_w_in=g_win,
        a_conv_w=d_aconv[:4], a_a_log=d_alog[:, :nh], a_dt_bias=d_dtb[:, :nh], a_out_norm_w=d_gnorm,
        a_w_out=g_wout, kv_norm_w=d_kvnorm, w_kv=g_wkv, b_norm_w=d_bnorm, b_w_q=g_wq,
        b_sinks=dsinks[:, :, 0].reshape(1, SWA_Q_HEADS), b_w_o=g_wo, rel_bias_table=g_table,
        ffn_norm_w=jnp.concatenate([gf0["norm_w"], gf1["norm_w"]], axis=0),
        ffn_w_up=gf0["w_up"],
        ffn_conv_w=jnp.stack([gf0["conv_w"], gf1["conv_w"]], axis=0),
        ffn_conv_b=jnp.concatenate([gf0["conv_b"], gf1["conv_b"]], axis=0),
        ffn_w_down=gf0["w_down"],
        final_norm_w=d_final,
    )
    scattered = dict(zip([n for n in BIG if n in ready], zip(pairs, parts)))
    scattered.update(zip(["a_w_in"], zip(last_pair, last_parts)))
    return loss, dx, grads, scattered


HBM_SPEC = pl.BlockSpec(memory_space=pltpu.HBM)
VMEM_SPEC = pl.BlockSpec(memory_space=pltpu.VMEM)


def _coords():
    return lax.axis_index("x"), lax.axis_index("y"), lax.axis_index("c")


def _remote(src, dst, send_sem, recv_sem, device):
    return pltpu.make_async_remote_copy(src_ref=src, dst_ref=dst, send_sem=send_sem, recv_sem=recv_sem,
                                        device_id=device, device_id_type=MESH)


def _other_chips(x, y):
    return [(1 - x, y), (x, 1 - y), (1 - x, 1 - y)]


def _gather_copies(shapes, split, ins, outs, send_sems, recv_sems):
    x, y, c = _coords()
    p = 2 * x + y
    ici, forwards, from_sibling = [], [], []
    for a, shape in enumerate(shapes):
        h = shape[0] // 2
        for j, chip in enumerate(_other_chips(x, y)):
            q = 2 * chip[0] + chip[1]
            if split[a]:
                mine, theirs = pl.ds(c * h, h), pl.ds((1 - c) * h, h)
                ici.append(_remote(ins[a].at[mine], outs[a].at[p, mine], send_sems.at[6 * a + j],
                                   recv_sems.at[6 * a + j], (*chip, c)))
                land = outs[a].at[q, mine]
                forwards.append(_remote(land, land, send_sems.at[6 * a + 3 + j], recv_sems.at[6 * a + 3 + j],
                                        (x, y, 1 - c)))
                land = outs[a].at[q, theirs]
                from_sibling.append(_remote(land, land, send_sems.at[6 * a + 3 + j], recv_sems.at[6 * a + 3 + j],
                                            (x, y, 1 - c)))
            else:
                ici.append(_remote(ins[a], outs[a].at[p], send_sems.at[6 * a + j], recv_sems.at[6 * a + j],
                                   (*chip, c)))
                forwards.append(None)
    return ici, forwards, from_sibling


def _gather_arrival(shapes, split, ins, outs, send_sems, recv_sems):
    x, y, c = _coords()
    ici, forwards, from_sibling = _gather_copies(shapes, split, ins, outs, send_sems, recv_sems)
    k = 0
    for a, shape in enumerate(shapes):
        h = shape[0] // 2
        for j, chip in enumerate(_other_chips(x, y)):
            q = 2 * chip[0] + chip[1]
            land = outs[a].at[q, pl.ds(c * h, h)] if split[a] else outs[a].at[q]
            _remote(land, land, send_sems.at[6 * a + j], recv_sems.at[6 * a + j], (*chip, c)).wait_recv()
            if forwards[k] is not None:
                forwards[k].start()
            k += 1
    for cp in from_sibling:
        cp.wait_recv()
    for cp in ici + [f for f in forwards if f is not None]:
        cp.wait_send()


def _all_gather(arrs, split, remote):
    n = len(arrs)
    now = [a for a in range(n) if remote[a]]
    shapes = [arrs[a].shape for a in now]
    splits = [split[a] for a in now]

    def body(*refs):
        ins, outs, stage = refs[:n], refs[n:2 * n], refs[2 * n:3 * n]
        send_sems, recv_sems, in_sems, out_sems = refs[3 * n:]
        p = 2 * lax.axis_index("x") + lax.axis_index("y")
        gathered = ([ins[a] for a in now], [outs[a] for a in now], send_sems, recv_sems)
        loads = [pltpu.make_async_copy(ins[a], stage[a], in_sems.at[a]) for a in range(n)]
        for cp in loads:
            cp.start()
        for cp in _gather_copies(shapes, splits, *gathered)[0]:
            cp.start()
        stores = [pltpu.make_async_copy(stage[a], outs[a].at[p], out_sems.at[a]) for a in range(n)]
        for a in range(n):
            loads[a].wait()
            stores[a].start()
        _gather_arrival(shapes, splits, *gathered)
        for cp in stores:
            cp.wait()

    return pl.pallas_call(
        body, name="weights_all_gather", in_specs=[HBM_SPEC] * n, out_specs=[HBM_SPEC] * n,
        out_shape=[jax.ShapeDtypeStruct((N_CHIPS,) + a.shape, a.dtype) for a in arrs],
        scratch_shapes=[pltpu.VMEM(a.shape, a.dtype) for a in arrs]
        + [pltpu.SemaphoreType.DMA((6 * len(now),)), pltpu.SemaphoreType.DMA((6 * len(now),)),
           pltpu.SemaphoreType.DMA((n,)), pltpu.SemaphoreType.DMA((n,))],
        compiler_params=pltpu.CompilerParams(vmem_limit_bytes=VMEM_LIMIT),
    )(*arrs)


PAIR_SWAP_PIECES = 2


def _swap_copies(shapes, ins, other, send_sems, recv_sems):
    x, y, c = _coords()
    copies = []
    for a, shape in enumerate(shapes):
        h = shape[1] // 2
        piece = h // PAIR_SWAP_PIECES
        for q in range(N_CHIPS):
            for r in range(PAIR_SWAP_PIECES):
                k = (a * N_CHIPS + q) * PAIR_SWAP_PIECES + r
                copies.append(_remote(ins[a].at[q, pl.ds((1 - c) * h + r * piece, piece)],
                                      other[a].at[q, pl.ds(r * piece, piece)], send_sems.at[k], recv_sems.at[k],
                                      (x, y, 1 - c)))
    return copies


def _swap_out_shapes(gs):
    return [jax.ShapeDtypeStruct((N_CHIPS, g.shape[1] // 2, g.shape[2]), g.dtype) for g in gs]


def _pair_swap(gs, tag):
    n = len(gs)
    shapes = [g.shape for g in gs]

    def body(*refs):
        copies = _swap_copies(shapes, refs[:n], refs[n:2 * n], *refs[2 * n:])
        for cp in copies:
            cp.start()
        for cp in copies:
            cp.wait()

    nsem = n * N_CHIPS * PAIR_SWAP_PIECES
    return pl.pallas_call(
        body, name=f"grads_pair_swap_{tag}", in_specs=[HBM_SPEC] * n, out_specs=[HBM_SPEC] * n,
        out_shape=_swap_out_shapes(gs),
        scratch_shapes=[pltpu.SemaphoreType.DMA((nsem,)), pltpu.SemaphoreType.DMA((nsem,))],
    )(*gs)


def _scatter_copies(ins, outs, send_sems, recv_sems):
    x, y, c = _coords()
    copies = []
    for a in range(len(ins)):
        for j, chip in enumerate(_other_chips(x, y)):
            q = 2 * chip[0] + chip[1]
            copies.append(_remote(ins[a].at[q], outs[a].at[j], send_sems.at[3 * a + j], recv_sems.at[3 * a + j],
                                  (*chip, c)))
    return copies


def _scatter_shapes(ps):
    return [jax.ShapeDtypeStruct((N_CHIPS - 1,) + a.shape[1:], a.dtype) for a in ps]


def _pair_share(rs):
    n = len(rs)

    def body(*refs):
        ins, outs, stage = refs[:n], refs[n:2 * n], refs[2 * n:3 * n]
        send_sems, recv_sems, in_sems, out_sems = refs[3 * n:]
        x, y, c = _coords()

        def mine(a):
            h = rs[a].shape[0]
            return outs[a].at[pl.ds(c * h, h)]

        loads = [pltpu.make_async_copy(ins[a], stage[a], in_sems.at[a]) for a in range(n)]
        for cp in loads:
            cp.start()
        sends = [_remote(ins[a], mine(a), send_sems.at[a], recv_sems.at[a], (x, y, 1 - c)) for a in range(n)]
        for cp in sends:
            cp.start()
        stores = [pltpu.make_async_copy(stage[a], mine(a), out_sems.at[a]) for a in range(n)]
        for a in range(n):
            loads[a].wait()
            stores[a].start()
        for a in range(n):
            h = rs[a].shape[0]
            land = outs[a].at[pl.ds((1 - c) * h, h)]
            _remote(land, land, send_sems.at[a], recv_sems.at[a], (x, y, 1 - c)).wait_recv()
        for cp in sends:
            cp.wait_send()
        for cp in stores:
            cp.wait()

    return pl.pallas_call(
        body, name="grads_pair_share", in_specs=[HBM_SPEC] * n, out_specs=[HBM_SPEC] * n,
        out_shape=[jax.ShapeDtypeStruct((2 * a.shape[0], a.shape[1]), a.dtype) for a in rs],
        scratch_shapes=[pltpu.VMEM(a.shape, a.dtype) for a in rs] + [pltpu.SemaphoreType.DMA((n,))] * 4,
        compiler_params=pltpu.CompilerParams(vmem_limit_bytes=VMEM_LIMIT),
    )(*rs)


def _small_all_reduce(buf):
    R = buf.shape[0]
    ndev = 2 * N_CHIPS

    def body(in_ref, out_ref, gath, send_sems, recv_sems):
        x, y, c = _coords()
        me = 4 * x + 2 * y + c
        gath[me] = in_ref[...]
        peers = []
        for d in range(1, ndev):
            px = 1 - x if d & 4 else x
            py = 1 - y if d & 2 else y
            pc = 1 - c if d & 1 else c
            peers.append((px, py, pc))
        sends = []
        for d, peer in enumerate(peers):
            cp = _remote(in_ref, gath.at[me], send_sems.at[d], recv_sems.at[d], peer)
            cp.start()
            sends.append(cp)
        for d, peer in enumerate(peers):
            land = gath.at[4 * peer[0] + 2 * peer[1] + peer[2]]
            _remote(land, land, send_sems.at[d], recv_sems.at[d], peer).wait_recv()
        for cp in sends:
            cp.wait_send()
        acc = gath[0]
        for s in range(1, ndev):
            acc = acc + gath[s]
        out_ref[...] = acc

    return pl.pallas_call(
        body, name="small_all_reduce", in_specs=[VMEM_SPEC], out_specs=VMEM_SPEC,
        out_shape=jax.ShapeDtypeStruct(buf.shape, F32),
        scratch_shapes=[pltpu.VMEM((ndev, R, LANE), F32), pltpu.SemaphoreType.DMA((ndev - 1,)),
                        pltpu.SemaphoreType.DMA((ndev - 1,))],
    )(buf)


def _pair_add(name, own, other):
    h = own.shape[1]
    tm = _tile(h, (128, 64, 32, 16))

    def fn(i, a, b):
        return (a + b,)

    return _rowcall(name, fn, h, tm, [(own, "row", None), (other, "row", None)], [(own.shape, BF16, "row")])[0]


def _chip_add(name, own, parts):
    h = parts.shape[1]
    tm = _tile(h, (128, 64, 32, 16))

    def fn(i, o, a):
        a = a.astype(F32)
        return (((o.astype(F32) + a[0]) + a[1]) + a[2],)

    return _rowcall(name, fn, h, tm, [(own, "row", None), (parts, "row", None)], [(parts.shape[1:], F32, "row")])[0]


def _adamw(name, w, g, m, v):
    R = w.shape[0]
    tm = _tile(R, (256, 128, 64, 32, 16, 8))

    def fn(i, wv, gv, mv, vv):
        m2 = ADAM_B1 * mv + (1.0 - ADAM_B1) * gv
        v2 = ADAM_B2 * vv + (1.0 - ADAM_B2) * (gv * gv)
        m_hat = m2 / (1.0 - ADAM_B1 ** ADAM_STEP)
        v_hat = v2 / (1.0 - ADAM_B2 ** ADAM_STEP)
        delta = -ADAM_LR * (m_hat / (jnp.sqrt(v_hat) + ADAM_EPS) + ADAM_WD * wv)
        return delta, m2, v2

    ins = [(a, "row", None) for a in (w, g, m, v)]
    return _rowcall(name, fn, R, tm, ins, [(w.shape, F32, "row")] * 3)


def _pack(arrs):
    flat = jnp.concatenate([a.reshape(-1).astype(F32) for a in arrs])
    size = flat.shape[0]
    padded = -(-size // (SUBLANE * LANE)) * SUBLANE * LANE
    return jnp.pad(flat, (0, padded - size)).reshape(-1, LANE)


def _unpack(buf, shapes):
    flat = buf.reshape(-1)
    out, off = [], 0
    for s in shapes:
        size = math.prod(s)
        out.append(flat[off:off + size].reshape(s))
        off += size
    return out


BIG = ("a_w_in", "a_w_out", "w_kv", "b_w_q", "b_w_o", "ffn_w_up", "ffn_w_down")
WEIGHTS = ("a_norm_w", "a_w_in", "a_conv_w", "a_a_log", "a_dt_bias", "a_out_norm_w", "a_w_out", "kv_norm_w", "w_kv",
           "b_norm_w", "b_w_q", "b_sinks", "b_w_o", "rel_bias_table", "ffn_norm_w", "ffn_w_up", "ffn_conv_w",
           "ffn_conv_b", "ffn_w_down", "final_norm_w")
SMALL = tuple(n for n in WEIGHTS if n not in BIG)
SMALL_SHARDED = {"a_norm_w": 1, "a_conv_w": 2, "ffn_conv_w": 2}


def _quarter_2d(name, a):
    if name in ("ffn_w_up", "ffn_w_down"):
        return a.reshape(a.shape[0] * a.shape[1], a.shape[2])
    return a.reshape(a.shape[-2], a.shape[-1])


def _whole_weights(w):
    bigs = [_quarter_2d(n, w[n]).astype(BF16) for n in BIG]
    smalls = [w["a_norm_w"], w["a_conv_w"][0], w["ffn_conv_w"].reshape(6, DFF2_SHARD)]
    remote = [True] + [False] * (len(bigs) - 1) + [True] * len(smalls)
    g = _all_gather(bigs + smalls, [True] * len(bigs) + [False] * len(smalls), remote)
    w_in = g[0].transpose(1, 0, 2).reshape(D, GDN_IN)
    nh = GDN_V_HEADS
    zpad = jnp.zeros((D, LANE - nh), BF16)
    w_in_ba = jnp.concatenate([w_in[:, GDN_MAIN:GDN_MAIN + nh], zpad, w_in[:, GDN_MAIN + nh:], zpad], axis=1)
    lane_pad = lambda a: jnp.pad(a, ((0, 0), (0, LANE - nh)))
    early = dict(
        a_norm_w=g[7].reshape(1, D), w_in_main=w_in[:, :GDN_MAIN], w_in_ba=w_in_ba,
        a_conv_w=g[8].transpose(1, 0, 2).reshape(4, GDN_CONV), a_log=lane_pad(w["a_a_log"]),
        dt_bias=lane_pad(w["a_dt_bias"]), a_out_norm_w=w["a_out_norm_w"],
        kv_norm_w=w["kv_norm_w"].reshape(1, D), b_norm_w=w["b_norm_w"],
        sinks=jnp.broadcast_to(w["b_sinks"].reshape(SWA_KV_HEADS, SWA_GROUP, 1), (SWA_KV_HEADS, SWA_GROUP, LANE)),
        rel_table_t=w["rel_bias_table"].T, ffn_norm_w=w["ffn_norm_w"],
        ffn_conv_w=g[9].reshape(N_CHIPS, 2, 3, DFF2_SHARD).transpose(1, 2, 0, 3).reshape(2, 3, DFF2),
        ffn_conv_b=w["ffn_conv_b"], final_norm_w=w["final_norm_w"].reshape(1, D),
    )
    return early, (bigs[1:], g[1:len(bigs)])


def _late_weights(g):
    return dict(
        w_out=g[0].reshape(GDN_V, D), w_kv=g[1].reshape(D, 2 * SWA_KV_HEADS * SWA_HD), w_q=g[2].reshape(D, D),
        w_o=g[3].reshape(D, D), w_up=g[4].reshape(N_CHIPS, 2, D, DFF2_SHARD),
        w_down=g[5].reshape(N_CHIPS, 2, DFF_SHARD, D).transpose(1, 0, 2, 3).reshape(2, DFF, D),
    )


def _chip_major(name, g):
    if name == "a_w_in":
        return g.reshape(D, N_CHIPS, GDN_IN_SHARD).transpose(1, 0, 2)
    if name == "ffn_w_up":
        return g.reshape(N_CHIPS, 2 * D, DFF2_SHARD)
    if name == "ffn_w_down":
        return g.reshape(N_CHIPS, 2 * DFF_SHARD, D)
    return g.reshape(N_CHIPS, g.shape[0] // N_CHIPS, g.shape[1])


def kernel(x, a_norm_w, a_w_in, a_conv_w, a_a_log, a_dt_bias, a_out_norm_w, a_w_out, kv_norm_w, w_kv, b_norm_w, b_w_q, b_sinks, b_w_o, rel_bias_table, ffn_norm_w, ffn_w_up, ffn_conv_w, ffn_conv_b, ffn_w_down, final_norm_w, loss_target, m_a_norm_w, m_a_w_in, m_a_conv_w, m_a_a_log, m_a_dt_bias, m_a_out_norm_w, m_a_w_out, m_kv_norm_w, m_w_kv, m_b_norm_w, m_b_w_q, m_b_sinks, m_b_w_o, m_rel_bias_table, m_ffn_norm_w, m_ffn_w_up, m_ffn_conv_w, m_ffn_conv_b, m_ffn_w_down, m_final_norm_w, v_a_norm_w, v_a_w_in, v_a_conv_w, v_a_a_log, v_a_dt_bias, v_a_out_norm_w, v_a_w_out, v_kv_norm_w, v_w_kv, v_b_norm_w, v_b_w_q, v_b_sinks, v_b_w_o, v_rel_bias_table, v_ffn_norm_w, v_ffn_w_up, v_ffn_conv_w, v_ffn_conv_b, v_ffn_w_down, v_final_norm_w):
    w = dict(zip(WEIGHTS, (a_norm_w, a_w_in, a_conv_w, a_a_log, a_dt_bias, a_out_norm_w, a_w_out, kv_norm_w, w_kv,
                           b_norm_w, b_w_q, b_sinks, b_w_o, rel_bias_table, ffn_norm_w, ffn_w_up, ffn_conv_w,
                           ffn_conv_b, ffn_w_down, final_norm_w)))
    m = dict(zip(WEIGHTS, (m_a_norm_w, m_a_w_in, m_a_conv_w, m_a_a_log, m_a_dt_bias, m_a_out_norm_w, m_a_w_out,
                           m_kv_norm_w, m_w_kv, m_b_norm_w, m_b_w_q, m_b_sinks, m_b_w_o, m_rel_bias_table,
                           m_ffn_norm_w, m_ffn_w_up, m_ffn_conv_w, m_ffn_conv_b, m_ffn_w_down, m_final_norm_w)))
    v = dict(zip(WEIGHTS, (v_a_norm_w, v_a_w_in, v_a_conv_w, v_a_a_log, v_a_dt_bias, v_a_out_norm_w, v_a_w_out,
                           v_kv_norm_w, v_w_kv, v_b_norm_w, v_b_w_q, v_b_sinks, v_b_w_o, v_rel_bias_table,
                           v_ffn_norm_w, v_ffn_w_up, v_ffn_conv_w, v_ffn_conv_b, v_ffn_w_down, v_final_norm_w)))
    T = x.shape[1]
    chip = 2 * lax.axis_index("x") + lax.axis_index("y")

    core = lax.axis_index("c")

    def pair_sums(names, whole, other):
        own = [lax.dynamic_slice_in_dim(g, core * (g.shape[1] // 2), g.shape[1] // 2, 1) for g in whole]
        return [_pair_add(f"pair_add_{n}", a, b) for n, a, b in zip(names, own, other)]

    early, late = _whole_weights(w)
    loss_part, dx, grads, scattered = _local_step(x.reshape(T, D), loss_target.reshape(T, D), early, late, pair_sums)

    assert all(n in scattered for n in BIG)
    halves = [_chip_add(f"chip_add_{n}", lax.dynamic_index_in_dim(scattered[n][0], chip, 0, keepdims=False),
                        scattered[n][1]) for n in BIG]
    quarter = _pair_share(halves)
    out_g, out_d, out_m, out_v = {}, {}, {}, {}
    for n, g2 in zip(BIG, quarter):
        res = _adamw(f"adamw_{n}", _quarter_2d(n, w[n]), g2, _quarter_2d(n, m[n]), _quarter_2d(n, v[n]))
        out_g[n] = g2.reshape(w[n].shape)
        out_d[n], out_m[n], out_v[n] = (r.reshape(w[n].shape) for r in res)

    whole = [grads[n] for n in SMALL]
    summed = _unpack(_small_all_reduce(_pack([loss_part[0:1, 0:1]] + whole)), [(1, 1)] + [a.shape for a in whole])
    loss = summed[0].reshape(())
    small_g = []
    for n, g in zip(SMALL, summed[1:]):
        if n in SMALL_SHARDED:
            axis = SMALL_SHARDED[n]
            g = g.reshape(w[n].shape[:axis] + (-1,) + w[n].shape[axis + 1:])
            size = w[n].shape[axis]
            g = lax.dynamic_slice_in_dim(g, chip * size, size, axis)
        small_g.append(g.reshape(w[n].shape))
    shapes = [w[n].shape for n in SMALL]
    res = _adamw("adamw_small", _pack([w[n] for n in SMALL]), _pack(small_g), _pack([m[n] for n in SMALL]),
                 _pack([v[n] for n in SMALL]))
    small_d, small_m, small_v = (_unpack(r, shapes) for r in res)
    for i, n in enumerate(SMALL):
        out_g[n], out_d[n], out_m[n], out_v[n] = small_g[i], small_d[i], small_m[i], small_v[i]

    return (loss, dx.reshape(x.shape), *[out_g[n] for n in WEIGHTS], *[out_d[n] for n in WEIGHTS],
            *[out_m[n] for n in WEIGHTS], *[out_v[n] for n in WEIGHTS])
```

```python
import functools
import math

import jax
import jax.numpy as jnp
from jax import lax
from jax.experimental import pallas as pl
from jax.experimental.pallas import tpu as pltpu

F32 = jnp.float32
BF16 = jnp.bfloat16
MESH = pl.DeviceIdType.MESH
HIGHEST = lax.Precision.HIGHEST

D = 1024
EPS = 1e-6
NEG_INF = -1e30
N_CHIPS = 4

GDN_QK_HEADS = 8
GDN_V_HEADS = 16
GDN_HD = 128
GDN_QK = GDN_QK_HEADS * GDN_HD
GDN_V = GDN_V_HEADS * GDN_HD
GDN_CONV = 2 * GDN_QK + GDN_V
GDN_MAIN = GDN_CONV + GDN_V
GDN_IN = GDN_MAIN + 2 * GDN_V_HEADS
GDN_IN_SHARD = GDN_IN // N_CHIPS
GDN_CHUNK = 64

SWA_Q_HEADS = 16
SWA_KV_HEADS = 4
SWA_GROUP = 4
SWA_HD = 64
SWA_BLOCK = 128
REL_BUCKETS = 32
REL_MAX_DISTANCE = 128

DFF = 2816
DFF2 = 2 * DFF
DFF2_SHARD = DFF2 // N_CHIPS
DFF_SHARD = DFF // N_CHIPS

ADAM_LR = 0.001
ADAM_B1 = 0.9
ADAM_B2 = 0.999
ADAM_EPS = 1e-08
ADAM_WD = 0.01
ADAM_STEP = 10

LANE = 128
SUBLANE = 8
VMEM_LIMIT = 56 * 1024 * 1024


def _params(sem, vmem=VMEM_LIMIT):
    return pltpu.CompilerParams(dimension_semantics=sem, vmem_limit_bytes=vmem)


def _rowcall(name, fn, T, tm, ins, outs, swap=()):
    n = T // tm
    nswap = len(swap)
    swap_shapes = [g.shape for g in swap]
    r8 = tm // SUBLANE
    last8 = T // SUBLANE - 1
    arrays, in_specs = [], []
    for arr, kind, cols in ins:
        arrays.append(arr)
        if kind == "full":
            in_specs.append(pl.BlockSpec(arr.shape, functools.partial(lambda nd, i: (0,) * nd, arr.ndim)))
        elif arr.ndim == 2:
            w, ci = cols if cols is not None else (arr.shape[1], 0)
            if kind == "row":
                in_specs.append(pl.BlockSpec((tm, w), functools.partial(lambda ci, i: (i, ci), ci)))
            elif kind == "prev":
                in_specs.append(pl.BlockSpec(
                    (SUBLANE, w), functools.partial(lambda ci, i: (jnp.maximum(i * r8 - 1, 0), ci), ci)))
            else:
                in_specs.append(pl.BlockSpec(
                    (SUBLANE, w), functools.partial(lambda ci, i: (jnp.minimum((i + 1) * r8, last8), ci), ci)))
        else:
            lead = arr.shape[:-2]
            in_specs.append(pl.BlockSpec(lead + (tm, arr.shape[-1]),
                                         functools.partial(lambda nl, i: (0,) * nl + (i, 0), len(lead))))
    out_shape, out_specs = [], []
    for shape, dtype, kind in outs:
        out_shape.append(jax.ShapeDtypeStruct(shape, dtype))
        if kind == "acc":
            out_specs.append(pl.BlockSpec(shape, functools.partial(lambda nd, i: (0,) * nd, len(shape))))
        else:
            lead = shape[:-2]
            out_specs.append(pl.BlockSpec(lead + (tm, shape[-1]),
                                          functools.partial(lambda nl, i: (0,) * nl + (i, 0), len(lead))))
    nin = len(arrays)

    nout = len(outs)

    def body(*refs):
        i = pl.program_id(0)
        if nswap:
            comm = (swap_shapes, refs[nin:nin + nswap], refs[nin + nswap + nout:nin + 2 * nswap + nout], refs[-2],
                    refs[-1])

            @pl.when(i == 0)
            def _():
                for cp in _swap_copies(*comm):
                    cp.start()

            @pl.when(i == n - 1)
            def _():
                for cp in _swap_copies(*comm):
                    cp.wait()

        vals = [r[...] for r in refs[:nin]]
        res = fn(i, *vals)
        for (shape, dtype, kind), o, r in zip(outs, refs[nin + nswap:], res):
            if kind == "row":
                o[...] = r.astype(dtype)
            else:
                @pl.when(i == 0)
                def _():
                    o[...] = r.astype(dtype)

                @pl.when(i > 0)
                def _():
                    o[...] += r.astype(dtype)

    anywhere = pl.BlockSpec(memory_space=pl.ANY)
    nsem = nswap * N_CHIPS * PAIR_SWAP_PIECES
    return pl.pallas_call(
        body, name=name, grid=(n,), in_specs=in_specs + [anywhere] * nswap, out_specs=out_specs + [anywhere] * nswap,
        out_shape=out_shape + _swap_out_shapes(swap),
        scratch_shapes=[pltpu.SemaphoreType.DMA((nsem,)), pltpu.SemaphoreType.DMA((nsem,))] if nswap else [],
        compiler_params=_params(("arbitrary",)),
    )(*arrays, *swap)


def _mm(name, a, b, out_shape, out_dtype, grid, a_spec, b_spec, o_spec, dims, acc_shape, res=None, precision=None,
        into=None, scatter=(), post=None):
    nk = grid[2]
    ns = len(scatter)
    a, norm_w = a if isinstance(a, tuple) else (a, None)
    normed = norm_w is not None
    posted = post is not None
    n_in = 2 + normed + (res is not None) + (into is not None) + 3 * posted + ns
    n_out = 1 + posted + ns

    def body(*refs):
        a_ref, b_ref, o_ref = refs[0], refs[1], refs[n_in]
        r_ref = refs[2 + normed] if res is not None else None
        first = pl.program_id(0) == 0
        if ns:
            comm = (refs[n_in - ns:n_in], refs[n_in + n_out - ns:n_in + n_out], refs[-2], refs[-1])
            steps = [pl.program_id(d) for d in range(3)]

            @pl.when((steps[0] == 0) & (steps[1] == 0) & (steps[2] == 0))
            def _():
                for cp in _scatter_copies(*comm):
                    cp.start()

            @pl.when((steps[0] == grid[0] - 1) & (steps[1] == grid[1] - 1) & (steps[2] == grid[2] - 1))
            def _():
                copies = _scatter_copies(*comm)
                for cp in copies:
                    cp.wait_recv()
                for cp in copies:
                    cp.wait_send()

        av, bv = a_ref[...], b_ref[...]
        if normed:
            av = _rms_core(av, refs[2][...])
        if precision is None:
            av, bv = av.astype(BF16), bv.astype(BF16)
        p = lax.dot_general(av, bv, (dims, ((), ())), preferred_element_type=F32, precision=precision)

        def finish(x):
            if res is not None:
                x = x + r_ref[...].astype(F32)
            if posted:
                h_ref, w_ref, add_ref = refs[n_in - ns - 3:n_in - ns]
                dh, dw = jax.vjp(_rms_core, h_ref[...], w_ref[...])[1](x)
                x = dh + add_ref[...]
                dw_ref = refs[n_in + 1]

                @pl.when(first)
                def _():
                    dw_ref[...] = dw

                @pl.when(jnp.logical_not(first))
                def _():
                    dw_ref[...] += dw

            o_ref[...] = x.astype(out_dtype).reshape(o_ref.shape)

        if nk == 1:
            finish(p)
        else:
            acc = refs[n_in + n_out]
            k = pl.program_id(2)

            @pl.when(k == 0)
            def _():
                acc[...] = p

            @pl.when(k > 0)
            def _():
                acc[...] += p

            @pl.when(k == nk - 1)
            def _():
                finish(acc[...])

    anywhere = pl.BlockSpec(memory_space=pl.ANY)
    ops = [a, b] + ([norm_w] if normed else []) + ([res] if res is not None else [])
    ops += ([into] if into is not None else []) + (list(post) if posted else []) + list(scatter)
    whole = lambda arr: pl.BlockSpec(arr.shape, lambda i, j, k: (0, 0))
    specs = [a_spec, b_spec] + ([whole(norm_w)] if normed else [])
    specs += [o_spec] if res is not None else []
    specs += [anywhere] if into is not None else []
    specs += ([o_spec, whole(post[1]), o_spec] if posted else []) + [anywhere] * ns
    out = pl.pallas_call(
        body, name=name, grid=grid, in_specs=specs,
        out_specs=[o_spec] + ([whole(post[1])] if posted else []) + [anywhere] * ns,
        out_shape=[jax.ShapeDtypeStruct(out_shape, out_dtype)]
        + ([jax.ShapeDtypeStruct(post[1].shape, F32)] if posted else []) + _scatter_shapes(scatter),
        input_output_aliases={2 + normed + (res is not None): 0} if into is not None else {},
        scratch_shapes=([pltpu.VMEM(acc_shape, F32)] if nk > 1 else [])
        + ([pltpu.SemaphoreType.DMA((3 * ns,)), pltpu.SemaphoreType.DMA((3 * ns,))] if ns else []),
        compiler_params=_params(("arbitrary",) * 3 if ns or posted else ("parallel", "parallel", "arbitrary")),
    )(*ops)
    return out if n_out > 1 else out[0]


NN = ((1,), (0,))
NT = ((1,), (1,))
TN = ((0,), (0,))


BIG_TILES = (1024, 512, 256, 128)


def _tile(n, pref):
    for t in pref:
        if n % t == 0:
            return t
    return n


def _rows_of(a):
    return a[0] if isinstance(a, tuple) else a


def _mm_nn(name, a, w, out_dtype, res=None, precision=None):
    M, K = _rows_of(a).shape
    N = w.shape[1]
    tm = _tile(M, BIG_TILES if K <= 2048 else BIG_TILES[1:])
    tn = _tile(N, BIG_TILES)
    return _mm(name, a, w, (M, N), out_dtype, (M // tm, N // tn, 1),
               pl.BlockSpec((tm, K), lambda i, j, k: (i, 0)), pl.BlockSpec((K, tn), lambda i, j, k: (0, j)),
               pl.BlockSpec((tm, tn), lambda i, j, k: (i, j)), NN, (tm, tn), res=res, precision=precision)


def _mm_nt(name, g, w, out_dtype, res=None, precision=None, scatter=(), post=None):
    M, N = g.shape
    K = w.shape[0]
    tm, tk = _tile(M, BIG_TILES if post is None else BIG_TILES[1:]), _tile(K, (1024, 1408, 512, 256, 128))
    tn = _tile(N, (1536,) + BIG_TILES)
    return _mm(name, g, w, (M, K), out_dtype, (M // tm, K // tk, N // tn),
               pl.BlockSpec((tm, tn), lambda i, j, k: (i, k)), pl.BlockSpec((tk, tn), lambda i, j, k: (j, k)),
               pl.BlockSpec((tm, tk), lambda i, j, k: (i, j)), NT, (tm, tk), res=res, precision=precision,
               scatter=scatter, post=post)


def _mm_tn(name, a, g, out_dtype=F32, precision=None):
    T, K = _rows_of(a).shape
    N = g.shape[1]
    tk, tn = _tile(K, (1024, 1408, 512, 256, 128)), _tile(N, BIG_TILES)
    assert tk == K or not isinstance(a, tuple)
    tt = _tile(T, BIG_TILES)
    return _mm(name, a, g, (K, N), out_dtype, (K // tk, N // tn, T // tt),
               pl.BlockSpec((tt, tk), lambda i, j, k: (k, i)), pl.BlockSpec((tt, tn), lambda i, j, k: (k, j)),
               pl.BlockSpec((tk, tn), lambda i, j, k: (i, j)), TN, (tk, tn), precision=precision)


def _mm_up(name, n, wup, layer):
    T = _rows_of(n).shape[0]
    tm = _tile(T, BIG_TILES)
    return _mm(name, n, wup, (T, DFF2), BF16, (T // tm, N_CHIPS, 1),
               pl.BlockSpec((tm, D), lambda i, j, k: (i, 0)),
               pl.BlockSpec((None, None, D, DFF2_SHARD), lambda i, j, k: (j, layer, 0, 0)),
               pl.BlockSpec((tm, DFF2_SHARD), lambda i, j, k: (i, j)), NN, (tm, DFF2_SHARD))


def _mm_up_nt(name, du, wup, layer, post):
    T = du.shape[0]
    tm, tk = _tile(T, BIG_TILES[1:]), D
    return _mm(name, du, wup, (T, D), F32, (T // tm, D // tk, N_CHIPS),
               pl.BlockSpec((tm, DFF2_SHARD), lambda i, j, k: (i, k)),
               pl.BlockSpec((None, None, tk, DFF2_SHARD), lambda i, j, k: (k, layer, j, 0)),
               pl.BlockSpec((tm, tk), lambda i, j, k: (i, j)), NT, (tm, tk), post=post)


def _mm_up_tn(name, n, du, layer, into):
    T = _rows_of(n).shape[0]
    tk, tt = D, _tile(T, BIG_TILES)
    return _mm(name, n, du, (N_CHIPS, 2, D, DFF2_SHARD), F32, (D // tk, N_CHIPS, T // tt),
               pl.BlockSpec((tt, tk), lambda i, j, k: (k, i)), pl.BlockSpec((tt, DFF2_SHARD), lambda i, j, k: (k, j)),
               pl.BlockSpec((None, None, tk, DFF2_SHARD), lambda i, j, k: (j, layer, i, 0)), TN, (tk, DFF2_SHARD),
               into=into)


def _mm_down_tn(name, act, dout, layer, into):
    T = act.shape[0]
    tk, tn, tt = 2 * DFF_SHARD, _tile(D, BIG_TILES), _tile(T, BIG_TILES)
    return _mm(name, act, dout, (2, 2, 2, DFF_SHARD, D), F32, (DFF // tk, D // tn, T // tt),
               pl.BlockSpec((tt, tk), lambda i, j, k: (k, i)), pl.BlockSpec((tt, tn), lambda i, j, k: (k, j)),
               pl.BlockSpec((None, 2, None, DFF_SHARD, tn), lambda i, j, k: (i, 0, layer, 0, j)), TN, (tk, tn),
               into=into)


def _sigmoid(x):
    return 0.5 * jnp.tanh(0.5 * x) + 0.5


def _silu(x):
    return x * _sigmoid(x)


def _softplus(x):
    return jnp.maximum(x, 0.0) + jnp.log(1.0 + jnp.exp(-jnp.abs(x)))


def _rms_core(h, w):
    return h * lax.rsqrt(jnp.mean(h * h, axis=-1, keepdims=True) + EPS) * w


def _shift_down(x, halo, s, i):
    if s == 0:
        return x
    tm = x.shape[0]
    rolled = pltpu.roll(x, s, 0)
    patch = pltpu.roll(jnp.where(i == 0, 0.0, halo), s, 0)
    row = lax.broadcasted_iota(jnp.int32, patch.shape, 0)
    top = jnp.where(row < s, patch, rolled[:SUBLANE])
    return jnp.concatenate([top, rolled[SUBLANE:]], axis=0) if tm > SUBLANE else top


def _shift_up(x, halo, s, i, n):
    if s == 0:
        return x
    tm = x.shape[0]
    rolled = pltpu.roll(x, tm - s, 0)
    patch = pltpu.roll(jnp.where(i == n - 1, 0.0, halo), SUBLANE - s, 0)
    row = lax.broadcasted_iota(jnp.int32, patch.shape, 0)
    bottom = jnp.where(row >= SUBLANE - s, patch, rolled[tm - SUBLANE:])
    return jnp.concatenate([rolled[:tm - SUBLANE], bottom], axis=0) if tm > SUBLANE else bottom


def _taps(x, halo, K, i):
    return [_shift_down(x, halo, K - 1 - j, i) for j in range(K)]


def _conv_fwd(taps, w):
    y = w[0:1, :] * taps[0]
    for j in range(1, len(taps)):
        y = y + w[j:j + 1, :] * taps[j]
    return y


def _conv_dx(dy, halo_next, w, i, n):
    K = w.shape[0]
    dx = w[K - 1:K, :] * dy
    for j in range(K - 1):
        dx = dx + w[j:j + 1, :] * _shift_up(dy, halo_next, K - 1 - j, i, n)
    return dx


def _conv_dw(dy, taps):
    rows = [jnp.sum(dy * tap, axis=0, keepdims=True) for tap in taps]
    return jnp.concatenate(rows + [jnp.zeros((SUBLANE - len(taps), dy.shape[1]), F32)], axis=0)


def _l2(x):
    return x * lax.rsqrt(jnp.sum(x * x, axis=-1, keepdims=True) + EPS)


def _gdn_post_core(yq, yk, yv, pb, pa, a_log, dtb):
    qn = tuple(_l2(_silu(a)) * (GDN_HD ** -0.5) for a in yq)
    kn = tuple(_l2(_silu(a)) for a in yk)
    v = _silu(yv)
    beta = _sigmoid(pb)
    g = -jnp.exp(a_log) * _softplus(pa + dtb)
    return qn, kn, v, beta, g


def _heads(x, n):
    return tuple(x[:, GDN_HD * h:GDN_HD * (h + 1)] for h in range(n))


def _gdn_pre_fwd(pm, pba, conv_w, a_log, dtb, tm=128):
    T = pm.shape[0]
    tm = min(tm, T)

    def fn(i, x, halo, pbav, cw, al, db):
        y = _conv_fwd(_taps(x.astype(F32), halo.astype(F32), 4, i), cw)
        qn, kn, v, beta, g = _gdn_post_core(_heads(y[:, :GDN_QK], 8), _heads(y[:, GDN_QK:2 * GDN_QK], 8),
                                            y[:, 2 * GDN_QK:], pbav[:, :LANE], pbav[:, LANE:], al, db)
        return jnp.stack(qn), jnp.stack(kn), jnp.stack(_heads(v, GDN_V_HEADS)), beta, g

    ins = [(pm, "row", (GDN_CONV, 0)), (pm, "prev", (GDN_CONV, 0)), (pba, "row", None),
           (conv_w, "full", None), (a_log, "full", None), (dtb, "full", None)]
    outs = [((GDN_QK_HEADS, T, GDN_HD), BF16, "row"), ((GDN_QK_HEADS, T, GDN_HD), BF16, "row"),
            ((GDN_V_HEADS, T, GDN_HD), BF16, "row"), ((T, LANE), F32, "row"), ((T, LANE), F32, "row")]
    return _rowcall("gdn_pre_fwd", fn, T, tm, ins, outs)


def _gdn_pre_bwd(pm, pba, conv_w, a_log, dtb, dqn, dkn, dv, dbeta, dg, tm=128):
    T = pm.shape[0]
    tm = min(tm, T)

    def fn(i, x, halo, pbav, cw, al, db, dqv, dkv, dvv, dbv, dgv):
        taps = _taps(x.astype(F32), halo.astype(F32), 4, i)
        y = _conv_fwd(taps, cw)
        prim = (_heads(y[:, :GDN_QK], 8), _heads(y[:, GDN_QK:2 * GDN_QK], 8), y[:, 2 * GDN_QK:],
                pbav[:, :LANE], pbav[:, LANE:], al, db)
        _, vjp = jax.vjp(_gdn_post_core, *prim)
        cot = (tuple(dqv[h].astype(F32) for h in range(8)), tuple(dkv[h].astype(F32) for h in range(8)),
               jnp.concatenate([dvv[h].astype(F32) for h in range(GDN_V_HEADS)], axis=1), dbv, dgv)
        dyq, dyk, dyv, dpb, dpa, dal, ddb = vjp(cot)
        dy = jnp.concatenate(list(dyq) + list(dyk) + [dyv], axis=1)
        dcw = _conv_dw(dy, taps)
        return dy, jnp.concatenate([dpb, dpa], axis=1), dcw, dal, ddb

    ins = [(pm, "row", (GDN_CONV, 0)), (pm, "prev", (GDN_CONV, 0)), (pba, "row", None),
           (conv_w, "full", None), (a_log, "full", None), (dtb, "full", None),
           (dqn, "row", None), (dkn, "row", None), (dv, "row", None), (dbeta, "row", None), (dg, "row", None)]
    outs = [((T, GDN_CONV), BF16, "row"), ((T, 2 * LANE), F32, "row"), ((SUBLANE, GDN_CONV), F32, "acc"),
            ((1, LANE), F32, "acc"), ((1, LANE), F32, "acc")]
    return _rowcall("gdn_pre_bwd", fn, T, tm, ins, outs)


def _gdn_conv_bwd(dy, dz, conv_w, tm=256):
    T = dy.shape[0]
    tm = min(tm, T)
    n = T // tm

    def fn(i, dyv, halo, dzv, cw):
        dx = _conv_dx(dyv.astype(F32), halo.astype(F32), cw, i, n)
        return (jnp.concatenate([dx.astype(BF16), dzv.astype(BF16)], axis=1),)

    ins = [(dy, "row", None), (dy, "next", None), (dz, "row", None), (conv_w, "full", None)]
    return _rowcall("gdn_conv_bwd", fn, T, tm, ins, [((T, GDN_MAIN), BF16, "row")])[0]


def _bdot(a, b, dims=NN):
    return lax.dot_general(a.astype(BF16), b.astype(BF16), (dims, ((), ())), preferred_element_type=F32)


BNN = ((2,), (1,))
BNT = ((2,), (2,))
BTN = ((1,), (1,))


def _bmm(a, b, dims=BNN):
    return lax.dot_general(a.astype(BF16), b.astype(BF16), (dims, ((0,), (0,))), preferred_element_type=F32)


def _bmm3(a, b):
    ah, bh = a.astype(BF16), b.astype(BF16)
    al, bl = (a - ah.astype(F32)).astype(BF16), (b - bh.astype(F32)).astype(BF16)
    dn = (BNN, ((0,), (0,)))
    return (lax.dot_general(ah, bh, dn, preferred_element_type=F32)
            + lax.dot_general(al, bh, dn, preferred_element_type=F32)
            + lax.dot_general(ah, bl, dn, preferred_element_type=F32))


def _tri_inv(m):
    C = m.shape[-1]
    r = lax.broadcasted_iota(jnp.int32, (C, C), 0)
    c = lax.broadcasted_iota(jnp.int32, (C, C), 1)
    t = jnp.where(r == c, 1.0, 0.0) - m
    pw = _bmm3(m, m)
    t = t + _bmm3(t, pw)
    for _ in range(int(math.log2(C)) - 2):
        pw = _bmm(pw, pw)
        t = t + _bmm(t, pw)
    return t


def _tri_inv_vjp(t, dt):
    tt = jnp.swapaxes(t, 1, 2)
    return -_bmm(_bmm(tt, dt), tt)


def _twice(a):
    return jnp.broadcast_to(a[:, None], (a.shape[0], 2) + a.shape[1:]).reshape((2 * a.shape[0],) + a.shape[1:])


def _gdn_gates(grow, brow):
    C = grow.shape[2]
    r = lax.broadcasted_iota(jnp.int32, (C, C), 0)
    c = lax.broadcasted_iota(jnp.int32, (C, C), 1)
    tril, eye = r >= c, r == c
    gcol = jnp.sum(jnp.where(eye, grow, 0.0), axis=2, keepdims=True)
    bcol = jnp.sum(jnp.where(eye, brow, 0.0), axis=2, keepdims=True)
    gc_col = jnp.sum(jnp.where(tril, grow, 0.0), axis=2, keepdims=True)
    gc_row = jnp.sum(jnp.where(r <= c, gcol, 0.0), axis=1, keepdims=True)
    gc_last = jnp.sum(grow, axis=2, keepdims=True)
    decay = jnp.where(tril, jnp.exp(jnp.where(tril, gc_col - gc_row, 0.0)), 0.0)
    return bcol, gc_col, gc_last, decay


def _gdn_m(k, bcol, decay):
    C = k.shape[1]
    strict = lax.broadcasted_iota(jnp.int32, (C, C), 0) > lax.broadcasted_iota(jnp.int32, (C, C), 1)
    return jnp.where(strict, bcol * _twice(_bmm(k, k, BNT)) * decay, 0.0)


def _gdn_rest(q, k, v, bcol, gc_col, gc_last, decay, t_mat, S):
    qk = _twice(_bmm(q, k, BNT))
    k2, q2 = _twice(k), _twice(q)
    egc = jnp.exp(gc_col)
    u = _bmm(t_mat, v * bcol)
    w = _bmm(t_mat, k2 * (bcol * egc))
    v_new = u - _bmm(w, S)
    o = _bmm(q2 * egc, S) + _bmm(qk * decay, v_new)
    s_new = S * jnp.exp(gc_last) + _bmm(k2 * jnp.exp(gc_last - gc_col), v_new, BTN)
    return o, s_new


def _gdn_tb(T):
    return min(256, T)


def _gate_rows(g):
    T = g.shape[0]
    g = g[:, :GDN_V_HEADS].reshape(T // GDN_CHUNK, GDN_CHUNK, GDN_V_HEADS)
    return g.transpose(0, 2, 1)[:, :, None, :]


def _gate_cols(g):
    nc = g.shape[0]
    g = g[:, :, 0, :].transpose(0, 2, 1).reshape(nc * GDN_CHUNK, GDN_V_HEADS)
    return jnp.pad(g, ((0, 0), (0, LANE - GDN_V_HEADS)))


def _gdn_fwd(qn, kn, v, g, beta, gather=None):
    T = qn.shape[1]
    tb = _gdn_tb(T)
    nc = tb // GDN_CHUNK
    nsteps = T // tb
    quarters, buffers = gather if gather is not None else ((), ())
    ng = len(quarters)
    shapes = [a.shape for a in quarters]
    splits = [True] * ng

    def body(*refs):
        q_ref, k_ref, v_ref, g_ref, b_ref = refs[:5]
        src = refs[5:5 + ng]
        o_ref, sall_ref, tall_ref = refs[5 + 2 * ng:8 + 2 * ng]
        dst = refs[8 + 2 * ng:8 + 3 * ng]
        s_scr = refs[8 + 3 * ng]
        step = pl.program_id(0)

        @pl.when(step == 0)
        def _():
            s_scr[...] = jnp.zeros(s_scr.shape, F32)
            if ng:
                for cp in _gather_copies(shapes, splits, src, dst, *refs[9 + 3 * ng:])[0]:
                    cp.start()

        def chunk(ci, carry):
            rows = pl.ds(pl.multiple_of(ci * GDN_CHUNK, GDN_CHUNK), GDN_CHUNK)
            s = s_scr[...]
            sall_ref[ci] = s
            q, k = q_ref[:, rows, :].astype(F32), k_ref[:, rows, :].astype(F32)
            bcol, gc_col, gc_last, decay = _gdn_gates(g_ref[ci], b_ref[ci])
            t_mat = _tri_inv(_gdn_m(k, bcol, decay)).astype(BF16)
            tall_ref[ci] = t_mat
            o, s_new = _gdn_rest(q, k, v_ref[:, rows, :].astype(F32), bcol, gc_col, gc_last, decay,
                                 t_mat.astype(F32), s)
            o_ref[:, rows, :] = o.astype(o_ref.dtype)
            s_scr[...] = s_new
            return carry

        lax.fori_loop(0, nc, chunk, 0)

        if ng:
            @pl.when(step == nsteps - 1)
            def _():
                _gather_arrival(shapes, splits, src, dst, *refs[9 + 3 * ng:])

    qk_spec = pl.BlockSpec((GDN_QK_HEADS, tb, GDN_HD), lambda i: (0, i, 0))
    v_spec = pl.BlockSpec((GDN_V_HEADS, tb, GDN_HD), lambda i: (0, i, 0))
    g_spec = pl.BlockSpec((nc, GDN_V_HEADS, 1, GDN_CHUNK), lambda i: (i, 0, 0, 0))
    anywhere = pl.BlockSpec(memory_space=pl.ANY)
    return pl.pallas_call(
        body, name="gdn_fwd", grid=(nsteps,),
        in_specs=[qk_spec, qk_spec, v_spec, g_spec, g_spec] + [anywhere] * (2 * ng),
        out_specs=[v_spec, pl.BlockSpec((nc, GDN_V_HEADS, GDN_HD, GDN_HD), lambda i: (i, 0, 0, 0)),
                   pl.BlockSpec((nc, GDN_V_HEADS, GDN_CHUNK, GDN_CHUNK), lambda i: (i, 0, 0, 0))] + [anywhere] * ng,
        out_shape=[jax.ShapeDtypeStruct((GDN_V_HEADS, T, GDN_HD), BF16),
                   jax.ShapeDtypeStruct((T // GDN_CHUNK, GDN_V_HEADS, GDN_HD, GDN_HD), F32),
                   jax.ShapeDtypeStruct((T // GDN_CHUNK, GDN_V_HEADS, GDN_CHUNK, GDN_CHUNK), BF16)]
        + [jax.ShapeDtypeStruct(b.shape, b.dtype) for b in buffers],
        input_output_aliases={5 + ng + a: 3 + a for a in range(ng)},
        scratch_shapes=[pltpu.VMEM((GDN_V_HEADS, GDN_HD, GDN_HD), F32)]
        + ([pltpu.SemaphoreType.DMA((6 * ng,)), pltpu.SemaphoreType.DMA((6 * ng,))] if ng else []),
        compiler_params=_params(("arbitrary",)),
    )(qn, kn, v, g, beta, *quarters, *buffers)


def _gdn_bwd(qn, kn, v, g, beta, sall, tall, do, scatter=()):
    T = qn.shape[1]
    tb = _gdn_tb(T)
    nc = tb // GDN_CHUNK
    nb = T // tb
    ns = len(scatter)

    def body(*refs):
        q_ref, k_ref, v_ref, g_ref, b_ref, sall_ref, tall_ref, do_ref = refs[:8]
        dq_ref, dk_ref, dv_ref, dg_ref, db_ref = refs[8 + ns:13 + ns]
        ds_scr = refs[13 + 2 * ns]
        comm = (refs[8:8 + ns], refs[13 + ns:13 + 2 * ns], *refs[14 + 2 * ns:])
        step = pl.program_id(0)

        @pl.when(step == 0)
        def _():
            ds_scr[...] = jnp.zeros(ds_scr.shape, F32)
            if ns:
                for cp in _scatter_copies(*comm):
                    cp.start()

        def chunk(cr, carry):
            ci = nc - 1 - cr
            rows = pl.ds(pl.multiple_of(ci * GDN_CHUNK, GDN_CHUNK), GDN_CHUNK)
            k, t_mat = k_ref[:, rows, :].astype(F32), tall_ref[ci].astype(F32)
            (bcol, gc_col, gc_last, decay), vjp_gates = jax.vjp(_gdn_gates, g_ref[ci], b_ref[ci])
            _, vjp = jax.vjp(_gdn_rest, q_ref[:, rows, :].astype(F32), k, v_ref[:, rows, :].astype(F32),
                             bcol, gc_col, gc_last, decay, t_mat, sall_ref[ci])
            dq, dk, dv, dbcol, dgc_col, dgc_last, ddecay, dt, ds = vjp((do_ref[:, rows, :].astype(F32), ds_scr[...]))
            _, vjp_m = jax.vjp(_gdn_m, k, bcol, decay)
            dk_m, dbcol_m, ddecay_m = vjp_m(_tri_inv_vjp(t_mat, dt))
            dg, db = vjp_gates((dbcol + dbcol_m, dgc_col, dgc_last, ddecay + ddecay_m))
            ds_scr[...] = ds
            dq_ref[:, rows, :] = dq
            dk_ref[:, rows, :] = dk + dk_m
            dv_ref[:, rows, :] = dv
            dg_ref[ci] = dg
            db_ref[ci] = db
            return carry

        lax.fori_loop(0, nc, chunk, 0)

        if ns:
            @pl.when(step == nb - 1)
            def _():
                copies = _scatter_copies(*comm)
                for cp in copies:
                    cp.wait_recv()
                for cp in copies:
                    cp.wait_send()

    qk_spec = pl.BlockSpec((GDN_QK_HEADS, tb, GDN_HD), lambda i: (0, nb - 1 - i, 0))
    v_spec = pl.BlockSpec((GDN_V_HEADS, tb, GDN_HD), lambda i: (0, nb - 1 - i, 0))
    g_spec = pl.BlockSpec((nc, GDN_V_HEADS, 1, GDN_CHUNK), lambda i: (nb - 1 - i, 0, 0, 0))
    s_spec = pl.BlockSpec((nc, GDN_V_HEADS, GDN_HD, GDN_HD), lambda i: (nb - 1 - i, 0, 0, 0))
    t_spec = pl.BlockSpec((nc, GDN_V_HEADS, GDN_CHUNK, GDN_CHUNK), lambda i: (nb - 1 - i, 0, 0, 0))
    anywhere = pl.BlockSpec(memory_space=pl.ANY)
    return pl.pallas_call(
        body, name="gdn_bwd", grid=(nb,),
        in_specs=[qk_spec, qk_spec, v_spec, g_spec, g_spec, s_spec, t_spec, v_spec] + [anywhere] * ns,
        out_specs=[qk_spec, qk_spec, v_spec, g_spec, g_spec] + [anywhere] * ns,
        out_shape=[jax.ShapeDtypeStruct((GDN_QK_HEADS, T, GDN_HD), F32),
                   jax.ShapeDtypeStruct((GDN_QK_HEADS, T, GDN_HD), F32),
                   jax.ShapeDtypeStruct((GDN_V_HEADS, T, GDN_HD), F32),
                   jax.ShapeDtypeStruct(g.shape, F32), jax.ShapeDtypeStruct(g.shape, F32)]
        + _scatter_shapes(scatter),
        scratch_shapes=[pltpu.VMEM((GDN_V_HEADS, GDN_HD, GDN_HD), F32)]
        + ([pltpu.SemaphoreType.DMA((3 * ns,)), pltpu.SemaphoreType.DMA((3 * ns,))] if ns else []),
        compiler_params=_params(("arbitrary",)),
    )(qn, kn, v, g, beta, sall, tall, do, *scatter)


def _gnorm_core(o, z, w):
    return tuple(_rms_core(oh, w) * _silu(zh) for oh, zh in zip(o, z))


def _gnorm_fwd(o, pm, w, tm=256):
    T = pm.shape[0]
    tm = min(tm, T)

    def fn(i, ov, zv, wv):
        zf = zv.astype(F32)
        out = _gnorm_core(tuple(ov[h].astype(F32) for h in range(GDN_V_HEADS)), _heads(zf, GDN_V_HEADS), wv)
        return (jnp.concatenate(out, axis=1),)

    ins = [(o, "row", None), (pm, "row", (GDN_V, 2)), (w, "full", None)]
    return _rowcall("gnorm_fwd", fn, T, tm, ins, [((T, GDN_V), BF16, "row")])[0]


def _gnorm_bwd(o, pm, w, don, tm=128, swap=()):
    T = pm.shape[0]
    tm = min(tm, T)

    def fn(i, ov, zv, wv, dv):
        zf, df = zv.astype(F32), dv.astype(F32)
        _, vjp = jax.vjp(_gnorm_core, tuple(ov[h].astype(F32) for h in range(GDN_V_HEADS)),
                         _heads(zf, GDN_V_HEADS), wv)
        do, dz, dw = vjp(_heads(df, GDN_V_HEADS))
        return jnp.stack(do), jnp.concatenate(dz, axis=1), dw

    ins = [(o, "row", None), (pm, "row", (GDN_V, 2)), (w, "full", None), (don, "row", None)]
    outs = [((GDN_V_HEADS, T, GDN_HD), BF16, "row"), ((T, GDN_V), BF16, "row"), ((1, GDN_HD), F32, "acc")]
    return _rowcall("gnorm_bwd", fn, T, tm, ins, outs, swap=swap)


def _ffn_act_fwd(name, up, conv_w, conv_b, tm=256):
    T = up.shape[0]
    tm = min(tm, T)
    r8 = tm // SUBLANE
    chunk = DFF // 2

    def body(x_ref, h_ref, cw_ref, cb_ref, o_ref):
        i = pl.program_id(0)
        for c in range(DFF // chunk):
            halves = []
            for base in (c * chunk, DFF + c * chunk):
                cols = slice(base, base + chunk)
                taps = _taps(x_ref[:, cols].astype(F32), h_ref[:, cols].astype(F32), 3, i)
                halves.append(_conv_fwd(taps, cw_ref[:, cols]) + cb_ref[:, cols])
            o_ref[:, c * chunk:(c + 1) * chunk] = (_silu(halves[0]) * halves[1]).astype(o_ref.dtype)

    return pl.pallas_call(
        body, name=name, grid=(T // tm,),
        in_specs=[pl.BlockSpec((tm, DFF2), lambda i: (i, 0)),
                  pl.BlockSpec((SUBLANE, DFF2), lambda i: (jnp.maximum(i * r8 - 1, 0), 0)),
                  pl.BlockSpec(conv_w.shape, lambda i: (0, 0)), pl.BlockSpec(conv_b.shape, lambda i: (0, 0))],
        out_specs=pl.BlockSpec((tm, DFF), lambda i: (i, 0)), out_shape=jax.ShapeDtypeStruct((T, DFF), BF16),
        compiler_params=_params(("arbitrary",)),
    )(up, up, conv_w, conv_b)


def _ffn_act_bwd(name, up, conv_w, conv_b, dact, tm=256):
    T = up.shape[0]
    tm = min(tm, T)

    def fn(i, x, halo, cw, cb, da):
        taps = _taps(x.astype(F32), halo.astype(F32), 3, i)
        da = da.astype(F32)
        u = _conv_fwd(taps, cw) + cb
        gate, val = u[:, :DFF], u[:, DFF:]
        sg = _sigmoid(gate)
        dgate = da * val * sg * (1.0 + gate * (1.0 - sg))
        dval = da * gate * sg
        du = jnp.concatenate([dgate, dval], axis=1)
        return du, _conv_dw(du, taps), jnp.sum(du, axis=0, keepdims=True)

    ins = [(up, "row", None), (up, "prev", None), (conv_w, "full", None), (conv_b, "full", None),
           (dact, "row", None)]
    outs = [((T, DFF2), BF16, "row"), ((SUBLANE, DFF2), F32, "acc"), ((1, DFF2), F32, "acc")]
    return _rowcall(name, fn, T, tm, ins, outs)


def _ffn_conv_bwd(name, du, conv_w, tm=256):
    T = du.shape[0]
    tm = min(tm, T)
    n = T // tm

    def fn(i, dv, halo, cw):
        return (_conv_dx(dv.astype(F32), halo.astype(F32), cw, i, n),)

    ins = [(du, "row", None), (du, "next", None), (conv_w, "full", None)]
    return _rowcall(name, fn, T, tm, ins, [((T, DFF2), BF16, "row")])[0]


GROUP_ROWS = SWA_GROUP * SWA_BLOCK


def _attn_core(q, kp, kc, vp, vc, bias, sink, mask):
    kcat = jnp.concatenate([kp, kc], axis=0)
    vcat = jnp.concatenate([vp, vc], axis=0)
    s = _bdot(q * (SWA_HD ** -0.5), kcat, NT) + bias
    s = jnp.where(mask, s, NEG_INF)
    m = lax.stop_gradient(jnp.maximum(jnp.max(s, axis=-1, keepdims=True), sink))
    p = jnp.exp(s - m)
    denom = jnp.sum(p, axis=-1, keepdims=True) + jnp.exp(sink - m)
    return _bdot(p / denom, vcat)


def _attn_mask(i):
    qi = lax.broadcasted_iota(jnp.int32, (GROUP_ROWS, 2 * SWA_BLOCK), 0) & (SWA_BLOCK - 1)
    ki = lax.broadcasted_iota(jnp.int32, (GROUP_ROWS, 2 * SWA_BLOCK), 1)
    dist = qi + SWA_BLOCK - ki
    return (dist >= 0) & (dist < SWA_BLOCK) & ((ki >= SWA_BLOCK) | (i > 0))


def _head_cols(h):
    return slice(h * SWA_HD, (h + 1) * SWA_HD)


def _block_rows(b):
    return slice(b * SWA_BLOCK, (b + 1) * SWA_BLOCK)


def _stacked_heads(ref, rows, j):
    return jnp.concatenate([ref[rows, _head_cols(SWA_GROUP * j + g)].astype(F32) for g in range(SWA_GROUP)], axis=0)


def _store_heads(ref, rows, j, stacked):
    for g in range(SWA_GROUP):
        ref[rows, _head_cols(SWA_GROUP * j + g)] = stacked[g * SWA_BLOCK:(g + 1) * SWA_BLOCK].astype(ref.dtype)


def _attn_chains(i, nblk, q_ref, kvc_ref, kvp_ref, b_ref, s_ref):
    chains = []
    for b in range(nblk):
        rows = _block_rows(b)
        mask = _attn_mask(i) if b == 0 else _attn_mask(1)
        before, before_rows = (kvp_ref, _block_rows(0)) if b == 0 else (kvc_ref, _block_rows(b - 1))
        for j in range(SWA_KV_HEADS):
            heads = slice(SWA_GROUP * j, SWA_GROUP * (j + 1))
            k_cols, v_cols = _head_cols(j), _head_cols(SWA_KV_HEADS + j)
            sink = jnp.concatenate(
                [jnp.broadcast_to(s_ref[j, g:g + 1, 0:1], (SWA_BLOCK, 1)) for g in range(SWA_GROUP)], axis=0)
            ops = (_stacked_heads(q_ref, rows, j), before[before_rows, k_cols].astype(F32),
                   kvc_ref[rows, k_cols].astype(F32), before[before_rows, v_cols].astype(F32),
                   kvc_ref[rows, v_cols].astype(F32), b_ref[heads].reshape(GROUP_ROWS, 2 * SWA_BLOCK), sink)
            chains.append((rows, j, ops, mask))
    return chains


def _attn_fwd_flat(q, kv, bias, sinks):
    T = q.shape[0]
    nblk = _tile(T // SWA_BLOCK, (4, 2, 1))
    rows = nblk * SWA_BLOCK

    def body(q_ref, kvc_ref, kvp_ref, b_ref, s_ref, o_ref):
        chains = _attn_chains(pl.program_id(0), nblk, q_ref, kvc_ref, kvp_ref, b_ref, s_ref)
        outs = [_attn_core(*ops, mask) for _, _, ops, mask in chains]
        for (blk, j, _, _), out in zip(chains, outs):
            _store_heads(o_ref, blk, j, out)

    q_spec = pl.BlockSpec((rows, q.shape[1]), lambda i: (i, 0))
    cur = pl.BlockSpec((rows, kv.shape[1]), lambda i: (i, 0))
    prev = pl.BlockSpec((SWA_BLOCK, kv.shape[1]), lambda i: (jnp.maximum(nblk * i - 1, 0), 0))
    return pl.pallas_call(
        body, name="attn_fwd", grid=(T // rows,),
        in_specs=[q_spec, cur, prev, pl.BlockSpec(bias.shape, lambda i: (0, 0, 0)),
                  pl.BlockSpec(sinks.shape, lambda i: (0, 0, 0))],
        out_specs=q_spec, out_shape=jax.ShapeDtypeStruct(q.shape, BF16),
        compiler_params=_params(("arbitrary",)),
    )(q, kv, kv, bias, sinks)


def _attn_bwd_flat(q, kv, bias, sinks, do):
    T = q.shape[0]
    nb = T // SWA_BLOCK
    rows = _block_rows(0)

    def body(q_ref, kvc_ref, kvp_ref, b_ref, s_ref, do_ref, dq_ref, dkv_ref, db_ref, dsk_ref, carry):
        i = pl.program_id(0)

        @pl.when(i < nb)
        def _():
            chains = _attn_chains(i, 1, q_ref, kvc_ref, kvp_ref, b_ref, s_ref)
            cots = [_stacked_heads(do_ref, rows, j) for _, j, _, _ in chains]
            grads = [jax.vjp(functools.partial(_attn_core, mask=mask), *ops)[1](cot)
                     for (_, _, ops, mask), cot in zip(chains, cots)]
            for j, (dq, dkp, dkc, dvp, dvc, db, dsc) in enumerate(grads):
                heads = slice(SWA_GROUP * j, SWA_GROUP * (j + 1))
                k_cols, v_cols = _head_cols(j), _head_cols(SWA_KV_HEADS + j)
                _store_heads(dq_ref, rows, j, dq)
                db = db.reshape(SWA_GROUP, SWA_BLOCK, 2 * SWA_BLOCK)
                dsk = jnp.concatenate(
                    [jnp.broadcast_to(jnp.sum(dsc[g * SWA_BLOCK:(g + 1) * SWA_BLOCK], axis=0, keepdims=True),
                                      (1, LANE)) for g in range(SWA_GROUP)], axis=0)

                @pl.when(i == 0)
                def _():
                    db_ref[heads] = db
                    dsk_ref[j] = dsk

                @pl.when(i > 0)
                def _():
                    db_ref[heads] += db
                    dsk_ref[j] += dsk
                    dkv_ref[:, k_cols] = (carry[:, k_cols] + dkp).astype(dkv_ref.dtype)
                    dkv_ref[:, v_cols] = (carry[:, v_cols] + dvp).astype(dkv_ref.dtype)

                carry[:, k_cols] = dkc
                carry[:, v_cols] = dvc

        @pl.when(i == nb)
        def _():
            dkv_ref[...] = carry[...].astype(dkv_ref.dtype)

    last = nb - 1
    q_spec = pl.BlockSpec((SWA_BLOCK, q.shape[1]), lambda i: (jnp.minimum(i, last), 0))
    cur = pl.BlockSpec((SWA_BLOCK, kv.shape[1]), lambda i: (jnp.minimum(i, last), 0))
    prev = pl.BlockSpec((SWA_BLOCK, kv.shape[1]), lambda i: (jnp.clip(i - 1, 0, last), 0))
    b_spec = pl.BlockSpec(bias.shape, lambda i: (0, 0, 0))
    s_spec = pl.BlockSpec(sinks.shape, lambda i: (0, 0, 0))
    return pl.pallas_call(
        body, name="attn_bwd", grid=(nb + 1,),
        in_specs=[q_spec, cur, prev, b_spec, s_spec, q_spec],
        out_specs=[q_spec, prev, b_spec, s_spec],
        out_shape=[jax.ShapeDtypeStruct(q.shape, BF16), jax.ShapeDtypeStruct(kv.shape, BF16),
                   jax.ShapeDtypeStruct(bias.shape, F32), jax.ShapeDtypeStruct(sinks.shape, F32)],
        scratch_shapes=[pltpu.VMEM((SWA_BLOCK, kv.shape[1]), F32)],
        compiler_params=_params(("arbitrary",)),
    )(q, kv, kv, bias, sinks, do)


def _rel_onehot():
    qi = jnp.arange(SWA_BLOCK)[:, None]
    ki = jnp.arange(2 * SWA_BLOCK)[None, :]
    n = jnp.maximum(qi + SWA_BLOCK - ki, 0)
    max_exact = REL_BUCKETS // 2
    nf = jnp.maximum(n, 1).astype(F32)
    large = max_exact + (jnp.log(nf / max_exact) / math.log(REL_MAX_DISTANCE / max_exact)
                         * (REL_BUCKETS - max_exact)).astype(jnp.int32)
    bucket = jnp.where(n < max_exact, n, jnp.minimum(large, REL_BUCKETS - 1)).reshape(-1)
    return (bucket[None, :] == jnp.arange(REL_BUCKETS)[:, None]).astype(F32)


def _final(h, w, target, tm=256):
    T = h.shape[0]
    tm = min(tm, T)

    def fn(i, hv, wv, tv):
        y, vjp = jax.vjp(_rms_core, hv, wv)
        err = y - tv
        dh, dw = vjp(err * (1.0 / D))
        part = 0.5 * jnp.sum(jnp.sum(err * err, axis=1, keepdims=True) * (1.0 / D), axis=0, keepdims=True)
        return jnp.broadcast_to(part, (SUBLANE, LANE)), dh, dw

    ins = [(h, "row", None), (w, "full", None), (target, "row", None)]
    outs = [((SUBLANE, LANE), F32, "acc"), ((T, D), F32, "row"), ((1, D), F32, "acc")]
    return _rowcall("final", fn, T, tm, ins, outs)


def _ffn_fwd(tag, h, P, layer):
    n = (h, P["ffn_norm_w"][layer:layer + 1])
    up = _mm_up(f"{tag}_up", n, P["w_up"], layer)
    act = _ffn_act_fwd(f"{tag}_act", up, P["ffn_conv_w"][layer], P["ffn_conv_b"][layer:layer + 1])
    out = _mm_nn(f"{tag}_down", act, P["w_down"][layer], F32, res=h)
    return out, (n, up, act)


def _ffn_bwd(tag, h, saved, dout, P, layer, into=(None, None)):
    n, up, act = saved
    cw, cb = P["ffn_conv_w"][layer], P["ffn_conv_b"][layer:layer + 1]
    dact = _mm_nt(f"{tag}_down_dx", dout, P["w_down"][layer], BF16)
    g_down = _mm_down_tn(f"{tag}_down_dw", act, dout, layer, into[1])
    du, dcw, dcb = _ffn_act_bwd(f"{tag}_act_bwd", up, cw, cb, dact)
    dup = _ffn_conv_bwd(f"{tag}_conv_bwd", du, cw)
    g_up = _mm_up_tn(f"{tag}_up_dw", n, dup, layer, into[0])
    dh, dnw = _mm_up_nt(f"{tag}_up_dx", dup, P["w_up"], layer, post=(h, P["ffn_norm_w"][layer:layer + 1], dout))
    return dh, dict(w_down=g_down, w_up=g_up, conv_w=dcw[:3], conv_b=dcb, norm_w=dnw)


def _local_step(x, target, P, late=None, pair_sums=None):
    T = x.shape[0]
    n0 = (x, P["a_norm_w"])
    pm = _mm_nn("gdn_in", n0, P["w_in_main"], BF16)
    pba = _mm_nn("gdn_in_ba", n0, P["w_in_ba"], F32)
    qn, kn, v, beta, g = _gdn_pre_fwd(pm, pba, P["a_conv_w"], P["a_log"], P["dt_bias"])
    g_rows, beta_rows = _gate_rows(g), _gate_rows(beta)
    o, sall, tall, *gathered = _gdn_fwd(qn, kn, v, g_rows, beta_rows, gather=late)
    if late is not None:
        P = {**P, **_late_weights(gathered)}
    on = _gnorm_fwd(o, pm, P["a_out_norm_w"])
    h1 = _mm_nn("gdn_out", on, P["w_out"], F32, res=x)
    h2, ffn0 = _ffn_fwd("ffn0", h1, P, 0)
    nkv = (h2, P["kv_norm_w"])
    kv = _mm_nn("kv_proj", nkv, P["w_kv"], BF16)
    nb = (h2, P["b_norm_w"])
    qp = _mm_nn("q_proj", nb, P["w_q"], BF16)
    onehot = _rel_onehot()
    bias = _mm_nn("rel_bias", P["rel_table_t"], onehot, F32, precision=HIGHEST)
    bias = bias.reshape(SWA_Q_HEADS, SWA_BLOCK, 2 * SWA_BLOCK)
    oa = _attn_fwd_flat(qp, kv, bias, P["sinks"])
    h3 = _mm_nn("o_proj", oa, P["w_o"], F32, res=h2)
    h4, ffn1 = _ffn_fwd("ffn1", h3, P, 1)
    loss, dh4, d_final = _final(h4, P["final_norm_w"], target)

    dh3, gf1 = _ffn_bwd("ffn1", h3, ffn1, dh4, P, 1)
    doa = _mm_nt("o_proj_dx", dh3, P["w_o"], BF16)
    g_wo = _mm_tn("o_proj_dw", oa, dh3)
    dqp, dkv, dbias, dsinks = _attn_bwd_flat(qp, kv, bias, P["sinks"], doa)
    g_wq = _mm_tn("q_proj_dw", nb, dqp)
    g_wkv = _mm_tn("kv_proj_dw", nkv, dkv)
    dh2, d_bnorm = _mm_nt("q_proj_dx", dqp, P["w_q"], F32, post=(h2, P["b_norm_w"], dh3))
    dh2, d_kvnorm = _mm_nt("kv_proj_dx", dkv, P["w_kv"], F32, post=(h2, P["kv_norm_w"], dh2))
    g_table = _mm_nt("rel_bias_dw", onehot, dbias.reshape(SWA_Q_HEADS, -1), F32, precision=HIGHEST)
    dh1, gf0 = _ffn_bwd("ffn0", h1, ffn0, dh2, P, 0, into=(gf1["w_up"], gf1["w_down"]))
    don = _mm_nt("gdn_out_dx", dh1, P["w_out"], BF16)
    g_wout = _mm_tn("gdn_out_dw", on, dh1)
    ready = dict(a_w_out=g_wout, w_kv=g_wkv, b_w_q=g_wq, b_w_o=g_wo, ffn_w_up=gf0["w_up"], ffn_w_down=gf0["w_down"])
    names = [n for n in BIG if n in ready]
    whole = [_chip_major(n, ready[n]) for n in names] if pair_sums is not None else []
    do, dz, d_gnorm, *other = _gnorm_bwd(o, pm, P["a_out_norm_w"], don, swap=whole)
    pairs = pair_sums(names, whole, other) if pair_sums is not None else []
    dq, dk, dv, dg, dbeta, *parts = _gdn_bwd(qn, kn, v, g_rows, beta_rows, sall, tall, do, scatter=pairs)
    dy, dpba, d_aconv, d_alog, d_dtb = _gdn_pre_bwd(pm, pba, P["a_conv_w"], P["a_log"], P["dt_bias"],
                                                    dq, dk, dv, _gate_cols(dbeta), _gate_cols(dg))
    dpm = _gdn_conv_bwd(dy, dz, P["a_conv_w"])
    g_win_main = _mm_tn("gdn_in_dw", n0, dpm)
    g_win_ba = _mm_tn("gdn_in_ba_dw", n0, dpba)
    nh = GDN_V_HEADS
    g_win = jnp.concatenate([g_win_main, g_win_ba[:, :nh], g_win_ba[:, LANE:LANE + nh]], axis=1)
    last_whole = [_chip_major("a_w_in", g_win)]
    last_pair = pair_sums(["a_w_in"], last_whole, _pair_swap(last_whole, "late")) if pair_sums is not None else []
    dn0 = _mm_nt("gdn_in_dx", dpm, P["w_in_main"], F32, scatter=last_pair)
    dn0, last_parts = (dn0[0], dn0[1:]) if last_pair else (dn0, [])
    dx, d_anorm = _mm_nt("gdn_in_ba_dx", dpba, P["w_in_ba"], F32, res=dn0, post=(x, P["a_norm_w"], dh1))

    nh = GDN_V_HEADS
    grads = dict(
        a_norm_w=d_anorm,
        a_w_in=g_win,
        a_conv_w=d_aconv[:4], a_a_log=d_alog[:, :nh], a_dt_bias=d_dtb[:, :nh], a_out_norm_w=d_gnorm,
        a_w_out=g_wout, kv_norm_w=d_kvnorm, w_kv=g_wkv, b_norm_w=d_bnorm, b_w_q=g_wq,
        b_sinks=dsinks[:, :, 0].reshape(1, SWA_Q_HEADS), b_w_o=g_wo, rel_bias_table=g_table,
        ffn_norm_w=jnp.concatenate([gf0["norm_w"], gf1["norm_w"]], axis=0),
        ffn_w_up=gf0["w_up"],
        ffn_conv_w=jnp.stack([gf0["conv_w"], gf1["conv_w"]], axis=0),
        ffn_conv_b=jnp.concatenate([gf0["conv_b"], gf1["conv_b"]], axis=0),
        ffn_w_down=gf0["w_down"],
        final_norm_w=d_final,
    )
    scattered = dict(zip([n for n in BIG if n in ready], zip(pairs, parts)))
    scattered.update(zip(["a_w_in"], zip(last_pair, last_parts)))
    return loss, dx, grads, scattered


HBM_SPEC = pl.BlockSpec(memory_space=pltpu.HBM)
VMEM_SPEC = pl.BlockSpec(memory_space=pltpu.VMEM)


def _coords():
    return lax.axis_index("x"), lax.axis_index("y"), lax.axis_index("c")


def _remote(src, dst, send_sem, recv_sem, device):
    return pltpu.make_async_remote_copy(src_ref=src, dst_ref=dst, send_sem=send_sem, recv_sem=recv_sem,
                                        device_id=device, device_id_type=MESH)


def _other_chips(x, y):
    return [(1 - x, y), (x, 1 - y), (1 - x, 1 - y)]


def _gather_copies(shapes, split, ins, outs, send_sems, recv_sems):
    x, y, c = _coords()
    p = 2 * x + y
    ici, forwards, from_sibling = [], [], []
    for a, shape in enumerate(shapes):
        h = shape[0] // 2
        for j, chip in enumerate(_other_chips(x, y)):
            q = 2 * chip[0] + chip[1]
            if split[a]:
                mine, theirs = pl.ds(c * h, h), pl.ds((1 - c) * h, h)
                ici.append(_remote(ins[a].at[mine], outs[a].at[p, mine], send_sems.at[6 * a + j],
                                   recv_sems.at[6 * a + j], (*chip, c)))
                land = outs[a].at[q, mine]
                forwards.append(_remote(land, land, send_sems.at[6 * a + 3 + j], recv_sems.at[6 * a + 3 + j],
                                        (x, y, 1 - c)))
                land = outs[a].at[q, theirs]
                from_sibling.append(_remote(land, land, send_sems.at[6 * a + 3 + j], recv_sems.at[6 * a + 3 + j],
                                            (x, y, 1 - c)))
            else:
                ici.append(_remote(ins[a], outs[a].at[p], send_sems.at[6 * a + j], recv_sems.at[6 * a + j],
                                   (*chip, c)))
                forwards.append(None)
    return ici, forwards, from_sibling


def _gather_arrival(shapes, split, ins, outs, send_sems, recv_sems):
    x, y, c = _coords()
    ici, forwards, from_sibling = _gather_copies(shapes, split, ins, outs, send_sems, recv_sems)
    k = 0
    for a, shape in enumerate(shapes):
        h = shape[0] // 2
        for j, chip in enumerate(_other_chips(x, y)):
            q = 2 * chip[0] + chip[1]
            land = outs[a].at[q, pl.ds(c * h, h)] if split[a] else outs[a].at[q]
            _remote(land, land, send_sems.at[6 * a + j], recv_sems.at[6 * a + j], (*chip, c)).wait_recv()
            if forwards[k] is not None:
                forwards[k].start()
            k += 1
    for cp in from_sibling:
        cp.wait_recv()
    for cp in ici + [f for f in forwards if f is not None]:
        cp.wait_send()


def _all_gather(arrs, split, remote):
    n = len(arrs)
    now = [a for a in range(n) if remote[a]]
    shapes = [arrs[a].shape for a in now]
    splits = [split[a] for a in now]

    def body(*refs):
        ins, outs, stage = refs[:n], refs[n:2 * n], refs[2 * n:3 * n]
        send_sems, recv_sems, in_sems, out_sems = refs[3 * n:]
        p = 2 * lax.axis_index("x") + lax.axis_index("y")
        gathered = ([ins[a] for a in now], [outs[a] for a in now], send_sems, recv_sems)
        loads = [pltpu.make_async_copy(ins[a], stage[a], in_sems.at[a]) for a in range(n)]
        for cp in loads:
            cp.start()
        for cp in _gather_copies(shapes, splits, *gathered)[0]:
            cp.start()
        stores = [pltpu.make_async_copy(stage[a], outs[a].at[p], out_sems.at[a]) for a in range(n)]
        for a in range(n):
            loads[a].wait()
            stores[a].start()
        _gather_arrival(shapes, splits, *gathered)
        for cp in stores:
            cp.wait()

    return pl.pallas_call(
        body, name="weights_all_gather", in_specs=[HBM_SPEC] * n, out_specs=[HBM_SPEC] * n,
        out_shape=[jax.ShapeDtypeStruct((N_CHIPS,) + a.shape, a.dtype) for a in arrs],
        scratch_shapes=[pltpu.VMEM(a.shape, a.dtype) for a in arrs]
        + [pltpu.SemaphoreType.DMA((6 * len(now),)), pltpu.SemaphoreType.DMA((6 * len(now),)),
           pltpu.SemaphoreType.DMA((n,)), pltpu.SemaphoreType.DMA((n,))],
        compiler_params=pltpu.CompilerParams(vmem_limit_bytes=VMEM_LIMIT),
    )(*arrs)


PAIR_SWAP_PIECES = 2


def _swap_copies(shapes, ins, other, send_sems, recv_sems):
    x, y, c = _coords()
    copies = []
    for a, shape in enumerate(shapes):
        h = shape[1] // 2
        piece = h // PAIR_SWAP_PIECES
        for q in range(N_CHIPS):
            for r in range(PAIR_SWAP_PIECES):
                k = (a * N_CHIPS + q) * PAIR_SWAP_PIECES + r
                copies.append(_remote(ins[a].at[q, pl.ds((1 - c) * h + r * piece, piece)],
                                      other[a].at[q, pl.ds(r * piece, piece)], send_sems.at[k], recv_sems.at[k],
                                      (x, y, 1 - c)))
    return copies


def _swap_out_shapes(gs):
    return [jax.ShapeDtypeStruct((N_CHIPS, g.shape[1] // 2, g.shape[2]), g.dtype) for g in gs]


def _pair_swap(gs, tag):
    n = len(gs)
    shapes = [g.shape for g in gs]

    def body(*refs):
        copies = _swap_copies(shapes, refs[:n], refs[n:2 * n], *refs[2 * n:])
        for cp in copies:
            cp.start()
        for cp in copies:
            cp.wait()

    nsem = n * N_CHIPS * PAIR_SWAP_PIECES
    return pl.pallas_call(
        body, name=f"grads_pair_swap_{tag}", in_specs=[HBM_SPEC] * n, out_specs=[HBM_SPEC] * n,
        out_shape=_swap_out_shapes(gs),
        scratch_shapes=[pltpu.SemaphoreType.DMA((nsem,)), pltpu.SemaphoreType.DMA((nsem,))],
    )(*gs)


def _scatter_copies(ins, outs, send_sems, recv_sems):
    x, y, c = _coords()
    copies = []
    for a in range(len(ins)):
        for j, chip in enumerate(_other_chips(x, y)):
            q = 2 * chip[0] + chip[1]
            copies.append(_remote(ins[a].at[q], outs[a].at[j], send_sems.at[3 * a + j], recv_sems.at[3 * a + j],
                                  (*chip, c)))
    return copies


def _scatter_shapes(ps):
    return [jax.ShapeDtypeStruct((N_CHIPS - 1,) + a.shape[1:], a.dtype) for a in ps]


def _pair_share(rs):
    n = len(rs)

    def body(*refs):
        ins, outs, stage = refs[:n], refs[n:2 * n], refs[2 * n:3 * n]
        send_sems, recv_sems, in_sems, out_sems = refs[3 * n:]
        x, y, c = _coords()

        def mine(a):
            h = rs[a].shape[0]
            return outs[a].at[pl.ds(c * h, h)]

        loads = [pltpu.make_async_copy(ins[a], stage[a], in_sems.at[a]) for a in range(n)]
        for cp in loads:
            cp.start()
        sends = [_remote(ins[a], mine(a), send_sems.at[a], recv_sems.at[a], (x, y, 1 - c)) for a in range(n)]
        for cp in sends:
            cp.start()
        stores = [pltpu.make_async_copy(stage[a], mine(a), out_sems.at[a]) for a in range(n)]
        for a in range(n):
            loads[a].wait()
            stores[a].start()
        for a in range(n):
            h = rs[a].shape[0]
            land = outs[a].at[pl.ds((1 - c) * h, h)]
            _remote(land, land, send_sems.at[a], recv_sems.at[a], (x, y, 1 - c)).wait_recv()
        for cp in sends:
            cp.wait_send()
        for cp in stores:
            cp.wait()

    return pl.pallas_call(
        body, name="grads_pair_share", in_specs=[HBM_SPEC] * n, out_specs=[HBM_SPEC] * n,
        out_shape=[jax.ShapeDtypeStruct((2 * a.shape[0], a.shape[1]), a.dtype) for a in rs],
        scratch_shapes=[pltpu.VMEM(a.shape, a.dtype) for a in rs] + [pltpu.SemaphoreType.DMA((n,))] * 4,
        compiler_params=pltpu.CompilerParams(vmem_limit_bytes=VMEM_LIMIT),
    )(*rs)


def _small_all_reduce(buf):
    R = buf.shape[0]
    ndev = 2 * N_CHIPS

    def body(in_ref, out_ref, gath, send_sems, recv_sems):
        x, y, c = _coords()
        me = 4 * x + 2 * y + c
        gath[me] = in_ref[...]
        peers = []
        for d in range(1, ndev):
            px = 1 - x if d & 4 else x
            py = 1 - y if d & 2 else y
            pc = 1 - c if d & 1 else c
            peers.append((px, py, pc))
        sends = []
        for d, peer in enumerate(peers):
            cp = _remote(in_ref, gath.at[me], send_sems.at[d], recv_sems.at[d], peer)
            cp.start()
            sends.append(cp)
        for d, peer in enumerate(peers):
            land = gath.at[4 * peer[0] + 2 * peer[1] + peer[2]]
            _remote(land, land, send_sems.at[d], recv_sems.at[d], peer).wait_recv()
        for cp in sends:
            cp.wait_send()
        acc = gath[0]
        for s in range(1, ndev):
            acc = acc + gath[s]
        out_ref[...] = acc

    return pl.pallas_call(
        body, name="small_all_reduce", in_specs=[VMEM_SPEC], out_specs=VMEM_SPEC,
        out_shape=jax.ShapeDtypeStruct(buf.shape, F32),
        scratch_shapes=[pltpu.VMEM((ndev, R, LANE), F32), pltpu.SemaphoreType.DMA((ndev - 1,)),
                        pltpu.SemaphoreType.DMA((ndev - 1,))],
    )(buf)


def _pair_add(name, own, other):
    h = own.shape[1]
    tm = _tile(h, (128, 64, 32, 16))

    def fn(i, a, b):
        return (a + b,)

    return _rowcall(name, fn, h, tm, [(own, "row", None), (other, "row", None)], [(own.shape, BF16, "row")])[0]


def _chip_add(name, own, parts):
    h = parts.shape[1]
    tm = _tile(h, (128, 64, 32, 16))

    def fn(i, o, a):
        a = a.astype(F32)
        return (((o.astype(F32) + a[0]) + a[1]) + a[2],)

    return _rowcall(name, fn, h, tm, [(own, "row", None), (parts, "row", None)], [(parts.shape[1:], F32, "row")])[0]


def _adamw(name, w, g, m, v):
    R = w.shape[0]
    tm = _tile(R, (256, 128, 64, 32, 16, 8))

    def fn(i, wv, gv, mv, vv):
        m2 = ADAM_B1 * mv + (1.0 - ADAM_B1) * gv
        v2 = ADAM_B2 * vv + (1.0 - ADAM_B2) * (gv * gv)
        m_hat = m2 / (1.0 - ADAM_B1 ** ADAM_STEP)
        v_hat = v2 / (1.0 - ADAM_B2 ** ADAM_STEP)
        delta = -ADAM_LR * (m_hat / (jnp.sqrt(v_hat) + ADAM_EPS) + ADAM_WD * wv)
        return delta, m2, v2

    ins = [(a, "row", None) for a in (w, g, m, v)]
    return _rowcall(name, fn, R, tm, ins, [(w.shape, F32, "row")] * 3)


def _pack(arrs):
    flat = jnp.concatenate([a.reshape(-1).astype(F32) for a in arrs])
    size = flat.shape[0]
    padded = -(-size // (SUBLANE * LANE)) * SUBLANE * LANE
    return jnp.pad(flat, (0, padded - size)).reshape(-1, LANE)


def _unpack(buf, shapes):
    flat = buf.reshape(-1)
    out, off = [], 0
    for s in shapes:
        size = math.prod(s)
        out.append(flat[off:off + size].reshape(s))
        off += size
    return out


BIG = ("a_w_in", "a_w_out", "w_kv", "b_w_q", "b_w_o", "ffn_w_up", "ffn_w_down")
WEIGHTS = ("a_norm_w", "a_w_in", "a_conv_w", "a_a_log", "a_dt_bias", "a_out_norm_w", "a_w_out", "kv_norm_w", "w_kv",
           "b_norm_w", "b_w_q", "b_sinks", "b_w_o", "rel_bias_table", "ffn_norm_w", "ffn_w_up", "ffn_conv_w",
           "ffn_conv_b", "ffn_w_down", "final_norm_w")
SMALL = tuple(n for n in WEIGHTS if n not in BIG)
SMALL_SHARDED = {"a_norm_w": 1, "a_conv_w": 2, "ffn_conv_w": 2}


def _quarter_2d(name, a):
    if name in ("ffn_w_up", "ffn_w_down"):
        return a.reshape(a.shape[0] * a.shape[1], a.shape[2])
    return a.reshape(a.shape[-2], a.shape[-1])


def _whole_weights(w):
    bigs = [_quarter_2d(n, w[n]).astype(BF16) for n in BIG]
    smalls = [w["a_norm_w"], w["a_conv_w"][0], w["ffn_conv_w"].reshape(6, DFF2_SHARD)]
    remote = [True] + [False] * (len(bigs) - 1) + [True] * len(smalls)
    g = _all_gather(bigs + smalls, [True] * len(bigs) + [False] * len(smalls), remote)
    w_in = g[0].transpose(1, 0, 2).reshape(D, GDN_IN)
    nh = GDN_V_HEADS
    zpad = jnp.zeros((D, LANE - nh), BF16)
    w_in_ba = jnp.concatenate([w_in[:, GDN_MAIN:GDN_MAIN + nh], zpad, w_in[:, GDN_MAIN + nh:], zpad], axis=1)
    lane_pad = lambda a: jnp.pad(a, ((0, 0), (0, LANE - nh)))
    early = dict(
        a_norm_w=g[7].reshape(1, D), w_in_main=w_in[:, :GDN_MAIN], w_in_ba=w_in_ba,
        a_conv_w=g[8].transpose(1, 0, 2).reshape(4, GDN_CONV), a_log=lane_pad(w["a_a_log"]),
        dt_bias=lane_pad(w["a_dt_bias"]), a_out_norm_w=w["a_out_norm_w"],
        kv_norm_w=w["kv_norm_w"].reshape(1, D), b_norm_w=w["b_norm_w"],
        sinks=jnp.broadcast_to(w["b_sinks"].reshape(SWA_KV_HEADS, SWA_GROUP, 1), (SWA_KV_HEADS, SWA_GROUP, LANE)),
        rel_table_t=w["rel_bias_table"].T, ffn_norm_w=w["ffn_norm_w"],
        ffn_conv_w=g[9].reshape(N_CHIPS, 2, 3, DFF2_SHARD).transpose(1, 2, 0, 3).reshape(2, 3, DFF2),
        ffn_conv_b=w["ffn_conv_b"], final_norm_w=w["final_norm_w"].reshape(1, D),
    )
    return early, (bigs[1:], g[1:len(bigs)])


def _late_weights(g):
    return dict(
        w_out=g[0].reshape(GDN_V, D), w_kv=g[1].reshape(D, 2 * SWA_KV_HEADS * SWA_HD), w_q=g[2].reshape(D, D),
        w_o=g[3].reshape(D, D), w_up=g[4].reshape(N_CHIPS, 2, D, DFF2_SHARD),
        w_down=g[5].reshape(N_CHIPS, 2, DFF_SHARD, D).transpose(1, 0, 2, 3).reshape(2, DFF, D),
    )


def _chip_major(name, g):
    if name == "a_w_in":
        return g.reshape(D, N_CHIPS, GDN_IN_SHARD).transpose(1, 0, 2)
    if name == "ffn_w_up":
        return g.reshape(N_CHIPS, 2 * D, DFF2_SHARD)
    if name == "ffn_w_down":
        return g.reshape(N_CHIPS, 2 * DFF_SHARD, D)
    return g.reshape(N_CHIPS, g.shape[0] // N_CHIPS, g.shape[1])


def kernel(x, a_norm_w, a_w_in, a_conv_w, a_a_log, a_dt_bias, a_out_norm_w, a_w_out, kv_norm_w, w_kv, b_norm_w, b_w_q, b_sinks, b_w_o, rel_bias_table, ffn_norm_w, ffn_w_up, ffn_conv_w, ffn_conv_b, ffn_w_down, final_norm_w, loss_target, m_a_norm_w, m_a_w_in, m_a_conv_w, m_a_a_log, m_a_dt_bias, m_a_out_norm_w, m_a_w_out, m_kv_norm_w, m_w_kv, m_b_norm_w, m_b_w_q, m_b_sinks, m_b_w_o, m_rel_bias_table, m_ffn_norm_w, m_ffn_w_up, m_ffn_conv_w, m_ffn_conv_b, m_ffn_w_down, m_final_norm_w, v_a_norm_w, v_a_w_in, v_a_conv_w, v_a_a_log, v_a_dt_bias, v_a_out_norm_w, v_a_w_out, v_kv_norm_w, v_w_kv, v_b_norm_w, v_b_w_q, v_b_sinks, v_b_w_o, v_rel_bias_table, v_ffn_norm_w, v_ffn_w_up, v_ffn_conv_w, v_ffn_conv_b, v_ffn_w_down, v_final_norm_w):
    w = dict(zip(WEIGHTS, (a_norm_w, a_w_in, a_conv_w, a_a_log, a_dt_bias, a_out_norm_w, a_w_out, kv_norm_w, w_kv,
                           b_norm_w, b_w_q, b_sinks, b_w_o, rel_bias_table, ffn_norm_w, ffn_w_up, ffn_conv_w,
                           ffn_conv_b, ffn_w_down, final_norm_w)))
    m = dict(zip(WEIGHTS, (m_a_norm_w, m_a_w_in, m_a_conv_w, m_a_a_log, m_a_dt_bias, m_a_out_norm_w, m_a_w_out,
                           m_kv_norm_w, m_w_kv, m_b_norm_w, m_b_w_q, m_b_sinks, m_b_w_o, m_rel_bias_table,
                           m_ffn_norm_w, m_ffn_w_up, m_ffn_conv_w, m_ffn_conv_b, m_ffn_w_down, m_final_norm_w)))
    v = dict(zip(WEIGHTS, (v_a_norm_w, v_a_w_in, v_a_conv_w, v_a_a_log, v_a_dt_bias, v_a_out_norm_w, v_a_w_out,
                           v_kv_norm_w, v_w_kv, v_b_norm_w, v_b_w_q, v_b_sinks, v_b_w_o, v_rel_bias_table,
                           v_ffn_norm_w, v_ffn_w_up, v_ffn_conv_w, v_ffn_conv_b, v_ffn_w_down, v_final_norm_w)))
    T = x.shape[1]
    chip = 2 * lax.axis_index("x") + lax.axis_index("y")

    core = lax.axis_index("c")

    def pair_sums(names, whole, other):
        own = [lax.dynamic_slice_in_dim(g, core * (g.shape[1] // 2), g.shape[1] // 2, 1) for g in whole]
        return [_pair_add(f"pair_add_{n}", a, b) for n, a, b in zip(names, own, other)]

    early, late = _whole_weights(w)
    loss_part, dx, grads, scattered = _local_step(x.reshape(T, D), loss_target.reshape(T, D), early, late, pair_sums)

    assert all(n in scattered for n in BIG)
    halves = [_chip_add(f"chip_add_{n}", lax.dynamic_index_in_dim(scattered[n][0], chip, 0, keepdims=False),
                        scattered[n][1]) for n in BIG]
    quarter = _pair_share(halves)
    out_g, out_d, out_m, out_v = {}, {}, {}, {}
    for n, g2 in zip(BIG, quarter):
        res = _adamw(f"adamw_{n}", _quarter_2d(n, w[n]), g2, _quarter_2d(n, m[n]), _quarter_2d(n, v[n]))
        out_g[n] = g2.reshape(w[n].shape)
        out_d[n], out_m[n], out_v[n] = (r.reshape(w[n].shape) for r in res)

    whole = [grads[n] for n in SMALL]
    summed = _unpack(_small_all_reduce(_pack([loss_part[0:1, 0:1]] + whole)), [(1, 1)] + [a.shape for a in whole])
    loss = summed[0].reshape(())
    small_g = []
    for n, g in zip(SMALL, summed[1:]):
        if n in SMALL_SHARDED:
            axis = SMALL_SHARDED[n]
            g = g.reshape(w[n].shape[:axis] + (-1,) + w[n].shape[axis + 1:])
            size = w[n].shape[axis]
            g = lax.dynamic_slice_in_dim(g, chip * size, size, axis)
        small_g.append(g.reshape(w[n].shape))
    shapes = [w[n].shape for n in SMALL]
    res = _adamw("adamw_small", _pack([w[n] for n in SMALL]), _pack(small_g), _pack([m[n] for n in SMALL]),
                 _pack([v[n] for n in SMALL]))
    small_d, small_m, small_v = (_unpack(r, shapes) for r in res)
    for i, n in enumerate(SMALL):
        out_g[n], out_d[n], out_m[n], out_v[n] = small_g[i], small_d[i], small_m[i], small_v[i]

    return (loss, dx.reshape(x.shape), *[out_g[n] for n in WEIGHTS], *[out_d[n] for n in WEIGHTS],
            *[out_m[n] for n in WEIGHTS], *[out_v[n] for n in WEIGHTS])
```

```python
import functools
import math

import jax
import jax.numpy as jnp
from jax import lax
from jax.experimental import pallas as pl
from jax.experimental.pallas import tpu as pltpu

F32 = jnp.float32
BF16 = jnp.bfloat16
MESH = pl.DeviceIdType.MESH
HIGHEST = lax.Precision.HIGHEST

D = 1024
EPS = 1e-6
NEG_INF = -1e30
N_CHIPS = 4

GDN_QK_HEADS = 8
GDN_V_HEADS = 16
GDN_HD = 128
GDN_QK = GDN_QK_HEADS * GDN_HD
GDN_V = GDN_V_HEADS * GDN_HD
GDN_CONV = 2 * GDN_QK + GDN_V
GDN_MAIN = GDN_CONV + GDN_V
GDN_IN = GDN_MAIN + 2 * GDN_V_HEADS
GDN_IN_SHARD = GDN_IN // N_CHIPS
GDN_CHUNK = 64

SWA_Q_HEADS = 16
SWA_KV_HEADS = 4
SWA_GROUP = 4
SWA_HD = 64
SWA_BLOCK = 128
REL_BUCKETS = 32
REL_MAX_DISTANCE = 128

DFF = 2816
DFF2 = 2 * DFF
DFF2_SHARD = DFF2 // N_CHIPS
DFF_SHARD = DFF // N_CHIPS

ADAM_LR = 0.001
ADAM_B1 = 0.9
ADAM_B2 = 0.999
ADAM_EPS = 1e-08
ADAM_WD = 0.01
ADAM_STEP = 10

LANE = 128
SUBLANE = 8
VMEM_LIMIT = 56 * 1024 * 1024


def _params(sem, vmem=VMEM_LIMIT):
    return pltpu.CompilerParams(dimension_semantics=sem, vmem_limit_bytes=vmem)


def _rowcall(name, fn, T, tm, ins, outs, swap=()):
    n = T // tm
    nswap = len(swap)
    swap_shapes = [g.shape for g in swap]
    r8 = tm // SUBLANE
    last8 = T // SUBLANE - 1
    arrays, in_specs = [], []
    for arr, kind, cols in ins:
        arrays.append(arr)
        if kind == "full":
            in_specs.append(pl.BlockSpec(arr.shape, functools.partial(lambda nd, i: (0,) * nd, arr.ndim)))
        elif arr.ndim == 2:
            w, ci = cols if cols is not None else (arr.shape[1], 0)
            if kind == "row":
                in_specs.append(pl.BlockSpec((tm, w), functools.partial(lambda ci, i: (i, ci), ci)))
            elif kind == "prev":
                in_specs.append(pl.BlockSpec(
                    (SUBLANE, w), functools.partial(lambda ci, i: (jnp.maximum(i * r8 - 1, 0), ci), ci)))
            else:
                in_specs.append(pl.BlockSpec(
                    (SUBLANE, w), functools.partial(lambda ci, i: (jnp.minimum((i + 1) * r8, last8), ci), ci)))
        else:
            lead = arr.shape[:-2]
            in_specs.append(pl.BlockSpec(lead + (tm, arr.shape[-1]),
                                         functools.partial(lambda nl, i: (0,) * nl + (i, 0), len(lead))))
    out_shape, out_specs = [], []
    for shape, dtype, kind in outs:
        out_shape.append(jax.ShapeDtypeStruct(shape, dtype))
        if kind == "acc":
            out_specs.append(pl.BlockSpec(shape, functools.partial(lambda nd, i: (0,) * nd, len(shape))))
        else:
            lead = shape[:-2]
            out_specs.append(pl.BlockSpec(lead + (tm, shape[-1]),
                                          functools.partial(lambda nl, i: (0,) * nl + (i, 0), len(lead))))
    nin = len(arrays)

    nout = len(outs)

    def body(*refs):
        i = pl.program_id(0)
        if nswap:
            comm = (swap_shapes, refs[nin:nin + nswap], refs[nin + nswap + nout:nin + 2 * nswap + nout], refs[-2],
                    refs[-1])

            @pl.when(i == 0)
            def _():
                for cp in _swap_copies(*comm):
                    cp.start()

            @pl.when(i == n - 1)
            def _():
                for cp in _swap_copies(*comm):
                    cp.wait()

        vals = [r[...] for r in refs[:nin]]
        res = fn(i, *vals)
        for (shape, dtype, kind), o, r in zip(outs, refs[nin + nswap:], res):
            if kind == "row":
                o[...] = r.astype(dtype)
            else:
                @pl.when(i == 0)
                def _():
                    o[...] = r.astype(dtype)

                @pl.when(i > 0)
                def _():
                    o[...] += r.astype(dtype)

    anywhere = pl.BlockSpec(memory_space=pl.ANY)
    nsem = nswap * N_CHIPS * PAIR_SWAP_PIECES
    return pl.pallas_call(
        body, name=name, grid=(n,), in_specs=in_specs + [anywhere] * nswap, out_specs=out_specs + [anywhere] * nswap,
        out_shape=out_shape + _swap_out_shapes(swap),
        scratch_shapes=[pltpu.SemaphoreType.DMA((nsem,)), pltpu.SemaphoreType.DMA((nsem,))] if nswap else [],
        compiler_params=_params(("arbitrary",)),
    )(*arrays, *swap)


def _mm(name, a, b, out_shape, out_dtype, grid, a_spec, b_spec, o_spec, dims, acc_shape, res=None, precision=None,
        into=None, scatter=(), post=None):
    nk = grid[2]
    ns = len(scatter)
    a, norm_w = a if isinstance(a, tuple) else (a, None)
    normed = norm_w is not None
    posted = post is not None
    n_in = 2 + normed + (res is not None) + (into is not None) + 3 * posted + ns
    n_out = 1 + posted + ns

    def body(*refs):
        a_ref, b_ref, o_ref = refs[0], refs[1], refs[n_in]
        r_ref = refs[2 + normed] if res is not None else None
        first = pl.program_id(0) == 0
        if ns:
            comm = (refs[n_in - ns:n_in], refs[n_in + n_out - ns:n_in + n_out], refs[-2], refs[-1])
            steps = [pl.program_id(d) for d in range(3)]

            @pl.when((steps[0] == 0) & (steps[1] == 0) & (steps[2] == 0))
            def _():
                for cp in _scatter_copies(*comm):
                    cp.start()

            @pl.when((steps[0] == grid[0] - 1) & (steps[1] == grid[1] - 1) & (steps[2] == grid[2] - 1))
            def _():
                copies = _scatter_copies(*comm)
                for cp in copies:
                    cp.wait_recv()
                for cp in copies:
                    cp.wait_send()

        av, bv = a_ref[...], b_ref[...]
        if normed:
            av = _rms_core(av, refs[2][...])
        if precision is None:
            av, bv = av.astype(BF16), bv.astype(BF16)
        p = lax.dot_general(av, bv, (dims, ((), ())), preferred_element_type=F32, precision=precision)

        def finish(x):
            if res is not None:
                x = x + r_ref[...].astype(F32)
            if posted:
                h_ref, w_ref, add_ref = refs[n_in - ns - 3:n_in - ns]
                dh, dw = jax.vjp(_rms_core, h_ref[...], w_ref[...])[1](x)
                x = dh + add_ref[...]
                dw_ref = refs[n_in + 1]

                @pl.when(first)
                def _():
                    dw_ref[...] = dw

                @pl.when(jnp.logical_not(first))
                def _():
                    dw_ref[...] += dw

            o_ref[...] = x.astype(out_dtype).reshape(o_ref.shape)

        if nk == 1:
            finish(p)
        else:
            acc = refs[n_in + n_out]
            k = pl.program_id(2)

            @pl.when(k == 0)
            def _():
                acc[...] = p

            @pl.when(k > 0)
            def _():
                acc[...] += p

            @pl.when(k == nk - 1)
            def _():
                finish(acc[...])

    anywhere = pl.BlockSpec(memory_space=pl.ANY)
    ops = [a, b] + ([norm_w] if normed else []) + ([res] if res is not None else [])
    ops += ([into] if into is not None else []) + (list(post) if posted else []) + list(scatter)
    whole = lambda arr: pl.BlockSpec(arr.shape, lambda i, j, k: (0, 0))
    specs = [a_spec, b_spec] + ([whole(norm_w)] if normed else [])
    specs += [o_spec] if res is not None else []
    specs += [anywhere] if into is not None else []
    specs += ([o_spec, whole(post[1]), o_spec] if posted else []) + [anywhere] * ns
    out = pl.pallas_call(
        body, name=name, grid=grid, in_specs=specs,
        out_specs=[o_spec] + ([whole(post[1])] if posted else []) + [anywhere] * ns,
        out_shape=[jax.ShapeDtypeStruct(out_shape, out_dtype)]
        + ([jax.ShapeDtypeStruct(post[1].shape, F32)] if posted else []) + _scatter_shapes(scatter),
        input_output_aliases={2 + normed + (res is not None): 0} if into is not None else {},
        scratch_shapes=([pltpu.VMEM(acc_shape, F32)] if nk > 1 else [])
        + ([pltpu.SemaphoreType.DMA((3 * ns,)), pltpu.SemaphoreType.DMA((3 * ns,))] if ns else []),
        compiler_params=_params(("arbitrary",) * 3 if ns or posted else ("parallel", "parallel", "arbitrary")),
    )(*ops)
    return out if n_out > 1 else out[0]


NN = ((1,), (0,))
NT = ((1,), (1,))
TN = ((0,), (0,))


BIG_TILES = (1024, 512, 256, 128)


def _tile(n, pref):
    for t in pref:
        if n % t == 0:
            return t
    return n


def _rows_of(a):
    return a[0] if isinstance(a, tuple) else a


def _mm_nn(name, a, w, out_dtype, res=None, precision=None):
    M, K = _rows_of(a).shape
    N = w.shape[1]
    tm = _tile(M, BIG_TILES if K <= 2048 else BIG_TILES[1:])
    tn = _tile(N, BIG_TILES)
    return _mm(name, a, w, (M, N), out_dtype, (M // tm, N // tn, 1),
               pl.BlockSpec((tm, K), lambda i, j, k: (i, 0)), pl.BlockSpec((K, tn), lambda i, j, k: (0, j)),
               pl.BlockSpec((tm, tn), lambda i, j, k: (i, j)), NN, (tm, tn), res=res, precision=precision)


def _mm_nt(name, g, w, out_dtype, res=None, precision=None, scatter=(), post=None):
    M, N = g.shape
    K = w.shape[0]
    tm, tk = _tile(M, BIG_TILES if post is None else BIG_TILES[1:]), _tile(K, (1024, 1408, 512, 256, 128))
    tn = _tile(N, (1536,) + BIG_TILES)
    return _mm(name, g, w, (M, K), out_dtype, (M // tm, K // tk, N // tn),
               pl.BlockSpec((tm, tn), lambda i, j, k: (i, k)), pl.BlockSpec((tk, tn), lambda i, j, k: (j, k)),
               pl.BlockSpec((tm, tk), lambda i, j, k: (i, j)), NT, (tm, tk), res=res, precision=precision,
               scatter=scatter, post=post)


def _mm_tn(name, a, g, out_dtype=F32, precision=None):
    T, K = _rows_of(a).shape
    N = g.shape[1]
    tk, tn = _tile(K, (1024, 1408, 512, 256, 128)), _tile(N, BIG_TILES)
    assert tk == K or not isinstance(a, tuple)
    tt = _tile(T, BIG_TILES)
    return _mm(name, a, g, (K, N), out_dtype, (K // tk, N // tn, T // tt),
               pl.BlockSpec((tt, tk), lambda i, j, k: (k, i)), pl.BlockSpec((tt, tn), lambda i, j, k: (k, j)),
               pl.BlockSpec((tk, tn), lambda i, j, k: (i, j)), TN, (tk, tn), precision=precision)


def _mm_up(name, n, wup, layer):
    T = _rows_of(n).shape[0]
    tm = _tile(T, BIG_TILES)
    return _mm(name, n, wup, (T, DFF2), BF16, (T // tm, N_CHIPS, 1),
               pl.BlockSpec((tm, D), lambda i, j, k: (i, 0)),
               pl.BlockSpec((None, None, D, DFF2_SHARD), lambda i, j, k: (j, layer, 0, 0)),
               pl.BlockSpec((tm, DFF2_SHARD), lambda i, j, k: (i, j)), NN, (tm, DFF2_SHARD))


def _mm_up_nt(name, du, wup, layer, post):
    T = du.shape[0]
    tm, tk = _tile(T, BIG_TILES[1:]), D
    return _mm(name, du, wup, (T, D), F32, (T // tm, D // tk, N_CHIPS),
               pl.BlockSpec((tm, DFF2_SHARD), lambda i, j, k: (i, k)),
               pl.BlockSpec((None, None, tk, DFF2_SHARD), lambda i, j, k: (k, layer, j, 0)),
               pl.BlockSpec((tm, tk), lambda i, j, k: (i, j)), NT, (tm, tk), post=post)


def _mm_up_tn(name, n, du, layer, into):
    T = _rows_of(n).shape[0]
    tk, tt = D, _tile(T, BIG_TILES)
    return _mm(name, n, du, (N_CHIPS, 2, D, DFF2_SHARD), F32, (D // tk, N_CHIPS, T // tt),
               pl.BlockSpec((tt, tk), lambda i, j, k: (k, i)), pl.BlockSpec((tt, DFF2_SHARD), lambda i, j, k: (k, j)),
               pl.BlockSpec((None, None, tk, DFF2_SHARD), lambda i, j, k: (j, layer, i, 0)), TN, (tk, DFF2_SHARD),
               into=into)


def _mm_down_tn(name, act, dout, layer, into):
    T = act.shape[0]
    tk, tn, tt = 2 * DFF_SHARD, _tile(D, BIG_TILES), _tile(T, BIG_TILES)
    return _mm(name, act, dout, (2, 2, 2, DFF_SHARD, D), F32, (DFF // tk, D // tn, T // tt),
               pl.BlockSpec((tt, tk), lambda i, j, k: (k, i)), pl.BlockSpec((tt, tn), lambda i, j, k: (k, j)),
               pl.BlockSpec((None, 2, None, DFF_SHARD, tn), lambda i, j, k: (i, 0, layer, 0, j)), TN, (tk, tn),
               into=into)


def _sigmoid(x):
    return 0.5 * jnp.tanh(0.5 * x) + 0.5


def _silu(x):
    return x * _sigmoid(x)


def _softplus(x):
    return jnp.maximum(x, 0.0) + jnp.log(1.0 + jnp.exp(-jnp.abs(x)))


def _rms_core(h, w):
    return h * lax.rsqrt(jnp.mean(h * h, axis=-1, keepdims=True) + EPS) * w


def _shift_down(x, halo, s, i):
    if s == 0:
        return x
    tm = x.shape[0]
    rolled = pltpu.roll(x, s, 0)
    patch = pltpu.roll(jnp.where(i == 0, 0.0, halo), s, 0)
    row = lax.broadcasted_iota(jnp.int32, patch.shape, 0)
    top = jnp.where(row < s, patch, rolled[:SUBLANE])
    return jnp.concatenate([top, rolled[SUBLANE:]], axis=0) if tm > SUBLANE else top


def _shift_up(x, halo, s, i, n):
    if s == 0:
        return x
    tm = x.shape[0]
    rolled = pltpu.roll(x, tm - s, 0)
    patch = pltpu.roll(jnp.where(i == n - 1, 0.0, halo), SUBLANE - s, 0)
    row = lax.broadcasted_iota(jnp.int32, patch.shape, 0)
    bottom = jnp.where(row >= SUBLANE - s, patch, rolled[tm - SUBLANE:])
    return jnp.concatenate([rolled[:tm - SUBLANE], bottom], axis=0) if tm > SUBLANE else bottom


def _taps(x, halo, K, i):
    return [_shift_down(x, halo, K - 1 - j, i) for j in range(K)]


def _conv_fwd(taps, w):
    y = w[0:1, :] * taps[0]
    for j in range(1, len(taps)):
        y = y + w[j:j + 1, :] * taps[j]
    return y


def _conv_dx(dy, halo_next, w, i, n):
    K = w.shape[0]
    dx = w[K - 1:K, :] * dy
    for j in range(K - 1):
        dx = dx + w[j:j + 1, :] * _shift_up(dy, halo_next, K - 1 - j, i, n)
    return dx


def _conv_dw(dy, taps):
    rows = [jnp.sum(dy * tap, axis=0, keepdims=True) for tap in taps]
    return jnp.concatenate(rows + [jnp.zeros((SUBLANE - len(taps), dy.shape[1]), F32)], axis=0)


def _l2(x):
    return x * lax.rsqrt(jnp.sum(x * x, axis=-1, keepdims=True) + EPS)


def _gdn_post_core(yq, yk, yv, pb, pa, a_log, dtb):
    qn = tuple(_l2(_silu(a)) * (GDN_HD ** -0.5) for a in yq)
    kn = tuple(_l2(_silu(a)) for a in yk)
    v = _silu(yv)
    beta = _sigmoid(pb)
    g = -jnp.exp(a_log) * _softplus(pa + dtb)
    return qn, kn, v, beta, g


def _heads(x, n):
    return tuple(x[:, GDN_HD * h:GDN_HD * (h + 1)] for h in range(n))


def _gdn_pre_fwd(pm, pba, conv_w, a_log, dtb, tm=128):
    T = pm.shape[0]
    tm = min(tm, T)

    def fn(i, x, halo, pbav, cw, al, db):
        y = _conv_fwd(_taps(x.astype(F32), halo.astype(F32), 4, i), cw)
        qn, kn, v, beta, g = _gdn_post_core(_heads(y[:, :GDN_QK], 8), _heads(y[:, GDN_QK:2 * GDN_QK], 8),
                                            y[:, 2 * GDN_QK:], pbav[:, :LANE], pbav[:, LANE:], al, db)
        return jnp.stack(qn), jnp.stack(kn), jnp.stack(_heads(v, GDN_V_HEADS)), beta, g

    ins = [(pm, "row", (GDN_CONV, 0)), (pm, "prev", (GDN_CONV, 0)), (pba, "row", None),
           (conv_w, "full", None), (a_log, "full", None), (dtb, "full", None)]
    outs = [((GDN_QK_HEADS, T, GDN_HD), BF16, "row"), ((GDN_QK_HEADS, T, GDN_HD), BF16, "row"),
            ((GDN_V_HEADS, T, GDN_HD), BF16, "row"), ((T, LANE), F32, "row"), ((T, LANE), F32, "row")]
    return _rowcall("gdn_pre_fwd", fn, T, tm, ins, outs)


def _gdn_pre_bwd(pm, pba, conv_w, a_log, dtb, dqn, dkn, dv, dbeta, dg, tm=128):
    T = pm.shape[0]
    tm = min(tm, T)

    def fn(i, x, halo, pbav, cw, al, db, dqv, dkv, dvv, dbv, dgv):
        taps = _taps(x.astype(F32), halo.astype(F32), 4, i)
        y = _conv_fwd(taps, cw)
        prim = (_heads(y[:, :GDN_QK], 8), _heads(y[:, GDN_QK:2 * GDN_QK], 8), y[:, 2 * GDN_QK:],
                pbav[:, :LANE], pbav[:, LANE:], al, db)
        _, vjp = jax.vjp(_gdn_post_core, *prim)
        cot = (tuple(dqv[h].astype(F32) for h in range(8)), tuple(dkv[h].astype(F32) for h in range(8)),
               jnp.concatenate([dvv[h].astype(F32) for h in range(GDN_V_HEADS)], axis=1), dbv, dgv)
        dyq, dyk, dyv, dpb, dpa, dal, ddb = vjp(cot)
        dy = jnp.concatenate(list(dyq) + list(dyk) + [dyv], axis=1)
        dcw = _conv_dw(dy, taps)
        return dy, jnp.concatenate([dpb, dpa], axis=1), dcw, dal, ddb

    ins = [(pm, "row", (GDN_CONV, 0)), (pm, "prev", (GDN_CONV, 0)), (pba, "row", None),
           (conv_w, "full", None), (a_log, "full", None), (dtb, "full", None),
           (dqn, "row", None), (dkn, "row", None), (dv, "row", None), (dbeta, "row", None), (dg, "row", None)]
    outs = [((T, GDN_CONV), BF16, "row"), ((T, 2 * LANE), F32, "row"), ((SUBLANE, GDN_CONV), F32, "acc"),
            ((1, LANE), F32, "acc"), ((1, LANE), F32, "acc")]
    return _rowcall("gdn_pre_bwd", fn, T, tm, ins, outs)


def _gdn_conv_bwd(dy, dz, conv_w, tm=256):
    T = dy.shape[0]
    tm = min(tm, T)
    n = T // tm

    def fn(i, dyv, halo, dzv, cw):
        dx = _conv_dx(dyv.astype(F32), halo.astype(F32), cw, i, n)
        return (jnp.concatenate([dx.astype(BF16), dzv.astype(BF16)], axis=1),)

    ins = [(dy, "row", None), (dy, "next", None), (dz, "row", None), (conv_w, "full", None)]
    return _rowcall("gdn_conv_bwd", fn, T, tm, ins, [((T, GDN_MAIN), BF16, "row")])[0]


def _bdot(a, b, dims=NN):
    return lax.dot_general(a.astype(BF16), b.astype(BF16), (dims, ((), ())), preferred_element_type=F32)


BNN = ((2,), (1,))
BNT = ((2,), (2,))
BTN = ((1,), (1,))


def _bmm(a, b, dims=BNN):
    return lax.dot_general(a.astype(BF16), b.astype(BF16), (dims, ((0,), (0,))), preferred_element_type=F32)


def _bmm3(a, b):
    ah, bh = a.astype(BF16), b.astype(BF16)
    al, bl = (a - ah.astype(F32)).astype(BF16), (b - bh.astype(F32)).astype(BF16)
    dn = (BNN, ((0,), (0,)))
    return (lax.dot_general(ah, bh, dn, preferred_element_type=F32)
            + lax.dot_general(al, bh, dn, preferred_element_type=F32)
            + lax.dot_general(ah, bl, dn, preferred_element_type=F32))


def _tri_inv(m):
    C = m.shape[-1]
    r = lax.broadcasted_iota(jnp.int32, (C, C), 0)
    c = lax.broadcasted_iota(jnp.int32, (C, C), 1)
    t = jnp.where(r == c, 1.0, 0.0) - m
    pw = _bmm3(m, m)
    t = t + _bmm3(t, pw)
    for _ in range(int(math.log2(C)) - 2):
        pw = _bmm(pw, pw)
        t = t + _bmm(t, pw)
    return t


def _tri_inv_vjp(t, dt):
    tt = jnp.swapaxes(t, 1, 2)
    return -_bmm(_bmm(tt, dt), tt)


def _twice(a):
    return jnp.broadcast_to(a[:, None], (a.shape[0], 2) + a.shape[1:]).reshape((2 * a.shape[0],) + a.shape[1:])


def _gdn_gates(grow, brow):
    C = grow.shape[2]
    r = lax.broadcasted_iota(jnp.int32, (C, C), 0)
    c = lax.broadcasted_iota(jnp.int32, (C, C), 1)
    tril, eye = r >= c, r == c
    gcol = jnp.sum(jnp.where(eye, grow, 0.0), axis=2, keepdims=True)
    bcol = jnp.sum(jnp.where(eye, brow, 0.0), axis=2, keepdims=True)
    gc_col = jnp.sum(jnp.where(tril, grow, 0.0), axis=2, keepdims=True)
    gc_row = jnp.sum(jnp.where(r <= c, gcol, 0.0), axis=1, keepdims=True)
    gc_last = jnp.sum(grow, axis=2, keepdims=True)
    decay = jnp.where(tril, jnp.exp(jnp.where(tril, gc_col - gc_row, 0.0)), 0.0)
    return bcol, gc_col, gc_last, decay


def _gdn_m(k, bcol, decay):
    C = k.shape[1]
    strict = lax.broadcasted_iota(jnp.int32, (C, C), 0) > lax.broadcasted_iota(jnp.int32, (C, C), 1)
    return jnp.where(strict, bcol * _twice(_bmm(k, k, BNT)) * decay, 0.0)


def _gdn_rest(q, k, v, bcol, gc_col, gc_last, decay, t_mat, S):
    qk = _twice(_bmm(q, k, BNT))
    k2, q2 = _twice(k), _twice(q)
    egc = jnp.exp(gc_col)
    u = _bmm(t_mat, v * bcol)
    w = _bmm(t_mat, k2 * (bcol * egc))
    v_new = u - _bmm(w, S)
    o = _bmm(q2 * egc, S) + _bmm(qk * decay, v_new)
    s_new = S * jnp.exp(gc_last) + _bmm(k2 * jnp.exp(gc_last - gc_col), v_new, BTN)
    return o, s_new


def _gdn_tb(T):
    return min(256, T)


def _gate_rows(g):
    T = g.shape[0]
    g = g[:, :GDN_V_HEADS].reshape(T // GDN_CHUNK, GDN_CHUNK, GDN_V_HEADS)
    return g.transpose(0, 2, 1)[:, :, None, :]


def _gate_cols(g):
    nc = g.shape[0]
    g = g[:, :, 0, :].transpose(0, 2, 1).reshape(nc * GDN_CHUNK, GDN_V_HEADS)
    return jnp.pad(g, ((0, 0), (0, LANE - GDN_V_HEADS)))


def _gdn_fwd(qn, kn, v, g, beta, gather=None):
    T = qn.shape[1]
    tb = _gdn_tb(T)
    nc = tb // GDN_CHUNK
    nsteps = T // tb
    quarters, buffers = gather if gather is not None else ((), ())
    ng = len(quarters)
    shapes = [a.shape for a in quarters]
    splits = [True] * ng

    def body(*refs):
        q_ref, k_ref, v_ref, g_ref, b_ref = refs[:5]
        src = refs[5:5 + ng]
        o_ref, sall_ref, tall_ref = refs[5 + 2 * ng:8 + 2 * ng]
        dst = refs[8 + 2 * ng:8 + 3 * ng]
        s_scr = refs[8 + 3 * ng]
        step = pl.program_id(0)

        @pl.when(step == 0)
        def _():
            s_scr[...] = jnp.zeros(s_scr.shape, F32)
            if ng:
                for cp in _gather_copies(shapes, splits, src, dst, *refs[9 + 3 * ng:])[0]:
                    cp.start()

        def chunk(ci, carry):
            rows = pl.ds(pl.multiple_of(ci * GDN_CHUNK, GDN_CHUNK), GDN_CHUNK)
            s = s_scr[...]
            sall_ref[ci] = s
            q, k = q_ref[:, rows, :].astype(F32), k_ref[:, rows, :].astype(F32)
            bcol, gc_col, gc_last, decay = _gdn_gates(g_ref[ci], b_ref[ci])
            t_mat = _tri_inv(_gdn_m(k, bcol, decay)).astype(BF16)
            tall_ref[ci] = t_mat
            o, s_new = _gdn_rest(q, k, v_ref[:, rows, :].astype(F32), bcol, gc_col, gc_last, decay,
                                 t_mat.astype(F32), s)
            o_ref[:, rows, :] = o.astype(o_ref.dtype)
            s_scr[...] = s_new
            return carry

        lax.fori_loop(0, nc, chunk, 0)

        if ng:
            @pl.when(step == nsteps - 1)
            def _():
                _gather_arrival(shapes, splits, src, dst, *refs[9 + 3 * ng:])

    qk_spec = pl.BlockSpec((GDN_QK_HEADS, tb, GDN_HD), lambda i: (0, i, 0))
    v_spec = pl.BlockSpec((GDN_V_HEADS, tb, GDN_HD), lambda i: (0, i, 0))
    g_spec = pl.BlockSpec((nc, GDN_V_HEADS, 1, GDN_CHUNK), lambda i: (i, 0, 0, 0))
    anywhere = pl.BlockSpec(memory_space=pl.ANY)
    return pl.pallas_call(
        body, name="gdn_fwd", grid=(nsteps,),
        in_specs=[qk_spec, qk_spec, v_spec, g_spec, g_spec] + [anywhere] * (2 * ng),
        out_specs=[v_spec, pl.BlockSpec((nc, GDN_V_HEADS, GDN_HD, GDN_HD), lambda i: (i, 0, 0, 0)),
                   pl.BlockSpec((nc, GDN_V_HEADS, GDN_CHUNK, GDN_CHUNK), lambda i: (i, 0, 0, 0))] + [anywhere] * ng,
        out_shape=[jax.ShapeDtypeStruct((GDN_V_HEADS, T, GDN_HD), BF16),
                   jax.ShapeDtypeStruct((T // GDN_CHUNK, GDN_V_HEADS, GDN_HD, GDN_HD), F32),
                   jax.ShapeDtypeStruct((T // GDN_CHUNK, GDN_V_HEADS, GDN_CHUNK, GDN_CHUNK), BF16)]
        + [jax.ShapeDtypeStruct(b.shape, b.dtype) for b in buffers],
        input_output_aliases={5 + ng + a: 3 + a for a in range(ng)},
        scratch_shapes=[pltpu.VMEM((GDN_V_HEADS, GDN_HD, GDN_HD), F32)]
        + ([pltpu.SemaphoreType.DMA((6 * ng,)), pltpu.SemaphoreType.DMA((6 * ng,))] if ng else []),
        compiler_params=_params(("arbitrary",)),
    )(qn, kn, v, g, beta, *quarters, *buffers)


def _gdn_bwd(qn, kn, v, g, beta, sall, tall, do, scatter=()):
    T = qn.shape[1]
    tb = _gdn_tb(T)
    nc = tb // GDN_CHUNK
    nb = T // tb
    ns = len(scatter)

    def body(*refs):
        q_ref, k_ref, v_ref, g_ref, b_ref, sall_ref, tall_ref, do_ref = refs[:8]
        dq_ref, dk_ref, dv_ref, dg_ref, db_ref = refs[8 + ns:13 + ns]
        ds_scr = refs[13 + 2 * ns]
        comm = (refs[8:8 + ns], refs[13 + ns:13 + 2 * ns], *refs[14 + 2 * ns:])
        step = pl.program_id(0)

        @pl.when(step == 0)
        def _():
            ds_scr[...] = jnp.zeros(ds_scr.shape, F32)
            if ns:
                for cp in _scatter_copies(*comm):
                    cp.start()

        def chunk(cr, carry):
            ci = nc - 1 - cr
            rows = pl.ds(pl.multiple_of(ci * GDN_CHUNK, GDN_CHUNK), GDN_CHUNK)
            k, t_mat = k_ref[:, rows, :].astype(F32), tall_ref[ci].astype(F32)
            (bcol, gc_col, gc_last, decay), vjp_gates = jax.vjp(_gdn_gates, g_ref[ci], b_ref[ci])
            _, vjp = jax.vjp(_gdn_rest, q_ref[:, rows, :].astype(F32), k, v_ref[:, rows, :].astype(F32),
                             bcol, gc_col, gc_last, decay, t_mat, sall_ref[ci])
            dq, dk, dv, dbcol, dgc_col, dgc_last, ddecay, dt, ds = vjp((do_ref[:, rows, :].astype(F32), ds_scr[...]))
            _, vjp_m = jax.vjp(_gdn_m, k, bcol, decay)
            dk_m, dbcol_m, ddecay_m = vjp_m(_tri_inv_vjp(t_mat, dt))
            dg, db = vjp_gates((dbcol + dbcol_m, dgc_col, dgc_last, ddecay + ddecay_m))
            ds_scr[...] = ds
            dq_ref[:, rows, :] = dq
            dk_ref[:, rows, :] = dk + dk_m
            dv_ref[:, rows, :] = dv
            dg_ref[ci] = dg
            db_ref[ci] = db
            return carry

        lax.fori_loop(0, nc, chunk, 0)

        if ns:
            @pl.when(step == nb - 1)
            def _():
                copies = _scatter_copies(*comm)
                for cp in copies:
                    cp.wait_recv()
                for cp in copies:
                    cp.wait_send()

    qk_spec = pl.BlockSpec((GDN_QK_HEADS, tb, GDN_HD), lambda i: (0, nb - 1 - i, 0))
    v_spec = pl.BlockSpec((GDN_V_HEADS, tb, GDN_HD), lambda i: (0, nb - 1 - i, 0))
    g_spec = pl.BlockSpec((nc, GDN_V_HEADS, 1, GDN_CHUNK), lambda i: (nb - 1 - i, 0, 0, 0))
    s_spec = pl.BlockSpec((nc, GDN_V_HEADS, GDN_HD, GDN_HD), lambda i: (nb - 1 - i, 0, 0, 0))
    t_spec = pl.BlockSpec((nc, GDN_V_HEADS, GDN_CHUNK, GDN_CHUNK), lambda i: (nb - 1 - i, 0, 0, 0))
    anywhere = pl.BlockSpec(memory_space=pl.ANY)
    return pl.pallas_call(
        body, name="gdn_bwd", grid=(nb,),
        in_specs=[qk_spec, qk_spec, v_spec, g_spec, g_spec, s_spec, t_spec, v_spec] + [anywhere] * ns,
        out_specs=[qk_spec, qk_spec, v_spec, g_spec, g_spec] + [anywhere] * ns,
        out_shape=[jax.ShapeDtypeStruct((GDN_QK_HEADS, T, GDN_HD), F32),
                   jax.ShapeDtypeStruct((GDN_QK_HEADS, T, GDN_HD), F32),
                   jax.ShapeDtypeStruct((GDN_V_HEADS, T, GDN_HD), F32),
                   jax.ShapeDtypeStruct(g.shape, F32), jax.ShapeDtypeStruct(g.shape, F32)]
        + _scatter_shapes(scatter),
        scratch_shapes=[pltpu.VMEM((GDN_V_HEADS, GDN_HD, GDN_HD), F32)]
        + ([pltpu.SemaphoreType.DMA((3 * ns,)), pltpu.SemaphoreType.DMA((3 * ns,))] if ns else []),
        compiler_params=_params(("arbitrary",)),
    )(qn, kn, v, g, beta, sall, tall, do, *scatter)


def _gnorm_core(o, z, w):
    return tuple(_rms_core(oh, w) * _silu(zh) for oh, zh in zip(o, z))


def _gnorm_fwd(o, pm, w, tm=256):
    T = pm.shape[0]
    tm = min(tm, T)

    def fn(i, ov, zv, wv):
        zf = zv.astype(F32)
        out = _gnorm_core(tuple(ov[h].astype(F32) for h in range(GDN_V_HEADS)), _heads(zf, GDN_V_HEADS), wv)
        return (jnp.concatenate(out, axis=1),)

    ins = [(o, "row", None), (pm, "row", (GDN_V, 2)), (w, "full", None)]
    return _rowcall("gnorm_fwd", fn, T, tm, ins, [((T, GDN_V), BF16, "row")])[0]


def _gnorm_bwd(o, pm, w, don, tm=128, swap=()):
    T = pm.shape[0]
    tm = min(tm, T)

    def fn(i, ov, zv, wv, dv):
        zf, df = zv.astype(F32), dv.astype(F32)
        _, vjp = jax.vjp(_gnorm_core, tuple(ov[h].astype(F32) for h in range(GDN_V_HEADS)),
                         _heads(zf, GDN_V_HEADS), wv)
        do, dz, dw = vjp(_heads(df, GDN_V_HEADS))
        return jnp.stack(do), jnp.concatenate(dz, axis=1), dw

    ins = [(o, "row", None), (pm, "row", (GDN_V, 2)), (w, "full", None), (don, "row", None)]
    outs = [((GDN_V_HEADS, T, GDN_HD), BF16, "row"), ((T, GDN_V), BF16, "row"), ((1, GDN_HD), F32, "acc")]
    return _rowcall("gnorm_bwd", fn, T, tm, ins, outs, swap=swap)


def _ffn_act_fwd(name, up, conv_w, conv_b, tm=256):
    T = up.shape[0]
    tm = min(tm, T)

    def fn(i, x, halo, cw, cb):
        u = _conv_fwd(_taps(x.astype(F32), halo.astype(F32), 3, i), cw) + cb
        return (_silu(u[:, :DFF]) * u[:, DFF:],)

    ins = [(up, "row", None), (up, "prev", None), (conv_w, "full", None), (conv_b, "full", None)]
    return _rowcall(name, fn, T, tm, ins, [((T, DFF), BF16, "row")])[0]


def _ffn_act_bwd(name, up, conv_w, conv_b, dact, tm=256):
    T = up.shape[0]
    tm = min(tm, T)

    def fn(i, x, halo, cw, cb, da):
        taps = _taps(x.astype(F32), halo.astype(F32), 3, i)
        da = da.astype(F32)
        u = _conv_fwd(taps, cw) + cb
        gate, val = u[:, :DFF], u[:, DFF:]
        sg = _sigmoid(gate)
        dgate = da * val * sg * (1.0 + gate * (1.0 - sg))
        dval = da * gate * sg
        du = jnp.concatenate([dgate, dval], axis=1)
        return du, _conv_dw(du, taps), jnp.sum(du, axis=0, keepdims=True)

    ins = [(up, "row", None), (up, "prev", None), (conv_w, "full", None), (conv_b, "full", None),
           (dact, "row", None)]
    outs = [((T, DFF2), BF16, "row"), ((SUBLANE, DFF2), F32, "acc"), ((1, DFF2), F32, "acc")]
    return _rowcall(name, fn, T, tm, ins, outs)


def _ffn_conv_bwd(name, du, conv_w, tm=256):
    T = du.shape[0]
    tm = min(tm, T)
    n = T // tm

    def fn(i, dv, halo, cw):
        return (_conv_dx(dv.astype(F32), halo.astype(F32), cw, i, n),)

    ins = [(du, "row", None), (du, "next", None), (conv_w, "full", None)]
    return _rowcall(name, fn, T, tm, ins, [((T, DFF2), BF16, "row")])[0]


GROUP_ROWS = SWA_GROUP * SWA_BLOCK


def _attn_core(q, kp, kc, vp, vc, bias, sink, mask):
    kcat = jnp.concatenate([kp, kc], axis=0)
    vcat = jnp.concatenate([vp, vc], axis=0)
    s = _bdot(q * (SWA_HD ** -0.5), kcat, NT) + bias
    s = jnp.where(mask, s, NEG_INF)
    m = lax.stop_gradient(jnp.maximum(jnp.max(s, axis=-1, keepdims=True), sink))
    p = jnp.exp(s - m)
    denom = jnp.sum(p, axis=-1, keepdims=True) + jnp.exp(sink - m)
    return _bdot(p / denom, vcat)


def _attn_mask(i):
    qi = lax.broadcasted_iota(jnp.int32, (GROUP_ROWS, 2 * SWA_BLOCK), 0) & (SWA_BLOCK - 1)
    ki = lax.broadcasted_iota(jnp.int32, (GROUP_ROWS, 2 * SWA_BLOCK), 1)
    dist = qi + SWA_BLOCK - ki
    return (dist >= 0) & (dist < SWA_BLOCK) & ((ki >= SWA_BLOCK) | (i > 0))


def _head_cols(h):
    return slice(h * SWA_HD, (h + 1) * SWA_HD)


def _block_rows(b):
    return slice(b * SWA_BLOCK, (b + 1) * SWA_BLOCK)


def _stacked_heads(ref, rows, j):
    return jnp.concatenate([ref[rows, _head_cols(SWA_GROUP * j + g)].astype(F32) for g in range(SWA_GROUP)], axis=0)


def _store_heads(ref, rows, j, stacked):
    for g in range(SWA_GROUP):
        ref[rows, _head_cols(SWA_GROUP * j + g)] = stacked[g * SWA_BLOCK:(g + 1) * SWA_BLOCK].astype(ref.dtype)


def _attn_chains(i, nblk, q_ref, kvc_ref, kvp_ref, b_ref, s_ref):
    chains = []
    for b in range(nblk):
        rows = _block_rows(b)
        mask = _attn_mask(i) if b == 0 else _attn_mask(1)
        before, before_rows = (kvp_ref, _block_rows(0)) if b == 0 else (kvc_ref, _block_rows(b - 1))
        for j in range(SWA_KV_HEADS):
            heads = slice(SWA_GROUP * j, SWA_GROUP * (j + 1))
            k_cols, v_cols = _head_cols(j), _head_cols(SWA_KV_HEADS + j)
            sink = jnp.concatenate(
                [jnp.broadcast_to(s_ref[j, g:g + 1, 0:1], (SWA_BLOCK, 1)) for g in range(SWA_GROUP)], axis=0)
            ops = (_stacked_heads(q_ref, rows, j), before[before_rows, k_cols].astype(F32),
                   kvc_ref[rows, k_cols].astype(F32), before[before_rows, v_cols].astype(F32),
                   kvc_ref[rows, v_cols].astype(F32), b_ref[heads].reshape(GROUP_ROWS, 2 * SWA_BLOCK), sink)
            chains.append((rows, j, ops, mask))
    return chains


def _attn_fwd_flat(q, kv, bias, sinks):
    T = q.shape[0]
    nblk = _tile(T // SWA_BLOCK, (4, 2, 1))
    rows = nblk * SWA_BLOCK

    def body(q_ref, kvc_ref, kvp_ref, b_ref, s_ref, o_ref):
        chains = _attn_chains(pl.program_id(0), nblk, q_ref, kvc_ref, kvp_ref, b_ref, s_ref)
        outs = [_attn_core(*ops, mask) for _, _, ops, mask in chains]
        for (blk, j, _, _), out in zip(chains, outs):
            _store_heads(o_ref, blk, j, out)

    q_spec = pl.BlockSpec((rows, q.shape[1]), lambda i: (i, 0))
    cur = pl.BlockSpec((rows, kv.shape[1]), lambda i: (i, 0))
    prev = pl.BlockSpec((SWA_BLOCK, kv.shape[1]), lambda i: (jnp.maximum(nblk * i - 1, 0), 0))
    return pl.pallas_call(
        body, name="attn_fwd", grid=(T // rows,),
        in_specs=[q_spec, cur, prev, pl.BlockSpec(bias.shape, lambda i: (0, 0, 0)),
                  pl.BlockSpec(sinks.shape, lambda i: (0, 0, 0))],
        out_specs=q_spec, out_shape=jax.ShapeDtypeStruct(q.shape, BF16),
        compiler_params=_params(("arbitrary",)),
    )(q, kv, kv, bias, sinks)


def _attn_bwd_flat(q, kv, bias, sinks, do):
    T = q.shape[0]
    nb = T // SWA_BLOCK
    rows = _block_rows(0)

    def body(q_ref, kvc_ref, kvp_ref, b_ref, s_ref, do_ref, dq_ref, dkv_ref, db_ref, dsk_ref, carry):
        i = pl.program_id(0)

        @pl.when(i < nb)
        def _():
            chains = _attn_chains(i, 1, q_ref, kvc_ref, kvp_ref, b_ref, s_ref)
            cots = [_stacked_heads(do_ref, rows, j) for _, j, _, _ in chains]
            grads = [jax.vjp(functools.partial(_attn_core, mask=mask), *ops)[1](cot)
                     for (_, _, ops, mask), cot in zip(chains, cots)]
            for j, (dq, dkp, dkc, dvp, dvc, db, dsc) in enumerate(grads):
                heads = slice(SWA_GROUP * j, SWA_GROUP * (j + 1))
                k_cols, v_cols = _head_cols(j), _head_cols(SWA_KV_HEADS + j)
                _store_heads(dq_ref, rows, j, dq)
                db = db.reshape(SWA_GROUP, SWA_BLOCK, 2 * SWA_BLOCK)
                dsk = jnp.concatenate(
                    [jnp.broadcast_to(jnp.sum(dsc[g * SWA_BLOCK:(g + 1) * SWA_BLOCK], axis=0, keepdims=True),
                                      (1, LANE)) for g in range(SWA_GROUP)], axis=0)

                @pl.when(i == 0)
                def _():
                    db_ref[heads] = db
                    dsk_ref[j] = dsk

                @pl.when(i > 0)
                def _():
                    db_ref[heads] += db
                    dsk_ref[j] += dsk
                    dkv_ref[:, k_cols] = (carry[:, k_cols] + dkp).astype(dkv_ref.dtype)
                    dkv_ref[:, v_cols] = (carry[:, v_cols] + dvp).astype(dkv_ref.dtype)

                carry[:, k_cols] = dkc
                carry[:, v_cols] = dvc

        @pl.when(i == nb)
        def _():
            dkv_ref[...] = carry[...].astype(dkv_ref.dtype)

    last = nb - 1
    q_spec = pl.BlockSpec((SWA_BLOCK, q.shape[1]), lambda i: (jnp.minimum(i, last), 0))
    cur = pl.BlockSpec((SWA_BLOCK, kv.shape[1]), lambda i: (jnp.minimum(i, last), 0))
    prev = pl.BlockSpec((SWA_BLOCK, kv.shape[1]), lambda i: (jnp.clip(i - 1, 0, last), 0))
    b_spec = pl.BlockSpec(bias.shape, lambda i: (0, 0, 0))
    s_spec = pl.BlockSpec(sinks.shape, lambda i: (0, 0, 0))
    return pl.pallas_call(
        body, name="attn_bwd", grid=(nb + 1,),
        in_specs=[q_spec, cur, prev, b_spec, s_spec, q_spec],
        out_specs=[q_spec, prev, b_spec, s_spec],
        out_shape=[jax.ShapeDtypeStruct(q.shape, BF16), jax.ShapeDtypeStruct(kv.shape, BF16),
                   jax.ShapeDtypeStruct(bias.shape, F32), jax.ShapeDtypeStruct(sinks.shape, F32)],
        scratch_shapes=[pltpu.VMEM((SWA_BLOCK, kv.shape[1]), F32)],
        compiler_params=_params(("arbitrary",)),
    )(q, kv, kv, bias, sinks, do)


def _rel_onehot():
    qi = jnp.arange(SWA_BLOCK)[:, None]
    ki = jnp.arange(2 * SWA_BLOCK)[None, :]
    n = jnp.maximum(qi + SWA_BLOCK - ki, 0)
    max_exact = REL_BUCKETS // 2
    nf = jnp.maximum(n, 1).astype(F32)
    large = max_exact + (jnp.log(nf / max_exact) / math.log(REL_MAX_DISTANCE / max_exact)
                         * (REL_BUCKETS - max_exact)).astype(jnp.int32)
    bucket = jnp.where(n < max_exact, n, jnp.minimum(large, REL_BUCKETS - 1)).reshape(-1)
    return (bucket[None, :] == jnp.arange(REL_BUCKETS)[:, None]).astype(F32)


def _final(h, w, target, tm=512):
    T = h.shape[0]
    tm = min(tm, T)

    def fn(i, hv, wv, tv):
        y, vjp = jax.vjp(_rms_core, hv, wv)
        err = y - tv
        dh, dw = vjp(err * (1.0 / D))
        part = 0.5 * jnp.sum(jnp.sum(err * err, axis=1, keepdims=True) * (1.0 / D), axis=0, keepdims=True)
        return jnp.broadcast_to(part, (SUBLANE, LANE)), dh, dw

    ins = [(h, "row", None), (w, "full", None), (target, "row", None)]
    outs = [((SUBLANE, LANE), F32, "acc"), ((T, D), F32, "row"), ((1, D), F32, "acc")]
    return _rowcall("final", fn, T, tm, ins, outs)


def _ffn_fwd(tag, h, P, layer):
    n = (h, P["ffn_norm_w"][layer:layer + 1])
    up = _mm_up(f"{tag}_up", n, P["w_up"], layer)
    act = _ffn_act_fwd(f"{tag}_act", up, P["ffn_conv_w"][layer], P["ffn_conv_b"][layer:layer + 1])
    out = _mm_nn(f"{tag}_down", act, P["w_down"][layer], F32, res=h)
    return out, (n, up, act)


def _ffn_bwd(tag, h, saved, dout, P, layer, into=(None, None)):
    n, up, act = saved
    cw, cb = P["ffn_conv_w"][layer], P["ffn_conv_b"][layer:layer + 1]
    dact = _mm_nt(f"{tag}_down_dx", dout, P["w_down"][layer], BF16)
    g_down = _mm_down_tn(f"{tag}_down_dw", act, dout, layer, into[1])
    du, dcw, dcb = _ffn_act_bwd(f"{tag}_act_bwd", up, cw, cb, dact)
    dup = _ffn_conv_bwd(f"{tag}_conv_bwd", du, cw)
    g_up = _mm_up_tn(f"{tag}_up_dw", n, dup, layer, into[0])
    dh, dnw = _mm_up_nt(f"{tag}_up_dx", dup, P["w_up"], layer, post=(h, P["ffn_norm_w"][layer:layer + 1], dout))
    return dh, dict(w_down=g_down, w_up=g_up, conv_w=dcw[:3], conv_b=dcb, norm_w=dnw)


def _local_step(x, target, P, late=None, pair_sums=None):
    T = x.shape[0]
    n0 = (x, P["a_norm_w"])
    pm = _mm_nn("gdn_in", n0, P["w_in_main"], BF16)
    pba = _mm_nn("gdn_in_ba", n0, P["w_in_ba"], F32)
    qn, kn, v, beta, g = _gdn_pre_fwd(pm, pba, P["a_conv_w"], P["a_log"], P["dt_bias"])
    g_rows, beta_rows = _gate_rows(g), _gate_rows(beta)
    o, sall, tall, *gathered = _gdn_fwd(qn, kn, v, g_rows, beta_rows, gather=late)
    if late is not None:
        P = {**P, **_late_weights(gathered)}
    on = _gnorm_fwd(o, pm, P["a_out_norm_w"])
    h1 = _mm_nn("gdn_out", on, P["w_out"], F32, res=x)
    h2, ffn0 = _ffn_fwd("ffn0", h1, P, 0)
    nkv = (h2, P["kv_norm_w"])
    kv = _mm_nn("kv_proj", nkv, P["w_kv"], BF16)
    nb = (h2, P["b_norm_w"])
    qp = _mm_nn("q_proj", nb, P["w_q"], BF16)
    onehot = _rel_onehot()
    bias = _mm_nn("rel_bias", P["rel_table_t"], onehot, F32, precision=HIGHEST)
    bias = bias.reshape(SWA_Q_HEADS, SWA_BLOCK, 2 * SWA_BLOCK)
    oa = _attn_fwd_flat(qp, kv, bias, P["sinks"])
    h3 = _mm_nn("o_proj", oa, P["w_o"], F32, res=h2)
    h4, ffn1 = _ffn_fwd("ffn1", h3, P, 1)
    loss, dh4, d_final = _final(h4, P["final_norm_w"], target)

    dh3, gf1 = _ffn_bwd("ffn1", h3, ffn1, dh4, P, 1)
    doa = _mm_nt("o_proj_dx", dh3, P["w_o"], BF16)
    g_wo = _mm_tn("o_proj_dw", oa, dh3)
    dqp, dkv, dbias, dsinks = _attn_bwd_flat(qp, kv, bias, P["sinks"], doa)
    g_wq = _mm_tn("q_proj_dw", nb, dqp)
    g_wkv = _mm_tn("kv_proj_dw", nkv, dkv)
    dh2, d_bnorm = _mm_nt("q_proj_dx", dqp, P["w_q"], F32, post=(h2, P["b_norm_w"], dh3))
    dh2, d_kvnorm = _mm_nt("kv_proj_dx", dkv, P["w_kv"], F32, post=(h2, P["kv_norm_w"], dh2))
    g_table = _mm_nt("rel_bias_dw", onehot, dbias.reshape(SWA_Q_HEADS, -1), F32, precision=HIGHEST)
    dh1, gf0 = _ffn_bwd("ffn0", h1, ffn0, dh2, P, 0, into=(gf1["w_up"], gf1["w_down"]))
    don = _mm_nt("gdn_out_dx", dh1, P["w_out"], BF16)
    g_wout = _mm_tn("gdn_out_dw", on, dh1)
    ready = dict(a_w_out=g_wout, w_kv=g_wkv, b_w_q=g_wq, b_w_o=g_wo, ffn_w_up=gf0["w_up"], ffn_w_down=gf0["w_down"])
    names = [n for n in BIG if n in ready]
    whole = [_chip_major(n, ready[n]) for n in names] if pair_sums is not None else []
    do, dz, d_gnorm, *other = _gnorm_bwd(o, pm, P["a_out_norm_w"], don, swap=whole)
    pairs = pair_sums(names, whole, other) if pair_sums is not None else []
    dq, dk, dv, dg, dbeta, *parts = _gdn_bwd(qn, kn, v, g_rows, beta_rows, sall, tall, do, scatter=pairs)
    dy, dpba, d_aconv, d_alog, d_dtb = _gdn_pre_bwd(pm, pba, P["a_conv_w"], P["a_log"], P["dt_bias"],
                                                    dq, dk, dv, _gate_cols(dbeta), _gate_cols(dg))
    dpm = _gdn_conv_bwd(dy, dz, P["a_conv_w"])
    g_win_main = _mm_tn("gdn_in_dw", n0, dpm)
    g_win_ba = _mm_tn("gdn_in_ba_dw", n0, dpba)
    nh = GDN_V_HEADS
    g_win = jnp.concatenate([g_win_main, g_win_ba[:, :nh], g_win_ba[:, LANE:LANE + nh]], axis=1)
    last_whole = [_chip_major("a_w_in", g_win)]
    last_pair = pair_sums(["a_w_in"], last_whole, _pair_swap(last_whole, "late")) if pair_sums is not None else []
    dn0 = _mm_nt("gdn_in_dx", dpm, P["w_in_main"], F32, scatter=last_pair)
    dn0, last_parts = (dn0[0], dn0[1:]) if last_pair else (dn0, [])
    dx, d_anorm = _mm_nt("gdn_in_ba_dx", dpba, P["w_in_ba"], F32, res=dn0, post=(x, P["a_norm_w"], dh1))

    nh = GDN_V_HEADS
    grads = dict(
        a_norm_w=d_anorm,
        a_w_in=g_win,
        a_conv_w=d_aconv[:4], a_a_log=d_alog[:, :nh], a_dt_bias=d_dtb[:, :nh], a_out_norm_w=d_gnorm,
        a_w_out=g_wout, kv_norm_w=d_kvnorm, w_kv=g_wkv, b_norm_w=d_bnorm, b_w_q=g_wq,
        b_sinks=dsinks[:, :, 0].reshape(1, SWA_Q_HEADS), b_w_o=g_wo, rel_bias_table=g_table,
        ffn_norm_w=jnp.concatenate([gf0["norm_w"], gf1["norm_w"]], axis=0),
        ffn_w_up=gf0["w_up"],
        ffn_conv_w=jnp.stack([gf0["conv_w"], gf1["conv_w"]], axis=0),
        ffn_conv_b=jnp.concatenate([gf0["conv_b"], gf1["conv_b"]], axis=0),
        ffn_w_down=gf0["w_down"],
        final_norm_w=d_final,
    )
    scattered = dict(zip([n for n in BIG if n in ready], zip(pairs, parts)))
    scattered.update(zip(["a_w_in"], zip(last_pair, last_parts)))
    return loss, dx, grads, scattered


HBM_SPEC = pl.BlockSpec(memory_space=pltpu.HBM)
VMEM_SPEC = pl.BlockSpec(memory_space=pltpu.VMEM)


def _coords():
    return lax.axis_index("x"), lax.axis_index("y"), lax.axis_index("c")


def _remote(src, dst, send_sem, recv_sem, device):
    return pltpu.make_async_remote_copy(src_ref=src, dst_ref=dst, send_sem=send_sem, recv_sem=recv_sem,
                                        device_id=device, device_id_type=MESH)


def _other_chips(x, y):
    return [(1 - x, y), (x, 1 - y), (1 - x, 1 - y)]


def _gather_copies(shapes, split, ins, outs, send_sems, recv_sems):
    x, y, c = _coords()
    p = 2 * x + y
    ici, forwards, from_sibling = [], [], []
    for a, shape in enumerate(shapes):
        h = shape[0] // 2
        for j, chip in enumerate(_other_chips(x, y)):
            q = 2 * chip[0] + chip[1]
            if split[a]:
                mine, theirs = pl.ds(c * h, h), pl.ds((1 - c) * h, h)
                ici.append(_remote(ins[a].at[mine], outs[a].at[p, mine], send_sems.at[6 * a + j],
                                   recv_sems.at[6 * a + j], (*chip, c)))
                land = outs[a].at[q, mine]
                forwards.append(_remote(land, land, send_sems.at[6 * a + 3 + j], recv_sems.at[6 * a + 3 + j],
                                        (x, y, 1 - c)))
                land = outs[a].at[q, theirs]
                from_sibling.append(_remote(land, land, send_sems.at[6 * a + 3 + j], recv_sems.at[6 * a + 3 + j],
                                            (x, y, 1 - c)))
            else:
                ici.append(_remote(ins[a], outs[a].at[p], send_sems.at[6 * a + j], recv_sems.at[6 * a + j],
                                   (*chip, c)))
                forwards.append(None)
    return ici, forwards, from_sibling


def _gather_arrival(shapes, split, ins, outs, send_sems, recv_sems):
    x, y, c = _coords()
    ici, forwards, from_sibling = _gather_copies(shapes, split, ins, outs, send_sems, recv_sems)
    k = 0
    for a, shape in enumerate(shapes):
        h = shape[0] // 2
        for j, chip in enumerate(_other_chips(x, y)):
            q = 2 * chip[0] + chip[1]
            land = outs[a].at[q, pl.ds(c * h, h)] if split[a] else outs[a].at[q]
            _remote(land, land, send_sems.at[6 * a + j], recv_sems.at[6 * a + j], (*chip, c)).wait_recv()
            if forwards[k] is not None:
                forwards[k].start()
            k += 1
    for cp in from_sibling:
        cp.wait_recv()
    for cp in ici + [f for f in forwards if f is not None]:
        cp.wait_send()


def _all_gather(arrs, split, remote):
    n = len(arrs)
    now = [a for a in range(n) if remote[a]]
    shapes = [arrs[a].shape for a in now]
    splits = [split[a] for a in now]

    def body(*refs):
        ins, outs, stage = refs[:n], refs[n:2 * n], refs[2 * n:3 * n]
        send_sems, recv_sems, in_sems, out_sems = refs[3 * n:]
        p = 2 * lax.axis_index("x") + lax.axis_index("y")
        gathered = ([ins[a] for a in now], [outs[a] for a in now], send_sems, recv_sems)
        loads = [pltpu.make_async_copy(ins[a], stage[a], in_sems.at[a]) for a in range(n)]
        for cp in loads:
            cp.start()
        for cp in _gather_copies(shapes, splits, *gathered)[0]:
            cp.start()
        stores = [pltpu.make_async_copy(stage[a], outs[a].at[p], out_sems.at[a]) for a in range(n)]
        for a in range(n):
            loads[a].wait()
            stores[a].start()
        _gather_arrival(shapes, splits, *gathered)
        for cp in stores:
            cp.wait()

    return pl.pallas_call(
        body, name="weights_all_gather", in_specs=[HBM_SPEC] * n, out_specs=[HBM_SPEC] * n,
        out_shape=[jax.ShapeDtypeStruct((N_CHIPS,) + a.shape, a.dtype) for a in arrs],
        scratch_shapes=[pltpu.VMEM(a.shape, a.dtype) for a in arrs]
        + [pltpu.SemaphoreType.DMA((6 * len(now),)), pltpu.SemaphoreType.DMA((6 * len(now),)),
           pltpu.SemaphoreType.DMA((n,)), pltpu.SemaphoreType.DMA((n,))],
        compiler_params=pltpu.CompilerParams(vmem_limit_bytes=VMEM_LIMIT),
    )(*arrs)


PAIR_SWAP_PIECES = 2


def _swap_copies(shapes, ins, other, send_sems, recv_sems):
    x, y, c = _coords()
    copies = []
    for a, shape in enumerate(shapes):
        h = shape[1] // 2
        piece = h // PAIR_SWAP_PIECES
        for q in range(N_CHIPS):
            for r in range(PAIR_SWAP_PIECES):
                k = (a * N_CHIPS + q) * PAIR_SWAP_PIECES + r
                copies.append(_remote(ins[a].at[q, pl.ds((1 - c) * h + r * piece, piece)],
                                      other[a].at[q, pl.ds(r * piece, piece)], send_sems.at[k], recv_sems.at[k],
                                      (x, y, 1 - c)))
    return copies


def _swap_out_shapes(gs):
    return [jax.ShapeDtypeStruct((N_CHIPS, g.shape[1] // 2, g.shape[2]), g.dtype) for g in gs]


def _pair_swap(gs, tag):
    n = len(gs)
    shapes = [g.shape for g in gs]

    def body(*refs):
        copies = _swap_copies(shapes, refs[:n], refs[n:2 * n], *refs[2 * n:])
        for cp in copies:
            cp.start()
        for cp in copies:
            cp.wait()

    nsem = n * N_CHIPS * PAIR_SWAP_PIECES
    return pl.pallas_call(
        body, name=f"grads_pair_swap_{tag}", in_specs=[HBM_SPEC] * n, out_specs=[HBM_SPEC] * n,
        out_shape=_swap_out_shapes(gs),
        scratch_shapes=[pltpu.SemaphoreType.DMA((nsem,)), pltpu.SemaphoreType.DMA((nsem,))],
    )(*gs)


def _scatter_copies(ins, outs, send_sems, recv_sems):
    x, y, c = _coords()
    copies = []
    for a in range(len(ins)):
        for j, chip in enumerate(_other_chips(x, y)):
            q = 2 * chip[0] + chip[1]
            copies.append(_remote(ins[a].at[q], outs[a].at[j], send_sems.at[3 * a + j], recv_sems.at[3 * a + j],
                                  (*chip, c)))
    return copies


def _scatter_shapes(ps):
    return [jax.ShapeDtypeStruct((N_CHIPS - 1,) + a.shape[1:], a.dtype) for a in ps]


def _pair_share(rs):
    n = len(rs)

    def body(*refs):
        ins, outs, stage = refs[:n], refs[n:2 * n], refs[2 * n:3 * n]
        send_sems, recv_sems, in_sems, out_sems = refs[3 * n:]
        x, y, c = _coords()

        def mine(a):
            h = rs[a].shape[0]
            return outs[a].at[pl.ds(c * h, h)]

        loads = [pltpu.make_async_copy(ins[a], stage[a], in_sems.at[a]) for a in range(n)]
        for cp in loads:
            cp.start()
        sends = [_remote(ins[a], mine(a), send_sems.at[a], recv_sems.at[a], (x, y, 1 - c)) for a in range(n)]
        for cp in sends:
            cp.start()
        stores = [pltpu.make_async_copy(stage[a], mine(a), out_sems.at[a]) for a in range(n)]
        for a in range(n):
            loads[a].wait()
            stores[a].start()
        for a in range(n):
            h = rs[a].shape[0]
            land = outs[a].at[pl.ds((1 - c) * h, h)]
            _remote(land, land, send_sems.at[a], recv_sems.at[a], (x, y, 1 - c)).wait_recv()
        for cp in sends:
            cp.wait_send()
        for cp in stores:
            cp.wait()

    return pl.pallas_call(
        body, name="grads_pair_share", in_specs=[HBM_SPEC] * n, out_specs=[HBM_SPEC] * n,
        out_shape=[jax.ShapeDtypeStruct((2 * a.shape[0], a.shape[1]), a.dtype) for a in rs],
        scratch_shapes=[pltpu.VMEM(a.shape, a.dtype) for a in rs] + [pltpu.SemaphoreType.DMA((n,))] * 4,
        compiler_params=pltpu.CompilerParams(vmem_limit_bytes=VMEM_LIMIT),
    )(*rs)


def _small_all_reduce(buf):
    R = buf.shape[0]
    ndev = 2 * N_CHIPS

    def body(in_ref, out_ref, gath, send_sems, recv_sems):
        x, y, c = _coords()
        me = 4 * x + 2 * y + c
        gath[me] = in_ref[...]
        peers = []
        for d in range(1, ndev):
            px = 1 - x if d & 4 else x
            py = 1 - y if d & 2 else y
            pc = 1 - c if d & 1 else c
            peers.append((px, py, pc))
        sends = []
        for d, peer in enumerate(peers):
            cp = _remote(in_ref, gath.at[me], send_sems.at[d], recv_sems.at[d], peer)
            cp.start()
            sends.append(cp)
        for d, peer in enumerate(peers):
            land = gath.at[4 * peer[0] + 2 * peer[1] + peer[2]]
            _remote(land, land, send_sems.at[d], recv_sems.at[d], peer).wait_recv()
        for cp in sends:
            cp.wait_send()
        acc = gath[0]
        for s in range(1, ndev):
            acc = acc + gath[s]
        out_ref[...] = acc

    return pl.pallas_call(
        body, name="small_all_reduce", in_specs=[VMEM_SPEC], out_specs=VMEM_SPEC,
        out_shape=jax.ShapeDtypeStruct(buf.shape, F32),
        scratch_shapes=[pltpu.VMEM((ndev, R, LANE), F32), pltpu.SemaphoreType.DMA((ndev - 1,)),
                        pltpu.SemaphoreType.DMA((ndev - 1,))],
    )(buf)


def _pair_add(name, own, other):
    h = own.shape[1]
    tm = _tile(h, (128, 64, 32, 16))

    def fn(i, a, b):
        return (a + b,)

    return _rowcall(name, fn, h, tm, [(own, "row", None), (other, "row", None)], [(own.shape, BF16, "row")])[0]


def _chip_add(name, own, parts):
    h = parts.shape[1]
    tm = _tile(h, (128, 64, 32, 16))

    def fn(i, o, a):
        a = a.astype(F32)
        return (((o.astype(F32) + a[0]) + a[1]) + a[2],)

    return _rowcall(name, fn, h, tm, [(own, "row", None), (parts, "row", None)], [(parts.shape[1:], F32, "row")])[0]


def _adamw(name, w, g, m, v):
    R = w.shape[0]
    tm = _tile(R, (256, 128, 64, 32, 16, 8))

    def fn(i, wv, gv, mv, vv):
        m2 = ADAM_B1 * mv + (1.0 - ADAM_B1) * gv
        v2 = ADAM_B2 * vv + (1.0 - ADAM_B2) * (gv * gv)
        m_hat = m2 / (1.0 - ADAM_B1 ** ADAM_STEP)
        v_hat = v2 / (1.0 - ADAM_B2 ** ADAM_STEP)
        delta = -ADAM_LR * (m_hat / (jnp.sqrt(v_hat) + ADAM_EPS) + ADAM_WD * wv)
        return delta, m2, v2

    ins = [(a, "row", None) for a in (w, g, m, v)]
    return _rowcall(name, fn, R, tm, ins, [(w.shape, F32, "row")] * 3)


def _pack(arrs):
    flat = jnp.concatenate([a.reshape(-1).astype(F32) for a in arrs])
    size = flat.shape[0]
    padded = -(-size // (SUBLANE * LANE)) * SUBLANE * LANE
    return jnp.pad(flat, (0, padded - size)).reshape(-1, LANE)


def _unpack(buf, shapes):
    flat = buf.reshape(-1)
    out, off = [], 0
    for s in shapes:
        size = math.prod(s)
        out.append(flat[off:off + size].reshape(s))
        off += size
    return out


BIG = ("a_w_in", "a_w_out", "w_kv", "b_w_q", "b_w_o", "ffn_w_up", "ffn_w_down")
WEIGHTS = ("a_norm_w", "a_w_in", "a_conv_w", "a_a_log", "a_dt_bias", "a_out_norm_w", "a_w_out", "kv_norm_w", "w_kv",
           "b_norm_w", "b_w_q", "b_sinks", "b_w_o", "rel_bias_table", "ffn_norm_w", "ffn_w_up", "ffn_conv_w",
           "ffn_conv_b", "ffn_w_down", "final_norm_w")
SMALL = tuple(n for n in WEIGHTS if n not in BIG)
SMALL_SHARDED = {"a_norm_w": 1, "a_conv_w": 2, "ffn_conv_w": 2}


def _quarter_2d(name, a):
    if name in ("ffn_w_up", "ffn_w_down"):
        return a.reshape(a.shape[0] * a.shape[1], a.shape[2])
    return a.reshape(a.shape[-2], a.shape[-1])


def _whole_weights(w):
    bigs = [_quarter_2d(n, w[n]).astype(BF16) for n in BIG]
    smalls = [w["a_norm_w"], w["a_conv_w"][0], w["ffn_conv_w"].reshape(6, DFF2_SHARD)]
    remote = [True] + [False] * (len(bigs) - 1) + [True] * len(smalls)
    g = _all_gather(bigs + smalls, [True] * len(bigs) + [False] * len(smalls), remote)
    w_in = g[0].transpose(1, 0, 2).reshape(D, GDN_IN)
    nh = GDN_V_HEADS
    zpad = jnp.zeros((D, LANE - nh), BF16)
    w_in_ba = jnp.concatenate([w_in[:, GDN_MAIN:GDN_MAIN + nh], zpad, w_in[:, GDN_MAIN + nh:], zpad], axis=1)
    lane_pad = lambda a: jnp.pad(a, ((0, 0), (0, LANE - nh)))
    early = dict(
        a_norm_w=g[7].reshape(1, D), w_in_main=w_in[:, :GDN_MAIN], w_in_ba=w_in_ba,
        a_conv_w=g[8].transpose(1, 0, 2).reshape(4, GDN_CONV), a_log=lane_pad(w["a_a_log"]),
        dt_bias=lane_pad(w["a_dt_bias"]), a_out_norm_w=w["a_out_norm_w"],
        kv_norm_w=w["kv_norm_w"].reshape(1, D), b_norm_w=w["b_norm_w"],
        sinks=jnp.broadcast_to(w["b_sinks"].reshape(SWA_KV_HEADS, SWA_GROUP, 1), (SWA_KV_HEADS, SWA_GROUP, LANE)),
        rel_table_t=w["rel_bias_table"].T, ffn_norm_w=w["ffn_norm_w"],
        ffn_conv_w=g[9].reshape(N_CHIPS, 2, 3, DFF2_SHARD).transpose(1, 2, 0, 3).reshape(2, 3, DFF2),
        ffn_conv_b=w["ffn_conv_b"], final_norm_w=w["final_norm_w"].reshape(1, D),
    )
    return early, (bigs[1:], g[1:len(bigs)])


def _late_weights(g):
    return dict(
        w_out=g[0].reshape(GDN_V, D), w_kv=g[1].reshape(D, 2 * SWA_KV_HEADS * SWA_HD), w_q=g[2].reshape(D, D),
        w_o=g[3].reshape(D, D), w_up=g[4].reshape(N_CHIPS, 2, D, DFF2_SHARD),
        w_down=g[5].reshape(N_CHIPS, 2, DFF_SHARD, D).transpose(1, 0, 2, 3).reshape(2, DFF, D),
    )


def _chip_major(name, g):
    if name == "a_w_in":
        return g.reshape(D, N_CHIPS, GDN_IN_SHARD).transpose(1, 0, 2)
    if name == "ffn_w_up":
        return g.reshape(N_CHIPS, 2 * D, DFF2_SHARD)
    if name == "ffn_w_down":
        return g.reshape(N_CHIPS, 2 * DFF_SHARD, D)
    return g.reshape(N_CHIPS, g.shape[0] // N_CHIPS, g.shape[1])


def kernel(x, a_norm_w, a_w_in, a_conv_w, a_a_log, a_dt_bias, a_out_norm_w, a_w_out, kv_norm_w, w_kv, b_norm_w, b_w_q, b_sinks, b_w_o, rel_bias_table, ffn_norm_w, ffn_w_up, ffn_conv_w, ffn_conv_b, ffn_w_down, final_norm_w, loss_target, m_a_norm_w, m_a_w_in, m_a_conv_w, m_a_a_log, m_a_dt_bias, m_a_out_norm_w, m_a_w_out, m_kv_norm_w, m_w_kv, m_b_norm_w, m_b_w_q, m_b_sinks, m_b_w_o, m_rel_bias_table, m_ffn_norm_w, m_ffn_w_up, m_ffn_conv_w, m_ffn_conv_b, m_ffn_w_down, m_final_norm_w, v_a_norm_w, v_a_w_in, v_a_conv_w, v_a_a_log, v_a_dt_bias, v_a_out_norm_w, v_a_w_out, v_kv_norm_w, v_w_kv, v_b_norm_w, v_b_w_q, v_b_sinks, v_b_w_o, v_rel_bias_table, v_ffn_norm_w, v_ffn_w_up, v_ffn_conv_w, v_ffn_conv_b, v_ffn_w_down, v_final_norm_w):
    w = dict(zip(WEIGHTS, (a_norm_w, a_w_in, a_conv_w, a_a_log, a_dt_bias, a_out_norm_w, a_w_out, kv_norm_w, w_kv,
                           b_norm_w, b_w_q, b_sinks, b_w_o, rel_bias_table, ffn_norm_w, ffn_w_up, ffn_conv_w,
                           ffn_conv_b, ffn_w_down, final_norm_w)))
    m = dict(zip(WEIGHTS, (m_a_norm_w, m_a_w_in, m_a_conv_w, m_a_a_log, m_a_dt_bias, m_a_out_norm_w, m_a_w_out,
                           m_kv_norm_w, m_w_kv, m_b_norm_w, m_b_w_q, m_b_sinks, m_b_w_o, m_rel_bias_table,
                           m_ffn_norm_w, m_ffn_w_up, m_ffn_conv_w, m_ffn_conv_b, m_ffn_w_down, m_final_norm_w)))
    v = dict(zip(WEIGHTS, (v_a_norm_w, v_a_w_in, v_a_conv_w, v_a_a_log, v_a_dt_bias, v_a_out_norm_w, v_a_w_out,
                           v_kv_norm_w, v_w_kv, v_b_norm_w, v_b_w_q, v_b_sinks, v_b_w_o, v_rel_bias_table,
                           v_ffn_norm_w, v_ffn_w_up, v_ffn_conv_w, v_ffn_conv_b, v_ffn_w_down, v_final_norm_w)))
    T = x.shape[1]
    chip = 2 * lax.axis_index("x") + lax.axis_index("y")

    core = lax.axis_index("c")

    def pair_sums(names, whole, other):
        own = [lax.dynamic_slice_in_dim(g, core * (g.shape[1] // 2), g.shape[1] // 2, 1) for g in whole]
        return [_pair_add(f"pair_add_{n}", a, b) for n, a, b in zip(names, own, other)]

    early, late = _whole_weights(w)
    loss_part, dx, grads, scattered = _local_step(x.reshape(T, D), loss_target.reshape(T, D), early, late, pair_sums)

    assert all(n in scattered for n in BIG)
    halves = [_chip_add(f"chip_add_{n}", lax.dynamic_index_in_dim(scattered[n][0], chip, 0, keepdims=False),
                        scattered[n][1]) for n in BIG]
    quarter = _pair_share(halves)
    out_g, out_d, out_m, out_v = {}, {}, {}, {}
    for n, g2 in zip(BIG, quarter):
        res = _adamw(f"adamw_{n}", _quarter_2d(n, w[n]), g2, _quarter_2d(n, m[n]), _quarter_2d(n, v[n]))
        out_g[n] = g2.reshape(w[n].shape)
        out_d[n], out_m[n], out_v[n] = (r.reshape(w[n].shape) for r in res)

    whole = [grads[n] for n in SMALL]
    summed = _unpack(_small_all_reduce(_pack([loss_part[0:1, 0:1]] + whole)), [(1, 1)] + [a.shape for a in whole])
    loss = summed[0].reshape(())
    small_g = []
    for n, g in zip(SMALL, summed[1:]):
        if n in SMALL_SHARDED:
            axis = SMALL_SHARDED[n]
            g = g.reshape(w[n].shape[:axis] + (-1,) + w[n].shape[axis + 1:])
            size = w[n].shape[axis]
            g = lax.dynamic_slice_in_dim(g, chip * size, size, axis)
        small_g.append(g.reshape(w[n].shape))
    shapes = [w[n].shape for n in SMALL]
    res = _adamw("adamw_small", _pack([w[n] for n in SMALL]), _pack(small_g), _pack([m[n] for n in SMALL]),
                 _pack([v[n] for n in SMALL]))
    small_d, small_m, small_v = (_unpack(r, shapes) for r in res)
    for i, n in enumerate(SMALL):
        out_g[n], out_d[n], out_m[n], out_v[n] = small_g[i], small_d[i], small_m[i], small_v[i]

    return (loss, dx.reshape(x.shape), *[out_g[n] for n in WEIGHTS], *[out_d[n] for n in WEIGHTS],
            *[out_m[n] for n in WEIGHTS], *[out_v[n] for n in WEIGHTS])
```
